```python
import jax, jax.numpy as jnp
from jax import lax
import numpy as np

D_MODEL = 2048
BATCH = 1
SEQ = 8192
DEPTH = 1
DEC_BATCH = 32
DEC_SEQ = 16
PAST_LEN = 1024

CHUNK = 64
N_META = 16
EPS = 1e-6

GLA_HEADS = 4
GLA_DK = D_MODEL // 2 // GLA_HEADS
GLA_DV = D_MODEL // GLA_HEADS
GLA_RANK = 16
GLA_GATE_NORM = 16.0
GLA_BLOCK = 16
GLA_LOG_ALPHA_MIN = -5.0

MLA_HEADS = 16
MLA_Q_RANK = 512
MLA_KV_RANK = 512
MLA_NOPE = 128
MLA_ROPE = 64
MLA_V = 128
ROPE_THETA = 10000.0
Q_BLOCK = 128

D_FF = 5632
CONV_W = 3

GLA_QK = GLA_HEADS * GLA_DK
GLA_VW = GLA_HEADS * GLA_DV
MLA_QW = MLA_HEADS * (MLA_NOPE + MLA_ROPE)
IN_SPLITS = (GLA_QK, GLA_QK, GLA_VW, GLA_VW, GLA_RANK, MLA_Q_RANK, MLA_KV_RANK, MLA_ROPE, D_MODEL, D_MODEL)
IN_COLS = sum(IN_SPLITS)

kernel_name = "hybrid_gla_mla_convffn_stream_step"


def rmsnorm(x, g):
    x32 = x.astype(jnp.float32)
    y = x32 * lax.rsqrt(jnp.mean(x32 * x32, axis=-1, keepdims=True) + EPS)
    return (y * g.astype(jnp.float32)).astype(x.dtype)


def split_cols(h):
    out, start = [], 0
    for w in IN_SPLITS:
        out.append(h[..., start:start + w])
        start += w
    return out


def rope_angles(pos):
    inv = ROPE_THETA ** (-jnp.arange(0, MLA_ROPE, 2, dtype=jnp.float32) / MLA_ROPE)
    ang = pos.astype(jnp.float32)[:, None] * inv[None, :]
    return jnp.cos(ang), jnp.sin(ang)


def apply_rope(x, cos, sin):
    x32 = x.astype(jnp.float32)
    x1, x2 = x32[..., :MLA_ROPE // 2], x32[..., MLA_ROPE // 2:]
    return jnp.concatenate([x1 * cos - x2 * sin, x2 * cos + x1 * sin], -1).astype(x.dtype)


def gla_recurrence(q, k, v, log_a, s0):
    B, T, H, _ = q.shape
    pad = (-T) % GLA_BLOCK
    nb = (T + pad) // GLA_BLOCK

    def blk(a):
        a = jnp.pad(a.astype(jnp.float32), ((0, 0), (0, pad), (0, 0), (0, 0)))
        return a.reshape(B, nb, GLA_BLOCK, H, a.shape[-1])

    q, k, v, log_a = blk(q), blk(k), blk(v), blk(log_a)
    b = jnp.cumsum(log_a, axis=2)
    b_last = b[:, :, -1:]
    q_dec = q * jnp.exp(b)
    k_inv = k * jnp.exp(-b)
    k_end = k * jnp.exp(b_last - b)
    causal = jnp.tril(jnp.ones((GLA_BLOCK, GLA_BLOCK), dtype=bool))
    att = jnp.where(causal, jnp.einsum("bnchd,bnshd->bnhcs", q_dec, k_inv), 0.0)
    o_intra = jnp.einsum("bnhcs,bnshe->bnche", att, v)
    decay = jnp.exp(b_last[:, :, 0])

    def step(S, inp):
        q_n, k_n, v_n, d_n = inp
        o_n = jnp.einsum("bchd,bhde->bche", q_n, S)
        S = d_n[..., None] * S + jnp.einsum("bchd,bche->bhde", k_n, v_n)
        return S, o_n

    xs = tuple(jnp.moveaxis(a, 1, 0) for a in (q_dec, k_end, v, decay))
    S, o_inter = lax.scan(step, s0.astype(jnp.float32), xs)
    o = o_intra + jnp.moveaxis(o_inter, 0, 1)
    return o.reshape(B, nb * GLA_BLOCK, H, v.shape[-1])[:, :T], S


def attend(q_nope, q_pe, k_nope, k_pe, v, q_chunk, k_chunk):
    scale = (MLA_NOPE + MLA_ROPE) ** -0.5
    s = (jnp.einsum("bqhd,bkhd->bhqk", q_nope, k_nope).astype(jnp.float32)
         + jnp.einsum("bqhr,bkr->bhqk", q_pe, k_pe).astype(jnp.float32)) * scale
    visible = q_chunk[:, None] >= k_chunk[None, :]
    p = jax.nn.softmax(jnp.where(visible, s, -1e30), axis=-1)
    return jnp.einsum("bhqk,bkhe->bqhe", p.astype(v.dtype), v)


def block_causal_attention(q_nope, q_pe, k_nope, k_pe, v, q_chunk, k_chunk):
    B, Tq = q_nope.shape[:2]
    if Tq <= Q_BLOCK:
        return attend(q_nope, q_pe, k_nope, k_pe, v, q_chunk, k_chunk)
    pad = (-Tq) % Q_BLOCK
    nq = (Tq + pad) // Q_BLOCK

    def blocks(a):
        a = jnp.pad(a, [(0, 0), (0, pad)] + [(0, 0)] * (a.ndim - 2))
        return jnp.moveaxis(a.reshape(B, nq, Q_BLOCK, *a.shape[2:]), 1, 0)

    qc = jnp.pad(q_chunk, (0, pad), mode="edge").reshape(nq, Q_BLOCK)
    out = lax.map(lambda a: attend(a[0], a[1], k_nope, k_pe, v, a[2], k_chunk),
                  (blocks(q_nope), blocks(q_pe), qc))
    out = jnp.moveaxis(out, 0, 1).reshape(B, nq * Q_BLOCK, MLA_HEADS, v.shape[-1])
    return out[:, :Tq]


def trunk_layer(x, pos, q_chunk, past_latent, past_krope, past_chunk, gla_state, conv_hist,
                g_mix, w_in, w_a2, b_a, g_gla_out, g_q, w_uq, g_kv, w_uk, w_uv, w_o,
                g_ffn, w_up, conv_w, conv_b, w_down):
    B, T, _ = x.shape
    h = rmsnorm(x, g_mix)
    q_g, k_g, v_g, r_g, a_g, c_q, c_kv, k_pe, gate_a, gate_b = split_cols(h @ w_in)

    q_g = q_g.reshape(B, T, GLA_HEADS, GLA_DK) * (GLA_DK ** -0.5)
    k_g = k_g.reshape(B, T, GLA_HEADS, GLA_DK)
    v_g = v_g.reshape(B, T, GLA_HEADS, GLA_DV)
    log_a = jax.nn.log_sigmoid((a_g @ w_a2 + b_a).astype(jnp.float32)) / GLA_GATE_NORM
    log_a = jnp.maximum(log_a, GLA_LOG_ALPHA_MIN).reshape(B, T, GLA_HEADS, GLA_DK)
    o_g, new_state = gla_recurrence(q_g, k_g, v_g, log_a, gla_state)
    o_g = rmsnorm(o_g.astype(x.dtype), g_gla_out).reshape(B, T, GLA_VW)
    branch_a = o_g * jax.nn.silu(r_g)

    q = (rmsnorm(c_q, g_q) @ w_uq).reshape(B, T, MLA_HEADS, MLA_NOPE + MLA_ROPE)
    q_nope, q_pe = q[..., :MLA_NOPE], q[..., MLA_NOPE:]
    latent = rmsnorm(c_kv, g_kv)
    cos, sin = rope_angles(pos)
    k_pe = apply_rope(k_pe, cos, sin)
    q_pe = apply_rope(q_pe, cos[:, None], sin[:, None])
    all_latent = jnp.concatenate([past_latent.astype(latent.dtype), latent], 1)
    all_kpe = jnp.concatenate([past_krope.astype(k_pe.dtype), k_pe], 1)
    Tk = all_latent.shape[1]
    k_nope = (all_latent @ w_uk).reshape(B, Tk, MLA_HEADS, MLA_NOPE)
    v = (all_latent @ w_uv).reshape(B, Tk, MLA_HEADS, MLA_V)
    k_chunk = jnp.concatenate([past_chunk, q_chunk])
    o_m = block_causal_attention(q_nope, q_pe, k_nope, all_kpe, v, q_chunk, k_chunk)
    branch_b = o_m.reshape(B, T, MLA_HEADS * MLA_V)

    merged = jax.nn.sigmoid(gate_a) * branch_a + jax.nn.sigmoid(gate_b) * branch_b
    x = x + merged @ w_o

    u = rmsnorm(x, g_ffn) @ w_up
    u_ext = jnp.concatenate([conv_hist.astype(u.dtype), u], 1)
    c = conv_b + conv_w[0] * u_ext[:, 0:T]
    for j in range(1, CONV_W):
        c = c + conv_w[j] * u_ext[:, j:j + T]
    x = x + (jax.nn.silu(c[..., :D_FF]) * c[..., D_FF:]) @ w_down
    return x, latent, k_pe, new_state.astype(x.dtype), u_ext[:, -(CONV_W - 1):]


def setup_inputs(seed: int = 0) -> dict:
    key = jax.random.key(seed)
    ks = jax.random.split(key, 32)
    f32 = jnp.float32

    def nrm(k, shape, scale):
        return scale * jax.random.normal(k, shape, f32)

    def gain(k, shape):
        return 1.0 + 0.02 * jax.random.normal(k, shape, f32)

    return {
        "x_prompt": nrm(ks[0], (BATCH, SEQ, D_MODEL), 1.0),
        "x_sample": nrm(ks[1], (DEC_BATCH, DEC_SEQ, D_MODEL), 1.0),
        "cache_mla_latent": nrm(ks[2], (DEPTH, DEC_BATCH, PAST_LEN, MLA_KV_RANK), 1.0),
        "cache_mla_krope": nrm(ks[3], (DEPTH, DEC_BATCH, PAST_LEN, MLA_ROPE), 1.0),
        "state_gla": nrm(ks[4], (DEPTH, DEC_BATCH, GLA_HEADS, GLA_DK, GLA_DV), 1.0),
        "cache_ffn_conv": nrm(ks[5], (DEPTH, DEC_BATCH, CONV_W - 1, 2 * D_FF), 1.0),
        "meta_tokens": nrm(ks[6], (N_META, D_MODEL), 1.0),
        "g_mix": gain(ks[7], (DEPTH, D_MODEL)),
        "w_in": nrm(ks[8], (DEPTH, D_MODEL, IN_COLS), D_MODEL ** -0.5),
        "w_a2": nrm(ks[9], (DEPTH, GLA_RANK, GLA_QK), GLA_RANK ** -0.5),
        "b_a": nrm(ks[10], (DEPTH, GLA_QK), 0.1),
        "g_gla_out": gain(ks[11], (DEPTH, GLA_DV)),
        "g_q": gain(ks[12], (DEPTH, MLA_Q_RANK)),
        "w_uq": nrm(ks[13], (DEPTH, MLA_Q_RANK, MLA_QW), MLA_Q_RANK ** -0.5),
        "g_kv": gain(ks[14], (DEPTH, MLA_KV_RANK)),
        "w_uk": nrm(ks[15], (DEPTH, MLA_KV_RANK, MLA_HEADS * MLA_NOPE), MLA_KV_RANK ** -0.5),
        "w_uv": nrm(ks[16], (DEPTH, MLA_KV_RANK, MLA_HEADS * MLA_V), MLA_KV_RANK ** -0.5),
        "w_o": nrm(ks[17], (DEPTH, D_MODEL, D_MODEL), D_MODEL ** -0.5),
        "g_ffn": gain(ks[18], (DEPTH, D_MODEL)),
        "w_up": nrm(ks[19], (DEPTH, D_MODEL, 2 * D_FF), D_MODEL ** -0.5),
        "conv_w": nrm(ks[20], (DEPTH, CONV_W, 2 * D_FF), CONV_W ** -0.5),
        "conv_b": nrm(ks[21], (DEPTH, 2 * D_FF), 0.02),
        "w_down": nrm(ks[22], (DEPTH, D_FF, D_MODEL), D_FF ** -0.5),
        "final_norm": gain(ks[23], (D_MODEL,)),
    }


def reference(x_prompt, x_sample, cache_mla_latent, cache_mla_krope, state_gla, cache_ffn_conv,
              meta_tokens, g_mix, w_in, w_a2, b_a, g_gla_out, g_q, w_uq, g_kv, w_uk, w_uv, w_o,
              g_ffn, w_up, conv_w, conv_b, w_down, final_norm):
    B, T_p, _ = x_prompt.shape
    T_s = x_sample.shape[1]
    P = cache_mla_latent.shape[2]
    dt = x_prompt.dtype

    xp = jnp.concatenate([jnp.broadcast_to(meta_tokens.astype(dt)[None], (B, N_META, D_MODEL)), x_prompt], 1)
    idx = jnp.arange(T_p + N_META, dtype=jnp.int32)
    chunk_p = jnp.where(idx < N_META, -1, (idx - N_META) // CHUNK).astype(jnp.int32)
    no_chunk = jnp.zeros((0,), jnp.int32)
    pos_s = P + jnp.arange(T_s, dtype=jnp.int32)
    chunk_s = pos_s // CHUNK
    past_chunk_s = jnp.arange(P, dtype=jnp.int32) // CHUNK
    xs = x_sample

    lat_p, kpe_p, gla_p, conv_p = [], [], [], []
    lat_s, kpe_s, gla_s, conv_s = [], [], [], []
    for l in range(DEPTH):
        wl = (g_mix[l], w_in[l], w_a2[l], b_a[l], g_gla_out[l], g_q[l], w_uq[l], g_kv[l],
              w_uk[l], w_uv[l], w_o[l], g_ffn[l], w_up[l], conv_w[l], conv_b[l], w_down[l])
        xp, lat, kpe, st, cv = trunk_layer(
            xp, idx, chunk_p,
            jnp.zeros((B, 0, MLA_KV_RANK), dt), jnp.zeros((B, 0, MLA_ROPE), dt), no_chunk,
            jnp.zeros((B, GLA_HEADS, GLA_DK, GLA_DV), jnp.float32),
            jnp.zeros((B, CONV_W - 1, 2 * D_FF), dt), *wl)
        lat_p.append(lat); kpe_p.append(kpe); gla_p.append(st); conv_p.append(cv)
        xs, lat, kpe, st, cv = trunk_layer(
            xs, pos_s, chunk_s, cache_mla_latent[l], cache_mla_krope[l], past_chunk_s,
            state_gla[l], cache_ffn_conv[l], *wl)
        lat_s.append(lat); kpe_s.append(kpe); gla_s.append(st); conv_s.append(cv)

    y_prompt = rmsnorm(xp, final_norm)[:, N_META:]
    y_sample = rmsnorm(xs, final_norm)
    return (y_prompt, y_sample,
            jnp.stack(lat_p), jnp.stack(kpe_p), jnp.stack(gla_p), jnp.stack(conv_p),
            jnp.stack(lat_s), jnp.stack(kpe_s), jnp.stack(gla_s), jnp.stack(conv_s))
```

```python
import functools

import jax
import jax.numpy as jnp
from jax import lax
from jax.experimental import pallas as pl
from jax.experimental.pallas import tpu as pltpu

BF16 = jnp.bfloat16
F32 = jnp.float32

CHUNK = 64
CHUNK_SHIFT = 6
N_META = 16
EPS = 1e-6
GLA_HEADS = 4
GLA_GATE_NORM = 16.0
GLA_LOG_ALPHA_MIN = -5.0
MLA_HEADS = 16
MLA_NOPE = 128
MLA_ROPE = 64
MLA_V = 128
ROPE_THETA = 10000.0
CONV_W = 3
NEG_BIG = -1e30

LANE = 128
ROW_TILE = 768
VMEM_LIMIT = 56 * 1024 * 1024


def _cparams(sem, vmem=VMEM_LIMIT):
    return pltpu.CompilerParams(dimension_semantics=sem, vmem_limit_bytes=vmem)


def _rmsnorm(x, g):
    return x * lax.rsqrt(jnp.mean(x * x, axis=-1, keepdims=True) + EPS) * g


def _sigmoid(x):
    return 1.0 / (1.0 + jnp.exp(-x))


def _pick(n, cands):
    for c in cands:
        if n % c == 0:
            return c
    raise ValueError(f"no tile in {cands} divides {n}")


def _norm_cast_kernel(x_ref, g_ref, o_ref):
    o_ref[...] = _rmsnorm(x_ref[...], g_ref[...]).astype(o_ref.dtype)


def _norm_cast(x, g):
    m, d = x.shape
    tm = _pick(m, (384, 256, 128))
    return pl.pallas_call(
        _norm_cast_kernel,
        grid=(m // tm,),
        in_specs=[pl.BlockSpec((tm, d), lambda i: (i, 0)),
                  pl.BlockSpec((1, d), lambda i: (0, 0))],
        out_specs=pl.BlockSpec((tm, d), lambda i: (i, 0)),
        out_shape=jax.ShapeDtypeStruct((m, d), BF16),
        compiler_params=_cparams(("parallel",)),
        name="norm_cast",
    )(x, g.reshape(1, d))


def _matmul_kernel(a_ref, b_ref, o_ref):
    o_ref[...] = jnp.dot(a_ref[...], b_ref[...], preferred_element_type=F32).astype(o_ref.dtype)


def _matmul(a, b, out_dtype, tn):
    m, k = a.shape
    n = b.shape[1]
    tm = _pick(m, (ROW_TILE, 512, 384, 128))
    return pl.pallas_call(
        _matmul_kernel,
        grid=(n // tn, m // tm),
        in_specs=[pl.BlockSpec((tm, k), lambda j, i: (i, 0)),
                  pl.BlockSpec((k, tn), lambda j, i: (0, j))],
        out_specs=pl.BlockSpec((tm, tn), lambda j, i: (i, j)),
        out_shape=jax.ShapeDtypeStruct((m, n), out_dtype),
        compiler_params=_cparams(("parallel", "parallel")),
        name="in_proj",
    )(a, b)


def _split3(x):
    a = x.astype(BF16)
    r1 = x - a.astype(F32)
    b = r1.astype(BF16)
    c = (r1 - b.astype(F32)).astype(BF16)
    return a, b, c


def _gla_kernel(q_ref, k_ref, v_ref, r_ref, ga_ref, a_ref, wa_ref, ba_ref, go_ref, s0_ref,
                o_ref, sout_ref, s_scr, *, C, SB, T, scale):
    c_idx = pl.program_id(2)
    n_chunks = pl.num_programs(2)
    dv = v_ref.shape[1]

    @pl.when(c_idx == 0)
    def _():
        s_scr[...] = s0_ref[0, 0]

    z = jnp.dot(a_ref[...].astype(BF16), wa_ref[...], preferred_element_type=F32) + ba_ref[...]
    log_sig = jnp.minimum(z, 0.0) - jnp.log1p(jnp.exp(-jnp.abs(z)))
    la = jnp.maximum(log_sig * (1.0 / GLA_GATE_NORM), GLA_LOG_ALPHA_MIN)
    rows = c_idx * C + lax.broadcasted_iota(jnp.int32, (C, 1), 0)
    la = jnp.where(rows < T, la, 0.0)

    ri = lax.broadcasted_iota(jnp.int32, (C, C), 0)
    ci = lax.broadcasted_iota(jnp.int32, (C, C), 1)
    tri = jnp.where(ri >= ci, 1.0, 0.0).astype(BF16)
    ones = jnp.ones((C, LANE), BF16)
    cs = jnp.zeros_like(la)
    dsum = jnp.zeros((la.shape[1], LANE), F32)
    for piece in _split3(la):
        cs = cs + jnp.dot(tri, piece, preferred_element_type=F32)
        dsum = dsum + lax.dot_general(piece, ones, (((0,), (0,)), ((), ())),
                                      preferred_element_type=F32)
    c_last = cs[C - 1:C, :]

    q = q_ref[...].astype(F32) * scale
    k = k_ref[...].astype(F32)
    v = v_ref[...]
    s_old = s_scr[...]

    o_inter = jnp.dot((q * jnp.exp(cs)).astype(BF16), s_old.astype(BF16),
                      preferred_element_type=F32)
    k_end = (k * jnp.exp(c_last - cs)).astype(BF16)
    upd = lax.dot_general(k_end, v, (((0,), (0,)), ((), ())), preferred_element_type=F32)
    dcol = jnp.exp(dsum)
    s_new = jnp.concatenate([dcol] * (dv // LANE), axis=1) * s_old + upd
    s_scr[...] = s_new

    sr = lax.broadcasted_iota(jnp.int32, (SB, SB), 0)
    sc = lax.broadcasted_iota(jnp.int32, (SB, SB), 1)
    causal = sr >= sc
    nt = (((1,), (1,)), ((), ()))
    outs = []
    for i in range(C // SB):
        lo = i * SB
        cs_i = cs[lo:lo + SB]
        q_i = q[lo:lo + SB]
        k_i = k[lo:lo + SB]
        start = cs[lo - 1:lo] if i > 0 else jnp.zeros_like(c_last)
        mid = 0.5 * (start + cs[lo + SB - 1:lo + SB])
        qd = (q_i * jnp.exp(cs_i - mid)).astype(BF16)
        kd = (k_i * jnp.exp(mid - cs_i)).astype(BF16)
        att = lax.dot_general(qd, kd, nt, preferred_element_type=F32)
        att = jnp.where(causal, att, 0.0)
        o_i = jnp.dot(att.astype(BF16), v[lo:lo + SB], preferred_element_type=F32)
        if i > 0:
            qo = (q_i * jnp.exp(cs_i - start)).astype(BF16)
            ko = (k[:lo] * jnp.exp(start - cs[:lo])).astype(BF16)
            att_o = lax.dot_general(qo, ko, nt, preferred_element_type=F32)
            o_i = o_i + jnp.dot(att_o.astype(BF16), v[:lo], preferred_element_type=F32)
        outs.append(o_i)
    o = o_inter + (jnp.concatenate(outs, axis=0) if len(outs) > 1 else outs[0])

    on = _rmsnorm(o, go_ref[...])
    r = r_ref[...].astype(F32)
    g = ga_ref[...].astype(F32)
    o_ref[...] = (_sigmoid(g) * (on * (r * _sigmoid(r)))).astype(o_ref.dtype)

    @pl.when(c_idx == n_chunks - 1)
    def _():
        sout_ref[0, 0] = s_scr[...]


def _gla(big, small, wa_pad, b_a, g_out, s0, *, B, T, Tp, dk, dv, col):
    C = min(128, Tp)
    SB = min(32, C)
    nc = Tp // C
    H = GLA_HEADS
    rb = lambda b, h, c: b * nc + c
    kern = functools.partial(_gla_kernel, C=C, SB=SB, T=T, scale=dk ** -0.5)
    return pl.pallas_call(
        kern,
        grid=(B, H, nc),
        in_specs=[
            pl.BlockSpec((C, dk), lambda b, h, c: (rb(b, h, c), col["q"] // dk + h)),
            pl.BlockSpec((C, dk), lambda b, h, c: (rb(b, h, c), col["k"] // dk + h)),
            pl.BlockSpec((C, dv), lambda b, h, c: (rb(b, h, c), col["v"] // dv + h)),
            pl.BlockSpec((C, dv), lambda b, h, c: (rb(b, h, c), col["r"] // dv + h)),
            pl.BlockSpec((C, dv), lambda b, h, c: (rb(b, h, c), col["ga"] // dv + h)),
            pl.BlockSpec((C, LANE), lambda b, h, c: (rb(b, h, c), col["a"] // LANE)),
            pl.BlockSpec((LANE, dk), lambda b, h, c: (0, h)),
            pl.BlockSpec((1, dk), lambda b, h, c: (0, h)),
            pl.BlockSpec((1, dv), lambda b, h, c: (0, 0)),
            pl.BlockSpec((1, 1, dk, dv), lambda b, h, c: (b, h, 0, 0)),
        ],
        out_specs=[
            pl.BlockSpec((C, dv), lambda b, h, c: (rb(b, h, c), h)),
            pl.BlockSpec((1, 1, dk, dv), lambda b, h, c: (b, h, 0, 0)),
        ],
        out_shape=[jax.ShapeDtypeStruct((B * Tp, H * dv), BF16),
                   jax.ShapeDtypeStruct((B, H, dk, dv), F32)],
        scratch_shapes=[pltpu.VMEM((dk, dv), F32)],
        compiler_params=_cparams(("parallel", "parallel", "arbitrary")),
        name="gla",
    )(big, big, big, big, big, small, wa_pad, b_a.reshape(1, -1), g_out.reshape(1, -1), s0)


def _qprep_kernel(cq_ref, gq_ref, wn_ref, wp_ref, wps_ref, cos_ref, sin_ref, q_ref):
    hq = _rmsnorm(cq_ref[...], gq_ref[...]).astype(BF16)
    qn = jnp.dot(hq, wn_ref[...], preferred_element_type=F32)
    qp = jnp.dot(hq, wp_ref[...], preferred_element_type=F32)
    qs = jnp.dot(hq, wps_ref[...], preferred_element_type=F32)
    cos = cos_ref[...]
    sin = sin_ref[...]
    for h in range(MLA_HEADS):
        sl = slice(h * LANE, (h + 1) * LANE)
        q_ref[h, :, 0:LANE] = qn[:, sl].astype(BF16)
        q_ref[h, :, LANE:2 * LANE] = (qp[:, sl] * cos + qs[:, sl] * sin).astype(BF16)


def _qprep(small, g_q, wn, wp, wps, cos_t, sin_t, *, col):
    m = small.shape[0]
    rq = wn.shape[0]
    tm = _pick(m, (256, 128))
    full = lambda i: (0, 0)
    return pl.pallas_call(
        _qprep_kernel,
        grid=(m // tm,),
        in_specs=[pl.BlockSpec((tm, rq), lambda i: (i, col["cq"] // rq)),
                  pl.BlockSpec((1, rq), full),
                  pl.BlockSpec(wn.shape, full),
                  pl.BlockSpec(wp.shape, full),
                  pl.BlockSpec(wps.shape, full),
                  pl.BlockSpec((tm, LANE), lambda i: (i, 0)),
                  pl.BlockSpec((tm, LANE), lambda i: (i, 0))],
        out_specs=pl.BlockSpec((MLA_HEADS, tm, 2 * LANE), lambda i: (0, i, 0)),
        out_shape=jax.ShapeDtypeStruct((MLA_HEADS, m, 2 * LANE), BF16),
        compiler_params=_cparams(("parallel",)),
        name="mla_q",
    )(small, g_q.reshape(1, -1), wn, wp, wps, cos_t, sin_t)


def _lat_kernel(ckv_ref, kpe_ref, gkv_ref, cos_ref, sin_ref, lat_ref, kr_ref):
    lat_ref[...] = _rmsnorm(ckv_ref[...], gkv_ref[...])
    blk = kpe_ref[...]
    kr_ref[...] = blk * cos_ref[...] + pltpu.roll(blk, LANE // 2, 1) * sin_ref[...]


def _lat(small, g_kv, cos_t, sin_t, *, col):
    m = small.shape[0]
    rk = g_kv.shape[0]
    tm = _pick(m, (256, 128))
    return pl.pallas_call(
        _lat_kernel,
        grid=(m // tm,),
        in_specs=[pl.BlockSpec((tm, rk), lambda i: (i, col["ckv"] // rk)),
                  pl.BlockSpec((tm, LANE), lambda i: (i, col["kpe"] // LANE)),
                  pl.BlockSpec((1, rk), lambda i: (0, 0)),
                  pl.BlockSpec((tm, LANE), lambda i: (i, 0)),
                  pl.BlockSpec((tm, LANE), lambda i: (i, 0))],
        out_specs=[pl.BlockSpec((tm, rk), lambda i: (i, 0)),
                   pl.BlockSpec((tm, LANE), lambda i: (i, 0))],
        out_shape=[jax.ShapeDtypeStruct((m, rk), F32),
                   jax.ShapeDtypeStruct((m, LANE), F32)],
        compiler_params=_cparams(("parallel",)),
        name="mla_latent",
    )(small, small, g_kv.reshape(1, -1), cos_t, sin_t)


def _kvup_kernel(lat_ref, kr_ref, wuk_ref, wuv_ref, k_ref, v_ref):
    lat = lat_ref[...].astype(BF16)
    kn = jnp.dot(lat, wuk_ref[...], preferred_element_type=F32)
    vv = jnp.dot(lat, wuv_ref[...], preferred_element_type=F32)
    kp = kr_ref[...].astype(BF16)
    for h in range(MLA_HEADS):
        sl = slice(h * LANE, (h + 1) * LANE)
        k_ref[h, :, 0:LANE] = kn[:, sl].astype(BF16)
        k_ref[h, :, LANE:2 * LANE] = kp
        v_ref[h] = vv[:, sl].astype(BF16)


def _kvup(lat, kr, wuk, wuv):
    m, rk = lat.shape
    tm = _pick(m, (256, 128))
    full = lambda i: (0, 0)
    return pl.pallas_call(
        _kvup_kernel,
        grid=(m // tm,),
        in_specs=[pl.BlockSpec((tm, rk), lambda i: (i, 0)),
                  pl.BlockSpec((tm, LANE), lambda i: (i, 0)),
                  pl.BlockSpec(wuk.shape, full),
                  pl.BlockSpec(wuv.shape, full)],
        out_specs=[pl.BlockSpec((MLA_HEADS, tm, 2 * LANE), lambda i: (0, i, 0)),
                   pl.BlockSpec((MLA_HEADS, tm, LANE), lambda i: (0, i, 0))],
        out_shape=[jax.ShapeDtypeStruct((MLA_HEADS, m, 2 * LANE), BF16),
                   jax.ShapeDtypeStruct((MLA_HEADS, m, LANE), BF16)],
        compiler_params=_cparams(("parallel",)),
        name="mla_kv",
    )(lat, kr, wuk, wuv)


def _last_kblock(qi, *, tq, tk, nk, q_off, k_off):
    top_chunk = ((qi + 1) * tq - 1 + q_off) // CHUNK
    last_key = (top_chunk + 1) * CHUNK - 1 - k_off
    return jnp.minimum(last_key // tk, nk - 1)


def _attn_kernel(q_ref, k_ref, v_ref, o_ref, m_scr, l_scr, acc_scr, *, hps, tq, tk, nk,
                 q_off, k_off, scale):
    qi = pl.program_id(2)
    ki = pl.program_id(3)

    @pl.when(ki == 0)
    def _():
        m_scr[...] = jnp.full(m_scr.shape, NEG_BIG, F32)
        l_scr[...] = jnp.zeros(l_scr.shape, F32)
        acc_scr[...] = jnp.zeros(acc_scr.shape, F32)

    @pl.when(ki <= _last_kblock(qi, tq=tq, tk=tk, nk=nk, q_off=q_off, k_off=k_off))
    def _():
        q_chunk = (qi * tq + q_off + lax.broadcasted_iota(jnp.int32, (tq, 1), 0)) >> CHUNK_SHIFT
        k_chunk = (ki * tk + k_off + lax.broadcasted_iota(jnp.int32, (1, tk), 1)) >> CHUNK_SHIFT
        visible = q_chunk >= k_chunk

        def head(h, carry):
            s = lax.dot_general(q_ref[h], k_ref[h], (((1,), (1,)), ((), ())),
                                preferred_element_type=F32) * scale
            s = jnp.where(visible, s, NEG_BIG)
            m_prev = m_scr[h]
            m_new = jnp.maximum(m_prev, jnp.max(s, axis=-1, keepdims=True))
            p = jnp.exp(s - m_new)
            alpha = jnp.exp(m_prev - m_new)
            l_scr[h] = alpha * l_scr[h] + jnp.sum(p, axis=-1, keepdims=True)
            acc_scr[h] = alpha * acc_scr[h] + jnp.dot(p.astype(BF16), v_ref[h],
                                                      preferred_element_type=F32)
            m_scr[h] = m_new
            return carry

        lax.fori_loop(0, hps, head, 0)

    @pl.when(ki == nk - 1)
    def _():
        for h in range(hps):
            o_ref[:, h * LANE:(h + 1) * LANE] = (acc_scr[h] / l_scr[h]).astype(o_ref.dtype)


def _attention(q, k, v, *, B, Tq, Tk, tq, tk, hps, q_off, k_off):
    nq = Tq // tq
    nk = Tk // tk
    hg = MLA_HEADS // hps
    dqk = q.shape[2]
    dvh = v.shape[2]
    last = functools.partial(_last_kblock, tq=tq, tk=tk, nk=nk, q_off=q_off, k_off=k_off)
    kern = functools.partial(_attn_kernel, hps=hps, tq=tq, tk=tk, nk=nk, q_off=q_off,
                             k_off=k_off, scale=(MLA_NOPE + MLA_ROPE) ** -0.5)
    kv_row = lambda b, g, i, j: b * nk + jnp.minimum(j, last(i))
    return pl.pallas_call(
        kern,
        grid=(B, hg, nq, nk),
        in_specs=[pl.BlockSpec((hps, tq, dqk), lambda b, g, i, j: (g, b * nq + i, 0)),
                  pl.BlockSpec((hps, tk, dqk), lambda b, g, i, j: (g, kv_row(b, g, i, j), 0)),
                  pl.BlockSpec((hps, tk, dvh), lambda b, g, i, j: (g, kv_row(b, g, i, j), 0))],
        out_specs=pl.BlockSpec((tq, hps * dvh), lambda b, g, i, j: (b * nq + i, g)),
        out_shape=jax.ShapeDtypeStruct((B * Tq, MLA_HEADS * dvh), BF16),
        scratch_shapes=[pltpu.VMEM((hps, tq, 1), F32),
                        pltpu.VMEM((hps, tq, 1), F32),
                        pltpu.VMEM((hps, tq, dvh), F32)],
        compiler_params=_cparams(("parallel", "parallel", "parallel", "arbitrary")),
        name="mla_attn",
    )(q, k, v)


def _merge_kernel(a_ref, gb_ref, om_ref, x_ref, wo_ref, gf_ref, x1_ref, h2_ref):
    merged = a_ref[...].astype(F32) + _sigmoid(gb_ref[...].astype(F32)) * om_ref[...].astype(F32)
    x1 = x_ref[...] + jnp.dot(merged.astype(BF16), wo_ref[...], preferred_element_type=F32)
    x1_ref[...] = x1
    h2_ref[...] = _rmsnorm(x1, gf_ref[...]).astype(BF16)


def _merge(branch_a, big, o_m, x, wo, g_ffn, *, col):
    m, d = x.shape
    tm = _pick(m, (384, 256, 128))
    row = lambda i: (i, 0)
    return pl.pallas_call(
        _merge_kernel,
        grid=(m // tm,),
        in_specs=[pl.BlockSpec((tm, d), row),
                  pl.BlockSpec((tm, d), lambda i: (i, col["gb"] // d)),
                  pl.BlockSpec((tm, d), row),
                  pl.BlockSpec((tm, d), row),
                  pl.BlockSpec(wo.shape, lambda i: (0, 0)),
                  pl.BlockSpec((1, d), lambda i: (0, 0))],
        out_specs=[pl.BlockSpec((tm, d), row), pl.BlockSpec((tm, d), row)],
        out_shape=[jax.ShapeDtypeStruct((m, d), F32), jax.ShapeDtypeStruct((m, d), BF16)],
        compiler_params=_cparams(("parallel",)),
        name="merge_out_proj",
    )(branch_a, big, o_m, x, wo, g_ffn.reshape(1, -1))


HALO = 8


def _ffn_kernel(h_ref, x1_ref, wa_ref, wb_ref, wd_ref, cwa_ref, cwb_ref, cba_ref, cbb_ref,
                ha_ref, hb_ref, gf_ref, y_ref, ca_ref, cb_ref, ext_scr, carry_scr,
                *, bb, r, tf, loc, carried):
    rt = pl.program_id(1)
    f = pl.program_id(2)
    nf = pl.num_programs(2)
    d = h_ref.shape[2]
    h = h_ref[...].reshape(bb * r, d)

    if carried:
        @pl.when(rt == 0)
        def _():
            carry_scr[f, 0] = ha_ref[...]
            carry_scr[f, 1] = hb_ref[...]

    conv = []
    for half, (w_ref, cw_ref, cbias_ref, hist_ref, cout_ref) in enumerate(
            ((wa_ref, cwa_ref, cba_ref, ha_ref, ca_ref),
             (wb_ref, cwb_ref, cbb_ref, hb_ref, cb_ref))):
        u = jnp.dot(h, w_ref[...], preferred_element_type=F32).reshape(bb, r, tf)
        ext_scr[half, :, HALO:HALO + r, :] = u
        ext_scr[half, :, HALO - 2:HALO, :] = carry_scr[f, half] if carried else hist_ref[...]
        u1 = ext_scr[half, :, HALO - 1:HALO - 1 + r, :]
        u2 = ext_scr[half, :, HALO - 2:HALO - 2 + r, :]
        cw = cw_ref[...]
        conv.append(cbias_ref[...] + cw[0:1] * u2 + cw[1:2] * u1 + cw[2:3] * u)
        if carried:
            carry_scr[f, half] = ext_scr[half, :, HALO + r - 2:HALO + r, :]
        cout_ref[0] = ext_scr[half, :, HALO + loc:HALO + loc + 2, :]

    act = (conv[0] * _sigmoid(conv[0])) * conv[1]
    part = jnp.dot(act.reshape(bb * r, tf).astype(BF16), wd_ref[...],
                   preferred_element_type=F32).reshape(bb, r, d)

    @pl.when(f == 0)
    def _():
        y_ref[...] = part

    @pl.when(f > 0)
    def _():
        y_ref[...] += part

    @pl.when(f == nf - 1)
    def _():
        y_ref[...] = _rmsnorm(x1_ref[...] + y_ref[...], gf_ref[...])


def _ffn(h2, x1, w_up, w_down, conv_w, conv_b, hist, g_final, *, B, T, Tp):
    d = h2.shape[1]
    dff = w_down.shape[0]
    tf = _pick(dff, (256, 128))
    nf = dff // tf
    if Tp <= 128:
        bb, r = B, Tp
    else:
        bb, r = 1, _pick(Tp, (ROW_TILE, 128))
    nrt = Tp // r
    carried = nrt > 1
    loc = (T - 2) - (nrt - 1) * r
    assert 0 <= loc <= r - 2, "final two valid rows must sit in the last row tile"
    kern = functools.partial(_ffn_kernel, bb=bb, r=r, tf=tf, loc=loc, carried=carried)
    rows = lambda s, t, f: (s, t, 0)
    carry_shape = (nf, 2, bb, 2, tf) if carried else (1, 1, 1, 2, LANE)
    y, ca, cb = pl.pallas_call(
        kern,
        grid=(B // bb, nrt, nf),
        in_specs=[pl.BlockSpec((bb, r, d), rows),
                  pl.BlockSpec((bb, r, d), rows),
                  pl.BlockSpec((d, tf), lambda s, t, f: (0, f)),
                  pl.BlockSpec((d, tf), lambda s, t, f: (0, nf + f)),
                  pl.BlockSpec((tf, d), lambda s, t, f: (f, 0)),
                  pl.BlockSpec((CONV_W, tf), lambda s, t, f: (0, f)),
                  pl.BlockSpec((CONV_W, tf), lambda s, t, f: (0, nf + f)),
                  pl.BlockSpec((1, tf), lambda s, t, f: (0, f)),
                  pl.BlockSpec((1, tf), lambda s, t, f: (0, nf + f)),
                  pl.BlockSpec((bb, 2, tf), lambda s, t, f: (s, 0, f)),
                  pl.BlockSpec((bb, 2, tf), lambda s, t, f: (s, 0, nf + f)),
                  pl.BlockSpec((1, d), lambda s, t, f: (0, 0))],
        out_specs=[pl.BlockSpec((bb, r, d), rows),
                   pl.BlockSpec((1, bb, 2, tf), lambda s, t, f: (t, s, 0, f)),
                   pl.BlockSpec((1, bb, 2, tf), lambda s, t, f: (t, s, 0, f))],
        out_shape=[jax.ShapeDtypeStruct((B, Tp, d), F32),
                   jax.ShapeDtypeStruct((nrt, B, 2, dff), F32),
                   jax.ShapeDtypeStruct((nrt, B, 2, dff), F32)],
        scratch_shapes=[pltpu.VMEM((2, bb, HALO + r, tf), F32),
                        pltpu.VMEM(carry_shape, F32)],
        compiler_params=_cparams(("parallel", "arbitrary", "arbitrary")),
        name="conv_ffn",
    )(h2.reshape(B, Tp, d), x1.reshape(B, Tp, d), w_up, w_up, w_down, conv_w, conv_w,
      conv_b.reshape(1, -1), conv_b.reshape(1, -1), hist, hist, g_final.reshape(1, -1))
    return y, jnp.concatenate([ca[nrt - 1], cb[nrt - 1]], axis=-1)


def _rope_tables(pos):
    half = MLA_ROPE // 2
    inv = ROPE_THETA ** (-jnp.arange(0, MLA_ROPE, 2, dtype=F32) / MLA_ROPE)
    ang = pos.astype(F32)[:, None] * inv[None, :]
    cos, sin = jnp.cos(ang), jnp.sin(ang)
    zero = jnp.zeros((pos.shape[0], LANE - 2 * half), F32)
    return (jnp.concatenate([cos, cos, zero], axis=1),
            jnp.concatenate([-sin, sin, zero], axis=1))


def _stream(x, w, *, B, T, Tp, pos, q_off, k_off, past_lat, past_kr, s0, hist):
    col = w["col"]
    dk, dv = w["dk"], w["dv"]
    h = _norm_cast(x, w["g_mix"])
    big = _matmul(h, w["w_big"], BF16, tn=1024)
    small = _matmul(h, w["w_small"], F32, tn=w["w_small"].shape[1])

    branch_a, state = _gla(big, small, w["wa_pad"], w["b_a"], w["g_gla_out"], s0,
                           B=B, T=T, Tp=Tp, dk=dk, dv=dv, col=col)

    cos_t, sin_t = _rope_tables(pos)
    q = _qprep(small, w["g_q"], w["wq_nope"], w["wq_pe"], w["wq_pe_sw"], cos_t, sin_t, col=col)
    lat, kr = _lat(small, w["g_kv"], cos_t, sin_t, col=col)
    if past_lat is None:
        all_lat, all_kr, Tk = lat, kr, Tp
    else:
        P = past_lat.shape[1]
        rk = lat.shape[1]
        all_lat = jnp.concatenate([past_lat, lat.reshape(B, Tp, rk)], axis=1).reshape(-1, rk)
        past_kr = jnp.pad(past_kr, ((0, 0), (0, 0), (0, LANE - past_kr.shape[2])))
        all_kr = jnp.concatenate([past_kr, kr.reshape(B, Tp, LANE)], axis=1).reshape(-1, LANE)
        Tk = P + Tp
    k, v = _kvup(all_lat, all_kr, w["w_uk"], w["w_uv"])
    if Tp <= 128:
        tq, tk, hps = Tp, Tk, MLA_HEADS
    else:
        tq = tk = _pick(Tp, (ROW_TILE, 128))
        hps = MLA_HEADS // 2
    o_m = _attention(q, k, v, B=B, Tq=Tp, Tk=Tk, tq=tq, tk=tk, hps=hps, q_off=q_off, k_off=k_off)

    x1, h2 = _merge(branch_a, big, o_m, x, w["w_o"], w["g_ffn"], col=col)
    y, conv = _ffn(h2, x1, w["w_up"], w["w_down"], w["conv_w"], w["conv_b"], hist,
                   w["final_norm"], B=B, T=T, Tp=Tp)
    return y, lat, kr, state, conv


def _prep_weights(g_mix, w_in, w_a2, b_a, g_gla_out, g_q, w_uq, g_kv, w_uk, w_uv, w_o,
                  g_ffn, w_up, conv_w, conv_b, w_down, final_norm):
    d = w_in.shape[0]
    rank, gqk = w_a2.shape
    gvw = GLA_HEADS * g_gla_out.shape[0]
    rq, rk = g_q.shape[0], g_kv.shape[0]
    half = MLA_ROPE // 2
    o, offs = 0, {}
    for name, width in (("q", gqk), ("k", gqk), ("v", gvw), ("r", gvw), ("a", rank),
                        ("cq", rq), ("ckv", rk), ("kpe", MLA_ROPE), ("ga", d), ("gb", d)):
        offs[name] = (o, o + width)
        o += width
    assert o == w_in.shape[1]
    sl = lambda n: w_in[:, offs[n][0]:offs[n][1]]
    w_big = jnp.concatenate([sl("q"), sl("k"), sl("v"), sl("r"), sl("ga"), sl("gb")], axis=1)
    kpe = sl("kpe")
    kpe_sw = jnp.concatenate([kpe[:, half:], kpe[:, :half]], axis=1)
    a_pad = jnp.zeros((d, LANE - rank), w_in.dtype)
    w_small = jnp.concatenate([sl("cq"), sl("ckv"), kpe, kpe_sw, sl("a"), a_pad], axis=1)
    col = {"q": 0, "k": gqk, "v": 2 * gqk, "r": 2 * gqk + gvw, "ga": 2 * gqk + 2 * gvw,
           "gb": 2 * gqk + 2 * gvw + d,
           "cq": 0, "ckv": rq, "kpe": rq + rk, "a": rq + rk + 2 * MLA_ROPE}

    w3 = w_uq.reshape(rq, MLA_HEADS, MLA_NOPE + MLA_ROPE)
    pe = w3[:, :, MLA_NOPE:]
    pe_sw = jnp.concatenate([pe[:, :, half:], pe[:, :, :half]], axis=2)
    zpad = jnp.zeros((rq, MLA_HEADS, LANE - MLA_ROPE), w_uq.dtype)
    flat = lambda t: t.reshape(rq, -1).astype(BF16)
    wa_pad = jnp.concatenate([w_a2, jnp.zeros((LANE - rank, gqk), w_a2.dtype)], axis=0)
    return dict(
        col=col, dk=gqk // GLA_HEADS, dv=g_gla_out.shape[0],
        g_mix=g_mix, w_big=w_big.astype(BF16), w_small=w_small.astype(BF16),
        wa_pad=wa_pad.astype(BF16), b_a=b_a, g_gla_out=g_gla_out, g_q=g_q,
        wq_nope=flat(w3[:, :, :MLA_NOPE]),
        wq_pe=flat(jnp.concatenate([pe, zpad], axis=2)),
        wq_pe_sw=flat(jnp.concatenate([pe_sw, zpad], axis=2)),
        g_kv=g_kv, w_uk=w_uk.astype(BF16), w_uv=w_uv.astype(BF16), w_o=w_o.astype(BF16),
        g_ffn=g_ffn, w_up=w_up.astype(BF16), conv_w=conv_w, conv_b=conv_b,
        w_down=w_down.astype(BF16), final_norm=final_norm)


def kernel(x_prompt, x_sample, cache_mla_latent, cache_mla_krope, state_gla, cache_ffn_conv,
           meta_tokens, g_mix, w_in, w_a2, b_a, g_gla_out, g_q, w_uq, g_kv, w_uk, w_uv, w_o,
           g_ffn, w_up, conv_w, conv_b, w_down, final_norm):
    assert w_in.shape[0] == 1, "single trunk layer"
    bp, seq, d = x_prompt.shape
    assert bp == 1
    bs, ts, _ = x_sample.shape
    P = cache_mla_latent.shape[2]
    w = _prep_weights(g_mix[0], w_in[0], w_a2[0], b_a[0], g_gla_out[0], g_q[0], w_uq[0],
                      g_kv[0], w_uk[0], w_uv[0], w_o[0], g_ffn[0], w_up[0], conv_w[0],
                      conv_b[0], w_down[0], final_norm)
    dk, dv, dff2 = w["dk"], w["dv"], conv_w.shape[2]

    T = seq + N_META
    tile = ROW_TILE if T >= 8 * ROW_TILE else 128
    Tp = -(-T // tile) * tile
    xp = jnp.concatenate([meta_tokens.astype(F32), x_prompt[0],
                          jnp.zeros((Tp - T, d), F32)], axis=0)
    yp, lat_p, kr_p, st_p, cv_p = _stream(
        xp, w, B=1, T=T, Tp=Tp, pos=jnp.arange(Tp, dtype=jnp.int32),
        q_off=-N_META, k_off=-N_META, past_lat=None, past_kr=None,
        s0=jnp.zeros((1, GLA_HEADS, dk, dv), F32), hist=jnp.zeros((1, CONV_W - 1, dff2), F32))

    pos_s = jnp.tile(P + jnp.arange(ts, dtype=jnp.int32), bs)
    ys, lat_s, kr_s, st_s, cv_s = _stream(
        x_sample.reshape(bs * ts, d), w, B=bs, T=ts, Tp=ts, pos=pos_s, q_off=P, k_off=0,
        past_lat=cache_mla_latent[0], past_kr=cache_mla_krope[0], s0=state_gla[0],
        hist=cache_ffn_conv[0])

    rk = lat_p.shape[1]
    return (yp[:, N_META:T],
            ys,
            lat_p[:T].reshape(1, 1, T, rk),
            kr_p[:T, :MLA_ROPE].reshape(1, 1, T, MLA_ROPE),
            st_p[None],
            cv_p[None],
            lat_s.reshape(1, bs, ts, rk),
            kr_s[:, :MLA_ROPE].reshape(1, bs, ts, MLA_ROPE),
            st_s[None],
            cv_s[None])
```

```python
import functools

import jax
import jax.numpy as jnp
from jax import lax
from jax.experimental import pallas as pl
from jax.experimental.pallas import tpu as pltpu

BF16 = jnp.bfloat16
F32 = jnp.float32

CHUNK = 64
CHUNK_SHIFT = 6
N_META = 16
EPS = 1e-6
GLA_HEADS = 4
GLA_GATE_NORM = 16.0
GLA_LOG_ALPHA_MIN = -5.0
MLA_HEADS = 16
MLA_NOPE = 128
MLA_ROPE = 64
MLA_V = 128
ROPE_THETA = 10000.0
CONV_W = 3
NEG_BIG = -1e30
LOG2E = 1.4426950408889634
QK_SCALE_LOG2E = (MLA_NOPE + MLA_ROPE) ** -0.5 * LOG2E

LANE = 128
ROW_TILE = 1024
FFN_ROW_TILE = 512
VMEM_LIMIT = 56 * 1024 * 1024


def _cparams(sem, vmem=VMEM_LIMIT):
    return pltpu.CompilerParams(dimension_semantics=sem, vmem_limit_bytes=vmem)


def _rmsnorm(x, g):
    return x * lax.rsqrt(jnp.mean(x * x, axis=-1, keepdims=True) + EPS) * g


def _sigmoid(x):
    return 1.0 / (1.0 + jnp.exp(-x))


def _pick(n, cands):
    for c in cands:
        if n % c == 0:
            return c
    if n < min(cands):
        return n
    raise ValueError(f"no tile in {cands} divides {n}")


def _norm_cast_kernel(x_ref, g_ref, o_ref):
    o_ref[...] = _rmsnorm(x_ref[...], g_ref[...]).astype(o_ref.dtype)


def _norm_cast(x, g):
    m, d = x.shape
    tm = _pick(m, (384, 256, 128))
    return pl.pallas_call(
        _norm_cast_kernel,
        grid=(m // tm,),
        in_specs=[pl.BlockSpec((tm, d), lambda i: (i, 0)),
                  pl.BlockSpec((1, d), lambda i: (0, 0))],
        out_specs=pl.BlockSpec((tm, d), lambda i: (i, 0)),
        out_shape=jax.ShapeDtypeStruct((m, d), BF16),
        compiler_params=_cparams(("parallel",)),
        name="norm_cast",
    )(x, g.reshape(1, d))


def _matmul_kernel(a_ref, b_ref, o_ref):
    o_ref[...] = jnp.dot(a_ref[...], b_ref[...], preferred_element_type=F32).astype(o_ref.dtype)


def _matmul(a, b, out_dtype, tn):
    m, k = a.shape
    n = b.shape[1]
    tm = _pick(m, (ROW_TILE, 512, 384, 128))
    return pl.pallas_call(
        _matmul_kernel,
        grid=(n // tn, m // tm),
        in_specs=[pl.BlockSpec((tm, k), lambda j, i: (i, 0)),
                  pl.BlockSpec((k, tn), lambda j, i: (0, j))],
        out_specs=pl.BlockSpec((tm, tn), lambda j, i: (i, j)),
        out_shape=jax.ShapeDtypeStruct((m, n), out_dtype),
        compiler_params=_cparams(("parallel", "parallel")),
        name="in_proj",
    )(a, b)


def _split3(x):
    a = x.astype(BF16)
    r1 = x - a.astype(F32)
    b = r1.astype(BF16)
    c = (r1 - b.astype(F32)).astype(BF16)
    return a, b, c


def _gla_kernel(q_ref, k_ref, v_ref, r_ref, ga_ref, a_ref, wa_ref, ba_ref, go_ref, s0_ref,
                o_ref, sout_ref, s_scr, *, C, SB, T, scale):
    c_idx = pl.program_id(2)
    n_chunks = pl.num_programs(2)
    dv = v_ref.shape[1]

    @pl.when(c_idx == 0)
    def _():
        s_scr[...] = s0_ref[0, 0]

    z = jnp.dot(a_ref[...].astype(BF16), wa_ref[...], preferred_element_type=F32) + ba_ref[...]
    log_sig = jnp.minimum(z, 0.0) - jnp.log1p(jnp.exp(-jnp.abs(z)))
    la = jnp.maximum(log_sig * (1.0 / GLA_GATE_NORM), GLA_LOG_ALPHA_MIN)
    rows = c_idx * C + lax.broadcasted_iota(jnp.int32, (C, 1), 0)
    la = jnp.where(rows < T, la, 0.0)

    ri = lax.broadcasted_iota(jnp.int32, (C, C), 0)
    ci = lax.broadcasted_iota(jnp.int32, (C, C), 1)
    tri = jnp.where(ri >= ci, 1.0, 0.0).astype(BF16)
    ones = jnp.ones((C, LANE), BF16)
    cs = jnp.zeros_like(la)
    dsum = jnp.zeros((la.shape[1], LANE), F32)
    for piece in _split3(la):
        cs = cs + jnp.dot(tri, piece, preferred_element_type=F32)
        dsum = dsum + lax.dot_general(piece, ones, (((0,), (0,)), ((), ())),
                                      preferred_element_type=F32)
    c_last = cs[C - 1:C, :]

    q = q_ref[...].astype(F32) * scale
    k = k_ref[...].astype(F32)
    v = v_ref[...]
    s_old = s_scr[...]

    o_inter = jnp.dot((q * jnp.exp(cs)).astype(BF16), s_old.astype(BF16),
                      preferred_element_type=F32)
    k_end = (k * jnp.exp(c_last - cs)).astype(BF16)
    upd = lax.dot_general(k_end, v, (((0,), (0,)), ((), ())), preferred_element_type=F32)
    dcol = jnp.exp(dsum)
    s_new = jnp.concatenate([dcol] * (dv // LANE), axis=1) * s_old + upd
    s_scr[...] = s_new

    sr = lax.broadcasted_iota(jnp.int32, (SB, SB), 0)
    sc = lax.broadcasted_iota(jnp.int32, (SB, SB), 1)
    causal = sr >= sc
    nt = (((1,), (1,)), ((), ()))
    outs = []
    for i in range(C // SB):
        lo = i * SB
        cs_i = cs[lo:lo + SB]
        q_i = q[lo:lo + SB]
        k_i = k[lo:lo + SB]
        start = cs[lo - 1:lo] if i > 0 else jnp.zeros_like(c_last)
        mid = 0.5 * (start + cs[lo + SB - 1:lo + SB])
        qd = (q_i * jnp.exp(cs_i - mid)).astype(BF16)
        kd = (k_i * jnp.exp(mid - cs_i)).astype(BF16)
        att = lax.dot_general(qd, kd, nt, preferred_element_type=F32)
        att = jnp.where(causal, att, 0.0)
        o_i = jnp.dot(att.astype(BF16), v[lo:lo + SB], preferred_element_type=F32)
        if i > 0:
            qo = (q_i * jnp.exp(cs_i - start)).astype(BF16)
            ko = (k[:lo] * jnp.exp(start - cs[:lo])).astype(BF16)
            att_o = lax.dot_general(qo, ko, nt, preferred_element_type=F32)
            o_i = o_i + jnp.dot(att_o.astype(BF16), v[:lo], preferred_element_type=F32)
        outs.append(o_i)
    o = o_inter + (jnp.concatenate(outs, axis=0) if len(outs) > 1 else outs[0])

    on = _rmsnorm(o, go_ref[...])
    r = r_ref[...].astype(F32)
    g = ga_ref[...].astype(F32)
    o_ref[...] = (_sigmoid(g) * (on * (r * _sigmoid(r)))).astype(o_ref.dtype)

    @pl.when(c_idx == n_chunks - 1)
    def _():
        sout_ref[0, 0] = s_scr[...]


def _gla(big, small, wa_pad, b_a, g_out, s0, *, B, T, Tp, dk, dv, col):
    C = min(128, Tp)
    SB = min(32, C)
    nc = Tp // C
    H = GLA_HEADS
    rb = lambda b, h, c: b * nc + c
    kern = functools.partial(_gla_kernel, C=C, SB=SB, T=T, scale=dk ** -0.5)
    return pl.pallas_call(
        kern,
        grid=(B, H, nc),
        in_specs=[
            pl.BlockSpec((C, dk), lambda b, h, c: (rb(b, h, c), col["q"] // dk + h)),
            pl.BlockSpec((C, dk), lambda b, h, c: (rb(b, h, c), col["k"] // dk + h)),
            pl.BlockSpec((C, dv), lambda b, h, c: (rb(b, h, c), col["v"] // dv + h)),
            pl.BlockSpec((C, dv), lambda b, h, c: (rb(b, h, c), col["r"] // dv + h)),
            pl.BlockSpec((C, dv), lambda b, h, c: (rb(b, h, c), col["ga"] // dv + h)),
            pl.BlockSpec((C, LANE), lambda b, h, c: (rb(b, h, c), col["a"] // LANE)),
            pl.BlockSpec((LANE, dk), lambda b, h, c: (0, h)),
            pl.BlockSpec((1, dk), lambda b, h, c: (0, h)),
            pl.BlockSpec((1, dv), lambda b, h, c: (0, 0)),
            pl.BlockSpec((1, 1, dk, dv), lambda b, h, c: (b, h, 0, 0)),
        ],
        out_specs=[
            pl.BlockSpec((C, dv), lambda b, h, c: (rb(b, h, c), h)),
            pl.BlockSpec((1, 1, dk, dv), lambda b, h, c: (b, h, 0, 0)),
        ],
        out_shape=[jax.ShapeDtypeStruct((B * Tp, H * dv), BF16),
                   jax.ShapeDtypeStruct((B, H, dk, dv), F32)],
        scratch_shapes=[pltpu.VMEM((dk, dv), F32)],
        compiler_params=_cparams(("parallel", "parallel", "arbitrary")),
        name="gla",
    )(big, big, big, big, big, small, wa_pad, b_a.reshape(1, -1), g_out.reshape(1, -1), s0)


def _qprep_kernel(cq_ref, gq_ref, wn_ref, wp_ref, wps_ref, cos_ref, sin_ref, q_ref):
    hq = _rmsnorm(cq_ref[...], gq_ref[...]).astype(BF16)
    qn = jnp.dot(hq, wn_ref[...], preferred_element_type=F32)
    qp = jnp.dot(hq, wp_ref[...], preferred_element_type=F32)
    qs = jnp.dot(hq, wps_ref[...], preferred_element_type=F32)
    cos = cos_ref[...] * QK_SCALE_LOG2E
    sin = sin_ref[...] * QK_SCALE_LOG2E
    for h in range(MLA_HEADS):
        sl = slice(h * LANE, (h + 1) * LANE)
        q_ref[h, :, 0:LANE] = (qn[:, sl] * QK_SCALE_LOG2E).astype(BF16)
        q_ref[h, :, LANE:2 * LANE] = (qp[:, sl] * cos + qs[:, sl] * sin).astype(BF16)


def _qprep(small, g_q, wn, wp, wps, cos_t, sin_t, *, col):
    m = small.shape[0]
    rq = wn.shape[0]
    tm = _pick(m, (256, 128))
    full = lambda i: (0, 0)
    return pl.pallas_call(
        _qprep_kernel,
        grid=(m // tm,),
        in_specs=[pl.BlockSpec((tm, rq), lambda i: (i, col["cq"] // rq)),
                  pl.BlockSpec((1, rq), full),
                  pl.BlockSpec(wn.shape, full),
                  pl.BlockSpec(wp.shape, full),
                  pl.BlockSpec(wps.shape, full),
                  pl.BlockSpec((tm, LANE), lambda i: (i, 0)),
                  pl.BlockSpec((tm, LANE), lambda i: (i, 0))],
        out_specs=pl.BlockSpec((MLA_HEADS, tm, 2 * LANE), lambda i: (0, i, 0)),
        out_shape=jax.ShapeDtypeStruct((MLA_HEADS, m, 2 * LANE), BF16),
        compiler_params=_cparams(("parallel",)),
        name="mla_q",
    )(small, g_q.reshape(1, -1), wn, wp, wps, cos_t, sin_t)


def _lat_kernel(ckv_ref, kpe_ref, gkv_ref, cos_ref, sin_ref, lat_ref, kr_ref):
    lat_ref[...] = _rmsnorm(ckv_ref[...], gkv_ref[...])
    blk = kpe_ref[...]
    kr_ref[...] = blk * cos_ref[...] + pltpu.roll(blk, LANE // 2, 1) * sin_ref[...]


def _lat(small, g_kv, cos_t, sin_t, *, col):
    m = small.shape[0]
    rk = g_kv.shape[0]
    tm = _pick(m, (256, 128))
    return pl.pallas_call(
        _lat_kernel,
        grid=(m // tm,),
        in_specs=[pl.BlockSpec((tm, rk), lambda i: (i, col["ckv"] // rk)),
                  pl.BlockSpec((tm, LANE), lambda i: (i, col["kpe"] // LANE)),
                  pl.BlockSpec((1, rk), lambda i: (0, 0)),
                  pl.BlockSpec((tm, LANE), lambda i: (i, 0)),
                  pl.BlockSpec((tm, LANE), lambda i: (i, 0))],
        out_specs=[pl.BlockSpec((tm, rk), lambda i: (i, 0)),
                   pl.BlockSpec((tm, LANE), lambda i: (i, 0))],
        out_shape=[jax.ShapeDtypeStruct((m, rk), F32),
                   jax.ShapeDtypeStruct((m, LANE), F32)],
        compiler_params=_cparams(("parallel",)),
        name="mla_latent",
    )(small, small, g_kv.reshape(1, -1), cos_t, sin_t)


def _kvup_kernel(lat_ref, kr_ref, wuk_ref, wuv_ref, k_ref, v_ref, *, v_transposed):
    lat = lat_ref[...].astype(BF16)
    kn = jnp.dot(lat, wuk_ref[...], preferred_element_type=F32)
    kp = kr_ref[...].astype(BF16)
    if v_transposed:
        vv = lax.dot_general(wuv_ref[...], lat, (((1,), (1,)), ((), ())),
                             preferred_element_type=F32)
    else:
        vv = jnp.dot(lat, wuv_ref[...], preferred_element_type=F32)
    for h in range(MLA_HEADS):
        sl = slice(h * LANE, (h + 1) * LANE)
        k_ref[h, :, 0:LANE] = kn[:, sl].astype(BF16)
        k_ref[h, :, LANE:2 * LANE] = kp
        v_ref[h] = (vv[sl, :] if v_transposed else vv[:, sl]).astype(BF16)


def _kvup(lat, kr, wuk, wuv, *, v_transposed=False):
    m, rk = lat.shape
    tm = _pick(m, (512, 256, 128))
    full = lambda i: (0, 0)
    if v_transposed:
        v_spec = pl.BlockSpec((MLA_HEADS, LANE, tm), lambda i: (0, 0, i))
        v_shape = (MLA_HEADS, LANE, m)
    else:
        v_spec = pl.BlockSpec((MLA_HEADS, tm, LANE), lambda i: (0, i, 0))
        v_shape = (MLA_HEADS, m, LANE)
    return pl.pallas_call(
        functools.partial(_kvup_kernel, v_transposed=v_transposed),
        grid=(m // tm,),
        in_specs=[pl.BlockSpec((tm, rk), lambda i: (i, 0)),
                  pl.BlockSpec((tm, LANE), lambda i: (i, 0)),
                  pl.BlockSpec(wuk.shape, full),
                  pl.BlockSpec(wuv.shape, full)],
        out_specs=[pl.BlockSpec((MLA_HEADS, tm, 2 * LANE), lambda i: (0, i, 0)), v_spec],
        out_shape=[jax.ShapeDtypeStruct((MLA_HEADS, m, 2 * LANE), BF16),
                   jax.ShapeDtypeStruct(v_shape, BF16)],
        compiler_params=_cparams(("parallel",)),
        name="mla_kv",
    )(lat, kr, wuk, wuv)


def _last_kblock(qi, *, tq, tk, nk, q_off, k_off):
    top_chunk = ((qi + 1) * tq - 1 + q_off) // CHUNK
    last_key = (top_chunk + 1) * CHUNK - 1 - k_off
    return jnp.minimum(last_key // tk, nk - 1)


def _attn_kernel(q_ref, k_ref, v_ref, o_ref, m_scr, l_scr, acc_scr, *, hps, tq, tk, nk,
                 q_off, k_off):
    qi = pl.program_id(2)
    ki = pl.program_id(3)

    @pl.when(ki == 0)
    def _():
        m_scr[...] = jnp.full(m_scr.shape, NEG_BIG, F32)
        l_scr[...] = jnp.zeros(l_scr.shape, F32)
        acc_scr[...] = jnp.zeros(acc_scr.shape, F32)

    @pl.when(ki <= _last_kblock(qi, tq=tq, tk=tk, nk=nk, q_off=q_off, k_off=k_off))
    def _():
        q_chunk = (qi * tq + q_off + lax.broadcasted_iota(jnp.int32, (tq, 1), 0)) >> CHUNK_SHIFT
        k_chunk = (ki * tk + k_off + lax.broadcasted_iota(jnp.int32, (1, tk), 1)) >> CHUNK_SHIFT
        visible = q_chunk >= k_chunk

        def head(h, carry):
            s = lax.dot_general(q_ref[h], k_ref[h], (((1,), (1,)), ((), ())),
                                preferred_element_type=F32)
            s = jnp.where(visible, s, NEG_BIG)
            m_prev = m_scr[h]
            m_new = jnp.maximum(m_prev, jnp.max(s, axis=-1, keepdims=True))
            p = jnp.exp2(s - m_new)
            alpha = jnp.exp2(m_prev - m_new)
            l_scr[h] = alpha * l_scr[h] + jnp.sum(p, axis=-1, keepdims=True)
            acc_scr[h] = alpha * acc_scr[h] + jnp.dot(p.astype(BF16), v_ref[h],
                                                      preferred_element_type=F32)
            m_scr[h] = m_new
            return carry

        lax.fori_loop(0, hps, head, 0)

    @pl.when(ki == nk - 1)
    def _():
        for h in range(hps):
            o_ref[:, h * LANE:(h + 1) * LANE] = (acc_scr[h] / l_scr[h]).astype(o_ref.dtype)


def _attention(q, k, v, *, B, Tq, Tk, tq, tk, hps, q_off, k_off):
    nq = Tq // tq
    nk = Tk // tk
    hg = MLA_HEADS // hps
    dqk = q.shape[2]
    dvh = v.shape[2]
    last = functools.partial(_last_kblock, tq=tq, tk=tk, nk=nk, q_off=q_off, k_off=k_off)
    kern = functools.partial(_attn_kernel, hps=hps, tq=tq, tk=tk, nk=nk, q_off=q_off,
                             k_off=k_off)
    kv_row = lambda b, g, i, j: b * nk + jnp.minimum(j, last(i))
    return pl.pallas_call(
        kern,
        grid=(B, hg, nq, nk),
        in_specs=[pl.BlockSpec((hps, tq, dqk), lambda b, g, i, j: (g, b * nq + i, 0)),
                  pl.BlockSpec((hps, tk, dqk), lambda b, g, i, j: (g, kv_row(b, g, i, j), 0)),
                  pl.BlockSpec((hps, tk, dvh), lambda b, g, i, j: (g, kv_row(b, g, i, j), 0))],
        out_specs=pl.BlockSpec((tq, hps * dvh), lambda b, g, i, j: (b * nq + i, g)),
        out_shape=jax.ShapeDtypeStruct((B * Tq, MLA_HEADS * dvh), BF16),
        scratch_shapes=[pltpu.VMEM((hps, tq, 1), F32),
                        pltpu.VMEM((hps, tq, 1), F32),
                        pltpu.VMEM((hps, tq, dvh), F32)],
        compiler_params=_cparams(("parallel", "parallel", "parallel", "arbitrary")),
        name="mla_attn",
    )(q, k, v)


def _attn_t_kernel(q_ref, k_ref, vt_ref, kp_ref, vtp_ref, o_ref, m_scr, l_scr, acc_scr,
                   *, hps, t, nk):
    qi = pl.program_id(1)
    ki = pl.program_id(2)
    nt = (((1,), (1,)), ((), ()))

    def scores(k_blk, h):
        return lax.dot_general(k_blk, q_ref[h], nt, preferred_element_type=F32)

    @pl.when(ki == 0)
    def _():
        for h in range(hps):
            s = scores(kp_ref[h], h)
            m = jnp.max(s, axis=0, keepdims=True)
            p = jnp.exp2(s - m)
            m_scr[h] = m
            l_scr[h] = jnp.sum(p, axis=0, keepdims=True)
            acc_scr[h] = jnp.dot(vtp_ref[h], p.astype(BF16), preferred_element_type=F32)

    def step(masked):
        if masked:
            k_chunk = (ki * t + lax.broadcasted_iota(jnp.int32, (t, 1), 0)) >> CHUNK_SHIFT
            q_chunk = (qi * t + lax.broadcasted_iota(jnp.int32, (1, t), 1)) >> CHUNK_SHIFT
            bias = jnp.where(q_chunk >= k_chunk, 0.0, NEG_BIG)
        for h in range(hps):
            s = scores(k_ref[h], h)
            if masked:
                s = s + bias
            m_prev = m_scr[h]
            m_new = jnp.maximum(m_prev, jnp.max(s, axis=0, keepdims=True))
            p = jnp.exp2(s - m_new)
            alpha = jnp.exp2(m_prev - m_new)
            l_scr[h] = alpha * l_scr[h] + jnp.sum(p, axis=0, keepdims=True)
            acc_scr[h] = alpha * acc_scr[h] + jnp.dot(vt_ref[h], p.astype(BF16),
                                                      preferred_element_type=F32)
            m_scr[h] = m_new

    @pl.when(ki < qi)
    def _():
        step(False)

    @pl.when(ki == qi)
    def _():
        step(True)

    @pl.when(ki == nk - 1)
    def _():
        for h in range(hps):
            o_t = acc_scr[h] / l_scr[h]
            o_ref[:, h * LANE:(h + 1) * LANE] = o_t.T.astype(o_ref.dtype)


def _attention_t(q, k, vt, k_pre, vt_pre, *, T, t, hps):
    n = T // t
    hg = MLA_HEADS // hps
    dqk = q.shape[2]
    npre = k_pre.shape[1]
    kern = functools.partial(_attn_t_kernel, hps=hps, t=t, nk=n)
    return pl.pallas_call(
        kern,
        grid=(hg, n, n),
        in_specs=[pl.BlockSpec((hps, t, dqk), lambda g, i, j: (g, i, 0)),
                  pl.BlockSpec((hps, t, dqk), lambda g, i, j: (g, jnp.minimum(j, i), 0)),
                  pl.BlockSpec((hps, LANE, t), lambda g, i, j: (g, 0, jnp.minimum(j, i))),
                  pl.BlockSpec((hps, npre, dqk), lambda g, i, j: (g, 0, 0)),
                  pl.BlockSpec((hps, LANE, npre), lambda g, i, j: (g, 0, 0))],
        out_specs=pl.BlockSpec((t, hps * LANE), lambda g, i, j: (i, g)),
        out_shape=jax.ShapeDtypeStruct((T, MLA_HEADS * LANE), BF16),
        scratch_shapes=[pltpu.VMEM((hps, 1, t), F32),
                        pltpu.VMEM((hps, 1, t), F32),
                        pltpu.VMEM((hps, LANE, t), F32)],
        compiler_params=_cparams(("parallel", "parallel", "arbitrary")),
        name="mla_attn_t",
    )(q, k, vt, k_pre, vt_pre)


def _merge_kernel(a_ref, gb_ref, om_ref, x_ref, wo_ref, gf_ref, x1_ref, h2_ref):
    merged = a_ref[...].astype(F32) + _sigmoid(gb_ref[...].astype(F32)) * om_ref[...].astype(F32)
    x1 = x_ref[...] + jnp.dot(merged.astype(BF16), wo_ref[...], preferred_element_type=F32)
    x1_ref[...] = x1
    h2_ref[...] = _rmsnorm(x1, gf_ref[...]).astype(BF16)


def _merge(branch_a, big, o_m, x, wo, g_ffn, *, col):
    m, d = x.shape
    tm = _pick(m, (384, 256, 128))
    row = lambda i: (i, 0)
    return pl.pallas_call(
        _merge_kernel,
        grid=(m // tm,),
        in_specs=[pl.BlockSpec((tm, d), row),
                  pl.BlockSpec((tm, d), lambda i: (i, col["gb"] // d)),
                  pl.BlockSpec((tm, d), row),
                  pl.BlockSpec((tm, d), row),
                  pl.BlockSpec(wo.shape, lambda i: (0, 0)),
                  pl.BlockSpec((1, d), lambda i: (0, 0))],
        out_specs=[pl.BlockSpec((tm, d), row), pl.BlockSpec((tm, d), row)],
        out_shape=[jax.ShapeDtypeStruct((m, d), F32), jax.ShapeDtypeStruct((m, d), BF16)],
        compiler_params=_cparams(("parallel",)),
        name="merge_out_proj",
    )(branch_a, big, o_m, x, wo, g_ffn.reshape(1, -1))


HALO = 8


def _ffn_kernel(h_ref, x1_ref, wa_ref, wb_ref, wd_ref, cwa_ref, cwb_ref, cba_ref, cbb_ref,
                ha_ref, hb_ref, gf_ref, y_ref, ca_ref, cb_ref, ext_scr, carry_scr,
                *, bb, r, tf, loc, carried):
    rt = pl.program_id(1)
    f = pl.program_id(2)
    nf = pl.num_programs(2)
    d = h_ref.shape[2]
    h = h_ref[...].reshape(bb * r, d)

    if carried:
        @pl.when(rt == 0)
        def _():
            carry_scr[f, 0] = ha_ref[...]
            carry_scr[f, 1] = hb_ref[...]

    conv = []
    for half, (w_ref, cw_ref, cbias_ref, hist_ref, cout_ref) in enumerate(
            ((wa_ref, cwa_ref, cba_ref, ha_ref, ca_ref),
             (wb_ref, cwb_ref, cbb_ref, hb_ref, cb_ref))):
        u = jnp.dot(h, w_ref[...], preferred_element_type=F32).reshape(bb, r, tf)
        ext_scr[half, :, HALO:HALO + r, :] = u
        ext_scr[half, :, HALO - 2:HALO, :] = carry_scr[f, half] if carried else hist_ref[...]
        u1 = ext_scr[half, :, HALO - 1:HALO - 1 + r, :]
        u2 = ext_scr[half, :, HALO - 2:HALO - 2 + r, :]
        cw = cw_ref[...]
        conv.append(cbias_ref[...] + cw[0:1] * u2 + cw[1:2] * u1 + cw[2:3] * u)
        if carried:
            carry_scr[f, half] = ext_scr[half, :, HALO + r - 2:HALO + r, :]
        cout_ref[0] = ext_scr[half, :, HALO + loc:HALO + loc + 2, :]

    act = (conv[0] * _sigmoid(conv[0])) * conv[1]
    part = jnp.dot(act.reshape(bb * r, tf).astype(BF16), wd_ref[...],
                   preferred_element_type=F32).reshape(bb, r, d)

    @pl.when(f == 0)
    def _():
        y_ref[...] = part

    @pl.when(f > 0)
    def _():
        y_ref[...] += part

    @pl.when(f == nf - 1)
    def _():
        y_ref[...] = _rmsnorm(x1_ref[...] + y_ref[...], gf_ref[...])


def _ffn(h2, x1, w_up, w_down, conv_w, conv_b, hist, g_final, *, B, T, Tp):
    d = h2.shape[1]
    dff = w_down.shape[0]
    tf = _pick(dff, (256, 128))
    nf = dff // tf
    if Tp <= 128:
        bb, r = B, Tp
    else:
        bb, r = 1, _pick(Tp, (FFN_ROW_TILE, 128))
    nrt = Tp // r
    carried = nrt > 1
    loc = (T - 2) - (nrt - 1) * r
    assert 0 <= loc <= r - 2, "final two valid rows must sit in the last row tile"
    kern = functools.partial(_ffn_kernel, bb=bb, r=r, tf=tf, loc=loc, carried=carried)
    rows = lambda s, t, f: (s, t, 0)
    carry_shape = (nf, 2, bb, 2, tf) if carried else (1, 1, 1, 2, LANE)
    y, ca, cb = pl.pallas_call(
        kern,
        grid=(B // bb, nrt, nf),
        in_specs=[pl.BlockSpec((bb, r, d), rows),
                  pl.BlockSpec((bb, r, d), rows),
                  pl.BlockSpec((d, tf), lambda s, t, f: (0, f)),
                  pl.BlockSpec((d, tf), lambda s, t, f: (0, nf + f)),
                  pl.BlockSpec((tf, d), lambda s, t, f: (f, 0)),
                  pl.BlockSpec((CONV_W, tf), lambda s, t, f: (0, f)),
                  pl.BlockSpec((CONV_W, tf), lambda s, t, f: (0, nf + f)),
                  pl.BlockSpec((1, tf), lambda s, t, f: (0, f)),
                  pl.BlockSpec((1, tf), lambda s, t, f: (0, nf + f)),
                  pl.BlockSpec((bb, 2, tf), lambda s, t, f: (s, 0, f)),
                  pl.BlockSpec((bb, 2, tf), lambda s, t, f: (s, 0, nf + f)),
                  pl.BlockSpec((1, d), lambda s, t, f: (0, 0))],
        out_specs=[pl.BlockSpec((bb, r, d), rows),
                   pl.BlockSpec((1, bb, 2, tf), lambda s, t, f: (t, s, 0, f)),
                   pl.BlockSpec((1, bb, 2, tf), lambda s, t, f: (t, s, 0, f))],
        out_shape=[jax.ShapeDtypeStruct((B, Tp, d), F32),
                   jax.ShapeDtypeStruct((nrt, B, 2, dff), F32),
                   jax.ShapeDtypeStruct((nrt, B, 2, dff), F32)],
        scratch_shapes=[pltpu.VMEM((2, bb, HALO + r, tf), F32),
                        pltpu.VMEM(carry_shape, F32)],
        compiler_params=_cparams(("parallel", "arbitrary", "arbitrary")),
        name="conv_ffn",
    )(h2.reshape(B, Tp, d), x1.reshape(B, Tp, d), w_up, w_up, w_down, conv_w, conv_w,
      conv_b.reshape(1, -1), conv_b.reshape(1, -1), hist, hist, g_final.reshape(1, -1))
    return y, jnp.concatenate([ca[nrt - 1], cb[nrt - 1]], axis=-1)


def _rope_tables(pos):
    half = MLA_ROPE // 2
    inv = ROPE_THETA ** (-jnp.arange(0, MLA_ROPE, 2, dtype=F32) / MLA_ROPE)
    ang = pos.astype(F32)[:, None] * inv[None, :]
    cos, sin = jnp.cos(ang), jnp.sin(ang)
    zero = jnp.zeros((pos.shape[0], LANE - 2 * half), F32)
    return (jnp.concatenate([cos, cos, zero], axis=1),
            jnp.concatenate([-sin, sin, zero], axis=1))


def _stream(x, w, *, B, T, pos, q_off, s0, hist, past_lat=None, past_kr=None, prefix=None,
            emit_prefix=False):
    col = w["col"]
    dk, dv = w["dk"], w["dv"]
    h = _norm_cast(x, w["g_mix"])
    big = _matmul(h, w["w_big"], BF16, tn=1024)
    small = _matmul(h, w["w_small"], F32, tn=w["w_small"].shape[1])

    branch_a, state = _gla(big, small, w["wa_pad"], w["b_a"], w["g_gla_out"], s0,
                           B=B, T=T, Tp=T, dk=dk, dv=dv, col=col)

    cos_t, sin_t = _rope_tables(pos)
    q = _qprep(small, w["g_q"], w["wq_nope"], w["wq_pe"], w["wq_pe_sw"], cos_t, sin_t, col=col)
    lat, kr = _lat(small, w["g_kv"], cos_t, sin_t, col=col)
    own_prefix = None
    if prefix is not None:
        assert B == 1 and past_lat is None
        k, vt = _kvup(lat, kr, w["w_uk"], w["w_uv_t"], v_transposed=True)
        o_m = _attention_t(q, k, vt, prefix[0], prefix[1], T=T, t=_pick(T, (1024, 128)),
                           hps=MLA_HEADS // 4)
    else:
        if past_lat is None:
            all_lat, all_kr, Tk = lat, kr, T
        else:
            P = past_lat.shape[1]
            rk = lat.shape[1]
            all_lat = jnp.concatenate([past_lat, lat.reshape(B, T, rk)], axis=1).reshape(-1, rk)
            past_kr = jnp.pad(past_kr, ((0, 0), (0, 0), (0, LANE - past_kr.shape[2])))
            all_kr = jnp.concatenate([past_kr, kr.reshape(B, T, LANE)], axis=1).reshape(-1, LANE)
            Tk = P + T
        k, v = _kvup(all_lat, all_kr, w["w_uk"], w["w_uv"])
        if emit_prefix:
            own_prefix = _kvup(lat, kr, w["w_uk"], w["w_uv_t"], v_transposed=True)
        o_m = _attention(q, k, v, B=B, Tq=T, Tk=Tk, tq=T, tk=Tk, hps=MLA_HEADS,
                         q_off=q_off, k_off=0)

    x1, h2 = _merge(branch_a, big, o_m, x, w["w_o"], w["g_ffn"], col=col)
    y, conv = _ffn(h2, x1, w["w_up"], w["w_down"], w["conv_w"], w["conv_b"], hist,
                   w["final_norm"], B=B, T=T, Tp=T)
    return y, lat, kr, state, conv, own_prefix


def _prep_weights(g_mix, w_in, w_a2, b_a, g_gla_out, g_q, w_uq, g_kv, w_uk, w_uv, w_o,
                  g_ffn, w_up, conv_w, conv_b, w_down, final_norm):
    d = w_in.shape[0]
    rank, gqk = w_a2.shape
    gvw = GLA_HEADS * g_gla_out.shape[0]
    rq, rk = g_q.shape[0], g_kv.shape[0]
    half = MLA_ROPE // 2
    o, offs = 0, {}
    for name, width in (("q", gqk), ("k", gqk), ("v", gvw), ("r", gvw), ("a", rank),
                        ("cq", rq), ("ckv", rk), ("kpe", MLA_ROPE), ("ga", d), ("gb", d)):
        offs[name] = (o, o + width)
        o += width
    assert o == w_in.shape[1]
    sl = lambda n: w_in[:, offs[n][0]:offs[n][1]]
    w_big = jnp.concatenate([sl("q"), sl("k"), sl("v"), sl("r"), sl("ga"), sl("gb")], axis=1)
    kpe = sl("kpe")
    kpe_sw = jnp.concatenate([kpe[:, half:], kpe[:, :half]], axis=1)
    a_pad = jnp.zeros((d, LANE - rank), w_in.dtype)
    w_small = jnp.concatenate([sl("cq"), sl("ckv"), kpe, kpe_sw, sl("a"), a_pad], axis=1)
    col = {"q": 0, "k": gqk, "v": 2 * gqk, "r": 2 * gqk + gvw, "ga": 2 * gqk + 2 * gvw,
           "gb": 2 * gqk + 2 * gvw + d,
           "cq": 0, "ckv": rq, "kpe": rq + rk, "a": rq + rk + 2 * MLA_ROPE}

    w3 = w_uq.reshape(rq, MLA_HEADS, MLA_NOPE + MLA_ROPE)
    pe = w3[:, :, MLA_NOPE:]
    pe_sw = jnp.concatenate([pe[:, :, half:], pe[:, :, :half]], axis=2)
    zpad = jnp.zeros((rq, MLA_HEADS, LANE - MLA_ROPE), w_uq.dtype)
    flat = lambda t: t.reshape(rq, -1).astype(BF16)
    wa_pad = jnp.concatenate([w_a2, jnp.zeros((LANE - rank, gqk), w_a2.dtype)], axis=0)
    return dict(
        col=col, dk=gqk // GLA_HEADS, dv=g_gla_out.shape[0],
        g_mix=g_mix, w_big=w_big.astype(BF16), w_small=w_small.astype(BF16),
        wa_pad=wa_pad.astype(BF16), b_a=b_a, g_gla_out=g_gla_out, g_q=g_q,
        wq_nope=flat(w3[:, :, :MLA_NOPE]),
        wq_pe=flat(jnp.concatenate([pe, zpad], axis=2)),
        wq_pe_sw=flat(jnp.concatenate([pe_sw, zpad], axis=2)),
        g_kv=g_kv, w_uk=w_uk.astype(BF16), w_uv=w_uv.astype(BF16),
        w_uv_t=w_uv.T.astype(BF16), w_o=w_o.astype(BF16),
        g_ffn=g_ffn, w_up=w_up.astype(BF16), conv_w=conv_w, conv_b=conv_b,
        w_down=w_down.astype(BF16), final_norm=final_norm)


def kernel(x_prompt, x_sample, cache_mla_latent, cache_mla_krope, state_gla, cache_ffn_conv,
           meta_tokens, g_mix, w_in, w_a2, b_a, g_gla_out, g_q, w_uq, g_kv, w_uk, w_uv, w_o,
           g_ffn, w_up, conv_w, conv_b, w_down, final_norm):
    assert w_in.shape[0] == 1, "single trunk layer"
    bp, seq, d = x_prompt.shape
    assert bp == 1
    bs, ts, _ = x_sample.shape
    P = cache_mla_latent.shape[2]
    w = _prep_weights(g_mix[0], w_in[0], w_a2[0], b_a[0], g_gla_out[0], g_q[0], w_uq[0],
                      g_kv[0], w_uk[0], w_uv[0], w_o[0], g_ffn[0], w_up[0], conv_w[0],
                      conv_b[0], w_down[0], final_norm)
    dk, dv, dff2 = w["dk"], w["dv"], conv_w.shape[2]

    n_meta = meta_tokens.shape[0]
    assert n_meta == N_META and seq % CHUNK == 0
    _, lat_m, kr_m, st_m, cv_m, prefix = _stream(
        meta_tokens.astype(F32), w, B=1, T=n_meta, pos=jnp.arange(n_meta, dtype=jnp.int32),
        q_off=0, s0=jnp.zeros((1, GLA_HEADS, dk, dv), F32),
        hist=jnp.zeros((1, CONV_W - 1, dff2), F32), emit_prefix=True)
    yp, lat_p, kr_p, st_p, cv_p, _ = _stream(
        x_prompt[0], w, B=1, T=seq, pos=n_meta + jnp.arange(seq, dtype=jnp.int32),
        q_off=0, s0=st_m, hist=cv_m, prefix=prefix)

    pos_s = jnp.tile(P + jnp.arange(ts, dtype=jnp.int32), bs)
    ys, lat_s, kr_s, st_s, cv_s, _ = _stream(
        x_sample.reshape(bs * ts, d), w, B=bs, T=ts, pos=pos_s, q_off=P,
        past_lat=cache_mla_latent[0], past_kr=cache_mla_krope[0], s0=state_gla[0],
        hist=cache_ffn_conv[0])

    rk = lat_p.shape[1]
    T = n_meta + seq
    return (yp,
            ys,
            jnp.concatenate([lat_m, lat_p], axis=0).reshape(1, 1, T, rk),
            jnp.concatenate([kr_m, kr_p], axis=0)[:, :MLA_ROPE].reshape(1, 1, T, MLA_ROPE),
            st_p[None],
            cv_p[None],
            lat_s.reshape(1, bs, ts, rk),
            kr_s[:, :MLA_ROPE].reshape(1, bs, ts, MLA_ROPE),
            st_s[None],
            cv_s[None])
```

```python
import functools

import jax
import jax.numpy as jnp
from jax import lax
from jax.experimental import pallas as pl
from jax.experimental.pallas import tpu as pltpu

BF16 = jnp.bfloat16
F32 = jnp.float32

CHUNK = 64
CHUNK_SHIFT = 6
N_META = 16
EPS = 1e-6
GLA_HEADS = 4
GLA_GATE_NORM = 16.0
GLA_LOG_ALPHA_MIN = -5.0
MLA_HEADS = 16
MLA_NOPE = 128
MLA_ROPE = 64
MLA_V = 128
ROPE_THETA = 10000.0
CONV_W = 3
NEG_BIG = -1e30
LOG2E = 1.4426950408889634
QK_SCALE_LOG2E = (MLA_NOPE + MLA_ROPE) ** -0.5 * LOG2E

LANE = 128
ROW_TILE = 1024
FFN_ROW_TILE = 512
VMEM_LIMIT = 56 * 1024 * 1024


def _cparams(sem, vmem=VMEM_LIMIT):
    return pltpu.CompilerParams(dimension_semantics=sem, vmem_limit_bytes=vmem)


def _rmsnorm(x, g):
    return x * lax.rsqrt(jnp.mean(x * x, axis=-1, keepdims=True) + EPS) * g


def _sigmoid(x):
    return 1.0 / (1.0 + jnp.exp(-x))


def _pick(n, cands):
    for c in cands:
        if n % c == 0:
            return c
    if n < min(cands):
        return n
    raise ValueError(f"no tile in {cands} divides {n}")


def _norm_cast_kernel(x_ref, g_ref, o_ref):
    o_ref[...] = _rmsnorm(x_ref[...], g_ref[...]).astype(o_ref.dtype)


def _norm_cast(x, g):
    m, d = x.shape
    tm = _pick(m, (384, 256, 128))
    return pl.pallas_call(
        _norm_cast_kernel,
        grid=(m // tm,),
        in_specs=[pl.BlockSpec((tm, d), lambda i: (i, 0)),
                  pl.BlockSpec((1, d), lambda i: (0, 0))],
        out_specs=pl.BlockSpec((tm, d), lambda i: (i, 0)),
        out_shape=jax.ShapeDtypeStruct((m, d), BF16),
        compiler_params=_cparams(("parallel",)),
        name="norm_cast",
    )(x, g.reshape(1, d))


def _matmul_kernel(a_ref, b_ref, o_ref):
    o_ref[...] = jnp.dot(a_ref[...], b_ref[...], preferred_element_type=F32).astype(o_ref.dtype)


def _matmul(a, b, out_dtype, tn):
    m, k = a.shape
    n = b.shape[1]
    tm = _pick(m, (ROW_TILE, 512, 384, 128))
    return pl.pallas_call(
        _matmul_kernel,
        grid=(n // tn, m // tm),
        in_specs=[pl.BlockSpec((tm, k), lambda j, i: (i, 0)),
                  pl.BlockSpec((k, tn), lambda j, i: (0, j))],
        out_specs=pl.BlockSpec((tm, tn), lambda j, i: (i, j)),
        out_shape=jax.ShapeDtypeStruct((m, n), out_dtype),
        compiler_params=_cparams(("parallel", "parallel")),
        name="in_proj",
    )(a, b)


def _split3(x):
    a = x.astype(BF16)
    r1 = x - a.astype(F32)
    b = r1.astype(BF16)
    c = (r1 - b.astype(F32)).astype(BF16)
    return a, b, c


def _gla_kernel(q_ref, k_ref, v_ref, r_ref, ga_ref, a_ref, wa_ref, ba_ref, go_ref, s0_ref,
                o_ref, sout_ref, s_scr, *, C, SB, T, H, dk, dv):
    c_idx = pl.program_id(1)
    n_chunks = pl.num_programs(1)

    @pl.when(c_idx == 0)
    def _():
        s_scr[...] = s0_ref[0]

    z = jnp.dot(a_ref[...].astype(BF16), wa_ref[...], preferred_element_type=F32) + ba_ref[...]
    log_sig = jnp.minimum(z, 0.0) - jnp.log1p(jnp.exp(-jnp.abs(z)))
    la = jnp.maximum(log_sig * (1.0 / GLA_GATE_NORM), GLA_LOG_ALPHA_MIN)
    if T % C:
        rows = c_idx * C + lax.broadcasted_iota(jnp.int32, (C, 1), 0)
        la = jnp.where(rows < T, la, 0.0)

    ri = lax.broadcasted_iota(jnp.int32, (C, C), 0)
    ci = lax.broadcasted_iota(jnp.int32, (C, C), 1)
    tri = jnp.where(ri >= ci, 1.0, 0.0).astype(BF16)
    ones = jnp.ones((C, LANE), BF16)
    cs_all = jnp.zeros_like(la)
    dsum_all = jnp.zeros((la.shape[1], LANE), F32)
    for piece in _split3(la):
        cs_all = cs_all + jnp.dot(tri, piece, preferred_element_type=F32)
        dsum_all = dsum_all + lax.dot_general(piece, ones, (((0,), (0,)), ((), ())),
                                              preferred_element_type=F32)

    sr = lax.broadcasted_iota(jnp.int32, (SB, SB), 0)
    sc = lax.broadcasted_iota(jnp.int32, (SB, SB), 1)
    causal = sr >= sc
    nt = (((1,), (1,)), ((), ()))
    scale = dk ** -0.5

    for h in range(H):
        ksl = slice(h * dk, (h + 1) * dk)
        vsl = slice(h * dv, (h + 1) * dv)
        cs = cs_all[:, ksl]
        c_last = cs[C - 1:C, :]
        q = q_ref[:, ksl].astype(F32) * scale
        k = k_ref[:, ksl].astype(F32)
        v = v_ref[:, vsl]
        s_old = s_scr[h]

        o_inter = jnp.dot((q * jnp.exp(cs)).astype(BF16), s_old.astype(BF16),
                          preferred_element_type=F32)
        k_end = (k * jnp.exp(c_last - cs)).astype(BF16)
        upd = lax.dot_general(k_end, v, (((0,), (0,)), ((), ())), preferred_element_type=F32)
        dcol = jnp.exp(dsum_all[ksl, :])
        s_scr[h] = jnp.concatenate([dcol] * (dv // LANE), axis=1) * s_old + upd

        outs = []
        for i in range(C // SB):
            lo = i * SB
            cs_i = cs[lo:lo + SB]
            q_i = q[lo:lo + SB]
            k_i = k[lo:lo + SB]
            start = cs[lo - 1:lo] if i > 0 else jnp.zeros_like(c_last)
            mid = 0.5 * (start + cs[lo + SB - 1:lo + SB])
            qd = (q_i * jnp.exp(cs_i - mid)).astype(BF16)
            kd = (k_i * jnp.exp(mid - cs_i)).astype(BF16)
            att = lax.dot_general(qd, kd, nt, preferred_element_type=F32)
            att = jnp.where(causal, att, 0.0)
            o_i = jnp.dot(att.astype(BF16), v[lo:lo + SB], preferred_element_type=F32)
            if i > 0:
                qo = (q_i * jnp.exp(cs_i - start)).astype(BF16)
                ko = (k[:lo] * jnp.exp(start - cs[:lo])).astype(BF16)
                att_o = lax.dot_general(qo, ko, nt, preferred_element_type=F32)
                o_i = o_i + jnp.dot(att_o.astype(BF16), v[:lo], preferred_element_type=F32)
            outs.append(o_i)
        o = o_inter + (jnp.concatenate(outs, axis=0) if len(outs) > 1 else outs[0])

        on = _rmsnorm(o, go_ref[...])
        r = r_ref[:, vsl].astype(F32)
        g = ga_ref[:, vsl].astype(F32)
        o_ref[:, vsl] = (_sigmoid(g) * (on * (r * _sigmoid(r)))).astype(o_ref.dtype)

    @pl.when(c_idx == n_chunks - 1)
    def _():
        sout_ref[0] = s_scr[...]


def _gla(big, small, wa_pad, b_a, g_out, s0, *, B, T, Tp, dk, dv, col):
    C = min(128, Tp)
    SB = min(32, C)
    nc = Tp // C
    H = GLA_HEADS
    qk, vw = H * dk, H * dv
    rb = lambda b, c: b * nc + c
    kern = functools.partial(_gla_kernel, C=C, SB=SB, T=T, H=H, dk=dk, dv=dv)
    return pl.pallas_call(
        kern,
        grid=(B, nc),
        in_specs=[
            pl.BlockSpec((C, qk), lambda b, c: (rb(b, c), col["q"] // qk)),
            pl.BlockSpec((C, qk), lambda b, c: (rb(b, c), col["k"] // qk)),
            pl.BlockSpec((C, vw), lambda b, c: (rb(b, c), col["v"] // vw)),
            pl.BlockSpec((C, vw), lambda b, c: (rb(b, c), col["r"] // vw)),
            pl.BlockSpec((C, vw), lambda b, c: (rb(b, c), col["ga"] // vw)),
            pl.BlockSpec((C, LANE), lambda b, c: (rb(b, c), col["a"] // LANE)),
            pl.BlockSpec((LANE, qk), lambda b, c: (0, 0)),
            pl.BlockSpec((1, qk), lambda b, c: (0, 0)),
            pl.BlockSpec((1, dv), lambda b, c: (0, 0)),
            pl.BlockSpec((1, H, dk, dv), lambda b, c: (b, 0, 0, 0)),
        ],
        out_specs=[
            pl.BlockSpec((C, vw), lambda b, c: (rb(b, c), 0)),
            pl.BlockSpec((1, H, dk, dv), lambda b, c: (b, 0, 0, 0)),
        ],
        out_shape=[jax.ShapeDtypeStruct((B * Tp, vw), BF16),
                   jax.ShapeDtypeStruct((B, H, dk, dv), F32)],
        scratch_shapes=[pltpu.VMEM((H, dk, dv), F32)],
        compiler_params=_cparams(("parallel", "arbitrary")),
        name="gla",
    )(big, big, big, big, big, small, wa_pad, b_a.reshape(1, -1), g_out.reshape(1, -1), s0)


def _qprep_kernel(cq_ref, gq_ref, wn_ref, wp_ref, wps_ref, cos_ref, sin_ref, q_ref):
    hq = _rmsnorm(cq_ref[...], gq_ref[...]).astype(BF16)
    qn = jnp.dot(hq, wn_ref[...], preferred_element_type=F32)
    qp = jnp.dot(hq, wp_ref[...], preferred_element_type=F32)
    qs = jnp.dot(hq, wps_ref[...], preferred_element_type=F32)
    cos = cos_ref[...] * QK_SCALE_LOG2E
    sin = sin_ref[...] * QK_SCALE_LOG2E
    for h in range(MLA_HEADS):
        sl = slice(h * LANE, (h + 1) * LANE)
        q_ref[h, :, 0:LANE] = (qn[:, sl] * QK_SCALE_LOG2E).astype(BF16)
        q_ref[h, :, LANE:2 * LANE] = (qp[:, sl] * cos + qs[:, sl] * sin).astype(BF16)


def _qprep(small, g_q, wn, wp, wps, cos_t, sin_t, *, col):
    m = small.shape[0]
    rq = wn.shape[0]
    tm = _pick(m, (256, 128))
    full = lambda i: (0, 0)
    return pl.pallas_call(
        _qprep_kernel,
        grid=(m // tm,),
        in_specs=[pl.BlockSpec((tm, rq), lambda i: (i, col["cq"] // rq)),
                  pl.BlockSpec((1, rq), full),
                  pl.BlockSpec(wn.shape, full),
                  pl.BlockSpec(wp.shape, full),
                  pl.BlockSpec(wps.shape, full),
                  pl.BlockSpec((tm, LANE), lambda i: (i, 0)),
                  pl.BlockSpec((tm, LANE), lambda i: (i, 0))],
        out_specs=pl.BlockSpec((MLA_HEADS, tm, 2 * LANE), lambda i: (0, i, 0)),
        out_shape=jax.ShapeDtypeStruct((MLA_HEADS, m, 2 * LANE), BF16),
        compiler_params=_cparams(("parallel",)),
        name="mla_q",
    )(small, g_q.reshape(1, -1), wn, wp, wps, cos_t, sin_t)


def _lat_kernel(ckv_ref, kpe_ref, gkv_ref, cos_ref, sin_ref, lat_ref, kr_ref):
    lat_ref[...] = _rmsnorm(ckv_ref[...], gkv_ref[...])
    blk = kpe_ref[...]
    kr_ref[...] = blk * cos_ref[...] + pltpu.roll(blk, LANE // 2, 1) * sin_ref[...]


def _lat(small, g_kv, cos_t, sin_t, *, col):
    m = small.shape[0]
    rk = g_kv.shape[0]
    tm = _pick(m, (256, 128))
    return pl.pallas_call(
        _lat_kernel,
        grid=(m // tm,),
        in_specs=[pl.BlockSpec((tm, rk), lambda i: (i, col["ckv"] // rk)),
                  pl.BlockSpec((tm, LANE), lambda i: (i, col["kpe"] // LANE)),
                  pl.BlockSpec((1, rk), lambda i: (0, 0)),
                  pl.BlockSpec((tm, LANE), lambda i: (i, 0)),
                  pl.BlockSpec((tm, LANE), lambda i: (i, 0))],
        out_specs=[pl.BlockSpec((tm, rk), lambda i: (i, 0)),
                   pl.BlockSpec((tm, LANE), lambda i: (i, 0))],
        out_shape=[jax.ShapeDtypeStruct((m, rk), F32),
                   jax.ShapeDtypeStruct((m, LANE), F32)],
        compiler_params=_cparams(("parallel",)),
        name="mla_latent",
    )(small, small, g_kv.reshape(1, -1), cos_t, sin_t)


def _kvup_kernel(lat_ref, kr_ref, wuk_ref, wuv_ref, k_ref, v_ref, *, v_transposed):
    lat = lat_ref[...].astype(BF16)
    kn = jnp.dot(lat, wuk_ref[...], preferred_element_type=F32)
    kp = kr_ref[...].astype(BF16)
    if v_transposed:
        vv = lax.dot_general(wuv_ref[...], lat, (((1,), (1,)), ((), ())),
                             preferred_element_type=F32)
    else:
        vv = jnp.dot(lat, wuv_ref[...], preferred_element_type=F32)
    for h in range(MLA_HEADS):
        sl = slice(h * LANE, (h + 1) * LANE)
        k_ref[h, :, 0:LANE] = kn[:, sl].astype(BF16)
        k_ref[h, :, LANE:2 * LANE] = kp
        v_ref[h] = (vv[sl, :] if v_transposed else vv[:, sl]).astype(BF16)


def _kvup(lat, kr, wuk, wuv, *, v_transposed=False):
    m, rk = lat.shape
    tm = _pick(m, (512, 256, 128))
    full = lambda i: (0, 0)
    if v_transposed:
        v_spec = pl.BlockSpec((MLA_HEADS, LANE, tm), lambda i: (0, 0, i))
        v_shape = (MLA_HEADS, LANE, m)
    else:
        v_spec = pl.BlockSpec((MLA_HEADS, tm, LANE), lambda i: (0, i, 0))
        v_shape = (MLA_HEADS, m, LANE)
    return pl.pallas_call(
        functools.partial(_kvup_kernel, v_transposed=v_transposed),
        grid=(m // tm,),
        in_specs=[pl.BlockSpec((tm, rk), lambda i: (i, 0)),
                  pl.BlockSpec((tm, LANE), lambda i: (i, 0)),
                  pl.BlockSpec(wuk.shape, full),
                  pl.BlockSpec(wuv.shape, full)],
        out_specs=[pl.BlockSpec((MLA_HEADS, tm, 2 * LANE), lambda i: (0, i, 0)), v_spec],
        out_shape=[jax.ShapeDtypeStruct((MLA_HEADS, m, 2 * LANE), BF16),
                   jax.ShapeDtypeStruct(v_shape, BF16)],
        compiler_params=_cparams(("parallel",)),
        name="mla_kv",
    )(lat, kr, wuk, wuv)


def _last_kblock(qi, *, tq, tk, nk, q_off, k_off):
    top_chunk = ((qi + 1) * tq - 1 + q_off) // CHUNK
    last_key = (top_chunk + 1) * CHUNK - 1 - k_off
    return jnp.minimum(last_key // tk, nk - 1)


def _attn_kernel(q_ref, k_ref, v_ref, o_ref, m_scr, l_scr, acc_scr, *, hps, tq, tk, nk,
                 q_off, k_off):
    qi = pl.program_id(2)
    ki = pl.program_id(3)

    @pl.when(ki == 0)
    def _():
        m_scr[...] = jnp.full(m_scr.shape, NEG_BIG, F32)
        l_scr[...] = jnp.zeros(l_scr.shape, F32)
        acc_scr[...] = jnp.zeros(acc_scr.shape, F32)

    @pl.when(ki <= _last_kblock(qi, tq=tq, tk=tk, nk=nk, q_off=q_off, k_off=k_off))
    def _():
        q_chunk = (qi * tq + q_off + lax.broadcasted_iota(jnp.int32, (tq, 1), 0)) >> CHUNK_SHIFT
        k_chunk = (ki * tk + k_off + lax.broadcasted_iota(jnp.int32, (1, tk), 1)) >> CHUNK_SHIFT
        visible = q_chunk >= k_chunk

        def head(h, carry):
            s = lax.dot_general(q_ref[h], k_ref[h], (((1,), (1,)), ((), ())),
                                preferred_element_type=F32)
            s = jnp.where(visible, s, NEG_BIG)
            m_prev = m_scr[h]
            m_new = jnp.maximum(m_prev, jnp.max(s, axis=-1, keepdims=True))
            p = jnp.exp2(s - m_new)
            alpha = jnp.exp2(m_prev - m_new)
            l_scr[h] = alpha * l_scr[h] + jnp.sum(p, axis=-1, keepdims=True)
            acc_scr[h] = alpha * acc_scr[h] + jnp.dot(p.astype(BF16), v_ref[h],
                                                      preferred_element_type=F32)
            m_scr[h] = m_new
            return carry

        lax.fori_loop(0, hps, head, 0)

    @pl.when(ki == nk - 1)
    def _():
        for h in range(hps):
            o_ref[:, h * LANE:(h + 1) * LANE] = (acc_scr[h] / l_scr[h]).astype(o_ref.dtype)


def _attention(q, k, v, *, B, Tq, Tk, tq, tk, hps, q_off, k_off):
    nq = Tq // tq
    nk = Tk // tk
    hg = MLA_HEADS // hps
    dqk = q.shape[2]
    dvh = v.shape[2]
    last = functools.partial(_last_kblock, tq=tq, tk=tk, nk=nk, q_off=q_off, k_off=k_off)
    kern = functools.partial(_attn_kernel, hps=hps, tq=tq, tk=tk, nk=nk, q_off=q_off,
                             k_off=k_off)
    kv_row = lambda b, g, i, j: b * nk + jnp.minimum(j, last(i))
    return pl.pallas_call(
        kern,
        grid=(B, hg, nq, nk),
        in_specs=[pl.BlockSpec((hps, tq, dqk), lambda b, g, i, j: (g, b * nq + i, 0)),
                  pl.BlockSpec((hps, tk, dqk), lambda b, g, i, j: (g, kv_row(b, g, i, j), 0)),
                  pl.BlockSpec((hps, tk, dvh), lambda b, g, i, j: (g, kv_row(b, g, i, j), 0))],
        out_specs=pl.BlockSpec((tq, hps * dvh), lambda b, g, i, j: (b * nq + i, g)),
        out_shape=jax.ShapeDtypeStruct((B * Tq, MLA_HEADS * dvh), BF16),
        scratch_shapes=[pltpu.VMEM((hps, tq, 1), F32),
                        pltpu.VMEM((hps, tq, 1), F32),
                        pltpu.VMEM((hps, tq, dvh), F32)],
        compiler_params=_cparams(("parallel", "parallel", "parallel", "arbitrary")),
        name="mla_attn",
    )(q, k, v)


def _attn_t_kernel(q_ref, k_ref, vt_ref, kp_ref, vtp_ref, o_ref, m_scr, l_scr, acc_scr,
                   *, hps, t, nk):
    qi = pl.program_id(1)
    ki = pl.program_id(2)
    nt = (((1,), (1,)), ((), ()))

    def scores(k_blk, h):
        return lax.dot_general(k_blk, q_ref[h], nt, preferred_element_type=F32)

    @pl.when(ki == 0)
    def _():
        for h in range(hps):
            s = scores(kp_ref[h], h)
            m = jnp.max(s, axis=0, keepdims=True)
            p = jnp.exp2(s - m)
            m_scr[h] = m
            l_scr[h] = jnp.sum(p, axis=0, keepdims=True)
            acc_scr[h] = jnp.dot(vtp_ref[h], p.astype(BF16), preferred_element_type=F32)

    def step(masked):
        if masked:
            k_chunk = (ki * t + lax.broadcasted_iota(jnp.int32, (t, 1), 0)) >> CHUNK_SHIFT
            q_chunk = (qi * t + lax.broadcasted_iota(jnp.int32, (1, t), 1)) >> CHUNK_SHIFT
            bias = jnp.where(q_chunk >= k_chunk, 0.0, NEG_BIG)
        s_next = scores(k_ref[0], 0)
        for h in range(hps):
            s = s_next
            if h + 1 < hps:
                s_next = scores(k_ref[h + 1], h + 1)
            if masked:
                s = s + bias
            m_prev = m_scr[h]
            m_new = jnp.maximum(m_prev, jnp.max(s, axis=0, keepdims=True))
            p = jnp.exp2(s - m_new)
            alpha = jnp.exp2(m_prev - m_new)
            l_scr[h] = alpha * l_scr[h] + jnp.sum(p, axis=0, keepdims=True)
            acc_scr[h] = alpha * acc_scr[h] + jnp.dot(vt_ref[h], p.astype(BF16),
                                                      preferred_element_type=F32)
            m_scr[h] = m_new

    @pl.when(ki < qi)
    def _():
        step(False)

    @pl.when(ki == qi)
    def _():
        step(True)

    @pl.when(ki == nk - 1)
    def _():
        for h in range(hps):
            o_t = acc_scr[h] / l_scr[h]
            o_ref[:, h * LANE:(h + 1) * LANE] = o_t.T.astype(o_ref.dtype)


def _attention_t(q, k, vt, k_pre, vt_pre, *, T, t, hps):
    n = T // t
    hg = MLA_HEADS // hps
    dqk = q.shape[2]
    npre = k_pre.shape[1]
    kern = functools.partial(_attn_t_kernel, hps=hps, t=t, nk=n)
    return pl.pallas_call(
        kern,
        grid=(hg, n, n),
        in_specs=[pl.BlockSpec((hps, t, dqk), lambda g, i, j: (g, i, 0)),
                  pl.BlockSpec((hps, t, dqk), lambda g, i, j: (g, jnp.minimum(j, i), 0)),
                  pl.BlockSpec((hps, LANE, t), lambda g, i, j: (g, 0, jnp.minimum(j, i))),
                  pl.BlockSpec((hps, npre, dqk), lambda g, i, j: (g, 0, 0)),
                  pl.BlockSpec((hps, LANE, npre), lambda g, i, j: (g, 0, 0))],
        out_specs=pl.BlockSpec((t, hps * LANE), lambda g, i, j: (i, g)),
        out_shape=jax.ShapeDtypeStruct((T, MLA_HEADS * LANE), BF16),
        scratch_shapes=[pltpu.VMEM((hps, 1, t), F32),
                        pltpu.VMEM((hps, 1, t), F32),
                        pltpu.VMEM((hps, LANE, t), F32)],
        compiler_params=_cparams(("parallel", "parallel", "arbitrary")),
        name="mla_attn_t",
    )(q, k, vt, k_pre, vt_pre)


def _absorb_q_kernel(q_ref, w_ref, o_ref):
    o_ref[0] = jnp.dot(q_ref[0, :, 0:MLA_NOPE], w_ref[0],
                       preferred_element_type=F32).astype(o_ref.dtype)


def _absorb_q(q, w_uk_t3):
    heads, rows, dqk = q.shape
    rk = w_uk_t3.shape[2]
    return pl.pallas_call(
        _absorb_q_kernel,
        grid=(heads,),
        in_specs=[pl.BlockSpec((1, rows, dqk), lambda h: (h, 0, 0)),
                  pl.BlockSpec((1, MLA_NOPE, rk), lambda h: (h, 0, 0))],
        out_specs=pl.BlockSpec((1, rows, rk), lambda h: (h, 0, 0)),
        out_shape=jax.ShapeDtypeStruct((heads, rows, rk), BF16),
        compiler_params=_cparams(("parallel",)),
        name="mla_absorb_q",
    )(q, w_uk_t3)


def _attn_latent_kernel(ql_ref, q_ref, plat_ref, pkr_ref, lat_ref, kr_ref, o_ref, *, T, P):
    heads, _, rk = ql_ref.shape
    rows = heads * T
    nt = (((1,), (1,)), ((), ()))
    ql = ql_ref[...].reshape(rows, rk)
    qpe = q_ref[:, :, LANE:2 * LANE].reshape(rows, LANE)[:, 0:MLA_ROPE]
    lat_all = jnp.concatenate([plat_ref[0].astype(BF16), lat_ref[...].astype(BF16)], axis=0)
    kr_all = jnp.concatenate([pkr_ref[0], kr_ref[:, 0:MLA_ROPE]], axis=0).astype(BF16)
    s = (lax.dot_general(ql, lat_all, nt, preferred_element_type=F32)
         + lax.dot_general(qpe, kr_all, nt, preferred_element_type=F32))
    tok = lax.rem(lax.broadcasted_iota(jnp.int32, (rows, 1), 0), T)
    q_chunk = (P + tok) >> CHUNK_SHIFT
    k_chunk = lax.broadcasted_iota(jnp.int32, (1, P + T), 1) >> CHUNK_SHIFT
    s = jnp.where(q_chunk >= k_chunk, s, NEG_BIG)
    p = jnp.exp2(s - jnp.max(s, axis=-1, keepdims=True))
    o = jnp.dot(p.astype(BF16), lat_all, preferred_element_type=F32)
    o = o / jnp.sum(p, axis=-1, keepdims=True)
    o_ref[...] = o.reshape(heads, T, rk).astype(o_ref.dtype)


def _attn_latent(qlat, q, past_lat, past_kr, lat, kr, *, B, T):
    heads, _, rk = qlat.shape
    P = past_lat.shape[1]
    kern = functools.partial(_attn_latent_kernel, T=T, P=P)
    return pl.pallas_call(
        kern,
        grid=(B,),
        in_specs=[pl.BlockSpec((heads, T, rk), lambda b: (0, b, 0)),
                  pl.BlockSpec((heads, T, q.shape[2]), lambda b: (0, b, 0)),
                  pl.BlockSpec((1, P, rk), lambda b: (b, 0, 0)),
                  pl.BlockSpec((1, P, past_kr.shape[2]), lambda b: (b, 0, 0)),
                  pl.BlockSpec((T, rk), lambda b: (b, 0)),
                  pl.BlockSpec((T, LANE), lambda b: (b, 0))],
        out_specs=pl.BlockSpec((heads, T, rk), lambda b: (0, b, 0)),
        out_shape=jax.ShapeDtypeStruct((heads, B * T, rk), BF16),
        compiler_params=_cparams(("parallel",)),
        name="mla_attn_latent",
    )(qlat, q, past_lat, past_kr, lat, kr)


def _absorb_out_kernel(o_ref, w_ref, out_ref):
    out_ref[...] = jnp.dot(o_ref[0], w_ref[0], preferred_element_type=F32).astype(out_ref.dtype)


def _absorb_out(olat, w_uv3):
    heads, rows, rk = olat.shape
    dvh = w_uv3.shape[2]
    return pl.pallas_call(
        _absorb_out_kernel,
        grid=(heads,),
        in_specs=[pl.BlockSpec((1, rows, rk), lambda h: (h, 0, 0)),
                  pl.BlockSpec((1, rk, dvh), lambda h: (h, 0, 0))],
        out_specs=pl.BlockSpec((rows, dvh), lambda h: (0, h)),
        out_shape=jax.ShapeDtypeStruct((rows, heads * dvh), BF16),
        compiler_params=_cparams(("parallel",)),
        name="mla_absorb_out",
    )(olat, w_uv3)


def _merge_kernel(a_ref, gb_ref, om_ref, x_ref, wo_ref, gf_ref, x1_ref, h2_ref):
    merged = a_ref[...].astype(F32) + _sigmoid(gb_ref[...].astype(F32)) * om_ref[...].astype(F32)
    x1 = x_ref[...] + jnp.dot(merged.astype(BF16), wo_ref[...], preferred_element_type=F32)
    x1_ref[...] = x1
    h2_ref[...] = _rmsnorm(x1, gf_ref[...]).astype(BF16)


def _merge(branch_a, big, o_m, x, wo, g_ffn, *, col):
    m, d = x.shape
    tm = _pick(m, (384, 256, 128))
    row = lambda i: (i, 0)
    return pl.pallas_call(
        _merge_kernel,
        grid=(m // tm,),
        in_specs=[pl.BlockSpec((tm, d), row),
                  pl.BlockSpec((tm, d), lambda i: (i, col["gb"] // d)),
                  pl.BlockSpec((tm, d), row),
                  pl.BlockSpec((tm, d), row),
                  pl.BlockSpec(wo.shape, lambda i: (0, 0)),
                  pl.BlockSpec((1, d), lambda i: (0, 0))],
        out_specs=[pl.BlockSpec((tm, d), row), pl.BlockSpec((tm, d), row)],
        out_shape=[jax.ShapeDtypeStruct((m, d), F32), jax.ShapeDtypeStruct((m, d), BF16)],
        compiler_params=_cparams(("parallel",)),
        name="merge_out_proj",
    )(branch_a, big, o_m, x, wo, g_ffn.reshape(1, -1))


HALO = 8


def _ffn_kernel(h_ref, x1_ref, wa_ref, wb_ref, wd_ref, cwa_ref, cwb_ref, cba_ref, cbb_ref,
                ha_ref, hb_ref, gf_ref, y_ref, ca_ref, cb_ref, ext_scr, carry_scr,
                *, bb, r, tf, loc, carried):
    rt = pl.program_id(1)
    f = pl.program_id(2)
    nf = pl.num_programs(2)
    d = h_ref.shape[2]
    h = h_ref[...].reshape(bb * r, d)

    if carried:
        @pl.when(rt == 0)
        def _():
            carry_scr[f, 0] = ha_ref[...]
            carry_scr[f, 1] = hb_ref[...]

    conv = []
    for half, (w_ref, cw_ref, cbias_ref, hist_ref, cout_ref) in enumerate(
            ((wa_ref, cwa_ref, cba_ref, ha_ref, ca_ref),
             (wb_ref, cwb_ref, cbb_ref, hb_ref, cb_ref))):
        u = jnp.dot(h, w_ref[...], preferred_element_type=F32).reshape(bb, r, tf)
        ext_scr[half, :, HALO:HALO + r, :] = u
        ext_scr[half, :, HALO - 2:HALO, :] = carry_scr[f, half] if carried else hist_ref[...]
        u1 = ext_scr[half, :, HALO - 1:HALO - 1 + r, :]
        u2 = ext_scr[half, :, HALO - 2:HALO - 2 + r, :]
        cw = cw_ref[...]
        conv.append(cbias_ref[...] + cw[0:1] * u2 + cw[1:2] * u1 + cw[2:3] * u)
        if carried:
            carry_scr[f, half] = ext_scr[half, :, HALO + r - 2:HALO + r, :]
        cout_ref[0] = ext_scr[half, :, HALO + loc:HALO + loc + 2, :]

    @pl.when(f == 0)
    def _():
        y_ref[...] = jnp.zeros(y_ref.shape, F32)

    act = (conv[0] * _sigmoid(conv[0])) * conv[1]
    y_ref[...] += jnp.dot(act.reshape(bb * r, tf).astype(BF16), wd_ref[...],
                          preferred_element_type=F32).reshape(bb, r, d)

    @pl.when(f == nf - 1)
    def _():
        y_ref[...] = _rmsnorm(x1_ref[...] + y_ref[...], gf_ref[...])


def _ffn(h2, x1, w_up, w_down, conv_w, conv_b, hist, g_final, *, B, T, Tp):
    d = h2.shape[1]
    dff = w_down.shape[0]
    tf = _pick(dff, (512, 256, 128))
    nf = dff // tf
    if Tp <= 128:
        bb, r = B, Tp
    else:
        bb, r = 1, _pick(Tp, (FFN_ROW_TILE, 128))
    nrt = Tp // r
    carried = nrt > 1
    loc = (T - 2) - (nrt - 1) * r
    assert 0 <= loc <= r - 2, "final two valid rows must sit in the last row tile"
    kern = functools.partial(_ffn_kernel, bb=bb, r=r, tf=tf, loc=loc, carried=carried)
    rows = lambda s, t, f: (s, t, 0)
    carry_shape = (nf, 2, bb, 2, tf) if carried else (1, 1, 1, 2, LANE)
    y, ca, cb = pl.pallas_call(
        kern,
        grid=(B // bb, nrt, nf),
        in_specs=[pl.BlockSpec((bb, r, d), rows),
                  pl.BlockSpec((bb, r, d), rows),
                  pl.BlockSpec((d, tf), lambda s, t, f: (0, f)),
                  pl.BlockSpec((d, tf), lambda s, t, f: (0, nf + f)),
                  pl.BlockSpec((tf, d), lambda s, t, f: (f, 0)),
                  pl.BlockSpec((CONV_W, tf), lambda s, t, f: (0, f)),
                  pl.BlockSpec((CONV_W, tf), lambda s, t, f: (0, nf + f)),
                  pl.BlockSpec((1, tf), lambda s, t, f: (0, f)),
                  pl.BlockSpec((1, tf), lambda s, t, f: (0, nf + f)),
                  pl.BlockSpec((bb, 2, tf), lambda s, t, f: (s, 0, f)),
                  pl.BlockSpec((bb, 2, tf), lambda s, t, f: (s, 0, nf + f)),
                  pl.BlockSpec((1, d), lambda s, t, f: (0, 0))],
        out_specs=[pl.BlockSpec((bb, r, d), rows),
                   pl.BlockSpec((1, bb, 2, tf), lambda s, t, f: (t, s, 0, f)),
                   pl.BlockSpec((1, bb, 2, tf), lambda s, t, f: (t, s, 0, f))],
        out_shape=[jax.ShapeDtypeStruct((B, Tp, d), F32),
                   jax.ShapeDtypeStruct((nrt, B, 2, dff), F32),
                   jax.ShapeDtypeStruct((nrt, B, 2, dff), F32)],
        scratch_shapes=[pltpu.VMEM((2, bb, HALO + r, tf), F32),
                        pltpu.VMEM(carry_shape, F32)],
        compiler_params=_cparams(("parallel", "arbitrary", "arbitrary")),
        name="conv_ffn",
    )(h2.reshape(B, Tp, d), x1.reshape(B, Tp, d), w_up, w_up, w_down, conv_w, conv_w,
      conv_b.reshape(1, -1), conv_b.reshape(1, -1), hist, hist, g_final.reshape(1, -1))
    return y, jnp.concatenate([ca[nrt - 1], cb[nrt - 1]], axis=-1)


def _rope_tables(pos):
    half = MLA_ROPE // 2
    inv = ROPE_THETA ** (-jnp.arange(0, MLA_ROPE, 2, dtype=F32) / MLA_ROPE)
    ang = pos.astype(F32)[:, None] * inv[None, :]
    cos, sin = jnp.cos(ang), jnp.sin(ang)
    zero = jnp.zeros((pos.shape[0], LANE - 2 * half), F32)
    return (jnp.concatenate([cos, cos, zero], axis=1),
            jnp.concatenate([-sin, sin, zero], axis=1))


def _stream(x, w, *, B, T, pos, q_off, s0, hist, past_lat=None, past_kr=None, prefix=None,
            emit_prefix=False):
    col = w["col"]
    dk, dv = w["dk"], w["dv"]
    h = _norm_cast(x, w["g_mix"])
    big = _matmul(h, w["w_big"], BF16, tn=1024)
    small = _matmul(h, w["w_small"], F32, tn=w["w_small"].shape[1])

    branch_a, state = _gla(big, small, w["wa_pad"], w["b_a"], w["g_gla_out"], s0,
                           B=B, T=T, Tp=T, dk=dk, dv=dv, col=col)

    cos_t, sin_t = _rope_tables(pos)
    q = _qprep(small, w["g_q"], w["wq_nope"], w["wq_pe"], w["wq_pe_sw"], cos_t, sin_t, col=col)
    lat, kr = _lat(small, w["g_kv"], cos_t, sin_t, col=col)
    own_prefix = None
    if prefix is not None:
        assert B == 1 and past_lat is None
        k, vt = _kvup(lat, kr, w["w_uk"], w["w_uv_t"], v_transposed=True)
        o_m = _attention_t(q, k, vt, prefix[0], prefix[1], T=T, t=_pick(T, (1024, 128)),
                           hps=MLA_HEADS // 4)
    elif past_lat is not None:
        qlat = _absorb_q(q, w["w_uk_t3"])
        olat = _attn_latent(qlat, q, past_lat, past_kr, lat, kr, B=B, T=T)
        o_m = _absorb_out(olat, w["w_uv3"])
    else:
        k, v = _kvup(lat, kr, w["w_uk"], w["w_uv"])
        if emit_prefix:
            own_prefix = _kvup(lat, kr, w["w_uk"], w["w_uv_t"], v_transposed=True)
        o_m = _attention(q, k, v, B=B, Tq=T, Tk=T, tq=T, tk=T, hps=MLA_HEADS,
                         q_off=q_off, k_off=0)

    x1, h2 = _merge(branch_a, big, o_m, x, w["w_o"], w["g_ffn"], col=col)
    y, conv = _ffn(h2, x1, w["w_up"], w["w_down"], w["conv_w"], w["conv_b"], hist,
                   w["final_norm"], B=B, T=T, Tp=T)
    return y, lat, kr, state, conv, own_prefix


def _prep_weights(g_mix, w_in, w_a2, b_a, g_gla_out, g_q, w_uq, g_kv, w_uk, w_uv, w_o,
                  g_ffn, w_up, conv_w, conv_b, w_down, final_norm):
    d = w_in.shape[0]
    rank, gqk = w_a2.shape
    gvw = GLA_HEADS * g_gla_out.shape[0]
    rq, rk = g_q.shape[0], g_kv.shape[0]
    half = MLA_ROPE // 2
    o, offs = 0, {}
    for name, width in (("q", gqk), ("k", gqk), ("v", gvw), ("r", gvw), ("a", rank),
                        ("cq", rq), ("ckv", rk), ("kpe", MLA_ROPE), ("ga", d), ("gb", d)):
        offs[name] = (o, o + width)
        o += width
    assert o == w_in.shape[1]
    sl = lambda n: w_in[:, offs[n][0]:offs[n][1]]
    w_big = jnp.concatenate([sl("q"), sl("k"), sl("v"), sl("r"), sl("ga"), sl("gb")], axis=1)
    kpe = sl("kpe")
    kpe_sw = jnp.concatenate([kpe[:, half:], kpe[:, :half]], axis=1)
    a_pad = jnp.zeros((d, LANE - rank), w_in.dtype)
    w_small = jnp.concatenate([sl("cq"), sl("ckv"), kpe, kpe_sw, sl("a"), a_pad], axis=1)
    col = {"q": 0, "k": gqk, "v": 2 * gqk, "r": 2 * gqk + gvw, "ga": 2 * gqk + 2 * gvw,
           "gb": 2 * gqk + 2 * gvw + d,
           "cq": 0, "ckv": rq, "kpe": rq + rk, "a": rq + rk + 2 * MLA_ROPE}

    w3 = w_uq.reshape(rq, MLA_HEADS, MLA_NOPE + MLA_ROPE)
    pe = w3[:, :, MLA_NOPE:]
    pe_sw = jnp.concatenate([pe[:, :, half:], pe[:, :, :half]], axis=2)
    zpad = jnp.zeros((rq, MLA_HEADS, LANE - MLA_ROPE), w_uq.dtype)
    flat = lambda t: t.reshape(rq, -1).astype(BF16)
    wa_pad = jnp.concatenate([w_a2, jnp.zeros((LANE - rank, gqk), w_a2.dtype)], axis=0)
    return dict(
        col=col, dk=gqk // GLA_HEADS, dv=g_gla_out.shape[0],
        g_mix=g_mix, w_big=w_big.astype(BF16), w_small=w_small.astype(BF16),
        wa_pad=wa_pad.astype(BF16), b_a=b_a, g_gla_out=g_gla_out, g_q=g_q,
        wq_nope=flat(w3[:, :, :MLA_NOPE]),
        wq_pe=flat(jnp.concatenate([pe, zpad], axis=2)),
        wq_pe_sw=flat(jnp.concatenate([pe_sw, zpad], axis=2)),
        g_kv=g_kv, w_uk=w_uk.astype(BF16), w_uv=w_uv.astype(BF16),
        w_uv_t=w_uv.T.astype(BF16),
        w_uk_t3=w_uk.reshape(rk, MLA_HEADS, MLA_NOPE).transpose(1, 2, 0).astype(BF16),
        w_uv3=w_uv.reshape(rk, MLA_HEADS, MLA_V).transpose(1, 0, 2).astype(BF16),
        w_o=w_o.astype(BF16),
        g_ffn=g_ffn, w_up=w_up.astype(BF16), conv_w=conv_w, conv_b=conv_b,
        w_down=w_down.astype(BF16), final_norm=final_norm)


def kernel(x_prompt, x_sample, cache_mla_latent, cache_mla_krope, state_gla, cache_ffn_conv,
           meta_tokens, g_mix, w_in, w_a2, b_a, g_gla_out, g_q, w_uq, g_kv, w_uk, w_uv, w_o,
           g_ffn, w_up, conv_w, conv_b, w_down, final_norm):
    assert w_in.shape[0] == 1, "single trunk layer"
    bp, seq, d = x_prompt.shape
    assert bp == 1
    bs, ts, _ = x_sample.shape
    P = cache_mla_latent.shape[2]
    w = _prep_weights(g_mix[0], w_in[0], w_a2[0], b_a[0], g_gla_out[0], g_q[0], w_uq[0],
                      g_kv[0], w_uk[0], w_uv[0], w_o[0], g_ffn[0], w_up[0], conv_w[0],
                      conv_b[0], w_down[0], final_norm)
    dk, dv, dff2 = w["dk"], w["dv"], conv_w.shape[2]

    n_meta = meta_tokens.shape[0]
    assert n_meta == N_META and seq % CHUNK == 0
    _, lat_m, kr_m, st_m, cv_m, prefix = _stream(
        meta_tokens.astype(F32), w, B=1, T=n_meta, pos=jnp.arange(n_meta, dtype=jnp.int32),
        q_off=0, s0=jnp.zeros((1, GLA_HEADS, dk, dv), F32),
        hist=jnp.zeros((1, CONV_W - 1, dff2), F32), emit_prefix=True)
    yp, lat_p, kr_p, st_p, cv_p, _ = _stream(
        x_prompt[0], w, B=1, T=seq, pos=n_meta + jnp.arange(seq, dtype=jnp.int32),
        q_off=0, s0=st_m, hist=cv_m, prefix=prefix)

    pos_s = jnp.tile(P + jnp.arange(ts, dtype=jnp.int32), bs)
    ys, lat_s, kr_s, st_s, cv_s, _ = _stream(
        x_sample.reshape(bs * ts, d), w, B=bs, T=ts, pos=pos_s, q_off=P,
        past_lat=cache_mla_latent[0], past_kr=cache_mla_krope[0], s0=state_gla[0],
        hist=cache_ffn_conv[0])

    rk = lat_p.shape[1]
    T = n_meta + seq
    return (yp,
            ys,
            jnp.concatenate([lat_m, lat_p], axis=0).reshape(1, 1, T, rk),
            jnp.concatenate([kr_m, kr_p], axis=0)[:, :MLA_ROPE].reshape(1, 1, T, MLA_ROPE),
            st_p[None],
            cv_p[None],
            lat_s.reshape(1, bs, ts, rk),
            kr_s[:, :MLA_ROPE].reshape(1, bs, ts, MLA_ROPE),
            st_s[None],
            cv_s[None])
```

```python
import functools

import jax
import jax.numpy as jnp
from jax import lax
from jax.experimental import pallas as pl
from jax.experimental.pallas import tpu as pltpu

BF16 = jnp.bfloat16
F32 = jnp.float32

CHUNK = 64
CHUNK_SHIFT = 6
N_META = 16
EPS = 1e-6
GLA_HEADS = 4
GLA_GATE_NORM = 16.0
GLA_LOG_ALPHA_MIN = -5.0
MLA_HEADS = 16
MLA_NOPE = 128
MLA_ROPE = 64
MLA_V = 128
ROPE_THETA = 10000.0
CONV_W = 3
NEG_BIG = -1e30
LOG2E = 1.4426950408889634
QK_SCALE_LOG2E = (MLA_NOPE + MLA_ROPE) ** -0.5 * LOG2E

LANE = 128
ROW_TILE = 1024
VMEM_LIMIT = 56 * 1024 * 1024


def _cparams(sem, vmem=VMEM_LIMIT):
    return pltpu.CompilerParams(dimension_semantics=sem, vmem_limit_bytes=vmem)


def _rmsnorm(x, g):
    return x * lax.rsqrt(jnp.mean(x * x, axis=-1, keepdims=True) + EPS) * g


def _sigmoid(x):
    return 1.0 / (1.0 + jnp.exp(-x))


def _pick(n, cands):
    for c in cands:
        if n % c == 0:
            return c
    if n < min(cands):
        return n
    raise ValueError(f"no tile in {cands} divides {n}")


def _norm_cast_kernel(x_ref, g_ref, o_ref):
    o_ref[...] = _rmsnorm(x_ref[...], g_ref[...]).astype(o_ref.dtype)


def _norm_cast(x, g):
    m, d = x.shape
    tm = _pick(m, (384, 256, 128))
    return pl.pallas_call(
        _norm_cast_kernel,
        grid=(m // tm,),
        in_specs=[pl.BlockSpec((tm, d), lambda i: (i, 0)),
                  pl.BlockSpec((1, d), lambda i: (0, 0))],
        out_specs=pl.BlockSpec((tm, d), lambda i: (i, 0)),
        out_shape=jax.ShapeDtypeStruct((m, d), BF16),
        compiler_params=_cparams(("parallel",)),
        name="norm_cast",
    )(x, g.reshape(1, d))


def _matmul_kernel(a_ref, b_ref, o_ref):
    o_ref[...] = jnp.dot(a_ref[...], b_ref[...], preferred_element_type=F32).astype(o_ref.dtype)


def _matmul(a, b, out_dtype, tn):
    m, k = a.shape
    n = b.shape[1]
    tm = _pick(m, (ROW_TILE, 512, 384, 128))
    return pl.pallas_call(
        _matmul_kernel,
        grid=(n // tn, m // tm),
        in_specs=[pl.BlockSpec((tm, k), lambda j, i: (i, 0)),
                  pl.BlockSpec((k, tn), lambda j, i: (0, j))],
        out_specs=pl.BlockSpec((tm, tn), lambda j, i: (i, j)),
        out_shape=jax.ShapeDtypeStruct((m, n), out_dtype),
        compiler_params=_cparams(("parallel", "parallel")),
        name="in_proj",
    )(a, b)


def _matmul_w32_kernel(a_ref, w_ref, o_ref, w_scr):
    @pl.when(pl.program_id(1) == 0)
    def _():
        w_scr[...] = w_ref[...].astype(BF16)

    o_ref[...] = jnp.dot(a_ref[...], w_scr[...], preferred_element_type=F32).astype(o_ref.dtype)


def _matmul_w32(a, w, n, out_dtype, tn):
    m, k = a.shape
    tm = _pick(m, (ROW_TILE, 512, 384, 128))
    return pl.pallas_call(
        _matmul_w32_kernel,
        grid=(n // tn, m // tm),
        in_specs=[pl.BlockSpec((tm, k), lambda j, i: (i, 0)),
                  pl.BlockSpec((k, tn), lambda j, i: (0, j))],
        out_specs=pl.BlockSpec((tm, tn), lambda j, i: (i, j)),
        out_shape=jax.ShapeDtypeStruct((m, n), out_dtype),
        scratch_shapes=[pltpu.VMEM((k, tn), BF16)],
        compiler_params=_cparams(("parallel", "arbitrary")),
        name="in_proj_w32",
    )(a, w)


def _split3(x):
    a = x.astype(BF16)
    r1 = x - a.astype(F32)
    b = r1.astype(BF16)
    c = (r1 - b.astype(F32)).astype(BF16)
    return a, b, c


def _gla_kernel(q_ref, k_ref, v_ref, r_ref, ga_ref, a_ref, wa_ref, ba_ref, go_ref, s0_ref,
                o_ref, sout_ref, s_scr, *, C, SB, T, H, dk, dv):
    c_idx = pl.program_id(1)
    n_chunks = pl.num_programs(1)

    @pl.when(c_idx == 0)
    def _():
        s_scr[...] = s0_ref[0]

    z = jnp.dot(a_ref[...].astype(BF16), wa_ref[...], preferred_element_type=F32) + ba_ref[...]
    log_sig = jnp.minimum(z, 0.0) - jnp.log1p(jnp.exp(-jnp.abs(z)))
    la = jnp.maximum(log_sig * (1.0 / GLA_GATE_NORM), GLA_LOG_ALPHA_MIN)
    if T % C:
        rows = c_idx * C + lax.broadcasted_iota(jnp.int32, (C, 1), 0)
        la = jnp.where(rows < T, la, 0.0)

    ri = lax.broadcasted_iota(jnp.int32, (C, C), 0)
    ci = lax.broadcasted_iota(jnp.int32, (C, C), 1)
    tri = jnp.where(ri >= ci, 1.0, 0.0).astype(BF16)
    ones = jnp.ones((C, LANE), BF16)
    cs_all = jnp.zeros_like(la)
    dsum_all = jnp.zeros((la.shape[1], LANE), F32)
    for piece in _split3(la):
        cs_all = cs_all + jnp.dot(tri, piece, preferred_element_type=F32)
        dsum_all = dsum_all + lax.dot_general(piece, ones, (((0,), (0,)), ((), ())),
                                              preferred_element_type=F32)

    sr = lax.broadcasted_iota(jnp.int32, (SB, SB), 0)
    sc = lax.broadcasted_iota(jnp.int32, (SB, SB), 1)
    causal = sr >= sc
    nt = (((1,), (1,)), ((), ()))
    scale = dk ** -0.5

    for h in range(H):
        ksl = slice(h * dk, (h + 1) * dk)
        vsl = slice(h * dv, (h + 1) * dv)
        cs = cs_all[:, ksl]
        c_last = cs[C - 1:C, :]
        q = q_ref[:, ksl].astype(F32) * scale
        k = k_ref[:, ksl].astype(F32)
        v = v_ref[:, vsl]
        s_old = s_scr[h]

        o_inter = jnp.dot((q * jnp.exp(cs)).astype(BF16), s_old.astype(BF16),
                          preferred_element_type=F32)
        k_end = (k * jnp.exp(c_last - cs)).astype(BF16)
        upd = lax.dot_general(k_end, v, (((0,), (0,)), ((), ())), preferred_element_type=F32)
        dcol = jnp.exp(dsum_all[ksl, :])
        s_scr[h] = jnp.concatenate([dcol] * (dv // LANE), axis=1) * s_old + upd

        outs = []
        for i in range(C // SB):
            lo = i * SB
            cs_i = cs[lo:lo + SB]
            q_i = q[lo:lo + SB]
            k_i = k[lo:lo + SB]
            start = cs[lo - 1:lo] if i > 0 else jnp.zeros_like(c_last)
            mid = 0.5 * (start + cs[lo + SB - 1:lo + SB])
            qd = (q_i * jnp.exp(cs_i - mid)).astype(BF16)
            kd = (k_i * jnp.exp(mid - cs_i)).astype(BF16)
            att = lax.dot_general(qd, kd, nt, preferred_element_type=F32)
            att = jnp.where(causal, att, 0.0)
            o_i = jnp.dot(att.astype(BF16), v[lo:lo + SB], preferred_element_type=F32)
            if i > 0:
                qo = (q_i * jnp.exp(cs_i - start)).astype(BF16)
                ko = (k[:lo] * jnp.exp(start - cs[:lo])).astype(BF16)
                att_o = lax.dot_general(qo, ko, nt, preferred_element_type=F32)
                o_i = o_i + jnp.dot(att_o.astype(BF16), v[:lo], preferred_element_type=F32)
            outs.append(o_i)
        o = o_inter + (jnp.concatenate(outs, axis=0) if len(outs) > 1 else outs[0])

        on = _rmsnorm(o, go_ref[...])
        r = r_ref[:, vsl].astype(F32)
        g = ga_ref[:, vsl].astype(F32)
        o_ref[:, vsl] = (_sigmoid(g) * (on * (r * _sigmoid(r)))).astype(o_ref.dtype)

    @pl.when(c_idx == n_chunks - 1)
    def _():
        sout_ref[0] = s_scr[...]


def _gla(qkvr, gates, small, wa_pad, b_a, g_out, s0, *, B, T, Tp, dk, dv, col):
    C = min(128, Tp)
    SB = min(32, C)
    nc = Tp // C
    H = GLA_HEADS
    qk, vw = H * dk, H * dv
    rb = lambda b, c: b * nc + c
    kern = functools.partial(_gla_kernel, C=C, SB=SB, T=T, H=H, dk=dk, dv=dv)
    return pl.pallas_call(
        kern,
        grid=(B, nc),
        in_specs=[
            pl.BlockSpec((C, qk), lambda b, c: (rb(b, c), col["q"] // qk)),
            pl.BlockSpec((C, qk), lambda b, c: (rb(b, c), col["k"] // qk)),
            pl.BlockSpec((C, vw), lambda b, c: (rb(b, c), col["v"] // vw)),
            pl.BlockSpec((C, vw), lambda b, c: (rb(b, c), col["r"] // vw)),
            pl.BlockSpec((C, vw), lambda b, c: (rb(b, c), col["ga"] // vw)),
            pl.BlockSpec((C, LANE), lambda b, c: (rb(b, c), col["a"] // LANE)),
            pl.BlockSpec((LANE, qk), lambda b, c: (0, 0)),
            pl.BlockSpec((1, qk), lambda b, c: (0, 0)),
            pl.BlockSpec((1, dv), lambda b, c: (0, 0)),
            pl.BlockSpec((1, H, dk, dv), lambda b, c: (b, 0, 0, 0)),
        ],
        out_specs=[
            pl.BlockSpec((C, vw), lambda b, c: (rb(b, c), 0)),
            pl.BlockSpec((1, H, dk, dv), lambda b, c: (b, 0, 0, 0)),
        ],
        out_shape=[jax.ShapeDtypeStruct((B * Tp, vw), BF16),
                   jax.ShapeDtypeStruct((B, H, dk, dv), F32)],
        scratch_shapes=[pltpu.VMEM((H, dk, dv), F32)],
        compiler_params=_cparams(("parallel", "arbitrary")),
        name="gla",
    )(qkvr, qkvr, qkvr, qkvr, gates, small, wa_pad, b_a.reshape(1, -1), g_out.reshape(1, -1), s0)


def _qprep_kernel(cq_ref, gq_ref, wn_ref, wp_ref, wps_ref, cos_ref, sin_ref, q_ref):
    hq = _rmsnorm(cq_ref[...], gq_ref[...]).astype(BF16)
    qn = jnp.dot(hq, wn_ref[...], preferred_element_type=F32)
    qp = jnp.dot(hq, wp_ref[...], preferred_element_type=F32)
    qs = jnp.dot(hq, wps_ref[...], preferred_element_type=F32)
    cos = cos_ref[...] * QK_SCALE_LOG2E
    sin = sin_ref[...] * QK_SCALE_LOG2E
    for h in range(MLA_HEADS):
        sl = slice(h * LANE, (h + 1) * LANE)
        q_ref[h, :, 0:LANE] = (qn[:, sl] * QK_SCALE_LOG2E).astype(BF16)
        q_ref[h, :, LANE:2 * LANE] = (qp[:, sl] * cos + qs[:, sl] * sin).astype(BF16)


def _qprep(small, g_q, wn, wp, wps, cos_t, sin_t, *, col):
    m = small.shape[0]
    rq = wn.shape[0]
    tm = _pick(m, (256, 128))
    full = lambda i: (0, 0)
    return pl.pallas_call(
        _qprep_kernel,
        grid=(m // tm,),
        in_specs=[pl.BlockSpec((tm, rq), lambda i: (i, col["cq"] // rq)),
                  pl.BlockSpec((1, rq), full),
                  pl.BlockSpec(wn.shape, full),
                  pl.BlockSpec(wp.shape, full),
                  pl.BlockSpec(wps.shape, full),
                  pl.BlockSpec((tm, LANE), lambda i: (i, 0)),
                  pl.BlockSpec((tm, LANE), lambda i: (i, 0))],
        out_specs=pl.BlockSpec((MLA_HEADS, tm, 2 * LANE), lambda i: (0, i, 0)),
        out_shape=jax.ShapeDtypeStruct((MLA_HEADS, m, 2 * LANE), BF16),
        compiler_params=_cparams(("parallel",)),
        name="mla_q",
    )(small, g_q.reshape(1, -1), wn, wp, wps, cos_t, sin_t)


def _lat_kernel(ckv_ref, kpe_ref, gkv_ref, cos_ref, sin_ref, lat_ref, kr_ref):
    lat_ref[...] = _rmsnorm(ckv_ref[...], gkv_ref[...])
    blk = kpe_ref[...]
    kr_ref[...] = blk * cos_ref[...] + pltpu.roll(blk, LANE // 2, 1) * sin_ref[...]


def _lat(small, g_kv, cos_t, sin_t, *, col):
    m = small.shape[0]
    rk = g_kv.shape[0]
    tm = _pick(m, (256, 128))
    return pl.pallas_call(
        _lat_kernel,
        grid=(m // tm,),
        in_specs=[pl.BlockSpec((tm, rk), lambda i: (i, col["ckv"] // rk)),
                  pl.BlockSpec((tm, LANE), lambda i: (i, col["kpe"] // LANE)),
                  pl.BlockSpec((1, rk), lambda i: (0, 0)),
                  pl.BlockSpec((tm, LANE), lambda i: (i, 0)),
                  pl.BlockSpec((tm, LANE), lambda i: (i, 0))],
        out_specs=[pl.BlockSpec((tm, rk), lambda i: (i, 0)),
                   pl.BlockSpec((tm, LANE), lambda i: (i, 0))],
        out_shape=[jax.ShapeDtypeStruct((m, rk), F32),
                   jax.ShapeDtypeStruct((m, LANE), F32)],
        compiler_params=_cparams(("parallel",)),
        name="mla_latent",
    )(small, small, g_kv.reshape(1, -1), cos_t, sin_t)


def _kvup_kernel(lat_ref, kr_ref, wuk_ref, wuv_ref, k_ref, v_ref, *, v_transposed):
    lat = lat_ref[...].astype(BF16)
    kn = jnp.dot(lat, wuk_ref[...], preferred_element_type=F32)
    kp = kr_ref[...].astype(BF16)
    if v_transposed:
        vv = lax.dot_general(wuv_ref[...], lat, (((1,), (1,)), ((), ())),
                             preferred_element_type=F32)
    else:
        vv = jnp.dot(lat, wuv_ref[...], preferred_element_type=F32)
    for h in range(MLA_HEADS):
        sl = slice(h * LANE, (h + 1) * LANE)
        k_ref[h, :, 0:LANE] = kn[:, sl].astype(BF16)
        k_ref[h, :, LANE:2 * LANE] = kp
        v_ref[h] = (vv[sl, :] if v_transposed else vv[:, sl]).astype(BF16)


def _kvup(lat, kr, wuk, wuv, *, v_transposed=False):
    m, rk = lat.shape
    tm = _pick(m, (512, 256, 128))
    full = lambda i: (0, 0)
    if v_transposed:
        v_spec = pl.BlockSpec((MLA_HEADS, LANE, tm), lambda i: (0, 0, i))
        v_shape = (MLA_HEADS, LANE, m)
    else:
        v_spec = pl.BlockSpec((MLA_HEADS, tm, LANE), lambda i: (0, i, 0))
        v_shape = (MLA_HEADS, m, LANE)
    return pl.pallas_call(
        functools.partial(_kvup_kernel, v_transposed=v_transposed),
        grid=(m // tm,),
        in_specs=[pl.BlockSpec((tm, rk), lambda i: (i, 0)),
                  pl.BlockSpec((tm, LANE), lambda i: (i, 0)),
                  pl.BlockSpec(wuk.shape, full),
                  pl.BlockSpec(wuv.shape, full)],
        out_specs=[pl.BlockSpec((MLA_HEADS, tm, 2 * LANE), lambda i: (0, i, 0)), v_spec],
        out_shape=[jax.ShapeDtypeStruct((MLA_HEADS, m, 2 * LANE), BF16),
                   jax.ShapeDtypeStruct(v_shape, BF16)],
        compiler_params=_cparams(("parallel",)),
        name="mla_kv",
    )(lat, kr, wuk, wuv)


def _last_kblock(qi, *, tq, tk, nk, q_off, k_off):
    top_chunk = ((qi + 1) * tq - 1 + q_off) // CHUNK
    last_key = (top_chunk + 1) * CHUNK - 1 - k_off
    return jnp.minimum(last_key // tk, nk - 1)


def _attn_kernel(q_ref, k_ref, v_ref, o_ref, m_scr, l_scr, acc_scr, *, hps, tq, tk, nk,
                 q_off, k_off):
    qi = pl.program_id(2)
    ki = pl.program_id(3)

    @pl.when(ki == 0)
    def _():
        m_scr[...] = jnp.full(m_scr.shape, NEG_BIG, F32)
        l_scr[...] = jnp.zeros(l_scr.shape, F32)
        acc_scr[...] = jnp.zeros(acc_scr.shape, F32)

    @pl.when(ki <= _last_kblock(qi, tq=tq, tk=tk, nk=nk, q_off=q_off, k_off=k_off))
    def _():
        q_chunk = (qi * tq + q_off + lax.broadcasted_iota(jnp.int32, (tq, 1), 0)) >> CHUNK_SHIFT
        k_chunk = (ki * tk + k_off + lax.broadcasted_iota(jnp.int32, (1, tk), 1)) >> CHUNK_SHIFT
        visible = q_chunk >= k_chunk

        def head(h, carry):
            s = lax.dot_general(q_ref[h], k_ref[h], (((1,), (1,)), ((), ())),
                                preferred_element_type=F32)
            s = jnp.where(visible, s, NEG_BIG)
            m_prev = m_scr[h]
            m_new = jnp.maximum(m_prev, jnp.max(s, axis=-1, keepdims=True))
            p = jnp.exp2(s - m_new)
            alpha = jnp.exp2(m_prev - m_new)
            l_scr[h] = alpha * l_scr[h] + jnp.sum(p, axis=-1, keepdims=True)
            acc_scr[h] = alpha * acc_scr[h] + jnp.dot(p.astype(BF16), v_ref[h],
                                                      preferred_element_type=F32)
            m_scr[h] = m_new
            return carry

        lax.fori_loop(0, hps, head, 0)

    @pl.when(ki == nk - 1)
    def _():
        for h in range(hps):
            o_ref[:, h * LANE:(h + 1) * LANE] = (acc_scr[h] / l_scr[h]).astype(o_ref.dtype)


def _attention(q, k, v, *, B, Tq, Tk, tq, tk, hps, q_off, k_off):
    nq = Tq // tq
    nk = Tk // tk
    hg = MLA_HEADS // hps
    dqk = q.shape[2]
    dvh = v.shape[2]
    last = functools.partial(_last_kblock, tq=tq, tk=tk, nk=nk, q_off=q_off, k_off=k_off)
    kern = functools.partial(_attn_kernel, hps=hps, tq=tq, tk=tk, nk=nk, q_off=q_off,
                             k_off=k_off)
    kv_row = lambda b, g, i, j: b * nk + jnp.minimum(j, last(i))
    return pl.pallas_call(
        kern,
        grid=(B, hg, nq, nk),
        in_specs=[pl.BlockSpec((hps, tq, dqk), lambda b, g, i, j: (g, b * nq + i, 0)),
                  pl.BlockSpec((hps, tk, dqk), lambda b, g, i, j: (g, kv_row(b, g, i, j), 0)),
                  pl.BlockSpec((hps, tk, dvh), lambda b, g, i, j: (g, kv_row(b, g, i, j), 0))],
        out_specs=pl.BlockSpec((tq, hps * dvh), lambda b, g, i, j: (b * nq + i, g)),
        out_shape=jax.ShapeDtypeStruct((B * Tq, MLA_HEADS * dvh), BF16),
        scratch_shapes=[pltpu.VMEM((hps, tq, 1), F32),
                        pltpu.VMEM((hps, tq, 1), F32),
                        pltpu.VMEM((hps, tq, dvh), F32)],
        compiler_params=_cparams(("parallel", "parallel", "parallel", "arbitrary")),
        name="mla_attn",
    )(q, k, v)


def _attn_t_kernel(q_ref, k_ref, vt_ref, kp_ref, vtp_ref, o_ref, m_scr, l_scr, acc_scr,
                   *, hps, t, nk):
    qi = pl.program_id(1)
    ki = pl.program_id(2)
    nt = (((1,), (1,)), ((), ()))

    def scores(k_blk, h):
        return lax.dot_general(k_blk, q_ref[h], nt, preferred_element_type=F32)

    @pl.when(ki == 0)
    def _():
        for h in range(hps):
            s = scores(kp_ref[h], h)
            m = jnp.max(s, axis=0, keepdims=True)
            p = jnp.exp2(s - m)
            m_scr[h] = m
            l_scr[h] = jnp.sum(p, axis=0, keepdims=True)
            acc_scr[h] = jnp.dot(vtp_ref[h], p.astype(BF16), preferred_element_type=F32)

    def step(masked):
        if masked:
            k_chunk = (ki * t + lax.broadcasted_iota(jnp.int32, (t, 1), 0)) >> CHUNK_SHIFT
            q_chunk = (qi * t + lax.broadcasted_iota(jnp.int32, (1, t), 1)) >> CHUNK_SHIFT
            bias = jnp.where(q_chunk >= k_chunk, 0.0, NEG_BIG)
        s_next = scores(k_ref[0], 0)
        for h in range(hps):
            s = s_next
            if h + 1 < hps:
                s_next = scores(k_ref[h + 1], h + 1)
            if masked:
                s = s + bias
            m_prev = m_scr[h]
            m_new = jnp.maximum(m_prev, jnp.max(s, axis=0, keepdims=True))
            p = jnp.exp2(s - m_new)
            alpha = jnp.exp2(m_prev - m_new)
            l_scr[h] = alpha * l_scr[h] + jnp.sum(p, axis=0, keepdims=True)
            acc_scr[h] = alpha * acc_scr[h] + jnp.dot(vt_ref[h], p.astype(BF16),
                                                      preferred_element_type=F32)
            m_scr[h] = m_new

    @pl.when(ki < qi)
    def _():
        step(False)

    @pl.when(ki == qi)
    def _():
        step(True)

    @pl.when(ki == nk - 1)
    def _():
        for h in range(hps):
            o_t = acc_scr[h] / l_scr[h]
            o_ref[:, h * LANE:(h + 1) * LANE] = o_t.T.astype(o_ref.dtype)


def _attention_t(q, k, vt, k_pre, vt_pre, *, T, t, hps):
    n = T // t
    hg = MLA_HEADS // hps
    dqk = q.shape[2]
    npre = k_pre.shape[1]
    kern = functools.partial(_attn_t_kernel, hps=hps, t=t, nk=n)
    return pl.pallas_call(
        kern,
        grid=(hg, n, n),
        in_specs=[pl.BlockSpec((hps, t, dqk), lambda g, i, j: (g, i, 0)),
                  pl.BlockSpec((hps, t, dqk), lambda g, i, j: (g, jnp.minimum(j, i), 0)),
                  pl.BlockSpec((hps, LANE, t), lambda g, i, j: (g, 0, jnp.minimum(j, i))),
                  pl.BlockSpec((hps, npre, dqk), lambda g, i, j: (g, 0, 0)),
                  pl.BlockSpec((hps, LANE, npre), lambda g, i, j: (g, 0, 0))],
        out_specs=pl.BlockSpec((t, hps * LANE), lambda g, i, j: (i, g)),
        out_shape=jax.ShapeDtypeStruct((T, MLA_HEADS * LANE), BF16),
        scratch_shapes=[pltpu.VMEM((hps, 1, t), F32),
                        pltpu.VMEM((hps, 1, t), F32),
                        pltpu.VMEM((hps, LANE, t), F32)],
        compiler_params=_cparams(("parallel", "parallel", "arbitrary")),
        name="mla_attn_t",
    )(q, k, vt, k_pre, vt_pre)


def _absorb_q_kernel(q_ref, w_ref, o_ref):
    o_ref[0] = jnp.dot(q_ref[0, :, 0:MLA_NOPE], w_ref[0],
                       preferred_element_type=F32).astype(o_ref.dtype)


def _absorb_q(q, w_uk_t3):
    heads, rows, dqk = q.shape
    rk = w_uk_t3.shape[2]
    return pl.pallas_call(
        _absorb_q_kernel,
        grid=(heads,),
        in_specs=[pl.BlockSpec((1, rows, dqk), lambda h: (h, 0, 0)),
                  pl.BlockSpec((1, MLA_NOPE, rk), lambda h: (h, 0, 0))],
        out_specs=pl.BlockSpec((1, rows, rk), lambda h: (h, 0, 0)),
        out_shape=jax.ShapeDtypeStruct((heads, rows, rk), BF16),
        compiler_params=_cparams(("parallel",)),
        name="mla_absorb_q",
    )(q, w_uk_t3)


def _attn_latent_kernel(ql_ref, q_ref, plat_ref, pkr_ref, lat_ref, kr_ref, o_ref, *, T, P):
    heads, _, rk = ql_ref.shape
    rows = heads * T
    nt = (((1,), (1,)), ((), ()))
    ql = ql_ref[...].reshape(rows, rk)
    qpe = q_ref[:, :, LANE:2 * LANE].reshape(rows, LANE)[:, 0:MLA_ROPE]
    lat_all = jnp.concatenate([plat_ref[0].astype(BF16), lat_ref[...].astype(BF16)], axis=0)
    kr_all = jnp.concatenate([pkr_ref[0], kr_ref[:, 0:MLA_ROPE]], axis=0).astype(BF16)
    s = (lax.dot_general(ql, lat_all, nt, preferred_element_type=F32)
         + lax.dot_general(qpe, kr_all, nt, preferred_element_type=F32))
    tok = lax.rem(lax.broadcasted_iota(jnp.int32, (rows, 1), 0), T)
    q_chunk = (P + tok) >> CHUNK_SHIFT
    k_chunk = lax.broadcasted_iota(jnp.int32, (1, P + T), 1) >> CHUNK_SHIFT
    s = jnp.where(q_chunk >= k_chunk, s, NEG_BIG)
    p = jnp.exp2(s - jnp.max(s, axis=-1, keepdims=True))
    o = jnp.dot(p.astype(BF16), lat_all, preferred_element_type=F32)
    o = o / jnp.sum(p, axis=-1, keepdims=True)
    o_ref[...] = o.reshape(heads, T, rk).astype(o_ref.dtype)


def _attn_latent(qlat, q, past_lat, past_kr, lat, kr, *, B, T):
    heads, _, rk = qlat.shape
    P = past_lat.shape[1]
    kern = functools.partial(_attn_latent_kernel, T=T, P=P)
    return pl.pallas_call(
        kern,
        grid=(B,),
        in_specs=[pl.BlockSpec((heads, T, rk), lambda b: (0, b, 0)),
                  pl.BlockSpec((heads, T, q.shape[2]), lambda b: (0, b, 0)),
                  pl.BlockSpec((1, P, rk), lambda b: (b, 0, 0)),
                  pl.BlockSpec((1, P, past_kr.shape[2]), lambda b: (b, 0, 0)),
                  pl.BlockSpec((T, rk), lambda b: (b, 0)),
                  pl.BlockSpec((T, LANE), lambda b: (b, 0))],
        out_specs=pl.BlockSpec((heads, T, rk), lambda b: (0, b, 0)),
        out_shape=jax.ShapeDtypeStruct((heads, B * T, rk), BF16),
        compiler_params=_cparams(("parallel",)),
        name="mla_attn_latent",
    )(qlat, q, past_lat, past_kr, lat, kr)


def _absorb_out_kernel(o_ref, w_ref, out_ref):
    out_ref[...] = jnp.dot(o_ref[0], w_ref[0], preferred_element_type=F32).astype(out_ref.dtype)


def _absorb_out(olat, w_uv3):
    heads, rows, rk = olat.shape
    dvh = w_uv3.shape[2]
    return pl.pallas_call(
        _absorb_out_kernel,
        grid=(heads,),
        in_specs=[pl.BlockSpec((1, rows, rk), lambda h: (h, 0, 0)),
                  pl.BlockSpec((1, rk, dvh), lambda h: (h, 0, 0))],
        out_specs=pl.BlockSpec((rows, dvh), lambda h: (0, h)),
        out_shape=jax.ShapeDtypeStruct((rows, heads * dvh), BF16),
        compiler_params=_cparams(("parallel",)),
        name="mla_absorb_out",
    )(olat, w_uv3)


def _merge_kernel(a_ref, gb_ref, om_ref, x_ref, wo_ref, gf_ref, x1_ref, h2_ref):
    merged = a_ref[...].astype(F32) + _sigmoid(gb_ref[...].astype(F32)) * om_ref[...].astype(F32)
    x1 = x_ref[...] + jnp.dot(merged.astype(BF16), wo_ref[...], preferred_element_type=F32)
    x1_ref[...] = x1
    h2_ref[...] = _rmsnorm(x1, gf_ref[...]).astype(BF16)


def _merge(branch_a, gates, o_m, x, wo, g_ffn, *, col):
    m, d = x.shape
    tm = _pick(m, (384, 256, 128))
    row = lambda i: (i, 0)
    return pl.pallas_call(
        _merge_kernel,
        grid=(m // tm,),
        in_specs=[pl.BlockSpec((tm, d), row),
                  pl.BlockSpec((tm, d), lambda i: (i, col["gb"] // d)),
                  pl.BlockSpec((tm, d), row),
                  pl.BlockSpec((tm, d), row),
                  pl.BlockSpec(wo.shape, lambda i: (0, 0)),
                  pl.BlockSpec((1, d), lambda i: (0, 0))],
        out_specs=[pl.BlockSpec((tm, d), row), pl.BlockSpec((tm, d), row)],
        out_shape=[jax.ShapeDtypeStruct((m, d), F32), jax.ShapeDtypeStruct((m, d), BF16)],
        compiler_params=_cparams(("parallel",)),
        name="merge_out_proj",
    )(branch_a, gates, o_m, x, wo, g_ffn.reshape(1, -1))


HALO = 8


def _ffn_up_kernel(h_ref, wa_ref, wb_ref, cwa_ref, cwb_ref, cba_ref, cbb_ref, ha_ref, hb_ref,
                   act_ref, ca_ref, cb_ref, ext_scr, carry_scr, w_scr,
                   *, bb, r, tf, loc, carried):
    s = pl.program_id(1)
    rt = pl.program_id(2)
    d = h_ref.shape[2]

    @pl.when((s == 0) & (rt == 0))
    def _():
        w_scr[0] = wa_ref[...].astype(BF16)
        w_scr[1] = wb_ref[...].astype(BF16)

    if carried:
        @pl.when(rt == 0)
        def _():
            carry_scr[0] = ha_ref[...]
            carry_scr[1] = hb_ref[...]

    h = h_ref[...].reshape(bb * r, d)
    conv = []
    for half, (cw_ref, cbias_ref, hist_ref, cout_ref) in enumerate(
            ((cwa_ref, cba_ref, ha_ref, ca_ref), (cwb_ref, cbb_ref, hb_ref, cb_ref))):
        u = jnp.dot(h, w_scr[half], preferred_element_type=F32).reshape(bb, r, tf)
        ext_scr[half, :, HALO:HALO + r, :] = u
        ext_scr[half, :, HALO - 2:HALO, :] = carry_scr[half] if carried else hist_ref[...]
        u1 = ext_scr[half, :, HALO - 1:HALO - 1 + r, :]
        u2 = ext_scr[half, :, HALO - 2:HALO - 2 + r, :]
        cw = cw_ref[...]
        conv.append(cbias_ref[...] + cw[0:1] * u2 + cw[1:2] * u1 + cw[2:3] * u)
        if carried:
            carry_scr[half] = ext_scr[half, :, HALO + r - 2:HALO + r, :]
        cout_ref[0] = ext_scr[half, :, HALO + loc:HALO + loc + 2, :]

    act_ref[...] = ((conv[0] * _sigmoid(conv[0])) * conv[1]).astype(act_ref.dtype)


def _ffn_down_kernel(act_ref, wd_ref, x1_ref, gf_ref, y_ref):
    k = pl.program_id(1)

    @pl.when(k == 0)
    def _():
        y_ref[...] = jnp.zeros(y_ref.shape, F32)

    y_ref[...] += jnp.dot(act_ref[...], wd_ref[...], preferred_element_type=F32)

    @pl.when(k == pl.num_programs(1) - 1)
    def _():
        y_ref[...] = _rmsnorm(x1_ref[...] + y_ref[...], gf_ref[...])


def _ffn(h2, x1, w_up, w_down, conv_w, conv_b, hist, g_final, *, B, T, Tp):
    d = h2.shape[1]
    dff = w_down.shape[0]
    tf = _pick(dff, (512, 256, 128))
    nf = dff // tf
    if Tp <= 128:
        bb, r = B, Tp
    else:
        bb, r = 1, _pick(Tp, (ROW_TILE, 128))
    nrt = Tp // r
    carried = nrt > 1
    loc = (T - 2) - (nrt - 1) * r
    assert 0 <= loc <= r - 2, "final two valid rows must sit in the last row tile"
    kern = functools.partial(_ffn_up_kernel, bb=bb, r=r, tf=tf, loc=loc, carried=carried)
    carry_shape = (2, bb, 2, tf) if carried else (1, 1, 2, LANE)
    act, ca, cb = pl.pallas_call(
        kern,
        grid=(nf, B // bb, nrt),
        in_specs=[pl.BlockSpec((bb, r, d), lambda f, s, t: (s, t, 0)),
                  pl.BlockSpec((d, tf), lambda f, s, t: (0, f)),
                  pl.BlockSpec((d, tf), lambda f, s, t: (0, nf + f)),
                  pl.BlockSpec((CONV_W, tf), lambda f, s, t: (0, f)),
                  pl.BlockSpec((CONV_W, tf), lambda f, s, t: (0, nf + f)),
                  pl.BlockSpec((1, tf), lambda f, s, t: (0, f)),
                  pl.BlockSpec((1, tf), lambda f, s, t: (0, nf + f)),
                  pl.BlockSpec((bb, 2, tf), lambda f, s, t: (s, 0, f)),
                  pl.BlockSpec((bb, 2, tf), lambda f, s, t: (s, 0, nf + f))],
        out_specs=[pl.BlockSpec((bb, r, tf), lambda f, s, t: (s, t, f)),
                   pl.BlockSpec((1, bb, 2, tf), lambda f, s, t: (t, s, 0, f)),
                   pl.BlockSpec((1, bb, 2, tf), lambda f, s, t: (t, s, 0, f))],
        out_shape=[jax.ShapeDtypeStruct((B, Tp, dff), BF16),
                   jax.ShapeDtypeStruct((nrt, B, 2, dff), F32),
                   jax.ShapeDtypeStruct((nrt, B, 2, dff), F32)],
        scratch_shapes=[pltpu.VMEM((2, bb, HALO + r, tf), F32),
                        pltpu.VMEM(carry_shape, F32),
                        pltpu.VMEM((2, d, tf), BF16)],
        compiler_params=_cparams(("arbitrary", "arbitrary", "arbitrary")),
        name="conv_ffn_up",
    )(h2.reshape(B, Tp, d), w_up, w_up, conv_w, conv_w, conv_b.reshape(1, -1),
      conv_b.reshape(1, -1), hist, hist)

    m = B * Tp
    tm = _pick(m, (512, 256, 128))
    kc = _pick(dff, (1408, 512, 256, 128))
    y = pl.pallas_call(
        _ffn_down_kernel,
        grid=(m // tm, dff // kc),
        in_specs=[pl.BlockSpec((tm, kc), lambda i, k: (i, k)),
                  pl.BlockSpec((kc, d), lambda i, k: (k, 0)),
                  pl.BlockSpec((tm, d), lambda i, k: (i, 0)),
                  pl.BlockSpec((1, d), lambda i, k: (0, 0))],
        out_specs=pl.BlockSpec((tm, d), lambda i, k: (i, 0)),
        out_shape=jax.ShapeDtypeStruct((m, d), F32),
        compiler_params=_cparams(("parallel", "arbitrary")),
        name="ffn_down",
    )(act.reshape(m, dff), w_down, x1, g_final.reshape(1, -1))
    return y.reshape(B, Tp, d), jnp.concatenate([ca[nrt - 1], cb[nrt - 1]], axis=-1)


def _rope_tables(pos):
    half = MLA_ROPE // 2
    inv = ROPE_THETA ** (-jnp.arange(0, MLA_ROPE, 2, dtype=F32) / MLA_ROPE)
    ang = pos.astype(F32)[:, None] * inv[None, :]
    cos, sin = jnp.cos(ang), jnp.sin(ang)
    zero = jnp.zeros((pos.shape[0], LANE - 2 * half), F32)
    return (jnp.concatenate([cos, cos, zero], axis=1),
            jnp.concatenate([-sin, sin, zero], axis=1))


def _stream(x, w, *, B, T, pos, q_off, s0, hist, past_lat=None, past_kr=None, prefix=None,
            emit_prefix=False):
    col = w["col"]
    dk, dv = w["dk"], w["dv"]
    h = _norm_cast(x, w["g_mix"])
    qkvr = _matmul_w32(h, w["w_in"], w["n_qkvr"], BF16, tn=1024)
    gates = _matmul(h, w["w_gates"], BF16, tn=1024)
    small = _matmul(h, w["w_small"], F32, tn=w["w_small"].shape[1])

    branch_a, state = _gla(qkvr, gates, small, w["wa_pad"], w["b_a"], w["g_gla_out"], s0,
                           B=B, T=T, Tp=T, dk=dk, dv=dv, col=col)

    cos_t, sin_t = _rope_tables(pos)
    q = _qprep(small, w["g_q"], w["wq_nope"], w["wq_pe"], w["wq_pe_sw"], cos_t, sin_t, col=col)
    lat, kr = _lat(small, w["g_kv"], cos_t, sin_t, col=col)
    own_prefix = None
    if prefix is not None:
        assert B == 1 and past_lat is None
        k, vt = _kvup(lat, kr, w["w_uk"], w["w_uv_t"], v_transposed=True)
        o_m = _attention_t(q, k, vt, prefix[0], prefix[1], T=T, t=_pick(T, (1024, 128)),
                           hps=MLA_HEADS // 4)
    elif past_lat is not None:
        qlat = _absorb_q(q, w["w_uk_t3"])
        olat = _attn_latent(qlat, q, past_lat, past_kr, lat, kr, B=B, T=T)
        o_m = _absorb_out(olat, w["w_uv3"])
    else:
        k, v = _kvup(lat, kr, w["w_uk"], w["w_uv"])
        if emit_prefix:
            own_prefix = _kvup(lat, kr, w["w_uk"], w["w_uv_t"], v_transposed=True)
        o_m = _attention(q, k, v, B=B, Tq=T, Tk=T, tq=T, tk=T, hps=MLA_HEADS,
                         q_off=q_off, k_off=0)

    x1, h2 = _merge(branch_a, gates, o_m, x, w["w_o"], w["g_ffn"], col=col)
    y, conv = _ffn(h2, x1, w["w_up"], w["w_down"], w["conv_w"], w["conv_b"], hist,
                   w["final_norm"], B=B, T=T, Tp=T)
    return y, lat, kr, state, conv, own_prefix


def _prep_weights(g_mix, w_in, w_a2, b_a, g_gla_out, g_q, w_uq, g_kv, w_uk, w_uv, w_o,
                  g_ffn, w_up, conv_w, conv_b, w_down, final_norm):
    d = w_in.shape[0]
    rank, gqk = w_a2.shape
    gvw = GLA_HEADS * g_gla_out.shape[0]
    rq, rk = g_q.shape[0], g_kv.shape[0]
    half = MLA_ROPE // 2
    o, offs = 0, {}
    for name, width in (("q", gqk), ("k", gqk), ("v", gvw), ("r", gvw), ("a", rank),
                        ("cq", rq), ("ckv", rk), ("kpe", MLA_ROPE), ("ga", d), ("gb", d)):
        offs[name] = (o, o + width)
        o += width
    assert o == w_in.shape[1]
    sl = lambda n: w_in[:, offs[n][0]:offs[n][1]]
    n_qkvr = offs["r"][1]
    assert n_qkvr % 1024 == 0
    w_gates = w_in[:, offs["ga"][0]:offs["gb"][1]]
    kpe = sl("kpe")
    kpe_sw = jnp.concatenate([kpe[:, half:], kpe[:, :half]], axis=1)
    a_pad = jnp.zeros((d, LANE - rank), w_in.dtype)
    w_small = jnp.concatenate([sl("cq"), sl("ckv"), kpe, kpe_sw, sl("a"), a_pad], axis=1)
    col = {"q": 0, "k": gqk, "v": 2 * gqk, "r": 2 * gqk + gvw, "ga": 0, "gb": d,
           "cq": 0, "ckv": rq, "kpe": rq + rk, "a": rq + rk + 2 * MLA_ROPE}

    w3 = w_uq.reshape(rq, MLA_HEADS, MLA_NOPE + MLA_ROPE)
    pe = w3[:, :, MLA_NOPE:]
    pe_sw = jnp.concatenate([pe[:, :, half:], pe[:, :, :half]], axis=2)
    zpad = jnp.zeros((rq, MLA_HEADS, LANE - MLA_ROPE), w_uq.dtype)
    flat = lambda t: t.reshape(rq, -1).astype(BF16)
    wa_pad = jnp.concatenate([w_a2, jnp.zeros((LANE - rank, gqk), w_a2.dtype)], axis=0)
    return dict(
        col=col, dk=gqk // GLA_HEADS, dv=g_gla_out.shape[0],
        g_mix=g_mix, w_in=w_in, n_qkvr=n_qkvr, w_gates=w_gates.astype(BF16),
        w_small=w_small.astype(BF16),
        wa_pad=wa_pad.astype(BF16), b_a=b_a, g_gla_out=g_gla_out, g_q=g_q,
        wq_nope=flat(w3[:, :, :MLA_NOPE]),
        wq_pe=flat(jnp.concatenate([pe, zpad], axis=2)),
        wq_pe_sw=flat(jnp.concatenate([pe_sw, zpad], axis=2)),
        g_kv=g_kv, w_uk=w_uk.astype(BF16), w_uv=w_uv.astype(BF16),
        w_uv_t=w_uv.T.astype(BF16),
        w_uk_t3=w_uk.reshape(rk, MLA_HEADS, MLA_NOPE).transpose(1, 2, 0).astype(BF16),
        w_uv3=w_uv.reshape(rk, MLA_HEADS, MLA_V).transpose(1, 0, 2).astype(BF16),
        w_o=w_o.astype(BF16),
        g_ffn=g_ffn, w_up=w_up, conv_w=conv_w, conv_b=conv_b,
        w_down=w_down.astype(BF16), final_norm=final_norm)


def kernel(x_prompt, x_sample, cache_mla_latent, cache_mla_krope, state_gla, cache_ffn_conv,
           meta_tokens, g_mix, w_in, w_a2, b_a, g_gla_out, g_q, w_uq, g_kv, w_uk, w_uv, w_o,
           g_ffn, w_up, conv_w, conv_b, w_down, final_norm):
    assert w_in.shape[0] == 1, "single trunk layer"
    bp, seq, d = x_prompt.shape
    assert bp == 1
    bs, ts, _ = x_sample.shape
    P = cache_mla_latent.shape[2]
    w = _prep_weights(g_mix[0], w_in[0], w_a2[0], b_a[0], g_gla_out[0], g_q[0], w_uq[0],
                      g_kv[0], w_uk[0], w_uv[0], w_o[0], g_ffn[0], w_up[0], conv_w[0],
                      conv_b[0], w_down[0], final_norm)
    dk, dv, dff2 = w["dk"], w["dv"], conv_w.shape[2]

    n_meta = meta_tokens.shape[0]
    assert n_meta == N_META and seq % CHUNK == 0
    _, lat_m, kr_m, st_m, cv_m, prefix = _stream(
        meta_tokens.astype(F32), w, B=1, T=n_meta, pos=jnp.arange(n_meta, dtype=jnp.int32),
        q_off=0, s0=jnp.zeros((1, GLA_HEADS, dk, dv), F32),
        hist=jnp.zeros((1, CONV_W - 1, dff2), F32), emit_prefix=True)
    yp, lat_p, kr_p, st_p, cv_p, _ = _stream(
        x_prompt[0], w, B=1, T=seq, pos=n_meta + jnp.arange(seq, dtype=jnp.int32),
        q_off=0, s0=st_m, hist=cv_m, prefix=prefix)

    pos_s = jnp.tile(P + jnp.arange(ts, dtype=jnp.int32), bs)
    ys, lat_s, kr_s, st_s, cv_s, _ = _stream(
        x_sample.reshape(bs * ts, d), w, B=bs, T=ts, pos=pos_s, q_off=P,
        past_lat=cache_mla_latent[0], past_kr=cache_mla_krope[0], s0=state_gla[0],
        hist=cache_ffn_conv[0])

    rk = lat_p.shape[1]
    T = n_meta + seq
    return (yp,
            ys,
            jnp.concatenate([lat_m, lat_p], axis=0).reshape(1, 1, T, rk),
            jnp.concatenate([kr_m, kr_p], axis=0)[:, :MLA_ROPE].reshape(1, 1, T, MLA_ROPE),
            st_p[None],
            cv_p[None],
            lat_s.reshape(1, bs, ts, rk),
            kr_s[:, :MLA_ROPE].reshape(1, bs, ts, MLA_ROPE),
            st_s[None],
            cv_s[None])
```

```python
import functools

import jax
import jax.numpy as jnp
from jax import lax
from jax.experimental import pallas as pl
from jax.experimental.pallas import tpu as pltpu

BF16 = jnp.bfloat16
F32 = jnp.float32

CHUNK = 64
CHUNK_SHIFT = 6
N_META = 16
EPS = 1e-6
GLA_HEADS = 4
GLA_GATE_NORM = 16.0
GLA_LOG_ALPHA_MIN = -5.0
MLA_HEADS = 16
MLA_NOPE = 128
MLA_ROPE = 64
MLA_V = 128
ROPE_THETA = 10000.0
CONV_W = 3
NEG_BIG = -1e30
LOG2E = 1.4426950408889634
QK_SCALE_LOG2E = (MLA_NOPE + MLA_ROPE) ** -0.5 * LOG2E

LANE = 128
ROW_TILE = 1024
VMEM_LIMIT = 56 * 1024 * 1024


def _cparams(sem, vmem=VMEM_LIMIT):
    return pltpu.CompilerParams(dimension_semantics=sem, vmem_limit_bytes=vmem)


def _rmsnorm(x, g):
    return x * lax.rsqrt(jnp.mean(x * x, axis=-1, keepdims=True) + EPS) * g


def _sigmoid(x):
    return 1.0 / (1.0 + jnp.exp(-x))


def _pick(n, cands):
    for c in cands:
        if n % c == 0:
            return c
    if n < min(cands):
        return n
    raise ValueError(f"no tile in {cands} divides {n}")


def _norm_cast_kernel(x_ref, g_ref, o_ref):
    o_ref[...] = _rmsnorm(x_ref[...], g_ref[...]).astype(o_ref.dtype)


def _norm_cast(x, g):
    m, d = x.shape
    tm = _pick(m, (384, 256, 128))
    return pl.pallas_call(
        _norm_cast_kernel,
        grid=(m // tm,),
        in_specs=[pl.BlockSpec((tm, d), lambda i: (i, 0)),
                  pl.BlockSpec((1, d), lambda i: (0, 0))],
        out_specs=pl.BlockSpec((tm, d), lambda i: (i, 0)),
        out_shape=jax.ShapeDtypeStruct((m, d), BF16),
        compiler_params=_cparams(("parallel",)),
        name="norm_cast",
    )(x, g.reshape(1, d))


def _matmul_kernel(a_ref, b_ref, o_ref):
    o_ref[...] = jnp.dot(a_ref[...], b_ref[...], preferred_element_type=F32).astype(o_ref.dtype)


def _matmul(a, b, out_dtype, tn):
    m, k = a.shape
    n = b.shape[1]
    tm = _pick(m, (ROW_TILE, 512, 384, 128))
    return pl.pallas_call(
        _matmul_kernel,
        grid=(n // tn, m // tm),
        in_specs=[pl.BlockSpec((tm, k), lambda j, i: (i, 0)),
                  pl.BlockSpec((k, tn), lambda j, i: (0, j))],
        out_specs=pl.BlockSpec((tm, tn), lambda j, i: (i, j)),
        out_shape=jax.ShapeDtypeStruct((m, n), out_dtype),
        compiler_params=_cparams(("parallel", "parallel")),
        name="in_proj",
    )(a, b)


_NT = (((1,), (1,)), ((), ()))


def _matmul_wt_kernel(a_ref, w_ref, o_ref, w_scr):
    @pl.when(pl.program_id(1) == 0)
    def _():
        w_scr[...] = w_ref[...].astype(BF16)

    o_ref[...] = lax.dot_general(a_ref[...], w_scr[...], _NT,
                                 preferred_element_type=F32).astype(o_ref.dtype)


def _matmul_wt(a, w_t, row0, n, out_dtype, tn):
    m, k = a.shape
    tm = _pick(m, (ROW_TILE, 512, 384, 128))
    return pl.pallas_call(
        _matmul_wt_kernel,
        grid=(n // tn, m // tm),
        in_specs=[pl.BlockSpec((tm, k), lambda j, i: (i, 0)),
                  pl.BlockSpec((pl.Element(tn), pl.Element(k)),
                               lambda j, i: (pl.multiple_of(row0 + j * tn, 16), 0))],
        out_specs=pl.BlockSpec((tm, tn), lambda j, i: (i, j)),
        out_shape=jax.ShapeDtypeStruct((m, n), out_dtype),
        scratch_shapes=[pltpu.VMEM((tn, k), BF16)],
        compiler_params=_cparams(("parallel", "arbitrary")),
        name="in_proj_wt",
    )(a, w_t)


def _small_proj_kernel(a_ref, w_ref, o_ref, w_scr, *, rank, rq, rk):
    @pl.when(pl.program_id(0) == 0)
    def _():
        w = w_ref[...].astype(BF16)
        half = MLA_ROPE // 2
        pe0 = rank + rq + rk
        o_pe = rq + rk
        w_scr[0:rq] = w[rank:rank + rq]
        w_scr[rq:o_pe] = w[rank + rq:pe0]
        w_scr[o_pe:o_pe + MLA_ROPE] = w[pe0:pe0 + MLA_ROPE]
        w_scr[o_pe + MLA_ROPE:o_pe + MLA_ROPE + half] = w[pe0 + half:pe0 + MLA_ROPE]
        w_scr[o_pe + MLA_ROPE + half:o_pe + 2 * MLA_ROPE] = w[pe0:pe0 + half]
        o_a = o_pe + 2 * MLA_ROPE
        w_scr[o_a:o_a + rank] = w[0:rank]
        w_scr[o_a + rank:] = jnp.zeros((w_scr.shape[0] - o_a - rank, w_scr.shape[1]), BF16)

    o_ref[...] = lax.dot_general(a_ref[...], w_scr[...], _NT, preferred_element_type=F32)


def _small_proj(a, w_t, row0, *, rank, rq, rk):
    m, k = a.shape
    n_in = rank + rq + rk + MLA_ROPE
    n_out = rq + rk + 2 * MLA_ROPE + LANE
    tm = _pick(m, (ROW_TILE, 512, 384, 128))
    kern = functools.partial(_small_proj_kernel, rank=rank, rq=rq, rk=rk)
    return pl.pallas_call(
        kern,
        grid=(m // tm,),
        in_specs=[pl.BlockSpec((tm, k), lambda i: (i, 0)),
                  pl.BlockSpec((pl.Element(n_in), pl.Element(k)), lambda i: (row0, 0))],
        out_specs=pl.BlockSpec((tm, n_out), lambda i: (i, 0)),
        out_shape=jax.ShapeDtypeStruct((m, n_out), F32),
        scratch_shapes=[pltpu.VMEM((n_out, k), BF16)],
        compiler_params=_cparams(("arbitrary",)),
        name="in_proj_small",
    )(a, w_t)


def _split3(x):
    a = x.astype(BF16)
    r1 = x - a.astype(F32)
    b = r1.astype(BF16)
    c = (r1 - b.astype(F32)).astype(BF16)
    return a, b, c


def _gla_kernel(q_ref, k_ref, v_ref, r_ref, ga_ref, a_ref, wa_ref, ba_ref, go_ref, s0_ref,
                o_ref, sout_ref, s_scr, *, C, SB, T, H, dk, dv):
    c_idx = pl.program_id(1)
    n_chunks = pl.num_programs(1)

    @pl.when(c_idx == 0)
    def _():
        s_scr[...] = s0_ref[0]

    z = jnp.dot(a_ref[...].astype(BF16), wa_ref[...], preferred_element_type=F32) + ba_ref[...]
    log_sig = jnp.minimum(z, 0.0) - jnp.log1p(jnp.exp(-jnp.abs(z)))
    la = jnp.maximum(log_sig * (1.0 / GLA_GATE_NORM), GLA_LOG_ALPHA_MIN)
    if T % C:
        rows = c_idx * C + lax.broadcasted_iota(jnp.int32, (C, 1), 0)
        la = jnp.where(rows < T, la, 0.0)

    ri = lax.broadcasted_iota(jnp.int32, (C, C), 0)
    ci = lax.broadcasted_iota(jnp.int32, (C, C), 1)
    tri = jnp.where(ri >= ci, 1.0, 0.0).astype(BF16)
    ones = jnp.ones((C, LANE), BF16)
    cs_all = jnp.zeros_like(la)
    dsum_all = jnp.zeros((la.shape[1], LANE), F32)
    for piece in _split3(la):
        cs_all = cs_all + jnp.dot(tri, piece, preferred_element_type=F32)
        dsum_all = dsum_all + lax.dot_general(piece, ones, (((0,), (0,)), ((), ())),
                                              preferred_element_type=F32)

    sr = lax.broadcasted_iota(jnp.int32, (SB, SB), 0)
    sc = lax.broadcasted_iota(jnp.int32, (SB, SB), 1)
    causal = sr >= sc
    nt = (((1,), (1,)), ((), ()))
    scale = dk ** -0.5

    for h in range(H):
        ksl = slice(h * dk, (h + 1) * dk)
        vsl = slice(h * dv, (h + 1) * dv)
        cs = cs_all[:, ksl]
        c_last = cs[C - 1:C, :]
        q = q_ref[:, ksl].astype(F32) * scale
        k = k_ref[:, ksl].astype(F32)
        v = v_ref[:, vsl]
        s_old = s_scr[h]

        o_inter = jnp.dot((q * jnp.exp(cs)).astype(BF16), s_old.astype(BF16),
                          preferred_element_type=F32)
        k_end = (k * jnp.exp(c_last - cs)).astype(BF16)
        upd = lax.dot_general(k_end, v, (((0,), (0,)), ((), ())), preferred_element_type=F32)
        dcol = jnp.exp(dsum_all[ksl, :])
        s_scr[h] = jnp.concatenate([dcol] * (dv // LANE), axis=1) * s_old + upd

        outs = []
        for i in range(C // SB):
            lo = i * SB
            cs_i = cs[lo:lo + SB]
            q_i = q[lo:lo + SB]
            k_i = k[lo:lo + SB]
            start = cs[lo - 1:lo] if i > 0 else jnp.zeros_like(c_last)
            mid = 0.5 * (start + cs[lo + SB - 1:lo + SB])
            qd = (q_i * jnp.exp(cs_i - mid)).astype(BF16)
            kd = (k_i * jnp.exp(mid - cs_i)).astype(BF16)
            att = lax.dot_general(qd, kd, nt, preferred_element_type=F32)
            att = jnp.where(causal, att, 0.0)
            o_i = jnp.dot(att.astype(BF16), v[lo:lo + SB], preferred_element_type=F32)
            if i > 0:
                qo = (q_i * jnp.exp(cs_i - start)).astype(BF16)
                ko = (k[:lo] * jnp.exp(start - cs[:lo])).astype(BF16)
                att_o = lax.dot_general(qo, ko, nt, preferred_element_type=F32)
                o_i = o_i + jnp.dot(att_o.astype(BF16), v[:lo], preferred_element_type=F32)
            outs.append(o_i)
        o = o_inter + (jnp.concatenate(outs, axis=0) if len(outs) > 1 else outs[0])

        on = _rmsnorm(o, go_ref[...])
        r = r_ref[:, vsl].astype(F32)
        g = ga_ref[:, vsl].astype(F32)
        o_ref[:, vsl] = (_sigmoid(g) * (on * (r * _sigmoid(r)))).astype(o_ref.dtype)

    @pl.when(c_idx == n_chunks - 1)
    def _():
        sout_ref[0] = s_scr[...]


def _gla(qkvr, gates, small, wa_pad, b_a, g_out, s0, *, B, T, Tp, dk, dv, col):
    C = min(128, Tp)
    SB = min(32, C)
    nc = Tp // C
    H = GLA_HEADS
    qk, vw = H * dk, H * dv
    rb = lambda b, c: b * nc + c
    kern = functools.partial(_gla_kernel, C=C, SB=SB, T=T, H=H, dk=dk, dv=dv)
    return pl.pallas_call(
        kern,
        grid=(B, nc),
        in_specs=[
            pl.BlockSpec((C, qk), lambda b, c: (rb(b, c), col["q"] // qk)),
            pl.BlockSpec((C, qk), lambda b, c: (rb(b, c), col["k"] // qk)),
            pl.BlockSpec((C, vw), lambda b, c: (rb(b, c), col["v"] // vw)),
            pl.BlockSpec((C, vw), lambda b, c: (rb(b, c), col["r"] // vw)),
            pl.BlockSpec((C, vw), lambda b, c: (rb(b, c), col["ga"] // vw)),
            pl.BlockSpec((C, LANE), lambda b, c: (rb(b, c), col["a"] // LANE)),
            pl.BlockSpec((LANE, qk), lambda b, c: (0, 0)),
            pl.BlockSpec((1, qk), lambda b, c: (0, 0)),
            pl.BlockSpec((1, dv), lambda b, c: (0, 0)),
            pl.BlockSpec((1, H, dk, dv), lambda b, c: (b, 0, 0, 0)),
        ],
        out_specs=[
            pl.BlockSpec((C, vw), lambda b, c: (rb(b, c), 0)),
            pl.BlockSpec((1, H, dk, dv), lambda b, c: (b, 0, 0, 0)),
        ],
        out_shape=[jax.ShapeDtypeStruct((B * Tp, vw), BF16),
                   jax.ShapeDtypeStruct((B, H, dk, dv), F32)],
        scratch_shapes=[pltpu.VMEM((H, dk, dv), F32)],
        compiler_params=_cparams(("parallel", "arbitrary")),
        name="gla",
    )(qkvr, qkvr, qkvr, qkvr, gates, small, wa_pad, b_a.reshape(1, -1), g_out.reshape(1, -1), s0)


def _qprep_kernel(cq_ref, gq_ref, wn_ref, wp_ref, wps_ref, cos_ref, sin_ref, q_ref):
    hq = _rmsnorm(cq_ref[...], gq_ref[...]).astype(BF16)
    qn = jnp.dot(hq, wn_ref[...], preferred_element_type=F32)
    qp = jnp.dot(hq, wp_ref[...], preferred_element_type=F32)
    qs = jnp.dot(hq, wps_ref[...], preferred_element_type=F32)
    cos = cos_ref[...] * QK_SCALE_LOG2E
    sin = sin_ref[...] * QK_SCALE_LOG2E
    for h in range(MLA_HEADS):
        sl = slice(h * LANE, (h + 1) * LANE)
        q_ref[h, :, 0:LANE] = (qn[:, sl] * QK_SCALE_LOG2E).astype(BF16)
        q_ref[h, :, LANE:2 * LANE] = (qp[:, sl] * cos + qs[:, sl] * sin).astype(BF16)


def _qprep(small, g_q, wn, wp, wps, cos_t, sin_t, *, col):
    m = small.shape[0]
    rq = wn.shape[0]
    tm = _pick(m, (256, 128))
    full = lambda i: (0, 0)
    return pl.pallas_call(
        _qprep_kernel,
        grid=(m // tm,),
        in_specs=[pl.BlockSpec((tm, rq), lambda i: (i, col["cq"] // rq)),
                  pl.BlockSpec((1, rq), full),
                  pl.BlockSpec(wn.shape, full),
                  pl.BlockSpec(wp.shape, full),
                  pl.BlockSpec(wps.shape, full),
                  pl.BlockSpec((tm, LANE), lambda i: (i, 0)),
                  pl.BlockSpec((tm, LANE), lambda i: (i, 0))],
        out_specs=pl.BlockSpec((MLA_HEADS, tm, 2 * LANE), lambda i: (0, i, 0)),
        out_shape=jax.ShapeDtypeStruct((MLA_HEADS, m, 2 * LANE), BF16),
        compiler_params=_cparams(("parallel",)),
        name="mla_q",
    )(small, g_q.reshape(1, -1), wn, wp, wps, cos_t, sin_t)


def _lat_kernel(ckv_ref, kpe_ref, gkv_ref, cos_ref, sin_ref, lat_ref, kr_ref):
    lat_ref[...] = _rmsnorm(ckv_ref[...], gkv_ref[...])
    blk = kpe_ref[...]
    kr_ref[...] = blk * cos_ref[...] + pltpu.roll(blk, LANE // 2, 1) * sin_ref[...]


def _lat(small, g_kv, cos_t, sin_t, *, col):
    m = small.shape[0]
    rk = g_kv.shape[0]
    tm = _pick(m, (256, 128))
    return pl.pallas_call(
        _lat_kernel,
        grid=(m // tm,),
        in_specs=[pl.BlockSpec((tm, rk), lambda i: (i, col["ckv"] // rk)),
                  pl.BlockSpec((tm, LANE), lambda i: (i, col["kpe"] // LANE)),
                  pl.BlockSpec((1, rk), lambda i: (0, 0)),
                  pl.BlockSpec((tm, LANE), lambda i: (i, 0)),
                  pl.BlockSpec((tm, LANE), lambda i: (i, 0))],
        out_specs=[pl.BlockSpec((tm, rk), lambda i: (i, 0)),
                   pl.BlockSpec((tm, LANE), lambda i: (i, 0))],
        out_shape=[jax.ShapeDtypeStruct((m, rk), F32),
                   jax.ShapeDtypeStruct((m, LANE), F32)],
        compiler_params=_cparams(("parallel",)),
        name="mla_latent",
    )(small, small, g_kv.reshape(1, -1), cos_t, sin_t)


def _kvup_kernel(lat_ref, kr_ref, wuk_ref, wuv_ref, k_ref, v_ref, *, v_transposed):
    lat = lat_ref[...].astype(BF16)
    kn = jnp.dot(lat, wuk_ref[...], preferred_element_type=F32)
    kp = kr_ref[...].astype(BF16)
    if v_transposed:
        vv = lax.dot_general(wuv_ref[...], lat, (((1,), (1,)), ((), ())),
                             preferred_element_type=F32)
    else:
        vv = jnp.dot(lat, wuv_ref[...], preferred_element_type=F32)
    for h in range(MLA_HEADS):
        sl = slice(h * LANE, (h + 1) * LANE)
        k_ref[h, :, 0:LANE] = kn[:, sl].astype(BF16)
        k_ref[h, :, LANE:2 * LANE] = kp
        v_ref[h] = (vv[sl, :] if v_transposed else vv[:, sl]).astype(BF16)


def _kvup(lat, kr, wuk, wuv, *, v_transposed=False):
    m, rk = lat.shape
    tm = _pick(m, (512, 256, 128))
    full = lambda i: (0, 0)
    if v_transposed:
        v_spec = pl.BlockSpec((MLA_HEADS, LANE, tm), lambda i: (0, 0, i))
        v_shape = (MLA_HEADS, LANE, m)
    else:
        v_spec = pl.BlockSpec((MLA_HEADS, tm, LANE), lambda i: (0, i, 0))
        v_shape = (MLA_HEADS, m, LANE)
    return pl.pallas_call(
        functools.partial(_kvup_kernel, v_transposed=v_transposed),
        grid=(m // tm,),
        in_specs=[pl.BlockSpec((tm, rk), lambda i: (i, 0)),
                  pl.BlockSpec((tm, LANE), lambda i: (i, 0)),
                  pl.BlockSpec(wuk.shape, full),
                  pl.BlockSpec(wuv.shape, full)],
        out_specs=[pl.BlockSpec((MLA_HEADS, tm, 2 * LANE), lambda i: (0, i, 0)), v_spec],
        out_shape=[jax.ShapeDtypeStruct((MLA_HEADS, m, 2 * LANE), BF16),
                   jax.ShapeDtypeStruct(v_shape, BF16)],
        compiler_params=_cparams(("parallel",)),
        name="mla_kv",
    )(lat, kr, wuk, wuv)


def _last_kblock(qi, *, tq, tk, nk, q_off, k_off):
    top_chunk = ((qi + 1) * tq - 1 + q_off) // CHUNK
    last_key = (top_chunk + 1) * CHUNK - 1 - k_off
    return jnp.minimum(last_key // tk, nk - 1)


def _attn_kernel(q_ref, k_ref, v_ref, o_ref, m_scr, l_scr, acc_scr, *, hps, tq, tk, nk,
                 q_off, k_off):
    qi = pl.program_id(2)
    ki = pl.program_id(3)

    @pl.when(ki == 0)
    def _():
        m_scr[...] = jnp.full(m_scr.shape, NEG_BIG, F32)
        l_scr[...] = jnp.zeros(l_scr.shape, F32)
        acc_scr[...] = jnp.zeros(acc_scr.shape, F32)

    @pl.when(ki <= _last_kblock(qi, tq=tq, tk=tk, nk=nk, q_off=q_off, k_off=k_off))
    def _():
        q_chunk = (qi * tq + q_off + lax.broadcasted_iota(jnp.int32, (tq, 1), 0)) >> CHUNK_SHIFT
        k_chunk = (ki * tk + k_off + lax.broadcasted_iota(jnp.int32, (1, tk), 1)) >> CHUNK_SHIFT
        visible = q_chunk >= k_chunk

        def head(h, carry):
            s = lax.dot_general(q_ref[h], k_ref[h], (((1,), (1,)), ((), ())),
                                preferred_element_type=F32)
            s = jnp.where(visible, s, NEG_BIG)
            m_prev = m_scr[h]
            m_new = jnp.maximum(m_prev, jnp.max(s, axis=-1, keepdims=True))
            p = jnp.exp2(s - m_new)
            alpha = jnp.exp2(m_prev - m_new)
            l_scr[h] = alpha * l_scr[h] + jnp.sum(p, axis=-1, keepdims=True)
            acc_scr[h] = alpha * acc_scr[h] + jnp.dot(p.astype(BF16), v_ref[h],
                                                      preferred_element_type=F32)
            m_scr[h] = m_new
            return carry

        lax.fori_loop(0, hps, head, 0)

    @pl.when(ki == nk - 1)
    def _():
        for h in range(hps):
            o_ref[:, h * LANE:(h + 1) * LANE] = (acc_scr[h] / l_scr[h]).astype(o_ref.dtype)


def _attention(q, k, v, *, B, Tq, Tk, tq, tk, hps, q_off, k_off):
    nq = Tq // tq
    nk = Tk // tk
    hg = MLA_HEADS // hps
    dqk = q.shape[2]
    dvh = v.shape[2]
    last = functools.partial(_last_kblock, tq=tq, tk=tk, nk=nk, q_off=q_off, k_off=k_off)
    kern = functools.partial(_attn_kernel, hps=hps, tq=tq, tk=tk, nk=nk, q_off=q_off,
                             k_off=k_off)
    kv_row = lambda b, g, i, j: b * nk + jnp.minimum(j, last(i))
    return pl.pallas_call(
        kern,
        grid=(B, hg, nq, nk),
        in_specs=[pl.BlockSpec((hps, tq, dqk), lambda b, g, i, j: (g, b * nq + i, 0)),
                  pl.BlockSpec((hps, tk, dqk), lambda b, g, i, j: (g, kv_row(b, g, i, j), 0)),
                  pl.BlockSpec((hps, tk, dvh), lambda b, g, i, j: (g, kv_row(b, g, i, j), 0))],
        out_specs=pl.BlockSpec((tq, hps * dvh), lambda b, g, i, j: (b * nq + i, g)),
        out_shape=jax.ShapeDtypeStruct((B * Tq, MLA_HEADS * dvh), BF16),
        scratch_shapes=[pltpu.VMEM((hps, tq, 1), F32),
                        pltpu.VMEM((hps, tq, 1), F32),
                        pltpu.VMEM((hps, tq, dvh), F32)],
        compiler_params=_cparams(("parallel", "parallel", "parallel", "arbitrary")),
        name="mla_attn",
    )(q, k, v)


def _attn_t_kernel(q_ref, k_ref, vt_ref, kp_ref, vtp_ref, o_ref, m_scr, l_scr, acc_scr,
                   *, hps, t, nk):
    qi = pl.program_id(1)
    ki = pl.program_id(2)
    nt = (((1,), (1,)), ((), ()))

    def scores(k_blk, h):
        return lax.dot_general(k_blk, q_ref[h], nt, preferred_element_type=F32)

    @pl.when(ki == 0)
    def _():
        for h in range(hps):
            s = scores(kp_ref[h], h)
            m = jnp.max(s, axis=0, keepdims=True)
            p = jnp.exp2(s - m)
            m_scr[h] = m
            l_scr[h] = jnp.sum(p, axis=0, keepdims=True)
            acc_scr[h] = jnp.dot(vtp_ref[h], p.astype(BF16), preferred_element_type=F32)

    def step(masked):
        if masked:
            k_chunk = (ki * t + lax.broadcasted_iota(jnp.int32, (t, 1), 0)) >> CHUNK_SHIFT
            q_chunk = (qi * t + lax.broadcasted_iota(jnp.int32, (1, t), 1)) >> CHUNK_SHIFT
            bias = jnp.where(q_chunk >= k_chunk, 0.0, NEG_BIG)
        s_next = scores(k_ref[0], 0)
        for h in range(hps):
            s = s_next
            if h + 1 < hps:
                s_next = scores(k_ref[h + 1], h + 1)
            if masked:
                s = s + bias
            m_prev = m_scr[h]
            m_new = jnp.maximum(m_prev, jnp.max(s, axis=0, keepdims=True))
            p = jnp.exp2(s - m_new)
            alpha = jnp.exp2(m_prev - m_new)
            l_scr[h] = alpha * l_scr[h] + jnp.sum(p, axis=0, keepdims=True)
            acc_scr[h] = alpha * acc_scr[h] + jnp.dot(vt_ref[h], p.astype(BF16),
                                                      preferred_element_type=F32)
            m_scr[h] = m_new

    @pl.when(ki < qi)
    def _():
        step(False)

    @pl.when(ki == qi)
    def _():
        step(True)

    @pl.when(ki == nk - 1)
    def _():
        for h in range(hps):
            o_t = acc_scr[h] / l_scr[h]
            o_ref[:, h * LANE:(h + 1) * LANE] = o_t.T.astype(o_ref.dtype)


def _attention_t(q, k, vt, k_pre, vt_pre, *, T, t, hps):
    n = T // t
    hg = MLA_HEADS // hps
    dqk = q.shape[2]
    npre = k_pre.shape[1]
    kern = functools.partial(_attn_t_kernel, hps=hps, t=t, nk=n)
    return pl.pallas_call(
        kern,
        grid=(hg, n, n),
        in_specs=[pl.BlockSpec((hps, t, dqk), lambda g, i, j: (g, i, 0)),
                  pl.BlockSpec((hps, t, dqk), lambda g, i, j: (g, jnp.minimum(j, i), 0)),
                  pl.BlockSpec((hps, LANE, t), lambda g, i, j: (g, 0, jnp.minimum(j, i))),
                  pl.BlockSpec((hps, npre, dqk), lambda g, i, j: (g, 0, 0)),
                  pl.BlockSpec((hps, LANE, npre), lambda g, i, j: (g, 0, 0))],
        out_specs=pl.BlockSpec((t, hps * LANE), lambda g, i, j: (i, g)),
        out_shape=jax.ShapeDtypeStruct((T, MLA_HEADS * LANE), BF16),
        scratch_shapes=[pltpu.VMEM((hps, 1, t), F32),
                        pltpu.VMEM((hps, 1, t), F32),
                        pltpu.VMEM((hps, LANE, t), F32)],
        compiler_params=_cparams(("parallel", "parallel", "arbitrary")),
        name="mla_attn_t",
    )(q, k, vt, k_pre, vt_pre)


def _absorb_q_kernel(q_ref, w_ref, o_ref):
    o_ref[0] = jnp.dot(q_ref[0, :, 0:MLA_NOPE], w_ref[0],
                       preferred_element_type=F32).astype(o_ref.dtype)


def _absorb_q(q, w_uk_t3):
    heads, rows, dqk = q.shape
    rk = w_uk_t3.shape[2]
    return pl.pallas_call(
        _absorb_q_kernel,
        grid=(heads,),
        in_specs=[pl.BlockSpec((1, rows, dqk), lambda h: (h, 0, 0)),
                  pl.BlockSpec((1, MLA_NOPE, rk), lambda h: (h, 0, 0))],
        out_specs=pl.BlockSpec((1, rows, rk), lambda h: (h, 0, 0)),
        out_shape=jax.ShapeDtypeStruct((heads, rows, rk), BF16),
        compiler_params=_cparams(("parallel",)),
        name="mla_absorb_q",
    )(q, w_uk_t3)


def _attn_latent_kernel(ql_ref, q_ref, plat_ref, pkr_ref, lat_ref, kr_ref, o_ref, *, T, P):
    heads, _, rk = ql_ref.shape
    rows = heads * T
    nt = (((1,), (1,)), ((), ()))
    ql = ql_ref[...].reshape(rows, rk)
    qpe = q_ref[:, :, LANE:2 * LANE].reshape(rows, LANE)[:, 0:MLA_ROPE]
    lat_all = jnp.concatenate([plat_ref[0].astype(BF16), lat_ref[...].astype(BF16)], axis=0)
    kr_all = jnp.concatenate([pkr_ref[0], kr_ref[:, 0:MLA_ROPE]], axis=0).astype(BF16)
    s = (lax.dot_general(ql, lat_all, nt, preferred_element_type=F32)
         + lax.dot_general(qpe, kr_all, nt, preferred_element_type=F32))
    tok = lax.rem(lax.broadcasted_iota(jnp.int32, (rows, 1), 0), T)
    q_chunk = (P + tok) >> CHUNK_SHIFT
    k_chunk = lax.broadcasted_iota(jnp.int32, (1, P + T), 1) >> CHUNK_SHIFT
    s = jnp.where(q_chunk >= k_chunk, s, NEG_BIG)
    p = jnp.exp2(s - jnp.max(s, axis=-1, keepdims=True))
    o = jnp.dot(p.astype(BF16), lat_all, preferred_element_type=F32)
    o = o / jnp.sum(p, axis=-1, keepdims=True)
    o_ref[...] = o.reshape(heads, T, rk).astype(o_ref.dtype)


def _attn_latent(qlat, q, past_lat, past_kr, lat, kr, *, B, T):
    heads, _, rk = qlat.shape
    P = past_lat.shape[1]
    kern = functools.partial(_attn_latent_kernel, T=T, P=P)
    return pl.pallas_call(
        kern,
        grid=(B,),
        in_specs=[pl.BlockSpec((heads, T, rk), lambda b: (0, b, 0)),
                  pl.BlockSpec((heads, T, q.shape[2]), lambda b: (0, b, 0)),
                  pl.BlockSpec((1, P, rk), lambda b: (b, 0, 0)),
                  pl.BlockSpec((1, P, past_kr.shape[2]), lambda b: (b, 0, 0)),
                  pl.BlockSpec((T, rk), lambda b: (b, 0)),
                  pl.BlockSpec((T, LANE), lambda b: (b, 0))],
        out_specs=pl.BlockSpec((heads, T, rk), lambda b: (0, b, 0)),
        out_shape=jax.ShapeDtypeStruct((heads, B * T, rk), BF16),
        compiler_params=_cparams(("parallel",)),
        name="mla_attn_latent",
    )(qlat, q, past_lat, past_kr, lat, kr)


def _absorb_out_kernel(o_ref, w_ref, out_ref):
    out_ref[...] = jnp.dot(o_ref[0], w_ref[0], preferred_element_type=F32).astype(out_ref.dtype)


def _absorb_out(olat, w_uv3):
    heads, rows, rk = olat.shape
    dvh = w_uv3.shape[2]
    return pl.pallas_call(
        _absorb_out_kernel,
        grid=(heads,),
        in_specs=[pl.BlockSpec((1, rows, rk), lambda h: (h, 0, 0)),
                  pl.BlockSpec((1, rk, dvh), lambda h: (h, 0, 0))],
        out_specs=pl.BlockSpec((rows, dvh), lambda h: (0, h)),
        out_shape=jax.ShapeDtypeStruct((rows, heads * dvh), BF16),
        compiler_params=_cparams(("parallel",)),
        name="mla_absorb_out",
    )(olat, w_uv3)


def _merge_kernel(a_ref, gb_ref, om_ref, x_ref, wo_ref, gf_ref, x1_ref, h2_ref):
    merged = a_ref[...].astype(F32) + _sigmoid(gb_ref[...].astype(F32)) * om_ref[...].astype(F32)
    x1 = x_ref[...] + jnp.dot(merged.astype(BF16), wo_ref[...], preferred_element_type=F32)
    x1_ref[...] = x1
    h2_ref[...] = _rmsnorm(x1, gf_ref[...]).astype(BF16)


def _merge(branch_a, gates, o_m, x, wo, g_ffn, *, col):
    m, d = x.shape
    tm = _pick(m, (384, 256, 128))
    row = lambda i: (i, 0)
    return pl.pallas_call(
        _merge_kernel,
        grid=(m // tm,),
        in_specs=[pl.BlockSpec((tm, d), row),
                  pl.BlockSpec((tm, d), lambda i: (i, col["gb"] // d)),
                  pl.BlockSpec((tm, d), row),
                  pl.BlockSpec((tm, d), row),
                  pl.BlockSpec(wo.shape, lambda i: (0, 0)),
                  pl.BlockSpec((1, d), lambda i: (0, 0))],
        out_specs=[pl.BlockSpec((tm, d), row), pl.BlockSpec((tm, d), row)],
        out_shape=[jax.ShapeDtypeStruct((m, d), F32), jax.ShapeDtypeStruct((m, d), BF16)],
        compiler_params=_cparams(("parallel",)),
        name="merge_out_proj",
    )(branch_a, gates, o_m, x, wo, g_ffn.reshape(1, -1))


HALO = 8


def _ffn_up_kernel(h_ref, wa_ref, wb_ref, cwa_ref, cwb_ref, cba_ref, cbb_ref, ha_ref, hb_ref,
                   act_ref, ca_ref, cb_ref, ext_scr, carry_scr, w_scr,
                   *, bb, r, tf, loc, carried):
    s = pl.program_id(1)
    rt = pl.program_id(2)
    d = h_ref.shape[2]

    @pl.when((s == 0) & (rt == 0))
    def _():
        w_scr[0] = wa_ref[...].astype(BF16)
        w_scr[1] = wb_ref[...].astype(BF16)

    if carried:
        @pl.when(rt == 0)
        def _():
            carry_scr[0] = ha_ref[...]
            carry_scr[1] = hb_ref[...]

    h = h_ref[...].reshape(bb * r, d)
    conv = []
    for half, (cw_ref, cbias_ref, hist_ref, cout_ref) in enumerate(
            ((cwa_ref, cba_ref, ha_ref, ca_ref), (cwb_ref, cbb_ref, hb_ref, cb_ref))):
        u = jnp.dot(h, w_scr[half], preferred_element_type=F32).reshape(bb, r, tf)
        ext_scr[half, :, HALO:HALO + r, :] = u
        ext_scr[half, :, HALO - 2:HALO, :] = carry_scr[half] if carried else hist_ref[...]
        u1 = ext_scr[half, :, HALO - 1:HALO - 1 + r, :]
        u2 = ext_scr[half, :, HALO - 2:HALO - 2 + r, :]
        cw = cw_ref[...]
        conv.append(cbias_ref[...] + cw[0:1] * u2 + cw[1:2] * u1 + cw[2:3] * u)
        if carried:
            carry_scr[half] = ext_scr[half, :, HALO + r - 2:HALO + r, :]
        cout_ref[0] = ext_scr[half, :, HALO + loc:HALO + loc + 2, :]

    act_ref[...] = ((conv[0] * _sigmoid(conv[0])) * conv[1]).astype(act_ref.dtype)


def _ffn_down_kernel(act_ref, wd_ref, x1_ref, gf_ref, y_ref):
    k = pl.program_id(1)

    @pl.when(k == 0)
    def _():
        y_ref[...] = jnp.zeros(y_ref.shape, F32)

    y_ref[...] += jnp.dot(act_ref[...], wd_ref[...], preferred_element_type=F32)

    @pl.when(k == pl.num_programs(1) - 1)
    def _():
        y_ref[...] = _rmsnorm(x1_ref[...] + y_ref[...], gf_ref[...])


def _ffn(h2, x1, w_up, w_down, conv_w, conv_b, hist, g_final, *, B, T, Tp):
    d = h2.shape[1]
    dff = w_down.shape[0]
    tf = _pick(dff, (512, 256, 128))
    nf = dff // tf
    if Tp <= 128:
        bb, r = B, Tp
    else:
        bb, r = 1, _pick(Tp, (ROW_TILE, 128))
    nrt = Tp // r
    carried = nrt > 1
    loc = (T - 2) - (nrt - 1) * r
    assert 0 <= loc <= r - 2, "final two valid rows must sit in the last row tile"
    kern = functools.partial(_ffn_up_kernel, bb=bb, r=r, tf=tf, loc=loc, carried=carried)
    carry_shape = (2, bb, 2, tf) if carried else (1, 1, 2, LANE)
    act, ca, cb = pl.pallas_call(
        kern,
        grid=(nf, B // bb, nrt),
        in_specs=[pl.BlockSpec((bb, r, d), lambda f, s, t: (s, t, 0)),
                  pl.BlockSpec((d, tf), lambda f, s, t: (0, f)),
                  pl.BlockSpec((d, tf), lambda f, s, t: (0, nf + f)),
                  pl.BlockSpec((CONV_W, tf), lambda f, s, t: (0, f)),
                  pl.BlockSpec((CONV_W, tf), lambda f, s, t: (0, nf + f)),
                  pl.BlockSpec((1, tf), lambda f, s, t: (0, f)),
                  pl.BlockSpec((1, tf), lambda f, s, t: (0, nf + f)),
                  pl.BlockSpec((bb, 2, tf), lambda f, s, t: (s, 0, f)),
                  pl.BlockSpec((bb, 2, tf), lambda f, s, t: (s, 0, nf + f))],
        out_specs=[pl.BlockSpec((bb, r, tf), lambda f, s, t: (s, t, f)),
                   pl.BlockSpec((1, bb, 2, tf), lambda f, s, t: (t, s, 0, f)),
                   pl.BlockSpec((1, bb, 2, tf), lambda f, s, t: (t, s, 0, f))],
        out_shape=[jax.ShapeDtypeStruct((B, Tp, dff), BF16),
                   jax.ShapeDtypeStruct((nrt, B, 2, dff), F32),
                   jax.ShapeDtypeStruct((nrt, B, 2, dff), F32)],
        scratch_shapes=[pltpu.VMEM((2, bb, HALO + r, tf), F32),
                        pltpu.VMEM(carry_shape, F32),
                        pltpu.VMEM((2, d, tf), BF16)],
        compiler_params=_cparams(("arbitrary", "arbitrary", "arbitrary")),
        name="conv_ffn_up",
    )(h2.reshape(B, Tp, d), w_up, w_up, conv_w, conv_w, conv_b.reshape(1, -1),
      conv_b.reshape(1, -1), hist, hist)

    m = B * Tp
    tm = _pick(m, (512, 256, 128))
    kc = _pick(dff, (1408, 512, 256, 128))
    y = pl.pallas_call(
        _ffn_down_kernel,
        grid=(m // tm, dff // kc),
        in_specs=[pl.BlockSpec((tm, kc), lambda i, k: (i, k)),
                  pl.BlockSpec((kc, d), lambda i, k: (k, 0)),
                  pl.BlockSpec((tm, d), lambda i, k: (i, 0)),
                  pl.BlockSpec((1, d), lambda i, k: (0, 0))],
        out_specs=pl.BlockSpec((tm, d), lambda i, k: (i, 0)),
        out_shape=jax.ShapeDtypeStruct((m, d), F32),
        compiler_params=_cparams(("parallel", "arbitrary")),
        name="ffn_down",
    )(act.reshape(m, dff), w_down, x1, g_final.reshape(1, -1))
    return y.reshape(B, Tp, d), jnp.concatenate([ca[nrt - 1], cb[nrt - 1]], axis=-1)


def _rope_tables(pos):
    half = MLA_ROPE // 2
    inv = ROPE_THETA ** (-jnp.arange(0, MLA_ROPE, 2, dtype=F32) / MLA_ROPE)
    ang = pos.astype(F32)[:, None] * inv[None, :]
    cos, sin = jnp.cos(ang), jnp.sin(ang)
    zero = jnp.zeros((pos.shape[0], LANE - 2 * half), F32)
    return (jnp.concatenate([cos, cos, zero], axis=1),
            jnp.concatenate([-sin, sin, zero], axis=1))


def _stream(x, w, *, B, T, pos, q_off, s0, hist, past_lat=None, past_kr=None, prefix=None,
            emit_prefix=False):
    col = w["col"]
    dk, dv = w["dk"], w["dv"]
    h = _norm_cast(x, w["g_mix"])
    rows = w["in_rows"]
    qkvr = _matmul_wt(h, w["w_in_t"], rows["q"], rows["a"] - rows["q"], BF16, tn=1024)
    gates = _matmul_wt(h, w["w_in_t"], rows["ga"], rows["end"] - rows["ga"], BF16, tn=1024)
    small = _small_proj(h, w["w_in_t"], rows["a"], rank=rows["cq"] - rows["a"],
                        rq=rows["ckv"] - rows["cq"], rk=rows["kpe"] - rows["ckv"])

    branch_a, state = _gla(qkvr, gates, small, w["wa_pad"], w["b_a"], w["g_gla_out"], s0,
                           B=B, T=T, Tp=T, dk=dk, dv=dv, col=col)

    cos_t, sin_t = _rope_tables(pos)
    q = _qprep(small, w["g_q"], w["wq_nope"], w["wq_pe"], w["wq_pe_sw"], cos_t, sin_t, col=col)
    lat, kr = _lat(small, w["g_kv"], cos_t, sin_t, col=col)
    own_prefix = None
    if prefix is not None:
        assert B == 1 and past_lat is None
        k, vt = _kvup(lat, kr, w["w_uk"], w["w_uv_t"], v_transposed=True)
        o_m = _attention_t(q, k, vt, prefix[0], prefix[1], T=T, t=_pick(T, (1024, 128)),
                           hps=MLA_HEADS // 4)
    elif past_lat is not None:
        qlat = _absorb_q(q, w["w_uk_t3"])
        olat = _attn_latent(qlat, q, past_lat, past_kr, lat, kr, B=B, T=T)
        o_m = _absorb_out(olat, w["w_uv3"])
    else:
        k, v = _kvup(lat, kr, w["w_uk"], w["w_uv"])
        if emit_prefix:
            own_prefix = _kvup(lat, kr, w["w_uk"], w["w_uv_t"], v_transposed=True)
        o_m = _attention(q, k, v, B=B, Tq=T, Tk=T, tq=T, tk=T, hps=MLA_HEADS,
                         q_off=q_off, k_off=0)

    x1, h2 = _merge(branch_a, gates, o_m, x, w["w_o"], w["g_ffn"], col=col)
    y, conv = _ffn(h2, x1, w["w_up"], w["w_down"], w["conv_w"], w["conv_b"], hist,
                   w["final_norm"], B=B, T=T, Tp=T)
    return y, lat, kr, state, conv, own_prefix


def _prep_weights(g_mix, w_in, w_a2, b_a, g_gla_out, g_q, w_uq, g_kv, w_uk, w_uv, w_o,
                  g_ffn, w_up, conv_w, conv_b, w_down, final_norm):
    d = w_in.shape[0]
    rank, gqk = w_a2.shape
    gvw = GLA_HEADS * g_gla_out.shape[0]
    rq, rk = g_q.shape[0], g_kv.shape[0]
    half = MLA_ROPE // 2
    o, offs = 0, {}
    for name, width in (("q", gqk), ("k", gqk), ("v", gvw), ("r", gvw), ("a", rank),
                        ("cq", rq), ("ckv", rk), ("kpe", MLA_ROPE), ("ga", d), ("gb", d)):
        offs[name] = (o, o + width)
        o += width
    assert o == w_in.shape[1]
    in_rows = {name: lo for name, (lo, _) in offs.items()}
    in_rows["end"] = o
    assert all(v % 16 == 0 for v in in_rows.values())
    col = {"q": 0, "k": gqk, "v": 2 * gqk, "r": 2 * gqk + gvw, "ga": 0, "gb": d,
           "cq": 0, "ckv": rq, "kpe": rq + rk, "a": rq + rk + 2 * MLA_ROPE}

    w3 = w_uq.reshape(rq, MLA_HEADS, MLA_NOPE + MLA_ROPE)
    pe = w3[:, :, MLA_NOPE:]
    pe_sw = jnp.concatenate([pe[:, :, half:], pe[:, :, :half]], axis=2)
    zpad = jnp.zeros((rq, MLA_HEADS, LANE - MLA_ROPE), w_uq.dtype)
    flat = lambda t: t.reshape(rq, -1).astype(BF16)
    wa_pad = jnp.concatenate([w_a2, jnp.zeros((LANE - rank, gqk), w_a2.dtype)], axis=0)
    return dict(
        col=col, dk=gqk // GLA_HEADS, dv=g_gla_out.shape[0],
        g_mix=g_mix, w_in_t=jnp.swapaxes(w_in, 0, 1), in_rows=in_rows,
        wa_pad=wa_pad.astype(BF16), b_a=b_a, g_gla_out=g_gla_out, g_q=g_q,
        wq_nope=flat(w3[:, :, :MLA_NOPE]),
        wq_pe=flat(jnp.concatenate([pe, zpad], axis=2)),
        wq_pe_sw=flat(jnp.concatenate([pe_sw, zpad], axis=2)),
        g_kv=g_kv, w_uk=w_uk.astype(BF16), w_uv=w_uv.astype(BF16),
        w_uv_t=w_uv.T.astype(BF16),
        w_uk_t3=w_uk.reshape(rk, MLA_HEADS, MLA_NOPE).transpose(1, 2, 0).astype(BF16),
        w_uv3=w_uv.reshape(rk, MLA_HEADS, MLA_V).transpose(1, 0, 2).astype(BF16),
        w_o=w_o.astype(BF16),
        g_ffn=g_ffn, w_up=w_up, conv_w=conv_w, conv_b=conv_b,
        w_down=w_down.astype(BF16), final_norm=final_norm)


def kernel(x_prompt, x_sample, cache_mla_latent, cache_mla_krope, state_gla, cache_ffn_conv,
           meta_tokens, g_mix, w_in, w_a2, b_a, g_gla_out, g_q, w_uq, g_kv, w_uk, w_uv, w_o,
           g_ffn, w_up, conv_w, conv_b, w_down, final_norm):
    assert w_in.shape[0] == 1, "single trunk layer"
    bp, seq, d = x_prompt.shape
    assert bp == 1
    bs, ts, _ = x_sample.shape
    P = cache_mla_latent.shape[2]
    w = _prep_weights(g_mix[0], w_in[0], w_a2[0], b_a[0], g_gla_out[0], g_q[0], w_uq[0],
                      g_kv[0], w_uk[0], w_uv[0], w_o[0], g_ffn[0], w_up[0], conv_w[0],
                      conv_b[0], w_down[0], final_norm)
    dk, dv, dff2 = w["dk"], w["dv"], conv_w.shape[2]

    n_meta = meta_tokens.shape[0]
    assert n_meta == N_META and seq % CHUNK == 0
    _, lat_m, kr_m, st_m, cv_m, prefix = _stream(
        meta_tokens.astype(F32), w, B=1, T=n_meta, pos=jnp.arange(n_meta, dtype=jnp.int32),
        q_off=0, s0=jnp.zeros((1, GLA_HEADS, dk, dv), F32),
        hist=jnp.zeros((1, CONV_W - 1, dff2), F32), emit_prefix=True)
    yp, lat_p, kr_p, st_p, cv_p, _ = _stream(
        x_prompt[0], w, B=1, T=seq, pos=n_meta + jnp.arange(seq, dtype=jnp.int32),
        q_off=0, s0=st_m, hist=cv_m, prefix=prefix)

    pos_s = jnp.tile(P + jnp.arange(ts, dtype=jnp.int32), bs)
    ys, lat_s, kr_s, st_s, cv_s, _ = _stream(
        x_sample.reshape(bs * ts, d), w, B=bs, T=ts, pos=pos_s, q_off=P,
        past_lat=cache_mla_latent[0], past_kr=cache_mla_krope[0], s0=state_gla[0],
        hist=cache_ffn_conv[0])

    rk = lat_p.shape[1]
    T = n_meta + seq
    return (yp,
            ys,
            jnp.concatenate([lat_m, lat_p], axis=0).reshape(1, 1, T, rk),
            jnp.concatenate([kr_m, kr_p], axis=0)[:, :MLA_ROPE].reshape(1, 1, T, MLA_ROPE),
            st_p[None],
            cv_p[None],
            lat_s.reshape(1, bs, ts, rk),
            kr_s[:, :MLA_ROPE].reshape(1, bs, ts, MLA_ROPE),
            st_s[None],
            cv_s[None])
```

```python
import functools

import jax
import jax.numpy as jnp
from jax import lax
from jax.experimental import pallas as pl
from jax.experimental.pallas import tpu as pltpu

BF16 = jnp.bfloat16
F32 = jnp.float32

CHUNK = 64
CHUNK_SHIFT = 6
N_META = 16
EPS = 1e-6
GLA_HEADS = 4
GLA_GATE_NORM = 16.0
GLA_LOG_ALPHA_MIN = -5.0
MLA_HEADS = 16
MLA_NOPE = 128
MLA_ROPE = 64
MLA_V = 128
ROPE_THETA = 10000.0
CONV_W = 3
NEG_BIG = -1e30
LOG2E = 1.4426950408889634
QK_SCALE_LOG2E = (MLA_NOPE + MLA_ROPE) ** -0.5 * LOG2E

LANE = 128
VT_ONES = 16
ROW_TILE = 1024
VMEM_LIMIT = 56 * 1024 * 1024


def _cparams(sem, vmem=VMEM_LIMIT):
    return pltpu.CompilerParams(dimension_semantics=sem, vmem_limit_bytes=vmem)


def _rmsnorm(x, g):
    return x * lax.rsqrt(jnp.mean(x * x, axis=-1, keepdims=True) + EPS) * g


def _sigmoid(x):
    return 1.0 / (1.0 + jnp.exp(-x))


def _pick(n, cands):
    for c in cands:
        if n % c == 0:
            return c
    if n < min(cands):
        return n
    raise ValueError(f"no tile in {cands} divides {n}")


def _norm_cast_kernel(x_ref, g_ref, o_ref):
    o_ref[...] = _rmsnorm(x_ref[...], g_ref[...]).astype(o_ref.dtype)


def _norm_cast(x, g):
    m, d = x.shape
    tm = _pick(m, (384, 256, 128))
    return pl.pallas_call(
        _norm_cast_kernel,
        grid=(m // tm,),
        in_specs=[pl.BlockSpec((tm, d), lambda i: (i, 0)),
                  pl.BlockSpec((1, d), lambda i: (0, 0))],
        out_specs=pl.BlockSpec((tm, d), lambda i: (i, 0)),
        out_shape=jax.ShapeDtypeStruct((m, d), BF16),
        compiler_params=_cparams(("parallel",)),
        name="norm_cast",
    )(x, g.reshape(1, d))


def _matmul_kernel(a_ref, b_ref, o_ref):
    o_ref[...] = jnp.dot(a_ref[...], b_ref[...], preferred_element_type=F32).astype(o_ref.dtype)


def _matmul(a, b, out_dtype, tn):
    m, k = a.shape
    n = b.shape[1]
    tm = _pick(m, (ROW_TILE, 512, 384, 128))
    return pl.pallas_call(
        _matmul_kernel,
        grid=(n // tn, m // tm),
        in_specs=[pl.BlockSpec((tm, k), lambda j, i: (i, 0)),
                  pl.BlockSpec((k, tn), lambda j, i: (0, j))],
        out_specs=pl.BlockSpec((tm, tn), lambda j, i: (i, j)),
        out_shape=jax.ShapeDtypeStruct((m, n), out_dtype),
        compiler_params=_cparams(("parallel", "parallel")),
        name="in_proj",
    )(a, b)


_NT = (((1,), (1,)), ((), ()))


def _matmul_wt_kernel(a_ref, w_ref, o_ref, w_scr):
    @pl.when(pl.program_id(1) == 0)
    def _():
        w_scr[...] = w_ref[...].astype(BF16)

    o_ref[...] = lax.dot_general(a_ref[...], w_scr[...], _NT,
                                 preferred_element_type=F32).astype(o_ref.dtype)


def _matmul_wt(a, w_t, row0, n, out_dtype, tn):
    m, k = a.shape
    tm = _pick(m, (ROW_TILE, 512, 384, 128))
    return pl.pallas_call(
        _matmul_wt_kernel,
        grid=(n // tn, m // tm),
        in_specs=[pl.BlockSpec((tm, k), lambda j, i: (i, 0)),
                  pl.BlockSpec((pl.Element(tn), pl.Element(k)),
                               lambda j, i: (pl.multiple_of(row0 + j * tn, 16), 0))],
        out_specs=pl.BlockSpec((tm, tn), lambda j, i: (i, j)),
        out_shape=jax.ShapeDtypeStruct((m, n), out_dtype),
        scratch_shapes=[pltpu.VMEM((tn, k), BF16)],
        compiler_params=_cparams(("parallel", "arbitrary")),
        name="in_proj_wt",
    )(a, w_t)


def _small_proj_kernel(a_ref, w_ref, o_ref, w_scr, *, rank, rq, rk):
    @pl.when(pl.program_id(0) == 0)
    def _():
        w = w_ref[...].astype(BF16)
        half = MLA_ROPE // 2
        pe0 = rank + rq + rk
        o_pe = rq + rk
        w_scr[0:rq] = w[rank:rank + rq]
        w_scr[rq:o_pe] = w[rank + rq:pe0]
        w_scr[o_pe:o_pe + MLA_ROPE] = w[pe0:pe0 + MLA_ROPE]
        w_scr[o_pe + MLA_ROPE:o_pe + MLA_ROPE + half] = w[pe0 + half:pe0 + MLA_ROPE]
        w_scr[o_pe + MLA_ROPE + half:o_pe + 2 * MLA_ROPE] = w[pe0:pe0 + half]
        o_a = o_pe + 2 * MLA_ROPE
        w_scr[o_a:o_a + rank] = w[0:rank]
        w_scr[o_a + rank:] = jnp.zeros((w_scr.shape[0] - o_a - rank, w_scr.shape[1]), BF16)

    o_ref[...] = lax.dot_general(a_ref[...], w_scr[...], _NT, preferred_element_type=F32)


def _small_proj(a, w_t, row0, *, rank, rq, rk):
    m, k = a.shape
    n_in = rank + rq + rk + MLA_ROPE
    n_out = rq + rk + 2 * MLA_ROPE + LANE
    tm = _pick(m, (ROW_TILE, 512, 384, 128))
    kern = functools.partial(_small_proj_kernel, rank=rank, rq=rq, rk=rk)
    return pl.pallas_call(
        kern,
        grid=(m // tm,),
        in_specs=[pl.BlockSpec((tm, k), lambda i: (i, 0)),
                  pl.BlockSpec((pl.Element(n_in), pl.Element(k)), lambda i: (row0, 0))],
        out_specs=pl.BlockSpec((tm, n_out), lambda i: (i, 0)),
        out_shape=jax.ShapeDtypeStruct((m, n_out), F32),
        scratch_shapes=[pltpu.VMEM((n_out, k), BF16)],
        compiler_params=_cparams(("arbitrary",)),
        name="in_proj_small",
    )(a, w_t)


def _split3(x):
    a = x.astype(BF16)
    r1 = x - a.astype(F32)
    b = r1.astype(BF16)
    c = (r1 - b.astype(F32)).astype(BF16)
    return a, b, c


def _gla_kernel(q_ref, k_ref, v_ref, r_ref, ga_ref, a_ref, wa_ref, ba_ref, go_ref, s0_ref,
                o_ref, sout_ref, s_scr, *, C, SB, T, H, dk, dv):
    c_idx = pl.program_id(1)
    n_chunks = pl.num_programs(1)

    @pl.when(c_idx == 0)
    def _():
        s_scr[...] = s0_ref[0]

    z = jnp.dot(a_ref[...].astype(BF16), wa_ref[...], preferred_element_type=F32) + ba_ref[...]
    log_sig = jnp.minimum(z, 0.0) - jnp.log1p(jnp.exp(-jnp.abs(z)))
    la = jnp.maximum(log_sig * (1.0 / GLA_GATE_NORM), GLA_LOG_ALPHA_MIN)
    if T % C:
        rows = c_idx * C + lax.broadcasted_iota(jnp.int32, (C, 1), 0)
        la = jnp.where(rows < T, la, 0.0)

    ri = lax.broadcasted_iota(jnp.int32, (C, C), 0)
    ci = lax.broadcasted_iota(jnp.int32, (C, C), 1)
    tri = jnp.where(ri >= ci, 1.0, 0.0).astype(BF16)
    ones = jnp.ones((C, LANE), BF16)
    cs_all = jnp.zeros_like(la)
    dsum_all = jnp.zeros((la.shape[1], LANE), F32)
    for piece in _split3(la):
        cs_all = cs_all + jnp.dot(tri, piece, preferred_element_type=F32)
        dsum_all = dsum_all + lax.dot_general(piece, ones, (((0,), (0,)), ((), ())),
                                              preferred_element_type=F32)

    sr = lax.broadcasted_iota(jnp.int32, (SB, SB), 0)
    sc = lax.broadcasted_iota(jnp.int32, (SB, SB), 1)
    causal = sr >= sc
    nt = (((1,), (1,)), ((), ()))
    scale = dk ** -0.5

    for h in range(H):
        ksl = slice(h * dk, (h + 1) * dk)
        vsl = slice(h * dv, (h + 1) * dv)
        cs = cs_all[:, ksl]
        c_last = cs[C - 1:C, :]
        q = q_ref[:, ksl].astype(F32) * scale
        k = k_ref[:, ksl].astype(F32)
        v = v_ref[:, vsl]
        s_old = s_scr[h]

        o_inter = jnp.dot((q * jnp.exp(cs)).astype(BF16), s_old.astype(BF16),
                          preferred_element_type=F32)
        k_end = (k * jnp.exp(c_last - cs)).astype(BF16)
        upd = lax.dot_general(k_end, v, (((0,), (0,)), ((), ())), preferred_element_type=F32)
        dcol = jnp.exp(dsum_all[ksl, :])
        s_scr[h] = jnp.concatenate([dcol] * (dv // LANE), axis=1) * s_old + upd

        outs = []
        for i in range(C // SB):
            lo = i * SB
            cs_i = cs[lo:lo + SB]
            q_i = q[lo:lo + SB]
            k_i = k[lo:lo + SB]
            start = cs[lo - 1:lo] if i > 0 else jnp.zeros_like(c_last)
            mid = 0.5 * (start + cs[lo + SB - 1:lo + SB])
            qd = (q_i * jnp.exp(cs_i - mid)).astype(BF16)
            kd = (k_i * jnp.exp(mid - cs_i)).astype(BF16)
            att = lax.dot_general(qd, kd, nt, preferred_element_type=F32)
            att = jnp.where(causal, att, 0.0)
            o_i = jnp.dot(att.astype(BF16), v[lo:lo + SB], preferred_element_type=F32)
            if i > 0:
                qo = (q_i * jnp.exp(cs_i - start)).astype(BF16)
                ko = (k[:lo] * jnp.exp(start - cs[:lo])).astype(BF16)
                att_o = lax.dot_general(qo, ko, nt, preferred_element_type=F32)
                o_i = o_i + jnp.dot(att_o.astype(BF16), v[:lo], preferred_element_type=F32)
            outs.append(o_i)
        o = o_inter + (jnp.concatenate(outs, axis=0) if len(outs) > 1 else outs[0])

        on = _rmsnorm(o, go_ref[...])
        r = r_ref[:, vsl].astype(F32)
        g = ga_ref[:, vsl].astype(F32)
        o_ref[:, vsl] = (_sigmoid(g) * (on * (r * _sigmoid(r)))).astype(o_ref.dtype)

    @pl.when(c_idx == n_chunks - 1)
    def _():
        sout_ref[0] = s_scr[...]


def _gla(qkvr, gates, small, wa_pad, b_a, g_out, s0, *, B, T, Tp, dk, dv, col):
    C = min(128, Tp)
    SB = min(32, C)
    nc = Tp // C
    H = GLA_HEADS
    qk, vw = H * dk, H * dv
    rb = lambda b, c: b * nc + c
    kern = functools.partial(_gla_kernel, C=C, SB=SB, T=T, H=H, dk=dk, dv=dv)
    return pl.pallas_call(
        kern,
        grid=(B, nc),
        in_specs=[
            pl.BlockSpec((C, qk), lambda b, c: (rb(b, c), col["q"] // qk)),
            pl.BlockSpec((C, qk), lambda b, c: (rb(b, c), col["k"] // qk)),
            pl.BlockSpec((C, vw), lambda b, c: (rb(b, c), col["v"] // vw)),
            pl.BlockSpec((C, vw), lambda b, c: (rb(b, c), col["r"] // vw)),
            pl.BlockSpec((C, vw), lambda b, c: (rb(b, c), col["ga"] // vw)),
            pl.BlockSpec((C, LANE), lambda b, c: (rb(b, c), col["a"] // LANE)),
            pl.BlockSpec((LANE, qk), lambda b, c: (0, 0)),
            pl.BlockSpec((1, qk), lambda b, c: (0, 0)),
            pl.BlockSpec((1, dv), lambda b, c: (0, 0)),
            pl.BlockSpec((1, H, dk, dv), lambda b, c: (b, 0, 0, 0)),
        ],
        out_specs=[
            pl.BlockSpec((C, vw), lambda b, c: (rb(b, c), 0)),
            pl.BlockSpec((1, H, dk, dv), lambda b, c: (b, 0, 0, 0)),
        ],
        out_shape=[jax.ShapeDtypeStruct((B * Tp, vw), BF16),
                   jax.ShapeDtypeStruct((B, H, dk, dv), F32)],
        scratch_shapes=[pltpu.VMEM((H, dk, dv), F32)],
        compiler_params=_cparams(("parallel", "arbitrary")),
        name="gla",
    )(qkvr, qkvr, qkvr, qkvr, gates, small, wa_pad, b_a.reshape(1, -1), g_out.reshape(1, -1), s0)


def _qprep_kernel(cq_ref, gq_ref, wn_ref, wp_ref, wps_ref, cos_ref, sin_ref, q_ref):
    hq = _rmsnorm(cq_ref[...], gq_ref[...]).astype(BF16)
    qn = jnp.dot(hq, wn_ref[...], preferred_element_type=F32)
    qp = jnp.dot(hq, wp_ref[...], preferred_element_type=F32)
    qs = jnp.dot(hq, wps_ref[...], preferred_element_type=F32)
    cos = cos_ref[...] * QK_SCALE_LOG2E
    sin = sin_ref[...] * QK_SCALE_LOG2E
    for h in range(MLA_HEADS):
        sl = slice(h * LANE, (h + 1) * LANE)
        q_ref[h, :, 0:LANE] = (qn[:, sl] * QK_SCALE_LOG2E).astype(BF16)
        q_ref[h, :, LANE:2 * LANE] = (qp[:, sl] * cos + qs[:, sl] * sin).astype(BF16)


def _qprep(small, g_q, wn, wp, wps, cos_t, sin_t, *, col):
    m = small.shape[0]
    rq = wn.shape[0]
    tm = _pick(m, (256, 128))
    full = lambda i: (0, 0)
    return pl.pallas_call(
        _qprep_kernel,
        grid=(m // tm,),
        in_specs=[pl.BlockSpec((tm, rq), lambda i: (i, col["cq"] // rq)),
                  pl.BlockSpec((1, rq), full),
                  pl.BlockSpec(wn.shape, full),
                  pl.BlockSpec(wp.shape, full),
                  pl.BlockSpec(wps.shape, full),
                  pl.BlockSpec((tm, LANE), lambda i: (i, 0)),
                  pl.BlockSpec((tm, LANE), lambda i: (i, 0))],
        out_specs=pl.BlockSpec((MLA_HEADS, tm, 2 * LANE), lambda i: (0, i, 0)),
        out_shape=jax.ShapeDtypeStruct((MLA_HEADS, m, 2 * LANE), BF16),
        compiler_params=_cparams(("parallel",)),
        name="mla_q",
    )(small, g_q.reshape(1, -1), wn, wp, wps, cos_t, sin_t)


def _lat_kernel(ckv_ref, kpe_ref, gkv_ref, cos_ref, sin_ref, lat_ref, kr_ref):
    lat_ref[...] = _rmsnorm(ckv_ref[...], gkv_ref[...])
    blk = kpe_ref[...]
    kr_ref[...] = blk * cos_ref[...] + pltpu.roll(blk, LANE // 2, 1) * sin_ref[...]


def _lat(small, g_kv, cos_t, sin_t, *, col):
    m = small.shape[0]
    rk = g_kv.shape[0]
    tm = _pick(m, (256, 128))
    return pl.pallas_call(
        _lat_kernel,
        grid=(m // tm,),
        in_specs=[pl.BlockSpec((tm, rk), lambda i: (i, col["ckv"] // rk)),
                  pl.BlockSpec((tm, LANE), lambda i: (i, col["kpe"] // LANE)),
                  pl.BlockSpec((1, rk), lambda i: (0, 0)),
                  pl.BlockSpec((tm, LANE), lambda i: (i, 0)),
                  pl.BlockSpec((tm, LANE), lambda i: (i, 0))],
        out_specs=[pl.BlockSpec((tm, rk), lambda i: (i, 0)),
                   pl.BlockSpec((tm, LANE), lambda i: (i, 0))],
        out_shape=[jax.ShapeDtypeStruct((m, rk), F32),
                   jax.ShapeDtypeStruct((m, LANE), F32)],
        compiler_params=_cparams(("parallel",)),
        name="mla_latent",
    )(small, small, g_kv.reshape(1, -1), cos_t, sin_t)


def _kvup_kernel(lat_ref, kr_ref, wuk_ref, wuv_ref, k_ref, v_ref, *, v_transposed):
    lat = lat_ref[...].astype(BF16)
    kn = jnp.dot(lat, wuk_ref[...], preferred_element_type=F32)
    kp = kr_ref[...].astype(BF16)
    if v_transposed:
        vv = lax.dot_general(wuv_ref[...], lat, (((1,), (1,)), ((), ())),
                             preferred_element_type=F32)
    else:
        vv = jnp.dot(lat, wuv_ref[...], preferred_element_type=F32)
    for h in range(MLA_HEADS):
        sl = slice(h * LANE, (h + 1) * LANE)
        k_ref[h, :, 0:LANE] = kn[:, sl].astype(BF16)
        k_ref[h, :, LANE:2 * LANE] = kp
        if v_transposed:
            v_ref[h, 0:LANE, :] = vv[sl, :].astype(BF16)
            v_ref[h, LANE:LANE + VT_ONES, :] = jnp.ones((VT_ONES, vv.shape[1]), BF16)
        else:
            v_ref[h] = vv[:, sl].astype(BF16)


def _kvup(lat, kr, wuk, wuv, *, v_transposed=False):
    m, rk = lat.shape
    tm = _pick(m, (512, 256, 128))
    full = lambda i: (0, 0)
    if v_transposed:
        v_spec = pl.BlockSpec((MLA_HEADS, LANE + VT_ONES, tm), lambda i: (0, 0, i))
        v_shape = (MLA_HEADS, LANE + VT_ONES, m)
    else:
        v_spec = pl.BlockSpec((MLA_HEADS, tm, LANE), lambda i: (0, i, 0))
        v_shape = (MLA_HEADS, m, LANE)
    return pl.pallas_call(
        functools.partial(_kvup_kernel, v_transposed=v_transposed),
        grid=(m // tm,),
        in_specs=[pl.BlockSpec((tm, rk), lambda i: (i, 0)),
                  pl.BlockSpec((tm, LANE), lambda i: (i, 0)),
                  pl.BlockSpec(wuk.shape, full),
                  pl.BlockSpec(wuv.shape, full)],
        out_specs=[pl.BlockSpec((MLA_HEADS, tm, 2 * LANE), lambda i: (0, i, 0)), v_spec],
        out_shape=[jax.ShapeDtypeStruct((MLA_HEADS, m, 2 * LANE), BF16),
                   jax.ShapeDtypeStruct(v_shape, BF16)],
        compiler_params=_cparams(("parallel",)),
        name="mla_kv",
    )(lat, kr, wuk, wuv)


def _last_kblock(qi, *, tq, tk, nk, q_off, k_off):
    top_chunk = ((qi + 1) * tq - 1 + q_off) // CHUNK
    last_key = (top_chunk + 1) * CHUNK - 1 - k_off
    return jnp.minimum(last_key // tk, nk - 1)


def _attn_kernel(q_ref, k_ref, v_ref, o_ref, m_scr, l_scr, acc_scr, *, hps, tq, tk, nk,
                 q_off, k_off):
    qi = pl.program_id(2)
    ki = pl.program_id(3)

    @pl.when(ki == 0)
    def _():
        m_scr[...] = jnp.full(m_scr.shape, NEG_BIG, F32)
        l_scr[...] = jnp.zeros(l_scr.shape, F32)
        acc_scr[...] = jnp.zeros(acc_scr.shape, F32)

    @pl.when(ki <= _last_kblock(qi, tq=tq, tk=tk, nk=nk, q_off=q_off, k_off=k_off))
    def _():
        q_chunk = (qi * tq + q_off + lax.broadcasted_iota(jnp.int32, (tq, 1), 0)) >> CHUNK_SHIFT
        k_chunk = (ki * tk + k_off + lax.broadcasted_iota(jnp.int32, (1, tk), 1)) >> CHUNK_SHIFT
        visible = q_chunk >= k_chunk

        def head(h, carry):
            s = lax.dot_general(q_ref[h], k_ref[h], (((1,), (1,)), ((), ())),
                                preferred_element_type=F32)
            s = jnp.where(visible, s, NEG_BIG)
            m_prev = m_scr[h]
            m_new = jnp.maximum(m_prev, jnp.max(s, axis=-1, keepdims=True))
            p = jnp.exp2(s - m_new)
            alpha = jnp.exp2(m_prev - m_new)
            l_scr[h] = alpha * l_scr[h] + jnp.sum(p, axis=-1, keepdims=True)
            acc_scr[h] = alpha * acc_scr[h] + jnp.dot(p.astype(BF16), v_ref[h],
                                                      preferred_element_type=F32)
            m_scr[h] = m_new
            return carry

        lax.fori_loop(0, hps, head, 0)

    @pl.when(ki == nk - 1)
    def _():
        for h in range(hps):
            o_ref[:, h * LANE:(h + 1) * LANE] = (acc_scr[h] / l_scr[h]).astype(o_ref.dtype)


def _attention(q, k, v, *, B, Tq, Tk, tq, tk, hps, q_off, k_off):
    nq = Tq // tq
    nk = Tk // tk
    hg = MLA_HEADS // hps
    dqk = q.shape[2]
    dvh = v.shape[2]
    last = functools.partial(_last_kblock, tq=tq, tk=tk, nk=nk, q_off=q_off, k_off=k_off)
    kern = functools.partial(_attn_kernel, hps=hps, tq=tq, tk=tk, nk=nk, q_off=q_off,
                             k_off=k_off)
    kv_row = lambda b, g, i, j: b * nk + jnp.minimum(j, last(i))
    return pl.pallas_call(
        kern,
        grid=(B, hg, nq, nk),
        in_specs=[pl.BlockSpec((hps, tq, dqk), lambda b, g, i, j: (g, b * nq + i, 0)),
                  pl.BlockSpec((hps, tk, dqk), lambda b, g, i, j: (g, kv_row(b, g, i, j), 0)),
                  pl.BlockSpec((hps, tk, dvh), lambda b, g, i, j: (g, kv_row(b, g, i, j), 0))],
        out_specs=pl.BlockSpec((tq, hps * dvh), lambda b, g, i, j: (b * nq + i, g)),
        out_shape=jax.ShapeDtypeStruct((B * Tq, MLA_HEADS * dvh), BF16),
        scratch_shapes=[pltpu.VMEM((hps, tq, 1), F32),
                        pltpu.VMEM((hps, tq, 1), F32),
                        pltpu.VMEM((hps, tq, dvh), F32)],
        compiler_params=_cparams(("parallel", "parallel", "parallel", "arbitrary")),
        name="mla_attn",
    )(q, k, v)


def _attn_t_kernel(qi_ref, ki_ref, q_ref, k_ref, vt_ref, kp_ref, vtp_ref, o_ref, m_scr, acc_scr,
                   *, hps, t):
    pair = pl.program_id(1)
    qi = qi_ref[pair]
    ki = ki_ref[pair]
    nt = (((1,), (1,)), ((), ()))

    def scores(k_blk, h):
        return lax.dot_general(k_blk, q_ref[h], nt, preferred_element_type=F32)

    @pl.when(ki == 0)
    def _():
        for h in range(hps):
            s = scores(kp_ref[h], h)
            m = jnp.max(s, axis=0, keepdims=True)
            p = jnp.exp2((s - m).astype(BF16))
            m_scr[h] = m
            acc_scr[h] = jnp.dot(vtp_ref[h], p, preferred_element_type=F32)

    def step(masked):
        if masked:
            k_chunk = (ki * t + lax.broadcasted_iota(jnp.int32, (t, 1), 0)) >> CHUNK_SHIFT
            q_chunk = (qi * t + lax.broadcasted_iota(jnp.int32, (1, t), 1)) >> CHUNK_SHIFT
            bias = jnp.where(q_chunk >= k_chunk, 0.0, NEG_BIG)
        s_next = scores(k_ref[0], 0)
        for h in range(hps):
            s = s_next
            if h + 1 < hps:
                s_next = scores(k_ref[h + 1], h + 1)
            if masked:
                s = s + bias
            m_prev = m_scr[h]
            m_new = jnp.maximum(m_prev, jnp.max(s, axis=0, keepdims=True))
            p = jnp.exp2((s - m_new).astype(BF16))
            alpha = jnp.exp2(m_prev - m_new)
            acc_scr[h] = alpha * acc_scr[h] + jnp.dot(vt_ref[h], p, preferred_element_type=F32)
            m_scr[h] = m_new

    @pl.when(ki < qi)
    def _():
        step(False)

    @pl.when(ki == qi)
    def _():
        step(True)
        for h in range(hps):
            acc = acc_scr[h]
            o_t = acc[0:LANE] / acc[LANE:LANE + 1]
            o_ref[:, h * LANE:(h + 1) * LANE] = o_t.T.astype(o_ref.dtype)


def _attention_t(q, k, vt, k_pre, vt_pre, *, T, t, hps):
    n = T // t
    hg = MLA_HEADS // hps
    dqk = q.shape[2]
    npre = k_pre.shape[1]
    vrows = vt.shape[1]
    pairs = [(i, j) for i in range(n) for j in range(i + 1)]
    qi_arr = jnp.asarray([p[0] for p in pairs], jnp.int32)
    ki_arr = jnp.asarray([p[1] for p in pairs], jnp.int32)
    kern = functools.partial(_attn_t_kernel, hps=hps, t=t)
    grid_spec = pltpu.PrefetchScalarGridSpec(
        num_scalar_prefetch=2,
        grid=(hg, len(pairs)),
        in_specs=[pl.BlockSpec((hps, t, dqk), lambda g, p, qi, ki: (g, qi[p], 0)),
                  pl.BlockSpec((hps, t, dqk), lambda g, p, qi, ki: (g, ki[p], 0)),
                  pl.BlockSpec((hps, vrows, t), lambda g, p, qi, ki: (g, 0, ki[p])),
                  pl.BlockSpec((hps, npre, dqk), lambda g, p, qi, ki: (g, 0, 0)),
                  pl.BlockSpec((hps, vrows, npre), lambda g, p, qi, ki: (g, 0, 0))],
        out_specs=pl.BlockSpec((t, hps * LANE), lambda g, p, qi, ki: (qi[p], g)),
        scratch_shapes=[pltpu.VMEM((hps, 1, t), F32),
                        pltpu.VMEM((hps, vrows, t), F32)])
    return pl.pallas_call(
        kern,
        grid_spec=grid_spec,
        out_shape=jax.ShapeDtypeStruct((T, MLA_HEADS * LANE), BF16),
        compiler_params=_cparams(("parallel", "arbitrary")),
        name="mla_attn_t",
    )(qi_arr, ki_arr, q, k, vt, k_pre, vt_pre)


def _absorb_q_kernel(q_ref, w_ref, o_ref):
    o_ref[0] = jnp.dot(q_ref[0, :, 0:MLA_NOPE], w_ref[0],
                       preferred_element_type=F32).astype(o_ref.dtype)


def _absorb_q(q, w_uk_t3):
    heads, rows, dqk = q.shape
    rk = w_uk_t3.shape[2]
    return pl.pallas_call(
        _absorb_q_kernel,
        grid=(heads,),
        in_specs=[pl.BlockSpec((1, rows, dqk), lambda h: (h, 0, 0)),
                  pl.BlockSpec((1, MLA_NOPE, rk), lambda h: (h, 0, 0))],
        out_specs=pl.BlockSpec((1, rows, rk), lambda h: (h, 0, 0)),
        out_shape=jax.ShapeDtypeStruct((heads, rows, rk), BF16),
        compiler_params=_cparams(("parallel",)),
        name="mla_absorb_q",
    )(q, w_uk_t3)


def _attn_latent_kernel(ql_ref, q_ref, plat_ref, pkr_ref, lat_ref, kr_ref, o_ref, *, T, P):
    heads, _, rk = ql_ref.shape
    rows = heads * T
    nt = (((1,), (1,)), ((), ()))
    ql = ql_ref[...].reshape(rows, rk)
    qpe = q_ref[:, :, LANE:2 * LANE].reshape(rows, LANE)[:, 0:MLA_ROPE]
    lat_all = jnp.concatenate([plat_ref[0].astype(BF16), lat_ref[...].astype(BF16)], axis=0)
    kr_all = jnp.concatenate([pkr_ref[0], kr_ref[:, 0:MLA_ROPE]], axis=0).astype(BF16)
    s = (lax.dot_general(ql, lat_all, nt, preferred_element_type=F32)
         + lax.dot_general(qpe, kr_all, nt, preferred_element_type=F32))
    tok = lax.rem(lax.broadcasted_iota(jnp.int32, (rows, 1), 0), T)
    q_chunk = (P + tok) >> CHUNK_SHIFT
    k_chunk = lax.broadcasted_iota(jnp.int32, (1, P + T), 1) >> CHUNK_SHIFT
    s = jnp.where(q_chunk >= k_chunk, s, NEG_BIG)
    p = jnp.exp2(s - jnp.max(s, axis=-1, keepdims=True))
    o = jnp.dot(p.astype(BF16), lat_all, preferred_element_type=F32)
    o = o / jnp.sum(p, axis=-1, keepdims=True)
    o_ref[...] = o.reshape(heads, T, rk).astype(o_ref.dtype)


def _attn_latent(qlat, q, past_lat, past_kr, lat, kr, *, B, T):
    heads, _, rk = qlat.shape
    P = past_lat.shape[1]
    kern = functools.partial(_attn_latent_kernel, T=T, P=P)
    return pl.pallas_call(
        kern,
        grid=(B,),
        in_specs=[pl.BlockSpec((heads, T, rk), lambda b: (0, b, 0)),
                  pl.BlockSpec((heads, T, q.shape[2]), lambda b: (0, b, 0)),
                  pl.BlockSpec((1, P, rk), lambda b: (b, 0, 0)),
                  pl.BlockSpec((1, P, past_kr.shape[2]), lambda b: (b, 0, 0)),
                  pl.BlockSpec((T, rk), lambda b: (b, 0)),
                  pl.BlockSpec((T, LANE), lambda b: (b, 0))],
        out_specs=pl.BlockSpec((heads, T, rk), lambda b: (0, b, 0)),
        out_shape=jax.ShapeDtypeStruct((heads, B * T, rk), BF16),
        compiler_params=_cparams(("parallel",)),
        name="mla_attn_latent",
    )(qlat, q, past_lat, past_kr, lat, kr)


def _absorb_out_kernel(o_ref, w_ref, out_ref):
    out_ref[...] = jnp.dot(o_ref[0], w_ref[0], preferred_element_type=F32).astype(out_ref.dtype)


def _absorb_out(olat, w_uv3):
    heads, rows, rk = olat.shape
    dvh = w_uv3.shape[2]
    return pl.pallas_call(
        _absorb_out_kernel,
        grid=(heads,),
        in_specs=[pl.BlockSpec((1, rows, rk), lambda h: (h, 0, 0)),
                  pl.BlockSpec((1, rk, dvh), lambda h: (h, 0, 0))],
        out_specs=pl.BlockSpec((rows, dvh), lambda h: (0, h)),
        out_shape=jax.ShapeDtypeStruct((rows, heads * dvh), BF16),
        compiler_params=_cparams(("parallel",)),
        name="mla_absorb_out",
    )(olat, w_uv3)


def _merge_kernel(a_ref, gb_ref, om_ref, x_ref, wo_ref, gf_ref, x1_ref, h2_ref):
    merged = a_ref[...].astype(F32) + _sigmoid(gb_ref[...].astype(F32)) * om_ref[...].astype(F32)
    x1 = x_ref[...] + jnp.dot(merged.astype(BF16), wo_ref[...], preferred_element_type=F32)
    x1_ref[...] = x1
    h2_ref[...] = _rmsnorm(x1, gf_ref[...]).astype(BF16)


def _merge(branch_a, gates, o_m, x, wo, g_ffn, *, col):
    m, d = x.shape
    tm = _pick(m, (384, 256, 128))
    row = lambda i: (i, 0)
    return pl.pallas_call(
        _merge_kernel,
        grid=(m // tm,),
        in_specs=[pl.BlockSpec((tm, d), row),
                  pl.BlockSpec((tm, d), lambda i: (i, col["gb"] // d)),
                  pl.BlockSpec((tm, d), row),
                  pl.BlockSpec((tm, d), row),
                  pl.BlockSpec(wo.shape, lambda i: (0, 0)),
                  pl.BlockSpec((1, d), lambda i: (0, 0))],
        out_specs=[pl.BlockSpec((tm, d), row), pl.BlockSpec((tm, d), row)],
        out_shape=[jax.ShapeDtypeStruct((m, d), F32), jax.ShapeDtypeStruct((m, d), BF16)],
        compiler_params=_cparams(("parallel",)),
        name="merge_out_proj",
    )(branch_a, gates, o_m, x, wo, g_ffn.reshape(1, -1))


HALO = 8


def _ffn_up_kernel(h_ref, wa_ref, wb_ref, cwa_ref, cwb_ref, cba_ref, cbb_ref, ha_ref, hb_ref,
                   act_ref, ca_ref, cb_ref, ext_scr, carry_scr, w_scr,
                   *, bb, r, tf, loc, carried):
    s = pl.program_id(1)
    rt = pl.program_id(2)
    d = h_ref.shape[2]

    @pl.when((s == 0) & (rt == 0))
    def _():
        w_scr[0] = wa_ref[...].astype(BF16)
        w_scr[1] = wb_ref[...].astype(BF16)

    if carried:
        @pl.when(rt == 0)
        def _():
            carry_scr[0] = ha_ref[...]
            carry_scr[1] = hb_ref[...]

    h = h_ref[...].reshape(bb * r, d)
    conv = []
    for half, (cw_ref, cbias_ref, hist_ref, cout_ref) in enumerate(
            ((cwa_ref, cba_ref, ha_ref, ca_ref), (cwb_ref, cbb_ref, hb_ref, cb_ref))):
        u = jnp.dot(h, w_scr[half], preferred_element_type=F32).reshape(bb, r, tf)
        ext_scr[half, :, HALO:HALO + r, :] = u
        ext_scr[half, :, HALO - 2:HALO, :] = carry_scr[half] if carried else hist_ref[...]
        u1 = ext_scr[half, :, HALO - 1:HALO - 1 + r, :]
        u2 = ext_scr[half, :, HALO - 2:HALO - 2 + r, :]
        cw = cw_ref[...]
        conv.append(cbias_ref[...] + cw[0:1] * u2 + cw[1:2] * u1 + cw[2:3] * u)
        if carried:
            carry_scr[half] = ext_scr[half, :, HALO + r - 2:HALO + r, :]
        cout_ref[0] = ext_scr[half, :, HALO + loc:HALO + loc + 2, :]

    act_ref[...] = ((conv[0] * _sigmoid(conv[0])) * conv[1]).astype(act_ref.dtype)


def _ffn_down_kernel(act_ref, wd_ref, x1_ref, gf_ref, y_ref):
    k = pl.program_id(1)

    @pl.when(k == 0)
    def _():
        y_ref[...] = jnp.zeros(y_ref.shape, F32)

    y_ref[...] += jnp.dot(act_ref[...], wd_ref[...], preferred_element_type=F32)

    @pl.when(k == pl.num_programs(1) - 1)
    def _():
        y_ref[...] = _rmsnorm(x1_ref[...] + y_ref[...], gf_ref[...])


def _ffn(h2, x1, w_up, w_down, conv_w, conv_b, hist, g_final, *, B, T, Tp):
    d = h2.shape[1]
    dff = w_down.shape[0]
    tf = _pick(dff, (512, 256, 128))
    nf = dff // tf
    if Tp <= 128:
        bb, r = B, Tp
    else:
        bb, r = 1, _pick(Tp, (ROW_TILE, 128))
    nrt = Tp // r
    carried = nrt > 1
    loc = (T - 2) - (nrt - 1) * r
    assert 0 <= loc <= r - 2, "final two valid rows must sit in the last row tile"
    kern = functools.partial(_ffn_up_kernel, bb=bb, r=r, tf=tf, loc=loc, carried=carried)
    carry_shape = (2, bb, 2, tf) if carried else (1, 1, 2, LANE)
    act, ca, cb = pl.pallas_call(
        kern,
        grid=(nf, B // bb, nrt),
        in_specs=[pl.BlockSpec((bb, r, d), lambda f, s, t: (s, t, 0)),
                  pl.BlockSpec((d, tf), lambda f, s, t: (0, f)),
                  pl.BlockSpec((d, tf), lambda f, s, t: (0, nf + f)),
                  pl.BlockSpec((CONV_W, tf), lambda f, s, t: (0, f)),
                  pl.BlockSpec((CONV_W, tf), lambda f, s, t: (0, nf + f)),
                  pl.BlockSpec((1, tf), lambda f, s, t: (0, f)),
                  pl.BlockSpec((1, tf), lambda f, s, t: (0, nf + f)),
                  pl.BlockSpec((bb, 2, tf), lambda f, s, t: (s, 0, f)),
                  pl.BlockSpec((bb, 2, tf), lambda f, s, t: (s, 0, nf + f))],
        out_specs=[pl.BlockSpec((bb, r, tf), lambda f, s, t: (s, t, f)),
                   pl.BlockSpec((1, bb, 2, tf), lambda f, s, t: (t, s, 0, f)),
                   pl.BlockSpec((1, bb, 2, tf), lambda f, s, t: (t, s, 0, f))],
        out_shape=[jax.ShapeDtypeStruct((B, Tp, dff), BF16),
                   jax.ShapeDtypeStruct((nrt, B, 2, dff), F32),
                   jax.ShapeDtypeStruct((nrt, B, 2, dff), F32)],
        scratch_shapes=[pltpu.VMEM((2, bb, HALO + r, tf), F32),
                        pltpu.VMEM(carry_shape, F32),
                        pltpu.VMEM((2, d, tf), BF16)],
        compiler_params=_cparams(("arbitrary", "arbitrary", "arbitrary")),
        name="conv_ffn_up",
    )(h2.reshape(B, Tp, d), w_up, w_up, conv_w, conv_w, conv_b.reshape(1, -1),
      conv_b.reshape(1, -1), hist, hist)

    m = B * Tp
    tm = _pick(m, (512, 256, 128))
    kc = _pick(dff, (1408, 512, 256, 128))
    y = pl.pallas_call(
        _ffn_down_kernel,
        grid=(m // tm, dff // kc),
        in_specs=[pl.BlockSpec((tm, kc), lambda i, k: (i, k)),
                  pl.BlockSpec((kc, d), lambda i, k: (k, 0)),
                  pl.BlockSpec((tm, d), lambda i, k: (i, 0)),
                  pl.BlockSpec((1, d), lambda i, k: (0, 0))],
        out_specs=pl.BlockSpec((tm, d), lambda i, k: (i, 0)),
        out_shape=jax.ShapeDtypeStruct((m, d), F32),
        compiler_params=_cparams(("parallel", "arbitrary")),
        name="ffn_down",
    )(act.reshape(m, dff), w_down, x1, g_final.reshape(1, -1))
    return y.reshape(B, Tp, d), jnp.concatenate([ca[nrt - 1], cb[nrt - 1]], axis=-1)


def _rope_tables(pos):
    half = MLA_ROPE // 2
    inv = ROPE_THETA ** (-jnp.arange(0, MLA_ROPE, 2, dtype=F32) / MLA_ROPE)
    ang = pos.astype(F32)[:, None] * inv[None, :]
    cos, sin = jnp.cos(ang), jnp.sin(ang)
    zero = jnp.zeros((pos.shape[0], LANE - 2 * half), F32)
    return (jnp.concatenate([cos, cos, zero], axis=1),
            jnp.concatenate([-sin, sin, zero], axis=1))


def _stream(x, w, *, B, T, pos, q_off, s0, hist, past_lat=None, past_kr=None, prefix=None,
            emit_prefix=False):
    col = w["col"]
    dk, dv = w["dk"], w["dv"]
    h = _norm_cast(x, w["g_mix"])
    rows = w["in_rows"]
    qkvr = _matmul_wt(h, w["w_in_t"], rows["q"], rows["a"] - rows["q"], BF16, tn=1024)
    gates = _matmul_wt(h, w["w_in_t"], rows["ga"], rows["end"] - rows["ga"], BF16, tn=1024)
    small = _small_proj(h, w["w_in_t"], rows["a"], rank=rows["cq"] - rows["a"],
                        rq=rows["ckv"] - rows["cq"], rk=rows["kpe"] - rows["ckv"])

    branch_a, state = _gla(qkvr, gates, small, w["wa_pad"], w["b_a"], w["g_gla_out"], s0,
                           B=B, T=T, Tp=T, dk=dk, dv=dv, col=col)

    cos_t, sin_t = _rope_tables(pos)
    q = _qprep(small, w["g_q"], w["wq_nope"], w["wq_pe"], w["wq_pe_sw"], cos_t, sin_t, col=col)
    lat, kr = _lat(small, w["g_kv"], cos_t, sin_t, col=col)
    own_prefix = None
    if prefix is not None:
        assert B == 1 and past_lat is None
        k, vt = _kvup(lat, kr, w["w_uk"], w["w_uv_t"], v_transposed=True)
        o_m = _attention_t(q, k, vt, prefix[0], prefix[1], T=T, t=_pick(T, (1024, 128)),
                           hps=MLA_HEADS // 4)
    elif past_lat is not None:
        qlat = _absorb_q(q, w["w_uk_t3"])
        olat = _attn_latent(qlat, q, past_lat, past_kr, lat, kr, B=B, T=T)
        o_m = _absorb_out(olat, w["w_uv3"])
    else:
        k, v = _kvup(lat, kr, w["w_uk"], w["w_uv"])
        if emit_prefix:
            own_prefix = _kvup(lat, kr, w["w_uk"], w["w_uv_t"], v_transposed=True)
        o_m = _attention(q, k, v, B=B, Tq=T, Tk=T, tq=T, tk=T, hps=MLA_HEADS,
                         q_off=q_off, k_off=0)

    x1, h2 = _merge(branch_a, gates, o_m, x, w["w_o"], w["g_ffn"], col=col)
    y, conv = _ffn(h2, x1, w["w_up"], w["w_down"], w["conv_w"], w["conv_b"], hist,
                   w["final_norm"], B=B, T=T, Tp=T)
    return y, lat, kr, state, conv, own_prefix


def _prep_weights(g_mix, w_in, w_a2, b_a, g_gla_out, g_q, w_uq, g_kv, w_uk, w_uv, w_o,
                  g_ffn, w_up, conv_w, conv_b, w_down, final_norm):
    d = w_in.shape[0]
    rank, gqk = w_a2.shape
    gvw = GLA_HEADS * g_gla_out.shape[0]
    rq, rk = g_q.shape[0], g_kv.shape[0]
    half = MLA_ROPE // 2
    o, offs = 0, {}
    for name, width in (("q", gqk), ("k", gqk), ("v", gvw), ("r", gvw), ("a", rank),
                        ("cq", rq), ("ckv", rk), ("kpe", MLA_ROPE), ("ga", d), ("gb", d)):
        offs[name] = (o, o + width)
        o += width
    assert o == w_in.shape[1]
    in_rows = {name: lo for name, (lo, _) in offs.items()}
    in_rows["end"] = o
    assert all(v % 16 == 0 for v in in_rows.values())
    col = {"q": 0, "k": gqk, "v": 2 * gqk, "r": 2 * gqk + gvw, "ga": 0, "gb": d,
           "cq": 0, "ckv": rq, "kpe": rq + rk, "a": rq + rk + 2 * MLA_ROPE}

    w3 = w_uq.reshape(rq, MLA_HEADS, MLA_NOPE + MLA_ROPE)
    pe = w3[:, :, MLA_NOPE:]
    pe_sw = jnp.concatenate([pe[:, :, half:], pe[:, :, :half]], axis=2)
    zpad = jnp.zeros((rq, MLA_HEADS, LANE - MLA_ROPE), w_uq.dtype)
    flat = lambda t: t.reshape(rq, -1).astype(BF16)
    wa_pad = jnp.concatenate([w_a2, jnp.zeros((LANE - rank, gqk), w_a2.dtype)], axis=0)
    return dict(
        col=col, dk=gqk // GLA_HEADS, dv=g_gla_out.shape[0],
        g_mix=g_mix, w_in_t=jnp.swapaxes(w_in, 0, 1), in_rows=in_rows,
        wa_pad=wa_pad.astype(BF16), b_a=b_a, g_gla_out=g_gla_out, g_q=g_q,
        wq_nope=flat(w3[:, :, :MLA_NOPE]),
        wq_pe=flat(jnp.concatenate([pe, zpad], axis=2)),
        wq_pe_sw=flat(jnp.concatenate([pe_sw, zpad], axis=2)),
        g_kv=g_kv, w_uk=w_uk.astype(BF16), w_uv=w_uv.astype(BF16),
        w_uv_t=w_uv.T.astype(BF16),
        w_uk_t3=w_uk.reshape(rk, MLA_HEADS, MLA_NOPE).transpose(1, 2, 0).astype(BF16),
        w_uv3=w_uv.reshape(rk, MLA_HEADS, MLA_V).transpose(1, 0, 2).astype(BF16),
        w_o=w_o.astype(BF16),
        g_ffn=g_ffn, w_up=w_up, conv_w=conv_w, conv_b=conv_b,
        w_down=w_down.astype(BF16), final_norm=final_norm)


def kernel(x_prompt, x_sample, cache_mla_latent, cache_mla_krope, state_gla, cache_ffn_conv,
           meta_tokens, g_mix, w_in, w_a2, b_a, g_gla_out, g_q, w_uq, g_kv, w_uk, w_uv, w_o,
           g_ffn, w_up, conv_w, conv_b, w_down, final_norm):
    assert w_in.shape[0] == 1, "single trunk layer"
    bp, seq, d = x_prompt.shape
    assert bp == 1
    bs, ts, _ = x_sample.shape
    P = cache_mla_latent.shape[2]
    w = _prep_weights(g_mix[0], w_in[0], w_a2[0], b_a[0], g_gla_out[0], g_q[0], w_uq[0],
                      g_kv[0], w_uk[0], w_uv[0], w_o[0], g_ffn[0], w_up[0], conv_w[0],
                      conv_b[0], w_down[0], final_norm)
    dk, dv, dff2 = w["dk"], w["dv"], conv_w.shape[2]

    n_meta = meta_tokens.shape[0]
    assert n_meta == N_META and seq % CHUNK == 0
    _, lat_m, kr_m, st_m, cv_m, prefix = _stream(
        meta_tokens.astype(F32), w, B=1, T=n_meta, pos=jnp.arange(n_meta, dtype=jnp.int32),
        q_off=0, s0=jnp.zeros((1, GLA_HEADS, dk, dv), F32),
        hist=jnp.zeros((1, CONV_W - 1, dff2), F32), emit_prefix=True)
    yp, lat_p, kr_p, st_p, cv_p, _ = _stream(
        x_prompt[0], w, B=1, T=seq, pos=n_meta + jnp.arange(seq, dtype=jnp.int32),
        q_off=0, s0=st_m, hist=cv_m, prefix=prefix)

    pos_s = jnp.tile(P + jnp.arange(ts, dtype=jnp.int32), bs)
    ys, lat_s, kr_s, st_s, cv_s, _ = _stream(
        x_sample.reshape(bs * ts, d), w, B=bs, T=ts, pos=pos_s, q_off=P,
        past_lat=cache_mla_latent[0], past_kr=cache_mla_krope[0], s0=state_gla[0],
        hist=cache_ffn_conv[0])

    rk = lat_p.shape[1]
    T = n_meta + seq
    return (yp,
            ys,
            jnp.concatenate([lat_m, lat_p], axis=0).reshape(1, 1, T, rk),
            jnp.concatenate([kr_m, kr_p], axis=0)[:, :MLA_ROPE].reshape(1, 1, T, MLA_ROPE),
            st_p[None],
            cv_p[None],
            lat_s.reshape(1, bs, ts, rk),
            kr_s[:, :MLA_ROPE].reshape(1, bs, ts, MLA_ROPE),
            st_s[None],
            cv_s[None])
```

```python
import functools

import jax
import jax.numpy as jnp
from jax import lax
from jax.experimental import pallas as pl
from jax.experimental.pallas import tpu as pltpu

BF16 = jnp.bfloat16
F32 = jnp.float32

CHUNK = 64
CHUNK_SHIFT = 6
N_META = 16
EPS = 1e-6
GLA_HEADS = 4
GLA_GATE_NORM = 16.0
GLA_LOG_ALPHA_MIN = -5.0
MLA_HEADS = 16
MLA_NOPE = 128
MLA_ROPE = 64
MLA_V = 128
ROPE_THETA = 10000.0
CONV_W = 3
NEG_BIG = -1e30
LOG2E = 1.4426950408889634
QK_SCALE_LOG2E = (MLA_NOPE + MLA_ROPE) ** -0.5 * LOG2E

LANE = 128
VT_ONES = 16
SAFE_EXP = 64.0
ROW_TILE = 1024
VMEM_LIMIT = 56 * 1024 * 1024


def _cparams(sem, vmem=VMEM_LIMIT):
    return pltpu.CompilerParams(dimension_semantics=sem, vmem_limit_bytes=vmem)


def _rmsnorm(x, g):
    return x * lax.rsqrt(jnp.mean(x * x, axis=-1, keepdims=True) + EPS) * g


def _sigmoid(x):
    return 1.0 / (1.0 + jnp.exp(-x))


def _pick(n, cands):
    for c in cands:
        if n % c == 0:
            return c
    if n < min(cands):
        return n
    raise ValueError(f"no tile in {cands} divides {n}")


def _norm_cast_kernel(x_ref, g_ref, o_ref):
    o_ref[...] = _rmsnorm(x_ref[...], g_ref[...]).astype(o_ref.dtype)


def _norm_cast(x, g):
    m, d = x.shape
    tm = _pick(m, (384, 256, 128))
    return pl.pallas_call(
        _norm_cast_kernel,
        grid=(m // tm,),
        in_specs=[pl.BlockSpec((tm, d), lambda i: (i, 0)),
                  pl.BlockSpec((1, d), lambda i: (0, 0))],
        out_specs=pl.BlockSpec((tm, d), lambda i: (i, 0)),
        out_shape=jax.ShapeDtypeStruct((m, d), BF16),
        compiler_params=_cparams(("parallel",)),
        name="norm_cast",
    )(x, g.reshape(1, d))


def _matmul_kernel(a_ref, b_ref, o_ref):
    o_ref[...] = jnp.dot(a_ref[...], b_ref[...], preferred_element_type=F32).astype(o_ref.dtype)


def _matmul(a, b, out_dtype, tn):
    m, k = a.shape
    n = b.shape[1]
    tm = _pick(m, (ROW_TILE, 512, 384, 128))
    return pl.pallas_call(
        _matmul_kernel,
        grid=(n // tn, m // tm),
        in_specs=[pl.BlockSpec((tm, k), lambda j, i: (i, 0)),
                  pl.BlockSpec((k, tn), lambda j, i: (0, j))],
        out_specs=pl.BlockSpec((tm, tn), lambda j, i: (i, j)),
        out_shape=jax.ShapeDtypeStruct((m, n), out_dtype),
        compiler_params=_cparams(("parallel", "parallel")),
        name="in_proj",
    )(a, b)


_NT = (((1,), (1,)), ((), ()))


def _matmul_wt_kernel(a_ref, w_ref, o_ref, w_scr):
    @pl.when(pl.program_id(1) == 0)
    def _():
        w_scr[...] = w_ref[...].astype(BF16)

    o_ref[...] = lax.dot_general(a_ref[...], w_scr[...], _NT,
                                 preferred_element_type=F32).astype(o_ref.dtype)


def _matmul_wt(a, w_t, row0, n, out_dtype, tn):
    m, k = a.shape
    tm = _pick(m, (ROW_TILE, 512, 384, 128))
    return pl.pallas_call(
        _matmul_wt_kernel,
        grid=(n // tn, m // tm),
        in_specs=[pl.BlockSpec((tm, k), lambda j, i: (i, 0)),
                  pl.BlockSpec((pl.Element(tn), pl.Element(k)),
                               lambda j, i: (pl.multiple_of(row0 + j * tn, 16), 0))],
        out_specs=pl.BlockSpec((tm, tn), lambda j, i: (i, j)),
        out_shape=jax.ShapeDtypeStruct((m, n), out_dtype),
        scratch_shapes=[pltpu.VMEM((tn, k), BF16)],
        compiler_params=_cparams(("parallel", "arbitrary")),
        name="in_proj_wt",
    )(a, w_t)


def _small_proj_kernel(a_ref, w_ref, o_ref, w_scr, *, rank, rq, rk):
    @pl.when(pl.program_id(0) == 0)
    def _():
        w = w_ref[...].astype(BF16)
        half = MLA_ROPE // 2
        pe0 = rank + rq + rk
        o_pe = rq + rk
        w_scr[0:rq] = w[rank:rank + rq]
        w_scr[rq:o_pe] = w[rank + rq:pe0]
        w_scr[o_pe:o_pe + MLA_ROPE] = w[pe0:pe0 + MLA_ROPE]
        w_scr[o_pe + MLA_ROPE:o_pe + MLA_ROPE + half] = w[pe0 + half:pe0 + MLA_ROPE]
        w_scr[o_pe + MLA_ROPE + half:o_pe + 2 * MLA_ROPE] = w[pe0:pe0 + half]
        o_a = o_pe + 2 * MLA_ROPE
        w_scr[o_a:o_a + rank] = w[0:rank]
        w_scr[o_a + rank:] = jnp.zeros((w_scr.shape[0] - o_a - rank, w_scr.shape[1]), BF16)

    o_ref[...] = lax.dot_general(a_ref[...], w_scr[...], _NT, preferred_element_type=F32)


def _small_proj(a, w_t, row0, *, rank, rq, rk):
    m, k = a.shape
    n_in = rank + rq + rk + MLA_ROPE
    n_out = rq + rk + 2 * MLA_ROPE + LANE
    tm = _pick(m, (ROW_TILE, 512, 384, 128))
    kern = functools.partial(_small_proj_kernel, rank=rank, rq=rq, rk=rk)
    return pl.pallas_call(
        kern,
        grid=(m // tm,),
        in_specs=[pl.BlockSpec((tm, k), lambda i: (i, 0)),
                  pl.BlockSpec((pl.Element(n_in), pl.Element(k)), lambda i: (row0, 0))],
        out_specs=pl.BlockSpec((tm, n_out), lambda i: (i, 0)),
        out_shape=jax.ShapeDtypeStruct((m, n_out), F32),
        scratch_shapes=[pltpu.VMEM((n_out, k), BF16)],
        compiler_params=_cparams(("arbitrary",)),
        name="in_proj_small",
    )(a, w_t)


def _split3(x):
    a = x.astype(BF16)
    r1 = x - a.astype(F32)
    b = r1.astype(BF16)
    c = (r1 - b.astype(F32)).astype(BF16)
    return a, b, c


def _gla_kernel(q_ref, k_ref, v_ref, r_ref, ga_ref, a_ref, wa_ref, ba_ref, go_ref, s0_ref,
                o_ref, sout_ref, s_scr, *, C, SB, T, H, dk, dv):
    c_idx = pl.program_id(1)
    n_chunks = pl.num_programs(1)

    @pl.when(c_idx == 0)
    def _():
        s_scr[...] = s0_ref[0]

    z = jnp.dot(a_ref[...].astype(BF16), wa_ref[...], preferred_element_type=F32) + ba_ref[...]
    log_sig = jnp.minimum(z, 0.0) - jnp.log1p(jnp.exp(-jnp.abs(z)))
    la = jnp.maximum(log_sig * (1.0 / GLA_GATE_NORM), GLA_LOG_ALPHA_MIN)
    if T % C:
        rows = c_idx * C + lax.broadcasted_iota(jnp.int32, (C, 1), 0)
        la = jnp.where(rows < T, la, 0.0)

    ri = lax.broadcasted_iota(jnp.int32, (C, C), 0)
    ci = lax.broadcasted_iota(jnp.int32, (C, C), 1)
    tri = jnp.where(ri >= ci, 1.0, 0.0).astype(BF16)
    ones = jnp.ones((C, LANE), BF16)
    cs_all = jnp.zeros_like(la)
    dsum_all = jnp.zeros((la.shape[1], LANE), F32)
    for piece in _split3(la):
        cs_all = cs_all + jnp.dot(tri, piece, preferred_element_type=F32)
        dsum_all = dsum_all + lax.dot_general(piece, ones, (((0,), (0,)), ((), ())),
                                              preferred_element_type=F32)

    sr = lax.broadcasted_iota(jnp.int32, (SB, SB), 0)
    sc = lax.broadcasted_iota(jnp.int32, (SB, SB), 1)
    causal = sr >= sc
    nt = (((1,), (1,)), ((), ()))
    scale = dk ** -0.5

    for h in range(H):
        ksl = slice(h * dk, (h + 1) * dk)
        vsl = slice(h * dv, (h + 1) * dv)
        cs = cs_all[:, ksl]
        c_last = cs[C - 1:C, :]
        q = q_ref[:, ksl].astype(F32) * scale
        k = k_ref[:, ksl].astype(F32)
        v = v_ref[:, vsl]
        s_old = s_scr[h]

        o_inter = jnp.dot((q * jnp.exp(cs)).astype(BF16), s_old.astype(BF16),
                          preferred_element_type=F32)
        k_end = (k * jnp.exp(c_last - cs)).astype(BF16)
        upd = lax.dot_general(k_end, v, (((0,), (0,)), ((), ())), preferred_element_type=F32)
        dcol = jnp.exp(dsum_all[ksl, :])
        s_scr[h] = jnp.concatenate([dcol] * (dv // LANE), axis=1) * s_old + upd

        outs = []
        for i in range(C // SB):
            lo = i * SB
            cs_i = cs[lo:lo + SB]
            q_i = q[lo:lo + SB]
            k_i = k[lo:lo + SB]
            start = cs[lo - 1:lo] if i > 0 else jnp.zeros_like(c_last)
            mid = 0.5 * (start + cs[lo + SB - 1:lo + SB])
            qd = (q_i * jnp.exp(cs_i - mid)).astype(BF16)
            kd = (k_i * jnp.exp(mid - cs_i)).astype(BF16)
            att = lax.dot_general(qd, kd, nt, preferred_element_type=F32)
            att = jnp.where(causal, att, 0.0)
            o_i = jnp.dot(att.astype(BF16), v[lo:lo + SB], preferred_element_type=F32)
            if i > 0:
                qo = (q_i * jnp.exp(cs_i - start)).astype(BF16)
                ko = (k[:lo] * jnp.exp(start - cs[:lo])).astype(BF16)
                att_o = lax.dot_general(qo, ko, nt, preferred_element_type=F32)
                o_i = o_i + jnp.dot(att_o.astype(BF16), v[:lo], preferred_element_type=F32)
            outs.append(o_i)
        o = o_inter + (jnp.concatenate(outs, axis=0) if len(outs) > 1 else outs[0])

        on = _rmsnorm(o, go_ref[...])
        r = r_ref[:, vsl].astype(F32)
        g = ga_ref[:, vsl].astype(F32)
        o_ref[:, vsl] = (_sigmoid(g) * (on * (r * _sigmoid(r)))).astype(o_ref.dtype)

    @pl.when(c_idx == n_chunks - 1)
    def _():
        sout_ref[0] = s_scr[...]


def _gla(qkvr, gates, small, wa_pad, b_a, g_out, s0, *, B, T, Tp, dk, dv, col):
    C = min(128, Tp)
    SB = min(32, C)
    nc = Tp // C
    H = GLA_HEADS
    qk, vw = H * dk, H * dv
    rb = lambda b, c: b * nc + c
    kern = functools.partial(_gla_kernel, C=C, SB=SB, T=T, H=H, dk=dk, dv=dv)
    return pl.pallas_call(
        kern,
        grid=(B, nc),
        in_specs=[
            pl.BlockSpec((C, qk), lambda b, c: (rb(b, c), col["q"] // qk)),
            pl.BlockSpec((C, qk), lambda b, c: (rb(b, c), col["k"] // qk)),
            pl.BlockSpec((C, vw), lambda b, c: (rb(b, c), col["v"] // vw)),
            pl.BlockSpec((C, vw), lambda b, c: (rb(b, c), col["r"] // vw)),
            pl.BlockSpec((C, vw), lambda b, c: (rb(b, c), col["ga"] // vw)),
            pl.BlockSpec((C, LANE), lambda b, c: (rb(b, c), col["a"] // LANE)),
            pl.BlockSpec((LANE, qk), lambda b, c: (0, 0)),
            pl.BlockSpec((1, qk), lambda b, c: (0, 0)),
            pl.BlockSpec((1, dv), lambda b, c: (0, 0)),
            pl.BlockSpec((1, H, dk, dv), lambda b, c: (b, 0, 0, 0)),
        ],
        out_specs=[
            pl.BlockSpec((C, vw), lambda b, c: (rb(b, c), 0)),
            pl.BlockSpec((1, H, dk, dv), lambda b, c: (b, 0, 0, 0)),
        ],
        out_shape=[jax.ShapeDtypeStruct((B * Tp, vw), BF16),
                   jax.ShapeDtypeStruct((B, H, dk, dv), F32)],
        scratch_shapes=[pltpu.VMEM((H, dk, dv), F32)],
        compiler_params=_cparams(("parallel", "arbitrary")),
        name="gla",
    )(qkvr, qkvr, qkvr, qkvr, gates, small, wa_pad, b_a.reshape(1, -1), g_out.reshape(1, -1), s0)


def _qprep_kernel(cq_ref, gq_ref, wn_ref, wp_ref, wps_ref, cos_ref, sin_ref, q_ref):
    hq = _rmsnorm(cq_ref[...], gq_ref[...]).astype(BF16)
    qn = jnp.dot(hq, wn_ref[...], preferred_element_type=F32)
    qp = jnp.dot(hq, wp_ref[...], preferred_element_type=F32)
    qs = jnp.dot(hq, wps_ref[...], preferred_element_type=F32)
    cos = cos_ref[...] * QK_SCALE_LOG2E
    sin = sin_ref[...] * QK_SCALE_LOG2E
    for h in range(MLA_HEADS):
        sl = slice(h * LANE, (h + 1) * LANE)
        q_ref[h, :, 0:LANE] = (qn[:, sl] * QK_SCALE_LOG2E).astype(BF16)
        q_ref[h, :, LANE:2 * LANE] = (qp[:, sl] * cos + qs[:, sl] * sin).astype(BF16)


def _qprep(small, g_q, wn, wp, wps, cos_t, sin_t, *, col):
    m = small.shape[0]
    rq = wn.shape[0]
    tm = _pick(m, (256, 128))
    full = lambda i: (0, 0)
    return pl.pallas_call(
        _qprep_kernel,
        grid=(m // tm,),
        in_specs=[pl.BlockSpec((tm, rq), lambda i: (i, col["cq"] // rq)),
                  pl.BlockSpec((1, rq), full),
                  pl.BlockSpec(wn.shape, full),
                  pl.BlockSpec(wp.shape, full),
                  pl.BlockSpec(wps.shape, full),
                  pl.BlockSpec((tm, LANE), lambda i: (i, 0)),
                  pl.BlockSpec((tm, LANE), lambda i: (i, 0))],
        out_specs=pl.BlockSpec((MLA_HEADS, tm, 2 * LANE), lambda i: (0, i, 0)),
        out_shape=jax.ShapeDtypeStruct((MLA_HEADS, m, 2 * LANE), BF16),
        compiler_params=_cparams(("parallel",)),
        name="mla_q",
    )(small, g_q.reshape(1, -1), wn, wp, wps, cos_t, sin_t)


def _lat_kernel(ckv_ref, kpe_ref, gkv_ref, cos_ref, sin_ref, lat_ref, kr_ref):
    lat_ref[...] = _rmsnorm(ckv_ref[...], gkv_ref[...])
    blk = kpe_ref[...]
    kr_ref[...] = blk * cos_ref[...] + pltpu.roll(blk, LANE // 2, 1) * sin_ref[...]


def _lat(small, g_kv, cos_t, sin_t, *, col):
    m = small.shape[0]
    rk = g_kv.shape[0]
    tm = _pick(m, (256, 128))
    return pl.pallas_call(
        _lat_kernel,
        grid=(m // tm,),
        in_specs=[pl.BlockSpec((tm, rk), lambda i: (i, col["ckv"] // rk)),
                  pl.BlockSpec((tm, LANE), lambda i: (i, col["kpe"] // LANE)),
                  pl.BlockSpec((1, rk), lambda i: (0, 0)),
                  pl.BlockSpec((tm, LANE), lambda i: (i, 0)),
                  pl.BlockSpec((tm, LANE), lambda i: (i, 0))],
        out_specs=[pl.BlockSpec((tm, rk), lambda i: (i, 0)),
                   pl.BlockSpec((tm, LANE), lambda i: (i, 0))],
        out_shape=[jax.ShapeDtypeStruct((m, rk), F32),
                   jax.ShapeDtypeStruct((m, LANE), F32)],
        compiler_params=_cparams(("parallel",)),
        name="mla_latent",
    )(small, small, g_kv.reshape(1, -1), cos_t, sin_t)


def _kvup_kernel(lat_ref, kr_ref, wuk_ref, wuv_ref, k_ref, v_ref, *, v_transposed):
    lat = lat_ref[...].astype(BF16)
    kn = jnp.dot(lat, wuk_ref[...], preferred_element_type=F32)
    kr = kr_ref[...]
    lane = lax.broadcasted_iota(jnp.int32, kr.shape, 1)
    kp = jnp.where(lane == MLA_ROPE, 1.0, kr).astype(BF16)
    if v_transposed:
        vv = lax.dot_general(wuv_ref[...], lat, (((1,), (1,)), ((), ())),
                             preferred_element_type=F32)
    else:
        vv = jnp.dot(lat, wuv_ref[...], preferred_element_type=F32)
    for h in range(MLA_HEADS):
        sl = slice(h * LANE, (h + 1) * LANE)
        k_ref[h, :, 0:LANE] = kn[:, sl].astype(BF16)
        k_ref[h, :, LANE:2 * LANE] = kp
        if v_transposed:
            v_ref[h, 0:LANE, :] = vv[sl, :].astype(BF16)
            v_ref[h, LANE:LANE + VT_ONES, :] = jnp.ones((VT_ONES, vv.shape[1]), BF16)
        else:
            v_ref[h] = vv[:, sl].astype(BF16)


def _kvup(lat, kr, wuk, wuv, *, v_transposed=False):
    m, rk = lat.shape
    tm = _pick(m, (512, 256, 128))
    full = lambda i: (0, 0)
    if v_transposed:
        v_spec = pl.BlockSpec((MLA_HEADS, LANE + VT_ONES, tm), lambda i: (0, 0, i))
        v_shape = (MLA_HEADS, LANE + VT_ONES, m)
    else:
        v_spec = pl.BlockSpec((MLA_HEADS, tm, LANE), lambda i: (0, i, 0))
        v_shape = (MLA_HEADS, m, LANE)
    return pl.pallas_call(
        functools.partial(_kvup_kernel, v_transposed=v_transposed),
        grid=(m // tm,),
        in_specs=[pl.BlockSpec((tm, rk), lambda i: (i, 0)),
                  pl.BlockSpec((tm, LANE), lambda i: (i, 0)),
                  pl.BlockSpec(wuk.shape, full),
                  pl.BlockSpec(wuv.shape, full)],
        out_specs=[pl.BlockSpec((MLA_HEADS, tm, 2 * LANE), lambda i: (0, i, 0)), v_spec],
        out_shape=[jax.ShapeDtypeStruct((MLA_HEADS, m, 2 * LANE), BF16),
                   jax.ShapeDtypeStruct(v_shape, BF16)],
        compiler_params=_cparams(("parallel",)),
        name="mla_kv",
    )(lat, kr, wuk, wuv)


def _last_kblock(qi, *, tq, tk, nk, q_off, k_off):
    top_chunk = ((qi + 1) * tq - 1 + q_off) // CHUNK
    last_key = (top_chunk + 1) * CHUNK - 1 - k_off
    return jnp.minimum(last_key // tk, nk - 1)


def _attn_kernel(q_ref, k_ref, v_ref, o_ref, m_scr, l_scr, acc_scr, *, hps, tq, tk, nk,
                 q_off, k_off):
    qi = pl.program_id(2)
    ki = pl.program_id(3)

    @pl.when(ki == 0)
    def _():
        m_scr[...] = jnp.full(m_scr.shape, NEG_BIG, F32)
        l_scr[...] = jnp.zeros(l_scr.shape, F32)
        acc_scr[...] = jnp.zeros(acc_scr.shape, F32)

    @pl.when(ki <= _last_kblock(qi, tq=tq, tk=tk, nk=nk, q_off=q_off, k_off=k_off))
    def _():
        q_chunk = (qi * tq + q_off + lax.broadcasted_iota(jnp.int32, (tq, 1), 0)) >> CHUNK_SHIFT
        k_chunk = (ki * tk + k_off + lax.broadcasted_iota(jnp.int32, (1, tk), 1)) >> CHUNK_SHIFT
        visible = q_chunk >= k_chunk

        def head(h, carry):
            s = lax.dot_general(q_ref[h], k_ref[h], (((1,), (1,)), ((), ())),
                                preferred_element_type=F32)
            s = jnp.where(visible, s, NEG_BIG)
            m_prev = m_scr[h]
            m_new = jnp.maximum(m_prev, jnp.max(s, axis=-1, keepdims=True))
            p = jnp.exp2(s - m_new)
            alpha = jnp.exp2(m_prev - m_new)
            l_scr[h] = alpha * l_scr[h] + jnp.sum(p, axis=-1, keepdims=True)
            acc_scr[h] = alpha * acc_scr[h] + jnp.dot(p.astype(BF16), v_ref[h],
                                                      preferred_element_type=F32)
            m_scr[h] = m_new
            return carry

        lax.fori_loop(0, hps, head, 0)

    @pl.when(ki == nk - 1)
    def _():
        for h in range(hps):
            o_ref[:, h * LANE:(h + 1) * LANE] = (acc_scr[h] / l_scr[h]).astype(o_ref.dtype)


def _attention(q, k, v, *, B, Tq, Tk, tq, tk, hps, q_off, k_off):
    nq = Tq // tq
    nk = Tk // tk
    hg = MLA_HEADS // hps
    dqk = q.shape[2]
    dvh = v.shape[2]
    last = functools.partial(_last_kblock, tq=tq, tk=tk, nk=nk, q_off=q_off, k_off=k_off)
    kern = functools.partial(_attn_kernel, hps=hps, tq=tq, tk=tk, nk=nk, q_off=q_off,
                             k_off=k_off)
    kv_row = lambda b, g, i, j: b * nk + jnp.minimum(j, last(i))
    return pl.pallas_call(
        kern,
        grid=(B, hg, nq, nk),
        in_specs=[pl.BlockSpec((hps, tq, dqk), lambda b, g, i, j: (g, b * nq + i, 0)),
                  pl.BlockSpec((hps, tk, dqk), lambda b, g, i, j: (g, kv_row(b, g, i, j), 0)),
                  pl.BlockSpec((hps, tk, dvh), lambda b, g, i, j: (g, kv_row(b, g, i, j), 0))],
        out_specs=pl.BlockSpec((tq, hps * dvh), lambda b, g, i, j: (b * nq + i, g)),
        out_shape=jax.ShapeDtypeStruct((B * Tq, MLA_HEADS * dvh), BF16),
        scratch_shapes=[pltpu.VMEM((hps, tq, 1), F32),
                        pltpu.VMEM((hps, tq, 1), F32),
                        pltpu.VMEM((hps, tq, dvh), F32)],
        compiler_params=_cparams(("parallel", "parallel", "parallel", "arbitrary")),
        name="mla_attn",
    )(q, k, v)


def _attn_t_kernel(qi_ref, ki_ref, q_ref, k_ref, vt_ref, kp_ref, vtp_ref, o_ref,
                   q_scr, r_scr, acc_scr, *, hps, t):
    pair = pl.program_id(1)
    qi = qi_ref[pair]
    ki = ki_ref[pair]
    nt = (((1,), (1,)), ((), ()))
    pe = slice(LANE, 2 * LANE)
    lane = lax.broadcasted_iota(jnp.int32, (t, LANE), 1)

    def set_reference(h, r):
        neg_r = jnp.transpose(jnp.broadcast_to(-r, (LANE, t)))
        q_scr[h, :, pe] = jnp.where(lane == MLA_ROPE, neg_r.astype(BF16), q_ref[h, :, pe])
        r_scr[h] = r

    def shifted_scores(h):
        return lax.dot_general(k_ref[h], q_scr[h], nt, preferred_element_type=F32)

    @pl.when(ki == 0)
    def _():
        for h in range(hps):
            q_scr[h, :, 0:LANE] = q_ref[h, :, 0:LANE]
            s = lax.dot_general(kp_ref[h], q_ref[h], nt, preferred_element_type=F32)
            r = jnp.max(s, axis=0, keepdims=True).astype(BF16).astype(F32)
            p = jnp.exp2((s - r).astype(BF16))
            acc_scr[h] = jnp.dot(vtp_ref[h], p, preferred_element_type=F32)
            set_reference(h, r)

    def general(h, bias, keep_reference):
        sp = shifted_scores(h)
        if bias is not None:
            sp = sp + bias
        r = r_scr[h]
        rise = jnp.maximum(jnp.max(sp, axis=0, keepdims=True), 0.0)
        r_new = (r + rise).astype(BF16).astype(F32)
        delta = r_new - r
        p = jnp.exp2((sp - delta).astype(BF16))
        acc_scr[h] = jnp.exp2(-delta) * acc_scr[h] + jnp.dot(vt_ref[h], p,
                                                               preferred_element_type=F32)
        if keep_reference:
            set_reference(h, r_new)

    @pl.when(ki < qi)
    def _():
        unsafe = []
        for h in range(hps):
            sp = shifted_scores(h)
            safe = jnp.max(sp) <= SAFE_EXP
            part = jnp.dot(vt_ref[h], jnp.exp2(sp.astype(BF16)), preferred_element_type=F32)
            acc_scr[h] += jnp.where(safe, part, 0.0)
            unsafe.append(jnp.logical_not(safe))

        @pl.when(functools.reduce(jnp.logical_or, unsafe))
        def _():
            for h in range(hps):
                @pl.when(unsafe[h])
                def _():
                    general(h, None, True)

    @pl.when(ki == qi)
    def _():
        k_chunk = (ki * t + lax.broadcasted_iota(jnp.int32, (t, 1), 0)) >> CHUNK_SHIFT
        q_chunk = (qi * t + lax.broadcasted_iota(jnp.int32, (1, t), 1)) >> CHUNK_SHIFT
        bias = jnp.where(q_chunk >= k_chunk, 0.0, NEG_BIG)
        for h in range(hps):
            general(h, bias, False)
            acc = acc_scr[h]
            o_t = acc[0:LANE] / acc[LANE:LANE + 1]
            o_ref[:, h * LANE:(h + 1) * LANE] = o_t.T.astype(o_ref.dtype)


def _attention_t(q, k, vt, k_pre, vt_pre, *, T, t, hps):
    n = T // t
    hg = MLA_HEADS // hps
    dqk = q.shape[2]
    npre = k_pre.shape[1]
    vrows = vt.shape[1]
    pairs = [(i, j) for i in range(n) for j in range(i + 1)]
    qi_arr = jnp.asarray([p[0] for p in pairs], jnp.int32)
    ki_arr = jnp.asarray([p[1] for p in pairs], jnp.int32)
    kern = functools.partial(_attn_t_kernel, hps=hps, t=t)
    grid_spec = pltpu.PrefetchScalarGridSpec(
        num_scalar_prefetch=2,
        grid=(hg, len(pairs)),
        in_specs=[pl.BlockSpec((hps, t, dqk), lambda g, p, qi, ki: (g, qi[p], 0)),
                  pl.BlockSpec((hps, t, dqk), lambda g, p, qi, ki: (g, ki[p], 0)),
                  pl.BlockSpec((hps, vrows, t), lambda g, p, qi, ki: (g, 0, ki[p])),
                  pl.BlockSpec((hps, npre, dqk), lambda g, p, qi, ki: (g, 0, 0)),
                  pl.BlockSpec((hps, vrows, npre), lambda g, p, qi, ki: (g, 0, 0))],
        out_specs=pl.BlockSpec((t, hps * LANE), lambda g, p, qi, ki: (qi[p], g)),
        scratch_shapes=[pltpu.VMEM((hps, t, dqk), BF16),
                        pltpu.VMEM((hps, 1, t), F32),
                        pltpu.VMEM((hps, vrows, t), F32)])
    return pl.pallas_call(
        kern,
        grid_spec=grid_spec,
        out_shape=jax.ShapeDtypeStruct((T, MLA_HEADS * LANE), BF16),
        compiler_params=_cparams(("parallel", "arbitrary")),
        name="mla_attn_t",
    )(qi_arr, ki_arr, q, k, vt, k_pre, vt_pre)


def _absorb_q_kernel(q_ref, w_ref, o_ref):
    o_ref[0] = jnp.dot(q_ref[0, :, 0:MLA_NOPE], w_ref[0],
                       preferred_element_type=F32).astype(o_ref.dtype)


def _absorb_q(q, w_uk_t3):
    heads, rows, dqk = q.shape
    rk = w_uk_t3.shape[2]
    return pl.pallas_call(
        _absorb_q_kernel,
        grid=(heads,),
        in_specs=[pl.BlockSpec((1, rows, dqk), lambda h: (h, 0, 0)),
                  pl.BlockSpec((1, MLA_NOPE, rk), lambda h: (h, 0, 0))],
        out_specs=pl.BlockSpec((1, rows, rk), lambda h: (h, 0, 0)),
        out_shape=jax.ShapeDtypeStruct((heads, rows, rk), BF16),
        compiler_params=_cparams(("parallel",)),
        name="mla_absorb_q",
    )(q, w_uk_t3)


def _attn_latent_kernel(ql_ref, q_ref, plat_ref, pkr_ref, lat_ref, kr_ref, o_ref, *, T, P):
    heads, _, rk = ql_ref.shape
    rows = heads * T
    nt = (((1,), (1,)), ((), ()))
    ql = ql_ref[...].reshape(rows, rk)
    qpe = q_ref[:, :, LANE:2 * LANE].reshape(rows, LANE)[:, 0:MLA_ROPE]
    lat_all = jnp.concatenate([plat_ref[0].astype(BF16), lat_ref[...].astype(BF16)], axis=0)
    kr_all = jnp.concatenate([pkr_ref[0], kr_ref[:, 0:MLA_ROPE]], axis=0).astype(BF16)
    s = (lax.dot_general(ql, lat_all, nt, preferred_element_type=F32)
         + lax.dot_general(qpe, kr_all, nt, preferred_element_type=F32))
    tok = lax.rem(lax.broadcasted_iota(jnp.int32, (rows, 1), 0), T)
    q_chunk = (P + tok) >> CHUNK_SHIFT
    k_chunk = lax.broadcasted_iota(jnp.int32, (1, P + T), 1) >> CHUNK_SHIFT
    s = jnp.where(q_chunk >= k_chunk, s, NEG_BIG)
    p = jnp.exp2(s - jnp.max(s, axis=-1, keepdims=True))
    o = jnp.dot(p.astype(BF16), lat_all, preferred_element_type=F32)
    o = o / jnp.sum(p, axis=-1, keepdims=True)
    o_ref[...] = o.reshape(heads, T, rk).astype(o_ref.dtype)


def _attn_latent(qlat, q, past_lat, past_kr, lat, kr, *, B, T):
    heads, _, rk = qlat.shape
    P = past_lat.shape[1]
    kern = functools.partial(_attn_latent_kernel, T=T, P=P)
    return pl.pallas_call(
        kern,
        grid=(B,),
        in_specs=[pl.BlockSpec((heads, T, rk), lambda b: (0, b, 0)),
                  pl.BlockSpec((heads, T, q.shape[2]), lambda b: (0, b, 0)),
                  pl.BlockSpec((1, P, rk), lambda b: (b, 0, 0)),
                  pl.BlockSpec((1, P, past_kr.shape[2]), lambda b: (b, 0, 0)),
                  pl.BlockSpec((T, rk), lambda b: (b, 0)),
                  pl.BlockSpec((T, LANE), lambda b: (b, 0))],
        out_specs=pl.BlockSpec((heads, T, rk), lambda b: (0, b, 0)),
        out_shape=jax.ShapeDtypeStruct((heads, B * T, rk), BF16),
        compiler_params=_cparams(("parallel",)),
        name="mla_attn_latent",
    )(qlat, q, past_lat, past_kr, lat, kr)


def _absorb_out_kernel(o_ref, w_ref, out_ref):
    out_ref[...] = jnp.dot(o_ref[0], w_ref[0], preferred_element_type=F32).astype(out_ref.dtype)


def _absorb_out(olat, w_uv3):
    heads, rows, rk = olat.shape
    dvh = w_uv3.shape[2]
    return pl.pallas_call(
        _absorb_out_kernel,
        grid=(heads,),
        in_specs=[pl.BlockSpec((1, rows, rk), lambda h: (h, 0, 0)),
                  pl.BlockSpec((1, rk, dvh), lambda h: (h, 0, 0))],
        out_specs=pl.BlockSpec((rows, dvh), lambda h: (0, h)),
        out_shape=jax.ShapeDtypeStruct((rows, heads * dvh), BF16),
        compiler_params=_cparams(("parallel",)),
        name="mla_absorb_out",
    )(olat, w_uv3)


def _merge_kernel(a_ref, gb_ref, om_ref, x_ref, wo_ref, gf_ref, x1_ref, h2_ref):
    merged = a_ref[...].astype(F32) + _sigmoid(gb_ref[...].astype(F32)) * om_ref[...].astype(F32)
    x1 = x_ref[...] + jnp.dot(merged.astype(BF16), wo_ref[...], preferred_element_type=F32)
    x1_ref[...] = x1
    h2_ref[...] = _rmsnorm(x1, gf_ref[...]).astype(BF16)


def _merge(branch_a, gates, o_m, x, wo, g_ffn, *, col):
    m, d = x.shape
    tm = _pick(m, (384, 256, 128))
    row = lambda i: (i, 0)
    return pl.pallas_call(
        _merge_kernel,
        grid=(m // tm,),
        in_specs=[pl.BlockSpec((tm, d), row),
                  pl.BlockSpec((tm, d), lambda i: (i, col["gb"] // d)),
                  pl.BlockSpec((tm, d), row),
                  pl.BlockSpec((tm, d), row),
                  pl.BlockSpec(wo.shape, lambda i: (0, 0)),
                  pl.BlockSpec((1, d), lambda i: (0, 0))],
        out_specs=[pl.BlockSpec((tm, d), row), pl.BlockSpec((tm, d), row)],
        out_shape=[jax.ShapeDtypeStruct((m, d), F32), jax.ShapeDtypeStruct((m, d), BF16)],
        compiler_params=_cparams(("parallel",)),
        name="merge_out_proj",
    )(branch_a, gates, o_m, x, wo, g_ffn.reshape(1, -1))


HALO = 8


def _ffn_up_kernel(h_ref, wa_ref, wb_ref, cwa_ref, cwb_ref, cba_ref, cbb_ref, ha_ref, hb_ref,
                   act_ref, ca_ref, cb_ref, ext_scr, carry_scr, w_scr,
                   *, bb, r, tf, loc, carried):
    s = pl.program_id(1)
    rt = pl.program_id(2)
    d = h_ref.shape[2]

    @pl.when((s == 0) & (rt == 0))
    def _():
        w_scr[0] = wa_ref[...].astype(BF16)
        w_scr[1] = wb_ref[...].astype(BF16)

    if carried:
        @pl.when(rt == 0)
        def _():
            carry_scr[0] = ha_ref[...]
            carry_scr[1] = hb_ref[...]

    h = h_ref[...].reshape(bb * r, d)
    conv = []
    for half, (cw_ref, cbias_ref, hist_ref, cout_ref) in enumerate(
            ((cwa_ref, cba_ref, ha_ref, ca_ref), (cwb_ref, cbb_ref, hb_ref, cb_ref))):
        u = jnp.dot(h, w_scr[half], preferred_element_type=F32).reshape(bb, r, tf)
        ext_scr[half, :, HALO:HALO + r, :] = u
        ext_scr[half, :, HALO - 2:HALO, :] = carry_scr[half] if carried else hist_ref[...]
        u1 = ext_scr[half, :, HALO - 1:HALO - 1 + r, :]
        u2 = ext_scr[half, :, HALO - 2:HALO - 2 + r, :]
        cw = cw_ref[...]
        conv.append(cbias_ref[...] + cw[0:1] * u2 + cw[1:2] * u1 + cw[2:3] * u)
        if carried:
            carry_scr[half] = ext_scr[half, :, HALO + r - 2:HALO + r, :]
        cout_ref[0] = ext_scr[half, :, HALO + loc:HALO + loc + 2, :]

    act_ref[...] = ((conv[0] * _sigmoid(conv[0])) * conv[1]).astype(act_ref.dtype)


def _ffn_down_kernel(act_ref, wd_ref, x1_ref, gf_ref, y_ref):
    k = pl.program_id(1)

    @pl.when(k == 0)
    def _():
        y_ref[...] = jnp.zeros(y_ref.shape, F32)

    y_ref[...] += jnp.dot(act_ref[...], wd_ref[...], preferred_element_type=F32)

    @pl.when(k == pl.num_programs(1) - 1)
    def _():
        y_ref[...] = _rmsnorm(x1_ref[...] + y_ref[...], gf_ref[...])


def _ffn(h2, x1, w_up, w_down, conv_w, conv_b, hist, g_final, *, B, T, Tp):
    d = h2.shape[1]
    dff = w_down.shape[0]
    tf = _pick(dff, (512, 256, 128))
    nf = dff // tf
    if Tp <= 128:
        bb, r = B, Tp
    else:
        bb, r = 1, _pick(Tp, (ROW_TILE, 128))
    nrt = Tp // r
    carried = nrt > 1
    loc = (T - 2) - (nrt - 1) * r
    assert 0 <= loc <= r - 2, "final two valid rows must sit in the last row tile"
    kern = functools.partial(_ffn_up_kernel, bb=bb, r=r, tf=tf, loc=loc, carried=carried)
    carry_shape = (2, bb, 2, tf) if carried else (1, 1, 2, LANE)
    act, ca, cb = pl.pallas_call(
        kern,
        grid=(nf, B // bb, nrt),
        in_specs=[pl.BlockSpec((bb, r, d), lambda f, s, t: (s, t, 0)),
                  pl.BlockSpec((d, tf), lambda f, s, t: (0, f)),
                  pl.BlockSpec((d, tf), lambda f, s, t: (0, nf + f)),
                  pl.BlockSpec((CONV_W, tf), lambda f, s, t: (0, f)),
                  pl.BlockSpec((CONV_W, tf), lambda f, s, t: (0, nf + f)),
                  pl.BlockSpec((1, tf), lambda f, s, t: (0, f)),
                  pl.BlockSpec((1, tf), lambda f, s, t: (0, nf + f)),
                  pl.BlockSpec((bb, 2, tf), lambda f, s, t: (s, 0, f)),
                  pl.BlockSpec((bb, 2, tf), lambda f, s, t: (s, 0, nf + f))],
        out_specs=[pl.BlockSpec((bb, r, tf), lambda f, s, t: (s, t, f)),
                   pl.BlockSpec((1, bb, 2, tf), lambda f, s, t: (t, s, 0, f)),
                   pl.BlockSpec((1, bb, 2, tf), lambda f, s, t: (t, s, 0, f))],
        out_shape=[jax.ShapeDtypeStruct((B, Tp, dff), BF16),
                   jax.ShapeDtypeStruct((nrt, B, 2, dff), F32),
                   jax.ShapeDtypeStruct((nrt, B, 2, dff), F32)],
        scratch_shapes=[pltpu.VMEM((2, bb, HALO + r, tf), F32),
                        pltpu.VMEM(carry_shape, F32),
                        pltpu.VMEM((2, d, tf), BF16)],
        compiler_params=_cparams(("arbitrary", "arbitrary", "arbitrary")),
        name="conv_ffn_up",
    )(h2.reshape(B, Tp, d), w_up, w_up, conv_w, conv_w, conv_b.reshape(1, -1),
      conv_b.reshape(1, -1), hist, hist)

    m = B * Tp
    tm = _pick(m, (512, 256, 128))
    kc = _pick(dff, (1408, 512, 256, 128))
    y = pl.pallas_call(
        _ffn_down_kernel,
        grid=(m // tm, dff // kc),
        in_specs=[pl.BlockSpec((tm, kc), lambda i, k: (i, k)),
                  pl.BlockSpec((kc, d), lambda i, k: (k, 0)),
                  pl.BlockSpec((tm, d), lambda i, k: (i, 0)),
                  pl.BlockSpec((1, d), lambda i, k: (0, 0))],
        out_specs=pl.BlockSpec((tm, d), lambda i, k: (i, 0)),
        out_shape=jax.ShapeDtypeStruct((m, d), F32),
        compiler_params=_cparams(("parallel", "arbitrary")),
        name="ffn_down",
    )(act.reshape(m, dff), w_down, x1, g_final.reshape(1, -1))
    return y.reshape(B, Tp, d), jnp.concatenate([ca[nrt - 1], cb[nrt - 1]], axis=-1)


def _rope_tables(pos):
    half = MLA_ROPE // 2
    inv = ROPE_THETA ** (-jnp.arange(0, MLA_ROPE, 2, dtype=F32) / MLA_ROPE)
    ang = pos.astype(F32)[:, None] * inv[None, :]
    cos, sin = jnp.cos(ang), jnp.sin(ang)
    zero = jnp.zeros((pos.shape[0], LANE - 2 * half), F32)
    return (jnp.concatenate([cos, cos, zero], axis=1),
            jnp.concatenate([-sin, sin, zero], axis=1))


def _stream(x, w, *, B, T, pos, q_off, s0, hist, past_lat=None, past_kr=None, prefix=None,
            emit_prefix=False):
    col = w["col"]
    dk, dv = w["dk"], w["dv"]
    h = _norm_cast(x, w["g_mix"])
    rows = w["in_rows"]
    qkvr = _matmul_wt(h, w["w_in_t"], rows["q"], rows["a"] - rows["q"], BF16, tn=1024)
    gates = _matmul_wt(h, w["w_in_t"], rows["ga"], rows["end"] - rows["ga"], BF16, tn=1024)
    small = _small_proj(h, w["w_in_t"], rows["a"], rank=rows["cq"] - rows["a"],
                        rq=rows["ckv"] - rows["cq"], rk=rows["kpe"] - rows["ckv"])

    branch_a, state = _gla(qkvr, gates, small, w["wa_pad"], w["b_a"], w["g_gla_out"], s0,
                           B=B, T=T, Tp=T, dk=dk, dv=dv, col=col)

    cos_t, sin_t = _rope_tables(pos)
    q = _qprep(small, w["g_q"], w["wq_nope"], w["wq_pe"], w["wq_pe_sw"], cos_t, sin_t, col=col)
    lat, kr = _lat(small, w["g_kv"], cos_t, sin_t, col=col)
    own_prefix = None
    if prefix is not None:
        assert B == 1 and past_lat is None
        k, vt = _kvup(lat, kr, w["w_uk"], w["w_uv_t"], v_transposed=True)
        o_m = _attention_t(q, k, vt, prefix[0], prefix[1], T=T, t=_pick(T, (1024, 128)),
                           hps=MLA_HEADS // 4)
    elif past_lat is not None:
        qlat = _absorb_q(q, w["w_uk_t3"])
        olat = _attn_latent(qlat, q, past_lat, past_kr, lat, kr, B=B, T=T)
        o_m = _absorb_out(olat, w["w_uv3"])
    else:
        k, v = _kvup(lat, kr, w["w_uk"], w["w_uv"])
        if emit_prefix:
            own_prefix = _kvup(lat, kr, w["w_uk"], w["w_uv_t"], v_transposed=True)
        o_m = _attention(q, k, v, B=B, Tq=T, Tk=T, tq=T, tk=T, hps=MLA_HEADS,
                         q_off=q_off, k_off=0)

    x1, h2 = _merge(branch_a, gates, o_m, x, w["w_o"], w["g_ffn"], col=col)
    y, conv = _ffn(h2, x1, w["w_up"], w["w_down"], w["conv_w"], w["conv_b"], hist,
                   w["final_norm"], B=B, T=T, Tp=T)
    return y, lat, kr, state, conv, own_prefix


def _prep_weights(g_mix, w_in, w_a2, b_a, g_gla_out, g_q, w_uq, g_kv, w_uk, w_uv, w_o,
                  g_ffn, w_up, conv_w, conv_b, w_down, final_norm):
    d = w_in.shape[0]
    rank, gqk = w_a2.shape
    gvw = GLA_HEADS * g_gla_out.shape[0]
    rq, rk = g_q.shape[0], g_kv.shape[0]
    half = MLA_ROPE // 2
    o, offs = 0, {}
    for name, width in (("q", gqk), ("k", gqk), ("v", gvw), ("r", gvw), ("a", rank),
                        ("cq", rq), ("ckv", rk), ("kpe", MLA_ROPE), ("ga", d), ("gb", d)):
        offs[name] = (o, o + width)
        o += width
    assert o == w_in.shape[1]
    in_rows = {name: lo for name, (lo, _) in offs.items()}
    in_rows["end"] = o
    assert all(v % 16 == 0 for v in in_rows.values())
    col = {"q": 0, "k": gqk, "v": 2 * gqk, "r": 2 * gqk + gvw, "ga": 0, "gb": d,
           "cq": 0, "ckv": rq, "kpe": rq + rk, "a": rq + rk + 2 * MLA_ROPE}

    w3 = w_uq.reshape(rq, MLA_HEADS, MLA_NOPE + MLA_ROPE)
    pe = w3[:, :, MLA_NOPE:]
    pe_sw = jnp.concatenate([pe[:, :, half:], pe[:, :, :half]], axis=2)
    zpad = jnp.zeros((rq, MLA_HEADS, LANE - MLA_ROPE), w_uq.dtype)
    flat = lambda t: t.reshape(rq, -1).astype(BF16)
    wa_pad = jnp.concatenate([w_a2, jnp.zeros((LANE - rank, gqk), w_a2.dtype)], axis=0)
    return dict(
        col=col, dk=gqk // GLA_HEADS, dv=g_gla_out.shape[0],
        g_mix=g_mix, w_in_t=jnp.swapaxes(w_in, 0, 1), in_rows=in_rows,
        wa_pad=wa_pad.astype(BF16), b_a=b_a, g_gla_out=g_gla_out, g_q=g_q,
        wq_nope=flat(w3[:, :, :MLA_NOPE]),
        wq_pe=flat(jnp.concatenate([pe, zpad], axis=2)),
        wq_pe_sw=flat(jnp.concatenate([pe_sw, zpad], axis=2)),
        g_kv=g_kv, w_uk=w_uk.astype(BF16), w_uv=w_uv.astype(BF16),
        w_uv_t=w_uv.T.astype(BF16),
        w_uk_t3=w_uk.reshape(rk, MLA_HEADS, MLA_NOPE).transpose(1, 2, 0).astype(BF16),
        w_uv3=w_uv.reshape(rk, MLA_HEADS, MLA_V).transpose(1, 0, 2).astype(BF16),
        w_o=w_o.astype(BF16),
        g_ffn=g_ffn, w_up=w_up, conv_w=conv_w, conv_b=conv_b,
        w_down=w_down.astype(BF16), final_norm=final_norm)


def kernel(x_prompt, x_sample, cache_mla_latent, cache_mla_krope, state_gla, cache_ffn_conv,
           meta_tokens, g_mix, w_in, w_a2, b_a, g_gla_out, g_q, w_uq, g_kv, w_uk, w_uv, w_o,
           g_ffn, w_up, conv_w, conv_b, w_down, final_norm):
    assert w_in.shape[0] == 1, "single trunk layer"
    bp, seq, d = x_prompt.shape
    assert bp == 1
    bs, ts, _ = x_sample.shape
    P = cache_mla_latent.shape[2]
    w = _prep_weights(g_mix[0], w_in[0], w_a2[0], b_a[0], g_gla_out[0], g_q[0], w_uq[0],
                      g_kv[0], w_uk[0], w_uv[0], w_o[0], g_ffn[0], w_up[0], conv_w[0],
                      conv_b[0], w_down[0], final_norm)
    dk, dv, dff2 = w["dk"], w["dv"], conv_w.shape[2]

    n_meta = meta_tokens.shape[0]
    assert n_meta == N_META and seq % CHUNK == 0
    _, lat_m, kr_m, st_m, cv_m, prefix = _stream(
        meta_tokens.astype(F32), w, B=1, T=n_meta, pos=jnp.arange(n_meta, dtype=jnp.int32),
        q_off=0, s0=jnp.zeros((1, GLA_HEADS, dk, dv), F32),
        hist=jnp.zeros((1, CONV_W - 1, dff2), F32), emit_prefix=True)
    yp, lat_p, kr_p, st_p, cv_p, _ = _stream(
        x_prompt[0], w, B=1, T=seq, pos=n_meta + jnp.arange(seq, dtype=jnp.int32),
        q_off=0, s0=st_m, hist=cv_m, prefix=prefix)

    pos_s = jnp.tile(P + jnp.arange(ts, dtype=jnp.int32), bs)
    ys, lat_s, kr_s, st_s, cv_s, _ = _stream(
        x_sample.reshape(bs * ts, d), w, B=bs, T=ts, pos=pos_s, q_off=P,
        past_lat=cache_mla_latent[0], past_kr=cache_mla_krope[0], s0=state_gla[0],
        hist=cache_ffn_conv[0])

    rk = lat_p.shape[1]
    T = n_meta + seq
    return (yp,
            ys,
            jnp.concatenate([lat_m, lat_p], axis=0).reshape(1, 1, T, rk),
            jnp.concatenate([kr_m, kr_p], axis=0)[:, :MLA_ROPE].reshape(1, 1, T, MLA_ROPE),
            st_p[None],
            cv_p[None],
            lat_s.reshape(1, bs, ts, rk),
            kr_s[:, :MLA_ROPE].reshape(1, bs, ts, MLA_ROPE),
            st_s[None],
            cv_s[None])
```

```python
import functools

import jax
import jax.numpy as jnp
from jax import lax
from jax.experimental import pallas as pl
from jax.experimental.pallas import tpu as pltpu

BF16 = jnp.bfloat16
F32 = jnp.float32

CHUNK = 64
CHUNK_SHIFT = 6
N_META = 16
EPS = 1e-6
GLA_HEADS = 4
GLA_GATE_NORM = 16.0
GLA_LOG_ALPHA_MIN = -5.0
MLA_HEADS = 16
MLA_NOPE = 128
MLA_ROPE = 64
MLA_V = 128
ROPE_THETA = 10000.0
CONV_W = 3
NEG_BIG = -1e30
LOG2E = 1.4426950408889634
QK_SCALE_LOG2E = (MLA_NOPE + MLA_ROPE) ** -0.5 * LOG2E

LANE = 128
VT_ONES = 16
GLA_CHUNK = 256
SAFE_EXP = 64.0
ROW_TILE = 1024
VMEM_LIMIT = 56 * 1024 * 1024


def _cparams(sem, vmem=VMEM_LIMIT):
    return pltpu.CompilerParams(dimension_semantics=sem, vmem_limit_bytes=vmem)


def _rmsnorm(x, g):
    return x * lax.rsqrt(jnp.mean(x * x, axis=-1, keepdims=True) + EPS) * g


def _sigmoid(x):
    return 0.5 * jnp.tanh(0.5 * x) + 0.5


def _pick(n, cands):
    for c in cands:
        if n % c == 0:
            return c
    if n < min(cands):
        return n
    raise ValueError(f"no tile in {cands} divides {n}")


def _norm_cast_kernel(x_ref, g_ref, o_ref):
    o_ref[...] = _rmsnorm(x_ref[...], g_ref[...]).astype(o_ref.dtype)


def _norm_cast(x, g):
    m, d = x.shape
    tm = _pick(m, (384, 256, 128))
    return pl.pallas_call(
        _norm_cast_kernel,
        grid=(m // tm,),
        in_specs=[pl.BlockSpec((tm, d), lambda i: (i, 0)),
                  pl.BlockSpec((1, d), lambda i: (0, 0))],
        out_specs=pl.BlockSpec((tm, d), lambda i: (i, 0)),
        out_shape=jax.ShapeDtypeStruct((m, d), BF16),
        compiler_params=_cparams(("parallel",)),
        name="norm_cast",
    )(x, g.reshape(1, d))


def _matmul_kernel(a_ref, b_ref, o_ref):
    o_ref[...] = jnp.dot(a_ref[...], b_ref[...], preferred_element_type=F32).astype(o_ref.dtype)


def _matmul(a, b, out_dtype, tn):
    m, k = a.shape
    n = b.shape[1]
    tm = _pick(m, (ROW_TILE, 512, 384, 128))
    return pl.pallas_call(
        _matmul_kernel,
        grid=(n // tn, m // tm),
        in_specs=[pl.BlockSpec((tm, k), lambda j, i: (i, 0)),
                  pl.BlockSpec((k, tn), lambda j, i: (0, j))],
        out_specs=pl.BlockSpec((tm, tn), lambda j, i: (i, j)),
        out_shape=jax.ShapeDtypeStruct((m, n), out_dtype),
        compiler_params=_cparams(("parallel", "parallel")),
        name="in_proj",
    )(a, b)


_NT = (((1,), (1,)), ((), ()))


def _matmul_wt_kernel(a_ref, w_ref, o_ref, w_scr):
    @pl.when(pl.program_id(1) == 0)
    def _():
        w_scr[...] = w_ref[...].astype(BF16)

    o_ref[...] = lax.dot_general(a_ref[...], w_scr[...], _NT,
                                 preferred_element_type=F32).astype(o_ref.dtype)


def _matmul_wt(a, w_t, row0, n, out_dtype, tn):
    m, k = a.shape
    tm = _pick(m, (ROW_TILE, 512, 384, 128))
    return pl.pallas_call(
        _matmul_wt_kernel,
        grid=(n // tn, m // tm),
        in_specs=[pl.BlockSpec((tm, k), lambda j, i: (i, 0)),
                  pl.BlockSpec((pl.Element(tn), pl.Element(k)),
                               lambda j, i: (pl.multiple_of(row0 + j * tn, 16), 0))],
        out_specs=pl.BlockSpec((tm, tn), lambda j, i: (i, j)),
        out_shape=jax.ShapeDtypeStruct((m, n), out_dtype),
        scratch_shapes=[pltpu.VMEM((tn, k), BF16)],
        compiler_params=_cparams(("parallel", "arbitrary")),
        name="in_proj_wt",
    )(a, w_t)


def _small_proj_kernel(a_ref, w_ref, o_ref, w_scr, *, rank, rq, rk):
    @pl.when(pl.program_id(0) == 0)
    def _():
        w = w_ref[...].astype(BF16)
        half = MLA_ROPE // 2
        pe0 = rank + rq + rk
        o_pe = rq + rk
        w_scr[0:rq] = w[rank:rank + rq]
        w_scr[rq:o_pe] = w[rank + rq:pe0]
        w_scr[o_pe:o_pe + MLA_ROPE] = w[pe0:pe0 + MLA_ROPE]
        w_scr[o_pe + MLA_ROPE:o_pe + MLA_ROPE + half] = w[pe0 + half:pe0 + MLA_ROPE]
        w_scr[o_pe + MLA_ROPE + half:o_pe + 2 * MLA_ROPE] = w[pe0:pe0 + half]
        o_a = o_pe + 2 * MLA_ROPE
        w_scr[o_a:o_a + rank] = w[0:rank]
        w_scr[o_a + rank:] = jnp.zeros((w_scr.shape[0] - o_a - rank, w_scr.shape[1]), BF16)

    o_ref[...] = lax.dot_general(a_ref[...], w_scr[...], _NT, preferred_element_type=F32)


def _small_proj(a, w_t, row0, *, rank, rq, rk):
    m, k = a.shape
    n_in = rank + rq + rk + MLA_ROPE
    n_out = rq + rk + 2 * MLA_ROPE + LANE
    tm = _pick(m, (ROW_TILE, 512, 384, 128))
    kern = functools.partial(_small_proj_kernel, rank=rank, rq=rq, rk=rk)
    return pl.pallas_call(
        kern,
        grid=(m // tm,),
        in_specs=[pl.BlockSpec((tm, k), lambda i: (i, 0)),
                  pl.BlockSpec((pl.Element(n_in), pl.Element(k)), lambda i: (row0, 0))],
        out_specs=pl.BlockSpec((tm, n_out), lambda i: (i, 0)),
        out_shape=jax.ShapeDtypeStruct((m, n_out), F32),
        scratch_shapes=[pltpu.VMEM((n_out, k), BF16)],
        compiler_params=_cparams(("arbitrary",)),
        name="in_proj_small",
    )(a, w_t)


def _split3(x):
    a = x.astype(BF16)
    r1 = x - a.astype(F32)
    b = r1.astype(BF16)
    c = (r1 - b.astype(F32)).astype(BF16)
    return a, b, c


def _gla_kernel(q_ref, k_ref, v_ref, r_ref, ga_ref, a_ref, wa_ref, ba_ref, go_ref, s0_ref,
                o_ref, sout_ref, s_scr, *, C, SB, T, H, dk, dv):
    c_idx = pl.program_id(1)
    n_chunks = pl.num_programs(1)

    @pl.when(c_idx == 0)
    def _():
        s_scr[...] = s0_ref[0]

    z = jnp.dot(a_ref[...].astype(BF16), wa_ref[...], preferred_element_type=F32) + ba_ref[...]
    log_sig = jnp.minimum(z, 0.0) - jnp.log(1.0 + jnp.exp(-jnp.abs(z)))
    la = jnp.maximum(log_sig * (1.0 / GLA_GATE_NORM), GLA_LOG_ALPHA_MIN)
    if T % C:
        rows = c_idx * C + lax.broadcasted_iota(jnp.int32, (C, 1), 0)
        la = jnp.where(rows < T, la, 0.0)

    ri = lax.broadcasted_iota(jnp.int32, (C, C), 0)
    ci = lax.broadcasted_iota(jnp.int32, (C, C), 1)
    tri = jnp.where(ri >= ci, 1.0, 0.0).astype(BF16)
    ones = jnp.ones((C, LANE), BF16)
    cs_all = jnp.zeros_like(la)
    dsum_all = jnp.zeros((la.shape[1], LANE), F32)
    for piece in _split3(la):
        cs_all = cs_all + jnp.dot(tri, piece, preferred_element_type=F32)
        dsum_all = dsum_all + lax.dot_general(piece, ones, (((0,), (0,)), ((), ())),
                                              preferred_element_type=F32)

    sr = lax.broadcasted_iota(jnp.int32, (SB, SB), 0)
    sc = lax.broadcasted_iota(jnp.int32, (SB, SB), 1)
    causal = sr >= sc
    nt = (((1,), (1,)), ((), ()))
    scale = dk ** -0.5

    for h in range(H):
        ksl = slice(h * dk, (h + 1) * dk)
        vsl = slice(h * dv, (h + 1) * dv)
        cs = cs_all[:, ksl]
        c_last = cs[C - 1:C, :]
        q = q_ref[:, ksl].astype(F32) * scale
        k = k_ref[:, ksl].astype(F32)
        v = v_ref[:, vsl]
        s_old = s_scr[h]

        o_inter = jnp.dot((q * jnp.exp(cs)).astype(BF16), s_old.astype(BF16),
                          preferred_element_type=F32)
        k_end = (k * jnp.exp(c_last - cs)).astype(BF16)
        upd = lax.dot_general(k_end, v, (((0,), (0,)), ((), ())), preferred_element_type=F32)
        dcol = jnp.exp(dsum_all[ksl, :])
        s_scr[h] = jnp.concatenate([dcol] * (dv // LANE), axis=1) * s_old + upd

        outs = []
        for i in range(C // SB):
            lo = i * SB
            cs_i = cs[lo:lo + SB]
            q_i = q[lo:lo + SB]
            k_i = k[lo:lo + SB]
            start = cs[lo - 1:lo] if i > 0 else jnp.zeros_like(c_last)
            mid = 0.5 * (start + cs[lo + SB - 1:lo + SB])
            qd = (q_i * jnp.exp(cs_i - mid)).astype(BF16)
            kd = (k_i * jnp.exp(mid - cs_i)).astype(BF16)
            att = lax.dot_general(qd, kd, nt, preferred_element_type=F32)
            att = jnp.where(causal, att, 0.0)
            o_i = jnp.dot(att.astype(BF16), v[lo:lo + SB], preferred_element_type=F32)
            if i > 0:
                qo = (q_i * jnp.exp(cs_i - start)).astype(BF16)
                ko = (k[:lo] * jnp.exp(start - cs[:lo])).astype(BF16)
                att_o = lax.dot_general(qo, ko, nt, preferred_element_type=F32)
                o_i = o_i + jnp.dot(att_o.astype(BF16), v[:lo], preferred_element_type=F32)
            outs.append(o_i)
        o = o_inter + (jnp.concatenate(outs, axis=0) if len(outs) > 1 else outs[0])

        on = _rmsnorm(o, go_ref[...])
        r = r_ref[:, vsl].astype(F32)
        g = ga_ref[:, vsl].astype(F32)
        o_ref[:, vsl] = (_sigmoid(g) * (on * (r * _sigmoid(r)))).astype(o_ref.dtype)

    @pl.when(c_idx == n_chunks - 1)
    def _():
        sout_ref[0] = s_scr[...]


def _gla(qkvr, gates, small, wa_pad, b_a, g_out, s0, *, B, T, Tp, dk, dv, col):
    C = min(GLA_CHUNK, Tp)
    SB = min(32, C)
    nc = Tp // C
    H = GLA_HEADS
    qk, vw = H * dk, H * dv
    rb = lambda b, c: b * nc + c
    kern = functools.partial(_gla_kernel, C=C, SB=SB, T=T, H=H, dk=dk, dv=dv)
    return pl.pallas_call(
        kern,
        grid=(B, nc),
        in_specs=[
            pl.BlockSpec((C, qk), lambda b, c: (rb(b, c), col["q"] // qk)),
            pl.BlockSpec((C, qk), lambda b, c: (rb(b, c), col["k"] // qk)),
            pl.BlockSpec((C, vw), lambda b, c: (rb(b, c), col["v"] // vw)),
            pl.BlockSpec((C, vw), lambda b, c: (rb(b, c), col["r"] // vw)),
            pl.BlockSpec((C, vw), lambda b, c: (rb(b, c), col["ga"] // vw)),
            pl.BlockSpec((C, LANE), lambda b, c: (rb(b, c), col["a"] // LANE)),
            pl.BlockSpec((LANE, qk), lambda b, c: (0, 0)),
            pl.BlockSpec((1, qk), lambda b, c: (0, 0)),
            pl.BlockSpec((1, dv), lambda b, c: (0, 0)),
            pl.BlockSpec((1, H, dk, dv), lambda b, c: (b, 0, 0, 0)),
        ],
        out_specs=[
            pl.BlockSpec((C, vw), lambda b, c: (rb(b, c), 0)),
            pl.BlockSpec((1, H, dk, dv), lambda b, c: (b, 0, 0, 0)),
        ],
        out_shape=[jax.ShapeDtypeStruct((B * Tp, vw), BF16),
                   jax.ShapeDtypeStruct((B, H, dk, dv), F32)],
        scratch_shapes=[pltpu.VMEM((H, dk, dv), F32)],
        compiler_params=_cparams(("parallel", "arbitrary")),
        name="gla",
    )(qkvr, qkvr, qkvr, qkvr, gates, small, wa_pad, b_a.reshape(1, -1), g_out.reshape(1, -1), s0)


def _qprep_kernel(cq_ref, gq_ref, wn_ref, wp_ref, wps_ref, cos_ref, sin_ref, q_ref):
    hq = _rmsnorm(cq_ref[...], gq_ref[...]).astype(BF16)
    qn = jnp.dot(hq, wn_ref[...], preferred_element_type=F32)
    qp = jnp.dot(hq, wp_ref[...], preferred_element_type=F32)
    qs = jnp.dot(hq, wps_ref[...], preferred_element_type=F32)
    cos = cos_ref[...] * QK_SCALE_LOG2E
    sin = sin_ref[...] * QK_SCALE_LOG2E
    for h in range(MLA_HEADS):
        sl = slice(h * LANE, (h + 1) * LANE)
        q_ref[h, :, 0:LANE] = (qn[:, sl] * QK_SCALE_LOG2E).astype(BF16)
        q_ref[h, :, LANE:2 * LANE] = (qp[:, sl] * cos + qs[:, sl] * sin).astype(BF16)


def _qprep(small, g_q, wn, wp, wps, cos_t, sin_t, *, col):
    m = small.shape[0]
    rq = wn.shape[0]
    tm = _pick(m, (256, 128))
    full = lambda i: (0, 0)
    return pl.pallas_call(
        _qprep_kernel,
        grid=(m // tm,),
        in_specs=[pl.BlockSpec((tm, rq), lambda i: (i, col["cq"] // rq)),
                  pl.BlockSpec((1, rq), full),
                  pl.BlockSpec(wn.shape, full),
                  pl.BlockSpec(wp.shape, full),
                  pl.BlockSpec(wps.shape, full),
                  pl.BlockSpec((tm, LANE), lambda i: (i, 0)),
                  pl.BlockSpec((tm, LANE), lambda i: (i, 0))],
        out_specs=pl.BlockSpec((MLA_HEADS, tm, 2 * LANE), lambda i: (0, i, 0)),
        out_shape=jax.ShapeDtypeStruct((MLA_HEADS, m, 2 * LANE), BF16),
        compiler_params=_cparams(("parallel",)),
        name="mla_q",
    )(small, g_q.reshape(1, -1), wn, wp, wps, cos_t, sin_t)


def _lat_kernel(ckv_ref, kpe_ref, gkv_ref, cos_ref, sin_ref, lat_ref, kr_ref):
    lat_ref[...] = _rmsnorm(ckv_ref[...], gkv_ref[...])
    blk = kpe_ref[...]
    kr_ref[...] = blk * cos_ref[...] + pltpu.roll(blk, LANE // 2, 1) * sin_ref[...]


def _lat(small, g_kv, cos_t, sin_t, *, col):
    m = small.shape[0]
    rk = g_kv.shape[0]
    tm = _pick(m, (256, 128))
    return pl.pallas_call(
        _lat_kernel,
        grid=(m // tm,),
        in_specs=[pl.BlockSpec((tm, rk), lambda i: (i, col["ckv"] // rk)),
                  pl.BlockSpec((tm, LANE), lambda i: (i, col["kpe"] // LANE)),
                  pl.BlockSpec((1, rk), lambda i: (0, 0)),
                  pl.BlockSpec((tm, LANE), lambda i: (i, 0)),
                  pl.BlockSpec((tm, LANE), lambda i: (i, 0))],
        out_specs=[pl.BlockSpec((tm, rk), lambda i: (i, 0)),
                   pl.BlockSpec((tm, LANE), lambda i: (i, 0))],
        out_shape=[jax.ShapeDtypeStruct((m, rk), F32),
                   jax.ShapeDtypeStruct((m, LANE), F32)],
        compiler_params=_cparams(("parallel",)),
        name="mla_latent",
    )(small, small, g_kv.reshape(1, -1), cos_t, sin_t)


def _kvup_kernel(lat_ref, kr_ref, wuk_ref, wuv_ref, k_ref, v_ref, *, v_transposed):
    lat = lat_ref[...].astype(BF16)
    kn = jnp.dot(lat, wuk_ref[...], preferred_element_type=F32)
    kr = kr_ref[...]
    lane = lax.broadcasted_iota(jnp.int32, kr.shape, 1)
    kp = jnp.where(lane == MLA_ROPE, 1.0, kr).astype(BF16)
    if v_transposed:
        vv = lax.dot_general(wuv_ref[...], lat, (((1,), (1,)), ((), ())),
                             preferred_element_type=F32)
    else:
        vv = jnp.dot(lat, wuv_ref[...], preferred_element_type=F32)
    for h in range(MLA_HEADS):
        sl = slice(h * LANE, (h + 1) * LANE)
        k_ref[h, :, 0:LANE] = kn[:, sl].astype(BF16)
        k_ref[h, :, LANE:2 * LANE] = kp
        if v_transposed:
            v_ref[h, 0:LANE, :] = vv[sl, :].astype(BF16)
            v_ref[h, LANE:LANE + VT_ONES, :] = jnp.ones((VT_ONES, vv.shape[1]), BF16)
        else:
            v_ref[h] = vv[:, sl].astype(BF16)


def _kvup(lat, kr, wuk, wuv, *, v_transposed=False):
    m, rk = lat.shape
    tm = _pick(m, (512, 256, 128))
    full = lambda i: (0, 0)
    if v_transposed:
        v_spec = pl.BlockSpec((MLA_HEADS, LANE + VT_ONES, tm), lambda i: (0, 0, i))
        v_shape = (MLA_HEADS, LANE + VT_ONES, m)
    else:
        v_spec = pl.BlockSpec((MLA_HEADS, tm, LANE), lambda i: (0, i, 0))
        v_shape = (MLA_HEADS, m, LANE)
    return pl.pallas_call(
        functools.partial(_kvup_kernel, v_transposed=v_transposed),
        grid=(m // tm,),
        in_specs=[pl.BlockSpec((tm, rk), lambda i: (i, 0)),
                  pl.BlockSpec((tm, LANE), lambda i: (i, 0)),
                  pl.BlockSpec(wuk.shape, full),
                  pl.BlockSpec(wuv.shape, full)],
        out_specs=[pl.BlockSpec((MLA_HEADS, tm, 2 * LANE), lambda i: (0, i, 0)), v_spec],
        out_shape=[jax.ShapeDtypeStruct((MLA_HEADS, m, 2 * LANE), BF16),
                   jax.ShapeDtypeStruct(v_shape, BF16)],
        compiler_params=_cparams(("parallel",)),
        name="mla_kv",
    )(lat, kr, wuk, wuv)


def _last_kblock(qi, *, tq, tk, nk, q_off, k_off):
    top_chunk = ((qi + 1) * tq - 1 + q_off) // CHUNK
    last_key = (top_chunk + 1) * CHUNK - 1 - k_off
    return jnp.minimum(last_key // tk, nk - 1)


def _attn_kernel(q_ref, k_ref, v_ref, o_ref, m_scr, l_scr, acc_scr, *, hps, tq, tk, nk,
                 q_off, k_off):
    qi = pl.program_id(2)
    ki = pl.program_id(3)

    @pl.when(ki == 0)
    def _():
        m_scr[...] = jnp.full(m_scr.shape, NEG_BIG, F32)
        l_scr[...] = jnp.zeros(l_scr.shape, F32)
        acc_scr[...] = jnp.zeros(acc_scr.shape, F32)

    @pl.when(ki <= _last_kblock(qi, tq=tq, tk=tk, nk=nk, q_off=q_off, k_off=k_off))
    def _():
        q_chunk = (qi * tq + q_off + lax.broadcasted_iota(jnp.int32, (tq, 1), 0)) >> CHUNK_SHIFT
        k_chunk = (ki * tk + k_off + lax.broadcasted_iota(jnp.int32, (1, tk), 1)) >> CHUNK_SHIFT
        visible = q_chunk >= k_chunk

        def head(h, carry):
            s = lax.dot_general(q_ref[h], k_ref[h], (((1,), (1,)), ((), ())),
                                preferred_element_type=F32)
            s = jnp.where(visible, s, NEG_BIG)
            m_prev = m_scr[h]
            m_new = jnp.maximum(m_prev, jnp.max(s, axis=-1, keepdims=True))
            p = jnp.exp2(s - m_new)
            alpha = jnp.exp2(m_prev - m_new)
            l_scr[h] = alpha * l_scr[h] + jnp.sum(p, axis=-1, keepdims=True)
            acc_scr[h] = alpha * acc_scr[h] + jnp.dot(p.astype(BF16), v_ref[h],
                                                      preferred_element_type=F32)
            m_scr[h] = m_new
            return carry

        lax.fori_loop(0, hps, head, 0)

    @pl.when(ki == nk - 1)
    def _():
        for h in range(hps):
            o_ref[:, h * LANE:(h + 1) * LANE] = (acc_scr[h] / l_scr[h]).astype(o_ref.dtype)


def _attention(q, k, v, *, B, Tq, Tk, tq, tk, hps, q_off, k_off):
    nq = Tq // tq
    nk = Tk // tk
    hg = MLA_HEADS // hps
    dqk = q.shape[2]
    dvh = v.shape[2]
    last = functools.partial(_last_kblock, tq=tq, tk=tk, nk=nk, q_off=q_off, k_off=k_off)
    kern = functools.partial(_attn_kernel, hps=hps, tq=tq, tk=tk, nk=nk, q_off=q_off,
                             k_off=k_off)
    kv_row = lambda b, g, i, j: b * nk + jnp.minimum(j, last(i))
    return pl.pallas_call(
        kern,
        grid=(B, hg, nq, nk),
        in_specs=[pl.BlockSpec((hps, tq, dqk), lambda b, g, i, j: (g, b * nq + i, 0)),
                  pl.BlockSpec((hps, tk, dqk), lambda b, g, i, j: (g, kv_row(b, g, i, j), 0)),
                  pl.BlockSpec((hps, tk, dvh), lambda b, g, i, j: (g, kv_row(b, g, i, j), 0))],
        out_specs=pl.BlockSpec((tq, hps * dvh), lambda b, g, i, j: (b * nq + i, g)),
        out_shape=jax.ShapeDtypeStruct((B * Tq, MLA_HEADS * dvh), BF16),
        scratch_shapes=[pltpu.VMEM((hps, tq, 1), F32),
                        pltpu.VMEM((hps, tq, 1), F32),
                        pltpu.VMEM((hps, tq, dvh), F32)],
        compiler_params=_cparams(("parallel", "parallel", "parallel", "arbitrary")),
        name="mla_attn",
    )(q, k, v)


def _attn_t_kernel(qi_ref, ki_ref, q_ref, k_ref, vt_ref, kp_ref, vtp_ref, o_ref,
                   q_scr, r_scr, acc_scr, *, hps, t):
    pair = pl.program_id(1)
    qi = qi_ref[pair]
    ki = ki_ref[pair]
    nt = (((1,), (1,)), ((), ()))
    pe = slice(LANE, 2 * LANE)
    lane = lax.broadcasted_iota(jnp.int32, (t, LANE), 1)

    def set_reference(h, r):
        neg_r = jnp.transpose(jnp.broadcast_to(-r, (LANE, t)))
        q_scr[h, :, pe] = jnp.where(lane == MLA_ROPE, neg_r.astype(BF16), q_ref[h, :, pe])
        r_scr[h] = r

    def shifted_scores(h):
        return lax.dot_general(k_ref[h], q_scr[h], nt, preferred_element_type=F32)

    @pl.when(ki == 0)
    def _():
        for h in range(hps):
            q_scr[h, :, 0:LANE] = q_ref[h, :, 0:LANE]
            s = lax.dot_general(kp_ref[h], q_ref[h], nt, preferred_element_type=F32)
            r = jnp.max(s, axis=0, keepdims=True).astype(BF16).astype(F32)
            p = jnp.exp2((s - r).astype(BF16))
            acc_scr[h] = jnp.dot(vtp_ref[h], p, preferred_element_type=F32)
            set_reference(h, r)

    def general(h, bias, keep_reference):
        sp = shifted_scores(h)
        if bias is not None:
            sp = sp + bias
        r = r_scr[h]
        rise = jnp.maximum(jnp.max(sp, axis=0, keepdims=True), 0.0)
        r_new = (r + rise).astype(BF16).astype(F32)
        delta = r_new - r
        p = jnp.exp2((sp - delta).astype(BF16))
        acc_scr[h] = jnp.exp2(-delta) * acc_scr[h] + jnp.dot(vt_ref[h], p,
                                                               preferred_element_type=F32)
        if keep_reference:
            set_reference(h, r_new)

    @pl.when(ki < qi)
    def _():
        unsafe = []
        sp_next = shifted_scores(0)
        for h in range(hps):
            sp = sp_next
            if h + 1 < hps:
                sp_next = shifted_scores(h + 1)
            safe = jnp.max(sp) <= SAFE_EXP
            part = jnp.dot(vt_ref[h], jnp.exp2(sp.astype(BF16)), preferred_element_type=F32)
            acc_scr[h] += jnp.where(safe, part, 0.0)
            unsafe.append(jnp.logical_not(safe))

        @pl.when(functools.reduce(jnp.logical_or, unsafe))
        def _():
            for h in range(hps):
                @pl.when(unsafe[h])
                def _():
                    general(h, None, True)

    @pl.when(ki == qi)
    def _():
        k_chunk = (ki * t + lax.broadcasted_iota(jnp.int32, (t, 1), 0)) >> CHUNK_SHIFT
        q_chunk = (qi * t + lax.broadcasted_iota(jnp.int32, (1, t), 1)) >> CHUNK_SHIFT
        bias = jnp.where(q_chunk >= k_chunk, 0.0, NEG_BIG)
        for h in range(hps):
            general(h, bias, False)
            acc = acc_scr[h]
            o_t = acc[0:LANE] / acc[LANE:LANE + 1]
            o_ref[:, h * LANE:(h + 1) * LANE] = o_t.T.astype(o_ref.dtype)


def _attention_t(q, k, vt, k_pre, vt_pre, *, T, t, hps):
    n = T // t
    hg = MLA_HEADS // hps
    dqk = q.shape[2]
    npre = k_pre.shape[1]
    vrows = vt.shape[1]
    pairs = [(i, j) for i in range(n) for j in range(i + 1)]
    qi_arr = jnp.asarray([p[0] for p in pairs], jnp.int32)
    ki_arr = jnp.asarray([p[1] for p in pairs], jnp.int32)
    kern = functools.partial(_attn_t_kernel, hps=hps, t=t)
    grid_spec = pltpu.PrefetchScalarGridSpec(
        num_scalar_prefetch=2,
        grid=(hg, len(pairs)),
        in_specs=[pl.BlockSpec((hps, t, dqk), lambda g, p, qi, ki: (g, qi[p], 0)),
                  pl.BlockSpec((hps, t, dqk), lambda g, p, qi, ki: (g, ki[p], 0)),
                  pl.BlockSpec((hps, vrows, t), lambda g, p, qi, ki: (g, 0, ki[p])),
                  pl.BlockSpec((hps, npre, dqk), lambda g, p, qi, ki: (g, 0, 0)),
                  pl.BlockSpec((hps, vrows, npre), lambda g, p, qi, ki: (g, 0, 0))],
        out_specs=pl.BlockSpec((t, hps * LANE), lambda g, p, qi, ki: (qi[p], g)),
        scratch_shapes=[pltpu.VMEM((hps, t, dqk), BF16),
                        pltpu.VMEM((hps, 1, t), F32),
                        pltpu.VMEM((hps, vrows, t), F32)])
    return pl.pallas_call(
        kern,
        grid_spec=grid_spec,
        out_shape=jax.ShapeDtypeStruct((T, MLA_HEADS * LANE), BF16),
        compiler_params=_cparams(("parallel", "arbitrary")),
        name="mla_attn_t",
    )(qi_arr, ki_arr, q, k, vt, k_pre, vt_pre)


def _absorb_q_kernel(q_ref, w_ref, o_ref):
    o_ref[0] = jnp.dot(q_ref[0, :, 0:MLA_NOPE], w_ref[0],
                       preferred_element_type=F32).astype(o_ref.dtype)


def _absorb_q(q, w_uk_t3):
    heads, rows, dqk = q.shape
    rk = w_uk_t3.shape[2]
    return pl.pallas_call(
        _absorb_q_kernel,
        grid=(heads,),
        in_specs=[pl.BlockSpec((1, rows, dqk), lambda h: (h, 0, 0)),
                  pl.BlockSpec((1, MLA_NOPE, rk), lambda h: (h, 0, 0))],
        out_specs=pl.BlockSpec((1, rows, rk), lambda h: (h, 0, 0)),
        out_shape=jax.ShapeDtypeStruct((heads, rows, rk), BF16),
        compiler_params=_cparams(("parallel",)),
        name="mla_absorb_q",
    )(q, w_uk_t3)


def _attn_latent_kernel(ql_ref, q_ref, plat_ref, pkr_ref, lat_ref, kr_ref, o_ref, *, T, P):
    heads, _, rk = ql_ref.shape
    rows = heads * T
    nt = (((1,), (1,)), ((), ()))
    ql = ql_ref[...].reshape(rows, rk)
    qpe = q_ref[:, :, LANE:2 * LANE].reshape(rows, LANE)[:, 0:MLA_ROPE]
    lat_all = jnp.concatenate([plat_ref[0].astype(BF16), lat_ref[...].astype(BF16)], axis=0)
    kr_all = jnp.concatenate([pkr_ref[0], kr_ref[:, 0:MLA_ROPE]], axis=0).astype(BF16)
    s = (lax.dot_general(ql, lat_all, nt, preferred_element_type=F32)
         + lax.dot_general(qpe, kr_all, nt, preferred_element_type=F32))
    tok = lax.rem(lax.broadcasted_iota(jnp.int32, (rows, 1), 0), T)
    q_chunk = (P + tok) >> CHUNK_SHIFT
    k_chunk = lax.broadcasted_iota(jnp.int32, (1, P + T), 1) >> CHUNK_SHIFT
    s = jnp.where(q_chunk >= k_chunk, s, NEG_BIG)
    p = jnp.exp2(s - jnp.max(s, axis=-1, keepdims=True))
    o = jnp.dot(p.astype(BF16), lat_all, preferred_element_type=F32)
    o = o / jnp.sum(p, axis=-1, keepdims=True)
    o_ref[...] = o.reshape(heads, T, rk).astype(o_ref.dtype)


def _attn_latent(qlat, q, past_lat, past_kr, lat, kr, *, B, T):
    heads, _, rk = qlat.shape
    P = past_lat.shape[1]
    kern = functools.partial(_attn_latent_kernel, T=T, P=P)
    return pl.pallas_call(
        kern,
        grid=(B,),
        in_specs=[pl.BlockSpec((heads, T, rk), lambda b: (0, b, 0)),
                  pl.BlockSpec((heads, T, q.shape[2]), lambda b: (0, b, 0)),
                  pl.BlockSpec((1, P, rk), lambda b: (b, 0, 0)),
                  pl.BlockSpec((1, P, past_kr.shape[2]), lambda b: (b, 0, 0)),
                  pl.BlockSpec((T, rk), lambda b: (b, 0)),
                  pl.BlockSpec((T, LANE), lambda b: (b, 0))],
        out_specs=pl.BlockSpec((heads, T, rk), lambda b: (0, b, 0)),
        out_shape=jax.ShapeDtypeStruct((heads, B * T, rk), BF16),
        compiler_params=_cparams(("parallel",)),
        name="mla_attn_latent",
    )(qlat, q, past_lat, past_kr, lat, kr)


def _absorb_out_kernel(o_ref, w_ref, out_ref):
    out_ref[...] = jnp.dot(o_ref[0], w_ref[0], preferred_element_type=F32).astype(out_ref.dtype)


def _absorb_out(olat, w_uv3):
    heads, rows, rk = olat.shape
    dvh = w_uv3.shape[2]
    return pl.pallas_call(
        _absorb_out_kernel,
        grid=(heads,),
        in_specs=[pl.BlockSpec((1, rows, rk), lambda h: (h, 0, 0)),
                  pl.BlockSpec((1, rk, dvh), lambda h: (h, 0, 0))],
        out_specs=pl.BlockSpec((rows, dvh), lambda h: (0, h)),
        out_shape=jax.ShapeDtypeStruct((rows, heads * dvh), BF16),
        compiler_params=_cparams(("parallel",)),
        name="mla_absorb_out",
    )(olat, w_uv3)


def _merge_kernel(a_ref, gb_ref, om_ref, x_ref, wo_ref, gf_ref, x1_ref, h2_ref):
    merged = a_ref[...].astype(F32) + _sigmoid(gb_ref[...].astype(F32)) * om_ref[...].astype(F32)
    x1 = x_ref[...] + jnp.dot(merged.astype(BF16), wo_ref[...], preferred_element_type=F32)
    x1_ref[...] = x1
    h2_ref[...] = _rmsnorm(x1, gf_ref[...]).astype(BF16)


def _merge(branch_a, gates, o_m, x, wo, g_ffn, *, col):
    m, d = x.shape
    tm = _pick(m, (512, 384, 256, 128))
    row = lambda i: (i, 0)
    return pl.pallas_call(
        _merge_kernel,
        grid=(m // tm,),
        in_specs=[pl.BlockSpec((tm, d), row),
                  pl.BlockSpec((tm, d), lambda i: (i, col["gb"] // d)),
                  pl.BlockSpec((tm, d), row),
                  pl.BlockSpec((tm, d), row),
                  pl.BlockSpec(wo.shape, lambda i: (0, 0), pipeline_mode=pl.Buffered(1)),
                  pl.BlockSpec((1, d), lambda i: (0, 0))],
        out_specs=[pl.BlockSpec((tm, d), row), pl.BlockSpec((tm, d), row)],
        out_shape=[jax.ShapeDtypeStruct((m, d), F32), jax.ShapeDtypeStruct((m, d), BF16)],
        compiler_params=_cparams(("parallel",)),
        name="merge_out_proj",
    )(branch_a, gates, o_m, x, wo, g_ffn.reshape(1, -1))


HALO = 8


def _ffn_up_kernel(h_ref, wa_ref, wb_ref, cwa_ref, cwb_ref, cba_ref, cbb_ref, ha_ref, hb_ref,
                   act_ref, ca_ref, cb_ref, ext_scr, carry_scr, w_scr,
                   *, bb, r, tf, loc, carried):
    s = pl.program_id(1)
    rt = pl.program_id(2)
    d = h_ref.shape[2]

    @pl.when((s == 0) & (rt == 0))
    def _():
        w_scr[0] = wa_ref[...].astype(BF16)
        w_scr[1] = wb_ref[...].astype(BF16)

    if carried:
        @pl.when(rt == 0)
        def _():
            carry_scr[0] = ha_ref[...]
            carry_scr[1] = hb_ref[...]

    h = h_ref[...].reshape(bb * r, d)
    conv = []
    for half, (cw_ref, cbias_ref, hist_ref, cout_ref) in enumerate(
            ((cwa_ref, cba_ref, ha_ref, ca_ref), (cwb_ref, cbb_ref, hb_ref, cb_ref))):
        u = jnp.dot(h, w_scr[half], preferred_element_type=F32).reshape(bb, r, tf)
        ext_scr[half, :, HALO:HALO + r, :] = u
        ext_scr[half, :, HALO - 2:HALO, :] = carry_scr[half] if carried else hist_ref[...]
        u1 = ext_scr[half, :, HALO - 1:HALO - 1 + r, :]
        u2 = ext_scr[half, :, HALO - 2:HALO - 2 + r, :]
        cw = cw_ref[...]
        conv.append(cbias_ref[...] + cw[0:1] * u2 + cw[1:2] * u1 + cw[2:3] * u)
        if carried:
            carry_scr[half] = ext_scr[half, :, HALO + r - 2:HALO + r, :]
        cout_ref[0] = ext_scr[half, :, HALO + loc:HALO + loc + 2, :]

    act_ref[...] = ((conv[0] * _sigmoid(conv[0])) * conv[1]).astype(act_ref.dtype)


def _ffn_down_kernel(act_ref, wd_ref, x1_ref, gf_ref, y_ref):
    k = pl.program_id(1)

    @pl.when(k == 0)
    def _():
        y_ref[...] = jnp.zeros(y_ref.shape, F32)

    y_ref[...] += jnp.dot(act_ref[...], wd_ref[...], preferred_element_type=F32)

    @pl.when(k == pl.num_programs(1) - 1)
    def _():
        y_ref[...] = _rmsnorm(x1_ref[...] + y_ref[...], gf_ref[...])


def _ffn(h2, x1, w_up, w_down, conv_w, conv_b, hist, g_final, *, B, T, Tp):
    d = h2.shape[1]
    dff = w_down.shape[0]
    tf = _pick(dff, (512, 256, 128))
    nf = dff // tf
    if Tp <= 128:
        bb, r = B, Tp
    else:
        bb, r = 1, _pick(Tp, (ROW_TILE, 128))
    nrt = Tp // r
    carried = nrt > 1
    loc = (T - 2) - (nrt - 1) * r
    assert 0 <= loc <= r - 2, "final two valid rows must sit in the last row tile"
    kern = functools.partial(_ffn_up_kernel, bb=bb, r=r, tf=tf, loc=loc, carried=carried)
    carry_shape = (2, bb, 2, tf) if carried else (1, 1, 2, LANE)
    act, ca, cb = pl.pallas_call(
        kern,
        grid=(nf, B // bb, nrt),
        in_specs=[pl.BlockSpec((bb, r, d), lambda f, s, t: (s, t, 0)),
                  pl.BlockSpec((d, tf), lambda f, s, t: (0, f)),
                  pl.BlockSpec((d, tf), lambda f, s, t: (0, nf + f)),
                  pl.BlockSpec((CONV_W, tf), lambda f, s, t: (0, f)),
                  pl.BlockSpec((CONV_W, tf), lambda f, s, t: (0, nf + f)),
                  pl.BlockSpec((1, tf), lambda f, s, t: (0, f)),
                  pl.BlockSpec((1, tf), lambda f, s, t: (0, nf + f)),
                  pl.BlockSpec((bb, 2, tf), lambda f, s, t: (s, 0, f)),
                  pl.BlockSpec((bb, 2, tf), lambda f, s, t: (s, 0, nf + f))],
        out_specs=[pl.BlockSpec((bb, r, tf), lambda f, s, t: (s, t, f)),
                   pl.BlockSpec((1, bb, 2, tf), lambda f, s, t: (t, s, 0, f)),
                   pl.BlockSpec((1, bb, 2, tf), lambda f, s, t: (t, s, 0, f))],
        out_shape=[jax.ShapeDtypeStruct((B, Tp, dff), BF16),
                   jax.ShapeDtypeStruct((nrt, B, 2, dff), F32),
                   jax.ShapeDtypeStruct((nrt, B, 2, dff), F32)],
        scratch_shapes=[pltpu.VMEM((2, bb, HALO + r, tf), F32),
                        pltpu.VMEM(carry_shape, F32),
                        pltpu.VMEM((2, d, tf), BF16)],
        compiler_params=_cparams(("arbitrary", "arbitrary", "arbitrary")),
        name="conv_ffn_up",
    )(h2.reshape(B, Tp, d), w_up, w_up, conv_w, conv_w, conv_b.reshape(1, -1),
      conv_b.reshape(1, -1), hist, hist)

    m = B * Tp
    tm = _pick(m, (512, 256, 128))
    kc = _pick(dff, (1408, 512, 256, 128))
    y = pl.pallas_call(
        _ffn_down_kernel,
        grid=(m // tm, dff // kc),
        in_specs=[pl.BlockSpec((tm, kc), lambda i, k: (i, k)),
                  pl.BlockSpec((kc, d), lambda i, k: (k, 0)),
                  pl.BlockSpec((tm, d), lambda i, k: (i, 0)),
                  pl.BlockSpec((1, d), lambda i, k: (0, 0))],
        out_specs=pl.BlockSpec((tm, d), lambda i, k: (i, 0)),
        out_shape=jax.ShapeDtypeStruct((m, d), F32),
        compiler_params=_cparams(("parallel", "arbitrary")),
        name="ffn_down",
    )(act.reshape(m, dff), w_down, x1, g_final.reshape(1, -1))
    return y.reshape(B, Tp, d), jnp.concatenate([ca[nrt - 1], cb[nrt - 1]], axis=-1)


def _rope_tables(pos):
    half = MLA_ROPE // 2
    inv = ROPE_THETA ** (-jnp.arange(0, MLA_ROPE, 2, dtype=F32) / MLA_ROPE)
    ang = pos.astype(F32)[:, None] * inv[None, :]
    cos, sin = jnp.cos(ang), jnp.sin(ang)
    zero = jnp.zeros((pos.shape[0], LANE - 2 * half), F32)
    return (jnp.concatenate([cos, cos, zero], axis=1),
            jnp.concatenate([-sin, sin, zero], axis=1))


def _stream(x, w, *, B, T, pos, q_off, s0, hist, past_lat=None, past_kr=None, prefix=None,
            emit_prefix=False):
    col = w["col"]
    dk, dv = w["dk"], w["dv"]
    h = _norm_cast(x, w["g_mix"])
    rows = w["in_rows"]
    qkvr = _matmul_wt(h, w["w_in_t"], rows["q"], rows["a"] - rows["q"], BF16, tn=1024)
    gates = _matmul_wt(h, w["w_in_t"], rows["ga"], rows["end"] - rows["ga"], BF16, tn=1024)
    small = _small_proj(h, w["w_in_t"], rows["a"], rank=rows["cq"] - rows["a"],
                        rq=rows["ckv"] - rows["cq"], rk=rows["kpe"] - rows["ckv"])

    branch_a, state = _gla(qkvr, gates, small, w["wa_pad"], w["b_a"], w["g_gla_out"], s0,
                           B=B, T=T, Tp=T, dk=dk, dv=dv, col=col)

    cos_t, sin_t = _rope_tables(pos)
    q = _qprep(small, w["g_q"], w["wq_nope"], w["wq_pe"], w["wq_pe_sw"], cos_t, sin_t, col=col)
    lat, kr = _lat(small, w["g_kv"], cos_t, sin_t, col=col)
    own_prefix = None
    if prefix is not None:
        assert B == 1 and past_lat is None
        k, vt = _kvup(lat, kr, w["w_uk"], w["w_uv_t"], v_transposed=True)
        o_m = _attention_t(q, k, vt, prefix[0], prefix[1], T=T, t=_pick(T, (1024, 128)),
                           hps=MLA_HEADS // 4)
    elif past_lat is not None:
        qlat = _absorb_q(q, w["w_uk_t3"])
        olat = _attn_latent(qlat, q, past_lat, past_kr, lat, kr, B=B, T=T)
        o_m = _absorb_out(olat, w["w_uv3"])
    else:
        k, v = _kvup(lat, kr, w["w_uk"], w["w_uv"])
        if emit_prefix:
            own_prefix = _kvup(lat, kr, w["w_uk"], w["w_uv_t"], v_transposed=True)
        o_m = _attention(q, k, v, B=B, Tq=T, Tk=T, tq=T, tk=T, hps=MLA_HEADS,
                         q_off=q_off, k_off=0)

    x1, h2 = _merge(branch_a, gates, o_m, x, w["w_o"], w["g_ffn"], col=col)
    y, conv = _ffn(h2, x1, w["w_up"], w["w_down"], w["conv_w"], w["conv_b"], hist,
                   w["final_norm"], B=B, T=T, Tp=T)
    return y, lat, kr, state, conv, own_prefix


def _prep_weights(g_mix, w_in, w_a2, b_a, g_gla_out, g_q, w_uq, g_kv, w_uk, w_uv, w_o,
                  g_ffn, w_up, conv_w, conv_b, w_down, final_norm):
    d = w_in.shape[0]
    rank, gqk = w_a2.shape
    gvw = GLA_HEADS * g_gla_out.shape[0]
    rq, rk = g_q.shape[0], g_kv.shape[0]
    half = MLA_ROPE // 2
    o, offs = 0, {}
    for name, width in (("q", gqk), ("k", gqk), ("v", gvw), ("r", gvw), ("a", rank),
                        ("cq", rq), ("ckv", rk), ("kpe", MLA_ROPE), ("ga", d), ("gb", d)):
        offs[name] = (o, o + width)
        o += width
    assert o == w_in.shape[1]
    in_rows = {name: lo for name, (lo, _) in offs.items()}
    in_rows["end"] = o
    assert all(v % 16 == 0 for v in in_rows.values())
    col = {"q": 0, "k": gqk, "v": 2 * gqk, "r": 2 * gqk + gvw, "ga": 0, "gb": d,
           "cq": 0, "ckv": rq, "kpe": rq + rk, "a": rq + rk + 2 * MLA_ROPE}

    w3 = w_uq.reshape(rq, MLA_HEADS, MLA_NOPE + MLA_ROPE)
    pe = w3[:, :, MLA_NOPE:]
    pe_sw = jnp.concatenate([pe[:, :, half:], pe[:, :, :half]], axis=2)
    zpad = jnp.zeros((rq, MLA_HEADS, LANE - MLA_ROPE), w_uq.dtype)
    flat = lambda t: t.reshape(rq, -1).astype(BF16)
    wa_pad = jnp.concatenate([w_a2, jnp.zeros((LANE - rank, gqk), w_a2.dtype)], axis=0)
    return dict(
        col=col, dk=gqk // GLA_HEADS, dv=g_gla_out.shape[0],
        g_mix=g_mix, w_in_t=jnp.swapaxes(w_in, 0, 1), in_rows=in_rows,
        wa_pad=wa_pad.astype(BF16), b_a=b_a, g_gla_out=g_gla_out, g_q=g_q,
        wq_nope=flat(w3[:, :, :MLA_NOPE]),
        wq_pe=flat(jnp.concatenate([pe, zpad], axis=2)),
        wq_pe_sw=flat(jnp.concatenate([pe_sw, zpad], axis=2)),
        g_kv=g_kv, w_uk=w_uk.astype(BF16), w_uv=w_uv.astype(BF16),
        w_uv_t=w_uv.T.astype(BF16),
        w_uk_t3=w_uk.reshape(rk, MLA_HEADS, MLA_NOPE).transpose(1, 2, 0).astype(BF16),
        w_uv3=w_uv.reshape(rk, MLA_HEADS, MLA_V).transpose(1, 0, 2).astype(BF16),
        w_o=w_o.astype(BF16),
        g_ffn=g_ffn, w_up=w_up, conv_w=conv_w, conv_b=conv_b,
        w_down=w_down.astype(BF16), final_norm=final_norm)


def kernel(x_prompt, x_sample, cache_mla_latent, cache_mla_krope, state_gla, cache_ffn_conv,
           meta_tokens, g_mix, w_in, w_a2, b_a, g_gla_out, g_q, w_uq, g_kv, w_uk, w_uv, w_o,
           g_ffn, w_up, conv_w, conv_b, w_down, final_norm):
    assert w_in.shape[0] == 1, "single trunk layer"
    bp, seq, d = x_prompt.shape
    assert bp == 1
    bs, ts, _ = x_sample.shape
    P = cache_mla_latent.shape[2]
    w = _prep_weights(g_mix[0], w_in[0], w_a2[0], b_a[0], g_gla_out[0], g_q[0], w_uq[0],
                      g_kv[0], w_uk[0], w_uv[0], w_o[0], g_ffn[0], w_up[0], conv_w[0],
                      conv_b[0], w_down[0], final_norm)
    dk, dv, dff2 = w["dk"], w["dv"], conv_w.shape[2]

    n_meta = meta_tokens.shape[0]
    assert n_meta == N_META and seq % CHUNK == 0
    _, lat_m, kr_m, st_m, cv_m, prefix = _stream(
        meta_tokens.astype(F32), w, B=1, T=n_meta, pos=jnp.arange(n_meta, dtype=jnp.int32),
        q_off=0, s0=jnp.zeros((1, GLA_HEADS, dk, dv), F32),
        hist=jnp.zeros((1, CONV_W - 1, dff2), F32), emit_prefix=True)
    yp, lat_p, kr_p, st_p, cv_p, _ = _stream(
        x_prompt[0], w, B=1, T=seq, pos=n_meta + jnp.arange(seq, dtype=jnp.int32),
        q_off=0, s0=st_m, hist=cv_m, prefix=prefix)

    pos_s = jnp.tile(P + jnp.arange(ts, dtype=jnp.int32), bs)
    ys, lat_s, kr_s, st_s, cv_s, _ = _stream(
        x_sample.reshape(bs * ts, d), w, B=bs, T=ts, pos=pos_s, q_off=P,
        past_lat=cache_mla_latent[0], past_kr=cache_mla_krope[0], s0=state_gla[0],
        hist=cache_ffn_conv[0])

    rk = lat_p.shape[1]
    T = n_meta + seq
    return (yp,
            ys,
            jnp.concatenate([lat_m, lat_p], axis=0).reshape(1, 1, T, rk),
            jnp.concatenate([kr_m, kr_p], axis=0)[:, :MLA_ROPE].reshape(1, 1, T, MLA_ROPE),
            st_p[None],
            cv_p[None],
            lat_s.reshape(1, bs, ts, rk),
            kr_s[:, :MLA_ROPE].reshape(1, bs, ts, MLA_ROPE),
            st_s[None],
            cv_s[None])
```

```python
import functools

import jax
import jax.numpy as jnp
from jax import lax
from jax.experimental import pallas as pl
from jax.experimental.pallas import tpu as pltpu

BF16 = jnp.bfloat16
F32 = jnp.float32

CHUNK = 64
CHUNK_SHIFT = 6
N_META = 16
EPS = 1e-6
GLA_HEADS = 4
GLA_GATE_NORM = 16.0
GLA_LOG_ALPHA_MIN = -5.0
MLA_HEADS = 16
MLA_NOPE = 128
MLA_ROPE = 64
MLA_V = 128
ROPE_THETA = 10000.0
CONV_W = 3
NEG_BIG = -1e30
LOG2E = 1.4426950408889634
QK_SCALE_LOG2E = (MLA_NOPE + MLA_ROPE) ** -0.5 * LOG2E

LANE = 128
VT_ONES = 16
GLA_CHUNK = 256
FFN_SUB = 512
SAFE_EXP = 64.0
ROW_TILE = 1024
VMEM_LIMIT = 56 * 1024 * 1024


def _cparams(sem, vmem=VMEM_LIMIT):
    return pltpu.CompilerParams(dimension_semantics=sem, vmem_limit_bytes=vmem)


def _rmsnorm(x, g):
    return x * lax.rsqrt(jnp.mean(x * x, axis=-1, keepdims=True) + EPS) * g


def _sigmoid(x):
    return 0.5 * jnp.tanh(0.5 * x) + 0.5


def _pick(n, cands):
    for c in cands:
        if n % c == 0:
            return c
    if n < min(cands):
        return n
    raise ValueError(f"no tile in {cands} divides {n}")


def _norm_cast_kernel(x_ref, g_ref, o_ref):
    o_ref[...] = _rmsnorm(x_ref[...], g_ref[...]).astype(o_ref.dtype)


def _norm_cast(x, g):
    m, d = x.shape
    tm = _pick(m, (384, 256, 128))
    return pl.pallas_call(
        _norm_cast_kernel,
        grid=(m // tm,),
        in_specs=[pl.BlockSpec((tm, d), lambda i: (i, 0)),
                  pl.BlockSpec((1, d), lambda i: (0, 0))],
        out_specs=pl.BlockSpec((tm, d), lambda i: (i, 0)),
        out_shape=jax.ShapeDtypeStruct((m, d), BF16),
        compiler_params=_cparams(("parallel",)),
        name="norm_cast",
    )(x, g.reshape(1, d))


def _matmul_kernel(a_ref, b_ref, o_ref):
    o_ref[...] = jnp.dot(a_ref[...], b_ref[...], preferred_element_type=F32).astype(o_ref.dtype)


def _matmul(a, b, out_dtype, tn):
    m, k = a.shape
    n = b.shape[1]
    tm = _pick(m, (ROW_TILE, 512, 384, 128))
    return pl.pallas_call(
        _matmul_kernel,
        grid=(n // tn, m // tm),
        in_specs=[pl.BlockSpec((tm, k), lambda j, i: (i, 0)),
                  pl.BlockSpec((k, tn), lambda j, i: (0, j))],
        out_specs=pl.BlockSpec((tm, tn), lambda j, i: (i, j)),
        out_shape=jax.ShapeDtypeStruct((m, n), out_dtype),
        compiler_params=_cparams(("parallel", "parallel")),
        name="in_proj",
    )(a, b)


_NT = (((1,), (1,)), ((), ()))


def _matmul_wt_kernel(a_ref, w_ref, o_ref, w_scr):
    @pl.when(pl.program_id(1) == 0)
    def _():
        w_scr[...] = w_ref[...].astype(BF16)

    o_ref[...] = lax.dot_general(a_ref[...], w_scr[...], _NT,
                                 preferred_element_type=F32).astype(o_ref.dtype)


def _matmul_wt(a, w_t, row0, n, out_dtype, tn):
    m, k = a.shape
    tm = _pick(m, (ROW_TILE, 512, 384, 128))
    return pl.pallas_call(
        _matmul_wt_kernel,
        grid=(n // tn, m // tm),
        in_specs=[pl.BlockSpec((tm, k), lambda j, i: (i, 0)),
                  pl.BlockSpec((pl.Element(tn), pl.Element(k)),
                               lambda j, i: (pl.multiple_of(row0 + j * tn, 16), 0))],
        out_specs=pl.BlockSpec((tm, tn), lambda j, i: (i, j)),
        out_shape=jax.ShapeDtypeStruct((m, n), out_dtype),
        scratch_shapes=[pltpu.VMEM((tn, k), BF16)],
        compiler_params=_cparams(("parallel", "arbitrary")),
        name="in_proj_wt",
    )(a, w_t)


def _small_proj_kernel(a_ref, w_ref, o_ref, w_scr, *, rank, rq, rk):
    @pl.when(pl.program_id(0) == 0)
    def _():
        w = w_ref[...].astype(BF16)
        half = MLA_ROPE // 2
        pe0 = rank + rq + rk
        o_pe = rq + rk
        w_scr[0:rq] = w[rank:rank + rq]
        w_scr[rq:o_pe] = w[rank + rq:pe0]
        w_scr[o_pe:o_pe + MLA_ROPE] = w[pe0:pe0 + MLA_ROPE]
        w_scr[o_pe + MLA_ROPE:o_pe + MLA_ROPE + half] = w[pe0 + half:pe0 + MLA_ROPE]
        w_scr[o_pe + MLA_ROPE + half:o_pe + 2 * MLA_ROPE] = w[pe0:pe0 + half]
        o_a = o_pe + 2 * MLA_ROPE
        w_scr[o_a:o_a + rank] = w[0:rank]
        w_scr[o_a + rank:] = jnp.zeros((w_scr.shape[0] - o_a - rank, w_scr.shape[1]), BF16)

    o_ref[...] = lax.dot_general(a_ref[...], w_scr[...], _NT, preferred_element_type=F32)


def _small_proj(a, w_t, row0, *, rank, rq, rk):
    m, k = a.shape
    n_in = rank + rq + rk + MLA_ROPE
    n_out = rq + rk + 2 * MLA_ROPE + LANE
    tm = _pick(m, (ROW_TILE, 512, 384, 128))
    kern = functools.partial(_small_proj_kernel, rank=rank, rq=rq, rk=rk)
    return pl.pallas_call(
        kern,
        grid=(m // tm,),
        in_specs=[pl.BlockSpec((tm, k), lambda i: (i, 0)),
                  pl.BlockSpec((pl.Element(n_in), pl.Element(k)), lambda i: (row0, 0))],
        out_specs=pl.BlockSpec((tm, n_out), lambda i: (i, 0)),
        out_shape=jax.ShapeDtypeStruct((m, n_out), F32),
        scratch_shapes=[pltpu.VMEM((n_out, k), BF16)],
        compiler_params=_cparams(("arbitrary",)),
        name="in_proj_small",
    )(a, w_t)


def _split3(x):
    a = x.astype(BF16)
    r1 = x - a.astype(F32)
    b = r1.astype(BF16)
    c = (r1 - b.astype(F32)).astype(BF16)
    return a, b, c


def _gla_kernel(q_ref, k_ref, v_ref, r_ref, ga_ref, a_ref, wa_ref, ba_ref, go_ref, s0_ref,
                o_ref, sout_ref, s_scr, *, C, SB, T, H, dk, dv):
    c_idx = pl.program_id(1)
    n_chunks = pl.num_programs(1)

    @pl.when(c_idx == 0)
    def _():
        s_scr[...] = s0_ref[0]

    z = jnp.dot(a_ref[...].astype(BF16), wa_ref[...], preferred_element_type=F32) + ba_ref[...]
    log_sig = jnp.minimum(z, 0.0) - jnp.log(1.0 + jnp.exp(-jnp.abs(z)))
    la = jnp.maximum(log_sig * (1.0 / GLA_GATE_NORM), GLA_LOG_ALPHA_MIN)
    if T % C:
        rows = c_idx * C + lax.broadcasted_iota(jnp.int32, (C, 1), 0)
        la = jnp.where(rows < T, la, 0.0)

    ri = lax.broadcasted_iota(jnp.int32, (C, C), 0)
    ci = lax.broadcasted_iota(jnp.int32, (C, C), 1)
    tri = jnp.where(ri >= ci, 1.0, 0.0).astype(BF16)
    ones = jnp.ones((C, LANE), BF16)
    cs_all = jnp.zeros_like(la)
    dsum_all = jnp.zeros((la.shape[1], LANE), F32)
    for piece in _split3(la):
        cs_all = cs_all + jnp.dot(tri, piece, preferred_element_type=F32)
        dsum_all = dsum_all + lax.dot_general(piece, ones, (((0,), (0,)), ((), ())),
                                              preferred_element_type=F32)

    sr = lax.broadcasted_iota(jnp.int32, (SB, SB), 0)
    sc = lax.broadcasted_iota(jnp.int32, (SB, SB), 1)
    causal = sr >= sc
    nt = (((1,), (1,)), ((), ()))
    scale = dk ** -0.5

    for h in range(H):
        ksl = slice(h * dk, (h + 1) * dk)
        vsl = slice(h * dv, (h + 1) * dv)
        cs = cs_all[:, ksl]
        c_last = cs[C - 1:C, :]
        q = q_ref[:, ksl].astype(F32) * scale
        k = k_ref[:, ksl].astype(F32)
        v = v_ref[:, vsl]
        s_old = s_scr[h]

        o_inter = jnp.dot((q * jnp.exp(cs)).astype(BF16), s_old.astype(BF16),
                          preferred_element_type=F32)
        k_end = (k * jnp.exp(c_last - cs)).astype(BF16)
        upd = lax.dot_general(k_end, v, (((0,), (0,)), ((), ())), preferred_element_type=F32)
        dcol = jnp.exp(dsum_all[ksl, :])
        s_scr[h] = jnp.concatenate([dcol] * (dv // LANE), axis=1) * s_old + upd

        outs = []
        for i in range(C // SB):
            lo = i * SB
            cs_i = cs[lo:lo + SB]
            q_i = q[lo:lo + SB]
            k_i = k[lo:lo + SB]
            start = cs[lo - 1:lo] if i > 0 else jnp.zeros_like(c_last)
            mid = 0.5 * (start + cs[lo + SB - 1:lo + SB])
            qd = (q_i * jnp.exp(cs_i - mid)).astype(BF16)
            kd = (k_i * jnp.exp(mid - cs_i)).astype(BF16)
            att = lax.dot_general(qd, kd, nt, preferred_element_type=F32)
            att = jnp.where(causal, att, 0.0)
            o_i = jnp.dot(att.astype(BF16), v[lo:lo + SB], preferred_element_type=F32)
            if i > 0:
                qo = (q_i * jnp.exp(cs_i - start)).astype(BF16)
                ko = (k[:lo] * jnp.exp(start - cs[:lo])).astype(BF16)
                att_o = lax.dot_general(qo, ko, nt, preferred_element_type=F32)
                o_i = o_i + jnp.dot(att_o.astype(BF16), v[:lo], preferred_element_type=F32)
            outs.append(o_i)
        o = o_inter + (jnp.concatenate(outs, axis=0) if len(outs) > 1 else outs[0])

        on = _rmsnorm(o, go_ref[...])
        r = r_ref[:, vsl].astype(F32)
        g = ga_ref[:, vsl].astype(F32)
        o_ref[:, vsl] = (_sigmoid(g) * (on * (r * _sigmoid(r)))).astype(o_ref.dtype)

    @pl.when(c_idx == n_chunks - 1)
    def _():
        sout_ref[0] = s_scr[...]


def _gla(qkvr, gates, small, wa_pad, b_a, g_out, s0, *, B, T, Tp, dk, dv, col):
    C = min(GLA_CHUNK, Tp)
    SB = min(32, C)
    nc = Tp // C
    H = GLA_HEADS
    qk, vw = H * dk, H * dv
    rb = lambda b, c: b * nc + c
    kern = functools.partial(_gla_kernel, C=C, SB=SB, T=T, H=H, dk=dk, dv=dv)
    return pl.pallas_call(
        kern,
        grid=(B, nc),
        in_specs=[
            pl.BlockSpec((C, qk), lambda b, c: (rb(b, c), col["q"] // qk)),
            pl.BlockSpec((C, qk), lambda b, c: (rb(b, c), col["k"] // qk)),
            pl.BlockSpec((C, vw), lambda b, c: (rb(b, c), col["v"] // vw)),
            pl.BlockSpec((C, vw), lambda b, c: (rb(b, c), col["r"] // vw)),
            pl.BlockSpec((C, vw), lambda b, c: (rb(b, c), col["ga"] // vw)),
            pl.BlockSpec((C, LANE), lambda b, c: (rb(b, c), col["a"] // LANE)),
            pl.BlockSpec((LANE, qk), lambda b, c: (0, 0)),
            pl.BlockSpec((1, qk), lambda b, c: (0, 0)),
            pl.BlockSpec((1, dv), lambda b, c: (0, 0)),
            pl.BlockSpec((1, H, dk, dv), lambda b, c: (b, 0, 0, 0)),
        ],
        out_specs=[
            pl.BlockSpec((C, vw), lambda b, c: (rb(b, c), 0)),
            pl.BlockSpec((1, H, dk, dv), lambda b, c: (b, 0, 0, 0)),
        ],
        out_shape=[jax.ShapeDtypeStruct((B * Tp, vw), BF16),
                   jax.ShapeDtypeStruct((B, H, dk, dv), F32)],
        scratch_shapes=[pltpu.VMEM((H, dk, dv), F32)],
        compiler_params=_cparams(("parallel", "arbitrary")),
        name="gla",
    )(qkvr, qkvr, qkvr, qkvr, gates, small, wa_pad, b_a.reshape(1, -1), g_out.reshape(1, -1), s0)


def _qprep_kernel(cq_ref, gq_ref, wn_ref, wp_ref, wps_ref, cos_ref, sin_ref, q_ref):
    hq = _rmsnorm(cq_ref[...], gq_ref[...]).astype(BF16)
    qn = jnp.dot(hq, wn_ref[...], preferred_element_type=F32)
    qp = jnp.dot(hq, wp_ref[...], preferred_element_type=F32)
    qs = jnp.dot(hq, wps_ref[...], preferred_element_type=F32)
    cos = cos_ref[...] * QK_SCALE_LOG2E
    sin = sin_ref[...] * QK_SCALE_LOG2E
    for h in range(MLA_HEADS):
        sl = slice(h * LANE, (h + 1) * LANE)
        q_ref[h, :, 0:LANE] = (qn[:, sl] * QK_SCALE_LOG2E).astype(BF16)
        q_ref[h, :, LANE:2 * LANE] = (qp[:, sl] * cos + qs[:, sl] * sin).astype(BF16)


def _qprep(small, g_q, wn, wp, wps, cos_t, sin_t, *, col):
    m = small.shape[0]
    rq = wn.shape[0]
    tm = _pick(m, (256, 128))
    full = lambda i: (0, 0)
    return pl.pallas_call(
        _qprep_kernel,
        grid=(m // tm,),
        in_specs=[pl.BlockSpec((tm, rq), lambda i: (i, col["cq"] // rq)),
                  pl.BlockSpec((1, rq), full),
                  pl.BlockSpec(wn.shape, full),
                  pl.BlockSpec(wp.shape, full),
                  pl.BlockSpec(wps.shape, full),
                  pl.BlockSpec((tm, LANE), lambda i: (i, 0)),
                  pl.BlockSpec((tm, LANE), lambda i: (i, 0))],
        out_specs=pl.BlockSpec((MLA_HEADS, tm, 2 * LANE), lambda i: (0, i, 0)),
        out_shape=jax.ShapeDtypeStruct((MLA_HEADS, m, 2 * LANE), BF16),
        compiler_params=_cparams(("parallel",)),
        name="mla_q",
    )(small, g_q.reshape(1, -1), wn, wp, wps, cos_t, sin_t)


def _lat_kernel(ckv_ref, kpe_ref, gkv_ref, cos_ref, sin_ref, lat_ref, kr_ref):
    lat_ref[...] = _rmsnorm(ckv_ref[...], gkv_ref[...])
    blk = kpe_ref[...]
    kr_ref[...] = blk * cos_ref[...] + pltpu.roll(blk, LANE // 2, 1) * sin_ref[...]


def _lat(small, g_kv, cos_t, sin_t, *, col):
    m = small.shape[0]
    rk = g_kv.shape[0]
    tm = _pick(m, (256, 128))
    return pl.pallas_call(
        _lat_kernel,
        grid=(m // tm,),
        in_specs=[pl.BlockSpec((tm, rk), lambda i: (i, col["ckv"] // rk)),
                  pl.BlockSpec((tm, LANE), lambda i: (i, col["kpe"] // LANE)),
                  pl.BlockSpec((1, rk), lambda i: (0, 0)),
                  pl.BlockSpec((tm, LANE), lambda i: (i, 0)),
                  pl.BlockSpec((tm, LANE), lambda i: (i, 0))],
        out_specs=[pl.BlockSpec((tm, rk), lambda i: (i, 0)),
                   pl.BlockSpec((tm, LANE), lambda i: (i, 0))],
        out_shape=[jax.ShapeDtypeStruct((m, rk), F32),
                   jax.ShapeDtypeStruct((m, LANE), F32)],
        compiler_params=_cparams(("parallel",)),
        name="mla_latent",
    )(small, small, g_kv.reshape(1, -1), cos_t, sin_t)


def _kvup_kernel(lat_ref, kr_ref, wuk_ref, wuv_ref, k_ref, v_ref, *, v_transposed):
    lat = lat_ref[...].astype(BF16)
    kn = jnp.dot(lat, wuk_ref[...], preferred_element_type=F32)
    kr = kr_ref[...]
    lane = lax.broadcasted_iota(jnp.int32, kr.shape, 1)
    kp = jnp.where(lane == MLA_ROPE, 1.0, kr).astype(BF16)
    if v_transposed:
        vv = lax.dot_general(wuv_ref[...], lat, (((1,), (1,)), ((), ())),
                             preferred_element_type=F32)
    else:
        vv = jnp.dot(lat, wuv_ref[...], preferred_element_type=F32)
    for h in range(MLA_HEADS):
        sl = slice(h * LANE, (h + 1) * LANE)
        k_ref[h, :, 0:LANE] = kn[:, sl].astype(BF16)
        k_ref[h, :, LANE:2 * LANE] = kp
        if v_transposed:
            v_ref[h, 0:LANE, :] = vv[sl, :].astype(BF16)
            v_ref[h, LANE:LANE + VT_ONES, :] = jnp.ones((VT_ONES, vv.shape[1]), BF16)
        else:
            v_ref[h] = vv[:, sl].astype(BF16)


def _kvup(lat, kr, wuk, wuv, *, v_transposed=False):
    m, rk = lat.shape
    tm = _pick(m, (512, 256, 128))
    full = lambda i: (0, 0)
    if v_transposed:
        v_spec = pl.BlockSpec((MLA_HEADS, LANE + VT_ONES, tm), lambda i: (0, 0, i))
        v_shape = (MLA_HEADS, LANE + VT_ONES, m)
    else:
        v_spec = pl.BlockSpec((MLA_HEADS, tm, LANE), lambda i: (0, i, 0))
        v_shape = (MLA_HEADS, m, LANE)
    return pl.pallas_call(
        functools.partial(_kvup_kernel, v_transposed=v_transposed),
        grid=(m // tm,),
        in_specs=[pl.BlockSpec((tm, rk), lambda i: (i, 0)),
                  pl.BlockSpec((tm, LANE), lambda i: (i, 0)),
                  pl.BlockSpec(wuk.shape, full),
                  pl.BlockSpec(wuv.shape, full)],
        out_specs=[pl.BlockSpec((MLA_HEADS, tm, 2 * LANE), lambda i: (0, i, 0)), v_spec],
        out_shape=[jax.ShapeDtypeStruct((MLA_HEADS, m, 2 * LANE), BF16),
                   jax.ShapeDtypeStruct(v_shape, BF16)],
        compiler_params=_cparams(("parallel",)),
        name="mla_kv",
    )(lat, kr, wuk, wuv)


def _last_kblock(qi, *, tq, tk, nk, q_off, k_off):
    top_chunk = ((qi + 1) * tq - 1 + q_off) // CHUNK
    last_key = (top_chunk + 1) * CHUNK - 1 - k_off
    return jnp.minimum(last_key // tk, nk - 1)


def _attn_kernel(q_ref, k_ref, v_ref, o_ref, m_scr, l_scr, acc_scr, *, hps, tq, tk, nk,
                 q_off, k_off):
    qi = pl.program_id(2)
    ki = pl.program_id(3)

    @pl.when(ki == 0)
    def _():
        m_scr[...] = jnp.full(m_scr.shape, NEG_BIG, F32)
        l_scr[...] = jnp.zeros(l_scr.shape, F32)
        acc_scr[...] = jnp.zeros(acc_scr.shape, F32)

    @pl.when(ki <= _last_kblock(qi, tq=tq, tk=tk, nk=nk, q_off=q_off, k_off=k_off))
    def _():
        q_chunk = (qi * tq + q_off + lax.broadcasted_iota(jnp.int32, (tq, 1), 0)) >> CHUNK_SHIFT
        k_chunk = (ki * tk + k_off + lax.broadcasted_iota(jnp.int32, (1, tk), 1)) >> CHUNK_SHIFT
        visible = q_chunk >= k_chunk

        def head(h, carry):
            s = lax.dot_general(q_ref[h], k_ref[h], (((1,), (1,)), ((), ())),
                                preferred_element_type=F32)
            s = jnp.where(visible, s, NEG_BIG)
            m_prev = m_scr[h]
            m_new = jnp.maximum(m_prev, jnp.max(s, axis=-1, keepdims=True))
            p = jnp.exp2(s - m_new)
            alpha = jnp.exp2(m_prev - m_new)
            l_scr[h] = alpha * l_scr[h] + jnp.sum(p, axis=-1, keepdims=True)
            acc_scr[h] = alpha * acc_scr[h] + jnp.dot(p.astype(BF16), v_ref[h],
                                                      preferred_element_type=F32)
            m_scr[h] = m_new
            return carry

        lax.fori_loop(0, hps, head, 0)

    @pl.when(ki == nk - 1)
    def _():
        for h in range(hps):
            o_ref[:, h * LANE:(h + 1) * LANE] = (acc_scr[h] / l_scr[h]).astype(o_ref.dtype)


def _attention(q, k, v, *, B, Tq, Tk, tq, tk, hps, q_off, k_off):
    nq = Tq // tq
    nk = Tk // tk
    hg = MLA_HEADS // hps
    dqk = q.shape[2]
    dvh = v.shape[2]
    last = functools.partial(_last_kblock, tq=tq, tk=tk, nk=nk, q_off=q_off, k_off=k_off)
    kern = functools.partial(_attn_kernel, hps=hps, tq=tq, tk=tk, nk=nk, q_off=q_off,
                             k_off=k_off)
    kv_row = lambda b, g, i, j: b * nk + jnp.minimum(j, last(i))
    return pl.pallas_call(
        kern,
        grid=(B, hg, nq, nk),
        in_specs=[pl.BlockSpec((hps, tq, dqk), lambda b, g, i, j: (g, b * nq + i, 0)),
                  pl.BlockSpec((hps, tk, dqk), lambda b, g, i, j: (g, kv_row(b, g, i, j), 0)),
                  pl.BlockSpec((hps, tk, dvh), lambda b, g, i, j: (g, kv_row(b, g, i, j), 0))],
        out_specs=pl.BlockSpec((tq, hps * dvh), lambda b, g, i, j: (b * nq + i, g)),
        out_shape=jax.ShapeDtypeStruct((B * Tq, MLA_HEADS * dvh), BF16),
        scratch_shapes=[pltpu.VMEM((hps, tq, 1), F32),
                        pltpu.VMEM((hps, tq, 1), F32),
                        pltpu.VMEM((hps, tq, dvh), F32)],
        compiler_params=_cparams(("parallel", "parallel", "parallel", "arbitrary")),
        name="mla_attn",
    )(q, k, v)


def _attn_t_kernel(qi_ref, ki_ref, q_ref, k_ref, vt_ref, kp_ref, vtp_ref, o_ref,
                   q_scr, r_scr, acc_scr, *, hps, t):
    pair = pl.program_id(1)
    qi = qi_ref[pair]
    ki = ki_ref[pair]
    nt = (((1,), (1,)), ((), ()))
    pe = slice(LANE, 2 * LANE)
    lane = lax.broadcasted_iota(jnp.int32, (t, LANE), 1)

    def set_reference(h, r):
        neg_r = jnp.transpose(jnp.broadcast_to(-r, (LANE, t)))
        q_scr[h, :, pe] = jnp.where(lane == MLA_ROPE, neg_r.astype(BF16), q_ref[h, :, pe])
        r_scr[h] = r

    def shifted_scores(h):
        return lax.dot_general(k_ref[h], q_scr[h], nt, preferred_element_type=F32)

    @pl.when(ki == 0)
    def _():
        for h in range(hps):
            q_scr[h, :, 0:LANE] = q_ref[h, :, 0:LANE]
            s = lax.dot_general(kp_ref[h], q_ref[h], nt, preferred_element_type=F32)
            r = jnp.max(s, axis=0, keepdims=True).astype(BF16).astype(F32)
            p = jnp.exp2((s - r).astype(BF16))
            acc_scr[h] = jnp.dot(vtp_ref[h], p, preferred_element_type=F32)
            set_reference(h, r)

    def general(h, bias, keep_reference):
        sp = shifted_scores(h)
        if bias is not None:
            sp = sp + bias
        r = r_scr[h]
        rise = jnp.maximum(jnp.max(sp, axis=0, keepdims=True), 0.0)
        r_new = (r + rise).astype(BF16).astype(F32)
        delta = r_new - r
        p = jnp.exp2((sp - delta).astype(BF16))
        acc_scr[h] = jnp.exp2(-delta) * acc_scr[h] + jnp.dot(vt_ref[h], p,
                                                               preferred_element_type=F32)
        if keep_reference:
            set_reference(h, r_new)

    @pl.when(ki < qi)
    def _():
        unsafe = []
        sp_next = shifted_scores(0)
        for h in range(hps):
            sp = sp_next
            if h + 1 < hps:
                sp_next = shifted_scores(h + 1)
            safe = jnp.max(sp) <= SAFE_EXP
            part = jnp.dot(vt_ref[h], jnp.exp2(sp.astype(BF16)), preferred_element_type=F32)
            acc_scr[h] += jnp.where(safe, part, 0.0)
            unsafe.append(jnp.logical_not(safe))

        @pl.when(functools.reduce(jnp.logical_or, unsafe))
        def _():
            for h in range(hps):
                @pl.when(unsafe[h])
                def _():
                    general(h, None, True)

    @pl.when(ki == qi)
    def _():
        k_chunk = lax.broadcasted_iota(jnp.int32, (t, 1), 0) >> CHUNK_SHIFT
        q_chunk = lax.broadcasted_iota(jnp.int32, (1, t), 1) >> CHUNK_SHIFT
        bias = jnp.where(q_chunk >= k_chunk, 0.0, NEG_BIG)
        for h in range(hps):
            general(h, bias, False)
            acc = acc_scr[h]
            o_t = acc[0:LANE] / acc[LANE:LANE + 1]
            o_ref[:, h * LANE:(h + 1) * LANE] = o_t.T.astype(o_ref.dtype)


def _attention_t(q, k, vt, k_pre, vt_pre, *, T, t, hps):
    n = T // t
    hg = MLA_HEADS // hps
    dqk = q.shape[2]
    npre = k_pre.shape[1]
    vrows = vt.shape[1]
    pairs = [(i, j) for i in range(n) for j in range(i + 1)]
    qi_arr = jnp.asarray([p[0] for p in pairs], jnp.int32)
    ki_arr = jnp.asarray([p[1] for p in pairs], jnp.int32)
    kern = functools.partial(_attn_t_kernel, hps=hps, t=t)
    grid_spec = pltpu.PrefetchScalarGridSpec(
        num_scalar_prefetch=2,
        grid=(hg, len(pairs)),
        in_specs=[pl.BlockSpec((hps, t, dqk), lambda g, p, qi, ki: (g, qi[p], 0)),
                  pl.BlockSpec((hps, t, dqk), lambda g, p, qi, ki: (g, ki[p], 0)),
                  pl.BlockSpec((hps, vrows, t), lambda g, p, qi, ki: (g, 0, ki[p])),
                  pl.BlockSpec((hps, npre, dqk), lambda g, p, qi, ki: (g, 0, 0)),
                  pl.BlockSpec((hps, vrows, npre), lambda g, p, qi, ki: (g, 0, 0))],
        out_specs=pl.BlockSpec((t, hps * LANE), lambda g, p, qi, ki: (qi[p], g)),
        scratch_shapes=[pltpu.VMEM((hps, t, dqk), BF16),
                        pltpu.VMEM((hps, 1, t), F32),
                        pltpu.VMEM((hps, vrows, t), F32)])
    return pl.pallas_call(
        kern,
        grid_spec=grid_spec,
        out_shape=jax.ShapeDtypeStruct((T, MLA_HEADS * LANE), BF16),
        compiler_params=_cparams(("parallel", "arbitrary")),
        name="mla_attn_t",
    )(qi_arr, ki_arr, q, k, vt, k_pre, vt_pre)


def _absorb_q_kernel(q_ref, w_ref, o_ref):
    o_ref[0] = jnp.dot(q_ref[0, :, 0:MLA_NOPE], w_ref[0],
                       preferred_element_type=F32).astype(o_ref.dtype)


def _absorb_q(q, w_uk_t3):
    heads, rows, dqk = q.shape
    rk = w_uk_t3.shape[2]
    return pl.pallas_call(
        _absorb_q_kernel,
        grid=(heads,),
        in_specs=[pl.BlockSpec((1, rows, dqk), lambda h: (h, 0, 0)),
                  pl.BlockSpec((1, MLA_NOPE, rk), lambda h: (h, 0, 0))],
        out_specs=pl.BlockSpec((1, rows, rk), lambda h: (h, 0, 0)),
        out_shape=jax.ShapeDtypeStruct((heads, rows, rk), BF16),
        compiler_params=_cparams(("parallel",)),
        name="mla_absorb_q",
    )(q, w_uk_t3)


def _attn_latent_kernel(ql_ref, q_ref, plat_ref, pkr_ref, lat_ref, kr_ref, o_ref, *, T, P):
    heads, _, rk = ql_ref.shape
    rows = heads * T
    nt = (((1,), (1,)), ((), ()))
    ql = ql_ref[...].reshape(rows, rk)
    qpe = q_ref[:, :, LANE:2 * LANE].reshape(rows, LANE)[:, 0:MLA_ROPE]
    lat_all = jnp.concatenate([plat_ref[0].astype(BF16), lat_ref[...].astype(BF16)], axis=0)
    kr_all = jnp.concatenate([pkr_ref[0], kr_ref[:, 0:MLA_ROPE]], axis=0).astype(BF16)
    s = (lax.dot_general(ql, lat_all, nt, preferred_element_type=F32)
         + lax.dot_general(qpe, kr_all, nt, preferred_element_type=F32))
    tok = lax.rem(lax.broadcasted_iota(jnp.int32, (rows, 1), 0), T)
    q_chunk = (P + tok) >> CHUNK_SHIFT
    k_chunk = lax.broadcasted_iota(jnp.int32, (1, P + T), 1) >> CHUNK_SHIFT
    s = jnp.where(q_chunk >= k_chunk, s, NEG_BIG)
    p = jnp.exp2(s - jnp.max(s, axis=-1, keepdims=True))
    o = jnp.dot(p.astype(BF16), lat_all, preferred_element_type=F32)
    o = o / jnp.sum(p, axis=-1, keepdims=True)
    o_ref[...] = o.reshape(heads, T, rk).astype(o_ref.dtype)


def _attn_latent(qlat, q, past_lat, past_kr, lat, kr, *, B, T):
    heads, _, rk = qlat.shape
    P = past_lat.shape[1]
    kern = functools.partial(_attn_latent_kernel, T=T, P=P)
    return pl.pallas_call(
        kern,
        grid=(B,),
        in_specs=[pl.BlockSpec((heads, T, rk), lambda b: (0, b, 0)),
                  pl.BlockSpec((heads, T, q.shape[2]), lambda b: (0, b, 0)),
                  pl.BlockSpec((1, P, rk), lambda b: (b, 0, 0)),
                  pl.BlockSpec((1, P, past_kr.shape[2]), lambda b: (b, 0, 0)),
                  pl.BlockSpec((T, rk), lambda b: (b, 0)),
                  pl.BlockSpec((T, LANE), lambda b: (b, 0))],
        out_specs=pl.BlockSpec((heads, T, rk), lambda b: (0, b, 0)),
        out_shape=jax.ShapeDtypeStruct((heads, B * T, rk), BF16),
        compiler_params=_cparams(("parallel",)),
        name="mla_attn_latent",
    )(qlat, q, past_lat, past_kr, lat, kr)


def _absorb_out_kernel(o_ref, w_ref, out_ref):
    out_ref[...] = jnp.dot(o_ref[0], w_ref[0], preferred_element_type=F32).astype(out_ref.dtype)


def _absorb_out(olat, w_uv3):
    heads, rows, rk = olat.shape
    dvh = w_uv3.shape[2]
    return pl.pallas_call(
        _absorb_out_kernel,
        grid=(heads,),
        in_specs=[pl.BlockSpec((1, rows, rk), lambda h: (h, 0, 0)),
                  pl.BlockSpec((1, rk, dvh), lambda h: (h, 0, 0))],
        out_specs=pl.BlockSpec((rows, dvh), lambda h: (0, h)),
        out_shape=jax.ShapeDtypeStruct((rows, heads * dvh), BF16),
        compiler_params=_cparams(("parallel",)),
        name="mla_absorb_out",
    )(olat, w_uv3)


def _merge_kernel(a_ref, gb_ref, om_ref, x_ref, wo_ref, gf_ref, x1_ref, h2_ref):
    merged = a_ref[...].astype(F32) + _sigmoid(gb_ref[...].astype(F32)) * om_ref[...].astype(F32)
    x1 = x_ref[...] + jnp.dot(merged.astype(BF16), wo_ref[...], preferred_element_type=F32)
    x1_ref[...] = x1
    h2_ref[...] = _rmsnorm(x1, gf_ref[...]).astype(BF16)


def _merge(branch_a, gates, o_m, x, wo, g_ffn, *, col):
    m, d = x.shape
    tm = _pick(m, (512, 384, 256, 128))
    row = lambda i: (i, 0)
    return pl.pallas_call(
        _merge_kernel,
        grid=(m // tm,),
        in_specs=[pl.BlockSpec((tm, d), row),
                  pl.BlockSpec((tm, d), lambda i: (i, col["gb"] // d)),
                  pl.BlockSpec((tm, d), row),
                  pl.BlockSpec((tm, d), row),
                  pl.BlockSpec(wo.shape, lambda i: (0, 0), pipeline_mode=pl.Buffered(1)),
                  pl.BlockSpec((1, d), lambda i: (0, 0))],
        out_specs=[pl.BlockSpec((tm, d), row), pl.BlockSpec((tm, d), row)],
        out_shape=[jax.ShapeDtypeStruct((m, d), F32), jax.ShapeDtypeStruct((m, d), BF16)],
        compiler_params=_cparams(("parallel",)),
        name="merge_out_proj",
    )(branch_a, gates, o_m, x, wo, g_ffn.reshape(1, -1))


HALO = 8


def _ffn_up_kernel(h_ref, wa_ref, wb_ref, cwa_ref, cwb_ref, cba_ref, cbb_ref, ha_ref, hb_ref,
                   act_ref, ca_ref, cb_ref, ext_scr, carry_scr, w_scr,
                   *, bb, r, tf, loc, carried):
    s = pl.program_id(1)
    rt = pl.program_id(2)
    d = h_ref.shape[2]

    @pl.when((s == 0) & (rt == 0))
    def _():
        w_scr[0] = wa_ref[...].astype(BF16)
        w_scr[1] = wb_ref[...].astype(BF16)

    if carried:
        @pl.when(rt == 0)
        def _():
            carry_scr[0] = ha_ref[...]
            carry_scr[1] = hb_ref[...]

    h = h_ref[...].reshape(bb * r, d)
    sw = min(FFN_SUB, tf)
    for c0 in range(0, tf, sw):
        cs = slice(c0, c0 + sw)
        conv = []
        for half, (cw_ref, cbias_ref, hist_ref, cout_ref) in enumerate(
                ((cwa_ref, cba_ref, ha_ref, ca_ref), (cwb_ref, cbb_ref, hb_ref, cb_ref))):
            u = jnp.dot(h, w_scr[half, :, cs], preferred_element_type=F32).reshape(bb, r, sw)
            ext_scr[half, :, HALO:HALO + r, cs] = u
            ext_scr[half, :, HALO - 2:HALO, cs] = (carry_scr[half, :, :, cs] if carried
                                                   else hist_ref[:, :, cs])
            u1 = ext_scr[half, :, HALO - 1:HALO - 1 + r, cs]
            u2 = ext_scr[half, :, HALO - 2:HALO - 2 + r, cs]
            cw = cw_ref[:, cs]
            conv.append(cbias_ref[:, cs] + cw[0:1] * u2 + cw[1:2] * u1 + cw[2:3] * u)
            if carried:
                carry_scr[half, :, :, cs] = ext_scr[half, :, HALO + r - 2:HALO + r, cs]
            cout_ref[0, :, :, cs] = ext_scr[half, :, HALO + loc:HALO + loc + 2, cs]

        act_ref[:, :, cs] = ((conv[0] * _sigmoid(conv[0])) * conv[1]).astype(act_ref.dtype)


def _ffn_down_kernel(act_ref, wd_ref, x1_ref, gf_ref, y_ref):
    down = jnp.dot(act_ref[...], wd_ref[...], preferred_element_type=F32)
    y_ref[...] = _rmsnorm(x1_ref[...] + down, gf_ref[...])


def _ffn(h2, x1, w_up, w_down, conv_w, conv_b, hist, g_final, *, B, T, Tp):
    d = h2.shape[1]
    dff = w_down.shape[0]
    tf = _pick(dff, (512, 256, 128))
    nf = dff // tf
    if Tp <= 128:
        bb, r = B, Tp
    else:
        bb, r = 1, _pick(Tp, (ROW_TILE, 128))
    nrt = Tp // r
    carried = nrt > 1
    loc = (T - 2) - (nrt - 1) * r
    assert 0 <= loc <= r - 2, "final two valid rows must sit in the last row tile"
    kern = functools.partial(_ffn_up_kernel, bb=bb, r=r, tf=tf, loc=loc, carried=carried)
    carry_shape = (2, bb, 2, tf) if carried else (1, 1, 2, LANE)
    act, ca, cb = pl.pallas_call(
        kern,
        grid=(nf, B // bb, nrt),
        in_specs=[pl.BlockSpec((bb, r, d), lambda f, s, t: (s, t, 0)),
                  pl.BlockSpec((d, tf), lambda f, s, t: (0, f)),
                  pl.BlockSpec((d, tf), lambda f, s, t: (0, nf + f)),
                  pl.BlockSpec((CONV_W, tf), lambda f, s, t: (0, f)),
                  pl.BlockSpec((CONV_W, tf), lambda f, s, t: (0, nf + f)),
                  pl.BlockSpec((1, tf), lambda f, s, t: (0, f)),
                  pl.BlockSpec((1, tf), lambda f, s, t: (0, nf + f)),
                  pl.BlockSpec((bb, 2, tf), lambda f, s, t: (s, 0, f)),
                  pl.BlockSpec((bb, 2, tf), lambda f, s, t: (s, 0, nf + f))],
        out_specs=[pl.BlockSpec((bb, r, tf), lambda f, s, t: (s, t, f)),
                   pl.BlockSpec((1, bb, 2, tf), lambda f, s, t: (t, s, 0, f)),
                   pl.BlockSpec((1, bb, 2, tf), lambda f, s, t: (t, s, 0, f))],
        out_shape=[jax.ShapeDtypeStruct((B, Tp, dff), BF16),
                   jax.ShapeDtypeStruct((nrt, B, 2, dff), F32),
                   jax.ShapeDtypeStruct((nrt, B, 2, dff), F32)],
        scratch_shapes=[pltpu.VMEM((2, bb, HALO + r, tf), F32),
                        pltpu.VMEM(carry_shape, F32),
                        pltpu.VMEM((2, d, tf), BF16)],
        compiler_params=_cparams(("arbitrary", "arbitrary", "arbitrary")),
        name="conv_ffn_up",
    )(h2.reshape(B, Tp, d), w_up, w_up, conv_w, conv_w, conv_b.reshape(1, -1),
      conv_b.reshape(1, -1), hist, hist)

    m = B * Tp
    tm = _pick(m, (256, 128))
    y = pl.pallas_call(
        _ffn_down_kernel,
        grid=(m // tm,),
        in_specs=[pl.BlockSpec((tm, dff), lambda i: (i, 0)),
                  pl.BlockSpec((dff, d), lambda i: (0, 0), pipeline_mode=pl.Buffered(1)),
                  pl.BlockSpec((tm, d), lambda i: (i, 0)),
                  pl.BlockSpec((1, d), lambda i: (0, 0))],
        out_specs=pl.BlockSpec((tm, d), lambda i: (i, 0)),
        out_shape=jax.ShapeDtypeStruct((m, d), F32),
        compiler_params=_cparams(("parallel",)),
        name="ffn_down",
    )(act.reshape(m, dff), w_down, x1, g_final.reshape(1, -1))
    return y.reshape(B, Tp, d), jnp.concatenate([ca[nrt - 1], cb[nrt - 1]], axis=-1)


def _rope_tables(pos):
    half = MLA_ROPE // 2
    inv = ROPE_THETA ** (-jnp.arange(0, MLA_ROPE, 2, dtype=F32) / MLA_ROPE)
    ang = pos.astype(F32)[:, None] * inv[None, :]
    cos, sin = jnp.cos(ang), jnp.sin(ang)
    zero = jnp.zeros((pos.shape[0], LANE - 2 * half), F32)
    return (jnp.concatenate([cos, cos, zero], axis=1),
            jnp.concatenate([-sin, sin, zero], axis=1))


def _stream(x, w, *, B, T, pos, q_off, s0, hist, past_lat=None, past_kr=None, prefix=None,
            emit_prefix=False):
    col = w["col"]
    dk, dv = w["dk"], w["dv"]
    h = _norm_cast(x, w["g_mix"])
    rows = w["in_rows"]
    qkvr = _matmul_wt(h, w["w_in_t"], rows["q"], rows["a"] - rows["q"], BF16, tn=1024)
    gates = _matmul_wt(h, w["w_in_t"], rows["ga"], rows["end"] - rows["ga"], BF16, tn=1024)
    small = _small_proj(h, w["w_in_t"], rows["a"], rank=rows["cq"] - rows["a"],
                        rq=rows["ckv"] - rows["cq"], rk=rows["kpe"] - rows["ckv"])

    branch_a, state = _gla(qkvr, gates, small, w["wa_pad"], w["b_a"], w["g_gla_out"], s0,
                           B=B, T=T, Tp=T, dk=dk, dv=dv, col=col)

    cos_t, sin_t = _rope_tables(pos)
    q = _qprep(small, w["g_q"], w["wq_nope"], w["wq_pe"], w["wq_pe_sw"], cos_t, sin_t, col=col)
    lat, kr = _lat(small, w["g_kv"], cos_t, sin_t, col=col)
    own_prefix = None
    if prefix is not None:
        assert B == 1 and past_lat is None
        k, vt = _kvup(lat, kr, w["w_uk"], w["w_uv_t"], v_transposed=True)
        o_m = _attention_t(q, k, vt, prefix[0], prefix[1], T=T, t=_pick(T, (1024, 128)),
                           hps=MLA_HEADS // 4)
    elif past_lat is not None:
        qlat = _absorb_q(q, w["w_uk_t3"])
        olat = _attn_latent(qlat, q, past_lat, past_kr, lat, kr, B=B, T=T)
        o_m = _absorb_out(olat, w["w_uv3"])
    else:
        k, v = _kvup(lat, kr, w["w_uk"], w["w_uv"])
        if emit_prefix:
            own_prefix = _kvup(lat, kr, w["w_uk"], w["w_uv_t"], v_transposed=True)
        o_m = _attention(q, k, v, B=B, Tq=T, Tk=T, tq=T, tk=T, hps=MLA_HEADS,
                         q_off=q_off, k_off=0)

    x1, h2 = _merge(branch_a, gates, o_m, x, w["w_o"], w["g_ffn"], col=col)
    y, conv = _ffn(h2, x1, w["w_up"], w["w_down"], w["conv_w"], w["conv_b"], hist,
                   w["final_norm"], B=B, T=T, Tp=T)
    return y, lat, kr, state, conv, own_prefix


def _prep_weights(g_mix, w_in, w_a2, b_a, g_gla_out, g_q, w_uq, g_kv, w_uk, w_uv, w_o,
                  g_ffn, w_up, conv_w, conv_b, w_down, final_norm):
    d = w_in.shape[0]
    rank, gqk = w_a2.shape
    gvw = GLA_HEADS * g_gla_out.shape[0]
    rq, rk = g_q.shape[0], g_kv.shape[0]
    half = MLA_ROPE // 2
    o, offs = 0, {}
    for name, width in (("q", gqk), ("k", gqk), ("v", gvw), ("r", gvw), ("a", rank),
                        ("cq", rq), ("ckv", rk), ("kpe", MLA_ROPE), ("ga", d), ("gb", d)):
        offs[name] = (o, o + width)
        o += width
    assert o == w_in.shape[1]
    in_rows = {name: lo for name, (lo, _) in offs.items()}
    in_rows["end"] = o
    assert all(v % 16 == 0 for v in in_rows.values())
    col = {"q": 0, "k": gqk, "v": 2 * gqk, "r": 2 * gqk + gvw, "ga": 0, "gb": d,
           "cq": 0, "ckv": rq, "kpe": rq + rk, "a": rq + rk + 2 * MLA_ROPE}

    w3 = w_uq.reshape(rq, MLA_HEADS, MLA_NOPE + MLA_ROPE)
    pe = w3[:, :, MLA_NOPE:]
    pe_sw = jnp.concatenate([pe[:, :, half:], pe[:, :, :half]], axis=2)
    zpad = jnp.zeros((rq, MLA_HEADS, LANE - MLA_ROPE), w_uq.dtype)
    flat = lambda t: t.reshape(rq, -1).astype(BF16)
    wa_pad = jnp.concatenate([w_a2, jnp.zeros((LANE - rank, gqk), w_a2.dtype)], axis=0)
    return dict(
        col=col, dk=gqk // GLA_HEADS, dv=g_gla_out.shape[0],
        g_mix=g_mix, w_in_t=jnp.swapaxes(w_in, 0, 1), in_rows=in_rows,
        wa_pad=wa_pad.astype(BF16), b_a=b_a, g_gla_out=g_gla_out, g_q=g_q,
        wq_nope=flat(w3[:, :, :MLA_NOPE]),
        wq_pe=flat(jnp.concatenate([pe, zpad], axis=2)),
        wq_pe_sw=flat(jnp.concatenate([pe_sw, zpad], axis=2)),
        g_kv=g_kv, w_uk=w_uk.astype(BF16), w_uv=w_uv.astype(BF16),
        w_uv_t=w_uv.T.astype(BF16),
        w_uk_t3=w_uk.reshape(rk, MLA_HEADS, MLA_NOPE).transpose(1, 2, 0).astype(BF16),
        w_uv3=w_uv.reshape(rk, MLA_HEADS, MLA_V).transpose(1, 0, 2).astype(BF16),
        w_o=w_o.astype(BF16),
        g_ffn=g_ffn, w_up=w_up, conv_w=conv_w, conv_b=conv_b,
        w_down=w_down.astype(BF16), final_norm=final_norm)


def kernel(x_prompt, x_sample, cache_mla_latent, cache_mla_krope, state_gla, cache_ffn_conv,
           meta_tokens, g_mix, w_in, w_a2, b_a, g_gla_out, g_q, w_uq, g_kv, w_uk, w_uv, w_o,
           g_ffn, w_up, conv_w, conv_b, w_down, final_norm):
    assert w_in.shape[0] == 1, "single trunk layer"
    bp, seq, d = x_prompt.shape
    assert bp == 1
    bs, ts, _ = x_sample.shape
    P = cache_mla_latent.shape[2]
    w = _prep_weights(g_mix[0], w_in[0], w_a2[0], b_a[0], g_gla_out[0], g_q[0], w_uq[0],
                      g_kv[0], w_uk[0], w_uv[0], w_o[0], g_ffn[0], w_up[0], conv_w[0],
                      conv_b[0], w_down[0], final_norm)
    dk, dv, dff2 = w["dk"], w["dv"], conv_w.shape[2]

    n_meta = meta_tokens.shape[0]
    assert n_meta == N_META and seq % CHUNK == 0
    _, lat_m, kr_m, st_m, cv_m, prefix = _stream(
        meta_tokens.astype(F32), w, B=1, T=n_meta, pos=jnp.arange(n_meta, dtype=jnp.int32),
        q_off=0, s0=jnp.zeros((1, GLA_HEADS, dk, dv), F32),
        hist=jnp.zeros((1, CONV_W - 1, dff2), F32), emit_prefix=True)
    yp, lat_p, kr_p, st_p, cv_p, _ = _stream(
        x_prompt[0], w, B=1, T=seq, pos=n_meta + jnp.arange(seq, dtype=jnp.int32),
        q_off=0, s0=st_m, hist=cv_m, prefix=prefix)

    pos_s = jnp.tile(P + jnp.arange(ts, dtype=jnp.int32), bs)
    ys, lat_s, kr_s, st_s, cv_s, _ = _stream(
        x_sample.reshape(bs * ts, d), w, B=bs, T=ts, pos=pos_s, q_off=P,
        past_lat=cache_mla_latent[0], past_kr=cache_mla_krope[0], s0=state_gla[0],
        hist=cache_ffn_conv[0])

    rk = lat_p.shape[1]
    T = n_meta + seq
    return (yp,
            ys,
            jnp.concatenate([lat_m, lat_p], axis=0).reshape(1, 1, T, rk),
            jnp.concatenate([kr_m, kr_p], axis=0)[:, :MLA_ROPE].reshape(1, 1, T, MLA_ROPE),
            st_p[None],
            cv_p[None],
            lat_s.reshape(1, bs, ts, rk),
            kr_s[:, :MLA_ROPE].reshape(1, bs, ts, MLA_ROPE),
            st_s[None],
            cv_s[None])
```

```python
import functools

import jax
import jax.numpy as jnp
from jax import lax
from jax.experimental import pallas as pl
from jax.experimental.pallas import tpu as pltpu

BF16 = jnp.bfloat16
F32 = jnp.float32

CHUNK = 64
CHUNK_SHIFT = 6
N_META = 16
EPS = 1e-6
GLA_HEADS = 4
GLA_GATE_NORM = 16.0
GLA_LOG_ALPHA_MIN = -5.0
MLA_HEADS = 16
MLA_NOPE = 128
MLA_ROPE = 64
MLA_V = 128
ROPE_THETA = 10000.0
CONV_W = 3
NEG_BIG = -1e30
LOG2E = 1.4426950408889634
QK_SCALE_LOG2E = (MLA_NOPE + MLA_ROPE) ** -0.5 * LOG2E

LANE = 128
VT_ONES = 16
GLA_CHUNK = 256
FFN_SUB = 512
SAFE_EXP = 64.0
ROW_TILE = 1024
VMEM_LIMIT = 56 * 1024 * 1024


def _cparams(sem, vmem=VMEM_LIMIT):
    return pltpu.CompilerParams(dimension_semantics=sem, vmem_limit_bytes=vmem)


def _rmsnorm(x, g):
    return x * lax.rsqrt(jnp.mean(x * x, axis=-1, keepdims=True) + EPS) * g


def _sigmoid(x):
    return 0.5 * jnp.tanh(0.5 * x) + 0.5


def _pick(n, cands):
    for c in cands:
        if n % c == 0:
            return c
    fits = [t for t in range(16, min(n, max(cands)) + 1, 16) if n % t == 0]
    if not fits:
        raise ValueError(f"no tile in {cands} divides {n}")
    return fits[-1]


def _norm_cast_kernel(x_ref, g_ref, o_ref):
    o_ref[...] = _rmsnorm(x_ref[...], g_ref[...]).astype(o_ref.dtype)


def _norm_cast(x, g):
    m, d = x.shape
    tm = _pick(m, (384, 256, 128))
    return pl.pallas_call(
        _norm_cast_kernel,
        grid=(m // tm,),
        in_specs=[pl.BlockSpec((tm, d), lambda i: (i, 0)),
                  pl.BlockSpec((1, d), lambda i: (0, 0))],
        out_specs=pl.BlockSpec((tm, d), lambda i: (i, 0)),
        out_shape=jax.ShapeDtypeStruct((m, d), BF16),
        compiler_params=_cparams(("parallel",)),
        name="norm_cast",
    )(x, g.reshape(1, d))


def _matmul_kernel(a_ref, b_ref, o_ref):
    o_ref[...] = jnp.dot(a_ref[...], b_ref[...], preferred_element_type=F32).astype(o_ref.dtype)


def _matmul(a, b, out_dtype, tn):
    m, k = a.shape
    n = b.shape[1]
    tm = _pick(m, (ROW_TILE, 512, 384, 128))
    return pl.pallas_call(
        _matmul_kernel,
        grid=(n // tn, m // tm),
        in_specs=[pl.BlockSpec((tm, k), lambda j, i: (i, 0)),
                  pl.BlockSpec((k, tn), lambda j, i: (0, j))],
        out_specs=pl.BlockSpec((tm, tn), lambda j, i: (i, j)),
        out_shape=jax.ShapeDtypeStruct((m, n), out_dtype),
        compiler_params=_cparams(("parallel", "parallel")),
        name="in_proj",
    )(a, b)


_NT = (((1,), (1,)), ((), ()))


def _matmul_wt_kernel(a_ref, w_ref, o_ref, w_scr):
    @pl.when(pl.program_id(1) == 0)
    def _():
        w_scr[...] = w_ref[...].astype(BF16)

    o_ref[...] = lax.dot_general(a_ref[...], w_scr[...], _NT,
                                 preferred_element_type=F32).astype(o_ref.dtype)


def _matmul_wt(a, w_t, row0, n, out_dtype, tn):
    m, k = a.shape
    tm = _pick(m, (ROW_TILE, 512, 384, 128))
    return pl.pallas_call(
        _matmul_wt_kernel,
        grid=(n // tn, m // tm),
        in_specs=[pl.BlockSpec((tm, k), lambda j, i: (i, 0)),
                  pl.BlockSpec((pl.Element(tn), pl.Element(k)),
                               lambda j, i: (pl.multiple_of(row0 + j * tn, 16), 0))],
        out_specs=pl.BlockSpec((tm, tn), lambda j, i: (i, j)),
        out_shape=jax.ShapeDtypeStruct((m, n), out_dtype),
        scratch_shapes=[pltpu.VMEM((tn, k), BF16)],
        compiler_params=_cparams(("parallel", "arbitrary")),
        name="in_proj_wt",
    )(a, w_t)


def _small_proj_kernel(a_ref, w_ref, o_ref, w_scr, *, rank, rq, rk):
    @pl.when(pl.program_id(0) == 0)
    def _():
        w = w_ref[...].astype(BF16)
        half = MLA_ROPE // 2
        pe0 = rank + rq + rk
        o_pe = rq + rk
        w_scr[0:rq] = w[rank:rank + rq]
        w_scr[rq:o_pe] = w[rank + rq:pe0]
        w_scr[o_pe:o_pe + MLA_ROPE] = w[pe0:pe0 + MLA_ROPE]
        w_scr[o_pe + MLA_ROPE:o_pe + MLA_ROPE + half] = w[pe0 + half:pe0 + MLA_ROPE]
        w_scr[o_pe + MLA_ROPE + half:o_pe + 2 * MLA_ROPE] = w[pe0:pe0 + half]
        o_a = o_pe + 2 * MLA_ROPE
        w_scr[o_a:o_a + rank] = w[0:rank]
        w_scr[o_a + rank:] = jnp.zeros((w_scr.shape[0] - o_a - rank, w_scr.shape[1]), BF16)

    o_ref[...] = lax.dot_general(a_ref[...], w_scr[...], _NT, preferred_element_type=F32)


def _small_proj(a, w_t, row0, *, rank, rq, rk):
    m, k = a.shape
    n_in = rank + rq + rk + MLA_ROPE
    n_out = rq + rk + 2 * MLA_ROPE + LANE
    tm = _pick(m, (ROW_TILE, 512, 384, 128))
    kern = functools.partial(_small_proj_kernel, rank=rank, rq=rq, rk=rk)
    return pl.pallas_call(
        kern,
        grid=(m // tm,),
        in_specs=[pl.BlockSpec((tm, k), lambda i: (i, 0)),
                  pl.BlockSpec((pl.Element(n_in), pl.Element(k)), lambda i: (row0, 0))],
        out_specs=pl.BlockSpec((tm, n_out), lambda i: (i, 0)),
        out_shape=jax.ShapeDtypeStruct((m, n_out), F32),
        scratch_shapes=[pltpu.VMEM((n_out, k), BF16)],
        compiler_params=_cparams(("arbitrary",)),
        name="in_proj_small",
    )(a, w_t)


def _split3(x):
    a = x.astype(BF16)
    r1 = x - a.astype(F32)
    b = r1.astype(BF16)
    c = (r1 - b.astype(F32)).astype(BF16)
    return a, b, c


def _gla_kernel(q_ref, k_ref, v_ref, r_ref, ga_ref, a_ref, wa_ref, ba_ref, go_ref, s0_ref,
                o_ref, sout_ref, s_scr, *, C, SB, T, H, dk, dv):
    c_idx = pl.program_id(1)
    n_chunks = pl.num_programs(1)

    @pl.when(c_idx == 0)
    def _():
        s_scr[...] = s0_ref[0]

    z = jnp.dot(a_ref[...].astype(BF16), wa_ref[...], preferred_element_type=F32) + ba_ref[...]
    log_sig = jnp.minimum(z, 0.0) - jnp.log(1.0 + jnp.exp(-jnp.abs(z)))
    la = jnp.maximum(log_sig * (1.0 / GLA_GATE_NORM), GLA_LOG_ALPHA_MIN)
    if T % C:
        rows = c_idx * C + lax.broadcasted_iota(jnp.int32, (C, 1), 0)
        la = jnp.where(rows < T, la, 0.0)

    ri = lax.broadcasted_iota(jnp.int32, (C, C), 0)
    ci = lax.broadcasted_iota(jnp.int32, (C, C), 1)
    tri = jnp.where(ri >= ci, 1.0, 0.0).astype(BF16)
    ones = jnp.ones((C, LANE), BF16)
    cs_all = jnp.zeros_like(la)
    dsum_all = jnp.zeros((la.shape[1], LANE), F32)
    for piece in _split3(la):
        cs_all = cs_all + jnp.dot(tri, piece, preferred_element_type=F32)
        dsum_all = dsum_all + lax.dot_general(piece, ones, (((0,), (0,)), ((), ())),
                                              preferred_element_type=F32)

    sr = lax.broadcasted_iota(jnp.int32, (SB, SB), 0)
    sc = lax.broadcasted_iota(jnp.int32, (SB, SB), 1)
    causal = sr >= sc
    nt = (((1,), (1,)), ((), ()))
    scale = dk ** -0.5

    for h in range(H):
        ksl = slice(h * dk, (h + 1) * dk)
        vsl = slice(h * dv, (h + 1) * dv)
        cs = cs_all[:, ksl]
        c_last = cs[C - 1:C, :]
        q = q_ref[:, ksl].astype(F32) * scale
        k = k_ref[:, ksl].astype(F32)
        v = v_ref[:, vsl]
        s_old = s_scr[h]

        o_inter = jnp.dot((q * jnp.exp(cs)).astype(BF16), s_old.astype(BF16),
                          preferred_element_type=F32)
        k_end = (k * jnp.exp(c_last - cs)).astype(BF16)
        upd = lax.dot_general(k_end, v, (((0,), (0,)), ((), ())), preferred_element_type=F32)
        dcol = jnp.exp(dsum_all[ksl, :])
        s_scr[h] = jnp.concatenate([dcol] * (dv // LANE), axis=1) * s_old + upd

        outs = []
        for i in range(C // SB):
            lo = i * SB
            cs_i = cs[lo:lo + SB]
            q_i = q[lo:lo + SB]
            k_i = k[lo:lo + SB]
            start = cs[lo - 1:lo] if i > 0 else jnp.zeros_like(c_last)
            mid = 0.5 * (start + cs[lo + SB - 1:lo + SB])
            qd = (q_i * jnp.exp(cs_i - mid)).astype(BF16)
            kd = (k_i * jnp.exp(mid - cs_i)).astype(BF16)
            att = lax.dot_general(qd, kd, nt, preferred_element_type=F32)
            att = jnp.where(causal, att, 0.0)
            o_i = jnp.dot(att.astype(BF16), v[lo:lo + SB], preferred_element_type=F32)
            if i > 0:
                qo = (q_i * jnp.exp(cs_i - start)).astype(BF16)
                ko = (k[:lo] * jnp.exp(start - cs[:lo])).astype(BF16)
                att_o = lax.dot_general(qo, ko, nt, preferred_element_type=F32)
                o_i = o_i + jnp.dot(att_o.astype(BF16), v[:lo], preferred_element_type=F32)
            outs.append(o_i)
        o = o_inter + (jnp.concatenate(outs, axis=0) if len(outs) > 1 else outs[0])

        on = _rmsnorm(o, go_ref[...])
        r = r_ref[:, vsl].astype(F32)
        g = ga_ref[:, vsl].astype(F32)
        o_ref[:, vsl] = (_sigmoid(g) * (on * (r * _sigmoid(r)))).astype(o_ref.dtype)

    @pl.when(c_idx == n_chunks - 1)
    def _():
        sout_ref[0] = s_scr[...]


def _gla(qkvr, gates, small, wa_pad, b_a, g_out, s0, *, B, T, Tp, dk, dv, col, row0=0):
    C = min(GLA_CHUNK, Tp)
    SB = min(32, C)
    nc = Tp // C
    H = GLA_HEADS
    qk, vw = H * dk, H * dv
    assert row0 % C == 0
    rb = lambda b, c: row0 // C + b * nc + c
    kern = functools.partial(_gla_kernel, C=C, SB=SB, T=T, H=H, dk=dk, dv=dv)
    return pl.pallas_call(
        kern,
        grid=(B, nc),
        in_specs=[
            pl.BlockSpec((C, qk), lambda b, c: (rb(b, c), col["q"] // qk)),
            pl.BlockSpec((C, qk), lambda b, c: (rb(b, c), col["k"] // qk)),
            pl.BlockSpec((C, vw), lambda b, c: (rb(b, c), col["v"] // vw)),
            pl.BlockSpec((C, vw), lambda b, c: (rb(b, c), col["r"] // vw)),
            pl.BlockSpec((C, vw), lambda b, c: (rb(b, c), col["ga"] // vw)),
            pl.BlockSpec((C, LANE), lambda b, c: (rb(b, c), col["a"] // LANE)),
            pl.BlockSpec((LANE, qk), lambda b, c: (0, 0)),
            pl.BlockSpec((1, qk), lambda b, c: (0, 0)),
            pl.BlockSpec((1, dv), lambda b, c: (0, 0)),
            pl.BlockSpec((1, H, dk, dv), lambda b, c: (b, 0, 0, 0)),
        ],
        out_specs=[
            pl.BlockSpec((C, vw), lambda b, c: (b * nc + c, 0)),
            pl.BlockSpec((1, H, dk, dv), lambda b, c: (b, 0, 0, 0)),
        ],
        out_shape=[jax.ShapeDtypeStruct((B * Tp, vw), BF16),
                   jax.ShapeDtypeStruct((B, H, dk, dv), F32)],
        scratch_shapes=[pltpu.VMEM((H, dk, dv), F32)],
        compiler_params=_cparams(("parallel", "arbitrary")),
        name="gla",
    )(qkvr, qkvr, qkvr, qkvr, gates, small, wa_pad, b_a.reshape(1, -1), g_out.reshape(1, -1), s0)


def _qprep_kernel(cq_ref, gq_ref, wn_ref, wp_ref, wps_ref, cos_ref, sin_ref, q_ref):
    hq = _rmsnorm(cq_ref[...], gq_ref[...]).astype(BF16)
    qn = jnp.dot(hq, wn_ref[...], preferred_element_type=F32)
    qp = jnp.dot(hq, wp_ref[...], preferred_element_type=F32)
    qs = jnp.dot(hq, wps_ref[...], preferred_element_type=F32)
    cos = cos_ref[...] * QK_SCALE_LOG2E
    sin = sin_ref[...] * QK_SCALE_LOG2E
    for h in range(MLA_HEADS):
        sl = slice(h * LANE, (h + 1) * LANE)
        q_ref[h, :, 0:LANE] = (qn[:, sl] * QK_SCALE_LOG2E).astype(BF16)
        q_ref[h, :, LANE:2 * LANE] = (qp[:, sl] * cos + qs[:, sl] * sin).astype(BF16)


def _qprep(small, g_q, wn, wp, wps, cos_t, sin_t, *, col):
    m = small.shape[0]
    rq = wn.shape[0]
    tm = _pick(m, (256, 128))
    full = lambda i: (0, 0)
    return pl.pallas_call(
        _qprep_kernel,
        grid=(m // tm,),
        in_specs=[pl.BlockSpec((tm, rq), lambda i: (i, col["cq"] // rq)),
                  pl.BlockSpec((1, rq), full),
                  pl.BlockSpec(wn.shape, full),
                  pl.BlockSpec(wp.shape, full),
                  pl.BlockSpec(wps.shape, full),
                  pl.BlockSpec((tm, LANE), lambda i: (i, 0)),
                  pl.BlockSpec((tm, LANE), lambda i: (i, 0))],
        out_specs=pl.BlockSpec((MLA_HEADS, tm, 2 * LANE), lambda i: (0, i, 0)),
        out_shape=jax.ShapeDtypeStruct((MLA_HEADS, m, 2 * LANE), BF16),
        compiler_params=_cparams(("parallel",)),
        name="mla_q",
    )(small, g_q.reshape(1, -1), wn, wp, wps, cos_t, sin_t)


def _lat_kernel(ckv_ref, kpe_ref, gkv_ref, cos_ref, sin_ref, lat_ref, kr_ref):
    lat_ref[...] = _rmsnorm(ckv_ref[...], gkv_ref[...])
    blk = kpe_ref[...]
    kr_ref[...] = blk * cos_ref[...] + pltpu.roll(blk, LANE // 2, 1) * sin_ref[...]


def _lat(small, g_kv, cos_t, sin_t, *, col):
    m = small.shape[0]
    rk = g_kv.shape[0]
    tm = _pick(m, (256, 128))
    return pl.pallas_call(
        _lat_kernel,
        grid=(m // tm,),
        in_specs=[pl.BlockSpec((tm, rk), lambda i: (i, col["ckv"] // rk)),
                  pl.BlockSpec((tm, LANE), lambda i: (i, col["kpe"] // LANE)),
                  pl.BlockSpec((1, rk), lambda i: (0, 0)),
                  pl.BlockSpec((tm, LANE), lambda i: (i, 0)),
                  pl.BlockSpec((tm, LANE), lambda i: (i, 0))],
        out_specs=[pl.BlockSpec((tm, rk), lambda i: (i, 0)),
                   pl.BlockSpec((tm, LANE), lambda i: (i, 0))],
        out_shape=[jax.ShapeDtypeStruct((m, rk), F32),
                   jax.ShapeDtypeStruct((m, LANE), F32)],
        compiler_params=_cparams(("parallel",)),
        name="mla_latent",
    )(small, small, g_kv.reshape(1, -1), cos_t, sin_t)


def _kvup_kernel(lat_ref, kr_ref, wuk_ref, wuv_ref, k_ref, v_ref, *, v_transposed):
    lat = lat_ref[...].astype(BF16)
    kn = jnp.dot(lat, wuk_ref[...], preferred_element_type=F32)
    kr = kr_ref[...]
    lane = lax.broadcasted_iota(jnp.int32, kr.shape, 1)
    kp = jnp.where(lane == MLA_ROPE, 1.0, kr).astype(BF16)
    if v_transposed:
        vv = lax.dot_general(wuv_ref[...], lat, (((1,), (1,)), ((), ())),
                             preferred_element_type=F32)
    else:
        vv = jnp.dot(lat, wuv_ref[...], preferred_element_type=F32)
    for h in range(MLA_HEADS):
        sl = slice(h * LANE, (h + 1) * LANE)
        k_ref[h, :, 0:LANE] = kn[:, sl].astype(BF16)
        k_ref[h, :, LANE:2 * LANE] = kp
        if v_transposed:
            v_ref[h, 0:LANE, :] = vv[sl, :].astype(BF16)
            v_ref[h, LANE:LANE + VT_ONES, :] = jnp.ones((VT_ONES, vv.shape[1]), BF16)
        else:
            v_ref[h] = vv[:, sl].astype(BF16)


def _kvup(lat, kr, wuk, wuv, *, v_transposed=False):
    m, rk = lat.shape
    tm = _pick(m, (512, 256, 128))
    full = lambda i: (0, 0)
    if v_transposed:
        v_spec = pl.BlockSpec((MLA_HEADS, LANE + VT_ONES, tm), lambda i: (0, 0, i))
        v_shape = (MLA_HEADS, LANE + VT_ONES, m)
    else:
        v_spec = pl.BlockSpec((MLA_HEADS, tm, LANE), lambda i: (0, i, 0))
        v_shape = (MLA_HEADS, m, LANE)
    return pl.pallas_call(
        functools.partial(_kvup_kernel, v_transposed=v_transposed),
        grid=(m // tm,),
        in_specs=[pl.BlockSpec((tm, rk), lambda i: (i, 0)),
                  pl.BlockSpec((tm, LANE), lambda i: (i, 0)),
                  pl.BlockSpec(wuk.shape, full),
                  pl.BlockSpec(wuv.shape, full)],
        out_specs=[pl.BlockSpec((MLA_HEADS, tm, 2 * LANE), lambda i: (0, i, 0)), v_spec],
        out_shape=[jax.ShapeDtypeStruct((MLA_HEADS, m, 2 * LANE), BF16),
                   jax.ShapeDtypeStruct(v_shape, BF16)],
        compiler_params=_cparams(("parallel",)),
        name="mla_kv",
    )(lat, kr, wuk, wuv)


def _last_kblock(qi, *, tq, tk, nk, q_off, k_off):
    top_chunk = ((qi + 1) * tq - 1 + q_off) // CHUNK
    last_key = (top_chunk + 1) * CHUNK - 1 - k_off
    return jnp.minimum(last_key // tk, nk - 1)


def _attn_kernel(q_ref, k_ref, v_ref, o_ref, m_scr, l_scr, acc_scr, *, hps, tq, tk, nk,
                 q_off, k_off):
    qi = pl.program_id(2)
    ki = pl.program_id(3)

    @pl.when(ki == 0)
    def _():
        m_scr[...] = jnp.full(m_scr.shape, NEG_BIG, F32)
        l_scr[...] = jnp.zeros(l_scr.shape, F32)
        acc_scr[...] = jnp.zeros(acc_scr.shape, F32)

    @pl.when(ki <= _last_kblock(qi, tq=tq, tk=tk, nk=nk, q_off=q_off, k_off=k_off))
    def _():
        q_chunk = (qi * tq + q_off + lax.broadcasted_iota(jnp.int32, (tq, 1), 0)) >> CHUNK_SHIFT
        k_chunk = (ki * tk + k_off + lax.broadcasted_iota(jnp.int32, (1, tk), 1)) >> CHUNK_SHIFT
        visible = q_chunk >= k_chunk

        def head(h, carry):
            s = lax.dot_general(q_ref[h], k_ref[h], (((1,), (1,)), ((), ())),
                                preferred_element_type=F32)
            s = jnp.where(visible, s, NEG_BIG)
            m_prev = m_scr[h]
            m_new = jnp.maximum(m_prev, jnp.max(s, axis=-1, keepdims=True))
            p = jnp.exp2(s - m_new)
            alpha = jnp.exp2(m_prev - m_new)
            l_scr[h] = alpha * l_scr[h] + jnp.sum(p, axis=-1, keepdims=True)
            acc_scr[h] = alpha * acc_scr[h] + jnp.dot(p.astype(BF16), v_ref[h],
                                                      preferred_element_type=F32)
            m_scr[h] = m_new
            return carry

        lax.fori_loop(0, hps, head, 0)

    @pl.when(ki == nk - 1)
    def _():
        for h in range(hps):
            o_ref[:, h * LANE:(h + 1) * LANE] = (acc_scr[h] / l_scr[h]).astype(o_ref.dtype)


def _attention(q, k, v, *, B, Tq, Tk, tq, tk, hps, q_off, k_off):
    nq = Tq // tq
    nk = Tk // tk
    hg = MLA_HEADS // hps
    dqk = q.shape[2]
    dvh = v.shape[2]
    last = functools.partial(_last_kblock, tq=tq, tk=tk, nk=nk, q_off=q_off, k_off=k_off)
    kern = functools.partial(_attn_kernel, hps=hps, tq=tq, tk=tk, nk=nk, q_off=q_off,
                             k_off=k_off)
    kv_row = lambda b, g, i, j: b * nk + jnp.minimum(j, last(i))
    return pl.pallas_call(
        kern,
        grid=(B, hg, nq, nk),
        in_specs=[pl.BlockSpec((hps, tq, dqk), lambda b, g, i, j: (g, b * nq + i, 0)),
                  pl.BlockSpec((hps, tk, dqk), lambda b, g, i, j: (g, kv_row(b, g, i, j), 0)),
                  pl.BlockSpec((hps, tk, dvh), lambda b, g, i, j: (g, kv_row(b, g, i, j), 0))],
        out_specs=pl.BlockSpec((tq, hps * dvh), lambda b, g, i, j: (b * nq + i, g)),
        out_shape=jax.ShapeDtypeStruct((B * Tq, MLA_HEADS * dvh), BF16),
        scratch_shapes=[pltpu.VMEM((hps, tq, 1), F32),
                        pltpu.VMEM((hps, tq, 1), F32),
                        pltpu.VMEM((hps, tq, dvh), F32)],
        compiler_params=_cparams(("parallel", "parallel", "parallel", "arbitrary")),
        name="mla_attn",
    )(q, k, v)


def _attn_t_kernel(qi_ref, ki_ref, q_ref, k_ref, vt_ref, kp_ref, vtp_ref, o_ref,
                   q_scr, r_scr, acc_scr, *, hps, t):
    pair = pl.program_id(1)
    qi = qi_ref[pair]
    ki = ki_ref[pair]
    nt = (((1,), (1,)), ((), ()))
    pe = slice(LANE, 2 * LANE)
    lane = lax.broadcasted_iota(jnp.int32, (t, LANE), 1)

    def set_reference(h, r):
        neg_r = jnp.transpose(jnp.broadcast_to(-r, (LANE, t)))
        q_scr[h, :, pe] = jnp.where(lane == MLA_ROPE, neg_r.astype(BF16), q_ref[h, :, pe])
        r_scr[h] = r

    def shifted_scores(h):
        return lax.dot_general(k_ref[h], q_scr[h], nt, preferred_element_type=F32)

    @pl.when(ki == 0)
    def _():
        for h in range(hps):
            q_scr[h, :, 0:LANE] = q_ref[h, :, 0:LANE]
            s = lax.dot_general(kp_ref[h], q_ref[h], nt, preferred_element_type=F32)
            r = jnp.max(s, axis=0, keepdims=True).astype(BF16).astype(F32)
            p = jnp.exp2((s - r).astype(BF16))
            acc_scr[h] = jnp.dot(vtp_ref[h], p, preferred_element_type=F32)
            set_reference(h, r)

    def general(h, bias, keep_reference):
        sp = shifted_scores(h)
        if bias is not None:
            sp = sp + bias
        r = r_scr[h]
        rise = jnp.maximum(jnp.max(sp, axis=0, keepdims=True), 0.0)
        r_new = (r + rise).astype(BF16).astype(F32)
        delta = r_new - r
        p = jnp.exp2((sp - delta).astype(BF16))
        acc_scr[h] = jnp.exp2(-delta) * acc_scr[h] + jnp.dot(vt_ref[h], p,
                                                               preferred_element_type=F32)
        if keep_reference:
            set_reference(h, r_new)

    @pl.when(ki < qi)
    def _():
        unsafe = []
        sp_next = shifted_scores(0)
        for h in range(hps):
            sp = sp_next
            if h + 1 < hps:
                sp_next = shifted_scores(h + 1)
            safe = jnp.max(sp) <= SAFE_EXP
            part = jnp.dot(vt_ref[h], jnp.exp2(sp.astype(BF16)), preferred_element_type=F32)
            acc_scr[h] += jnp.where(safe, part, 0.0)
            unsafe.append(jnp.logical_not(safe))

        @pl.when(functools.reduce(jnp.logical_or, unsafe))
        def _():
            for h in range(hps):
                @pl.when(unsafe[h])
                def _():
                    general(h, None, True)

    @pl.when(ki == qi)
    def _():
        k_chunk = lax.broadcasted_iota(jnp.int32, (t, 1), 0) >> CHUNK_SHIFT
        q_chunk = lax.broadcasted_iota(jnp.int32, (1, t), 1) >> CHUNK_SHIFT
        bias = jnp.where(q_chunk >= k_chunk, 0.0, NEG_BIG)
        for h in range(hps):
            general(h, bias, False)
            acc = acc_scr[h]
            o_t = acc[0:LANE] / acc[LANE:LANE + 1]
            o_ref[:, h * LANE:(h + 1) * LANE] = o_t.T.astype(o_ref.dtype)


def _attention_t(q, k, vt, k_pre, vt_pre, *, T, t, hps):
    n = T // t
    hg = MLA_HEADS // hps
    dqk = q.shape[2]
    npre = k_pre.shape[1]
    vrows = vt.shape[1]
    pairs = [(i, j) for i in range(n) for j in range(i + 1)]
    qi_arr = jnp.asarray([p[0] for p in pairs], jnp.int32)
    ki_arr = jnp.asarray([p[1] for p in pairs], jnp.int32)
    kern = functools.partial(_attn_t_kernel, hps=hps, t=t)
    grid_spec = pltpu.PrefetchScalarGridSpec(
        num_scalar_prefetch=2,
        grid=(hg, len(pairs)),
        in_specs=[pl.BlockSpec((hps, t, dqk), lambda g, p, qi, ki: (g, qi[p], 0)),
                  pl.BlockSpec((hps, t, dqk), lambda g, p, qi, ki: (g, ki[p], 0)),
                  pl.BlockSpec((hps, vrows, t), lambda g, p, qi, ki: (g, 0, ki[p])),
                  pl.BlockSpec((hps, npre, dqk), lambda g, p, qi, ki: (g, 0, 0)),
                  pl.BlockSpec((hps, vrows, npre), lambda g, p, qi, ki: (g, 0, 0))],
        out_specs=pl.BlockSpec((t, hps * LANE), lambda g, p, qi, ki: (qi[p], g)),
        scratch_shapes=[pltpu.VMEM((hps, t, dqk), BF16),
                        pltpu.VMEM((hps, 1, t), F32),
                        pltpu.VMEM((hps, vrows, t), F32)])
    return pl.pallas_call(
        kern,
        grid_spec=grid_spec,
        out_shape=jax.ShapeDtypeStruct((T, MLA_HEADS * LANE), BF16),
        compiler_params=_cparams(("parallel", "arbitrary")),
        name="mla_attn_t",
    )(qi_arr, ki_arr, q, k, vt, k_pre, vt_pre)


def _absorb_q_kernel(q_ref, w_ref, o_ref):
    o_ref[0] = jnp.dot(q_ref[0, :, 0:MLA_NOPE], w_ref[0],
                       preferred_element_type=F32).astype(o_ref.dtype)


def _absorb_q(q, w_uk_t3):
    heads, rows, dqk = q.shape
    rk = w_uk_t3.shape[2]
    return pl.pallas_call(
        _absorb_q_kernel,
        grid=(heads,),
        in_specs=[pl.BlockSpec((1, rows, dqk), lambda h: (h, 0, 0)),
                  pl.BlockSpec((1, MLA_NOPE, rk), lambda h: (h, 0, 0))],
        out_specs=pl.BlockSpec((1, rows, rk), lambda h: (h, 0, 0)),
        out_shape=jax.ShapeDtypeStruct((heads, rows, rk), BF16),
        compiler_params=_cparams(("parallel",)),
        name="mla_absorb_q",
    )(q, w_uk_t3)


def _attn_latent_kernel(ql_ref, q_ref, plat_ref, pkr_ref, lat_ref, kr_ref, o_ref, *, T, P):
    heads, _, rk = ql_ref.shape
    rows = heads * T
    nt = (((1,), (1,)), ((), ()))
    ql = ql_ref[...].reshape(rows, rk)
    qpe = q_ref[:, :, LANE:2 * LANE].reshape(rows, LANE)[:, 0:MLA_ROPE]
    lat_all = jnp.concatenate([plat_ref[0].astype(BF16), lat_ref[...].astype(BF16)], axis=0)
    kr_all = jnp.concatenate([pkr_ref[0], kr_ref[:, 0:MLA_ROPE]], axis=0).astype(BF16)
    s = (lax.dot_general(ql, lat_all, nt, preferred_element_type=F32)
         + lax.dot_general(qpe, kr_all, nt, preferred_element_type=F32))
    tok = lax.rem(lax.broadcasted_iota(jnp.int32, (rows, 1), 0), T)
    q_chunk = (P + tok) >> CHUNK_SHIFT
    k_chunk = lax.broadcasted_iota(jnp.int32, (1, P + T), 1) >> CHUNK_SHIFT
    s = jnp.where(q_chunk >= k_chunk, s, NEG_BIG)
    p = jnp.exp2(s - jnp.max(s, axis=-1, keepdims=True))
    o = jnp.dot(p.astype(BF16), lat_all, preferred_element_type=F32)
    o = o / jnp.sum(p, axis=-1, keepdims=True)
    o_ref[...] = o.reshape(heads, T, rk).astype(o_ref.dtype)


def _attn_latent(qlat, q, past_lat, past_kr, lat, kr, *, B, T):
    heads, _, rk = qlat.shape
    P = past_lat.shape[1]
    kern = functools.partial(_attn_latent_kernel, T=T, P=P)
    return pl.pallas_call(
        kern,
        grid=(B,),
        in_specs=[pl.BlockSpec((heads, T, rk), lambda b: (0, b, 0)),
                  pl.BlockSpec((heads, T, q.shape[2]), lambda b: (0, b, 0)),
                  pl.BlockSpec((1, P, rk), lambda b: (b, 0, 0)),
                  pl.BlockSpec((1, P, past_kr.shape[2]), lambda b: (b, 0, 0)),
                  pl.BlockSpec((T, rk), lambda b: (b, 0)),
                  pl.BlockSpec((T, LANE), lambda b: (b, 0))],
        out_specs=pl.BlockSpec((heads, T, rk), lambda b: (0, b, 0)),
        out_shape=jax.ShapeDtypeStruct((heads, B * T, rk), BF16),
        compiler_params=_cparams(("parallel",)),
        name="mla_attn_latent",
    )(qlat, q, past_lat, past_kr, lat, kr)


def _absorb_out_kernel(o_ref, w_ref, out_ref):
    out_ref[...] = jnp.dot(o_ref[0], w_ref[0], preferred_element_type=F32).astype(out_ref.dtype)


def _absorb_out(olat, w_uv3):
    heads, rows, rk = olat.shape
    dvh = w_uv3.shape[2]
    return pl.pallas_call(
        _absorb_out_kernel,
        grid=(heads,),
        in_specs=[pl.BlockSpec((1, rows, rk), lambda h: (h, 0, 0)),
                  pl.BlockSpec((1, rk, dvh), lambda h: (h, 0, 0))],
        out_specs=pl.BlockSpec((rows, dvh), lambda h: (0, h)),
        out_shape=jax.ShapeDtypeStruct((rows, heads * dvh), BF16),
        compiler_params=_cparams(("parallel",)),
        name="mla_absorb_out",
    )(olat, w_uv3)


def _merge_kernel(a_ref, gb_ref, om_ref, x_ref, wo_ref, gf_ref, x1_ref, h2_ref):
    merged = a_ref[...].astype(F32) + _sigmoid(gb_ref[...].astype(F32)) * om_ref[...].astype(F32)
    x1 = x_ref[...] + jnp.dot(merged.astype(BF16), wo_ref[...], preferred_element_type=F32)
    x1_ref[...] = x1
    h2_ref[...] = _rmsnorm(x1, gf_ref[...]).astype(BF16)


def _merge(branch_a, gates, o_m, x, wo, g_ffn, *, col):
    m, d = x.shape
    tm = _pick(m, (512, 384, 256, 128))
    row = lambda i: (i, 0)
    return pl.pallas_call(
        _merge_kernel,
        grid=(m // tm,),
        in_specs=[pl.BlockSpec((tm, d), row),
                  pl.BlockSpec((tm, d), lambda i: (i, col["gb"] // d)),
                  pl.BlockSpec((tm, d), row),
                  pl.BlockSpec((tm, d), row),
                  pl.BlockSpec(wo.shape, lambda i: (0, 0), pipeline_mode=pl.Buffered(1)),
                  pl.BlockSpec((1, d), lambda i: (0, 0))],
        out_specs=[pl.BlockSpec((tm, d), row), pl.BlockSpec((tm, d), row)],
        out_shape=[jax.ShapeDtypeStruct((m, d), F32), jax.ShapeDtypeStruct((m, d), BF16)],
        compiler_params=_cparams(("parallel",)),
        name="merge_out_proj",
    )(branch_a, gates, o_m, x, wo, g_ffn.reshape(1, -1))


HALO = 8


def _ffn_up_kernel(h_ref, wa_ref, wb_ref, cwa_ref, cwb_ref, cba_ref, cbb_ref, ha_ref, hb_ref,
                   act_ref, ca_ref, cb_ref, ext_scr, carry_scr, w_scr,
                   *, bb, r, tf, loc, carried):
    s = pl.program_id(1)
    rt = pl.program_id(2)
    d = h_ref.shape[2]

    @pl.when((s == 0) & (rt == 0))
    def _():
        w_scr[0] = wa_ref[...].astype(BF16)
        w_scr[1] = wb_ref[...].astype(BF16)

    if carried:
        @pl.when(rt == 0)
        def _():
            carry_scr[0] = ha_ref[...]
            carry_scr[1] = hb_ref[...]

    h = h_ref[...].reshape(bb * r, d)
    sw = min(FFN_SUB, tf)
    for c0 in range(0, tf, sw):
        cs = slice(c0, c0 + sw)
        conv = []
        for half, (cw_ref, cbias_ref, hist_ref, cout_ref) in enumerate(
                ((cwa_ref, cba_ref, ha_ref, ca_ref), (cwb_ref, cbb_ref, hb_ref, cb_ref))):
            u = jnp.dot(h, w_scr[half, :, cs], preferred_element_type=F32).reshape(bb, r, sw)
            ext_scr[half, :, HALO:HALO + r, cs] = u
            ext_scr[half, :, HALO - 2:HALO, cs] = (carry_scr[half, :, :, cs] if carried
                                                   else hist_ref[:, :, cs])
            u1 = ext_scr[half, :, HALO - 1:HALO - 1 + r, cs]
            u2 = ext_scr[half, :, HALO - 2:HALO - 2 + r, cs]
            cw = cw_ref[:, cs]
            conv.append(cbias_ref[:, cs] + cw[0:1] * u2 + cw[1:2] * u1 + cw[2:3] * u)
            if carried:
                carry_scr[half, :, :, cs] = ext_scr[half, :, HALO + r - 2:HALO + r, cs]
            cout_ref[0, :, :, cs] = ext_scr[half, :, HALO + loc:HALO + loc + 2, cs]

        act_ref[:, :, cs] = ((conv[0] * _sigmoid(conv[0])) * conv[1]).astype(act_ref.dtype)


def _ffn_down_kernel(act_ref, wd_ref, x1_ref, gf_ref, y_ref):
    down = jnp.dot(act_ref[...], wd_ref[...], preferred_element_type=F32)
    y_ref[...] = _rmsnorm(x1_ref[...] + down, gf_ref[...])


def _ffn(h2, x1, w_up, w_down, conv_w, conv_b, hist, g_final, *, B, T, Tp):
    d = h2.shape[1]
    dff = w_down.shape[0]
    tf = _pick(dff, (512, 256, 128))
    nf = dff // tf
    if Tp <= 128:
        bb, r = B, Tp
    else:
        bb, r = 1, _pick(Tp, (ROW_TILE, 128))
    nrt = Tp // r
    carried = nrt > 1
    loc = (T - 2) - (nrt - 1) * r
    assert 0 <= loc <= r - 2, "final two valid rows must sit in the last row tile"
    kern = functools.partial(_ffn_up_kernel, bb=bb, r=r, tf=tf, loc=loc, carried=carried)
    carry_shape = (2, bb, 2, tf) if carried else (1, 1, 2, LANE)
    act, ca, cb = pl.pallas_call(
        kern,
        grid=(nf, B // bb, nrt),
        in_specs=[pl.BlockSpec((bb, r, d), lambda f, s, t: (s, t, 0)),
                  pl.BlockSpec((d, tf), lambda f, s, t: (0, f)),
                  pl.BlockSpec((d, tf), lambda f, s, t: (0, nf + f)),
                  pl.BlockSpec((CONV_W, tf), lambda f, s, t: (0, f)),
                  pl.BlockSpec((CONV_W, tf), lambda f, s, t: (0, nf + f)),
                  pl.BlockSpec((1, tf), lambda f, s, t: (0, f)),
                  pl.BlockSpec((1, tf), lambda f, s, t: (0, nf + f)),
                  pl.BlockSpec((bb, 2, tf), lambda f, s, t: (s, 0, f)),
                  pl.BlockSpec((bb, 2, tf), lambda f, s, t: (s, 0, nf + f))],
        out_specs=[pl.BlockSpec((bb, r, tf), lambda f, s, t: (s, t, f)),
                   pl.BlockSpec((1, bb, 2, tf), lambda f, s, t: (t, s, 0, f)),
                   pl.BlockSpec((1, bb, 2, tf), lambda f, s, t: (t, s, 0, f))],
        out_shape=[jax.ShapeDtypeStruct((B, Tp, dff), BF16),
                   jax.ShapeDtypeStruct((nrt, B, 2, dff), F32),
                   jax.ShapeDtypeStruct((nrt, B, 2, dff), F32)],
        scratch_shapes=[pltpu.VMEM((2, bb, HALO + r, tf), F32),
                        pltpu.VMEM(carry_shape, F32),
                        pltpu.VMEM((2, d, tf), BF16)],
        compiler_params=_cparams(("arbitrary", "arbitrary", "arbitrary")),
        name="conv_ffn_up",
    )(h2.reshape(B, Tp, d), w_up, w_up, conv_w, conv_w, conv_b.reshape(1, -1),
      conv_b.reshape(1, -1), hist, hist)

    m = B * Tp
    tm = _pick(m, (256, 128))
    y = pl.pallas_call(
        _ffn_down_kernel,
        grid=(m // tm,),
        in_specs=[pl.BlockSpec((tm, dff), lambda i: (i, 0)),
                  pl.BlockSpec((dff, d), lambda i: (0, 0), pipeline_mode=pl.Buffered(1)),
                  pl.BlockSpec((tm, d), lambda i: (i, 0)),
                  pl.BlockSpec((1, d), lambda i: (0, 0))],
        out_specs=pl.BlockSpec((tm, d), lambda i: (i, 0)),
        out_shape=jax.ShapeDtypeStruct((m, d), F32),
        compiler_params=_cparams(("parallel",)),
        name="ffn_down",
    )(act.reshape(m, dff), w_down, x1, g_final.reshape(1, -1))
    return y.reshape(B, Tp, d), jnp.concatenate([ca[nrt - 1], cb[nrt - 1]], axis=-1)


def _rope_tables(pos):
    half = MLA_ROPE // 2
    inv = ROPE_THETA ** (-jnp.arange(0, MLA_ROPE, 2, dtype=F32) / MLA_ROPE)
    ang = pos.astype(F32)[:, None] * inv[None, :]
    cos, sin = jnp.cos(ang), jnp.sin(ang)
    zero = jnp.zeros((pos.shape[0], LANE - 2 * half), F32)
    return (jnp.concatenate([cos, cos, zero], axis=1),
            jnp.concatenate([-sin, sin, zero], axis=1))


def _project(x, w, pos):
    col = w["col"]
    h = _norm_cast(x, w["g_mix"])
    rows = w["in_rows"]
    qkvr = _matmul_wt(h, w["w_in_t"], rows["q"], rows["a"] - rows["q"], BF16, tn=1024)
    gates = _matmul_wt(h, w["w_in_t"], rows["ga"], rows["end"] - rows["ga"], BF16, tn=1024)
    small = _small_proj(h, w["w_in_t"], rows["a"], rank=rows["cq"] - rows["a"],
                        rq=rows["ckv"] - rows["cq"], rk=rows["kpe"] - rows["ckv"])
    cos_t, sin_t = _rope_tables(pos)
    q = _qprep(small, w["g_q"], w["wq_nope"], w["wq_pe"], w["wq_pe_sw"], cos_t, sin_t, col=col)
    lat, kr = _lat(small, w["g_kv"], cos_t, sin_t, col=col)
    return dict(qkvr=qkvr, gates=gates, small=small, q=q, lat=lat, kr=kr)


def _finish(x, pr, branch_a, o_m, w, hist, *, B, T):
    x1, h2 = _merge(branch_a, pr["gates"], o_m, x, w["w_o"], w["g_ffn"], col=w["col"])
    return _ffn(h2, x1, w["w_up"], w["w_down"], w["conv_w"], w["conv_b"], hist,
                w["final_norm"], B=B, T=T, Tp=T)


def _gla_group(pr, w, s0, *, B, T, row0=0):
    return _gla(pr["qkvr"], pr["gates"], pr["small"], w["wa_pad"], w["b_a"], w["g_gla_out"],
                s0, B=B, T=T, Tp=T, dk=w["dk"], dv=w["dv"], col=w["col"], row0=row0)


def _long_stream(x, w, *, T, pos, s0, hist, prefix):
    pr = _project(x, w, pos)
    branch_a, state = _gla_group(pr, w, s0, B=1, T=T)
    k, vt = _kvup(pr["lat"], pr["kr"], w["w_uk"], w["w_uv_t"], v_transposed=True)
    o_m = _attention_t(pr["q"], k, vt, prefix[0], prefix[1], T=T, t=_pick(T, (1024, 128)),
                       hps=MLA_HEADS // 4)
    y, conv = _finish(x, pr, branch_a, o_m, w, hist, B=1, T=T)
    return y, pr["lat"], pr["kr"], state, conv


def _short_streams(x_s, x_m, w, *, B, T, P, past_lat, past_kr, s0_s, hist_s):
    assert x_m.shape[0] == T
    ns = B * T
    x = jnp.concatenate([x_s, x_m], axis=0)
    pos = jnp.concatenate([jnp.tile(P + jnp.arange(T, dtype=jnp.int32), B),
                           jnp.arange(T, dtype=jnp.int32)])
    pr = _project(x, w, pos)
    dk, dv = w["dk"], w["dv"]

    ba_s, st_s = _gla_group(pr, w, s0_s, B=B, T=T)
    ba_m, st_m = _gla_group(pr, w, jnp.zeros((1, GLA_HEADS, dk, dv), F32), B=1, T=T, row0=ns)

    qlat = _absorb_q(pr["q"], w["w_uk_t3"])
    olat = _attn_latent(qlat, pr["q"], past_lat, past_kr, pr["lat"], pr["kr"], B=B, T=T)
    om_s = _absorb_out(olat, w["w_uv3"])
    q_m, lat_m, kr_m = pr["q"][:, ns:], pr["lat"][ns:], pr["kr"][ns:]
    k_m, v_m = _kvup(lat_m, kr_m, w["w_uk"], w["w_uv"])
    prefix = _kvup(lat_m, kr_m, w["w_uk"], w["w_uv_t"], v_transposed=True)
    om_m = _attention(q_m, k_m, v_m, B=1, Tq=T, Tk=T, tq=T, tk=T, hps=MLA_HEADS,
                      q_off=0, k_off=0)

    hist = jnp.concatenate([hist_s, jnp.zeros((1,) + hist_s.shape[1:], F32)], axis=0)
    y, conv = _finish(x, pr, jnp.concatenate([ba_s, ba_m], axis=0),
                      jnp.concatenate([om_s, om_m], axis=0), w, hist, B=B + 1, T=T)
    sample = (y[:B], pr["lat"][:ns], pr["kr"][:ns], st_s, conv[:B])
    meta = (lat_m, kr_m, st_m, conv[B:], prefix)
    return sample, meta


def _prep_weights(g_mix, w_in, w_a2, b_a, g_gla_out, g_q, w_uq, g_kv, w_uk, w_uv, w_o,
                  g_ffn, w_up, conv_w, conv_b, w_down, final_norm):
    d = w_in.shape[0]
    rank, gqk = w_a2.shape
    gvw = GLA_HEADS * g_gla_out.shape[0]
    rq, rk = g_q.shape[0], g_kv.shape[0]
    half = MLA_ROPE // 2
    o, offs = 0, {}
    for name, width in (("q", gqk), ("k", gqk), ("v", gvw), ("r", gvw), ("a", rank),
                        ("cq", rq), ("ckv", rk), ("kpe", MLA_ROPE), ("ga", d), ("gb", d)):
        offs[name] = (o, o + width)
        o += width
    assert o == w_in.shape[1]
    in_rows = {name: lo for name, (lo, _) in offs.items()}
    in_rows["end"] = o
    assert all(v % 16 == 0 for v in in_rows.values())
    col = {"q": 0, "k": gqk, "v": 2 * gqk, "r": 2 * gqk + gvw, "ga": 0, "gb": d,
           "cq": 0, "ckv": rq, "kpe": rq + rk, "a": rq + rk + 2 * MLA_ROPE}

    w3 = w_uq.reshape(rq, MLA_HEADS, MLA_NOPE + MLA_ROPE)
    pe = w3[:, :, MLA_NOPE:]
    pe_sw = jnp.concatenate([pe[:, :, half:], pe[:, :, :half]], axis=2)
    zpad = jnp.zeros((rq, MLA_HEADS, LANE - MLA_ROPE), w_uq.dtype)
    flat = lambda t: t.reshape(rq, -1).astype(BF16)
    wa_pad = jnp.concatenate([w_a2, jnp.zeros((LANE - rank, gqk), w_a2.dtype)], axis=0)
    return dict(
        col=col, dk=gqk // GLA_HEADS, dv=g_gla_out.shape[0],
        g_mix=g_mix, w_in_t=jnp.swapaxes(w_in, 0, 1), in_rows=in_rows,
        wa_pad=wa_pad.astype(BF16), b_a=b_a, g_gla_out=g_gla_out, g_q=g_q,
        wq_nope=flat(w3[:, :, :MLA_NOPE]),
        wq_pe=flat(jnp.concatenate([pe, zpad], axis=2)),
        wq_pe_sw=flat(jnp.concatenate([pe_sw, zpad], axis=2)),
        g_kv=g_kv, w_uk=w_uk.astype(BF16), w_uv=w_uv.astype(BF16),
        w_uv_t=w_uv.T.astype(BF16),
        w_uk_t3=w_uk.reshape(rk, MLA_HEADS, MLA_NOPE).transpose(1, 2, 0).astype(BF16),
        w_uv3=w_uv.reshape(rk, MLA_HEADS, MLA_V).transpose(1, 0, 2).astype(BF16),
        w_o=w_o.astype(BF16),
        g_ffn=g_ffn, w_up=w_up, conv_w=conv_w, conv_b=conv_b,
        w_down=w_down.astype(BF16), final_norm=final_norm)


def kernel(x_prompt, x_sample, cache_mla_latent, cache_mla_krope, state_gla, cache_ffn_conv,
           meta_tokens, g_mix, w_in, w_a2, b_a, g_gla_out, g_q, w_uq, g_kv, w_uk, w_uv, w_o,
           g_ffn, w_up, conv_w, conv_b, w_down, final_norm):
    assert w_in.shape[0] == 1, "single trunk layer"
    bp, seq, d = x_prompt.shape
    assert bp == 1
    bs, ts, _ = x_sample.shape
    P = cache_mla_latent.shape[2]
    w = _prep_weights(g_mix[0], w_in[0], w_a2[0], b_a[0], g_gla_out[0], g_q[0], w_uq[0],
                      g_kv[0], w_uk[0], w_uv[0], w_o[0], g_ffn[0], w_up[0], conv_w[0],
                      conv_b[0], w_down[0], final_norm)
    dk, dv, dff2 = w["dk"], w["dv"], conv_w.shape[2]

    n_meta = meta_tokens.shape[0]
    assert n_meta == N_META == ts and seq % CHUNK == 0
    (ys, lat_s, kr_s, st_s, cv_s), (lat_m, kr_m, st_m, cv_m, prefix) = _short_streams(
        x_sample.reshape(bs * ts, d), meta_tokens.astype(F32), w, B=bs, T=ts, P=P,
        past_lat=cache_mla_latent[0], past_kr=cache_mla_krope[0], s0_s=state_gla[0],
        hist_s=cache_ffn_conv[0])
    yp, lat_p, kr_p, st_p, cv_p = _long_stream(
        x_prompt[0], w, T=seq, pos=n_meta + jnp.arange(seq, dtype=jnp.int32),
        s0=st_m, hist=cv_m, prefix=prefix)

    rk = lat_p.shape[1]
    T = n_meta + seq
    return (yp,
            ys,
            jnp.concatenate([lat_m, lat_p], axis=0).reshape(1, 1, T, rk),
            jnp.concatenate([kr_m, kr_p], axis=0)[:, :MLA_ROPE].reshape(1, 1, T, MLA_ROPE),
            st_p[None],
            cv_p[None],
            lat_s.reshape(1, bs, ts, rk),
            kr_s[:, :MLA_ROPE].reshape(1, bs, ts, MLA_ROPE),
            st_s[None],
            cv_s[None])
```

```python
import functools

import jax
import jax.numpy as jnp
from jax import lax
from jax.experimental import pallas as pl
from jax.experimental.pallas import tpu as pltpu

BF16 = jnp.bfloat16
F32 = jnp.float32

CHUNK = 64
CHUNK_SHIFT = 6
N_META = 16
EPS = 1e-6
GLA_HEADS = 4
GLA_GATE_NORM = 16.0
GLA_LOG_ALPHA_MIN = -5.0
MLA_HEADS = 16
MLA_NOPE = 128
MLA_ROPE = 64
MLA_V = 128
ROPE_THETA = 10000.0
CONV_W = 3
NEG_BIG = -1e30
LOG2E = 1.4426950408889634
QK_SCALE_LOG2E = (MLA_NOPE + MLA_ROPE) ** -0.5 * LOG2E

LANE = 128
VT_ONES = 16
GLA_CHUNK = 256
FFN_SUB = 512
SAFE_EXP = 64.0
ROW_TILE = 1024
VMEM_LIMIT = 56 * 1024 * 1024


def _cparams(sem, vmem=VMEM_LIMIT):
    return pltpu.CompilerParams(dimension_semantics=sem, vmem_limit_bytes=vmem)


def _rmsnorm(x, g):
    return x * lax.rsqrt(jnp.mean(x * x, axis=-1, keepdims=True) + EPS) * g


def _sigmoid(x):
    return 0.5 * jnp.tanh(0.5 * x) + 0.5


def _pick(n, cands):
    for c in cands:
        if n % c == 0:
            return c
    fits = [t for t in range(16, min(n, max(cands)) + 1, 16) if n % t == 0]
    if not fits:
        raise ValueError(f"no tile in {cands} divides {n}")
    return fits[-1]


_NT = (((1,), (1,)), ((), ()))


def _matmul_wt_kernel(a_ref, w_ref, o_ref, w_scr):
    @pl.when(pl.program_id(1) == 0)
    def _():
        w_scr[...] = w_ref[...].astype(BF16)

    o_ref[...] = lax.dot_general(a_ref[...], w_scr[...], _NT,
                                 preferred_element_type=F32).astype(o_ref.dtype)


def _matmul_wt(a, w_t, row0, n, out_dtype, tn):
    m, k = a.shape
    tm = _pick(m, (ROW_TILE, 512, 384, 128))
    return pl.pallas_call(
        _matmul_wt_kernel,
        grid=(n // tn, m // tm),
        in_specs=[pl.BlockSpec((tm, k), lambda j, i: (i, 0)),
                  pl.BlockSpec((pl.Element(tn), pl.Element(k)),
                               lambda j, i: (pl.multiple_of(row0 + j * tn, 16), 0))],
        out_specs=pl.BlockSpec((tm, tn), lambda j, i: (i, j)),
        out_shape=jax.ShapeDtypeStruct((m, n), out_dtype),
        scratch_shapes=[pltpu.VMEM((tn, k), BF16)],
        compiler_params=_cparams(("parallel", "arbitrary")),
        name="in_proj_wt",
    )(a, w_t)


def _front_kernel(x_ref, g_ref, w_ref, gkv_ref, cos_ref, sin_ref,
                  h_ref, o_ref, lat_ref, kr_ref, w_scr, *, rank, rq, rk):
    @pl.when(pl.program_id(0) == 0)
    def _():
        w = w_ref[...].astype(BF16)
        half = MLA_ROPE // 2
        pe0 = rank + rq + rk
        o_pe = rq + rk
        w_scr[0:rq] = w[rank:rank + rq]
        w_scr[rq:o_pe] = w[rank + rq:pe0]
        w_scr[o_pe:o_pe + MLA_ROPE] = w[pe0:pe0 + MLA_ROPE]
        w_scr[o_pe + MLA_ROPE:o_pe + MLA_ROPE + half] = w[pe0 + half:pe0 + MLA_ROPE]
        w_scr[o_pe + MLA_ROPE + half:o_pe + 2 * MLA_ROPE] = w[pe0:pe0 + half]
        o_a = o_pe + 2 * MLA_ROPE
        w_scr[o_a:o_a + rank] = w[0:rank]
        w_scr[o_a + rank:] = jnp.zeros((w_scr.shape[0] - o_a - rank, w_scr.shape[1]), BF16)

    h = _rmsnorm(x_ref[...], g_ref[...]).astype(BF16)
    h_ref[...] = h
    small = lax.dot_general(h, w_scr[...], _NT, preferred_element_type=F32)
    o_ref[...] = small
    lat_ref[...] = _rmsnorm(small[:, rq:rq + rk], gkv_ref[...])
    blk = small[:, rq + rk:rq + rk + LANE]
    kr_ref[...] = blk * cos_ref[...] + pltpu.roll(blk, LANE // 2, 1) * sin_ref[...]


def _front(x, g_mix, w_t, row0, g_kv, cos_t, sin_t, *, rank, rq, rk):
    m, k = x.shape
    n_in = rank + rq + rk + MLA_ROPE
    n_out = rq + rk + 2 * MLA_ROPE + LANE
    tm = _pick(m, (512, 384, 128))
    kern = functools.partial(_front_kernel, rank=rank, rq=rq, rk=rk)
    row = lambda i: (i, 0)
    return pl.pallas_call(
        kern,
        grid=(m // tm,),
        in_specs=[pl.BlockSpec((tm, k), row),
                  pl.BlockSpec((1, k), lambda i: (0, 0)),
                  pl.BlockSpec((pl.Element(n_in), pl.Element(k)), lambda i: (row0, 0),
                               pipeline_mode=pl.Buffered(1)),
                  pl.BlockSpec((1, rk), lambda i: (0, 0)),
                  pl.BlockSpec((tm, LANE), row),
                  pl.BlockSpec((tm, LANE), row)],
        out_specs=[pl.BlockSpec((tm, k), row),
                   pl.BlockSpec((tm, n_out), row),
                   pl.BlockSpec((tm, rk), row),
                   pl.BlockSpec((tm, LANE), row)],
        out_shape=[jax.ShapeDtypeStruct((m, k), BF16),
                   jax.ShapeDtypeStruct((m, n_out), F32),
                   jax.ShapeDtypeStruct((m, rk), F32),
                   jax.ShapeDtypeStruct((m, LANE), F32)],
        scratch_shapes=[pltpu.VMEM((n_out, k), BF16)],
        compiler_params=_cparams(("arbitrary",)),
        name="front_proj",
    )(x, g_mix.reshape(1, -1), w_t, g_kv.reshape(1, -1), cos_t, sin_t)


def _split3(x):
    a = x.astype(BF16)
    r1 = x - a.astype(F32)
    b = r1.astype(BF16)
    c = (r1 - b.astype(F32)).astype(BF16)
    return a, b, c


def _gla_kernel(q_ref, k_ref, v_ref, r_ref, ga_ref, a_ref, wa_ref, ba_ref, go_ref, s0_ref,
                o_ref, sout_ref, s_scr, *, C, SB, T, H, dk, dv):
    c_idx = pl.program_id(1)
    n_chunks = pl.num_programs(1)

    @pl.when(c_idx == 0)
    def _():
        s_scr[...] = s0_ref[0]

    z = jnp.dot(a_ref[...].astype(BF16), wa_ref[...], preferred_element_type=F32) + ba_ref[...]
    log_sig = jnp.minimum(z, 0.0) - jnp.log(1.0 + jnp.exp(-jnp.abs(z)))
    la = jnp.maximum(log_sig * (1.0 / GLA_GATE_NORM), GLA_LOG_ALPHA_MIN)
    if T % C:
        rows = c_idx * C + lax.broadcasted_iota(jnp.int32, (C, 1), 0)
        la = jnp.where(rows < T, la, 0.0)

    ri = lax.broadcasted_iota(jnp.int32, (C, C), 0)
    ci = lax.broadcasted_iota(jnp.int32, (C, C), 1)
    tri = jnp.where(ri >= ci, 1.0, 0.0).astype(BF16)
    ones = jnp.ones((C, LANE), BF16)
    cs_all = jnp.zeros_like(la)
    dsum_all = jnp.zeros((la.shape[1], LANE), F32)
    for piece in _split3(la):
        cs_all = cs_all + jnp.dot(tri, piece, preferred_element_type=F32)
        dsum_all = dsum_all + lax.dot_general(piece, ones, (((0,), (0,)), ((), ())),
                                              preferred_element_type=F32)

    sr = lax.broadcasted_iota(jnp.int32, (SB, SB), 0)
    sc = lax.broadcasted_iota(jnp.int32, (SB, SB), 1)
    causal = sr >= sc
    nt = (((1,), (1,)), ((), ()))
    scale = dk ** -0.5

    for h in range(H):
        ksl = slice(h * dk, (h + 1) * dk)
        vsl = slice(h * dv, (h + 1) * dv)
        cs = cs_all[:, ksl]
        c_last = cs[C - 1:C, :]
        q = q_ref[:, ksl].astype(F32) * scale
        k = k_ref[:, ksl].astype(F32)
        v = v_ref[:, vsl]
        s_old = s_scr[h]

        o_inter = jnp.dot((q * jnp.exp(cs)).astype(BF16), s_old.astype(BF16),
                          preferred_element_type=F32)
        k_end = (k * jnp.exp(c_last - cs)).astype(BF16)
        upd = lax.dot_general(k_end, v, (((0,), (0,)), ((), ())), preferred_element_type=F32)
        dcol = jnp.exp(dsum_all[ksl, :])
        s_scr[h] = jnp.concatenate([dcol] * (dv // LANE), axis=1) * s_old + upd

        outs = []
        for i in range(C // SB):
            lo = i * SB
            cs_i = cs[lo:lo + SB]
            q_i = q[lo:lo + SB]
            k_i = k[lo:lo + SB]
            start = cs[lo - 1:lo] if i > 0 else jnp.zeros_like(c_last)
            mid = 0.5 * (start + cs[lo + SB - 1:lo + SB])
            qd = (q_i * jnp.exp(cs_i - mid)).astype(BF16)
            kd = (k_i * jnp.exp(mid - cs_i)).astype(BF16)
            att = lax.dot_general(qd, kd, nt, preferred_element_type=F32)
            att = jnp.where(causal, att, 0.0)
            o_i = jnp.dot(att.astype(BF16), v[lo:lo + SB], preferred_element_type=F32)
            if i > 0:
                qo = (q_i * jnp.exp(cs_i - start)).astype(BF16)
                ko = (k[:lo] * jnp.exp(start - cs[:lo])).astype(BF16)
                att_o = lax.dot_general(qo, ko, nt, preferred_element_type=F32)
                o_i = o_i + jnp.dot(att_o.astype(BF16), v[:lo], preferred_element_type=F32)
            outs.append(o_i)
        o = o_inter + (jnp.concatenate(outs, axis=0) if len(outs) > 1 else outs[0])

        on = _rmsnorm(o, go_ref[...])
        r = r_ref[:, vsl].astype(F32)
        g = ga_ref[:, vsl].astype(F32)
        o_ref[:, vsl] = (_sigmoid(g) * (on * (r * _sigmoid(r)))).astype(o_ref.dtype)

    @pl.when(c_idx == n_chunks - 1)
    def _():
        sout_ref[0] = s_scr[...]


def _gla(qkvr, gates, small, wa_pad, b_a, g_out, s0, *, B, T, Tp, dk, dv, col, row0=0):
    C = min(GLA_CHUNK, Tp)
    SB = min(32, C)
    nc = Tp // C
    H = GLA_HEADS
    qk, vw = H * dk, H * dv
    assert row0 % C == 0
    rb = lambda b, c: row0 // C + b * nc + c
    kern = functools.partial(_gla_kernel, C=C, SB=SB, T=T, H=H, dk=dk, dv=dv)
    return pl.pallas_call(
        kern,
        grid=(B, nc),
        in_specs=[
            pl.BlockSpec((C, qk), lambda b, c: (rb(b, c), col["q"] // qk)),
            pl.BlockSpec((C, qk), lambda b, c: (rb(b, c), col["k"] // qk)),
            pl.BlockSpec((C, vw), lambda b, c: (rb(b, c), col["v"] // vw)),
            pl.BlockSpec((C, vw), lambda b, c: (rb(b, c), col["r"] // vw)),
            pl.BlockSpec((C, vw), lambda b, c: (rb(b, c), col["ga"] // vw)),
            pl.BlockSpec((C, LANE), lambda b, c: (rb(b, c), col["a"] // LANE)),
            pl.BlockSpec((LANE, qk), lambda b, c: (0, 0)),
            pl.BlockSpec((1, qk), lambda b, c: (0, 0)),
            pl.BlockSpec((1, dv), lambda b, c: (0, 0)),
            pl.BlockSpec((1, H, dk, dv), lambda b, c: (b, 0, 0, 0)),
        ],
        out_specs=[
            pl.BlockSpec((C, vw), lambda b, c: (b * nc + c, 0)),
            pl.BlockSpec((1, H, dk, dv), lambda b, c: (b, 0, 0, 0)),
        ],
        out_shape=[jax.ShapeDtypeStruct((B * Tp, vw), BF16),
                   jax.ShapeDtypeStruct((B, H, dk, dv), F32)],
        scratch_shapes=[pltpu.VMEM((H, dk, dv), F32)],
        compiler_params=_cparams(("parallel", "arbitrary")),
        name="gla",
    )(qkvr, qkvr, qkvr, qkvr, gates, small, wa_pad, b_a.reshape(1, -1), g_out.reshape(1, -1), s0)


def _qprep_kernel(cq_ref, gq_ref, wn_ref, wp_ref, wps_ref, cos_ref, sin_ref, q_ref):
    hq = _rmsnorm(cq_ref[...], gq_ref[...]).astype(BF16)
    qn = jnp.dot(hq, wn_ref[...], preferred_element_type=F32)
    qp = jnp.dot(hq, wp_ref[...], preferred_element_type=F32)
    qs = jnp.dot(hq, wps_ref[...], preferred_element_type=F32)
    cos = cos_ref[...] * QK_SCALE_LOG2E
    sin = sin_ref[...] * QK_SCALE_LOG2E
    for h in range(MLA_HEADS):
        sl = slice(h * LANE, (h + 1) * LANE)
        q_ref[h, :, 0:LANE] = (qn[:, sl] * QK_SCALE_LOG2E).astype(BF16)
        q_ref[h, :, LANE:2 * LANE] = (qp[:, sl] * cos + qs[:, sl] * sin).astype(BF16)


def _qprep(small, g_q, wn, wp, wps, cos_t, sin_t, *, col):
    m = small.shape[0]
    rq = wn.shape[0]
    tm = _pick(m, (256, 128))
    full = lambda i: (0, 0)
    return pl.pallas_call(
        _qprep_kernel,
        grid=(m // tm,),
        in_specs=[pl.BlockSpec((tm, rq), lambda i: (i, col["cq"] // rq)),
                  pl.BlockSpec((1, rq), full),
                  pl.BlockSpec(wn.shape, full),
                  pl.BlockSpec(wp.shape, full),
                  pl.BlockSpec(wps.shape, full),
                  pl.BlockSpec((tm, LANE), lambda i: (i, 0)),
                  pl.BlockSpec((tm, LANE), lambda i: (i, 0))],
        out_specs=pl.BlockSpec((MLA_HEADS, tm, 2 * LANE), lambda i: (0, i, 0)),
        out_shape=jax.ShapeDtypeStruct((MLA_HEADS, m, 2 * LANE), BF16),
        compiler_params=_cparams(("parallel",)),
        name="mla_q",
    )(small, g_q.reshape(1, -1), wn, wp, wps, cos_t, sin_t)


def _kvup_kernel(lat_ref, kr_ref, wuk_ref, wuv_ref, k_ref, v_ref, *, v_transposed):
    lat = lat_ref[...].astype(BF16)
    kn = jnp.dot(lat, wuk_ref[...], preferred_element_type=F32)
    kr = kr_ref[...]
    lane = lax.broadcasted_iota(jnp.int32, kr.shape, 1)
    kp = jnp.where(lane == MLA_ROPE, 1.0, kr).astype(BF16)
    if v_transposed:
        vv = lax.dot_general(wuv_ref[...], lat, (((1,), (1,)), ((), ())),
                             preferred_element_type=F32)
    else:
        vv = jnp.dot(lat, wuv_ref[...], preferred_element_type=F32)
    for h in range(MLA_HEADS):
        sl = slice(h * LANE, (h + 1) * LANE)
        k_ref[h, :, 0:LANE] = kn[:, sl].astype(BF16)
        k_ref[h, :, LANE:2 * LANE] = kp
        if v_transposed:
            v_ref[h, 0:LANE, :] = vv[sl, :].astype(BF16)
            v_ref[h, LANE:LANE + VT_ONES, :] = jnp.ones((VT_ONES, vv.shape[1]), BF16)
        else:
            v_ref[h] = vv[:, sl].astype(BF16)


def _kvup(lat, kr, wuk, wuv, *, v_transposed=False):
    m, rk = lat.shape
    tm = _pick(m, (512, 256, 128))
    full = lambda i: (0, 0)
    if v_transposed:
        v_spec = pl.BlockSpec((MLA_HEADS, LANE + VT_ONES, tm), lambda i: (0, 0, i))
        v_shape = (MLA_HEADS, LANE + VT_ONES, m)
    else:
        v_spec = pl.BlockSpec((MLA_HEADS, tm, LANE), lambda i: (0, i, 0))
        v_shape = (MLA_HEADS, m, LANE)
    return pl.pallas_call(
        functools.partial(_kvup_kernel, v_transposed=v_transposed),
        grid=(m // tm,),
        in_specs=[pl.BlockSpec((tm, rk), lambda i: (i, 0)),
                  pl.BlockSpec((tm, LANE), lambda i: (i, 0)),
                  pl.BlockSpec(wuk.shape, full),
                  pl.BlockSpec(wuv.shape, full)],
        out_specs=[pl.BlockSpec((MLA_HEADS, tm, 2 * LANE), lambda i: (0, i, 0)), v_spec],
        out_shape=[jax.ShapeDtypeStruct((MLA_HEADS, m, 2 * LANE), BF16),
                   jax.ShapeDtypeStruct(v_shape, BF16)],
        compiler_params=_cparams(("parallel",)),
        name="mla_kv",
    )(lat, kr, wuk, wuv)


def _last_kblock(qi, *, tq, tk, nk, q_off, k_off):
    top_chunk = ((qi + 1) * tq - 1 + q_off) // CHUNK
    last_key = (top_chunk + 1) * CHUNK - 1 - k_off
    return jnp.minimum(last_key // tk, nk - 1)


def _attn_kernel(q_ref, k_ref, v_ref, o_ref, m_scr, l_scr, acc_scr, *, hps, tq, tk, nk,
                 q_off, k_off):
    qi = pl.program_id(2)
    ki = pl.program_id(3)

    @pl.when(ki == 0)
    def _():
        m_scr[...] = jnp.full(m_scr.shape, NEG_BIG, F32)
        l_scr[...] = jnp.zeros(l_scr.shape, F32)
        acc_scr[...] = jnp.zeros(acc_scr.shape, F32)

    @pl.when(ki <= _last_kblock(qi, tq=tq, tk=tk, nk=nk, q_off=q_off, k_off=k_off))
    def _():
        q_chunk = (qi * tq + q_off + lax.broadcasted_iota(jnp.int32, (tq, 1), 0)) >> CHUNK_SHIFT
        k_chunk = (ki * tk + k_off + lax.broadcasted_iota(jnp.int32, (1, tk), 1)) >> CHUNK_SHIFT
        visible = q_chunk >= k_chunk

        def head(h, carry):
            s = lax.dot_general(q_ref[h], k_ref[h], (((1,), (1,)), ((), ())),
                                preferred_element_type=F32)
            s = jnp.where(visible, s, NEG_BIG)
            m_prev = m_scr[h]
            m_new = jnp.maximum(m_prev, jnp.max(s, axis=-1, keepdims=True))
            p = jnp.exp2(s - m_new)
            alpha = jnp.exp2(m_prev - m_new)
            l_scr[h] = alpha * l_scr[h] + jnp.sum(p, axis=-1, keepdims=True)
            acc_scr[h] = alpha * acc_scr[h] + jnp.dot(p.astype(BF16), v_ref[h],
                                                      preferred_element_type=F32)
            m_scr[h] = m_new
            return carry

        lax.fori_loop(0, hps, head, 0)

    @pl.when(ki == nk - 1)
    def _():
        for h in range(hps):
            o_ref[:, h * LANE:(h + 1) * LANE] = (acc_scr[h] / l_scr[h]).astype(o_ref.dtype)


def _attention(q, k, v, *, B, Tq, Tk, tq, tk, hps, q_off, k_off):
    nq = Tq // tq
    nk = Tk // tk
    hg = MLA_HEADS // hps
    dqk = q.shape[2]
    dvh = v.shape[2]
    last = functools.partial(_last_kblock, tq=tq, tk=tk, nk=nk, q_off=q_off, k_off=k_off)
    kern = functools.partial(_attn_kernel, hps=hps, tq=tq, tk=tk, nk=nk, q_off=q_off,
                             k_off=k_off)
    kv_row = lambda b, g, i, j: b * nk + jnp.minimum(j, last(i))
    return pl.pallas_call(
        kern,
        grid=(B, hg, nq, nk),
        in_specs=[pl.BlockSpec((hps, tq, dqk), lambda b, g, i, j: (g, b * nq + i, 0)),
                  pl.BlockSpec((hps, tk, dqk), lambda b, g, i, j: (g, kv_row(b, g, i, j), 0)),
                  pl.BlockSpec((hps, tk, dvh), lambda b, g, i, j: (g, kv_row(b, g, i, j), 0))],
        out_specs=pl.BlockSpec((tq, hps * dvh), lambda b, g, i, j: (b * nq + i, g)),
        out_shape=jax.ShapeDtypeStruct((B * Tq, MLA_HEADS * dvh), BF16),
        scratch_shapes=[pltpu.VMEM((hps, tq, 1), F32),
                        pltpu.VMEM((hps, tq, 1), F32),
                        pltpu.VMEM((hps, tq, dvh), F32)],
        compiler_params=_cparams(("parallel", "parallel", "parallel", "arbitrary")),
        name="mla_attn",
    )(q, k, v)


def _attn_t_kernel(qi_ref, ki_ref, q_ref, k_ref, vt_ref, kp_ref, vtp_ref, o_ref,
                   q_scr, r_scr, acc_scr, *, hps, t):
    pair = pl.program_id(1)
    qi = qi_ref[pair]
    ki = ki_ref[pair]
    nt = (((1,), (1,)), ((), ()))
    pe = slice(LANE, 2 * LANE)
    lane = lax.broadcasted_iota(jnp.int32, (t, LANE), 1)

    def set_reference(h, r):
        neg_r = jnp.transpose(jnp.broadcast_to(-r, (LANE, t)))
        q_scr[h, :, pe] = jnp.where(lane == MLA_ROPE, neg_r.astype(BF16), q_ref[h, :, pe])
        r_scr[h] = r

    def shifted_scores(h):
        return lax.dot_general(k_ref[h], q_scr[h], nt, preferred_element_type=F32)

    @pl.when(ki == 0)
    def _():
        for h in range(hps):
            q_scr[h, :, 0:LANE] = q_ref[h, :, 0:LANE]
            s = lax.dot_general(kp_ref[h], q_ref[h], nt, preferred_element_type=F32)
            r = jnp.max(s, axis=0, keepdims=True).astype(BF16).astype(F32)
            p = jnp.exp2((s - r).astype(BF16))
            acc_scr[h] = jnp.dot(vtp_ref[h], p, preferred_element_type=F32)
            set_reference(h, r)

    def general(h, bias, keep_reference):
        sp = shifted_scores(h)
        if bias is not None:
            sp = sp + bias
        r = r_scr[h]
        rise = jnp.maximum(jnp.max(sp, axis=0, keepdims=True), 0.0)
        r_new = (r + rise).astype(BF16).astype(F32)
        delta = r_new - r
        p = jnp.exp2((sp - delta).astype(BF16))
        acc_scr[h] = jnp.exp2(-delta) * acc_scr[h] + jnp.dot(vt_ref[h], p,
                                                               preferred_element_type=F32)
        if keep_reference:
            set_reference(h, r_new)

    @pl.when(ki < qi)
    def _():
        unsafe = []
        sp_next = shifted_scores(0)
        for h in range(hps):
            sp = sp_next
            if h + 1 < hps:
                sp_next = shifted_scores(h + 1)
            safe = jnp.max(sp) <= SAFE_EXP
            part = jnp.dot(vt_ref[h], jnp.exp2(sp.astype(BF16)), preferred_element_type=F32)
            acc_scr[h] += jnp.where(safe, part, 0.0)
            unsafe.append(jnp.logical_not(safe))

        @pl.when(functools.reduce(jnp.logical_or, unsafe))
        def _():
            for h in range(hps):
                @pl.when(unsafe[h])
                def _():
                    general(h, None, True)

    @pl.when(ki == qi)
    def _():
        k_chunk = lax.broadcasted_iota(jnp.int32, (t, 1), 0) >> CHUNK_SHIFT
        q_chunk = lax.broadcasted_iota(jnp.int32, (1, t), 1) >> CHUNK_SHIFT
        bias = jnp.where(q_chunk >= k_chunk, 0.0, NEG_BIG)
        for h in range(hps):
            general(h, bias, False)
            acc = acc_scr[h]
            o_t = acc[0:LANE] / acc[LANE:LANE + 1]
            o_ref[:, h * LANE:(h + 1) * LANE] = o_t.T.astype(o_ref.dtype)


def _attention_t(q, k, vt, k_pre, vt_pre, *, T, t, hps):
    n = T // t
    hg = MLA_HEADS // hps
    dqk = q.shape[2]
    npre = k_pre.shape[1]
    vrows = vt.shape[1]
    pairs = [(i, j) for i in range(n) for j in range(i + 1)]
    qi_arr = jnp.asarray([p[0] for p in pairs], jnp.int32)
    ki_arr = jnp.asarray([p[1] for p in pairs], jnp.int32)
    kern = functools.partial(_attn_t_kernel, hps=hps, t=t)
    grid_spec = pltpu.PrefetchScalarGridSpec(
        num_scalar_prefetch=2,
        grid=(hg, len(pairs)),
        in_specs=[pl.BlockSpec((hps, t, dqk), lambda g, p, qi, ki: (g, qi[p], 0)),
                  pl.BlockSpec((hps, t, dqk), lambda g, p, qi, ki: (g, ki[p], 0)),
                  pl.BlockSpec((hps, vrows, t), lambda g, p, qi, ki: (g, 0, ki[p])),
                  pl.BlockSpec((hps, npre, dqk), lambda g, p, qi, ki: (g, 0, 0)),
                  pl.BlockSpec((hps, vrows, npre), lambda g, p, qi, ki: (g, 0, 0))],
        out_specs=pl.BlockSpec((t, hps * LANE), lambda g, p, qi, ki: (qi[p], g)),
        scratch_shapes=[pltpu.VMEM((hps, t, dqk), BF16),
                        pltpu.VMEM((hps, 1, t), F32),
                        pltpu.VMEM((hps, vrows, t), F32)])
    return pl.pallas_call(
        kern,
        grid_spec=grid_spec,
        out_shape=jax.ShapeDtypeStruct((T, MLA_HEADS * LANE), BF16),
        compiler_params=_cparams(("parallel", "arbitrary")),
        name="mla_attn_t",
    )(qi_arr, ki_arr, q, k, vt, k_pre, vt_pre)


def _absorb_q_kernel(q_ref, w_ref, o_ref):
    o_ref[0] = jnp.dot(q_ref[0, :, 0:MLA_NOPE], w_ref[0],
                       preferred_element_type=F32).astype(o_ref.dtype)


def _absorb_q(q, w_uk_t3):
    heads, rows, dqk = q.shape
    rk = w_uk_t3.shape[2]
    return pl.pallas_call(
        _absorb_q_kernel,
        grid=(heads,),
        in_specs=[pl.BlockSpec((1, rows, dqk), lambda h: (h, 0, 0)),
                  pl.BlockSpec((1, MLA_NOPE, rk), lambda h: (h, 0, 0))],
        out_specs=pl.BlockSpec((1, rows, rk), lambda h: (h, 0, 0)),
        out_shape=jax.ShapeDtypeStruct((heads, rows, rk), BF16),
        compiler_params=_cparams(("parallel",)),
        name="mla_absorb_q",
    )(q, w_uk_t3)


def _attn_latent_kernel(ql_ref, q_ref, plat_ref, pkr_ref, lat_ref, kr_ref, o_ref, *, T, P):
    heads, _, rk = ql_ref.shape
    rows = heads * T
    nt = (((1,), (1,)), ((), ()))
    ql = ql_ref[...].reshape(rows, rk)
    qpe = q_ref[:, :, LANE:2 * LANE].reshape(rows, LANE)[:, 0:MLA_ROPE]
    lat_all = jnp.concatenate([plat_ref[0].astype(BF16), lat_ref[...].astype(BF16)], axis=0)
    kr_all = jnp.concatenate([pkr_ref[0], kr_ref[:, 0:MLA_ROPE]], axis=0).astype(BF16)
    s = (lax.dot_general(ql, lat_all, nt, preferred_element_type=F32)
         + lax.dot_general(qpe, kr_all, nt, preferred_element_type=F32))
    tok = lax.rem(lax.broadcasted_iota(jnp.int32, (rows, 1), 0), T)
    q_chunk = (P + tok) >> CHUNK_SHIFT
    k_chunk = lax.broadcasted_iota(jnp.int32, (1, P + T), 1) >> CHUNK_SHIFT
    s = jnp.where(q_chunk >= k_chunk, s, NEG_BIG)
    p = jnp.exp2(s - jnp.max(s, axis=-1, keepdims=True))
    o = jnp.dot(p.astype(BF16), lat_all, preferred_element_type=F32)
    o = o / jnp.sum(p, axis=-1, keepdims=True)
    o_ref[...] = o.reshape(heads, T, rk).astype(o_ref.dtype)


def _attn_latent(qlat, q, past_lat, past_kr, lat, kr, *, B, T):
    heads, _, rk = qlat.shape
    P = past_lat.shape[1]
    kern = functools.partial(_attn_latent_kernel, T=T, P=P)
    return pl.pallas_call(
        kern,
        grid=(B,),
        in_specs=[pl.BlockSpec((heads, T, rk), lambda b: (0, b, 0)),
                  pl.BlockSpec((heads, T, q.shape[2]), lambda b: (0, b, 0)),
                  pl.BlockSpec((1, P, rk), lambda b: (b, 0, 0)),
                  pl.BlockSpec((1, P, past_kr.shape[2]), lambda b: (b, 0, 0)),
                  pl.BlockSpec((T, rk), lambda b: (b, 0)),
                  pl.BlockSpec((T, LANE), lambda b: (b, 0))],
        out_specs=pl.BlockSpec((heads, T, rk), lambda b: (0, b, 0)),
        out_shape=jax.ShapeDtypeStruct((heads, B * T, rk), BF16),
        compiler_params=_cparams(("parallel",)),
        name="mla_attn_latent",
    )(qlat, q, past_lat, past_kr, lat, kr)


def _absorb_out_kernel(o_ref, w_ref, out_ref):
    out_ref[...] = jnp.dot(o_ref[0], w_ref[0], preferred_element_type=F32).astype(out_ref.dtype)


def _absorb_out(olat, w_uv3):
    heads, rows, rk = olat.shape
    dvh = w_uv3.shape[2]
    return pl.pallas_call(
        _absorb_out_kernel,
        grid=(heads,),
        in_specs=[pl.BlockSpec((1, rows, rk), lambda h: (h, 0, 0)),
                  pl.BlockSpec((1, rk, dvh), lambda h: (h, 0, 0))],
        out_specs=pl.BlockSpec((rows, dvh), lambda h: (0, h)),
        out_shape=jax.ShapeDtypeStruct((rows, heads * dvh), BF16),
        compiler_params=_cparams(("parallel",)),
        name="mla_absorb_out",
    )(olat, w_uv3)


def _merge_kernel(a_ref, gb_ref, om_ref, x_ref, wo_ref, gf_ref, x1_ref, h2_ref):
    merged = a_ref[...].astype(F32) + _sigmoid(gb_ref[...].astype(F32)) * om_ref[...].astype(F32)
    x1 = x_ref[...] + jnp.dot(merged.astype(BF16), wo_ref[...], preferred_element_type=F32)
    x1_ref[...] = x1
    h2_ref[...] = _rmsnorm(x1, gf_ref[...]).astype(BF16)


def _merge(branch_a, gates, o_m, x, wo, g_ffn, *, col):
    m, d = x.shape
    tm = _pick(m, (512, 384, 256, 128))
    row = lambda i: (i, 0)
    return pl.pallas_call(
        _merge_kernel,
        grid=(m // tm,),
        in_specs=[pl.BlockSpec((tm, d), row),
                  pl.BlockSpec((tm, d), lambda i: (i, col["gb"] // d)),
                  pl.BlockSpec((tm, d), row),
                  pl.BlockSpec((tm, d), row),
                  pl.BlockSpec(wo.shape, lambda i: (0, 0), pipeline_mode=pl.Buffered(1)),
                  pl.BlockSpec((1, d), lambda i: (0, 0))],
        out_specs=[pl.BlockSpec((tm, d), row), pl.BlockSpec((tm, d), row)],
        out_shape=[jax.ShapeDtypeStruct((m, d), F32), jax.ShapeDtypeStruct((m, d), BF16)],
        compiler_params=_cparams(("parallel",)),
        name="merge_out_proj",
    )(branch_a, gates, o_m, x, wo, g_ffn.reshape(1, -1))


HALO = 8


def _ffn_up_kernel(h_ref, wa_ref, wb_ref, cwa_ref, cwb_ref, cba_ref, cbb_ref, ha_ref, hb_ref,
                   act_ref, ca_ref, cb_ref, ext_scr, carry_scr, w_scr,
                   *, bb, r, tf, loc, carried):
    s = pl.program_id(1)
    rt = pl.program_id(2)
    d = h_ref.shape[2]

    @pl.when((s == 0) & (rt == 0))
    def _():
        w_scr[0] = wa_ref[...].astype(BF16)
        w_scr[1] = wb_ref[...].astype(BF16)

    if carried:
        @pl.when(rt == 0)
        def _():
            carry_scr[0] = ha_ref[...]
            carry_scr[1] = hb_ref[...]

    h = h_ref[...].reshape(bb * r, d)
    sw = min(FFN_SUB, tf)
    for c0 in range(0, tf, sw):
        cs = slice(c0, c0 + sw)
        conv = []
        for half, (cw_ref, cbias_ref, hist_ref, cout_ref) in enumerate(
                ((cwa_ref, cba_ref, ha_ref, ca_ref), (cwb_ref, cbb_ref, hb_ref, cb_ref))):
            u = jnp.dot(h, w_scr[half, :, cs], preferred_element_type=F32).reshape(bb, r, sw)
            ext_scr[half, :, HALO:HALO + r, cs] = u
            ext_scr[half, :, HALO - 2:HALO, cs] = (carry_scr[half, :, :, cs] if carried
                                                   else hist_ref[:, :, cs])
            u1 = ext_scr[half, :, HALO - 1:HALO - 1 + r, cs]
            u2 = ext_scr[half, :, HALO - 2:HALO - 2 + r, cs]
            cw = cw_ref[:, cs]
            conv.append(cbias_ref[:, cs] + cw[0:1] * u2 + cw[1:2] * u1 + cw[2:3] * u)
            if carried:
                carry_scr[half, :, :, cs] = ext_scr[half, :, HALO + r - 2:HALO + r, cs]
            cout_ref[0, :, :, cs] = ext_scr[half, :, HALO + loc:HALO + loc + 2, cs]

        act_ref[:, :, cs] = ((conv[0] * _sigmoid(conv[0])) * conv[1]).astype(act_ref.dtype)


def _ffn_down_kernel(act_ref, wd_ref, x1_ref, gf_ref, y_ref):
    down = jnp.dot(act_ref[...], wd_ref[...], preferred_element_type=F32)
    y_ref[...] = _rmsnorm(x1_ref[...] + down, gf_ref[...])


def _ffn(h2, x1, w_up, w_down, conv_w, conv_b, hist, g_final, *, B, T, Tp):
    d = h2.shape[1]
    dff = w_down.shape[0]
    tf = _pick(dff, (512, 256, 128))
    nf = dff // tf
    if Tp <= 128:
        bb, r = B, Tp
    else:
        bb, r = 1, _pick(Tp, (ROW_TILE, 128))
    nrt = Tp // r
    carried = nrt > 1
    loc = (T - 2) - (nrt - 1) * r
    assert 0 <= loc <= r - 2, "final two valid rows must sit in the last row tile"
    kern = functools.partial(_ffn_up_kernel, bb=bb, r=r, tf=tf, loc=loc, carried=carried)
    carry_shape = (2, bb, 2, tf) if carried else (1, 1, 2, LANE)
    act, ca, cb = pl.pallas_call(
        kern,
        grid=(nf, B // bb, nrt),
        in_specs=[pl.BlockSpec((bb, r, d), lambda f, s, t: (s, t, 0)),
                  pl.BlockSpec((d, tf), lambda f, s, t: (0, f)),
                  pl.BlockSpec((d, tf), lambda f, s, t: (0, nf + f)),
                  pl.BlockSpec((CONV_W, tf), lambda f, s, t: (0, f)),
                  pl.BlockSpec((CONV_W, tf), lambda f, s, t: (0, nf + f)),
                  pl.BlockSpec((1, tf), lambda f, s, t: (0, f)),
                  pl.BlockSpec((1, tf), lambda f, s, t: (0, nf + f)),
                  pl.BlockSpec((bb, 2, tf), lambda f, s, t: (s, 0, f)),
                  pl.BlockSpec((bb, 2, tf), lambda f, s, t: (s, 0, nf + f))],
        out_specs=[pl.BlockSpec((bb, r, tf), lambda f, s, t: (s, t, f)),
                   pl.BlockSpec((1, bb, 2, tf), lambda f, s, t: (t, s, 0, f)),
                   pl.BlockSpec((1, bb, 2, tf), lambda f, s, t: (t, s, 0, f))],
        out_shape=[jax.ShapeDtypeStruct((B, Tp, dff), BF16),
                   jax.ShapeDtypeStruct((nrt, B, 2, dff), F32),
                   jax.ShapeDtypeStruct((nrt, B, 2, dff), F32)],
        scratch_shapes=[pltpu.VMEM((2, bb, HALO + r, tf), F32),
                        pltpu.VMEM(carry_shape, F32),
                        pltpu.VMEM((2, d, tf), BF16)],
        compiler_params=_cparams(("arbitrary", "arbitrary", "arbitrary")),
        name="conv_ffn_up",
    )(h2.reshape(B, Tp, d), w_up, w_up, conv_w, conv_w, conv_b.reshape(1, -1),
      conv_b.reshape(1, -1), hist, hist)

    m = B * Tp
    tm = _pick(m, (256, 128))
    y = pl.pallas_call(
        _ffn_down_kernel,
        grid=(m // tm,),
        in_specs=[pl.BlockSpec((tm, dff), lambda i: (i, 0)),
                  pl.BlockSpec((dff, d), lambda i: (0, 0), pipeline_mode=pl.Buffered(1)),
                  pl.BlockSpec((tm, d), lambda i: (i, 0)),
                  pl.BlockSpec((1, d), lambda i: (0, 0))],
        out_specs=pl.BlockSpec((tm, d), lambda i: (i, 0)),
        out_shape=jax.ShapeDtypeStruct((m, d), F32),
        compiler_params=_cparams(("parallel",)),
        name="ffn_down",
    )(act.reshape(m, dff), w_down, x1, g_final.reshape(1, -1))
    return y.reshape(B, Tp, d), jnp.concatenate([ca[nrt - 1], cb[nrt - 1]], axis=-1)


def _rope_tables(pos):
    half = MLA_ROPE // 2
    inv = ROPE_THETA ** (-jnp.arange(0, MLA_ROPE, 2, dtype=F32) / MLA_ROPE)
    ang = pos.astype(F32)[:, None] * inv[None, :]
    cos, sin = jnp.cos(ang), jnp.sin(ang)
    zero = jnp.zeros((pos.shape[0], LANE - 2 * half), F32)
    return (jnp.concatenate([cos, cos, zero], axis=1),
            jnp.concatenate([-sin, sin, zero], axis=1))


def _project(x, w, pos):
    col = w["col"]
    rows = w["in_rows"]
    cos_t, sin_t = _rope_tables(pos)
    h, small, lat, kr = _front(x, w["g_mix"], w["w_in_t"], rows["a"], w["g_kv"], cos_t, sin_t,
                               rank=rows["cq"] - rows["a"], rq=rows["ckv"] - rows["cq"],
                               rk=rows["kpe"] - rows["ckv"])
    qkvr = _matmul_wt(h, w["w_in_t"], rows["q"], rows["a"] - rows["q"], BF16, tn=1024)
    gates = _matmul_wt(h, w["w_in_t"], rows["ga"], rows["end"] - rows["ga"], BF16, tn=1024)
    q = _qprep(small, w["g_q"], w["wq_nope"], w["wq_pe"], w["wq_pe_sw"], cos_t, sin_t, col=col)
    return dict(qkvr=qkvr, gates=gates, small=small, q=q, lat=lat, kr=kr)


def _finish(x, pr, branch_a, o_m, w, hist, *, B, T):
    x1, h2 = _merge(branch_a, pr["gates"], o_m, x, w["w_o"], w["g_ffn"], col=w["col"])
    return _ffn(h2, x1, w["w_up"], w["w_down"], w["conv_w"], w["conv_b"], hist,
                w["final_norm"], B=B, T=T, Tp=T)


def _gla_group(pr, w, s0, *, B, T, row0=0):
    return _gla(pr["qkvr"], pr["gates"], pr["small"], w["wa_pad"], w["b_a"], w["g_gla_out"],
                s0, B=B, T=T, Tp=T, dk=w["dk"], dv=w["dv"], col=w["col"], row0=row0)


def _long_stream(x, w, *, T, pos, s0, hist, prefix):
    pr = _project(x, w, pos)
    branch_a, state = _gla_group(pr, w, s0, B=1, T=T)
    k, vt = _kvup(pr["lat"], pr["kr"], w["w_uk"], w["w_uv_t"], v_transposed=True)
    o_m = _attention_t(pr["q"], k, vt, prefix[0], prefix[1], T=T, t=_pick(T, (1024, 128)),
                       hps=MLA_HEADS // 4)
    y, conv = _finish(x, pr, branch_a, o_m, w, hist, B=1, T=T)
    return y, pr["lat"], pr["kr"], state, conv


def _short_streams(x_s, x_m, w, *, B, T, P, past_lat, past_kr, s0_s, hist_s):
    assert x_m.shape[0] == T
    ns = B * T
    x = jnp.concatenate([x_s, x_m], axis=0)
    pos = jnp.concatenate([jnp.tile(P + jnp.arange(T, dtype=jnp.int32), B),
                           jnp.arange(T, dtype=jnp.int32)])
    pr = _project(x, w, pos)
    dk, dv = w["dk"], w["dv"]

    ba_s, st_s = _gla_group(pr, w, s0_s, B=B, T=T)
    ba_m, st_m = _gla_group(pr, w, jnp.zeros((1, GLA_HEADS, dk, dv), F32), B=1, T=T, row0=ns)

    qlat = _absorb_q(pr["q"], w["w_uk_t3"])
    olat = _attn_latent(qlat, pr["q"], past_lat, past_kr, pr["lat"], pr["kr"], B=B, T=T)
    om_s = _absorb_out(olat, w["w_uv3"])
    q_m, lat_m, kr_m = pr["q"][:, ns:], pr["lat"][ns:], pr["kr"][ns:]
    k_m, v_m = _kvup(lat_m, kr_m, w["w_uk"], w["w_uv"])
    prefix = _kvup(lat_m, kr_m, w["w_uk"], w["w_uv_t"], v_transposed=True)
    om_m = _attention(q_m, k_m, v_m, B=1, Tq=T, Tk=T, tq=T, tk=T, hps=MLA_HEADS,
                      q_off=0, k_off=0)

    hist = jnp.concatenate([hist_s, jnp.zeros((1,) + hist_s.shape[1:], F32)], axis=0)
    y, conv = _finish(x, pr, jnp.concatenate([ba_s, ba_m], axis=0),
                      jnp.concatenate([om_s, om_m], axis=0), w, hist, B=B + 1, T=T)
    sample = (y[:B], pr["lat"][:ns], pr["kr"][:ns], st_s, conv[:B])
    meta = (lat_m, kr_m, st_m, conv[B:], prefix)
    return sample, meta


def _prep_weights(g_mix, w_in, w_a2, b_a, g_gla_out, g_q, w_uq, g_kv, w_uk, w_uv, w_o,
                  g_ffn, w_up, conv_w, conv_b, w_down, final_norm):
    d = w_in.shape[0]
    rank, gqk = w_a2.shape
    gvw = GLA_HEADS * g_gla_out.shape[0]
    rq, rk = g_q.shape[0], g_kv.shape[0]
    half = MLA_ROPE // 2
    o, offs = 0, {}
    for name, width in (("q", gqk), ("k", gqk), ("v", gvw), ("r", gvw), ("a", rank),
                        ("cq", rq), ("ckv", rk), ("kpe", MLA_ROPE), ("ga", d), ("gb", d)):
        offs[name] = (o, o + width)
        o += width
    assert o == w_in.shape[1]
    in_rows = {name: lo for name, (lo, _) in offs.items()}
    in_rows["end"] = o
    assert all(v % 16 == 0 for v in in_rows.values())
    col = {"q": 0, "k": gqk, "v": 2 * gqk, "r": 2 * gqk + gvw, "ga": 0, "gb": d,
           "cq": 0, "ckv": rq, "kpe": rq + rk, "a": rq + rk + 2 * MLA_ROPE}

    w3 = w_uq.reshape(rq, MLA_HEADS, MLA_NOPE + MLA_ROPE)
    pe = w3[:, :, MLA_NOPE:]
    pe_sw = jnp.concatenate([pe[:, :, half:], pe[:, :, :half]], axis=2)
    zpad = jnp.zeros((rq, MLA_HEADS, LANE - MLA_ROPE), w_uq.dtype)
    flat = lambda t: t.reshape(rq, -1).astype(BF16)
    wa_pad = jnp.concatenate([w_a2, jnp.zeros((LANE - rank, gqk), w_a2.dtype)], axis=0)
    return dict(
        col=col, dk=gqk // GLA_HEADS, dv=g_gla_out.shape[0],
        g_mix=g_mix, w_in_t=jnp.swapaxes(w_in, 0, 1), in_rows=in_rows,
        wa_pad=wa_pad.astype(BF16), b_a=b_a, g_gla_out=g_gla_out, g_q=g_q,
        wq_nope=flat(w3[:, :, :MLA_NOPE]),
        wq_pe=flat(jnp.concatenate([pe, zpad], axis=2)),
        wq_pe_sw=flat(jnp.concatenate([pe_sw, zpad], axis=2)),
        g_kv=g_kv, w_uk=w_uk.astype(BF16), w_uv=w_uv.astype(BF16),
        w_uv_t=w_uv.T.astype(BF16),
        w_uk_t3=w_uk.reshape(rk, MLA_HEADS, MLA_NOPE).transpose(1, 2, 0).astype(BF16),
        w_uv3=w_uv.reshape(rk, MLA_HEADS, MLA_V).transpose(1, 0, 2).astype(BF16),
        w_o=w_o.astype(BF16),
        g_ffn=g_ffn, w_up=w_up, conv_w=conv_w, conv_b=conv_b,
        w_down=w_down.astype(BF16), final_norm=final_norm)


def kernel(x_prompt, x_sample, cache_mla_latent, cache_mla_krope, state_gla, cache_ffn_conv,
           meta_tokens, g_mix, w_in, w_a2, b_a, g_gla_out, g_q, w_uq, g_kv, w_uk, w_uv, w_o,
           g_ffn, w_up, conv_w, conv_b, w_down, final_norm):
    assert w_in.shape[0] == 1, "single trunk layer"
    bp, seq, d = x_prompt.shape
    assert bp == 1
    bs, ts, _ = x_sample.shape
    P = cache_mla_latent.shape[2]
    w = _prep_weights(g_mix[0], w_in[0], w_a2[0], b_a[0], g_gla_out[0], g_q[0], w_uq[0],
                      g_kv[0], w_uk[0], w_uv[0], w_o[0], g_ffn[0], w_up[0], conv_w[0],
                      conv_b[0], w_down[0], final_norm)
    dk, dv, dff2 = w["dk"], w["dv"], conv_w.shape[2]

    n_meta = meta_tokens.shape[0]
    assert n_meta == N_META == ts and seq % CHUNK == 0
    (ys, lat_s, kr_s, st_s, cv_s), (lat_m, kr_m, st_m, cv_m, prefix) = _short_streams(
        x_sample.reshape(bs * ts, d), meta_tokens.astype(F32), w, B=bs, T=ts, P=P,
        past_lat=cache_mla_latent[0], past_kr=cache_mla_krope[0], s0_s=state_gla[0],
        hist_s=cache_ffn_conv[0])
    yp, lat_p, kr_p, st_p, cv_p = _long_stream(
        x_prompt[0], w, T=seq, pos=n_meta + jnp.arange(seq, dtype=jnp.int32),
        s0=st_m, hist=cv_m, prefix=prefix)

    rk = lat_p.shape[1]
    T = n_meta + seq
    return (yp,
            ys,
            jnp.concatenate([lat_m, lat_p], axis=0).reshape(1, 1, T, rk),
            jnp.concatenate([kr_m, kr_p], axis=0)[:, :MLA_ROPE].reshape(1, 1, T, MLA_ROPE),
            st_p[None],
            cv_p[None],
            lat_s.reshape(1, bs, ts, rk),
            kr_s[:, :MLA_ROPE].reshape(1, bs, ts, MLA_ROPE),
            st_s[None],
            cv_s[None])
```

```python
import functools

import jax
import jax.numpy as jnp
from jax import lax
from jax.experimental import pallas as pl
from jax.experimental.pallas import tpu as pltpu

BF16 = jnp.bfloat16
F32 = jnp.float32

CHUNK = 64
CHUNK_SHIFT = 6
N_META = 16
EPS = 1e-6
GLA_HEADS = 4
GLA_GATE_NORM = 16.0
GLA_LOG_ALPHA_MIN = -5.0
MLA_HEADS = 16
MLA_NOPE = 128
MLA_ROPE = 64
MLA_V = 128
ROPE_THETA = 10000.0
CONV_W = 3
NEG_BIG = -1e30
LOG2E = 1.4426950408889634
QK_SCALE_LOG2E = (MLA_NOPE + MLA_ROPE) ** -0.5 * LOG2E

LANE = 128
VT_ONES = 16
GLA_CHUNK = 256
FFN_SUB = 512
SAFE_EXP = 64.0
ROW_TILE = 1024
VMEM_LIMIT = 56 * 1024 * 1024


def _cparams(sem, vmem=VMEM_LIMIT):
    return pltpu.CompilerParams(dimension_semantics=sem, vmem_limit_bytes=vmem)


def _rmsnorm(x, g):
    return x * lax.rsqrt(jnp.mean(x * x, axis=-1, keepdims=True) + EPS) * g


def _sigmoid(x):
    return 0.5 * jnp.tanh(0.5 * x) + 0.5


def _pick(n, cands):
    for c in cands:
        if n % c == 0:
            return c
    fits = [t for t in range(16, min(n, max(cands)) + 1, 16) if n % t == 0]
    if not fits:
        raise ValueError(f"no tile in {cands} divides {n}")
    return fits[-1]


_NT = (((1,), (1,)), ((), ()))


def _matmul_wt_kernel(a_ref, b_ref, w_ref, oa_ref, ob_ref, w_scr, *, a_tiles):
    i = pl.program_id(1)

    @pl.when(i == 0)
    def _():
        w_scr[...] = w_ref[...].astype(BF16)

    @pl.when(i < a_tiles)
    def _():
        oa_ref[...] = lax.dot_general(a_ref[...], w_scr[...], _NT,
                                      preferred_element_type=F32).astype(oa_ref.dtype)

    @pl.when(i == a_tiles)
    def _():
        ob_ref[...] = lax.dot_general(b_ref[...], w_scr[...], _NT,
                                      preferred_element_type=F32).astype(ob_ref.dtype)


def _matmul_wt(a, b, w_t, row0, n, out_dtype, tn):
    m, k = a.shape
    mb = b.shape[0]
    tm = _pick(m, (ROW_TILE, 512, 384, 128))
    nt = m // tm
    last = lambda i: jnp.minimum(i, nt - 1)
    return pl.pallas_call(
        functools.partial(_matmul_wt_kernel, a_tiles=nt),
        grid=(n // tn, nt + 1),
        in_specs=[pl.BlockSpec((tm, k), lambda j, i: (last(i), 0)),
                  pl.BlockSpec((mb, k), lambda j, i: (0, 0)),
                  pl.BlockSpec((pl.Element(tn), pl.Element(k)),
                               lambda j, i: (pl.multiple_of(row0 + j * tn, 16), 0))],
        out_specs=[pl.BlockSpec((tm, tn), lambda j, i: (last(i), j)),
                   pl.BlockSpec((mb, tn), lambda j, i: (0, j))],
        out_shape=[jax.ShapeDtypeStruct((m, n), out_dtype),
                   jax.ShapeDtypeStruct((mb, n), out_dtype)],
        scratch_shapes=[pltpu.VMEM((tn, k), BF16)],
        compiler_params=_cparams(("parallel", "arbitrary")),
        name="in_proj_wt",
    )(a, b, w_t)


def _front_kernel(x_ref, g_ref, w_ref, gkv_ref, cos_ref, sin_ref,
                  h_ref, o_ref, lat_ref, kr_ref, w_scr, *, rank, rq, rk):
    @pl.when(pl.program_id(0) == 0)
    def _():
        w = w_ref[...].astype(BF16)
        half = MLA_ROPE // 2
        pe0 = rank + rq + rk
        o_pe = rq + rk
        w_scr[0:rq] = w[rank:rank + rq]
        w_scr[rq:o_pe] = w[rank + rq:pe0]
        w_scr[o_pe:o_pe + MLA_ROPE] = w[pe0:pe0 + MLA_ROPE]
        w_scr[o_pe + MLA_ROPE:o_pe + MLA_ROPE + half] = w[pe0 + half:pe0 + MLA_ROPE]
        w_scr[o_pe + MLA_ROPE + half:o_pe + 2 * MLA_ROPE] = w[pe0:pe0 + half]
        o_a = o_pe + 2 * MLA_ROPE
        w_scr[o_a:o_a + rank] = w[0:rank]
        w_scr[o_a + rank:] = jnp.zeros((w_scr.shape[0] - o_a - rank, w_scr.shape[1]), BF16)

    h = _rmsnorm(x_ref[...], g_ref[...]).astype(BF16)
    h_ref[...] = h
    small = lax.dot_general(h, w_scr[...], _NT, preferred_element_type=F32)
    o_ref[...] = small
    lat_ref[...] = _rmsnorm(small[:, rq:rq + rk], gkv_ref[...])
    blk = small[:, rq + rk:rq + rk + LANE]
    kr_ref[...] = blk * cos_ref[...] + pltpu.roll(blk, LANE // 2, 1) * sin_ref[...]


def _front(x, g_mix, w_t, row0, g_kv, cos_t, sin_t, *, rank, rq, rk):
    m, k = x.shape
    n_in = rank + rq + rk + MLA_ROPE
    n_out = rq + rk + 2 * MLA_ROPE + LANE
    tm = _pick(m, (512, 384, 128))
    kern = functools.partial(_front_kernel, rank=rank, rq=rq, rk=rk)
    row = lambda i: (i, 0)
    return pl.pallas_call(
        kern,
        grid=(m // tm,),
        in_specs=[pl.BlockSpec((tm, k), row),
                  pl.BlockSpec((1, k), lambda i: (0, 0)),
                  pl.BlockSpec((pl.Element(n_in), pl.Element(k)), lambda i: (row0, 0),
                               pipeline_mode=pl.Buffered(1)),
                  pl.BlockSpec((1, rk), lambda i: (0, 0)),
                  pl.BlockSpec((tm, LANE), row),
                  pl.BlockSpec((tm, LANE), row)],
        out_specs=[pl.BlockSpec((tm, k), row),
                   pl.BlockSpec((tm, n_out), row),
                   pl.BlockSpec((tm, rk), row),
                   pl.BlockSpec((tm, LANE), row)],
        out_shape=[jax.ShapeDtypeStruct((m, k), BF16),
                   jax.ShapeDtypeStruct((m, n_out), F32),
                   jax.ShapeDtypeStruct((m, rk), F32),
                   jax.ShapeDtypeStruct((m, LANE), F32)],
        scratch_shapes=[pltpu.VMEM((n_out, k), BF16)],
        compiler_params=_cparams(("arbitrary",)),
        name="front_proj",
    )(x, g_mix.reshape(1, -1), w_t, g_kv.reshape(1, -1), cos_t, sin_t)


def _split3(x):
    a = x.astype(BF16)
    r1 = x - a.astype(F32)
    b = r1.astype(BF16)
    c = (r1 - b.astype(F32)).astype(BF16)
    return a, b, c


def _gla_kernel(q_ref, k_ref, v_ref, r_ref, ga_ref, a_ref, wa_ref, ba_ref, go_ref, s0_ref,
                o_ref, sout_ref, s_scr, *, C, SB, T, H, dk, dv):
    c_idx = pl.program_id(1)
    n_chunks = pl.num_programs(1)

    @pl.when(c_idx == 0)
    def _():
        s_scr[...] = s0_ref[0]

    z = jnp.dot(a_ref[...].astype(BF16), wa_ref[...], preferred_element_type=F32) + ba_ref[...]
    log_sig = jnp.minimum(z, 0.0) - jnp.log(1.0 + jnp.exp(-jnp.abs(z)))
    la = jnp.maximum(log_sig * (1.0 / GLA_GATE_NORM), GLA_LOG_ALPHA_MIN)
    if T % C:
        rows = c_idx * C + lax.broadcasted_iota(jnp.int32, (C, 1), 0)
        la = jnp.where(rows < T, la, 0.0)

    ri = lax.broadcasted_iota(jnp.int32, (C, C), 0)
    ci = lax.broadcasted_iota(jnp.int32, (C, C), 1)
    tri = jnp.where(ri >= ci, 1.0, 0.0).astype(BF16)
    ones = jnp.ones((C, LANE), BF16)
    cs_all = jnp.zeros_like(la)
    dsum_all = jnp.zeros((la.shape[1], LANE), F32)
    for piece in _split3(la):
        cs_all = cs_all + jnp.dot(tri, piece, preferred_element_type=F32)
        dsum_all = dsum_all + lax.dot_general(piece, ones, (((0,), (0,)), ((), ())),
                                              preferred_element_type=F32)

    sr = lax.broadcasted_iota(jnp.int32, (SB, SB), 0)
    sc = lax.broadcasted_iota(jnp.int32, (SB, SB), 1)
    causal = sr >= sc
    nt = (((1,), (1,)), ((), ()))
    scale = dk ** -0.5

    for h in range(H):
        ksl = slice(h * dk, (h + 1) * dk)
        vsl = slice(h * dv, (h + 1) * dv)
        cs = cs_all[:, ksl]
        c_last = cs[C - 1:C, :]
        q = q_ref[:, ksl].astype(F32) * scale
        k = k_ref[:, ksl].astype(F32)
        v = v_ref[:, vsl]
        s_old = s_scr[h]

        o_inter = jnp.dot((q * jnp.exp(cs)).astype(BF16), s_old.astype(BF16),
                          preferred_element_type=F32)
        k_end = (k * jnp.exp(c_last - cs)).astype(BF16)
        upd = lax.dot_general(k_end, v, (((0,), (0,)), ((), ())), preferred_element_type=F32)
        dcol = jnp.exp(dsum_all[ksl, :])
        s_scr[h] = jnp.concatenate([dcol] * (dv // LANE), axis=1) * s_old + upd

        outs = []
        for i in range(C // SB):
            lo = i * SB
            cs_i = cs[lo:lo + SB]
            q_i = q[lo:lo + SB]
            k_i = k[lo:lo + SB]
            start = cs[lo - 1:lo] if i > 0 else jnp.zeros_like(c_last)
            mid = 0.5 * (start + cs[lo + SB - 1:lo + SB])
            qd = (q_i * jnp.exp(cs_i - mid)).astype(BF16)
            kd = (k_i * jnp.exp(mid - cs_i)).astype(BF16)
            att = lax.dot_general(qd, kd, nt, preferred_element_type=F32)
            att = jnp.where(causal, att, 0.0)
            o_i = jnp.dot(att.astype(BF16), v[lo:lo + SB], preferred_element_type=F32)
            if i > 0:
                qo = (q_i * jnp.exp(cs_i - start)).astype(BF16)
                ko = (k[:lo] * jnp.exp(start - cs[:lo])).astype(BF16)
                att_o = lax.dot_general(qo, ko, nt, preferred_element_type=F32)
                o_i = o_i + jnp.dot(att_o.astype(BF16), v[:lo], preferred_element_type=F32)
            outs.append(o_i)
        o = o_inter + (jnp.concatenate(outs, axis=0) if len(outs) > 1 else outs[0])

        on = _rmsnorm(o, go_ref[...])
        r = r_ref[:, vsl].astype(F32)
        g = ga_ref[:, vsl].astype(F32)
        o_ref[:, vsl] = (_sigmoid(g) * (on * (r * _sigmoid(r)))).astype(o_ref.dtype)

    @pl.when(c_idx == n_chunks - 1)
    def _():
        sout_ref[0] = s_scr[...]


def _gla(qkvr, gates, small, wa_pad, b_a, g_out, s0, *, B, T, Tp, dk, dv, col, row0=0):
    C = min(GLA_CHUNK, Tp)
    SB = min(32, C)
    nc = Tp // C
    H = GLA_HEADS
    qk, vw = H * dk, H * dv
    assert row0 % C == 0
    rb = lambda b, c: row0 // C + b * nc + c
    kern = functools.partial(_gla_kernel, C=C, SB=SB, T=T, H=H, dk=dk, dv=dv)
    return pl.pallas_call(
        kern,
        grid=(B, nc),
        in_specs=[
            pl.BlockSpec((C, qk), lambda b, c: (rb(b, c), col["q"] // qk)),
            pl.BlockSpec((C, qk), lambda b, c: (rb(b, c), col["k"] // qk)),
            pl.BlockSpec((C, vw), lambda b, c: (rb(b, c), col["v"] // vw)),
            pl.BlockSpec((C, vw), lambda b, c: (rb(b, c), col["r"] // vw)),
            pl.BlockSpec((C, vw), lambda b, c: (rb(b, c), col["ga"] // vw)),
            pl.BlockSpec((C, LANE), lambda b, c: (rb(b, c), col["a"] // LANE)),
            pl.BlockSpec((LANE, qk), lambda b, c: (0, 0)),
            pl.BlockSpec((1, qk), lambda b, c: (0, 0)),
            pl.BlockSpec((1, dv), lambda b, c: (0, 0)),
            pl.BlockSpec((1, H, dk, dv), lambda b, c: (b, 0, 0, 0)),
        ],
        out_specs=[
            pl.BlockSpec((C, vw), lambda b, c: (b * nc + c, 0)),
            pl.BlockSpec((1, H, dk, dv), lambda b, c: (b, 0, 0, 0)),
        ],
        out_shape=[jax.ShapeDtypeStruct((B * Tp, vw), BF16),
                   jax.ShapeDtypeStruct((B, H, dk, dv), F32)],
        scratch_shapes=[pltpu.VMEM((H, dk, dv), F32)],
        compiler_params=_cparams(("parallel", "arbitrary")),
        name="gla",
    )(qkvr, qkvr, qkvr, qkvr, gates, small, wa_pad, b_a.reshape(1, -1), g_out.reshape(1, -1), s0)


def _qprep_kernel(cq_ref, gq_ref, wn_ref, wp_ref, wps_ref, cos_ref, sin_ref, q_ref):
    hq = _rmsnorm(cq_ref[...], gq_ref[...]).astype(BF16)
    qn = jnp.dot(hq, wn_ref[...], preferred_element_type=F32)
    qp = jnp.dot(hq, wp_ref[...], preferred_element_type=F32)
    qs = jnp.dot(hq, wps_ref[...], preferred_element_type=F32)
    cos = cos_ref[...] * QK_SCALE_LOG2E
    sin = sin_ref[...] * QK_SCALE_LOG2E
    for h in range(MLA_HEADS):
        sl = slice(h * LANE, (h + 1) * LANE)
        q_ref[h, :, 0:LANE] = (qn[:, sl] * QK_SCALE_LOG2E).astype(BF16)
        q_ref[h, :, LANE:2 * LANE] = (qp[:, sl] * cos + qs[:, sl] * sin).astype(BF16)


def _qprep(small, g_q, wn, wp, wps, cos_t, sin_t, *, col):
    m = small.shape[0]
    rq = wn.shape[0]
    tm = _pick(m, (256, 128))
    full = lambda i: (0, 0)
    return pl.pallas_call(
        _qprep_kernel,
        grid=(m // tm,),
        in_specs=[pl.BlockSpec((tm, rq), lambda i: (i, col["cq"] // rq)),
                  pl.BlockSpec((1, rq), full),
                  pl.BlockSpec(wn.shape, full),
                  pl.BlockSpec(wp.shape, full),
                  pl.BlockSpec(wps.shape, full),
                  pl.BlockSpec((tm, LANE), lambda i: (i, 0)),
                  pl.BlockSpec((tm, LANE), lambda i: (i, 0))],
        out_specs=pl.BlockSpec((MLA_HEADS, tm, 2 * LANE), lambda i: (0, i, 0)),
        out_shape=jax.ShapeDtypeStruct((MLA_HEADS, m, 2 * LANE), BF16),
        compiler_params=_cparams(("parallel",)),
        name="mla_q",
    )(small, g_q.reshape(1, -1), wn, wp, wps, cos_t, sin_t)


def _kvup_kernel(lat_ref, kr_ref, wuk_ref, wuv_ref, k_ref, v_ref, *, v_transposed):
    lat = lat_ref[...].astype(BF16)
    kn = jnp.dot(lat, wuk_ref[...], preferred_element_type=F32)
    kr = kr_ref[...]
    lane = lax.broadcasted_iota(jnp.int32, kr.shape, 1)
    kp = jnp.where(lane == MLA_ROPE, 1.0, kr).astype(BF16)
    if v_transposed:
        vv = lax.dot_general(wuv_ref[...], lat, (((1,), (1,)), ((), ())),
                             preferred_element_type=F32)
    else:
        vv = jnp.dot(lat, wuv_ref[...], preferred_element_type=F32)
    for h in range(MLA_HEADS):
        sl = slice(h * LANE, (h + 1) * LANE)
        k_ref[h, :, 0:LANE] = kn[:, sl].astype(BF16)
        k_ref[h, :, LANE:2 * LANE] = kp
        if v_transposed:
            v_ref[h, 0:LANE, :] = vv[sl, :].astype(BF16)
            v_ref[h, LANE:LANE + VT_ONES, :] = jnp.ones((VT_ONES, vv.shape[1]), BF16)
        else:
            v_ref[h] = vv[:, sl].astype(BF16)


def _kvup(lat, kr, wuk, wuv, *, v_transposed=False):
    m, rk = lat.shape
    tm = _pick(m, (512, 256, 128))
    full = lambda i: (0, 0)
    if v_transposed:
        v_spec = pl.BlockSpec((MLA_HEADS, LANE + VT_ONES, tm), lambda i: (0, 0, i))
        v_shape = (MLA_HEADS, LANE + VT_ONES, m)
    else:
        v_spec = pl.BlockSpec((MLA_HEADS, tm, LANE), lambda i: (0, i, 0))
        v_shape = (MLA_HEADS, m, LANE)
    return pl.pallas_call(
        functools.partial(_kvup_kernel, v_transposed=v_transposed),
        grid=(m // tm,),
        in_specs=[pl.BlockSpec((tm, rk), lambda i: (i, 0)),
                  pl.BlockSpec((tm, LANE), lambda i: (i, 0)),
                  pl.BlockSpec(wuk.shape, full),
                  pl.BlockSpec(wuv.shape, full)],
        out_specs=[pl.BlockSpec((MLA_HEADS, tm, 2 * LANE), lambda i: (0, i, 0)), v_spec],
        out_shape=[jax.ShapeDtypeStruct((MLA_HEADS, m, 2 * LANE), BF16),
                   jax.ShapeDtypeStruct(v_shape, BF16)],
        compiler_params=_cparams(("parallel",)),
        name="mla_kv",
    )(lat, kr, wuk, wuv)


def _last_kblock(qi, *, tq, tk, nk, q_off, k_off):
    top_chunk = ((qi + 1) * tq - 1 + q_off) // CHUNK
    last_key = (top_chunk + 1) * CHUNK - 1 - k_off
    return jnp.minimum(last_key // tk, nk - 1)


def _attn_kernel(q_ref, k_ref, v_ref, o_ref, m_scr, l_scr, acc_scr, *, hps, tq, tk, nk,
                 q_off, k_off):
    qi = pl.program_id(2)
    ki = pl.program_id(3)

    @pl.when(ki == 0)
    def _():
        m_scr[...] = jnp.full(m_scr.shape, NEG_BIG, F32)
        l_scr[...] = jnp.zeros(l_scr.shape, F32)
        acc_scr[...] = jnp.zeros(acc_scr.shape, F32)

    @pl.when(ki <= _last_kblock(qi, tq=tq, tk=tk, nk=nk, q_off=q_off, k_off=k_off))
    def _():
        q_chunk = (qi * tq + q_off + lax.broadcasted_iota(jnp.int32, (tq, 1), 0)) >> CHUNK_SHIFT
        k_chunk = (ki * tk + k_off + lax.broadcasted_iota(jnp.int32, (1, tk), 1)) >> CHUNK_SHIFT
        visible = q_chunk >= k_chunk

        def head(h, carry):
            s = lax.dot_general(q_ref[h], k_ref[h], (((1,), (1,)), ((), ())),
                                preferred_element_type=F32)
            s = jnp.where(visible, s, NEG_BIG)
            m_prev = m_scr[h]
            m_new = jnp.maximum(m_prev, jnp.max(s, axis=-1, keepdims=True))
            p = jnp.exp2(s - m_new)
            alpha = jnp.exp2(m_prev - m_new)
            l_scr[h] = alpha * l_scr[h] + jnp.sum(p, axis=-1, keepdims=True)
            acc_scr[h] = alpha * acc_scr[h] + jnp.dot(p.astype(BF16), v_ref[h],
                                                      preferred_element_type=F32)
            m_scr[h] = m_new
            return carry

        lax.fori_loop(0, hps, head, 0)

    @pl.when(ki == nk - 1)
    def _():
        for h in range(hps):
            o_ref[:, h * LANE:(h + 1) * LANE] = (acc_scr[h] / l_scr[h]).astype(o_ref.dtype)


def _attention(q, k, v, *, B, Tq, Tk, tq, tk, hps, q_off, k_off):
    nq = Tq // tq
    nk = Tk // tk
    hg = MLA_HEADS // hps
    dqk = q.shape[2]
    dvh = v.shape[2]
    last = functools.partial(_last_kblock, tq=tq, tk=tk, nk=nk, q_off=q_off, k_off=k_off)
    kern = functools.partial(_attn_kernel, hps=hps, tq=tq, tk=tk, nk=nk, q_off=q_off,
                             k_off=k_off)
    kv_row = lambda b, g, i, j: b * nk + jnp.minimum(j, last(i))
    return pl.pallas_call(
        kern,
        grid=(B, hg, nq, nk),
        in_specs=[pl.BlockSpec((hps, tq, dqk), lambda b, g, i, j: (g, b * nq + i, 0)),
                  pl.BlockSpec((hps, tk, dqk), lambda b, g, i, j: (g, kv_row(b, g, i, j), 0)),
                  pl.BlockSpec((hps, tk, dvh), lambda b, g, i, j: (g, kv_row(b, g, i, j), 0))],
        out_specs=pl.BlockSpec((tq, hps * dvh), lambda b, g, i, j: (b * nq + i, g)),
        out_shape=jax.ShapeDtypeStruct((B * Tq, MLA_HEADS * dvh), BF16),
        scratch_shapes=[pltpu.VMEM((hps, tq, 1), F32),
                        pltpu.VMEM((hps, tq, 1), F32),
                        pltpu.VMEM((hps, tq, dvh), F32)],
        compiler_params=_cparams(("parallel", "parallel", "parallel", "arbitrary")),
        name="mla_attn",
    )(q, k, v)


def _attn_t_kernel(qi_ref, ki_ref, q_ref, k_ref, vt_ref, kp_ref, vtp_ref, o_ref,
                   q_scr, r_scr, acc_scr, *, hps, t):
    pair = pl.program_id(1)
    qi = qi_ref[pair]
    ki = ki_ref[pair]
    nt = (((1,), (1,)), ((), ()))
    pe = slice(LANE, 2 * LANE)
    lane = lax.broadcasted_iota(jnp.int32, (t, LANE), 1)

    def set_reference(h, r):
        neg_r = jnp.transpose(jnp.broadcast_to(-r, (LANE, t)))
        q_scr[h, :, pe] = jnp.where(lane == MLA_ROPE, neg_r.astype(BF16), q_ref[h, :, pe])
        r_scr[h] = r

    def shifted_scores(h):
        return lax.dot_general(k_ref[h], q_scr[h], nt, preferred_element_type=F32)

    @pl.when(ki == 0)
    def _():
        for h in range(hps):
            q_scr[h, :, 0:LANE] = q_ref[h, :, 0:LANE]
            s = lax.dot_general(kp_ref[h], q_ref[h], nt, preferred_element_type=F32)
            r = jnp.max(s, axis=0, keepdims=True).astype(BF16).astype(F32)
            p = jnp.exp2((s - r).astype(BF16))
            acc_scr[h] = jnp.dot(vtp_ref[h], p, preferred_element_type=F32)
            set_reference(h, r)

    def general(h, bias, keep_reference):
        sp = shifted_scores(h)
        if bias is not None:
            sp = sp + bias
        r = r_scr[h]
        rise = jnp.maximum(jnp.max(sp, axis=0, keepdims=True), 0.0)
        r_new = (r + rise).astype(BF16).astype(F32)
        delta = r_new - r
        p = jnp.exp2((sp - delta).astype(BF16))
        acc_scr[h] = jnp.exp2(-delta) * acc_scr[h] + jnp.dot(vt_ref[h], p,
                                                               preferred_element_type=F32)
        if keep_reference:
            set_reference(h, r_new)

    @pl.when(ki < qi)
    def _():
        unsafe = []
        sp_next = shifted_scores(0)
        for h in range(hps):
            sp = sp_next
            if h + 1 < hps:
                sp_next = shifted_scores(h + 1)
            safe = jnp.max(sp) <= SAFE_EXP
            part = jnp.dot(vt_ref[h], jnp.exp2(sp.astype(BF16)), preferred_element_type=F32)
            acc_scr[h] += jnp.where(safe, part, 0.0)
            unsafe.append(jnp.logical_not(safe))

        @pl.when(functools.reduce(jnp.logical_or, unsafe))
        def _():
            for h in range(hps):
                @pl.when(unsafe[h])
                def _():
                    general(h, None, True)

    @pl.when(ki == qi)
    def _():
        k_chunk = lax.broadcasted_iota(jnp.int32, (t, 1), 0) >> CHUNK_SHIFT
        q_chunk = lax.broadcasted_iota(jnp.int32, (1, t), 1) >> CHUNK_SHIFT
        bias = jnp.where(q_chunk >= k_chunk, 0.0, NEG_BIG)
        for h in range(hps):
            general(h, bias, False)
            acc = acc_scr[h]
            o_t = acc[0:LANE] / acc[LANE:LANE + 1]
            o_ref[:, h * LANE:(h + 1) * LANE] = o_t.T.astype(o_ref.dtype)


def _attention_t(q, k, vt, k_pre, vt_pre, *, T, t, hps):
    n = T // t
    hg = MLA_HEADS // hps
    dqk = q.shape[2]
    npre = k_pre.shape[1]
    vrows = vt.shape[1]
    pairs = [(i, j) for i in range(n) for j in range(i + 1)]
    qi_arr = jnp.asarray([p[0] for p in pairs], jnp.int32)
    ki_arr = jnp.asarray([p[1] for p in pairs], jnp.int32)
    kern = functools.partial(_attn_t_kernel, hps=hps, t=t)
    grid_spec = pltpu.PrefetchScalarGridSpec(
        num_scalar_prefetch=2,
        grid=(hg, len(pairs)),
        in_specs=[pl.BlockSpec((hps, t, dqk), lambda g, p, qi, ki: (g, qi[p], 0)),
                  pl.BlockSpec((hps, t, dqk), lambda g, p, qi, ki: (g, ki[p], 0)),
                  pl.BlockSpec((hps, vrows, t), lambda g, p, qi, ki: (g, 0, ki[p])),
                  pl.BlockSpec((hps, npre, dqk), lambda g, p, qi, ki: (g, 0, 0)),
                  pl.BlockSpec((hps, vrows, npre), lambda g, p, qi, ki: (g, 0, 0))],
        out_specs=pl.BlockSpec((t, hps * LANE), lambda g, p, qi, ki: (qi[p], g)),
        scratch_shapes=[pltpu.VMEM((hps, t, dqk), BF16),
                        pltpu.VMEM((hps, 1, t), F32),
                        pltpu.VMEM((hps, vrows, t), F32)])
    return pl.pallas_call(
        kern,
        grid_spec=grid_spec,
        out_shape=jax.ShapeDtypeStruct((T, MLA_HEADS * LANE), BF16),
        compiler_params=_cparams(("parallel", "arbitrary")),
        name="mla_attn_t",
    )(qi_arr, ki_arr, q, k, vt, k_pre, vt_pre)


def _absorb_q_kernel(q_ref, w_ref, o_ref):
    o_ref[0] = jnp.dot(q_ref[0, :, 0:MLA_NOPE], w_ref[0],
                       preferred_element_type=F32).astype(o_ref.dtype)


def _absorb_q(q, w_uk_t3):
    heads, rows, dqk = q.shape
    rk = w_uk_t3.shape[2]
    return pl.pallas_call(
        _absorb_q_kernel,
        grid=(heads,),
        in_specs=[pl.BlockSpec((1, rows, dqk), lambda h: (h, 0, 0)),
                  pl.BlockSpec((1, MLA_NOPE, rk), lambda h: (h, 0, 0))],
        out_specs=pl.BlockSpec((1, rows, rk), lambda h: (h, 0, 0)),
        out_shape=jax.ShapeDtypeStruct((heads, rows, rk), BF16),
        compiler_params=_cparams(("parallel",)),
        name="mla_absorb_q",
    )(q, w_uk_t3)


def _attn_latent_kernel(ql_ref, q_ref, plat_ref, pkr_ref, lat_ref, kr_ref, o_ref, *, T, P):
    heads, _, rk = ql_ref.shape
    rows = heads * T
    nt = (((1,), (1,)), ((), ()))
    ql = ql_ref[...].reshape(rows, rk)
    qpe = q_ref[:, :, LANE:2 * LANE].reshape(rows, LANE)[:, 0:MLA_ROPE]
    lat_all = jnp.concatenate([plat_ref[0].astype(BF16), lat_ref[...].astype(BF16)], axis=0)
    kr_all = jnp.concatenate([pkr_ref[0], kr_ref[:, 0:MLA_ROPE]], axis=0).astype(BF16)
    s = (lax.dot_general(ql, lat_all, nt, preferred_element_type=F32)
         + lax.dot_general(qpe, kr_all, nt, preferred_element_type=F32))
    tok = lax.rem(lax.broadcasted_iota(jnp.int32, (rows, 1), 0), T)
    q_chunk = (P + tok) >> CHUNK_SHIFT
    k_chunk = lax.broadcasted_iota(jnp.int32, (1, P + T), 1) >> CHUNK_SHIFT
    s = jnp.where(q_chunk >= k_chunk, s, NEG_BIG)
    p = jnp.exp2(s - jnp.max(s, axis=-1, keepdims=True))
    o = jnp.dot(p.astype(BF16), lat_all, preferred_element_type=F32)
    o = o / jnp.sum(p, axis=-1, keepdims=True)
    o_ref[...] = o.reshape(heads, T, rk).astype(o_ref.dtype)


def _attn_latent(qlat, q, past_lat, past_kr, lat, kr, *, B, T):
    heads, _, rk = qlat.shape
    P = past_lat.shape[1]
    kern = functools.partial(_attn_latent_kernel, T=T, P=P)
    return pl.pallas_call(
        kern,
        grid=(B,),
        in_specs=[pl.BlockSpec((heads, T, rk), lambda b: (0, b, 0)),
                  pl.BlockSpec((heads, T, q.shape[2]), lambda b: (0, b, 0)),
                  pl.BlockSpec((1, P, rk), lambda b: (b, 0, 0)),
                  pl.BlockSpec((1, P, past_kr.shape[2]), lambda b: (b, 0, 0)),
                  pl.BlockSpec((T, rk), lambda b: (b, 0)),
                  pl.BlockSpec((T, LANE), lambda b: (b, 0))],
        out_specs=pl.BlockSpec((heads, T, rk), lambda b: (0, b, 0)),
        out_shape=jax.ShapeDtypeStruct((heads, B * T, rk), BF16),
        compiler_params=_cparams(("parallel",)),
        name="mla_attn_latent",
    )(qlat, q, past_lat, past_kr, lat, kr)


def _absorb_out_kernel(o_ref, w_ref, out_ref):
    out_ref[...] = jnp.dot(o_ref[0], w_ref[0], preferred_element_type=F32).astype(out_ref.dtype)


def _absorb_out(olat, w_uv3):
    heads, rows, rk = olat.shape
    dvh = w_uv3.shape[2]
    return pl.pallas_call(
        _absorb_out_kernel,
        grid=(heads,),
        in_specs=[pl.BlockSpec((1, rows, rk), lambda h: (h, 0, 0)),
                  pl.BlockSpec((1, rk, dvh), lambda h: (h, 0, 0))],
        out_specs=pl.BlockSpec((rows, dvh), lambda h: (0, h)),
        out_shape=jax.ShapeDtypeStruct((rows, heads * dvh), BF16),
        compiler_params=_cparams(("parallel",)),
        name="mla_absorb_out",
    )(olat, w_uv3)


def _merge_kernel(a_ref, gb_ref, om_ref, x_ref, wo_ref, gf_ref, x1_ref, h2_ref):
    merged = a_ref[...].astype(F32) + _sigmoid(gb_ref[...].astype(F32)) * om_ref[...].astype(F32)
    x1 = x_ref[...] + jnp.dot(merged.astype(BF16), wo_ref[...], preferred_element_type=F32)
    x1_ref[...] = x1
    h2_ref[...] = _rmsnorm(x1, gf_ref[...]).astype(BF16)


def _merge(branch_a, gates, o_m, x, wo, g_ffn, *, col):
    m, d = x.shape
    tm = _pick(m, (512, 384, 256, 128))
    row = lambda i: (i, 0)
    return pl.pallas_call(
        _merge_kernel,
        grid=(m // tm,),
        in_specs=[pl.BlockSpec((tm, d), row),
                  pl.BlockSpec((tm, d), lambda i: (i, col["gb"] // d)),
                  pl.BlockSpec((tm, d), row),
                  pl.BlockSpec((tm, d), row),
                  pl.BlockSpec(wo.shape, lambda i: (0, 0), pipeline_mode=pl.Buffered(1)),
                  pl.BlockSpec((1, d), lambda i: (0, 0))],
        out_specs=[pl.BlockSpec((tm, d), row), pl.BlockSpec((tm, d), row)],
        out_shape=[jax.ShapeDtypeStruct((m, d), F32), jax.ShapeDtypeStruct((m, d), BF16)],
        compiler_params=_cparams(("parallel",)),
        name="merge_out_proj",
    )(branch_a, gates, o_m, x, wo, g_ffn.reshape(1, -1))


HALO = 8


def _ffn_up_kernel(*refs, bb, r, tf, loc, carried, cast_down):
    (h_ref, wa_ref, wb_ref, cwa_ref, cwb_ref, cba_ref, cbb_ref, ha_ref, hb_ref), refs = \
        refs[:9], refs[9:]
    if cast_down:
        wd_ref, act_ref, ca_ref, cb_ref, wdb_ref, ext_scr, carry_scr, w_scr = refs
    else:
        act_ref, ca_ref, cb_ref, ext_scr, carry_scr, w_scr = refs
    s = pl.program_id(1)
    rt = pl.program_id(2)
    d = h_ref.shape[2]

    @pl.when((s == 0) & (rt == 0))
    def _():
        w_scr[0] = wa_ref[...].astype(BF16)
        w_scr[1] = wb_ref[...].astype(BF16)
        if cast_down:
            wdb_ref[...] = wd_ref[...].astype(BF16)

    if carried:
        @pl.when(rt == 0)
        def _():
            carry_scr[0] = ha_ref[...]
            carry_scr[1] = hb_ref[...]

    h = h_ref[...].reshape(bb * r, d)
    sw = min(FFN_SUB, tf)
    for c0 in range(0, tf, sw):
        cs = slice(c0, c0 + sw)
        conv = []
        for half, (cw_ref, cbias_ref, hist_ref, cout_ref) in enumerate(
                ((cwa_ref, cba_ref, ha_ref, ca_ref), (cwb_ref, cbb_ref, hb_ref, cb_ref))):
            u = jnp.dot(h, w_scr[half, :, cs], preferred_element_type=F32).reshape(bb, r, sw)
            ext_scr[half, :, HALO:HALO + r, cs] = u
            ext_scr[half, :, HALO - 2:HALO, cs] = (carry_scr[half, :, :, cs] if carried
                                                   else hist_ref[:, :, cs])
            u1 = ext_scr[half, :, HALO - 1:HALO - 1 + r, cs]
            u2 = ext_scr[half, :, HALO - 2:HALO - 2 + r, cs]
            cw = cw_ref[:, cs]
            conv.append(cbias_ref[:, cs] + cw[0:1] * u2 + cw[1:2] * u1 + cw[2:3] * u)
            if carried:
                carry_scr[half, :, :, cs] = ext_scr[half, :, HALO + r - 2:HALO + r, cs]
            cout_ref[0, :, :, cs] = ext_scr[half, :, HALO + loc:HALO + loc + 2, cs]

        act_ref[:, :, cs] = ((conv[0] * _sigmoid(conv[0])) * conv[1]).astype(act_ref.dtype)


def _ffn_down_kernel(act_ref, wd_ref, x1_ref, gf_ref, y_ref):
    down = jnp.dot(act_ref[...], wd_ref[...], preferred_element_type=F32)
    y_ref[...] = _rmsnorm(x1_ref[...] + down, gf_ref[...])


def _ffn(h2, x1, w_up, w_down, conv_w, conv_b, hist, g_final, *, B, T, Tp):
    d = h2.shape[1]
    dff = w_down.shape[0]
    cast_down = w_down.dtype != BF16
    tf = _pick(dff, (512, 256, 128))
    nf = dff // tf
    if Tp <= 128:
        bb, r = B, Tp
    else:
        bb, r = 1, _pick(Tp, (ROW_TILE, 128))
    nrt = Tp // r
    carried = nrt > 1
    loc = (T - 2) - (nrt - 1) * r
    assert 0 <= loc <= r - 2, "final two valid rows must sit in the last row tile"
    kern = functools.partial(_ffn_up_kernel, bb=bb, r=r, tf=tf, loc=loc, carried=carried,
                             cast_down=cast_down)
    carry_shape = (2, bb, 2, tf) if carried else (1, 1, 2, LANE)
    in_specs = [pl.BlockSpec((bb, r, d), lambda f, s, t: (s, t, 0)),
                pl.BlockSpec((d, tf), lambda f, s, t: (0, f)),
                pl.BlockSpec((d, tf), lambda f, s, t: (0, nf + f)),
                pl.BlockSpec((CONV_W, tf), lambda f, s, t: (0, f)),
                pl.BlockSpec((CONV_W, tf), lambda f, s, t: (0, nf + f)),
                pl.BlockSpec((1, tf), lambda f, s, t: (0, f)),
                pl.BlockSpec((1, tf), lambda f, s, t: (0, nf + f)),
                pl.BlockSpec((bb, 2, tf), lambda f, s, t: (s, 0, f)),
                pl.BlockSpec((bb, 2, tf), lambda f, s, t: (s, 0, nf + f))]
    out_specs = [pl.BlockSpec((bb, r, tf), lambda f, s, t: (s, t, f)),
                 pl.BlockSpec((1, bb, 2, tf), lambda f, s, t: (t, s, 0, f)),
                 pl.BlockSpec((1, bb, 2, tf), lambda f, s, t: (t, s, 0, f))]
    out_shape = [jax.ShapeDtypeStruct((B, Tp, dff), BF16),
                 jax.ShapeDtypeStruct((nrt, B, 2, dff), F32),
                 jax.ShapeDtypeStruct((nrt, B, 2, dff), F32)]
    args = [h2.reshape(B, Tp, d), w_up, w_up, conv_w, conv_w, conv_b.reshape(1, -1),
            conv_b.reshape(1, -1), hist, hist]
    if cast_down:
        in_specs.append(pl.BlockSpec((tf, d), lambda f, s, t: (f, 0)))
        out_specs.append(pl.BlockSpec((tf, d), lambda f, s, t: (f, 0)))
        out_shape.append(jax.ShapeDtypeStruct((dff, d), BF16))
        args.append(w_down)
    outs = pl.pallas_call(
        kern,
        grid=(nf, B // bb, nrt),
        in_specs=in_specs,
        out_specs=out_specs,
        out_shape=out_shape,
        scratch_shapes=[pltpu.VMEM((2, bb, HALO + r, tf), F32),
                        pltpu.VMEM(carry_shape, F32),
                        pltpu.VMEM((2, d, tf), BF16)],
        compiler_params=_cparams(("arbitrary", "arbitrary", "arbitrary")),
        name="conv_ffn_up",
    )(*args)
    act, ca, cb = outs[:3]
    if cast_down:
        w_down = outs[3]

    m = B * Tp
    tm = _pick(m, (256, 128))
    y = pl.pallas_call(
        _ffn_down_kernel,
        grid=(m // tm,),
        in_specs=[pl.BlockSpec((tm, dff), lambda i: (i, 0)),
                  pl.BlockSpec((dff, d), lambda i: (0, 0), pipeline_mode=pl.Buffered(1)),
                  pl.BlockSpec((tm, d), lambda i: (i, 0)),
                  pl.BlockSpec((1, d), lambda i: (0, 0))],
        out_specs=pl.BlockSpec((tm, d), lambda i: (i, 0)),
        out_shape=jax.ShapeDtypeStruct((m, d), F32),
        compiler_params=_cparams(("parallel",)),
        name="ffn_down",
    )(act.reshape(m, dff), w_down, x1, g_final.reshape(1, -1))
    return y.reshape(B, Tp, d), jnp.concatenate([ca[nrt - 1], cb[nrt - 1]], axis=-1), w_down


def _rope_tables(pos):
    half = MLA_ROPE // 2
    inv = ROPE_THETA ** (-jnp.arange(0, MLA_ROPE, 2, dtype=F32) / MLA_ROPE)
    ang = pos.astype(F32)[:, None] * inv[None, :]
    cos, sin = jnp.cos(ang), jnp.sin(ang)
    zero = jnp.zeros((pos.shape[0], LANE - 2 * half), F32)
    return (jnp.concatenate([cos, cos, zero], axis=1),
            jnp.concatenate([-sin, sin, zero], axis=1))


def _project(groups, w):
    col = w["col"]
    rows = w["in_rows"]
    out = []
    for x, pos in groups:
        cos_t, sin_t = _rope_tables(pos)
        h, small, lat, kr = _front(x, w["g_mix"], w["w_in_t"], rows["a"], w["g_kv"], cos_t,
                                   sin_t, rank=rows["cq"] - rows["a"],
                                   rq=rows["ckv"] - rows["cq"], rk=rows["kpe"] - rows["ckv"])
        q = _qprep(small, w["g_q"], w["wq_nope"], w["wq_pe"], w["wq_pe_sw"], cos_t, sin_t,
                   col=col)
        out.append(dict(h=h, small=small, q=q, lat=lat, kr=kr))
    (a, b) = out
    a["qkvr"], b["qkvr"] = _matmul_wt(a["h"], b["h"], w["w_in_t"], rows["q"],
                                      rows["a"] - rows["q"], BF16, tn=1024)
    a["gates"], b["gates"] = _matmul_wt(a["h"], b["h"], w["w_in_t"], rows["ga"],
                                        rows["end"] - rows["ga"], BF16, tn=1024)
    return a, b


def _finish(x, pr, branch_a, o_m, w, hist, *, B, T):
    x1, h2 = _merge(branch_a, pr["gates"], o_m, x, w["w_o"], w["g_ffn"], col=w["col"])
    y, conv, w["w_down"] = _ffn(h2, x1, w["w_up"], w["w_down"], w["conv_w"], w["conv_b"], hist,
                                w["final_norm"], B=B, T=T, Tp=T)
    return y, conv


def _gla_group(pr, w, s0, *, B, T, row0=0):
    return _gla(pr["qkvr"], pr["gates"], pr["small"], w["wa_pad"], w["b_a"], w["g_gla_out"],
                s0, B=B, T=T, Tp=T, dk=w["dk"], dv=w["dv"], col=w["col"], row0=row0)


def _long_stream(x, pr, w, *, T, s0, hist, prefix):
    branch_a, state = _gla_group(pr, w, s0, B=1, T=T)
    k, vt = _kvup(pr["lat"], pr["kr"], w["w_uk"], w["w_uv_t"], v_transposed=True)
    o_m = _attention_t(pr["q"], k, vt, prefix[0], prefix[1], T=T, t=_pick(T, (1024, 128)),
                       hps=MLA_HEADS // 4)
    y, conv = _finish(x, pr, branch_a, o_m, w, hist, B=1, T=T)
    return y, pr["lat"], pr["kr"], state, conv


def _short_streams(x, pr, w, *, B, T, past_lat, past_kr, s0_s, hist_s):
    ns = B * T
    dk, dv = w["dk"], w["dv"]

    ba_s, st_s = _gla_group(pr, w, s0_s, B=B, T=T)
    ba_m, st_m = _gla_group(pr, w, jnp.zeros((1, GLA_HEADS, dk, dv), F32), B=1, T=T, row0=ns)

    qlat = _absorb_q(pr["q"], w["w_uk_t3"])
    olat = _attn_latent(qlat, pr["q"], past_lat, past_kr, pr["lat"], pr["kr"], B=B, T=T)
    om_s = _absorb_out(olat, w["w_uv3"])
    q_m, lat_m, kr_m = pr["q"][:, ns:], pr["lat"][ns:], pr["kr"][ns:]
    k_m, v_m = _kvup(lat_m, kr_m, w["w_uk"], w["w_uv"])
    prefix = _kvup(lat_m, kr_m, w["w_uk"], w["w_uv_t"], v_transposed=True)
    om_m = _attention(q_m, k_m, v_m, B=1, Tq=T, Tk=T, tq=T, tk=T, hps=MLA_HEADS,
                      q_off=0, k_off=0)

    hist = jnp.concatenate([hist_s, jnp.zeros((1,) + hist_s.shape[1:], F32)], axis=0)
    y, conv = _finish(x, pr, jnp.concatenate([ba_s, ba_m], axis=0),
                      jnp.concatenate([om_s, om_m], axis=0), w, hist, B=B + 1, T=T)
    sample = (y[:B], pr["lat"][:ns], pr["kr"][:ns], st_s, conv[:B])
    meta = (lat_m, kr_m, st_m, conv[B:], prefix)
    return sample, meta


def _prep_weights(g_mix, w_in, w_a2, b_a, g_gla_out, g_q, w_uq, g_kv, w_uk, w_uv, w_o,
                  g_ffn, w_up, conv_w, conv_b, w_down, final_norm):
    d = w_in.shape[0]
    rank, gqk = w_a2.shape
    gvw = GLA_HEADS * g_gla_out.shape[0]
    rq, rk = g_q.shape[0], g_kv.shape[0]
    half = MLA_ROPE // 2
    o, offs = 0, {}
    for name, width in (("q", gqk), ("k", gqk), ("v", gvw), ("r", gvw), ("a", rank),
                        ("cq", rq), ("ckv", rk), ("kpe", MLA_ROPE), ("ga", d), ("gb", d)):
        offs[name] = (o, o + width)
        o += width
    assert o == w_in.shape[1]
    in_rows = {name: lo for name, (lo, _) in offs.items()}
    in_rows["end"] = o
    assert all(v % 16 == 0 for v in in_rows.values())
    col = {"q": 0, "k": gqk, "v": 2 * gqk, "r": 2 * gqk + gvw, "ga": 0, "gb": d,
           "cq": 0, "ckv": rq, "kpe": rq + rk, "a": rq + rk + 2 * MLA_ROPE}

    w3 = w_uq.reshape(rq, MLA_HEADS, MLA_NOPE + MLA_ROPE)
    pe = w3[:, :, MLA_NOPE:]
    pe_sw = jnp.concatenate([pe[:, :, half:], pe[:, :, :half]], axis=2)
    zpad = jnp.zeros((rq, MLA_HEADS, LANE - MLA_ROPE), w_uq.dtype)
    flat = lambda t: t.reshape(rq, -1).astype(BF16)
    wa_pad = jnp.concatenate([w_a2, jnp.zeros((LANE - rank, gqk), w_a2.dtype)], axis=0)
    return dict(
        col=col, dk=gqk // GLA_HEADS, dv=g_gla_out.shape[0],
        g_mix=g_mix, w_in_t=jnp.swapaxes(w_in, 0, 1), in_rows=in_rows,
        wa_pad=wa_pad.astype(BF16), b_a=b_a, g_gla_out=g_gla_out, g_q=g_q,
        wq_nope=flat(w3[:, :, :MLA_NOPE]),
        wq_pe=flat(jnp.concatenate([pe, zpad], axis=2)),
        wq_pe_sw=flat(jnp.concatenate([pe_sw, zpad], axis=2)),
        g_kv=g_kv, w_uk=w_uk.astype(BF16), w_uv=w_uv.astype(BF16),
        w_uv_t=w_uv.T.astype(BF16),
        w_uk_t3=w_uk.reshape(rk, MLA_HEADS, MLA_NOPE).transpose(1, 2, 0).astype(BF16),
        w_uv3=w_uv.reshape(rk, MLA_HEADS, MLA_V).transpose(1, 0, 2).astype(BF16),
        w_o=w_o.astype(BF16),
        g_ffn=g_ffn, w_up=w_up, conv_w=conv_w, conv_b=conv_b,
        w_down=w_down, final_norm=final_norm)


def kernel(x_prompt, x_sample, cache_mla_latent, cache_mla_krope, state_gla, cache_ffn_conv,
           meta_tokens, g_mix, w_in, w_a2, b_a, g_gla_out, g_q, w_uq, g_kv, w_uk, w_uv, w_o,
           g_ffn, w_up, conv_w, conv_b, w_down, final_norm):
    assert w_in.shape[0] == 1, "single trunk layer"
    bp, seq, d = x_prompt.shape
    assert bp == 1
    bs, ts, _ = x_sample.shape
    P = cache_mla_latent.shape[2]
    w = _prep_weights(g_mix[0], w_in[0], w_a2[0], b_a[0], g_gla_out[0], g_q[0], w_uq[0],
                      g_kv[0], w_uk[0], w_uv[0], w_o[0], g_ffn[0], w_up[0], conv_w[0],
                      conv_b[0], w_down[0], final_norm)
    dk, dv, dff2 = w["dk"], w["dv"], conv_w.shape[2]

    n_meta = meta_tokens.shape[0]
    assert n_meta == N_META == ts and seq % CHUNK == 0
    x_short = jnp.concatenate([x_sample.reshape(bs * ts, d), meta_tokens.astype(F32)], axis=0)
    pos_short = jnp.concatenate([jnp.tile(P + jnp.arange(ts, dtype=jnp.int32), bs),
                                 jnp.arange(n_meta, dtype=jnp.int32)])
    pos_long = n_meta + jnp.arange(seq, dtype=jnp.int32)
    pr_long, pr_short = _project([(x_prompt[0], pos_long), (x_short, pos_short)], w)
    (ys, lat_s, kr_s, st_s, cv_s), (lat_m, kr_m, st_m, cv_m, prefix) = _short_streams(
        x_short, pr_short, w, B=bs, T=ts, past_lat=cache_mla_latent[0],
        past_kr=cache_mla_krope[0], s0_s=state_gla[0], hist_s=cache_ffn_conv[0])
    yp, lat_p, kr_p, st_p, cv_p = _long_stream(
        x_prompt[0], pr_long, w, T=seq, s0=st_m, hist=cv_m, prefix=prefix)

    rk = lat_p.shape[1]
    T = n_meta + seq
    return (yp,
            ys,
            jnp.concatenate([lat_m, lat_p], axis=0).reshape(1, 1, T, rk),
            jnp.concatenate([kr_m, kr_p], axis=0)[:, :MLA_ROPE].reshape(1, 1, T, MLA_ROPE),
            st_p[None],
            cv_p[None],
            lat_s.reshape(1, bs, ts, rk),
            kr_s[:, :MLA_ROPE].reshape(1, bs, ts, MLA_ROPE),
            st_s[None],
            cv_s[None])
```

```python
import functools

import jax
import jax.numpy as jnp
from jax import lax
from jax.experimental import pallas as pl
from jax.experimental.pallas import tpu as pltpu

BF16 = jnp.bfloat16
F32 = jnp.float32

CHUNK = 64
CHUNK_SHIFT = 6
N_META = 16
EPS = 1e-6
GLA_HEADS = 4
GLA_GATE_NORM = 16.0
GLA_LOG_ALPHA_MIN = -5.0
MLA_HEADS = 16
MLA_NOPE = 128
MLA_ROPE = 64
MLA_V = 128
ROPE_THETA = 10000.0
CONV_W = 3
NEG_BIG = -1e30
LOG2E = 1.4426950408889634
QK_SCALE_LOG2E = (MLA_NOPE + MLA_ROPE) ** -0.5 * LOG2E

LANE = 128
VT_ONES = 16
GLA_CHUNK = 256
GLA_SEQS = 4
SAFE_EXP = 64.0
ROW_TILE = 1024
VMEM_LIMIT = 56 * 1024 * 1024


def _cparams(sem, vmem=VMEM_LIMIT):
    return pltpu.CompilerParams(dimension_semantics=sem, vmem_limit_bytes=vmem)


def _rmsnorm(x, g):
    return x * lax.rsqrt(jnp.mean(x * x, axis=-1, keepdims=True) + EPS) * g


def _sigmoid(x):
    return 0.5 * jnp.tanh(0.5 * x) + 0.5


def _pick(n, cands):
    for c in cands:
        if n % c == 0:
            return c
    fits = [t for t in range(16, min(n, max(cands)) + 1, 16) if n % t == 0]
    if not fits:
        raise ValueError(f"no tile in {cands} divides {n}")
    return fits[-1]


_NT = (((1,), (1,)), ((), ()))


def _matmul_wt_kernel(a_ref, w_ref, o_ref, w_scr):
    @pl.when(pl.program_id(1) == 0)
    def _():
        w_scr[...] = w_ref[...].astype(BF16)

    o_ref[...] = lax.dot_general(a_ref[...], w_scr[...], _NT,
                                 preferred_element_type=F32).astype(o_ref.dtype)


def _matmul_wt(a, w_t, row0, n, out_dtype, tn):
    m, k = a.shape
    tm = _pick(m, (ROW_TILE, 512, 384, 128))
    return pl.pallas_call(
        _matmul_wt_kernel,
        grid=(n // tn, m // tm),
        in_specs=[pl.BlockSpec((tm, k), lambda j, i: (i, 0)),
                  pl.BlockSpec((pl.Element(tn), pl.Element(k)),
                               lambda j, i: (pl.multiple_of(row0 + j * tn, 16), 0))],
        out_specs=pl.BlockSpec((tm, tn), lambda j, i: (i, j)),
        out_shape=jax.ShapeDtypeStruct((m, n), out_dtype),
        scratch_shapes=[pltpu.VMEM((tn, k), BF16)],
        compiler_params=_cparams(("parallel", "arbitrary")),
        name="in_proj_wt",
    )(a, w_t)


def _front_kernel(x_ref, g_ref, w_ref, gkv_ref, cos_ref, sin_ref,
                  h_ref, o_ref, lat_ref, kr_ref, w_scr, *, rank, rq, rk):
    @pl.when(pl.program_id(0) == 0)
    def _():
        w = w_ref[...].astype(BF16)
        half = MLA_ROPE // 2
        pe0 = rank + rq + rk
        o_pe = rq + rk
        w_scr[0:rq] = w[rank:rank + rq]
        w_scr[rq:o_pe] = w[rank + rq:pe0]
        w_scr[o_pe:o_pe + MLA_ROPE] = w[pe0:pe0 + MLA_ROPE]
        w_scr[o_pe + MLA_ROPE:o_pe + MLA_ROPE + half] = w[pe0 + half:pe0 + MLA_ROPE]
        w_scr[o_pe + MLA_ROPE + half:o_pe + 2 * MLA_ROPE] = w[pe0:pe0 + half]
        o_a = o_pe + 2 * MLA_ROPE
        w_scr[o_a:o_a + rank] = w[0:rank]
        w_scr[o_a + rank:] = jnp.zeros((w_scr.shape[0] - o_a - rank, w_scr.shape[1]), BF16)

    h = _rmsnorm(x_ref[...], g_ref[...]).astype(BF16)
    h_ref[...] = h
    small = lax.dot_general(h, w_scr[...], _NT, preferred_element_type=F32)
    o_ref[...] = small
    lat_ref[...] = _rmsnorm(small[:, rq:rq + rk], gkv_ref[...])
    blk = small[:, rq + rk:rq + rk + LANE]
    kr_ref[...] = blk * cos_ref[...] + pltpu.roll(blk, LANE // 2, 1) * sin_ref[...]


def _front(x, g_mix, w_t, row0, g_kv, cos_t, sin_t, *, rank, rq, rk):
    m, k = x.shape
    n_in = rank + rq + rk + MLA_ROPE
    n_out = rq + rk + 2 * MLA_ROPE + LANE
    tm = _pick(m, (512, 384, 128))
    kern = functools.partial(_front_kernel, rank=rank, rq=rq, rk=rk)
    row = lambda i: (i, 0)
    return pl.pallas_call(
        kern,
        grid=(m // tm,),
        in_specs=[pl.BlockSpec((tm, k), row),
                  pl.BlockSpec((1, k), lambda i: (0, 0)),
                  pl.BlockSpec((pl.Element(n_in), pl.Element(k)), lambda i: (row0, 0),
                               pipeline_mode=pl.Buffered(1)),
                  pl.BlockSpec((1, rk), lambda i: (0, 0)),
                  pl.BlockSpec((tm, LANE), row),
                  pl.BlockSpec((tm, LANE), row)],
        out_specs=[pl.BlockSpec((tm, k), row),
                   pl.BlockSpec((tm, n_out), row),
                   pl.BlockSpec((tm, rk), row),
                   pl.BlockSpec((tm, LANE), row)],
        out_shape=[jax.ShapeDtypeStruct((m, k), BF16),
                   jax.ShapeDtypeStruct((m, n_out), F32),
                   jax.ShapeDtypeStruct((m, rk), F32),
                   jax.ShapeDtypeStruct((m, LANE), F32)],
        scratch_shapes=[pltpu.VMEM((n_out, k), BF16)],
        compiler_params=_cparams(("arbitrary",)),
        name="front_proj",
    )(x, g_mix.reshape(1, -1), w_t, g_kv.reshape(1, -1), cos_t, sin_t)


def _split3(x):
    a = x.astype(BF16)
    r1 = x - a.astype(F32)
    b = r1.astype(BF16)
    c = (r1 - b.astype(F32)).astype(BF16)
    return a, b, c


def _gla_kernel(q_ref, k_ref, v_ref, r_ref, ga_ref, a_ref, wa_ref, ba_ref, go_ref, s0_ref,
                o_ref, sout_ref, s_scr, *, C, SB, T, H, dk, dv, S):
    c_idx = pl.program_id(1)
    n_chunks = pl.num_programs(1)
    R = S * C

    @pl.when(c_idx == 0)
    def _():
        s_scr[...] = s0_ref[...]

    z = jnp.dot(a_ref[...].astype(BF16), wa_ref[...], preferred_element_type=F32) + ba_ref[...]
    log_sig = jnp.minimum(z, 0.0) - jnp.log(1.0 + jnp.exp(-jnp.abs(z)))
    la = jnp.maximum(log_sig * (1.0 / GLA_GATE_NORM), GLA_LOG_ALPHA_MIN)
    if T % C:
        rows = c_idx * C + lax.broadcasted_iota(jnp.int32, (C, 1), 0)
        la = jnp.where(rows < T, la, 0.0)

    ri = lax.broadcasted_iota(jnp.int32, (R, R), 0)
    ci = lax.broadcasted_iota(jnp.int32, (R, R), 1)
    same_seq = (ri >= ci) if S == 1 else ((ri >= ci) & (ri - ci <= lax.rem(ri, C)))
    tri = jnp.where(same_seq, 1.0, 0.0).astype(BF16)
    ones = jnp.ones((C, LANE), BF16)
    cs_all = jnp.zeros_like(la)
    dsum_all = [jnp.zeros((la.shape[1], LANE), F32) for _ in range(S)]
    for piece in _split3(la):
        cs_all = cs_all + jnp.dot(tri, piece, preferred_element_type=F32)
        for si in range(S):
            dsum_all[si] = dsum_all[si] + lax.dot_general(
                piece[si * C:(si + 1) * C], ones, (((0,), (0,)), ((), ())),
                preferred_element_type=F32)

    sr = lax.broadcasted_iota(jnp.int32, (SB, SB), 0)
    sc = lax.broadcasted_iota(jnp.int32, (SB, SB), 1)
    causal = sr >= sc
    nt = (((1,), (1,)), ((), ()))
    scale = dk ** -0.5

    for si, h in [(si, h) for si in range(S) for h in range(H)]:
        rs = slice(si * C, (si + 1) * C)
        ksl = slice(h * dk, (h + 1) * dk)
        vsl = slice(h * dv, (h + 1) * dv)
        cs = cs_all[rs, ksl]
        c_last = cs[C - 1:C, :]
        q = q_ref[rs, ksl].astype(F32) * scale
        k = k_ref[rs, ksl].astype(F32)
        v = v_ref[rs, vsl]
        s_old = s_scr[si, h]

        o_inter = jnp.dot((q * jnp.exp(cs)).astype(BF16), s_old.astype(BF16),
                          preferred_element_type=F32)
        k_end = (k * jnp.exp(c_last - cs)).astype(BF16)
        upd = lax.dot_general(k_end, v, (((0,), (0,)), ((), ())), preferred_element_type=F32)
        dcol = jnp.exp(dsum_all[si][ksl, :])
        s_scr[si, h] = jnp.concatenate([dcol] * (dv // LANE), axis=1) * s_old + upd

        outs = []
        for i in range(C // SB):
            lo = i * SB
            cs_i = cs[lo:lo + SB]
            q_i = q[lo:lo + SB]
            k_i = k[lo:lo + SB]
            start = cs[lo - 1:lo] if i > 0 else jnp.zeros_like(c_last)
            mid = 0.5 * (start + cs[lo + SB - 1:lo + SB])
            qd = (q_i * jnp.exp(cs_i - mid)).astype(BF16)
            kd = (k_i * jnp.exp(mid - cs_i)).astype(BF16)
            att = lax.dot_general(qd, kd, nt, preferred_element_type=F32)
            att = jnp.where(causal, att, 0.0)
            o_i = jnp.dot(att.astype(BF16), v[lo:lo + SB], preferred_element_type=F32)
            if i > 0:
                qo = (q_i * jnp.exp(cs_i - start)).astype(BF16)
                ko = (k[:lo] * jnp.exp(start - cs[:lo])).astype(BF16)
                att_o = lax.dot_general(qo, ko, nt, preferred_element_type=F32)
                o_i = o_i + jnp.dot(att_o.astype(BF16), v[:lo], preferred_element_type=F32)
            outs.append(o_i)
        o = o_inter + (jnp.concatenate(outs, axis=0) if len(outs) > 1 else outs[0])

        on = _rmsnorm(o, go_ref[...])
        r = r_ref[rs, vsl].astype(F32)
        g = ga_ref[rs, vsl].astype(F32)
        o_ref[rs, vsl] = (_sigmoid(g) * (on * (r * _sigmoid(r)))).astype(o_ref.dtype)

    @pl.when(c_idx == n_chunks - 1)
    def _():
        sout_ref[...] = s_scr[...]


def _gla(qkvr, gates, small, wa_pad, b_a, g_out, s0, *, B, T, Tp, dk, dv, col, row0=0):
    C = min(GLA_CHUNK, Tp)
    SB = min(32, C)
    nc = Tp // C
    H = GLA_HEADS
    qk, vw = H * dk, H * dv
    S = _pick(B, (GLA_SEQS, 1)) if nc == 1 else 1
    R = S * C
    assert row0 % R == 0
    rb = lambda b, c: row0 // R + b * nc + c
    kern = functools.partial(_gla_kernel, C=C, SB=SB, T=T, H=H, dk=dk, dv=dv, S=S)
    return pl.pallas_call(
        kern,
        grid=(B // S, nc),
        in_specs=[
            pl.BlockSpec((R, qk), lambda b, c: (rb(b, c), col["q"] // qk)),
            pl.BlockSpec((R, qk), lambda b, c: (rb(b, c), col["k"] // qk)),
            pl.BlockSpec((R, vw), lambda b, c: (rb(b, c), col["v"] // vw)),
            pl.BlockSpec((R, vw), lambda b, c: (rb(b, c), col["r"] // vw)),
            pl.BlockSpec((R, vw), lambda b, c: (rb(b, c), col["ga"] // vw)),
            pl.BlockSpec((R, LANE), lambda b, c: (rb(b, c), col["a"] // LANE)),
            pl.BlockSpec((LANE, qk), lambda b, c: (0, 0)),
            pl.BlockSpec((1, qk), lambda b, c: (0, 0)),
            pl.BlockSpec((1, dv), lambda b, c: (0, 0)),
            pl.BlockSpec((S, H, dk, dv), lambda b, c: (b, 0, 0, 0)),
        ],
        out_specs=[
            pl.BlockSpec((R, vw), lambda b, c: (b * nc + c, 0)),
            pl.BlockSpec((S, H, dk, dv), lambda b, c: (b, 0, 0, 0)),
        ],
        out_shape=[jax.ShapeDtypeStruct((B * Tp, vw), BF16),
                   jax.ShapeDtypeStruct((B, H, dk, dv), F32)],
        scratch_shapes=[pltpu.VMEM((S, H, dk, dv), F32)],
        compiler_params=_cparams(("parallel", "arbitrary")),
        name="gla",
    )(qkvr, qkvr, qkvr, qkvr, gates, small, wa_pad, b_a.reshape(1, -1), g_out.reshape(1, -1), s0)


def _qprep_kernel(cq_ref, gq_ref, wn_ref, wp_ref, wps_ref, cos_ref, sin_ref, q_ref):
    hq = _rmsnorm(cq_ref[...], gq_ref[...]).astype(BF16)
    qn = jnp.dot(hq, wn_ref[...], preferred_element_type=F32)
    qp = jnp.dot(hq, wp_ref[...], preferred_element_type=F32)
    qs = jnp.dot(hq, wps_ref[...], preferred_element_type=F32)
    cos = cos_ref[...] * QK_SCALE_LOG2E
    sin = sin_ref[...] * QK_SCALE_LOG2E
    for h in range(MLA_HEADS):
        sl = slice(h * LANE, (h + 1) * LANE)
        q_ref[h, :, 0:LANE] = (qn[:, sl] * QK_SCALE_LOG2E).astype(BF16)
        q_ref[h, :, LANE:2 * LANE] = (qp[:, sl] * cos + qs[:, sl] * sin).astype(BF16)


def _qprep(small, g_q, wn, wp, wps, cos_t, sin_t, *, col):
    m = small.shape[0]
    rq = wn.shape[0]
    tm = _pick(m, (256, 128))
    full = lambda i: (0, 0)
    return pl.pallas_call(
        _qprep_kernel,
        grid=(m // tm,),
        in_specs=[pl.BlockSpec((tm, rq), lambda i: (i, col["cq"] // rq)),
                  pl.BlockSpec((1, rq), full),
                  pl.BlockSpec(wn.shape, full),
                  pl.BlockSpec(wp.shape, full),
                  pl.BlockSpec(wps.shape, full),
                  pl.BlockSpec((tm, LANE), lambda i: (i, 0)),
                  pl.BlockSpec((tm, LANE), lambda i: (i, 0))],
        out_specs=pl.BlockSpec((MLA_HEADS, tm, 2 * LANE), lambda i: (0, i, 0)),
        out_shape=jax.ShapeDtypeStruct((MLA_HEADS, m, 2 * LANE), BF16),
        compiler_params=_cparams(("parallel",)),
        name="mla_q",
    )(small, g_q.reshape(1, -1), wn, wp, wps, cos_t, sin_t)


def _kvup_kernel(lat_ref, kr_ref, wuk_ref, wuv_ref, k_ref, v_ref, *, v_transposed):
    lat = lat_ref[...].astype(BF16)
    kn = jnp.dot(lat, wuk_ref[...], preferred_element_type=F32)
    kr = kr_ref[...]
    lane = lax.broadcasted_iota(jnp.int32, kr.shape, 1)
    kp = jnp.where(lane == MLA_ROPE, 1.0, kr).astype(BF16)
    if v_transposed:
        vv = lax.dot_general(wuv_ref[...], lat, (((1,), (1,)), ((), ())),
                             preferred_element_type=F32)
    else:
        vv = jnp.dot(lat, wuv_ref[...], preferred_element_type=F32)
    for h in range(MLA_HEADS):
        sl = slice(h * LANE, (h + 1) * LANE)
        k_ref[h, :, 0:LANE] = kn[:, sl].astype(BF16)
        k_ref[h, :, LANE:2 * LANE] = kp
        if v_transposed:
            v_ref[h, 0:LANE, :] = vv[sl, :].astype(BF16)
            v_ref[h, LANE:LANE + VT_ONES, :] = jnp.ones((VT_ONES, vv.shape[1]), BF16)
        else:
            v_ref[h] = vv[:, sl].astype(BF16)


def _kvup(lat, kr, wuk, wuv, *, v_transposed=False):
    m, rk = lat.shape
    tm = _pick(m, (512, 256, 128))
    full = lambda i: (0, 0)
    if v_transposed:
        v_spec = pl.BlockSpec((MLA_HEADS, LANE + VT_ONES, tm), lambda i: (0, 0, i))
        v_shape = (MLA_HEADS, LANE + VT_ONES, m)
    else:
        v_spec = pl.BlockSpec((MLA_HEADS, tm, LANE), lambda i: (0, i, 0))
        v_shape = (MLA_HEADS, m, LANE)
    return pl.pallas_call(
        functools.partial(_kvup_kernel, v_transposed=v_transposed),
        grid=(m // tm,),
        in_specs=[pl.BlockSpec((tm, rk), lambda i: (i, 0)),
                  pl.BlockSpec((tm, LANE), lambda i: (i, 0)),
                  pl.BlockSpec(wuk.shape, full),
                  pl.BlockSpec(wuv.shape, full)],
        out_specs=[pl.BlockSpec((MLA_HEADS, tm, 2 * LANE), lambda i: (0, i, 0)), v_spec],
        out_shape=[jax.ShapeDtypeStruct((MLA_HEADS, m, 2 * LANE), BF16),
                   jax.ShapeDtypeStruct(v_shape, BF16)],
        compiler_params=_cparams(("parallel",)),
        name="mla_kv",
    )(lat, kr, wuk, wuv)


def _last_kblock(qi, *, tq, tk, nk, q_off, k_off):
    top_chunk = ((qi + 1) * tq - 1 + q_off) // CHUNK
    last_key = (top_chunk + 1) * CHUNK - 1 - k_off
    return jnp.minimum(last_key // tk, nk - 1)


def _attn_kernel(q_ref, k_ref, v_ref, o_ref, m_scr, l_scr, acc_scr, *, hps, tq, tk, nk,
                 q_off, k_off):
    qi = pl.program_id(2)
    ki = pl.program_id(3)

    @pl.when(ki == 0)
    def _():
        m_scr[...] = jnp.full(m_scr.shape, NEG_BIG, F32)
        l_scr[...] = jnp.zeros(l_scr.shape, F32)
        acc_scr[...] = jnp.zeros(acc_scr.shape, F32)

    @pl.when(ki <= _last_kblock(qi, tq=tq, tk=tk, nk=nk, q_off=q_off, k_off=k_off))
    def _():
        q_chunk = (qi * tq + q_off + lax.broadcasted_iota(jnp.int32, (tq, 1), 0)) >> CHUNK_SHIFT
        k_chunk = (ki * tk + k_off + lax.broadcasted_iota(jnp.int32, (1, tk), 1)) >> CHUNK_SHIFT
        visible = q_chunk >= k_chunk

        def head(h, carry):
            s = lax.dot_general(q_ref[h], k_ref[h], (((1,), (1,)), ((), ())),
                                preferred_element_type=F32)
            s = jnp.where(visible, s, NEG_BIG)
            m_prev = m_scr[h]
            m_new = jnp.maximum(m_prev, jnp.max(s, axis=-1, keepdims=True))
            p = jnp.exp2(s - m_new)
            alpha = jnp.exp2(m_prev - m_new)
            l_scr[h] = alpha * l_scr[h] + jnp.sum(p, axis=-1, keepdims=True)
            acc_scr[h] = alpha * acc_scr[h] + jnp.dot(p.astype(BF16), v_ref[h],
                                                      preferred_element_type=F32)
            m_scr[h] = m_new
            return carry

        lax.fori_loop(0, hps, head, 0)

    @pl.when(ki == nk - 1)
    def _():
        for h in range(hps):
            o_ref[:, h * LANE:(h + 1) * LANE] = (acc_scr[h] / l_scr[h]).astype(o_ref.dtype)


def _attention(q, k, v, *, B, Tq, Tk, tq, tk, hps, q_off, k_off):
    nq = Tq // tq
    nk = Tk // tk
    hg = MLA_HEADS // hps
    dqk = q.shape[2]
    dvh = v.shape[2]
    last = functools.partial(_last_kblock, tq=tq, tk=tk, nk=nk, q_off=q_off, k_off=k_off)
    kern = functools.partial(_attn_kernel, hps=hps, tq=tq, tk=tk, nk=nk, q_off=q_off,
                             k_off=k_off)
    kv_row = lambda b, g, i, j: b * nk + jnp.minimum(j, last(i))
    return pl.pallas_call(
        kern,
        grid=(B, hg, nq, nk),
        in_specs=[pl.BlockSpec((hps, tq, dqk), lambda b, g, i, j: (g, b * nq + i, 0)),
                  pl.BlockSpec((hps, tk, dqk), lambda b, g, i, j: (g, kv_row(b, g, i, j), 0)),
                  pl.BlockSpec((hps, tk, dvh), lambda b, g, i, j: (g, kv_row(b, g, i, j), 0))],
        out_specs=pl.BlockSpec((tq, hps * dvh), lambda b, g, i, j: (b * nq + i, g)),
        out_shape=jax.ShapeDtypeStruct((B * Tq, MLA_HEADS * dvh), BF16),
        scratch_shapes=[pltpu.VMEM((hps, tq, 1), F32),
                        pltpu.VMEM((hps, tq, 1), F32),
                        pltpu.VMEM((hps, tq, dvh), F32)],
        compiler_params=_cparams(("parallel", "parallel", "parallel", "arbitrary")),
        name="mla_attn",
    )(q, k, v)


def _attn_t_kernel(qi_ref, ki_ref, q_ref, k_ref, vt_ref, kp_ref, vtp_ref, o_ref,
                   q_scr, r_scr, acc_scr, *, hps, t):
    pair = pl.program_id(1)
    qi = qi_ref[pair]
    ki = ki_ref[pair]
    nt = (((1,), (1,)), ((), ()))
    pe = slice(LANE, 2 * LANE)
    lane = lax.broadcasted_iota(jnp.int32, (t, LANE), 1)

    def set_reference(h, r):
        neg_r = jnp.transpose(jnp.broadcast_to(-r, (LANE, t)))
        q_scr[h, :, pe] = jnp.where(lane == MLA_ROPE, neg_r.astype(BF16), q_ref[h, :, pe])
        r_scr[h] = r

    def shifted_scores(h):
        return lax.dot_general(k_ref[h], q_scr[h], nt, preferred_element_type=F32)

    @pl.when(ki == 0)
    def _():
        for h in range(hps):
            q_scr[h, :, 0:LANE] = q_ref[h, :, 0:LANE]
            s = lax.dot_general(kp_ref[h], q_ref[h], nt, preferred_element_type=F32)
            r = jnp.max(s, axis=0, keepdims=True).astype(BF16).astype(F32)
            p = jnp.exp2((s - r).astype(BF16))
            acc_scr[h] = jnp.dot(vtp_ref[h], p, preferred_element_type=F32)
            set_reference(h, r)

    def general(h, bias, keep_reference):
        sp = shifted_scores(h)
        if bias is not None:
            sp = sp + bias
        r = r_scr[h]
        rise = jnp.maximum(jnp.max(sp, axis=0, keepdims=True), 0.0)
        r_new = (r + rise).astype(BF16).astype(F32)
        delta = r_new - r
        p = jnp.exp2((sp - delta).astype(BF16))
        acc_scr[h] = jnp.exp2(-delta) * acc_scr[h] + jnp.dot(vt_ref[h], p,
                                                               preferred_element_type=F32)
        if keep_reference:
            set_reference(h, r_new)

    @pl.when(ki < qi)
    def _():
        unsafe = []
        sp_next = shifted_scores(0)
        for h in range(hps):
            sp = sp_next
            if h + 1 < hps:
                sp_next = shifted_scores(h + 1)
            safe = jnp.max(sp) <= SAFE_EXP
            part = jnp.dot(vt_ref[h], jnp.exp2(sp.astype(BF16)), preferred_element_type=F32)
            acc_scr[h] += jnp.where(safe, part, 0.0)
            unsafe.append(jnp.logical_not(safe))

        @pl.when(functools.reduce(jnp.logical_or, unsafe))
        def _():
            for h in range(hps):
                @pl.when(unsafe[h])
                def _():
                    general(h, None, True)

    @pl.when(ki == qi)
    def _():
        k_chunk = lax.broadcasted_iota(jnp.int32, (t, 1), 0) >> CHUNK_SHIFT
        q_chunk = lax.broadcasted_iota(jnp.int32, (1, t), 1) >> CHUNK_SHIFT
        bias = jnp.where(q_chunk >= k_chunk, 0.0, NEG_BIG)
        for h in range(hps):
            general(h, bias, False)
            acc = acc_scr[h]
            o_t = acc[0:LANE] / acc[LANE:LANE + 1]
            o_ref[:, h * LANE:(h + 1) * LANE] = o_t.T.astype(o_ref.dtype)


def _attention_t(q, k, vt, k_pre, vt_pre, *, T, t, hps):
    n = T // t
    hg = MLA_HEADS // hps
    dqk = q.shape[2]
    npre = k_pre.shape[1]
    vrows = vt.shape[1]
    pairs = [(i, j) for i in range(n) for j in range(i + 1)]
    qi_arr = jnp.asarray([p[0] for p in pairs], jnp.int32)
    ki_arr = jnp.asarray([p[1] for p in pairs], jnp.int32)
    kern = functools.partial(_attn_t_kernel, hps=hps, t=t)
    grid_spec = pltpu.PrefetchScalarGridSpec(
        num_scalar_prefetch=2,
        grid=(hg, len(pairs)),
        in_specs=[pl.BlockSpec((hps, t, dqk), lambda g, p, qi, ki: (g, qi[p], 0)),
                  pl.BlockSpec((hps, t, dqk), lambda g, p, qi, ki: (g, ki[p], 0)),
                  pl.BlockSpec((hps, vrows, t), lambda g, p, qi, ki: (g, 0, ki[p])),
                  pl.BlockSpec((hps, npre, dqk), lambda g, p, qi, ki: (g, 0, 0)),
                  pl.BlockSpec((hps, vrows, npre), lambda g, p, qi, ki: (g, 0, 0))],
        out_specs=pl.BlockSpec((t, hps * LANE), lambda g, p, qi, ki: (qi[p], g)),
        scratch_shapes=[pltpu.VMEM((hps, t, dqk), BF16),
                        pltpu.VMEM((hps, 1, t), F32),
                        pltpu.VMEM((hps, vrows, t), F32)])
    return pl.pallas_call(
        kern,
        grid_spec=grid_spec,
        out_shape=jax.ShapeDtypeStruct((T, MLA_HEADS * LANE), BF16),
        compiler_params=_cparams(("parallel", "arbitrary")),
        name="mla_attn_t",
    )(qi_arr, ki_arr, q, k, vt, k_pre, vt_pre)


def _absorb_q_kernel(q_ref, w_ref, o_ref):
    o_ref[0] = jnp.dot(q_ref[0, :, 0:MLA_NOPE], w_ref[0],
                       preferred_element_type=F32).astype(o_ref.dtype)


def _absorb_q(q, w_uk_t3):
    heads, rows, dqk = q.shape
    rk = w_uk_t3.shape[2]
    return pl.pallas_call(
        _absorb_q_kernel,
        grid=(heads,),
        in_specs=[pl.BlockSpec((1, rows, dqk), lambda h: (h, 0, 0)),
                  pl.BlockSpec((1, MLA_NOPE, rk), lambda h: (h, 0, 0))],
        out_specs=pl.BlockSpec((1, rows, rk), lambda h: (h, 0, 0)),
        out_shape=jax.ShapeDtypeStruct((heads, rows, rk), BF16),
        compiler_params=_cparams(("parallel",)),
        name="mla_absorb_q",
    )(q, w_uk_t3)


def _attn_latent_kernel(ql_ref, q_ref, plat_ref, pkr_ref, lat_ref, kr_ref, o_ref, *, T, P):
    heads, _, rk = ql_ref.shape
    rows = heads * T
    nt = (((1,), (1,)), ((), ()))
    ql = ql_ref[...].reshape(rows, rk)
    qpe = q_ref[:, :, LANE:2 * LANE].reshape(rows, LANE)[:, 0:MLA_ROPE]
    lat_all = jnp.concatenate([plat_ref[0].astype(BF16), lat_ref[...].astype(BF16)], axis=0)
    kr_all = jnp.concatenate([pkr_ref[0], kr_ref[:, 0:MLA_ROPE]], axis=0).astype(BF16)
    s = (lax.dot_general(ql, lat_all, nt, preferred_element_type=F32)
         + lax.dot_general(qpe, kr_all, nt, preferred_element_type=F32))
    tok = lax.rem(lax.broadcasted_iota(jnp.int32, (rows, 1), 0), T)
    q_chunk = (P + tok) >> CHUNK_SHIFT
    k_chunk = lax.broadcasted_iota(jnp.int32, (1, P + T), 1) >> CHUNK_SHIFT
    s = jnp.where(q_chunk >= k_chunk, s, NEG_BIG)
    p = jnp.exp2(s - jnp.max(s, axis=-1, keepdims=True))
    o = jnp.dot(p.astype(BF16), lat_all, preferred_element_type=F32)
    o = o / jnp.sum(p, axis=-1, keepdims=True)
    o_ref[...] = o.reshape(heads, T, rk).astype(o_ref.dtype)


def _attn_latent(qlat, q, past_lat, past_kr, lat, kr, *, B, T):
    heads, _, rk = qlat.shape
    P = past_lat.shape[1]
    kern = functools.partial(_attn_latent_kernel, T=T, P=P)
    return pl.pallas_call(
        kern,
        grid=(B,),
        in_specs=[pl.BlockSpec((heads, T, rk), lambda b: (0, b, 0)),
                  pl.BlockSpec((heads, T, q.shape[2]), lambda b: (0, b, 0)),
                  pl.BlockSpec((1, P, rk), lambda b: (b, 0, 0)),
                  pl.BlockSpec((1, P, past_kr.shape[2]), lambda b: (b, 0, 0)),
                  pl.BlockSpec((T, rk), lambda b: (b, 0)),
                  pl.BlockSpec((T, LANE), lambda b: (b, 0))],
        out_specs=pl.BlockSpec((heads, T, rk), lambda b: (0, b, 0)),
        out_shape=jax.ShapeDtypeStruct((heads, B * T, rk), BF16),
        compiler_params=_cparams(("parallel",)),
        name="mla_attn_latent",
    )(qlat, q, past_lat, past_kr, lat, kr)


def _absorb_out_kernel(o_ref, w_ref, out_ref):
    out_ref[...] = jnp.dot(o_ref[0], w_ref[0], preferred_element_type=F32).astype(out_ref.dtype)


def _absorb_out(olat, w_uv3):
    heads, rows, rk = olat.shape
    dvh = w_uv3.shape[2]
    return pl.pallas_call(
        _absorb_out_kernel,
        grid=(heads,),
        in_specs=[pl.BlockSpec((1, rows, rk), lambda h: (h, 0, 0)),
                  pl.BlockSpec((1, rk, dvh), lambda h: (h, 0, 0))],
        out_specs=pl.BlockSpec((rows, dvh), lambda h: (0, h)),
        out_shape=jax.ShapeDtypeStruct((rows, heads * dvh), BF16),
        compiler_params=_cparams(("parallel",)),
        name="mla_absorb_out",
    )(olat, w_uv3)


def _merge_kernel(a_ref, gb_ref, om_ref, x_ref, wo_ref, gf_ref, x1_ref, h2_ref):
    merged = a_ref[...].astype(F32) + _sigmoid(gb_ref[...].astype(F32)) * om_ref[...].astype(F32)
    x1 = x_ref[...] + jnp.dot(merged.astype(BF16), wo_ref[...], preferred_element_type=F32)
    x1_ref[...] = x1
    h2_ref[...] = _rmsnorm(x1, gf_ref[...]).astype(BF16)


def _merge(branch_a, gates, o_m, x, wo, g_ffn, *, col):
    m, d = x.shape
    tm = _pick(m, (512, 384, 256, 128))
    row = lambda i: (i, 0)
    return pl.pallas_call(
        _merge_kernel,
        grid=(m // tm,),
        in_specs=[pl.BlockSpec((tm, d), row),
                  pl.BlockSpec((tm, d), lambda i: (i, col["gb"] // d)),
                  pl.BlockSpec((tm, d), row),
                  pl.BlockSpec((tm, d), row),
                  pl.BlockSpec(wo.shape, lambda i: (0, 0), pipeline_mode=pl.Buffered(1)),
                  pl.BlockSpec((1, d), lambda i: (0, 0))],
        out_specs=[pl.BlockSpec((tm, d), row), pl.BlockSpec((tm, d), row)],
        out_shape=[jax.ShapeDtypeStruct((m, d), F32), jax.ShapeDtypeStruct((m, d), BF16)],
        compiler_params=_cparams(("parallel",)),
        name="merge_out_proj",
    )(branch_a, gates, o_m, x, wo, g_ffn.reshape(1, -1))


HALO = 8


def _ffn_up_kernel(*refs, bb, r, tf, loc, carried, cast_down):
    (h_ref, wa_ref, wb_ref, cwa_ref, cwb_ref, cba_ref, cbb_ref, ha_ref, hb_ref), refs = \
        refs[:9], refs[9:]
    if cast_down:
        wd_ref, act_ref, ca_ref, cb_ref, wdb_ref, ext_scr, carry_scr, w_scr = refs
    else:
        act_ref, ca_ref, cb_ref, ext_scr, carry_scr, w_scr = refs
    s = pl.program_id(1)
    rt = pl.program_id(2)
    d = h_ref.shape[2]

    @pl.when((s == 0) & (rt == 0))
    def _():
        w_scr[0] = wa_ref[...].astype(BF16)
        w_scr[1] = wb_ref[...].astype(BF16)
        if cast_down:
            wdb_ref[...] = wd_ref[...].astype(BF16)

    if carried:
        @pl.when(rt == 0)
        def _():
            carry_scr[0] = ha_ref[...]
            carry_scr[1] = hb_ref[...]

    h = h_ref[...].reshape(bb * r, d)
    conv = []
    for half, (cw_ref, cbias_ref, hist_ref, cout_ref) in enumerate(
            ((cwa_ref, cba_ref, ha_ref, ca_ref), (cwb_ref, cbb_ref, hb_ref, cb_ref))):
        u = jnp.dot(h, w_scr[half], preferred_element_type=F32).reshape(bb, r, tf)
        ext_scr[half, :, HALO:HALO + r, :] = u
        ext_scr[half, :, HALO - 2:HALO, :] = carry_scr[half] if carried else hist_ref[...]
        u1 = ext_scr[half, :, HALO - 1:HALO - 1 + r, :]
        u2 = ext_scr[half, :, HALO - 2:HALO - 2 + r, :]
        cw = cw_ref[...]
        conv.append(cbias_ref[...] + cw[0:1] * u2 + cw[1:2] * u1 + cw[2:3] * u)
        if carried:
            carry_scr[half] = ext_scr[half, :, HALO + r - 2:HALO + r, :]
        cout_ref[0] = ext_scr[half, :, HALO + loc:HALO + loc + 2, :]

    act_ref[...] = ((conv[0] * _sigmoid(conv[0])) * conv[1]).astype(act_ref.dtype)


def _ffn_down_kernel(act_ref, wd_ref, x1_ref, gf_ref, y_ref):
    down = jnp.dot(act_ref[...], wd_ref[...], preferred_element_type=F32)
    y_ref[...] = _rmsnorm(x1_ref[...] + down, gf_ref[...])


def _ffn(h2, x1, w_up, w_down, conv_w, conv_b, hist, g_final, *, B, T, Tp):
    d = h2.shape[1]
    dff = w_down.shape[0]
    cast_down = w_down.dtype != BF16
    tf = _pick(dff, (512, 256, 128))
    nf = dff // tf
    if Tp <= 128:
        bb, r = B, Tp
    else:
        bb, r = 1, _pick(Tp, (ROW_TILE, 128))
    nrt = Tp // r
    carried = nrt > 1
    loc = (T - 2) - (nrt - 1) * r
    assert 0 <= loc <= r - 2, "final two valid rows must sit in the last row tile"
    kern = functools.partial(_ffn_up_kernel, bb=bb, r=r, tf=tf, loc=loc, carried=carried,
                             cast_down=cast_down)
    carry_shape = (2, bb, 2, tf) if carried else (1, 1, 2, LANE)
    in_specs = [pl.BlockSpec((bb, r, d), lambda f, s, t: (s, t, 0)),
                pl.BlockSpec((d, tf), lambda f, s, t: (0, f)),
                pl.BlockSpec((d, tf), lambda f, s, t: (0, nf + f)),
                pl.BlockSpec((CONV_W, tf), lambda f, s, t: (0, f)),
                pl.BlockSpec((CONV_W, tf), lambda f, s, t: (0, nf + f)),
                pl.BlockSpec((1, tf), lambda f, s, t: (0, f)),
                pl.BlockSpec((1, tf), lambda f, s, t: (0, nf + f)),
                pl.BlockSpec((bb, 2, tf), lambda f, s, t: (s, 0, f)),
                pl.BlockSpec((bb, 2, tf), lambda f, s, t: (s, 0, nf + f))]
    out_specs = [pl.BlockSpec((bb, r, tf), lambda f, s, t: (s, t, f)),
                 pl.BlockSpec((1, bb, 2, tf), lambda f, s, t: (t, s, 0, f)),
                 pl.BlockSpec((1, bb, 2, tf), lambda f, s, t: (t, s, 0, f))]
    out_shape = [jax.ShapeDtypeStruct((B, Tp, dff), BF16),
                 jax.ShapeDtypeStruct((nrt, B, 2, dff), F32),
                 jax.ShapeDtypeStruct((nrt, B, 2, dff), F32)]
    args = [h2.reshape(B, Tp, d), w_up, w_up, conv_w, conv_w, conv_b.reshape(1, -1),
            conv_b.reshape(1, -1), hist, hist]
    if cast_down:
        in_specs.append(pl.BlockSpec((tf, d), lambda f, s, t: (f, 0)))
        out_specs.append(pl.BlockSpec((tf, d), lambda f, s, t: (f, 0)))
        out_shape.append(jax.ShapeDtypeStruct((dff, d), BF16))
        args.append(w_down)
    outs = pl.pallas_call(
        kern,
        grid=(nf, B // bb, nrt),
        in_specs=in_specs,
        out_specs=out_specs,
        out_shape=out_shape,
        scratch_shapes=[pltpu.VMEM((2, bb, HALO + r, tf), F32),
                        pltpu.VMEM(carry_shape, F32),
                        pltpu.VMEM((2, d, tf), BF16)],
        compiler_params=_cparams(("arbitrary", "arbitrary", "arbitrary")),
        name="conv_ffn_up",
    )(*args)
    act, ca, cb = outs[:3]
    if cast_down:
        w_down = outs[3]

    m = B * Tp
    tm = _pick(m, (256, 128))
    y = pl.pallas_call(
        _ffn_down_kernel,
        grid=(m // tm,),
        in_specs=[pl.BlockSpec((tm, dff), lambda i: (i, 0)),
                  pl.BlockSpec((dff, d), lambda i: (0, 0), pipeline_mode=pl.Buffered(1)),
                  pl.BlockSpec((tm, d), lambda i: (i, 0)),
                  pl.BlockSpec((1, d), lambda i: (0, 0))],
        out_specs=pl.BlockSpec((tm, d), lambda i: (i, 0)),
        out_shape=jax.ShapeDtypeStruct((m, d), F32),
        compiler_params=_cparams(("parallel",)),
        name="ffn_down",
    )(act.reshape(m, dff), w_down, x1, g_final.reshape(1, -1))
    return y.reshape(B, Tp, d), jnp.concatenate([ca[nrt - 1], cb[nrt - 1]], axis=-1), w_down


def _rope_tables(pos):
    half = MLA_ROPE // 2
    inv = ROPE_THETA ** (-jnp.arange(0, MLA_ROPE, 2, dtype=F32) / MLA_ROPE)
    ang = pos.astype(F32)[:, None] * inv[None, :]
    cos, sin = jnp.cos(ang), jnp.sin(ang)
    zero = jnp.zeros((pos.shape[0], LANE - 2 * half), F32)
    return (jnp.concatenate([cos, cos, zero], axis=1),
            jnp.concatenate([-sin, sin, zero], axis=1))


def _project(x, pos, w):
    col = w["col"]
    rows = w["in_rows"]
    cos_t, sin_t = _rope_tables(pos)
    h, small, lat, kr = _front(x, w["g_mix"], w["w_in_t"], rows["a"], w["g_kv"], cos_t, sin_t,
                               rank=rows["cq"] - rows["a"], rq=rows["ckv"] - rows["cq"],
                               rk=rows["kpe"] - rows["ckv"])
    qkvr = _matmul_wt(h, w["w_in_t"], rows["q"], rows["a"] - rows["q"], BF16, tn=1024)
    gates = _matmul_wt(h, w["w_in_t"], rows["ga"], rows["end"] - rows["ga"], BF16, tn=1024)
    q = _qprep(small, w["g_q"], w["wq_nope"], w["wq_pe"], w["wq_pe_sw"], cos_t, sin_t, col=col)
    return dict(qkvr=qkvr, gates=gates, small=small, q=q, lat=lat, kr=kr)


def _finish(x, pr, branch_a, o_m, w, hist, *, B, T):
    x1, h2 = _merge(branch_a, pr["gates"], o_m, x, w["w_o"], w["g_ffn"], col=w["col"])
    y, conv, w["w_down"] = _ffn(h2, x1, w["w_up"], w["w_down"], w["conv_w"], w["conv_b"], hist,
                                w["final_norm"], B=B, T=T, Tp=T)
    return y, conv


def _gla_group(pr, w, s0, *, B, T, row0=0):
    return _gla(pr["qkvr"], pr["gates"], pr["small"], w["wa_pad"], w["b_a"], w["g_gla_out"],
                s0, B=B, T=T, Tp=T, dk=w["dk"], dv=w["dv"], col=w["col"], row0=row0)


def _long_stream(x, pr, w, *, T, s0, hist, prefix):
    branch_a, state = _gla_group(pr, w, s0, B=1, T=T)
    k, vt = _kvup(pr["lat"], pr["kr"], w["w_uk"], w["w_uv_t"], v_transposed=True)
    o_m = _attention_t(pr["q"], k, vt, prefix[0], prefix[1], T=T, t=_pick(T, (1024, 128)),
                       hps=MLA_HEADS // 4)
    y, conv = _finish(x, pr, branch_a, o_m, w, hist, B=1, T=T)
    return y, pr["lat"], pr["kr"], state, conv


def _short_streams(x, pr, w, *, B, T, past_lat, past_kr, s0_s, hist_s):
    ns = B * T
    dk, dv = w["dk"], w["dv"]

    ba_s, st_s = _gla_group(pr, w, s0_s, B=B, T=T)
    ba_m, st_m = _gla_group(pr, w, jnp.zeros((1, GLA_HEADS, dk, dv), F32), B=1, T=T, row0=ns)

    qlat = _absorb_q(pr["q"], w["w_uk_t3"])
    olat = _attn_latent(qlat, pr["q"], past_lat, past_kr, pr["lat"], pr["kr"], B=B, T=T)
    om_s = _absorb_out(olat, w["w_uv3"])
    q_m, lat_m, kr_m = pr["q"][:, ns:], pr["lat"][ns:], pr["kr"][ns:]
    k_m, v_m = _kvup(lat_m, kr_m, w["w_uk"], w["w_uv"])
    prefix = _kvup(lat_m, kr_m, w["w_uk"], w["w_uv_t"], v_transposed=True)
    om_m = _attention(q_m, k_m, v_m, B=1, Tq=T, Tk=T, tq=T, tk=T, hps=MLA_HEADS,
                      q_off=0, k_off=0)

    hist = jnp.concatenate([hist_s, jnp.zeros((1,) + hist_s.shape[1:], F32)], axis=0)
    y, conv = _finish(x, pr, jnp.concatenate([ba_s, ba_m], axis=0),
                      jnp.concatenate([om_s, om_m], axis=0), w, hist, B=B + 1, T=T)
    sample = (y[:B], pr["lat"][:ns], pr["kr"][:ns], st_s, conv[:B])
    meta = (lat_m, kr_m, st_m, conv[B:], prefix)
    return sample, meta


def _prep_weights(g_mix, w_in, w_a2, b_a, g_gla_out, g_q, w_uq, g_kv, w_uk, w_uv, w_o,
                  g_ffn, w_up, conv_w, conv_b, w_down, final_norm):
    d = w_in.shape[0]
    rank, gqk = w_a2.shape
    gvw = GLA_HEADS * g_gla_out.shape[0]
    rq, rk = g_q.shape[0], g_kv.shape[0]
    half = MLA_ROPE // 2
    o, offs = 0, {}
    for name, width in (("q", gqk), ("k", gqk), ("v", gvw), ("r", gvw), ("a", rank),
                        ("cq", rq), ("ckv", rk), ("kpe", MLA_ROPE), ("ga", d), ("gb", d)):
        offs[name] = (o, o + width)
        o += width
    assert o == w_in.shape[1]
    in_rows = {name: lo for name, (lo, _) in offs.items()}
    in_rows["end"] = o
    assert all(v % 16 == 0 for v in in_rows.values())
    col = {"q": 0, "k": gqk, "v": 2 * gqk, "r": 2 * gqk + gvw, "ga": 0, "gb": d,
           "cq": 0, "ckv": rq, "kpe": rq + rk, "a": rq + rk + 2 * MLA_ROPE}

    w3 = w_uq.reshape(rq, MLA_HEADS, MLA_NOPE + MLA_ROPE)
    pe = w3[:, :, MLA_NOPE:]
    pe_sw = jnp.concatenate([pe[:, :, half:], pe[:, :, :half]], axis=2)
    zpad = jnp.zeros((rq, MLA_HEADS, LANE - MLA_ROPE), w_uq.dtype)
    flat = lambda t: t.reshape(rq, -1).astype(BF16)
    wa_pad = jnp.concatenate([w_a2, jnp.zeros((LANE - rank, gqk), w_a2.dtype)], axis=0)
    return dict(
        col=col, dk=gqk // GLA_HEADS, dv=g_gla_out.shape[0],
        g_mix=g_mix, w_in_t=jnp.swapaxes(w_in, 0, 1), in_rows=in_rows,
        wa_pad=wa_pad.astype(BF16), b_a=b_a, g_gla_out=g_gla_out, g_q=g_q,
        wq_nope=flat(w3[:, :, :MLA_NOPE]),
        wq_pe=flat(jnp.concatenate([pe, zpad], axis=2)),
        wq_pe_sw=flat(jnp.concatenate([pe_sw, zpad], axis=2)),
        g_kv=g_kv, w_uk=w_uk.astype(BF16), w_uv=w_uv.astype(BF16),
        w_uv_t=w_uv.T.astype(BF16),
        w_uk_t3=w_uk.reshape(rk, MLA_HEADS, MLA_NOPE).transpose(1, 2, 0).astype(BF16),
        w_uv3=w_uv.reshape(rk, MLA_HEADS, MLA_V).transpose(1, 0, 2).astype(BF16),
        w_o=w_o.astype(BF16),
        g_ffn=g_ffn, w_up=w_up, conv_w=conv_w, conv_b=conv_b,
        w_down=w_down, final_norm=final_norm)


def kernel(x_prompt, x_sample, cache_mla_latent, cache_mla_krope, state_gla, cache_ffn_conv,
           meta_tokens, g_mix, w_in, w_a2, b_a, g_gla_out, g_q, w_uq, g_kv, w_uk, w_uv, w_o,
           g_ffn, w_up, conv_w, conv_b, w_down, final_norm):
    assert w_in.shape[0] == 1, "single trunk layer"
    bp, seq, d = x_prompt.shape
    assert bp == 1
    bs, ts, _ = x_sample.shape
    P = cache_mla_latent.shape[2]
    w = _prep_weights(g_mix[0], w_in[0], w_a2[0], b_a[0], g_gla_out[0], g_q[0], w_uq[0],
                      g_kv[0], w_uk[0], w_uv[0], w_o[0], g_ffn[0], w_up[0], conv_w[0],
                      conv_b[0], w_down[0], final_norm)
    dk, dv, dff2 = w["dk"], w["dv"], conv_w.shape[2]

    n_meta = meta_tokens.shape[0]
    assert n_meta == N_META == ts and seq % CHUNK == 0
    x_short = jnp.concatenate([x_sample.reshape(bs * ts, d), meta_tokens.astype(F32)], axis=0)
    pos_short = jnp.concatenate([jnp.tile(P + jnp.arange(ts, dtype=jnp.int32), bs),
                                 jnp.arange(n_meta, dtype=jnp.int32)])
    pos_long = n_meta + jnp.arange(seq, dtype=jnp.int32)
    pr_short = _project(x_short, pos_short, w)
    pr_long = _project(x_prompt[0], pos_long, w)
    (ys, lat_s, kr_s, st_s, cv_s), (lat_m, kr_m, st_m, cv_m, prefix) = _short_streams(
        x_short, pr_short, w, B=bs, T=ts, past_lat=cache_mla_latent[0],
        past_kr=cache_mla_krope[0], s0_s=state_gla[0], hist_s=cache_ffn_conv[0])
    yp, lat_p, kr_p, st_p, cv_p = _long_stream(
        x_prompt[0], pr_long, w, T=seq, s0=st_m, hist=cv_m, prefix=prefix)

    rk = lat_p.shape[1]
    T = n_meta + seq
    return (yp,
            ys,
            jnp.concatenate([lat_m, lat_p], axis=0).reshape(1, 1, T, rk),
            jnp.concatenate([kr_m, kr_p], axis=0)[:, :MLA_ROPE].reshape(1, 1, T, MLA_ROPE),
            st_p[None],
            cv_p[None],
            lat_s.reshape(1, bs, ts, rk),
            kr_s[:, :MLA_ROPE].reshape(1, bs, ts, MLA_ROPE),
            st_s[None],
            cv_s[None])
```

```python
import functools

import jax
import jax.numpy as jnp
from jax import lax
from jax.experimental import pallas as pl
from jax.experimental.pallas import tpu as pltpu

BF16 = jnp.bfloat16
F32 = jnp.float32

CHUNK = 64
CHUNK_SHIFT = 6
N_META = 16
EPS = 1e-6
GLA_HEADS = 4
GLA_GATE_NORM = 16.0
GLA_LOG_ALPHA_MIN = -5.0
MLA_HEADS = 16
MLA_NOPE = 128
MLA_ROPE = 64
MLA_V = 128
ROPE_THETA = 10000.0
CONV_W = 3
NEG_BIG = -1e30
LOG2E = 1.4426950408889634
QK_SCALE_LOG2E = (MLA_NOPE + MLA_ROPE) ** -0.5 * LOG2E

LANE = 128
VT_ONES = 16
GLA_CHUNK = 256
GLA_SEQS = 4
SAFE_EXP = 64.0
ROW_TILE = 1024
VMEM_LIMIT = 56 * 1024 * 1024


def _cparams(sem, vmem=VMEM_LIMIT):
    return pltpu.CompilerParams(dimension_semantics=sem, vmem_limit_bytes=vmem)


def _rmsnorm(x, g):
    return x * lax.rsqrt(jnp.mean(x * x, axis=-1, keepdims=True) + EPS) * g


def _sigmoid(x):
    return 0.5 * jnp.tanh(0.5 * x) + 0.5


def _pick(n, cands):
    for c in cands:
        if n % c == 0:
            return c
    fits = [t for t in range(16, min(n, max(cands)) + 1, 16) if n % t == 0]
    if not fits:
        raise ValueError(f"no tile in {cands} divides {n}")
    return fits[-1]


_NT = (((1,), (1,)), ((), ()))


def _matmul_wt_kernel(a_ref, w_ref, o_ref, w_scr):
    @pl.when(pl.program_id(1) == 0)
    def _():
        w_scr[...] = w_ref[...].astype(BF16)

    o_ref[...] = lax.dot_general(a_ref[...], w_scr[...], _NT,
                                 preferred_element_type=F32).astype(o_ref.dtype)


def _matmul_wt(a, w_t, row0, n, out_dtype, tn):
    m, k = a.shape
    tm = _pick(m, (ROW_TILE, 512, 384, 128))
    return pl.pallas_call(
        _matmul_wt_kernel,
        grid=(n // tn, m // tm),
        in_specs=[pl.BlockSpec((tm, k), lambda j, i: (i, 0)),
                  pl.BlockSpec((pl.Element(tn), pl.Element(k)),
                               lambda j, i: (pl.multiple_of(row0 + j * tn, 16), 0))],
        out_specs=pl.BlockSpec((tm, tn), lambda j, i: (i, j)),
        out_shape=jax.ShapeDtypeStruct((m, n), out_dtype),
        scratch_shapes=[pltpu.VMEM((tn, k), BF16)],
        compiler_params=_cparams(("parallel", "arbitrary")),
        name="in_proj_wt",
    )(a, w_t)


def _front_kernel(x_ref, g_ref, w_ref, gkv_ref, cos_ref, sin_ref,
                  h_ref, o_ref, lat_ref, kr_ref, w_scr, *, rank, rq, rk):
    @pl.when(pl.program_id(0) == 0)
    def _():
        w = w_ref[...].astype(BF16)
        half = MLA_ROPE // 2
        pe0 = rank + rq + rk
        o_pe = rq + rk
        w_scr[0:rq] = w[rank:rank + rq]
        w_scr[rq:o_pe] = w[rank + rq:pe0]
        w_scr[o_pe:o_pe + MLA_ROPE] = w[pe0:pe0 + MLA_ROPE]
        w_scr[o_pe + MLA_ROPE:o_pe + MLA_ROPE + half] = w[pe0 + half:pe0 + MLA_ROPE]
        w_scr[o_pe + MLA_ROPE + half:o_pe + 2 * MLA_ROPE] = w[pe0:pe0 + half]
        o_a = o_pe + 2 * MLA_ROPE
        w_scr[o_a:o_a + rank] = w[0:rank]
        w_scr[o_a + rank:] = jnp.zeros((w_scr.shape[0] - o_a - rank, w_scr.shape[1]), BF16)

    h = _rmsnorm(x_ref[...], g_ref[...]).astype(BF16)
    h_ref[...] = h
    small = lax.dot_general(h, w_scr[...], _NT, preferred_element_type=F32)
    o_ref[...] = small
    lat_ref[...] = _rmsnorm(small[:, rq:rq + rk], gkv_ref[...])
    blk = small[:, rq + rk:rq + rk + LANE]
    kr_ref[...] = blk * cos_ref[...] + pltpu.roll(blk, LANE // 2, 1) * sin_ref[...]


def _front(x, g_mix, w_t, row0, g_kv, cos_t, sin_t, *, rank, rq, rk):
    m, k = x.shape
    n_in = rank + rq + rk + MLA_ROPE
    n_out = rq + rk + 2 * MLA_ROPE + LANE
    tm = _pick(m, (512, 384, 128))
    kern = functools.partial(_front_kernel, rank=rank, rq=rq, rk=rk)
    row = lambda i: (i, 0)
    return pl.pallas_call(
        kern,
        grid=(m // tm,),
        in_specs=[pl.BlockSpec((tm, k), row),
                  pl.BlockSpec((1, k), lambda i: (0, 0)),
                  pl.BlockSpec((pl.Element(n_in), pl.Element(k)), lambda i: (row0, 0),
                               pipeline_mode=pl.Buffered(1)),
                  pl.BlockSpec((1, rk), lambda i: (0, 0)),
                  pl.BlockSpec((tm, LANE), row),
                  pl.BlockSpec((tm, LANE), row)],
        out_specs=[pl.BlockSpec((tm, k), row),
                   pl.BlockSpec((tm, n_out), row),
                   pl.BlockSpec((tm, rk), row),
                   pl.BlockSpec((tm, LANE), row)],
        out_shape=[jax.ShapeDtypeStruct((m, k), BF16),
                   jax.ShapeDtypeStruct((m, n_out), F32),
                   jax.ShapeDtypeStruct((m, rk), F32),
                   jax.ShapeDtypeStruct((m, LANE), F32)],
        scratch_shapes=[pltpu.VMEM((n_out, k), BF16)],
        compiler_params=_cparams(("arbitrary",)),
        name="front_proj",
    )(x, g_mix.reshape(1, -1), w_t, g_kv.reshape(1, -1), cos_t, sin_t)


def _split3(x):
    a = x.astype(BF16)
    r1 = x - a.astype(F32)
    b = r1.astype(BF16)
    c = (r1 - b.astype(F32)).astype(BF16)
    return a, b, c


def _gla_kernel(q_ref, k_ref, v_ref, r_ref, ga_ref, a_ref, wa_ref, ba_ref, go_ref, s0_ref,
                o_ref, sout_ref, s_scr, *, C, SB, T, H, dk, dv, S):
    c_idx = pl.program_id(1)
    n_chunks = pl.num_programs(1)
    R = S * C

    @pl.when(c_idx == 0)
    def _():
        s_scr[...] = s0_ref[...]

    z = jnp.dot(a_ref[...].astype(BF16), wa_ref[...], preferred_element_type=F32) + ba_ref[...]
    log_sig = jnp.minimum(z, 0.0) - jnp.log(1.0 + jnp.exp(-jnp.abs(z)))
    la = jnp.maximum(log_sig * (1.0 / GLA_GATE_NORM), GLA_LOG_ALPHA_MIN)
    if T % C:
        rows = c_idx * C + lax.broadcasted_iota(jnp.int32, (C, 1), 0)
        la = jnp.where(rows < T, la, 0.0)

    ri = lax.broadcasted_iota(jnp.int32, (R, R), 0)
    ci = lax.broadcasted_iota(jnp.int32, (R, R), 1)
    same_seq = (ri >= ci) if S == 1 else ((ri >= ci) & (ri - ci <= lax.rem(ri, C)))
    tri = jnp.where(same_seq, 1.0, 0.0).astype(BF16)
    ones = jnp.ones((C, LANE), BF16)
    cs_all = jnp.zeros_like(la)
    dsum_all = [jnp.zeros((la.shape[1], LANE), F32) for _ in range(S)]
    for piece in _split3(la):
        cs_all = cs_all + jnp.dot(tri, piece, preferred_element_type=F32)
        for si in range(S):
            dsum_all[si] = dsum_all[si] + lax.dot_general(
                piece[si * C:(si + 1) * C], ones, (((0,), (0,)), ((), ())),
                preferred_element_type=F32)

    sr = lax.broadcasted_iota(jnp.int32, (SB, SB), 0)
    sc = lax.broadcasted_iota(jnp.int32, (SB, SB), 1)
    causal = sr >= sc
    nt = (((1,), (1,)), ((), ()))
    scale = dk ** -0.5

    for si, h in [(si, h) for si in range(S) for h in range(H)]:
        rs = slice(si * C, (si + 1) * C)
        ksl = slice(h * dk, (h + 1) * dk)
        vsl = slice(h * dv, (h + 1) * dv)
        cs = cs_all[rs, ksl]
        c_last = cs[C - 1:C, :]
        q = q_ref[rs, ksl].astype(F32) * scale
        k = k_ref[rs, ksl].astype(F32)
        v = v_ref[rs, vsl]
        s_old = s_scr[si, h]

        o_inter = jnp.dot((q * jnp.exp(cs)).astype(BF16), s_old.astype(BF16),
                          preferred_element_type=F32)
        k_end = (k * jnp.exp(c_last - cs)).astype(BF16)
        upd = lax.dot_general(k_end, v, (((0,), (0,)), ((), ())), preferred_element_type=F32)
        dcol = jnp.exp(dsum_all[si][ksl, :])
        s_scr[si, h] = jnp.concatenate([dcol] * (dv // LANE), axis=1) * s_old + upd

        outs = []
        for i in range(C // SB):
            lo = i * SB
            cs_i = cs[lo:lo + SB]
            q_i = q[lo:lo + SB]
            k_i = k[lo:lo + SB]
            start = cs[lo - 1:lo] if i > 0 else jnp.zeros_like(c_last)
            mid = 0.5 * (start + cs[lo + SB - 1:lo + SB])
            qd = (q_i * jnp.exp(cs_i - mid)).astype(BF16)
            kd = (k_i * jnp.exp(mid - cs_i)).astype(BF16)
            att = lax.dot_general(qd, kd, nt, preferred_element_type=F32)
            att = jnp.where(causal, att, 0.0)
            o_i = jnp.dot(att.astype(BF16), v[lo:lo + SB], preferred_element_type=F32)
            if i > 0:
                qo = (q_i * jnp.exp(cs_i - start)).astype(BF16)
                ko = (k[:lo] * jnp.exp(start - cs[:lo])).astype(BF16)
                att_o = lax.dot_general(qo, ko, nt, preferred_element_type=F32)
                o_i = o_i + jnp.dot(att_o.astype(BF16), v[:lo], preferred_element_type=F32)
            outs.append(o_i)
        o = o_inter + (jnp.concatenate(outs, axis=0) if len(outs) > 1 else outs[0])

        on = _rmsnorm(o, go_ref[...])
        r = r_ref[rs, vsl].astype(F32)
        g = ga_ref[rs, vsl].astype(F32)
        o_ref[rs, vsl] = (_sigmoid(g) * (on * (r * _sigmoid(r)))).astype(o_ref.dtype)

    @pl.when(c_idx == n_chunks - 1)
    def _():
        sout_ref[...] = s_scr[...]


def _gla(qkvr, gates, small, wa_pad, b_a, g_out, s0, *, B, T, Tp, dk, dv, col, row0=0):
    C = min(GLA_CHUNK, Tp)
    SB = min(32, C)
    nc = Tp // C
    H = GLA_HEADS
    qk, vw = H * dk, H * dv
    S = _pick(B, (GLA_SEQS, 1)) if nc == 1 else 1
    R = S * C
    assert row0 % R == 0
    rb = lambda b, c: row0 // R + b * nc + c
    kern = functools.partial(_gla_kernel, C=C, SB=SB, T=T, H=H, dk=dk, dv=dv, S=S)
    return pl.pallas_call(
        kern,
        grid=(B // S, nc),
        in_specs=[
            pl.BlockSpec((R, qk), lambda b, c: (rb(b, c), col["q"] // qk)),
            pl.BlockSpec((R, qk), lambda b, c: (rb(b, c), col["k"] // qk)),
            pl.BlockSpec((R, vw), lambda b, c: (rb(b, c), col["v"] // vw)),
            pl.BlockSpec((R, vw), lambda b, c: (rb(b, c), col["r"] // vw)),
            pl.BlockSpec((R, vw), lambda b, c: (rb(b, c), col["ga"] // vw)),
            pl.BlockSpec((R, LANE), lambda b, c: (rb(b, c), col["a"] // LANE)),
            pl.BlockSpec((LANE, qk), lambda b, c: (0, 0)),
            pl.BlockSpec((1, qk), lambda b, c: (0, 0)),
            pl.BlockSpec((1, dv), lambda b, c: (0, 0)),
            pl.BlockSpec((S, H, dk, dv), lambda b, c: (b, 0, 0, 0)),
        ],
        out_specs=[
            pl.BlockSpec((R, vw), lambda b, c: (b * nc + c, 0)),
            pl.BlockSpec((S, H, dk, dv), lambda b, c: (b, 0, 0, 0)),
        ],
        out_shape=[jax.ShapeDtypeStruct((B * Tp, vw), BF16),
                   jax.ShapeDtypeStruct((B, H, dk, dv), F32)],
        scratch_shapes=[pltpu.VMEM((S, H, dk, dv), F32)],
        compiler_params=_cparams(("parallel", "arbitrary")),
        name="gla",
    )(qkvr, qkvr, qkvr, qkvr, gates, small, wa_pad, b_a.reshape(1, -1), g_out.reshape(1, -1), s0)


def _qprep_kernel(cq_ref, gq_ref, wn_ref, wp_ref, wps_ref, cos_ref, sin_ref, q_ref):
    hq = _rmsnorm(cq_ref[...], gq_ref[...]).astype(BF16)
    qn = jnp.dot(hq, wn_ref[...], preferred_element_type=F32)
    qp = jnp.dot(hq, wp_ref[...], preferred_element_type=F32)
    qs = jnp.dot(hq, wps_ref[...], preferred_element_type=F32)
    cos = cos_ref[...] * QK_SCALE_LOG2E
    sin = sin_ref[...] * QK_SCALE_LOG2E
    for h in range(MLA_HEADS):
        sl = slice(h * LANE, (h + 1) * LANE)
        q_ref[h, :, 0:LANE] = (qn[:, sl] * QK_SCALE_LOG2E).astype(BF16)
        q_ref[h, :, LANE:2 * LANE] = (qp[:, sl] * cos + qs[:, sl] * sin).astype(BF16)


def _qprep(small, g_q, wn, wp, wps, cos_t, sin_t, *, col):
    m = small.shape[0]
    rq = wn.shape[0]
    tm = _pick(m, (256, 128))
    full = lambda i: (0, 0)
    return pl.pallas_call(
        _qprep_kernel,
        grid=(m // tm,),
        in_specs=[pl.BlockSpec((tm, rq), lambda i: (i, col["cq"] // rq)),
                  pl.BlockSpec((1, rq), full),
                  pl.BlockSpec(wn.shape, full),
                  pl.BlockSpec(wp.shape, full),
                  pl.BlockSpec(wps.shape, full),
                  pl.BlockSpec((tm, LANE), lambda i: (i, 0)),
                  pl.BlockSpec((tm, LANE), lambda i: (i, 0))],
        out_specs=pl.BlockSpec((MLA_HEADS, tm, 2 * LANE), lambda i: (0, i, 0)),
        out_shape=jax.ShapeDtypeStruct((MLA_HEADS, m, 2 * LANE), BF16),
        compiler_params=_cparams(("parallel",)),
        name="mla_q",
    )(small, g_q.reshape(1, -1), wn, wp, wps, cos_t, sin_t)


def _kvup_kernel(lat_ref, kr_ref, wuk_ref, wuv_ref, k_ref, v_ref, *, v_transposed):
    lat = lat_ref[...].astype(BF16)
    kn = jnp.dot(lat, wuk_ref[...], preferred_element_type=F32)
    kr = kr_ref[...]
    lane = lax.broadcasted_iota(jnp.int32, kr.shape, 1)
    kp = jnp.where(lane == MLA_ROPE, 1.0, kr).astype(BF16)
    if v_transposed:
        vv = lax.dot_general(wuv_ref[...], lat, (((1,), (1,)), ((), ())),
                             preferred_element_type=F32)
    else:
        vv = jnp.dot(lat, wuv_ref[...], preferred_element_type=F32)
    for h in range(MLA_HEADS):
        sl = slice(h * LANE, (h + 1) * LANE)
        k_ref[h, :, 0:LANE] = kn[:, sl].astype(BF16)
        k_ref[h, :, LANE:2 * LANE] = kp
        if v_transposed:
            v_ref[h, 0:LANE, :] = vv[sl, :].astype(BF16)
            v_ref[h, LANE:LANE + VT_ONES, :] = jnp.ones((VT_ONES, vv.shape[1]), BF16)
        else:
            v_ref[h] = vv[:, sl].astype(BF16)


def _kvup(lat, kr, wuk, wuv, *, v_transposed=False):
    m, rk = lat.shape
    tm = _pick(m, (512, 256, 128))
    full = lambda i: (0, 0)
    if v_transposed:
        v_spec = pl.BlockSpec((MLA_HEADS, LANE + VT_ONES, tm), lambda i: (0, 0, i))
        v_shape = (MLA_HEADS, LANE + VT_ONES, m)
    else:
        v_spec = pl.BlockSpec((MLA_HEADS, tm, LANE), lambda i: (0, i, 0))
        v_shape = (MLA_HEADS, m, LANE)
    return pl.pallas_call(
        functools.partial(_kvup_kernel, v_transposed=v_transposed),
        grid=(m // tm,),
        in_specs=[pl.BlockSpec((tm, rk), lambda i: (i, 0)),
                  pl.BlockSpec((tm, LANE), lambda i: (i, 0)),
                  pl.BlockSpec(wuk.shape, full),
                  pl.BlockSpec(wuv.shape, full)],
        out_specs=[pl.BlockSpec((MLA_HEADS, tm, 2 * LANE), lambda i: (0, i, 0)), v_spec],
        out_shape=[jax.ShapeDtypeStruct((MLA_HEADS, m, 2 * LANE), BF16),
                   jax.ShapeDtypeStruct(v_shape, BF16)],
        compiler_params=_cparams(("parallel",)),
        name="mla_kv",
    )(lat, kr, wuk, wuv)


def _last_kblock(qi, *, tq, tk, nk, q_off, k_off):
    top_chunk = ((qi + 1) * tq - 1 + q_off) // CHUNK
    last_key = (top_chunk + 1) * CHUNK - 1 - k_off
    return jnp.minimum(last_key // tk, nk - 1)


def _attn_kernel(q_ref, k_ref, v_ref, o_ref, m_scr, l_scr, acc_scr, *, hps, tq, tk, nk,
                 q_off, k_off):
    qi = pl.program_id(2)
    ki = pl.program_id(3)

    @pl.when(ki == 0)
    def _():
        m_scr[...] = jnp.full(m_scr.shape, NEG_BIG, F32)
        l_scr[...] = jnp.zeros(l_scr.shape, F32)
        acc_scr[...] = jnp.zeros(acc_scr.shape, F32)

    @pl.when(ki <= _last_kblock(qi, tq=tq, tk=tk, nk=nk, q_off=q_off, k_off=k_off))
    def _():
        q_chunk = (qi * tq + q_off + lax.broadcasted_iota(jnp.int32, (tq, 1), 0)) >> CHUNK_SHIFT
        k_chunk = (ki * tk + k_off + lax.broadcasted_iota(jnp.int32, (1, tk), 1)) >> CHUNK_SHIFT
        visible = q_chunk >= k_chunk

        def head(h, carry):
            s = lax.dot_general(q_ref[h], k_ref[h], (((1,), (1,)), ((), ())),
                                preferred_element_type=F32)
            s = jnp.where(visible, s, NEG_BIG)
            m_prev = m_scr[h]
            m_new = jnp.maximum(m_prev, jnp.max(s, axis=-1, keepdims=True))
            p = jnp.exp2(s - m_new)
            alpha = jnp.exp2(m_prev - m_new)
            l_scr[h] = alpha * l_scr[h] + jnp.sum(p, axis=-1, keepdims=True)
            acc_scr[h] = alpha * acc_scr[h] + jnp.dot(p.astype(BF16), v_ref[h],
                                                      preferred_element_type=F32)
            m_scr[h] = m_new
            return carry

        lax.fori_loop(0, hps, head, 0)

    @pl.when(ki == nk - 1)
    def _():
        for h in range(hps):
            o_ref[:, h * LANE:(h + 1) * LANE] = (acc_scr[h] / l_scr[h]).astype(o_ref.dtype)


def _attention(q, k, v, *, B, Tq, Tk, tq, tk, hps, q_off, k_off):
    nq = Tq // tq
    nk = Tk // tk
    hg = MLA_HEADS // hps
    dqk = q.shape[2]
    dvh = v.shape[2]
    last = functools.partial(_last_kblock, tq=tq, tk=tk, nk=nk, q_off=q_off, k_off=k_off)
    kern = functools.partial(_attn_kernel, hps=hps, tq=tq, tk=tk, nk=nk, q_off=q_off,
                             k_off=k_off)
    kv_row = lambda b, g, i, j: b * nk + jnp.minimum(j, last(i))
    return pl.pallas_call(
        kern,
        grid=(B, hg, nq, nk),
        in_specs=[pl.BlockSpec((hps, tq, dqk), lambda b, g, i, j: (g, b * nq + i, 0)),
                  pl.BlockSpec((hps, tk, dqk), lambda b, g, i, j: (g, kv_row(b, g, i, j), 0)),
                  pl.BlockSpec((hps, tk, dvh), lambda b, g, i, j: (g, kv_row(b, g, i, j), 0))],
        out_specs=pl.BlockSpec((tq, hps * dvh), lambda b, g, i, j: (b * nq + i, g)),
        out_shape=jax.ShapeDtypeStruct((B * Tq, MLA_HEADS * dvh), BF16),
        scratch_shapes=[pltpu.VMEM((hps, tq, 1), F32),
                        pltpu.VMEM((hps, tq, 1), F32),
                        pltpu.VMEM((hps, tq, dvh), F32)],
        compiler_params=_cparams(("parallel", "parallel", "parallel", "arbitrary")),
        name="mla_attn",
    )(q, k, v)


def _attn_t_kernel(qi_ref, ki_ref, q_ref, k_ref, vt_ref, kp_ref, vtp_ref, o_ref,
                   q_scr, r_scr, acc_scr, *, hps, t):
    pair = pl.program_id(1)
    qi = qi_ref[pair]
    ki = ki_ref[pair]
    nt = (((1,), (1,)), ((), ()))
    pe = slice(LANE, 2 * LANE)
    lane = lax.broadcasted_iota(jnp.int32, (t, LANE), 1)

    def set_reference(h, r):
        neg_r = jnp.transpose(jnp.broadcast_to(-r, (LANE, t)))
        q_scr[h, :, pe] = jnp.where(lane == MLA_ROPE, neg_r.astype(BF16), q_ref[h, :, pe])
        r_scr[h] = r

    def shifted_scores(h):
        return lax.dot_general(k_ref[h], q_scr[h], nt, preferred_element_type=F32)

    @pl.when(ki == 0)
    def _():
        for h in range(hps):
            q_scr[h, :, 0:LANE] = q_ref[h, :, 0:LANE]
            s = lax.dot_general(kp_ref[h], q_ref[h], nt, preferred_element_type=F32)
            r = jnp.max(s, axis=0, keepdims=True).astype(BF16).astype(F32)
            p = jnp.exp2((s - r).astype(BF16))
            acc_scr[h] = jnp.dot(vtp_ref[h], p, preferred_element_type=F32)
            set_reference(h, r)

    def general(h, bias, keep_reference):
        sp = shifted_scores(h)
        if bias is not None:
            sp = sp + bias
        r = r_scr[h]
        rise = jnp.maximum(jnp.max(sp, axis=0, keepdims=True), 0.0)
        r_new = (r + rise).astype(BF16).astype(F32)
        delta = r_new - r
        p = jnp.exp2((sp - delta).astype(BF16))
        acc_scr[h] = jnp.exp2(-delta) * acc_scr[h] + jnp.dot(vt_ref[h], p,
                                                               preferred_element_type=F32)
        if keep_reference:
            set_reference(h, r_new)

    @pl.when(ki < qi)
    def _():
        unsafe = []
        sp_next = shifted_scores(0)
        for h in range(hps):
            sp = sp_next
            if h + 1 < hps:
                sp_next = shifted_scores(h + 1)
            safe = jnp.max(sp) <= SAFE_EXP
            part = jnp.dot(vt_ref[h], jnp.exp2(sp.astype(BF16)), preferred_element_type=F32)
            acc_scr[h] += jnp.where(safe, part, 0.0)
            unsafe.append(jnp.logical_not(safe))

        @pl.when(functools.reduce(jnp.logical_or, unsafe))
        def _():
            for h in range(hps):
                @pl.when(unsafe[h])
                def _():
                    general(h, None, True)

    @pl.when(ki == qi)
    def _():
        k_chunk = lax.broadcasted_iota(jnp.int32, (t, 1), 0) >> CHUNK_SHIFT
        q_chunk = lax.broadcasted_iota(jnp.int32, (1, t), 1) >> CHUNK_SHIFT
        bias = jnp.where(q_chunk >= k_chunk, 0.0, NEG_BIG)
        for h in range(hps):
            general(h, bias, False)
            acc = acc_scr[h]
            o_t = acc[0:LANE] / acc[LANE:LANE + 1]
            o_ref[:, h * LANE:(h + 1) * LANE] = o_t.T.astype(o_ref.dtype)


def _attention_t(q, k, vt, k_pre, vt_pre, *, T, t, hps):
    n = T // t
    hg = MLA_HEADS // hps
    dqk = q.shape[2]
    npre = k_pre.shape[1]
    vrows = vt.shape[1]
    pairs = [(i, j) for i in range(n) for j in range(i + 1)]
    qi_arr = jnp.asarray([p[0] for p in pairs], jnp.int32)
    ki_arr = jnp.asarray([p[1] for p in pairs], jnp.int32)
    kern = functools.partial(_attn_t_kernel, hps=hps, t=t)
    grid_spec = pltpu.PrefetchScalarGridSpec(
        num_scalar_prefetch=2,
        grid=(hg, len(pairs)),
        in_specs=[pl.BlockSpec((hps, t, dqk), lambda g, p, qi, ki: (g, qi[p], 0)),
                  pl.BlockSpec((hps, t, dqk), lambda g, p, qi, ki: (g, ki[p], 0)),
                  pl.BlockSpec((hps, vrows, t), lambda g, p, qi, ki: (g, 0, ki[p])),
                  pl.BlockSpec((hps, npre, dqk), lambda g, p, qi, ki: (g, 0, 0)),
                  pl.BlockSpec((hps, vrows, npre), lambda g, p, qi, ki: (g, 0, 0))],
        out_specs=pl.BlockSpec((t, hps * LANE), lambda g, p, qi, ki: (qi[p], g)),
        scratch_shapes=[pltpu.VMEM((hps, t, dqk), BF16),
                        pltpu.VMEM((hps, 1, t), F32),
                        pltpu.VMEM((hps, vrows, t), F32)])
    return pl.pallas_call(
        kern,
        grid_spec=grid_spec,
        out_shape=jax.ShapeDtypeStruct((T, MLA_HEADS * LANE), BF16),
        compiler_params=_cparams(("parallel", "arbitrary")),
        name="mla_attn_t",
    )(qi_arr, ki_arr, q, k, vt, k_pre, vt_pre)


def _absorb_q_kernel(q_ref, w_ref, o_ref):
    o_ref[0] = jnp.dot(q_ref[0, :, 0:MLA_NOPE], w_ref[0],
                       preferred_element_type=F32).astype(o_ref.dtype)


def _absorb_q(q, w_uk_t3):
    heads, rows, dqk = q.shape
    rk = w_uk_t3.shape[2]
    return pl.pallas_call(
        _absorb_q_kernel,
        grid=(heads,),
        in_specs=[pl.BlockSpec((1, rows, dqk), lambda h: (h, 0, 0)),
                  pl.BlockSpec((1, MLA_NOPE, rk), lambda h: (h, 0, 0))],
        out_specs=pl.BlockSpec((1, rows, rk), lambda h: (h, 0, 0)),
        out_shape=jax.ShapeDtypeStruct((heads, rows, rk), BF16),
        compiler_params=_cparams(("parallel",)),
        name="mla_absorb_q",
    )(q, w_uk_t3)


def _attn_latent_kernel(ql_ref, q_ref, plat_ref, pkr_ref, lat_ref, kr_ref, o_ref, *, T, P, S):
    heads, _, rk = ql_ref.shape
    rows = heads * T
    nt = (((1,), (1,)), ((), ()))
    tok = lax.rem(lax.broadcasted_iota(jnp.int32, (rows, 1), 0), T)
    q_chunk = (P + tok) >> CHUNK_SHIFT
    k_chunk = lax.broadcasted_iota(jnp.int32, (1, P + T), 1) >> CHUNK_SHIFT
    visible = q_chunk >= k_chunk
    for si in range(S):
        ts = slice(si * T, (si + 1) * T)
        ql = ql_ref[:, ts, :].reshape(rows, rk)
        qpe = q_ref[:, ts, LANE:2 * LANE].reshape(rows, LANE)[:, 0:MLA_ROPE]
        lat_all = jnp.concatenate([plat_ref[si].astype(BF16), lat_ref[ts, :].astype(BF16)],
                                  axis=0)
        s_pe = jnp.concatenate(
            [jnp.dot(qpe, pkr_ref[si].astype(BF16), preferred_element_type=F32),
             lax.dot_general(qpe, kr_ref[ts, 0:MLA_ROPE].astype(BF16), nt,
                             preferred_element_type=F32)], axis=1)
        s = lax.dot_general(ql, lat_all, nt, preferred_element_type=F32) + s_pe
        s = jnp.where(visible, s, NEG_BIG)
        p = jnp.exp2(s - jnp.max(s, axis=-1, keepdims=True))
        o = jnp.dot(p.astype(BF16), lat_all, preferred_element_type=F32)
        o = o / jnp.sum(p, axis=-1, keepdims=True)
        o_ref[:, ts, :] = o.reshape(heads, T, rk).astype(o_ref.dtype)


def _attn_latent(qlat, q, past_lat, past_kr_t, lat, kr, *, B, T):
    heads, _, rk = qlat.shape
    P = past_lat.shape[1]
    S = _pick(B, (2, 1))
    kern = functools.partial(_attn_latent_kernel, T=T, P=P, S=S)
    return pl.pallas_call(
        kern,
        grid=(B // S,),
        in_specs=[pl.BlockSpec((heads, S * T, rk), lambda b: (0, b, 0)),
                  pl.BlockSpec((heads, S * T, q.shape[2]), lambda b: (0, b, 0)),
                  pl.BlockSpec((S, P, rk), lambda b: (b, 0, 0)),
                  pl.BlockSpec((S, past_kr_t.shape[1], P), lambda b: (b, 0, 0)),
                  pl.BlockSpec((S * T, rk), lambda b: (b, 0)),
                  pl.BlockSpec((S * T, LANE), lambda b: (b, 0))],
        out_specs=pl.BlockSpec((heads, S * T, rk), lambda b: (0, b, 0)),
        out_shape=jax.ShapeDtypeStruct((heads, B * T, rk), BF16),
        compiler_params=_cparams(("parallel",)),
        name="mla_attn_latent",
    )(qlat, q, past_lat, past_kr_t, lat, kr)


def _absorb_out_kernel(o_ref, w_ref, out_ref):
    out_ref[...] = jnp.dot(o_ref[0], w_ref[0], preferred_element_type=F32).astype(out_ref.dtype)


def _absorb_out(olat, w_uv3):
    heads, rows, rk = olat.shape
    dvh = w_uv3.shape[2]
    return pl.pallas_call(
        _absorb_out_kernel,
        grid=(heads,),
        in_specs=[pl.BlockSpec((1, rows, rk), lambda h: (h, 0, 0)),
                  pl.BlockSpec((1, rk, dvh), lambda h: (h, 0, 0))],
        out_specs=pl.BlockSpec((rows, dvh), lambda h: (0, h)),
        out_shape=jax.ShapeDtypeStruct((rows, heads * dvh), BF16),
        compiler_params=_cparams(("parallel",)),
        name="mla_absorb_out",
    )(olat, w_uv3)


def _merge_kernel(a_ref, gb_ref, om_ref, x_ref, wo_ref, gf_ref, x1_ref, h2_ref):
    merged = a_ref[...].astype(F32) + _sigmoid(gb_ref[...].astype(F32)) * om_ref[...].astype(F32)
    x1 = x_ref[...] + jnp.dot(merged.astype(BF16), wo_ref[...], preferred_element_type=F32)
    x1_ref[...] = x1
    h2_ref[...] = _rmsnorm(x1, gf_ref[...]).astype(BF16)


def _merge(branch_a, gates, o_m, x, wo, g_ffn, *, col):
    m, d = x.shape
    tm = _pick(m, (512, 384, 256, 128))
    row = lambda i: (i, 0)
    return pl.pallas_call(
        _merge_kernel,
        grid=(m // tm,),
        in_specs=[pl.BlockSpec((tm, d), row),
                  pl.BlockSpec((tm, d), lambda i: (i, col["gb"] // d)),
                  pl.BlockSpec((tm, d), row),
                  pl.BlockSpec((tm, d), row),
                  pl.BlockSpec(wo.shape, lambda i: (0, 0), pipeline_mode=pl.Buffered(1)),
                  pl.BlockSpec((1, d), lambda i: (0, 0))],
        out_specs=[pl.BlockSpec((tm, d), row), pl.BlockSpec((tm, d), row)],
        out_shape=[jax.ShapeDtypeStruct((m, d), F32), jax.ShapeDtypeStruct((m, d), BF16)],
        compiler_params=_cparams(("parallel",)),
        name="merge_out_proj",
    )(branch_a, gates, o_m, x, wo, g_ffn.reshape(1, -1))


HALO = 8


def _ffn_up_kernel(*refs, bb, r, tf, loc, carried, cast_down):
    (h_ref, wa_ref, wb_ref, cwa_ref, cwb_ref, cba_ref, cbb_ref, ha_ref, hb_ref), refs = \
        refs[:9], refs[9:]
    if cast_down:
        wd_ref, act_ref, ca_ref, cb_ref, wdb_ref, ext_scr, carry_scr, w_scr = refs
    else:
        act_ref, ca_ref, cb_ref, ext_scr, carry_scr, w_scr = refs
    s = pl.program_id(1)
    rt = pl.program_id(2)
    d = h_ref.shape[2]

    @pl.when((s == 0) & (rt == 0))
    def _():
        w_scr[0] = wa_ref[...].astype(BF16)
        w_scr[1] = wb_ref[...].astype(BF16)
        if cast_down:
            wdb_ref[...] = wd_ref[...].astype(BF16)

    if carried:
        @pl.when(rt == 0)
        def _():
            carry_scr[0] = ha_ref[...]
            carry_scr[1] = hb_ref[...]

    h = h_ref[...].reshape(bb * r, d)
    conv = []
    for half, (cw_ref, cbias_ref, hist_ref, cout_ref) in enumerate(
            ((cwa_ref, cba_ref, ha_ref, ca_ref), (cwb_ref, cbb_ref, hb_ref, cb_ref))):
        u = jnp.dot(h, w_scr[half], preferred_element_type=F32).reshape(bb, r, tf)
        ext_scr[half, :, HALO:HALO + r, :] = u
        ext_scr[half, :, HALO - 2:HALO, :] = carry_scr[half] if carried else hist_ref[...]
        u1 = ext_scr[half, :, HALO - 1:HALO - 1 + r, :]
        u2 = ext_scr[half, :, HALO - 2:HALO - 2 + r, :]
        cw = cw_ref[...]
        conv.append(cbias_ref[...] + cw[0:1] * u2 + cw[1:2] * u1 + cw[2:3] * u)
        if carried:
            carry_scr[half] = ext_scr[half, :, HALO + r - 2:HALO + r, :]
        cout_ref[0] = ext_scr[half, :, HALO + loc:HALO + loc + 2, :]

    act_ref[...] = ((conv[0] * _sigmoid(conv[0])) * conv[1]).astype(act_ref.dtype)


def _ffn_down_kernel(act_ref, wd_ref, x1_ref, gf_ref, y_ref):
    down = jnp.dot(act_ref[...], wd_ref[...], preferred_element_type=F32)
    y_ref[...] = _rmsnorm(x1_ref[...] + down, gf_ref[...])


def _ffn(h2, x1, w_up, w_down, conv_w, conv_b, hist, g_final, *, B, T, Tp):
    d = h2.shape[1]
    dff = w_down.shape[0]
    cast_down = w_down.dtype != BF16
    tf = _pick(dff, (512, 256, 128))
    nf = dff // tf
    if Tp <= 128:
        bb, r = B, Tp
    else:
        bb, r = 1, _pick(Tp, (ROW_TILE, 128))
    nrt = Tp // r
    carried = nrt > 1
    loc = (T - 2) - (nrt - 1) * r
    assert 0 <= loc <= r - 2, "final two valid rows must sit in the last row tile"
    kern = functools.partial(_ffn_up_kernel, bb=bb, r=r, tf=tf, loc=loc, carried=carried,
                             cast_down=cast_down)
    carry_shape = (2, bb, 2, tf) if carried else (1, 1, 2, LANE)
    in_specs = [pl.BlockSpec((bb, r, d), lambda f, s, t: (s, t, 0)),
                pl.BlockSpec((d, tf), lambda f, s, t: (0, f)),
                pl.BlockSpec((d, tf), lambda f, s, t: (0, nf + f)),
                pl.BlockSpec((CONV_W, tf), lambda f, s, t: (0, f)),
                pl.BlockSpec((CONV_W, tf), lambda f, s, t: (0, nf + f)),
                pl.BlockSpec((1, tf), lambda f, s, t: (0, f)),
                pl.BlockSpec((1, tf), lambda f, s, t: (0, nf + f)),
                pl.BlockSpec((bb, 2, tf), lambda f, s, t: (s, 0, f)),
                pl.BlockSpec((bb, 2, tf), lambda f, s, t: (s, 0, nf + f))]
    out_specs = [pl.BlockSpec((bb, r, tf), lambda f, s, t: (s, t, f)),
                 pl.BlockSpec((1, bb, 2, tf), lambda f, s, t: (t, s, 0, f)),
                 pl.BlockSpec((1, bb, 2, tf), lambda f, s, t: (t, s, 0, f))]
    out_shape = [jax.ShapeDtypeStruct((B, Tp, dff), BF16),
                 jax.ShapeDtypeStruct((nrt, B, 2, dff), F32),
                 jax.ShapeDtypeStruct((nrt, B, 2, dff), F32)]
    args = [h2.reshape(B, Tp, d), w_up, w_up, conv_w, conv_w, conv_b.reshape(1, -1),
            conv_b.reshape(1, -1), hist, hist]
    if cast_down:
        in_specs.append(pl.BlockSpec((tf, d), lambda f, s, t: (f, 0)))
        out_specs.append(pl.BlockSpec((tf, d), lambda f, s, t: (f, 0)))
        out_shape.append(jax.ShapeDtypeStruct((dff, d), BF16))
        args.append(w_down)
    outs = pl.pallas_call(
        kern,
        grid=(nf, B // bb, nrt),
        in_specs=in_specs,
        out_specs=out_specs,
        out_shape=out_shape,
        scratch_shapes=[pltpu.VMEM((2, bb, HALO + r, tf), F32),
                        pltpu.VMEM(carry_shape, F32),
                        pltpu.VMEM((2, d, tf), BF16)],
        compiler_params=_cparams(("arbitrary", "arbitrary", "arbitrary")),
        name="conv_ffn_up",
    )(*args)
    act, ca, cb = outs[:3]
    if cast_down:
        w_down = outs[3]

    m = B * Tp
    tm = _pick(m, (256, 128))
    y = pl.pallas_call(
        _ffn_down_kernel,
        grid=(m // tm,),
        in_specs=[pl.BlockSpec((tm, dff), lambda i: (i, 0)),
                  pl.BlockSpec((dff, d), lambda i: (0, 0), pipeline_mode=pl.Buffered(1)),
                  pl.BlockSpec((tm, d), lambda i: (i, 0)),
                  pl.BlockSpec((1, d), lambda i: (0, 0))],
        out_specs=pl.BlockSpec((tm, d), lambda i: (i, 0)),
        out_shape=jax.ShapeDtypeStruct((m, d), F32),
        compiler_params=_cparams(("parallel",)),
        name="ffn_down",
    )(act.reshape(m, dff), w_down, x1, g_final.reshape(1, -1))
    return y.reshape(B, Tp, d), jnp.concatenate([ca[nrt - 1], cb[nrt - 1]], axis=-1), w_down


def _rope_tables(pos):
    half = MLA_ROPE // 2
    inv = ROPE_THETA ** (-jnp.arange(0, MLA_ROPE, 2, dtype=F32) / MLA_ROPE)
    ang = pos.astype(F32)[:, None] * inv[None, :]
    cos, sin = jnp.cos(ang), jnp.sin(ang)
    zero = jnp.zeros((pos.shape[0], LANE - 2 * half), F32)
    return (jnp.concatenate([cos, cos, zero], axis=1),
            jnp.concatenate([-sin, sin, zero], axis=1))


def _project(x, pos, w):
    col = w["col"]
    rows = w["in_rows"]
    cos_t, sin_t = _rope_tables(pos)
    h, small, lat, kr = _front(x, w["g_mix"], w["w_in_t"], rows["a"], w["g_kv"], cos_t, sin_t,
                               rank=rows["cq"] - rows["a"], rq=rows["ckv"] - rows["cq"],
                               rk=rows["kpe"] - rows["ckv"])
    qkvr = _matmul_wt(h, w["w_in_t"], rows["q"], rows["a"] - rows["q"], BF16, tn=1024)
    gates = _matmul_wt(h, w["w_in_t"], rows["ga"], rows["end"] - rows["ga"], BF16, tn=1024)
    q = _qprep(small, w["g_q"], w["wq_nope"], w["wq_pe"], w["wq_pe_sw"], cos_t, sin_t, col=col)
    return dict(qkvr=qkvr, gates=gates, small=small, q=q, lat=lat, kr=kr)


def _finish(x, pr, branch_a, o_m, w, hist, *, B, T):
    x1, h2 = _merge(branch_a, pr["gates"], o_m, x, w["w_o"], w["g_ffn"], col=w["col"])
    y, conv, w["w_down"] = _ffn(h2, x1, w["w_up"], w["w_down"], w["conv_w"], w["conv_b"], hist,
                                w["final_norm"], B=B, T=T, Tp=T)
    return y, conv


def _gla_group(pr, w, s0, *, B, T, row0=0):
    return _gla(pr["qkvr"], pr["gates"], pr["small"], w["wa_pad"], w["b_a"], w["g_gla_out"],
                s0, B=B, T=T, Tp=T, dk=w["dk"], dv=w["dv"], col=w["col"], row0=row0)


def _long_stream(x, pr, w, *, T, s0, hist, prefix):
    branch_a, state = _gla_group(pr, w, s0, B=1, T=T)
    k, vt = _kvup(pr["lat"], pr["kr"], w["w_uk"], w["w_uv_t"], v_transposed=True)
    o_m = _attention_t(pr["q"], k, vt, prefix[0], prefix[1], T=T, t=_pick(T, (1024, 128)),
                       hps=MLA_HEADS // 4)
    y, conv = _finish(x, pr, branch_a, o_m, w, hist, B=1, T=T)
    return y, pr["lat"], pr["kr"], state, conv


def _short_streams(x, pr, w, *, B, T, past_lat, past_kr, s0_s, hist_s):
    ns = B * T
    dk, dv = w["dk"], w["dv"]

    ba_s, st_s = _gla_group(pr, w, s0_s, B=B, T=T)
    ba_m, st_m = _gla_group(pr, w, jnp.zeros((1, GLA_HEADS, dk, dv), F32), B=1, T=T, row0=ns)

    qlat = _absorb_q(pr["q"], w["w_uk_t3"])
    olat = _attn_latent(qlat, pr["q"], past_lat, jnp.swapaxes(past_kr, 1, 2), pr["lat"],
                        pr["kr"], B=B, T=T)
    om_s = _absorb_out(olat, w["w_uv3"])
    q_m, lat_m, kr_m = pr["q"][:, ns:], pr["lat"][ns:], pr["kr"][ns:]
    k_m, v_m = _kvup(lat_m, kr_m, w["w_uk"], w["w_uv"])
    prefix = _kvup(lat_m, kr_m, w["w_uk"], w["w_uv_t"], v_transposed=True)
    om_m = _attention(q_m, k_m, v_m, B=1, Tq=T, Tk=T, tq=T, tk=T, hps=MLA_HEADS,
                      q_off=0, k_off=0)

    hist = jnp.concatenate([hist_s, jnp.zeros((1,) + hist_s.shape[1:], F32)], axis=0)
    y, conv = _finish(x, pr, jnp.concatenate([ba_s, ba_m], axis=0),
                      jnp.concatenate([om_s, om_m], axis=0), w, hist, B=B + 1, T=T)
    sample = (y[:B], pr["lat"][:ns], pr["kr"][:ns], st_s, conv[:B])
    meta = (lat_m, kr_m, st_m, conv[B:], prefix)
    return sample, meta


def _prep_weights(g_mix, w_in, w_a2, b_a, g_gla_out, g_q, w_uq, g_kv, w_uk, w_uv, w_o,
                  g_ffn, w_up, conv_w, conv_b, w_down, final_norm):
    d = w_in.shape[0]
    rank, gqk = w_a2.shape
    gvw = GLA_HEADS * g_gla_out.shape[0]
    rq, rk = g_q.shape[0], g_kv.shape[0]
    half = MLA_ROPE // 2
    o, offs = 0, {}
    for name, width in (("q", gqk), ("k", gqk), ("v", gvw), ("r", gvw), ("a", rank),
                        ("cq", rq), ("ckv", rk), ("kpe", MLA_ROPE), ("ga", d), ("gb", d)):
        offs[name] = (o, o + width)
        o += width
    assert o == w_in.shape[1]
    in_rows = {name: lo for name, (lo, _) in offs.items()}
    in_rows["end"] = o
    assert all(v % 16 == 0 for v in in_rows.values())
    col = {"q": 0, "k": gqk, "v": 2 * gqk, "r": 2 * gqk + gvw, "ga": 0, "gb": d,
           "cq": 0, "ckv": rq, "kpe": rq + rk, "a": rq + rk + 2 * MLA_ROPE}

    w3 = w_uq.reshape(rq, MLA_HEADS, MLA_NOPE + MLA_ROPE)
    pe = w3[:, :, MLA_NOPE:]
    pe_sw = jnp.concatenate([pe[:, :, half:], pe[:, :, :half]], axis=2)
    zpad = jnp.zeros((rq, MLA_HEADS, LANE - MLA_ROPE), w_uq.dtype)
    flat = lambda t: t.reshape(rq, -1).astype(BF16)
    wa_pad = jnp.concatenate([w_a2, jnp.zeros((LANE - rank, gqk), w_a2.dtype)], axis=0)
    return dict(
        col=col, dk=gqk // GLA_HEADS, dv=g_gla_out.shape[0],
        g_mix=g_mix, w_in_t=jnp.swapaxes(w_in, 0, 1), in_rows=in_rows,
        wa_pad=wa_pad.astype(BF16), b_a=b_a, g_gla_out=g_gla_out, g_q=g_q,
        wq_nope=flat(w3[:, :, :MLA_NOPE]),
        wq_pe=flat(jnp.concatenate([pe, zpad], axis=2)),
        wq_pe_sw=flat(jnp.concatenate([pe_sw, zpad], axis=2)),
        g_kv=g_kv, w_uk=w_uk.astype(BF16), w_uv=w_uv.astype(BF16),
        w_uv_t=w_uv.T.astype(BF16),
        w_uk_t3=w_uk.reshape(rk, MLA_HEADS, MLA_NOPE).transpose(1, 2, 0).astype(BF16),
        w_uv3=w_uv.reshape(rk, MLA_HEADS, MLA_V).transpose(1, 0, 2).astype(BF16),
        w_o=w_o.astype(BF16),
        g_ffn=g_ffn, w_up=w_up, conv_w=conv_w, conv_b=conv_b,
        w_down=w_down, final_norm=final_norm)


def kernel(x_prompt, x_sample, cache_mla_latent, cache_mla_krope, state_gla, cache_ffn_conv,
           meta_tokens, g_mix, w_in, w_a2, b_a, g_gla_out, g_q, w_uq, g_kv, w_uk, w_uv, w_o,
           g_ffn, w_up, conv_w, conv_b, w_down, final_norm):
    assert w_in.shape[0] == 1, "single trunk layer"
    bp, seq, d = x_prompt.shape
    assert bp == 1
    bs, ts, _ = x_sample.shape
    P = cache_mla_latent.shape[2]
    w = _prep_weights(g_mix[0], w_in[0], w_a2[0], b_a[0], g_gla_out[0], g_q[0], w_uq[0],
                      g_kv[0], w_uk[0], w_uv[0], w_o[0], g_ffn[0], w_up[0], conv_w[0],
                      conv_b[0], w_down[0], final_norm)

    n_meta = meta_tokens.shape[0]
    assert n_meta == N_META == ts and seq % CHUNK == 0
    x_short = jnp.concatenate([x_sample.reshape(bs * ts, d), meta_tokens.astype(F32)], axis=0)
    pos_short = jnp.concatenate([jnp.tile(P + jnp.arange(ts, dtype=jnp.int32), bs),
                                 jnp.arange(n_meta, dtype=jnp.int32)])
    pos_long = n_meta + jnp.arange(seq, dtype=jnp.int32)
    pr_short = _project(x_short, pos_short, w)
    pr_long = _project(x_prompt[0], pos_long, w)
    (ys, lat_s, kr_s, st_s, cv_s), (lat_m, kr_m, st_m, cv_m, prefix) = _short_streams(
        x_short, pr_short, w, B=bs, T=ts, past_lat=cache_mla_latent[0],
        past_kr=cache_mla_krope[0], s0_s=state_gla[0], hist_s=cache_ffn_conv[0])
    yp, lat_p, kr_p, st_p, cv_p = _long_stream(
        x_prompt[0], pr_long, w, T=seq, s0=st_m, hist=cv_m, prefix=prefix)

    rk = lat_p.shape[1]
    T = n_meta + seq
    return (yp,
            ys,
            jnp.concatenate([lat_m, lat_p], axis=0).reshape(1, 1, T, rk),
            jnp.concatenate([kr_m, kr_p], axis=0)[:, :MLA_ROPE].reshape(1, 1, T, MLA_ROPE),
            st_p[None],
            cv_p[None],
            lat_s.reshape(1, bs, ts, rk),
            kr_s[:, :MLA_ROPE].reshape(1, bs, ts, MLA_ROPE),
            st_s[None],
            cv_s[None])
```

```python
import functools

import jax
import jax.numpy as jnp
from jax import lax
from jax.experimental import pallas as pl
from jax.experimental.pallas import tpu as pltpu

BF16 = jnp.bfloat16
F32 = jnp.float32

CHUNK = 64
CHUNK_SHIFT = 6
N_META = 16
EPS = 1e-6
GLA_HEADS = 4
GLA_GATE_NORM = 16.0
GLA_LOG_ALPHA_MIN = -5.0
MLA_HEADS = 16
MLA_NOPE = 128
MLA_ROPE = 64
MLA_V = 128
ROPE_THETA = 10000.0
CONV_W = 3
NEG_BIG = -1e30
LOG2E = 1.4426950408889634
QK_SCALE_LOG2E = (MLA_NOPE + MLA_ROPE) ** -0.5 * LOG2E

LANE = 128
VT_ONES = 16
GLA_CHUNK = 256
GLA_SEQS = 4
MASK_LANE0 = MLA_ROPE + 1
SAFE_EXP = 64.0
ROW_TILE = 1024
VMEM_LIMIT = 56 * 1024 * 1024


def _cparams(sem, vmem=VMEM_LIMIT):
    return pltpu.CompilerParams(dimension_semantics=sem, vmem_limit_bytes=vmem)


def _rmsnorm(x, g):
    return x * lax.rsqrt(jnp.mean(x * x, axis=-1, keepdims=True) + EPS) * g


def _sigmoid(x):
    return 0.5 * jnp.tanh(0.5 * x) + 0.5


def _pick(n, cands):
    for c in cands:
        if n % c == 0:
            return c
    fits = [t for t in range(16, min(n, max(cands)) + 1, 16) if n % t == 0]
    if not fits:
        raise ValueError(f"no tile in {cands} divides {n}")
    return fits[-1]


_NT = (((1,), (1,)), ((), ()))


def _matmul_wt_kernel(a_ref, w_ref, o_ref, w_scr):
    @pl.when(pl.program_id(1) == 0)
    def _():
        w_scr[...] = w_ref[...].astype(BF16)

    o_ref[...] = lax.dot_general(a_ref[...], w_scr[...], _NT,
                                 preferred_element_type=F32).astype(o_ref.dtype)


def _matmul_wt(a, w_t, row0, n, out_dtype, tn):
    m, k = a.shape
    tm = _pick(m, (ROW_TILE, 512, 384, 128))
    return pl.pallas_call(
        _matmul_wt_kernel,
        grid=(n // tn, m // tm),
        in_specs=[pl.BlockSpec((tm, k), lambda j, i: (i, 0)),
                  pl.BlockSpec((pl.Element(tn), pl.Element(k)),
                               lambda j, i: (pl.multiple_of(row0 + j * tn, 16), 0))],
        out_specs=pl.BlockSpec((tm, tn), lambda j, i: (i, j)),
        out_shape=jax.ShapeDtypeStruct((m, n), out_dtype),
        scratch_shapes=[pltpu.VMEM((tn, k), BF16)],
        compiler_params=_cparams(("parallel", "arbitrary")),
        name="in_proj_wt",
    )(a, w_t)


def _front_kernel(x_ref, g_ref, w_ref, gkv_ref, cos_ref, sin_ref,
                  h_ref, o_ref, lat_ref, kr_ref, w_scr, *, rank, rq, rk):
    @pl.when(pl.program_id(0) == 0)
    def _():
        w = w_ref[...].astype(BF16)
        half = MLA_ROPE // 2
        pe0 = rank + rq + rk
        o_pe = rq + rk
        w_scr[0:rq] = w[rank:rank + rq]
        w_scr[rq:o_pe] = w[rank + rq:pe0]
        w_scr[o_pe:o_pe + MLA_ROPE] = w[pe0:pe0 + MLA_ROPE]
        w_scr[o_pe + MLA_ROPE:o_pe + MLA_ROPE + half] = w[pe0 + half:pe0 + MLA_ROPE]
        w_scr[o_pe + MLA_ROPE + half:o_pe + 2 * MLA_ROPE] = w[pe0:pe0 + half]
        o_a = o_pe + 2 * MLA_ROPE
        w_scr[o_a:o_a + rank] = w[0:rank]
        w_scr[o_a + rank:] = jnp.zeros((w_scr.shape[0] - o_a - rank, w_scr.shape[1]), BF16)

    h = _rmsnorm(x_ref[...], g_ref[...]).astype(BF16)
    h_ref[...] = h
    small = lax.dot_general(h, w_scr[...], _NT, preferred_element_type=F32)
    o_ref[...] = small
    lat_ref[...] = _rmsnorm(small[:, rq:rq + rk], gkv_ref[...])
    blk = small[:, rq + rk:rq + rk + LANE]
    kr_ref[...] = blk * cos_ref[...] + pltpu.roll(blk, LANE // 2, 1) * sin_ref[...]


def _front(x, g_mix, w_t, row0, g_kv, cos_t, sin_t, *, rank, rq, rk):
    m, k = x.shape
    n_in = rank + rq + rk + MLA_ROPE
    n_out = rq + rk + 2 * MLA_ROPE + LANE
    tm = _pick(m, (512, 384, 128))
    kern = functools.partial(_front_kernel, rank=rank, rq=rq, rk=rk)
    row = lambda i: (i, 0)
    return pl.pallas_call(
        kern,
        grid=(m // tm,),
        in_specs=[pl.BlockSpec((tm, k), row),
                  pl.BlockSpec((1, k), lambda i: (0, 0)),
                  pl.BlockSpec((pl.Element(n_in), pl.Element(k)), lambda i: (row0, 0),
                               pipeline_mode=pl.Buffered(1)),
                  pl.BlockSpec((1, rk), lambda i: (0, 0)),
                  pl.BlockSpec((tm, LANE), row),
                  pl.BlockSpec((tm, LANE), row)],
        out_specs=[pl.BlockSpec((tm, k), row),
                   pl.BlockSpec((tm, n_out), row),
                   pl.BlockSpec((tm, rk), row),
                   pl.BlockSpec((tm, LANE), row)],
        out_shape=[jax.ShapeDtypeStruct((m, k), BF16),
                   jax.ShapeDtypeStruct((m, n_out), F32),
                   jax.ShapeDtypeStruct((m, rk), F32),
                   jax.ShapeDtypeStruct((m, LANE), F32)],
        scratch_shapes=[pltpu.VMEM((n_out, k), BF16)],
        compiler_params=_cparams(("arbitrary",)),
        name="front_proj",
    )(x, g_mix.reshape(1, -1), w_t, g_kv.reshape(1, -1), cos_t, sin_t)


def _split3(x):
    a = x.astype(BF16)
    r1 = x - a.astype(F32)
    b = r1.astype(BF16)
    c = (r1 - b.astype(F32)).astype(BF16)
    return a, b, c


def _gla_kernel(q_ref, k_ref, v_ref, r_ref, ga_ref, a_ref, wa_ref, ba_ref, go_ref, s0_ref,
                o_ref, sout_ref, s_scr, *, C, SB, T, H, dk, dv, S):
    c_idx = pl.program_id(1)
    n_chunks = pl.num_programs(1)
    R = S * C

    @pl.when(c_idx == 0)
    def _():
        s_scr[...] = s0_ref[...]

    z = jnp.dot(a_ref[...].astype(BF16), wa_ref[...], preferred_element_type=F32) + ba_ref[...]
    log_sig = jnp.minimum(z, 0.0) - jnp.log(1.0 + jnp.exp(-jnp.abs(z)))
    la = jnp.maximum(log_sig * (1.0 / GLA_GATE_NORM), GLA_LOG_ALPHA_MIN)
    if T % C:
        rows = c_idx * C + lax.broadcasted_iota(jnp.int32, (C, 1), 0)
        la = jnp.where(rows < T, la, 0.0)

    ri = lax.broadcasted_iota(jnp.int32, (R, R), 0)
    ci = lax.broadcasted_iota(jnp.int32, (R, R), 1)
    same_seq = (ri >= ci) if S == 1 else ((ri >= ci) & (ri - ci <= lax.rem(ri, C)))
    tri = jnp.where(same_seq, 1.0, 0.0).astype(BF16)
    ones = jnp.ones((C, LANE), BF16)
    cs_all = jnp.zeros_like(la)
    dsum_all = [jnp.zeros((la.shape[1], LANE), F32) for _ in range(S)]
    for piece in _split3(la):
        cs_all = cs_all + jnp.dot(tri, piece, preferred_element_type=F32)
        for si in range(S):
            dsum_all[si] = dsum_all[si] + lax.dot_general(
                piece[si * C:(si + 1) * C], ones, (((0,), (0,)), ((), ())),
                preferred_element_type=F32)

    sr = lax.broadcasted_iota(jnp.int32, (SB, SB), 0)
    sc = lax.broadcasted_iota(jnp.int32, (SB, SB), 1)
    causal = sr >= sc
    nt = (((1,), (1,)), ((), ()))
    scale = dk ** -0.5

    for si, h in [(si, h) for si in range(S) for h in range(H)]:
        rs = slice(si * C, (si + 1) * C)
        ksl = slice(h * dk, (h + 1) * dk)
        vsl = slice(h * dv, (h + 1) * dv)
        cs = cs_all[rs, ksl]
        c_last = cs[C - 1:C, :]
        q = q_ref[rs, ksl].astype(F32) * scale
        k = k_ref[rs, ksl].astype(F32)
        v = v_ref[rs, vsl]
        s_old = s_scr[si, h]

        o_inter = jnp.dot((q * jnp.exp(cs)).astype(BF16), s_old.astype(BF16),
                          preferred_element_type=F32)
        k_end = (k * jnp.exp(c_last - cs)).astype(BF16)
        upd = lax.dot_general(k_end, v, (((0,), (0,)), ((), ())), preferred_element_type=F32)
        dcol = jnp.exp(dsum_all[si][ksl, :])
        s_scr[si, h] = jnp.concatenate([dcol] * (dv // LANE), axis=1) * s_old + upd

        outs = []
        for i in range(C // SB):
            lo = i * SB
            cs_i = cs[lo:lo + SB]
            q_i = q[lo:lo + SB]
            k_i = k[lo:lo + SB]
            start = cs[lo - 1:lo] if i > 0 else jnp.zeros_like(c_last)
            mid = 0.5 * (start + cs[lo + SB - 1:lo + SB])
            qd = (q_i * jnp.exp(cs_i - mid)).astype(BF16)
            kd = (k_i * jnp.exp(mid - cs_i)).astype(BF16)
            att = lax.dot_general(qd, kd, nt, preferred_element_type=F32)
            att = jnp.where(causal, att, 0.0)
            o_i = jnp.dot(att.astype(BF16), v[lo:lo + SB], preferred_element_type=F32)
            if i > 0:
                qo = (q_i * jnp.exp(cs_i - start)).astype(BF16)
                ko = (k[:lo] * jnp.exp(start - cs[:lo])).astype(BF16)
                att_o = lax.dot_general(qo, ko, nt, preferred_element_type=F32)
                o_i = o_i + jnp.dot(att_o.astype(BF16), v[:lo], preferred_element_type=F32)
            outs.append(o_i)
        o = o_inter + (jnp.concatenate(outs, axis=0) if len(outs) > 1 else outs[0])

        on = _rmsnorm(o, go_ref[...])
        r = r_ref[rs, vsl].astype(F32)
        g = ga_ref[rs, vsl].astype(F32)
        o_ref[rs, vsl] = (_sigmoid(g) * (on * (r * _sigmoid(r)))).astype(o_ref.dtype)

    @pl.when(c_idx == n_chunks - 1)
    def _():
        sout_ref[...] = s_scr[...]


def _gla(qkvr, gates, small, wa_pad, b_a, g_out, s0, *, B, T, Tp, dk, dv, col, row0=0):
    C = min(GLA_CHUNK, Tp)
    SB = min(32, C)
    assert Tp % C == 0 and C % SB == 0
    nc = Tp // C
    H = GLA_HEADS
    qk, vw = H * dk, H * dv
    S = _pick(B, (GLA_SEQS, 1)) if nc == 1 else 1
    R = S * C
    assert row0 % R == 0
    rb = lambda b, c: row0 // R + b * nc + c
    kern = functools.partial(_gla_kernel, C=C, SB=SB, T=T, H=H, dk=dk, dv=dv, S=S)
    return pl.pallas_call(
        kern,
        grid=(B // S, nc),
        in_specs=[
            pl.BlockSpec((R, qk), lambda b, c: (rb(b, c), col["q"] // qk)),
            pl.BlockSpec((R, qk), lambda b, c: (rb(b, c), col["k"] // qk)),
            pl.BlockSpec((R, vw), lambda b, c: (rb(b, c), col["v"] // vw)),
            pl.BlockSpec((R, vw), lambda b, c: (rb(b, c), col["r"] // vw)),
            pl.BlockSpec((R, vw), lambda b, c: (rb(b, c), col["ga"] // vw)),
            pl.BlockSpec((R, LANE), lambda b, c: (rb(b, c), col["a"] // LANE)),
            pl.BlockSpec((LANE, qk), lambda b, c: (0, 0)),
            pl.BlockSpec((1, qk), lambda b, c: (0, 0)),
            pl.BlockSpec((1, dv), lambda b, c: (0, 0)),
            pl.BlockSpec((S, H, dk, dv), lambda b, c: (b, 0, 0, 0)),
        ],
        out_specs=[
            pl.BlockSpec((R, vw), lambda b, c: (b * nc + c, 0)),
            pl.BlockSpec((S, H, dk, dv), lambda b, c: (b, 0, 0, 0)),
        ],
        out_shape=[jax.ShapeDtypeStruct((B * Tp, vw), BF16),
                   jax.ShapeDtypeStruct((B, H, dk, dv), F32)],
        scratch_shapes=[pltpu.VMEM((S, H, dk, dv), F32)],
        compiler_params=_cparams(("parallel", "arbitrary")),
        name="gla",
    )(qkvr, qkvr, qkvr, qkvr, gates, small, wa_pad, b_a.reshape(1, -1), g_out.reshape(1, -1), s0)


def _qprep_kernel(cq_ref, gq_ref, wn_ref, wp_ref, wps_ref, cos_ref, sin_ref, q_ref, *,
                  chunk_tile):
    hq = _rmsnorm(cq_ref[...], gq_ref[...]).astype(BF16)
    qn = jnp.dot(hq, wn_ref[...], preferred_element_type=F32)
    qp = jnp.dot(hq, wp_ref[...], preferred_element_type=F32)
    qs = jnp.dot(hq, wps_ref[...], preferred_element_type=F32)
    cos = cos_ref[...] * QK_SCALE_LOG2E
    sin = sin_ref[...] * QK_SCALE_LOG2E
    tag = 0.0
    if chunk_tile:
        tm = cos.shape[0]
        row = pl.program_id(0) * tm + lax.broadcasted_iota(jnp.int32, (tm, LANE), 0)
        lane = lax.broadcasted_iota(jnp.int32, (tm, LANE), 1)
        chunk = (row & (chunk_tile - 1)) >> CHUNK_SHIFT
        tag = jnp.where(lane - MASK_LANE0 == chunk, 1.0, 0.0)
    for h in range(MLA_HEADS):
        sl = slice(h * LANE, (h + 1) * LANE)
        q_ref[h, :, 0:LANE] = (qn[:, sl] * QK_SCALE_LOG2E).astype(BF16)
        q_ref[h, :, LANE:2 * LANE] = (qp[:, sl] * cos + qs[:, sl] * sin + tag).astype(BF16)


def _qprep(small, g_q, wn, wp, wps, cos_t, sin_t, *, col, chunk_tile=0):
    m = small.shape[0]
    rq = wn.shape[0]
    tm = _pick(m, (512, 256, 128))
    full = lambda i: (0, 0)
    return pl.pallas_call(
        functools.partial(_qprep_kernel, chunk_tile=chunk_tile),
        grid=(m // tm,),
        in_specs=[pl.BlockSpec((tm, rq), lambda i: (i, col["cq"] // rq)),
                  pl.BlockSpec((1, rq), full),
                  pl.BlockSpec(wn.shape, full),
                  pl.BlockSpec(wp.shape, full),
                  pl.BlockSpec(wps.shape, full),
                  pl.BlockSpec((tm, LANE), lambda i: (i, 0)),
                  pl.BlockSpec((tm, LANE), lambda i: (i, 0))],
        out_specs=pl.BlockSpec((MLA_HEADS, tm, 2 * LANE), lambda i: (0, i, 0)),
        out_shape=jax.ShapeDtypeStruct((MLA_HEADS, m, 2 * LANE), BF16),
        compiler_params=_cparams(("parallel",)),
        name="mla_q",
    )(small, g_q.reshape(1, -1), wn, wp, wps, cos_t, sin_t)


def _kvup_kernel(lat_ref, kr_ref, wuk_ref, wuv_ref, k_ref, v_ref, *, v_transposed):
    lat = lat_ref[...].astype(BF16)
    kn = jnp.dot(lat, wuk_ref[...], preferred_element_type=F32)
    kr = kr_ref[...]
    lane = lax.broadcasted_iota(jnp.int32, kr.shape, 1)
    kp = jnp.where(lane == MLA_ROPE, 1.0, kr).astype(BF16)
    if v_transposed:
        vv = lax.dot_general(wuv_ref[...], lat, (((1,), (1,)), ((), ())),
                             preferred_element_type=F32)
    else:
        vv = jnp.dot(lat, wuv_ref[...], preferred_element_type=F32)
    for h in range(MLA_HEADS):
        sl = slice(h * LANE, (h + 1) * LANE)
        k_ref[h, :, 0:LANE] = kn[:, sl].astype(BF16)
        k_ref[h, :, LANE:2 * LANE] = kp
        if v_transposed:
            v_ref[h, 0:LANE, :] = vv[sl, :].astype(BF16)
            v_ref[h, LANE:LANE + VT_ONES, :] = jnp.ones((VT_ONES, vv.shape[1]), BF16)
        else:
            v_ref[h] = vv[:, sl].astype(BF16)


def _kvup(lat, kr, wuk, wuv, *, v_transposed=False):
    m, rk = lat.shape
    tm = _pick(m, (512, 256, 128))
    full = lambda i: (0, 0)
    if v_transposed:
        v_spec = pl.BlockSpec((MLA_HEADS, LANE + VT_ONES, tm), lambda i: (0, 0, i))
        v_shape = (MLA_HEADS, LANE + VT_ONES, m)
    else:
        v_spec = pl.BlockSpec((MLA_HEADS, tm, LANE), lambda i: (0, i, 0))
        v_shape = (MLA_HEADS, m, LANE)
    return pl.pallas_call(
        functools.partial(_kvup_kernel, v_transposed=v_transposed),
        grid=(m // tm,),
        in_specs=[pl.BlockSpec((tm, rk), lambda i: (i, 0)),
                  pl.BlockSpec((tm, LANE), lambda i: (i, 0)),
                  pl.BlockSpec(wuk.shape, full),
                  pl.BlockSpec(wuv.shape, full)],
        out_specs=[pl.BlockSpec((MLA_HEADS, tm, 2 * LANE), lambda i: (0, i, 0)), v_spec],
        out_shape=[jax.ShapeDtypeStruct((MLA_HEADS, m, 2 * LANE), BF16),
                   jax.ShapeDtypeStruct(v_shape, BF16)],
        compiler_params=_cparams(("parallel",)),
        name="mla_kv",
    )(lat, kr, wuk, wuv)


def _last_kblock(qi, *, tq, tk, nk, q_off, k_off):
    top_chunk = ((qi + 1) * tq - 1 + q_off) // CHUNK
    last_key = (top_chunk + 1) * CHUNK - 1 - k_off
    return jnp.minimum(last_key // tk, nk - 1)


def _attn_kernel(q_ref, k_ref, v_ref, o_ref, m_scr, l_scr, acc_scr, *, hps, tq, tk, nk,
                 q_off, k_off):
    qi = pl.program_id(2)
    ki = pl.program_id(3)

    @pl.when(ki == 0)
    def _():
        m_scr[...] = jnp.full(m_scr.shape, NEG_BIG, F32)
        l_scr[...] = jnp.zeros(l_scr.shape, F32)
        acc_scr[...] = jnp.zeros(acc_scr.shape, F32)

    @pl.when(ki <= _last_kblock(qi, tq=tq, tk=tk, nk=nk, q_off=q_off, k_off=k_off))
    def _():
        q_chunk = (qi * tq + q_off + lax.broadcasted_iota(jnp.int32, (tq, 1), 0)) >> CHUNK_SHIFT
        k_chunk = (ki * tk + k_off + lax.broadcasted_iota(jnp.int32, (1, tk), 1)) >> CHUNK_SHIFT
        visible = q_chunk >= k_chunk

        def head(h, carry):
            s = lax.dot_general(q_ref[h], k_ref[h], (((1,), (1,)), ((), ())),
                                preferred_element_type=F32)
            s = jnp.where(visible, s, NEG_BIG)
            m_prev = m_scr[h]
            m_new = jnp.maximum(m_prev, jnp.max(s, axis=-1, keepdims=True))
            p = jnp.exp2(s - m_new)
            alpha = jnp.exp2(m_prev - m_new)
            l_scr[h] = alpha * l_scr[h] + jnp.sum(p, axis=-1, keepdims=True)
            acc_scr[h] = alpha * acc_scr[h] + jnp.dot(p.astype(BF16), v_ref[h],
                                                      preferred_element_type=F32)
            m_scr[h] = m_new
            return carry

        lax.fori_loop(0, hps, head, 0)

    @pl.when(ki == nk - 1)
    def _():
        for h in range(hps):
            o_ref[:, h * LANE:(h + 1) * LANE] = (acc_scr[h] / l_scr[h]).astype(o_ref.dtype)


def _attention(q, k, v, *, B, Tq, Tk, tq, tk, hps, q_off, k_off):
    nq = Tq // tq
    nk = Tk // tk
    hg = MLA_HEADS // hps
    dqk = q.shape[2]
    dvh = v.shape[2]
    last = functools.partial(_last_kblock, tq=tq, tk=tk, nk=nk, q_off=q_off, k_off=k_off)
    kern = functools.partial(_attn_kernel, hps=hps, tq=tq, tk=tk, nk=nk, q_off=q_off,
                             k_off=k_off)
    kv_row = lambda b, g, i, j: b * nk + jnp.minimum(j, last(i))
    return pl.pallas_call(
        kern,
        grid=(B, hg, nq, nk),
        in_specs=[pl.BlockSpec((hps, tq, dqk), lambda b, g, i, j: (g, b * nq + i, 0)),
                  pl.BlockSpec((hps, tk, dqk), lambda b, g, i, j: (g, kv_row(b, g, i, j), 0)),
                  pl.BlockSpec((hps, tk, dvh), lambda b, g, i, j: (g, kv_row(b, g, i, j), 0))],
        out_specs=pl.BlockSpec((tq, hps * dvh), lambda b, g, i, j: (b * nq + i, g)),
        out_shape=jax.ShapeDtypeStruct((B * Tq, MLA_HEADS * dvh), BF16),
        scratch_shapes=[pltpu.VMEM((hps, tq, 1), F32),
                        pltpu.VMEM((hps, tq, 1), F32),
                        pltpu.VMEM((hps, tq, dvh), F32)],
        compiler_params=_cparams(("parallel", "parallel", "parallel", "arbitrary")),
        name="mla_attn",
    )(q, k, v)


def _attn_t_kernel(qi_ref, ki_ref, q_ref, k_ref, vt_ref, kp_ref, vtp_ref, o_ref,
                   q_scr, r_scr, acc_scr, *, hps, t):
    pair = pl.program_id(1)
    qi = qi_ref[pair]
    ki = ki_ref[pair]
    nt = (((1,), (1,)), ((), ()))
    pe = slice(LANE, 2 * LANE)
    lane = lax.broadcasted_iota(jnp.int32, (t, LANE), 1)

    def set_reference(h, r):
        neg_r = jnp.transpose(jnp.broadcast_to(-r, (LANE, t)))
        q_scr[h, :, pe] = jnp.where(lane == MLA_ROPE, neg_r.astype(BF16), q_ref[h, :, pe])
        r_scr[h] = r

    def shifted_scores(h, own_tile=False):
        k = k_ref[h]
        if own_tile:
            ahead = lane - MASK_LANE0
            k_chunk = lax.broadcasted_iota(jnp.int32, (t, LANE), 0) >> CHUNK_SHIFT
            hidden = (ahead >= 0) & (ahead < k_chunk)
            k = jnp.concatenate(
                [k[:, 0:LANE], jnp.where(hidden, jnp.asarray(NEG_BIG, BF16), k[:, pe])], axis=1)
        return lax.dot_general(k, q_scr[h], nt, preferred_element_type=F32)

    @pl.when(ki == 0)
    def _():
        for h in range(hps):
            q_scr[h, :, 0:LANE] = q_ref[h, :, 0:LANE]
            s = lax.dot_general(kp_ref[h], q_ref[h], nt, preferred_element_type=F32)
            r = jnp.max(s, axis=0, keepdims=True).astype(BF16).astype(F32)
            p = jnp.exp2((s - r).astype(BF16))
            acc_scr[h] = jnp.dot(vtp_ref[h], p, preferred_element_type=F32)
            set_reference(h, r)

    def general(h, own_tile):
        sp = shifted_scores(h, own_tile)
        r = r_scr[h]
        rise = jnp.maximum(jnp.max(sp, axis=0, keepdims=True), 0.0)
        r_new = (r + rise).astype(BF16).astype(F32)
        delta = r_new - r
        p = jnp.exp2((sp - delta).astype(BF16))
        acc_scr[h] = jnp.exp2(-delta) * acc_scr[h] + jnp.dot(vt_ref[h], p,
                                                               preferred_element_type=F32)
        if not own_tile:
            set_reference(h, r_new)

    def tile_step(own_tile):
        unsafe = []
        sp_next = shifted_scores(0, own_tile)
        for h in range(hps):
            sp = sp_next
            if h + 1 < hps:
                sp_next = shifted_scores(h + 1, own_tile)
            safe = jnp.max(sp) <= SAFE_EXP
            part = jnp.dot(vt_ref[h], jnp.exp2(sp.astype(BF16)), preferred_element_type=F32)
            acc_scr[h] += jnp.where(safe, part, 0.0)
            unsafe.append(jnp.logical_not(safe))

        @pl.when(functools.reduce(jnp.logical_or, unsafe))
        def _():
            for h in range(hps):
                @pl.when(unsafe[h])
                def _():
                    general(h, own_tile)

    @pl.when(ki < qi)
    def _():
        tile_step(False)

    @pl.when(ki == qi)
    def _():
        tile_step(True)
        for h in range(hps):
            acc = acc_scr[h]
            o_t = acc[0:LANE] / acc[LANE:LANE + 1]
            o_ref[:, h * LANE:(h + 1) * LANE] = o_t.T.astype(o_ref.dtype)


def _attention_t(q, k, vt, k_pre, vt_pre, *, T, t, hps):
    n = T // t
    hg = MLA_HEADS // hps
    dqk = q.shape[2]
    npre = k_pre.shape[1]
    vrows = vt.shape[1]
    pairs = [(i, j) for i in range(n) for j in range(i + 1)]
    qi_arr = jnp.asarray([p[0] for p in pairs], jnp.int32)
    ki_arr = jnp.asarray([p[1] for p in pairs], jnp.int32)
    kern = functools.partial(_attn_t_kernel, hps=hps, t=t)
    grid_spec = pltpu.PrefetchScalarGridSpec(
        num_scalar_prefetch=2,
        grid=(hg, len(pairs)),
        in_specs=[pl.BlockSpec((hps, t, dqk), lambda g, p, qi, ki: (g, qi[p], 0)),
                  pl.BlockSpec((hps, t, dqk), lambda g, p, qi, ki: (g, ki[p], 0)),
                  pl.BlockSpec((hps, vrows, t), lambda g, p, qi, ki: (g, 0, ki[p])),
                  pl.BlockSpec((hps, npre, dqk), lambda g, p, qi, ki: (g, 0, 0)),
                  pl.BlockSpec((hps, vrows, npre), lambda g, p, qi, ki: (g, 0, 0))],
        out_specs=pl.BlockSpec((t, hps * LANE), lambda g, p, qi, ki: (qi[p], g)),
        scratch_shapes=[pltpu.VMEM((hps, t, dqk), BF16),
                        pltpu.VMEM((hps, 1, t), F32),
                        pltpu.VMEM((hps, vrows, t), F32)])
    return pl.pallas_call(
        kern,
        grid_spec=grid_spec,
        out_shape=jax.ShapeDtypeStruct((T, MLA_HEADS * LANE), BF16),
        compiler_params=_cparams(("parallel", "arbitrary")),
        name="mla_attn_t",
    )(qi_arr, ki_arr, q, k, vt, k_pre, vt_pre)


def _absorb_q_kernel(q_ref, w_ref, o_ref):
    o_ref[0] = jnp.dot(q_ref[0, :, 0:MLA_NOPE], w_ref[0],
                       preferred_element_type=F32).astype(o_ref.dtype)


def _absorb_q(q, w_uk_t3):
    heads, rows, dqk = q.shape
    rk = w_uk_t3.shape[2]
    return pl.pallas_call(
        _absorb_q_kernel,
        grid=(heads,),
        in_specs=[pl.BlockSpec((1, rows, dqk), lambda h: (h, 0, 0)),
                  pl.BlockSpec((1, MLA_NOPE, rk), lambda h: (h, 0, 0))],
        out_specs=pl.BlockSpec((1, rows, rk), lambda h: (h, 0, 0)),
        out_shape=jax.ShapeDtypeStruct((heads, rows, rk), BF16),
        compiler_params=_cparams(("parallel",)),
        name="mla_absorb_q",
    )(q, w_uk_t3)


def _attn_latent_kernel(ql_ref, q_ref, plat_ref, pkr_ref, lat_ref, kr_ref, o_ref, *, T, P, S):
    heads, _, rk = ql_ref.shape
    rows = heads * T
    nt = (((1,), (1,)), ((), ()))
    tok = lax.rem(lax.broadcasted_iota(jnp.int32, (rows, 1), 0), T)
    q_chunk = (P + tok) >> CHUNK_SHIFT
    k_chunk = lax.broadcasted_iota(jnp.int32, (1, P + T), 1) >> CHUNK_SHIFT
    visible = q_chunk >= k_chunk
    for si in range(S):
        ts = slice(si * T, (si + 1) * T)
        ql = ql_ref[:, ts, :].reshape(rows, rk)
        qpe = q_ref[:, ts, LANE:2 * LANE].reshape(rows, LANE)[:, 0:MLA_ROPE]
        lat_all = jnp.concatenate([plat_ref[si].astype(BF16), lat_ref[ts, :].astype(BF16)],
                                  axis=0)
        s_pe = jnp.concatenate(
            [jnp.dot(qpe, pkr_ref[si].astype(BF16), preferred_element_type=F32),
             lax.dot_general(qpe, kr_ref[ts, 0:MLA_ROPE].astype(BF16), nt,
                             preferred_element_type=F32)], axis=1)
        s = lax.dot_general(ql, lat_all, nt, preferred_element_type=F32) + s_pe
        s = jnp.where(visible, s, NEG_BIG)
        p = jnp.exp2(s - jnp.max(s, axis=-1, keepdims=True))
        o = jnp.dot(p.astype(BF16), lat_all, preferred_element_type=F32)
        o = o / jnp.sum(p, axis=-1, keepdims=True)
        o_ref[:, ts, :] = o.reshape(heads, T, rk).astype(o_ref.dtype)


def _attn_latent(qlat, q, past_lat, past_kr_t, lat, kr, *, B, T):
    heads, _, rk = qlat.shape
    P = past_lat.shape[1]
    S = _pick(B, (2, 1))
    kern = functools.partial(_attn_latent_kernel, T=T, P=P, S=S)
    return pl.pallas_call(
        kern,
        grid=(B // S,),
        in_specs=[pl.BlockSpec((heads, S * T, rk), lambda b: (0, b, 0)),
                  pl.BlockSpec((heads, S * T, q.shape[2]), lambda b: (0, b, 0)),
                  pl.BlockSpec((S, P, rk), lambda b: (b, 0, 0)),
                  pl.BlockSpec((S, past_kr_t.shape[1], P), lambda b: (b, 0, 0)),
                  pl.BlockSpec((S * T, rk), lambda b: (b, 0)),
                  pl.BlockSpec((S * T, LANE), lambda b: (b, 0))],
        out_specs=pl.BlockSpec((heads, S * T, rk), lambda b: (0, b, 0)),
        out_shape=jax.ShapeDtypeStruct((heads, B * T, rk), BF16),
        compiler_params=_cparams(("parallel",)),
        name="mla_attn_latent",
    )(qlat, q, past_lat, past_kr_t, lat, kr)


def _absorb_out_kernel(o_ref, w_ref, out_ref):
    out_ref[...] = jnp.dot(o_ref[0], w_ref[0], preferred_element_type=F32).astype(out_ref.dtype)


def _absorb_out(olat, w_uv3):
    heads, rows, rk = olat.shape
    dvh = w_uv3.shape[2]
    return pl.pallas_call(
        _absorb_out_kernel,
        grid=(heads,),
        in_specs=[pl.BlockSpec((1, rows, rk), lambda h: (h, 0, 0)),
                  pl.BlockSpec((1, rk, dvh), lambda h: (h, 0, 0))],
        out_specs=pl.BlockSpec((rows, dvh), lambda h: (0, h)),
        out_shape=jax.ShapeDtypeStruct((rows, heads * dvh), BF16),
        compiler_params=_cparams(("parallel",)),
        name="mla_absorb_out",
    )(olat, w_uv3)


def _merge_kernel(a_ref, gb_ref, om_ref, x_ref, wo_ref, gf_ref, x1_ref, h2_ref):
    merged = a_ref[...].astype(F32) + _sigmoid(gb_ref[...].astype(F32)) * om_ref[...].astype(F32)
    x1 = x_ref[...] + jnp.dot(merged.astype(BF16), wo_ref[...], preferred_element_type=F32)
    x1_ref[...] = x1
    h2_ref[...] = _rmsnorm(x1, gf_ref[...]).astype(BF16)


def _merge(branch_a, gates, o_m, x, wo, g_ffn, *, col):
    m, d = x.shape
    tm = _pick(m, (512, 384, 256, 128))
    row = lambda i: (i, 0)
    return pl.pallas_call(
        _merge_kernel,
        grid=(m // tm,),
        in_specs=[pl.BlockSpec((tm, d), row),
                  pl.BlockSpec((tm, d), lambda i: (i, col["gb"] // d)),
                  pl.BlockSpec((tm, d), row),
                  pl.BlockSpec((tm, d), row),
                  pl.BlockSpec(wo.shape, lambda i: (0, 0), pipeline_mode=pl.Buffered(1)),
                  pl.BlockSpec((1, d), lambda i: (0, 0))],
        out_specs=[pl.BlockSpec((tm, d), row), pl.BlockSpec((tm, d), row)],
        out_shape=[jax.ShapeDtypeStruct((m, d), F32), jax.ShapeDtypeStruct((m, d), BF16)],
        compiler_params=_cparams(("parallel",)),
        name="merge_out_proj",
    )(branch_a, gates, o_m, x, wo, g_ffn.reshape(1, -1))


HALO = 8


def _ffn_up_kernel(*refs, bb, r, tf, loc, carried, cast_down):
    (h_ref, wa_ref, wb_ref, cwa_ref, cwb_ref, cba_ref, cbb_ref, ha_ref, hb_ref), refs = \
        refs[:9], refs[9:]
    if cast_down:
        wd_ref, act_ref, ca_ref, cb_ref, wdb_ref, ext_scr, carry_scr, w_scr = refs
    else:
        act_ref, ca_ref, cb_ref, ext_scr, carry_scr, w_scr = refs
    s = pl.program_id(1)
    rt = pl.program_id(2)
    d = h_ref.shape[2]

    @pl.when((s == 0) & (rt == 0))
    def _():
        w_scr[0] = wa_ref[...].astype(BF16)
        w_scr[1] = wb_ref[...].astype(BF16)
        if cast_down:
            wdb_ref[...] = wd_ref[...].astype(BF16)

    if carried:
        @pl.when(rt == 0)
        def _():
            carry_scr[0] = ha_ref[...]
            carry_scr[1] = hb_ref[...]

    h = h_ref[...].reshape(bb * r, d)
    conv = []
    for half, (cw_ref, cbias_ref, hist_ref, cout_ref) in enumerate(
            ((cwa_ref, cba_ref, ha_ref, ca_ref), (cwb_ref, cbb_ref, hb_ref, cb_ref))):
        u = jnp.dot(h, w_scr[half], preferred_element_type=F32).reshape(bb, r, tf)
        ext_scr[half, :, HALO:HALO + r, :] = u
        ext_scr[half, :, HALO - 2:HALO, :] = carry_scr[half] if carried else hist_ref[...]
        u1 = ext_scr[half, :, HALO - 1:HALO - 1 + r, :]
        u2 = ext_scr[half, :, HALO - 2:HALO - 2 + r, :]
        cw = cw_ref[...]
        conv.append(cbias_ref[...] + cw[0:1] * u2 + cw[1:2] * u1 + cw[2:3] * u)
        if carried:
            carry_scr[half] = ext_scr[half, :, HALO + r - 2:HALO + r, :]
        cout_ref[0] = ext_scr[half, :, HALO + loc:HALO + loc + 2, :]

    act_ref[...] = ((conv[0] * _sigmoid(conv[0])) * conv[1]).astype(act_ref.dtype)


def _ffn_down_kernel(act_ref, wd_ref, x1_ref, gf_ref, y_ref):
    down = jnp.dot(act_ref[...], wd_ref[...], preferred_element_type=F32)
    y_ref[...] = _rmsnorm(x1_ref[...] + down, gf_ref[...])


def _ffn(h2, x1, w_up, w_down, conv_w, conv_b, hist, g_final, *, B, T, Tp):
    d = h2.shape[1]
    dff = w_down.shape[0]
    cast_down = w_down.dtype != BF16
    tf = _pick(dff, (512, 256, 128))
    nf = dff // tf
    if Tp <= 128:
        bb, r = B, Tp
    else:
        bb, r = 1, _pick(Tp, (ROW_TILE, 128))
    nrt = Tp // r
    carried = nrt > 1
    loc = (T - 2) - (nrt - 1) * r
    assert 0 <= loc <= r - 2, "final two valid rows must sit in the last row tile"
    kern = functools.partial(_ffn_up_kernel, bb=bb, r=r, tf=tf, loc=loc, carried=carried,
                             cast_down=cast_down)
    carry_shape = (2, bb, 2, tf) if carried else (1, 1, 2, LANE)
    in_specs = [pl.BlockSpec((bb, r, d), lambda f, s, t: (s, t, 0)),
                pl.BlockSpec((d, tf), lambda f, s, t: (0, f)),
                pl.BlockSpec((d, tf), lambda f, s, t: (0, nf + f)),
                pl.BlockSpec((CONV_W, tf), lambda f, s, t: (0, f)),
                pl.BlockSpec((CONV_W, tf), lambda f, s, t: (0, nf + f)),
                pl.BlockSpec((1, tf), lambda f, s, t: (0, f)),
                pl.BlockSpec((1, tf), lambda f, s, t: (0, nf + f)),
                pl.BlockSpec((bb, 2, tf), lambda f, s, t: (s, 0, f)),
                pl.BlockSpec((bb, 2, tf), lambda f, s, t: (s, 0, nf + f))]
    out_specs = [pl.BlockSpec((bb, r, tf), lambda f, s, t: (s, t, f)),
                 pl.BlockSpec((1, bb, 2, tf), lambda f, s, t: (t, s, 0, f)),
                 pl.BlockSpec((1, bb, 2, tf), lambda f, s, t: (t, s, 0, f))]
    out_shape = [jax.ShapeDtypeStruct((B, Tp, dff), BF16),
                 jax.ShapeDtypeStruct((nrt, B, 2, dff), F32),
                 jax.ShapeDtypeStruct((nrt, B, 2, dff), F32)]
    args = [h2.reshape(B, Tp, d), w_up, w_up, conv_w, conv_w, conv_b.reshape(1, -1),
            conv_b.reshape(1, -1), hist, hist]
    if cast_down:
        in_specs.append(pl.BlockSpec((tf, d), lambda f, s, t: (f, 0)))
        out_specs.append(pl.BlockSpec((tf, d), lambda f, s, t: (f, 0)))
        out_shape.append(jax.ShapeDtypeStruct((dff, d), BF16))
        args.append(w_down)
    outs = pl.pallas_call(
        kern,
        grid=(nf, B // bb, nrt),
        in_specs=in_specs,
        out_specs=out_specs,
        out_shape=out_shape,
        scratch_shapes=[pltpu.VMEM((2, bb, HALO + r, tf), F32),
                        pltpu.VMEM(carry_shape, F32),
                        pltpu.VMEM((2, d, tf), BF16)],
        compiler_params=_cparams(("arbitrary", "arbitrary", "arbitrary")),
        name="conv_ffn_up",
    )(*args)
    act, ca, cb = outs[:3]
    if cast_down:
        w_down = outs[3]

    m = B * Tp
    tm = _pick(m, (256, 128))
    y = pl.pallas_call(
        _ffn_down_kernel,
        grid=(m // tm,),
        in_specs=[pl.BlockSpec((tm, dff), lambda i: (i, 0)),
                  pl.BlockSpec((dff, d), lambda i: (0, 0), pipeline_mode=pl.Buffered(1)),
                  pl.BlockSpec((tm, d), lambda i: (i, 0)),
                  pl.BlockSpec((1, d), lambda i: (0, 0))],
        out_specs=pl.BlockSpec((tm, d), lambda i: (i, 0)),
        out_shape=jax.ShapeDtypeStruct((m, d), F32),
        compiler_params=_cparams(("parallel",)),
        name="ffn_down",
    )(act.reshape(m, dff), w_down, x1, g_final.reshape(1, -1))
    return y.reshape(B, Tp, d), jnp.concatenate([ca[nrt - 1], cb[nrt - 1]], axis=-1), w_down


def _rope_tables(pos):
    half = MLA_ROPE // 2
    inv = ROPE_THETA ** (-jnp.arange(0, MLA_ROPE, 2, dtype=F32) / MLA_ROPE)
    ang = pos.astype(F32)[:, None] * inv[None, :]
    cos, sin = jnp.cos(ang), jnp.sin(ang)
    zero = jnp.zeros((pos.shape[0], LANE - 2 * half), F32)
    return (jnp.concatenate([cos, cos, zero], axis=1),
            jnp.concatenate([-sin, sin, zero], axis=1))


def _attn_tile(T):
    return _pick(T, (1024, 128))


def _project(x, pos, w, chunk_tile=0):
    col = w["col"]
    rows = w["in_rows"]
    cos_t, sin_t = _rope_tables(pos)
    h, small, lat, kr = _front(x, w["g_mix"], w["w_in_t"], rows["a"], w["g_kv"], cos_t, sin_t,
                               rank=rows["cq"] - rows["a"], rq=rows["ckv"] - rows["cq"],
                               rk=rows["kpe"] - rows["ckv"])
    qkvr = _matmul_wt(h, w["w_in_t"], rows["q"], rows["a"] - rows["q"], BF16, tn=1024)
    gates = _matmul_wt(h, w["w_in_t"], rows["ga"], rows["end"] - rows["ga"], BF16, tn=1024)
    q = _qprep(small, w["g_q"], w["wq_nope"], w["wq_pe"], w["wq_pe_sw"], cos_t, sin_t, col=col,
               chunk_tile=chunk_tile)
    return dict(qkvr=qkvr, gates=gates, small=small, q=q, lat=lat, kr=kr)


def _finish(x, pr, branch_a, o_m, w, hist, *, B, T):
    x1, h2 = _merge(branch_a, pr["gates"], o_m, x, w["w_o"], w["g_ffn"], col=w["col"])
    y, conv, w["w_down"] = _ffn(h2, x1, w["w_up"], w["w_down"], w["conv_w"], w["conv_b"], hist,
                                w["final_norm"], B=B, T=T, Tp=T)
    return y, conv


def _gla_group(pr, w, s0, *, B, T, row0=0):
    return _gla(pr["qkvr"], pr["gates"], pr["small"], w["wa_pad"], w["b_a"], w["g_gla_out"],
                s0, B=B, T=T, Tp=T, dk=w["dk"], dv=w["dv"], col=w["col"], row0=row0)


def _long_stream(x, pr, w, *, T, s0, hist, prefix):
    branch_a, state = _gla_group(pr, w, s0, B=1, T=T)
    k, vt = _kvup(pr["lat"], pr["kr"], w["w_uk"], w["w_uv_t"], v_transposed=True)
    o_m = _attention_t(pr["q"], k, vt, prefix[0], prefix[1], T=T, t=_attn_tile(T),
                       hps=MLA_HEADS // 4)
    y, conv = _finish(x, pr, branch_a, o_m, w, hist, B=1, T=T)
    return y, pr["lat"], pr["kr"], state, conv


def _short_streams(x, pr, w, *, B, T, past_lat, past_kr, s0_s, hist_s):
    ns = B * T
    dk, dv = w["dk"], w["dv"]

    ba_s, st_s = _gla_group(pr, w, s0_s, B=B, T=T)
    ba_m, st_m = _gla_group(pr, w, jnp.zeros((1, GLA_HEADS, dk, dv), F32), B=1, T=T, row0=ns)

    qlat = _absorb_q(pr["q"], w["w_uk_t3"])
    olat = _attn_latent(qlat, pr["q"], past_lat, jnp.swapaxes(past_kr, 1, 2), pr["lat"],
                        pr["kr"], B=B, T=T)
    om_s = _absorb_out(olat, w["w_uv3"])
    q_m, lat_m, kr_m = pr["q"][:, ns:], pr["lat"][ns:], pr["kr"][ns:]
    k_m, v_m = _kvup(lat_m, kr_m, w["w_uk"], w["w_uv"])
    prefix = _kvup(lat_m, kr_m, w["w_uk"], w["w_uv_t"], v_transposed=True)
    om_m = _attention(q_m, k_m, v_m, B=1, Tq=T, Tk=T, tq=T, tk=T, hps=MLA_HEADS,
                      q_off=0, k_off=0)

    hist = jnp.concatenate([hist_s, jnp.zeros((1,) + hist_s.shape[1:], F32)], axis=0)
    y, conv = _finish(x, pr, jnp.concatenate([ba_s, ba_m], axis=0),
                      jnp.concatenate([om_s, om_m], axis=0), w, hist, B=B + 1, T=T)
    sample = (y[:B], pr["lat"][:ns], pr["kr"][:ns], st_s, conv[:B])
    meta = (lat_m, kr_m, st_m, conv[B:], prefix)
    return sample, meta


def _prep_weights(g_mix, w_in, w_a2, b_a, g_gla_out, g_q, w_uq, g_kv, w_uk, w_uv, w_o,
                  g_ffn, w_up, conv_w, conv_b, w_down, final_norm):
    d = w_in.shape[0]
    rank, gqk = w_a2.shape
    gvw = GLA_HEADS * g_gla_out.shape[0]
    rq, rk = g_q.shape[0], g_kv.shape[0]
    half = MLA_ROPE // 2
    o, offs = 0, {}
    for name, width in (("q", gqk), ("k", gqk), ("v", gvw), ("r", gvw), ("a", rank),
                        ("cq", rq), ("ckv", rk), ("kpe", MLA_ROPE), ("ga", d), ("gb", d)):
        offs[name] = (o, o + width)
        o += width
    assert o == w_in.shape[1]
    in_rows = {name: lo for name, (lo, _) in offs.items()}
    in_rows["end"] = o
    assert all(v % 16 == 0 for v in in_rows.values())
    col = {"q": 0, "k": gqk, "v": 2 * gqk, "r": 2 * gqk + gvw, "ga": 0, "gb": d,
           "cq": 0, "ckv": rq, "kpe": rq + rk, "a": rq + rk + 2 * MLA_ROPE}

    w3 = w_uq.reshape(rq, MLA_HEADS, MLA_NOPE + MLA_ROPE)
    pe = w3[:, :, MLA_NOPE:]
    pe_sw = jnp.concatenate([pe[:, :, half:], pe[:, :, :half]], axis=2)
    zpad = jnp.zeros((rq, MLA_HEADS, LANE - MLA_ROPE), w_uq.dtype)
    flat = lambda t: t.reshape(rq, -1).astype(BF16)
    wa_pad = jnp.concatenate([w_a2, jnp.zeros((LANE - rank, gqk), w_a2.dtype)], axis=0)
    return dict(
        col=col, dk=gqk // GLA_HEADS, dv=g_gla_out.shape[0],
        g_mix=g_mix, w_in_t=jnp.swapaxes(w_in, 0, 1), in_rows=in_rows,
        wa_pad=wa_pad.astype(BF16), b_a=b_a, g_gla_out=g_gla_out, g_q=g_q,
        wq_nope=flat(w3[:, :, :MLA_NOPE]),
        wq_pe=flat(jnp.concatenate([pe, zpad], axis=2)),
        wq_pe_sw=flat(jnp.concatenate([pe_sw, zpad], axis=2)),
        g_kv=g_kv, w_uk=w_uk.astype(BF16), w_uv=w_uv.astype(BF16),
        w_uv_t=w_uv.T.astype(BF16),
        w_uk_t3=w_uk.reshape(rk, MLA_HEADS, MLA_NOPE).transpose(1, 2, 0).astype(BF16),
        w_uv3=w_uv.reshape(rk, MLA_HEADS, MLA_V).transpose(1, 0, 2).astype(BF16),
        w_o=w_o.astype(BF16),
        g_ffn=g_ffn, w_up=w_up, conv_w=conv_w, conv_b=conv_b,
        w_down=w_down, final_norm=final_norm)


def kernel(x_prompt, x_sample, cache_mla_latent, cache_mla_krope, state_gla, cache_ffn_conv,
           meta_tokens, g_mix, w_in, w_a2, b_a, g_gla_out, g_q, w_uq, g_kv, w_uk, w_uv, w_o,
           g_ffn, w_up, conv_w, conv_b, w_down, final_norm):
    assert w_in.shape[0] == 1, "single trunk layer"
    bp, seq, d = x_prompt.shape
    assert bp == 1
    bs, ts, _ = x_sample.shape
    P = cache_mla_latent.shape[2]
    w = _prep_weights(g_mix[0], w_in[0], w_a2[0], b_a[0], g_gla_out[0], g_q[0], w_uq[0],
                      g_kv[0], w_uk[0], w_uv[0], w_o[0], g_ffn[0], w_up[0], conv_w[0],
                      conv_b[0], w_down[0], final_norm)

    n_meta = meta_tokens.shape[0]
    assert n_meta == N_META == ts and seq % CHUNK == 0
    x_short = jnp.concatenate([x_sample.reshape(bs * ts, d), meta_tokens.astype(F32)], axis=0)
    pos_short = jnp.concatenate([jnp.tile(P + jnp.arange(ts, dtype=jnp.int32), bs),
                                 jnp.arange(n_meta, dtype=jnp.int32)])
    pos_long = n_meta + jnp.arange(seq, dtype=jnp.int32)
    pr_short = _project(x_short, pos_short, w)
    pr_long = _project(x_prompt[0], pos_long, w, chunk_tile=_attn_tile(seq))
    (ys, lat_s, kr_s, st_s, cv_s), (lat_m, kr_m, st_m, cv_m, prefix) = _short_streams(
        x_short, pr_short, w, B=bs, T=ts, past_lat=cache_mla_latent[0],
        past_kr=cache_mla_krope[0], s0_s=state_gla[0], hist_s=cache_ffn_conv[0])
    yp, lat_p, kr_p, st_p, cv_p = _long_stream(
        x_prompt[0], pr_long, w, T=seq, s0=st_m, hist=cv_m, prefix=prefix)

    rk = lat_p.shape[1]
    T = n_meta + seq
    return (yp,
            ys,
            jnp.concatenate([lat_m, lat_p], axis=0).reshape(1, 1, T, rk),
            jnp.concatenate([kr_m, kr_p], axis=0)[:, :MLA_ROPE].reshape(1, 1, T, MLA_ROPE),
            st_p[None],
            cv_p[None],
            lat_s.reshape(1, bs, ts, rk),
            kr_s[:, :MLA_ROPE].reshape(1, bs, ts, MLA_ROPE),
            st_s[None],
            cv_s[None])
```

```python
import functools

import jax
import jax.numpy as jnp
from jax import lax
from jax.experimental import pallas as pl
from jax.experimental.pallas import tpu as pltpu

BF16 = jnp.bfloat16
F32 = jnp.float32

CHUNK = 64
CHUNK_SHIFT = 6
N_META = 16
EPS = 1e-6
GLA_HEADS = 4
GLA_GATE_NORM = 16.0
GLA_LOG_ALPHA_MIN = -5.0
MLA_HEADS = 16
MLA_NOPE = 128
MLA_ROPE = 64
MLA_V = 128
ROPE_THETA = 10000.0
CONV_W = 3
NEG_BIG = -1e30
LOG2E = 1.4426950408889634
QK_SCALE_LOG2E = (MLA_NOPE + MLA_ROPE) ** -0.5 * LOG2E

LANE = 128
VT_ONES = 16
GLA_CHUNK = 256
GLA_SEQS = 4
MASK_LANE0 = MLA_ROPE + 1
SAFE_EXP = 64.0
ROW_TILE = 1024
VMEM_LIMIT = 56 * 1024 * 1024


def _cparams(sem, vmem=VMEM_LIMIT):
    return pltpu.CompilerParams(dimension_semantics=sem, vmem_limit_bytes=vmem)


def _rmsnorm(x, g):
    return x * lax.rsqrt(jnp.mean(x * x, axis=-1, keepdims=True) + EPS) * g


def _sigmoid(x):
    return 0.5 * jnp.tanh(0.5 * x) + 0.5


def _pick(n, cands):
    for c in cands:
        if n % c == 0:
            return c
    fits = [t for t in range(16, min(n, max(cands)) + 1, 16) if n % t == 0]
    if not fits:
        raise ValueError(f"no tile in {cands} divides {n}")
    return fits[-1]


_NT = (((1,), (1,)), ((), ()))


def _matmul_wt_kernel(a_ref, w_ref, o_ref, w_scr):
    @pl.when(pl.program_id(1) == 0)
    def _():
        w_scr[...] = w_ref[...].astype(BF16)

    o_ref[...] = lax.dot_general(a_ref[...], w_scr[...], _NT,
                                 preferred_element_type=F32).astype(o_ref.dtype)


def _matmul_wt(a, w_t, row0, n, out_dtype, tn):
    m, k = a.shape
    tm = _pick(m, (ROW_TILE, 512, 384, 128))
    return pl.pallas_call(
        _matmul_wt_kernel,
        grid=(n // tn, m // tm),
        in_specs=[pl.BlockSpec((tm, k), lambda j, i: (i, 0)),
                  pl.BlockSpec((pl.Element(tn), pl.Element(k)),
                               lambda j, i: (pl.multiple_of(row0 + j * tn, 16), 0))],
        out_specs=pl.BlockSpec((tm, tn), lambda j, i: (i, j)),
        out_shape=jax.ShapeDtypeStruct((m, n), out_dtype),
        scratch_shapes=[pltpu.VMEM((tn, k), BF16)],
        compiler_params=_cparams(("parallel", "arbitrary")),
        name="in_proj_wt",
    )(a, w_t)


def _front_kernel(x_ref, g_ref, w_ref, gkv_ref, cos_ref, sin_ref,
                  h_ref, o_ref, lat_ref, kr_ref, w_scr, *, rank, rq, rk):
    @pl.when(pl.program_id(0) == 0)
    def _():
        w = w_ref[...].astype(BF16)
        half = MLA_ROPE // 2
        pe0 = rank + rq + rk
        o_pe = rq + rk
        w_scr[0:rq] = w[rank:rank + rq]
        w_scr[rq:o_pe] = w[rank + rq:pe0]
        w_scr[o_pe:o_pe + MLA_ROPE] = w[pe0:pe0 + MLA_ROPE]
        w_scr[o_pe + MLA_ROPE:o_pe + MLA_ROPE + half] = w[pe0 + half:pe0 + MLA_ROPE]
        w_scr[o_pe + MLA_ROPE + half:o_pe + 2 * MLA_ROPE] = w[pe0:pe0 + half]
        o_a = o_pe + 2 * MLA_ROPE
        w_scr[o_a:o_a + rank] = w[0:rank]
        w_scr[o_a + rank:] = jnp.zeros((w_scr.shape[0] - o_a - rank, w_scr.shape[1]), BF16)

    h = _rmsnorm(x_ref[...], g_ref[...]).astype(BF16)
    h_ref[...] = h
    small = lax.dot_general(h, w_scr[...], _NT, preferred_element_type=F32)
    o_ref[...] = small
    lat_ref[...] = _rmsnorm(small[:, rq:rq + rk], gkv_ref[...])
    blk = small[:, rq + rk:rq + rk + LANE]
    kr_ref[...] = blk * cos_ref[...] + pltpu.roll(blk, LANE // 2, 1) * sin_ref[...]


def _front(x, g_mix, w_t, row0, g_kv, cos_t, sin_t, *, rank, rq, rk):
    m, k = x.shape
    n_in = rank + rq + rk + MLA_ROPE
    n_out = rq + rk + 2 * MLA_ROPE + LANE
    tm = _pick(m, (512, 384, 128))
    kern = functools.partial(_front_kernel, rank=rank, rq=rq, rk=rk)
    row = lambda i: (i, 0)
    return pl.pallas_call(
        kern,
        grid=(m // tm,),
        in_specs=[pl.BlockSpec((tm, k), row),
                  pl.BlockSpec((1, k), lambda i: (0, 0)),
                  pl.BlockSpec((pl.Element(n_in), pl.Element(k)), lambda i: (row0, 0),
                               pipeline_mode=pl.Buffered(1)),
                  pl.BlockSpec((1, rk), lambda i: (0, 0)),
                  pl.BlockSpec((tm, LANE), row),
                  pl.BlockSpec((tm, LANE), row)],
        out_specs=[pl.BlockSpec((tm, k), row),
                   pl.BlockSpec((tm, n_out), row),
                   pl.BlockSpec((tm, rk), row),
                   pl.BlockSpec((tm, LANE), row)],
        out_shape=[jax.ShapeDtypeStruct((m, k), BF16),
                   jax.ShapeDtypeStruct((m, n_out), F32),
                   jax.ShapeDtypeStruct((m, rk), F32),
                   jax.ShapeDtypeStruct((m, LANE), F32)],
        scratch_shapes=[pltpu.VMEM((n_out, k), BF16)],
        compiler_params=_cparams(("arbitrary",)),
        name="front_proj",
    )(x, g_mix.reshape(1, -1), w_t, g_kv.reshape(1, -1), cos_t, sin_t)


def _split3(x):
    a = x.astype(BF16)
    r1 = x - a.astype(F32)
    b = r1.astype(BF16)
    c = (r1 - b.astype(F32)).astype(BF16)
    return a, b, c


def _gla_kernel(q_ref, k_ref, v_ref, r_ref, ga_ref, a_ref, wa_ref, ba_ref, go_ref, s0_ref,
                o_ref, sout_ref, s_scr, *, C, SB, T, H, dk, dv, S):
    c_idx = pl.program_id(1)
    n_chunks = pl.num_programs(1)
    R = S * C

    @pl.when(c_idx == 0)
    def _():
        s_scr[...] = s0_ref[...]

    z = jnp.dot(a_ref[...].astype(BF16), wa_ref[...], preferred_element_type=F32) + ba_ref[...]
    log_sig = jnp.minimum(z, 0.0) - jnp.log(1.0 + jnp.exp(-jnp.abs(z)))
    la = jnp.maximum(log_sig * (1.0 / GLA_GATE_NORM), GLA_LOG_ALPHA_MIN)
    if T % C:
        rows = c_idx * C + lax.broadcasted_iota(jnp.int32, (C, 1), 0)
        la = jnp.where(rows < T, la, 0.0)

    ri = lax.broadcasted_iota(jnp.int32, (R, R), 0)
    ci = lax.broadcasted_iota(jnp.int32, (R, R), 1)
    same_seq = (ri >= ci) if S == 1 else ((ri >= ci) & (ri - ci <= lax.rem(ri, C)))
    tri = jnp.where(same_seq, 1.0, 0.0).astype(BF16)
    ones = jnp.ones((C, LANE), BF16)
    cs_all = jnp.zeros_like(la)
    dsum_all = [jnp.zeros((la.shape[1], LANE), F32) for _ in range(S)]
    for piece in _split3(la):
        cs_all = cs_all + jnp.dot(tri, piece, preferred_element_type=F32)
        for si in range(S):
            dsum_all[si] = dsum_all[si] + lax.dot_general(
                piece[si * C:(si + 1) * C], ones, (((0,), (0,)), ((), ())),
                preferred_element_type=F32)

    sr = lax.broadcasted_iota(jnp.int32, (SB, SB), 0)
    sc = lax.broadcasted_iota(jnp.int32, (SB, SB), 1)
    causal = sr >= sc
    nt = (((1,), (1,)), ((), ()))
    scale = dk ** -0.5

    for si, h in [(si, h) for si in range(S) for h in range(H)]:
        rs = slice(si * C, (si + 1) * C)
        ksl = slice(h * dk, (h + 1) * dk)
        vsl = slice(h * dv, (h + 1) * dv)
        cs = cs_all[rs, ksl]
        c_last = cs[C - 1:C, :]
        q = q_ref[rs, ksl].astype(F32) * scale
        k = k_ref[rs, ksl].astype(F32)
        v = v_ref[rs, vsl]
        s_old = s_scr[si, h]

        o_inter = jnp.dot((q * jnp.exp(cs)).astype(BF16), s_old.astype(BF16),
                          preferred_element_type=F32)
        k_end = (k * jnp.exp(c_last - cs)).astype(BF16)
        upd = lax.dot_general(k_end, v, (((0,), (0,)), ((), ())), preferred_element_type=F32)
        dcol = jnp.exp(dsum_all[si][ksl, :])
        s_scr[si, h] = jnp.concatenate([dcol] * (dv // LANE), axis=1) * s_old + upd

        outs = []
        for i in range(C // SB):
            lo = i * SB
            cs_i = cs[lo:lo + SB]
            q_i = q[lo:lo + SB]
            k_i = k[lo:lo + SB]
            start = cs[lo - 1:lo] if i > 0 else jnp.zeros_like(c_last)
            mid = 0.5 * (start + cs[lo + SB - 1:lo + SB])
            qd = (q_i * jnp.exp(cs_i - mid)).astype(BF16)
            kd = (k_i * jnp.exp(mid - cs_i)).astype(BF16)
            att = lax.dot_general(qd, kd, nt, preferred_element_type=F32)
            att = jnp.where(causal, att, 0.0)
            o_i = jnp.dot(att.astype(BF16), v[lo:lo + SB], preferred_element_type=F32)
            if i > 0:
                qo = (q_i * jnp.exp(cs_i - start)).astype(BF16)
                ko = (k[:lo] * jnp.exp(start - cs[:lo])).astype(BF16)
                att_o = lax.dot_general(qo, ko, nt, preferred_element_type=F32)
                o_i = o_i + jnp.dot(att_o.astype(BF16), v[:lo], preferred_element_type=F32)
            outs.append(o_i)
        o = o_inter + (jnp.concatenate(outs, axis=0) if len(outs) > 1 else outs[0])

        on = _rmsnorm(o, go_ref[...])
        r = r_ref[rs, vsl].astype(F32)
        g = ga_ref[rs, vsl].astype(F32)
        o_ref[rs, vsl] = (_sigmoid(g) * (on * (r * _sigmoid(r)))).astype(o_ref.dtype)

    @pl.when(c_idx == n_chunks - 1)
    def _():
        sout_ref[...] = s_scr[...]


def _gla(qkvr, gates, small, wa_pad, b_a, g_out, s0, *, B, T, Tp, dk, dv, col, row0=0):
    C = min(GLA_CHUNK, Tp)
    SB = min(32, C)
    assert Tp % C == 0 and C % SB == 0
    nc = Tp // C
    H = GLA_HEADS
    qk, vw = H * dk, H * dv
    S = _pick(B, (GLA_SEQS, 1)) if nc == 1 else 1
    R = S * C
    assert row0 % R == 0
    rb = lambda b, c: row0 // R + b * nc + c
    kern = functools.partial(_gla_kernel, C=C, SB=SB, T=T, H=H, dk=dk, dv=dv, S=S)
    return pl.pallas_call(
        kern,
        grid=(B // S, nc),
        in_specs=[
            pl.BlockSpec((R, qk), lambda b, c: (rb(b, c), col["q"] // qk)),
            pl.BlockSpec((R, qk), lambda b, c: (rb(b, c), col["k"] // qk)),
            pl.BlockSpec((R, vw), lambda b, c: (rb(b, c), col["v"] // vw)),
            pl.BlockSpec((R, vw), lambda b, c: (rb(b, c), col["r"] // vw)),
            pl.BlockSpec((R, vw), lambda b, c: (rb(b, c), col["ga"] // vw)),
            pl.BlockSpec((R, LANE), lambda b, c: (rb(b, c), col["a"] // LANE)),
            pl.BlockSpec((LANE, qk), lambda b, c: (0, 0)),
            pl.BlockSpec((1, qk), lambda b, c: (0, 0)),
            pl.BlockSpec((1, dv), lambda b, c: (0, 0)),
            pl.BlockSpec((S, H, dk, dv), lambda b, c: (b, 0, 0, 0)),
        ],
        out_specs=[
            pl.BlockSpec((R, vw), lambda b, c: (b * nc + c, 0)),
            pl.BlockSpec((S, H, dk, dv), lambda b, c: (b, 0, 0, 0)),
        ],
        out_shape=[jax.ShapeDtypeStruct((B * Tp, vw), BF16),
                   jax.ShapeDtypeStruct((B, H, dk, dv), F32)],
        scratch_shapes=[pltpu.VMEM((S, H, dk, dv), F32)],
        compiler_params=_cparams(("parallel", "arbitrary")),
        name="gla",
    )(qkvr, qkvr, qkvr, qkvr, gates, small, wa_pad, b_a.reshape(1, -1), g_out.reshape(1, -1), s0)


def _qprep_kernel(cq_ref, gq_ref, wn_ref, wp_ref, wps_ref, cos_ref, sin_ref, q_ref, *,
                  chunk_tile):
    hq = _rmsnorm(cq_ref[...], gq_ref[...]).astype(BF16)
    qn = jnp.dot(hq, wn_ref[...], preferred_element_type=F32)
    qp = jnp.dot(hq, wp_ref[...], preferred_element_type=F32)
    qs = jnp.dot(hq, wps_ref[...], preferred_element_type=F32)
    cos = cos_ref[...] * QK_SCALE_LOG2E
    sin = sin_ref[...] * QK_SCALE_LOG2E
    tag = 0.0
    if chunk_tile:
        tm = cos.shape[0]
        row = pl.program_id(0) * tm + lax.broadcasted_iota(jnp.int32, (tm, LANE), 0)
        lane = lax.broadcasted_iota(jnp.int32, (tm, LANE), 1)
        chunk = (row & (chunk_tile - 1)) >> CHUNK_SHIFT
        tag = jnp.where(lane - MASK_LANE0 == chunk, 1.0, 0.0)
    for h in range(MLA_HEADS):
        sl = slice(h * LANE, (h + 1) * LANE)
        q_ref[h, :, 0:LANE] = (qn[:, sl] * QK_SCALE_LOG2E).astype(BF16)
        q_ref[h, :, LANE:2 * LANE] = (qp[:, sl] * cos + qs[:, sl] * sin + tag).astype(BF16)


def _qprep(small, g_q, wn, wp, wps, cos_t, sin_t, *, col, chunk_tile=0):
    m = small.shape[0]
    rq = wn.shape[0]
    tm = _pick(m, (512, 256, 128))
    full = lambda i: (0, 0)
    return pl.pallas_call(
        functools.partial(_qprep_kernel, chunk_tile=chunk_tile),
        grid=(m // tm,),
        in_specs=[pl.BlockSpec((tm, rq), lambda i: (i, col["cq"] // rq)),
                  pl.BlockSpec((1, rq), full),
                  pl.BlockSpec(wn.shape, full),
                  pl.BlockSpec(wp.shape, full),
                  pl.BlockSpec(wps.shape, full),
                  pl.BlockSpec((tm, LANE), lambda i: (i, 0)),
                  pl.BlockSpec((tm, LANE), lambda i: (i, 0))],
        out_specs=pl.BlockSpec((MLA_HEADS, tm, 2 * LANE), lambda i: (0, i, 0)),
        out_shape=jax.ShapeDtypeStruct((MLA_HEADS, m, 2 * LANE), BF16),
        compiler_params=_cparams(("parallel",)),
        name="mla_q",
    )(small, g_q.reshape(1, -1), wn, wp, wps, cos_t, sin_t)


def _kvup_kernel(lat_ref, kr_ref, wuk_ref, wuv_ref, k_ref, v_ref, *, v_transposed):
    lat = lat_ref[...].astype(BF16)
    kn = jnp.dot(lat, wuk_ref[...], preferred_element_type=F32)
    kr = kr_ref[...]
    lane = lax.broadcasted_iota(jnp.int32, kr.shape, 1)
    kp = jnp.where(lane == MLA_ROPE, 1.0, kr).astype(BF16)
    if v_transposed:
        vv = lax.dot_general(wuv_ref[...], lat, (((1,), (1,)), ((), ())),
                             preferred_element_type=F32)
    else:
        vv = jnp.dot(lat, wuv_ref[...], preferred_element_type=F32)
    for h in range(MLA_HEADS):
        sl = slice(h * LANE, (h + 1) * LANE)
        k_ref[h, :, 0:LANE] = kn[:, sl].astype(BF16)
        k_ref[h, :, LANE:2 * LANE] = kp
        if v_transposed:
            v_ref[h, 0:LANE, :] = vv[sl, :].astype(BF16)
            v_ref[h, LANE:LANE + VT_ONES, :] = jnp.ones((VT_ONES, vv.shape[1]), BF16)
        else:
            v_ref[h] = vv[:, sl].astype(BF16)


def _kvup(lat, kr, wuk, wuv, *, v_transposed=False):
    m, rk = lat.shape
    tm = _pick(m, (512, 256, 128))
    full = lambda i: (0, 0)
    if v_transposed:
        v_spec = pl.BlockSpec((MLA_HEADS, LANE + VT_ONES, tm), lambda i: (0, 0, i))
        v_shape = (MLA_HEADS, LANE + VT_ONES, m)
    else:
        v_spec = pl.BlockSpec((MLA_HEADS, tm, LANE), lambda i: (0, i, 0))
        v_shape = (MLA_HEADS, m, LANE)
    return pl.pallas_call(
        functools.partial(_kvup_kernel, v_transposed=v_transposed),
        grid=(m // tm,),
        in_specs=[pl.BlockSpec((tm, rk), lambda i: (i, 0)),
                  pl.BlockSpec((tm, LANE), lambda i: (i, 0)),
                  pl.BlockSpec(wuk.shape, full),
                  pl.BlockSpec(wuv.shape, full)],
        out_specs=[pl.BlockSpec((MLA_HEADS, tm, 2 * LANE), lambda i: (0, i, 0)), v_spec],
        out_shape=[jax.ShapeDtypeStruct((MLA_HEADS, m, 2 * LANE), BF16),
                   jax.ShapeDtypeStruct(v_shape, BF16)],
        compiler_params=_cparams(("parallel",)),
        name="mla_kv",
    )(lat, kr, wuk, wuv)


def _last_kblock(qi, *, tq, tk, nk, q_off, k_off):
    top_chunk = ((qi + 1) * tq - 1 + q_off) // CHUNK
    last_key = (top_chunk + 1) * CHUNK - 1 - k_off
    return jnp.minimum(last_key // tk, nk - 1)


def _attn_kernel(q_ref, k_ref, v_ref, o_ref, m_scr, l_scr, acc_scr, *, hps, tq, tk, nk,
                 q_off, k_off):
    qi = pl.program_id(2)
    ki = pl.program_id(3)

    @pl.when(ki == 0)
    def _():
        m_scr[...] = jnp.full(m_scr.shape, NEG_BIG, F32)
        l_scr[...] = jnp.zeros(l_scr.shape, F32)
        acc_scr[...] = jnp.zeros(acc_scr.shape, F32)

    @pl.when(ki <= _last_kblock(qi, tq=tq, tk=tk, nk=nk, q_off=q_off, k_off=k_off))
    def _():
        q_chunk = (qi * tq + q_off + lax.broadcasted_iota(jnp.int32, (tq, 1), 0)) >> CHUNK_SHIFT
        k_chunk = (ki * tk + k_off + lax.broadcasted_iota(jnp.int32, (1, tk), 1)) >> CHUNK_SHIFT
        visible = q_chunk >= k_chunk

        def head(h, carry):
            s = lax.dot_general(q_ref[h], k_ref[h], (((1,), (1,)), ((), ())),
                                preferred_element_type=F32)
            s = jnp.where(visible, s, NEG_BIG)
            m_prev = m_scr[h]
            m_new = jnp.maximum(m_prev, jnp.max(s, axis=-1, keepdims=True))
            p = jnp.exp2(s - m_new)
            alpha = jnp.exp2(m_prev - m_new)
            l_scr[h] = alpha * l_scr[h] + jnp.sum(p, axis=-1, keepdims=True)
            acc_scr[h] = alpha * acc_scr[h] + jnp.dot(p.astype(BF16), v_ref[h],
                                                      preferred_element_type=F32)
            m_scr[h] = m_new
            return carry

        lax.fori_loop(0, hps, head, 0)

    @pl.when(ki == nk - 1)
    def _():
        for h in range(hps):
            o_ref[:, h * LANE:(h + 1) * LANE] = (acc_scr[h] / l_scr[h]).astype(o_ref.dtype)


def _attention(q, k, v, *, B, Tq, Tk, tq, tk, hps, q_off, k_off):
    nq = Tq // tq
    nk = Tk // tk
    hg = MLA_HEADS // hps
    dqk = q.shape[2]
    dvh = v.shape[2]
    last = functools.partial(_last_kblock, tq=tq, tk=tk, nk=nk, q_off=q_off, k_off=k_off)
    kern = functools.partial(_attn_kernel, hps=hps, tq=tq, tk=tk, nk=nk, q_off=q_off,
                             k_off=k_off)
    kv_row = lambda b, g, i, j: b * nk + jnp.minimum(j, last(i))
    return pl.pallas_call(
        kern,
        grid=(B, hg, nq, nk),
        in_specs=[pl.BlockSpec((hps, tq, dqk), lambda b, g, i, j: (g, b * nq + i, 0)),
                  pl.BlockSpec((hps, tk, dqk), lambda b, g, i, j: (g, kv_row(b, g, i, j), 0)),
                  pl.BlockSpec((hps, tk, dvh), lambda b, g, i, j: (g, kv_row(b, g, i, j), 0))],
        out_specs=pl.BlockSpec((tq, hps * dvh), lambda b, g, i, j: (b * nq + i, g)),
        out_shape=jax.ShapeDtypeStruct((B * Tq, MLA_HEADS * dvh), BF16),
        scratch_shapes=[pltpu.VMEM((hps, tq, 1), F32),
                        pltpu.VMEM((hps, tq, 1), F32),
                        pltpu.VMEM((hps, tq, dvh), F32)],
        compiler_params=_cparams(("parallel", "parallel", "parallel", "arbitrary")),
        name="mla_attn",
    )(q, k, v)


def _attn_t_kernel(qi_ref, ki_ref, q_ref, k_ref, vt_ref, kp_ref, vtp_ref, o_ref,
                   q_scr, r_scr, acc_scr, *, hps, t):
    pair = pl.program_id(1)
    qi = qi_ref[pair]
    ki = ki_ref[pair]
    nt = (((1,), (1,)), ((), ()))
    pe = slice(LANE, 2 * LANE)
    lane = lax.broadcasted_iota(jnp.int32, (t, LANE), 1)

    def set_reference(h, r):
        neg_r = jnp.transpose(jnp.broadcast_to(-r, (LANE, t)))
        q_scr[h, :, pe] = jnp.where(lane == MLA_ROPE, neg_r.astype(BF16), q_ref[h, :, pe])
        r_scr[h] = r

    def shifted_scores(h, own_tile=False):
        k = k_ref[h]
        if own_tile:
            ahead = lane - MASK_LANE0
            k_chunk = lax.broadcasted_iota(jnp.int32, (t, LANE), 0) >> CHUNK_SHIFT
            hidden = (ahead >= 0) & (ahead < k_chunk)
            k = jnp.concatenate(
                [k[:, 0:LANE], jnp.where(hidden, jnp.asarray(NEG_BIG, BF16), k[:, pe])], axis=1)
        return lax.dot_general(k, q_scr[h], nt, preferred_element_type=F32)

    @pl.when(ki == 0)
    def _():
        for h in range(hps):
            q_scr[h, :, 0:LANE] = q_ref[h, :, 0:LANE]
            s = lax.dot_general(kp_ref[h], q_ref[h], nt, preferred_element_type=F32)
            r = jnp.max(s, axis=0, keepdims=True).astype(BF16).astype(F32)
            p = jnp.exp2((s - r).astype(BF16))
            acc_scr[h] = jnp.dot(vtp_ref[h], p, preferred_element_type=F32)
            set_reference(h, r)

    def general(h, own_tile):
        sp = shifted_scores(h, own_tile)
        r = r_scr[h]
        rise = jnp.maximum(jnp.max(sp, axis=0, keepdims=True), 0.0)
        r_new = (r + rise).astype(BF16).astype(F32)
        delta = r_new - r
        p = jnp.exp2((sp - delta).astype(BF16))
        acc_scr[h] = jnp.exp2(-delta) * acc_scr[h] + jnp.dot(vt_ref[h], p,
                                                               preferred_element_type=F32)
        if not own_tile:
            set_reference(h, r_new)

    def tile_step(own_tile):
        unsafe = []
        sp_next = shifted_scores(0, own_tile)
        for h in range(hps):
            sp = sp_next
            if h + 1 < hps:
                sp_next = shifted_scores(h + 1, own_tile)
            safe = jnp.max(sp) <= SAFE_EXP
            part = jnp.dot(vt_ref[h], jnp.exp2(sp.astype(BF16)), preferred_element_type=F32)
            acc_scr[h] += jnp.where(safe, part, 0.0)
            unsafe.append(jnp.logical_not(safe))

        @pl.when(functools.reduce(jnp.logical_or, unsafe))
        def _():
            for h in range(hps):
                @pl.when(unsafe[h])
                def _():
                    general(h, own_tile)

    @pl.when(ki < qi)
    def _():
        tile_step(False)

    @pl.when(ki == qi)
    def _():
        tile_step(True)
        for h in range(hps):
            acc = acc_scr[h]
            o_t = acc[0:LANE] / acc[LANE:LANE + 1]
            o_ref[:, h * LANE:(h + 1) * LANE] = o_t.T.astype(o_ref.dtype)


def _attention_t(q, k, vt, k_pre, vt_pre, *, T, t, hps):
    n = T // t
    hg = MLA_HEADS // hps
    dqk = q.shape[2]
    npre = k_pre.shape[1]
    vrows = vt.shape[1]
    pairs = [(i, j) for i in range(n) for j in range(i + 1)]
    qi_arr = jnp.asarray([p[0] for p in pairs], jnp.int32)
    ki_arr = jnp.asarray([p[1] for p in pairs], jnp.int32)
    kern = functools.partial(_attn_t_kernel, hps=hps, t=t)
    grid_spec = pltpu.PrefetchScalarGridSpec(
        num_scalar_prefetch=2,
        grid=(hg, len(pairs)),
        in_specs=[pl.BlockSpec((hps, t, dqk), lambda g, p, qi, ki: (g, qi[p], 0)),
                  pl.BlockSpec((hps, t, dqk), lambda g, p, qi, ki: (g, ki[p], 0)),
                  pl.BlockSpec((hps, vrows, t), lambda g, p, qi, ki: (g, 0, ki[p])),
                  pl.BlockSpec((hps, npre, dqk), lambda g, p, qi, ki: (g, 0, 0)),
                  pl.BlockSpec((hps, vrows, npre), lambda g, p, qi, ki: (g, 0, 0))],
        out_specs=pl.BlockSpec((t, hps * LANE), lambda g, p, qi, ki: (qi[p], g)),
        scratch_shapes=[pltpu.VMEM((hps, t, dqk), BF16),
                        pltpu.VMEM((hps, 1, t), F32),
                        pltpu.VMEM((hps, vrows, t), F32)])
    return pl.pallas_call(
        kern,
        grid_spec=grid_spec,
        out_shape=jax.ShapeDtypeStruct((T, MLA_HEADS * LANE), BF16),
        compiler_params=_cparams(("parallel", "arbitrary")),
        name="mla_attn_t",
    )(qi_arr, ki_arr, q, k, vt, k_pre, vt_pre)


def _absorb_q_kernel(q_ref, w_ref, o_ref):
    o_ref[0] = jnp.dot(q_ref[0, :, 0:MLA_NOPE], w_ref[0],
                       preferred_element_type=F32).astype(o_ref.dtype)


def _absorb_q(q, w_uk_t3):
    heads, rows, dqk = q.shape
    rk = w_uk_t3.shape[2]
    return pl.pallas_call(
        _absorb_q_kernel,
        grid=(heads,),
        in_specs=[pl.BlockSpec((1, rows, dqk), lambda h: (h, 0, 0)),
                  pl.BlockSpec((1, MLA_NOPE, rk), lambda h: (h, 0, 0))],
        out_specs=pl.BlockSpec((1, rows, rk), lambda h: (h, 0, 0)),
        out_shape=jax.ShapeDtypeStruct((heads, rows, rk), BF16),
        compiler_params=_cparams(("parallel",)),
        name="mla_absorb_q",
    )(q, w_uk_t3)


def _attn_latent_kernel(ql_ref, q_ref, plat_ref, pkr_ref, lat_ref, kr_ref, o_ref, *, T, P, S):
    heads, _, rk = ql_ref.shape
    rows = heads * T
    nt = (((1,), (1,)), ((), ()))
    tok = lax.rem(lax.broadcasted_iota(jnp.int32, (rows, 1), 0), T)
    q_chunk = (P + tok) >> CHUNK_SHIFT
    k_chunk = lax.broadcasted_iota(jnp.int32, (1, P + T), 1) >> CHUNK_SHIFT
    visible = q_chunk >= k_chunk
    for si in range(S):
        ts = slice(si * T, (si + 1) * T)
        ql = ql_ref[:, ts, :].reshape(rows, rk)
        qpe = q_ref[:, ts, LANE:2 * LANE].reshape(rows, LANE)[:, 0:MLA_ROPE]
        lat_all = jnp.concatenate([plat_ref[si].astype(BF16), lat_ref[ts, :].astype(BF16)],
                                  axis=0)
        s_pe = jnp.concatenate(
            [jnp.dot(qpe, pkr_ref[si].astype(BF16), preferred_element_type=F32),
             lax.dot_general(qpe, kr_ref[ts, 0:MLA_ROPE].astype(BF16), nt,
                             preferred_element_type=F32)], axis=1)
        s = lax.dot_general(ql, lat_all, nt, preferred_element_type=F32) + s_pe
        s = jnp.where(visible, s, NEG_BIG)
        p = jnp.exp2(s - jnp.max(s, axis=-1, keepdims=True))
        o = jnp.dot(p.astype(BF16), lat_all, preferred_element_type=F32)
        o = o / jnp.sum(p, axis=-1, keepdims=True)
        o_ref[:, ts, :] = o.reshape(heads, T, rk).astype(o_ref.dtype)


def _attn_latent(qlat, q, past_lat, past_kr_t, lat, kr, *, B, T):
    heads, _, rk = qlat.shape
    P = past_lat.shape[1]
    S = _pick(B, (2, 1))
    kern = functools.partial(_attn_latent_kernel, T=T, P=P, S=S)
    return pl.pallas_call(
        kern,
        grid=(B // S,),
        in_specs=[pl.BlockSpec((heads, S * T, rk), lambda b: (0, b, 0)),
                  pl.BlockSpec((heads, S * T, q.shape[2]), lambda b: (0, b, 0)),
                  pl.BlockSpec((S, P, rk), lambda b: (b, 0, 0)),
                  pl.BlockSpec((S, past_kr_t.shape[1], P), lambda b: (b, 0, 0)),
                  pl.BlockSpec((S * T, rk), lambda b: (b, 0)),
                  pl.BlockSpec((S * T, LANE), lambda b: (b, 0))],
        out_specs=pl.BlockSpec((heads, S * T, rk), lambda b: (0, b, 0)),
        out_shape=jax.ShapeDtypeStruct((heads, B * T, rk), BF16),
        compiler_params=_cparams(("parallel",)),
        name="mla_attn_latent",
    )(qlat, q, past_lat, past_kr_t, lat, kr)


def _absorb_out_kernel(o_ref, w_ref, out_ref):
    out_ref[...] = jnp.dot(o_ref[0], w_ref[0], preferred_element_type=F32).astype(out_ref.dtype)


def _absorb_out(olat, w_uv3):
    heads, rows, rk = olat.shape
    dvh = w_uv3.shape[2]
    return pl.pallas_call(
        _absorb_out_kernel,
        grid=(heads,),
        in_specs=[pl.BlockSpec((1, rows, rk), lambda h: (h, 0, 0)),
                  pl.BlockSpec((1, rk, dvh), lambda h: (h, 0, 0))],
        out_specs=pl.BlockSpec((rows, dvh), lambda h: (0, h)),
        out_shape=jax.ShapeDtypeStruct((rows, heads * dvh), BF16),
        compiler_params=_cparams(("parallel",)),
        name="mla_absorb_out",
    )(olat, w_uv3)


def _merge_kernel(a_ref, gb_ref, om_ref, x_ref, wo_ref, gf_ref, x1_ref, h2_ref):
    merged = a_ref[...].astype(F32) + _sigmoid(gb_ref[...].astype(F32)) * om_ref[...].astype(F32)
    x1 = x_ref[...] + jnp.dot(merged.astype(BF16), wo_ref[...], preferred_element_type=F32)
    x1_ref[...] = x1
    h2_ref[...] = _rmsnorm(x1, gf_ref[...]).astype(BF16)


def _merge(branch_a, gates, o_m, x, wo, g_ffn, *, col):
    m, d = x.shape
    tm = _pick(m, (512, 384, 256, 128))
    row = lambda i: (i, 0)
    return pl.pallas_call(
        _merge_kernel,
        grid=(m // tm,),
        in_specs=[pl.BlockSpec((tm, d), row),
                  pl.BlockSpec((tm, d), lambda i: (i, col["gb"] // d)),
                  pl.BlockSpec((tm, d), row),
                  pl.BlockSpec((tm, d), row),
                  pl.BlockSpec(wo.shape, lambda i: (0, 0), pipeline_mode=pl.Buffered(1)),
                  pl.BlockSpec((1, d), lambda i: (0, 0))],
        out_specs=[pl.BlockSpec((tm, d), row), pl.BlockSpec((tm, d), row)],
        out_shape=[jax.ShapeDtypeStruct((m, d), F32), jax.ShapeDtypeStruct((m, d), BF16)],
        compiler_params=_cparams(("parallel",)),
        name="merge_out_proj",
    )(branch_a, gates, o_m, x, wo, g_ffn.reshape(1, -1))


HALO = 8


def _ffn_up_kernel(*refs, bb, r, tf, loc, carried, cast_down):
    (h_ref, wa_ref, wb_ref, cwa_ref, cwb_ref, cba_ref, cbb_ref, ha_ref, hb_ref), refs = \
        refs[:9], refs[9:]
    if cast_down:
        wd_ref, act_ref, ca_ref, cb_ref, wdb_ref, ext_scr, carry_scr, w_scr = refs
    else:
        act_ref, ca_ref, cb_ref, ext_scr, carry_scr, w_scr = refs
    s = pl.program_id(1)
    rt = pl.program_id(2)
    d = h_ref.shape[2]

    @pl.when((s == 0) & (rt == 0))
    def _():
        w_scr[0] = wa_ref[...].astype(BF16)
        w_scr[1] = wb_ref[...].astype(BF16)
        if cast_down:
            wdb_ref[...] = wd_ref[...].astype(BF16)

    if carried:
        @pl.when(rt == 0)
        def _():
            carry_scr[0] = ha_ref[...]
            carry_scr[1] = hb_ref[...]

    h = h_ref[...].reshape(bb * r, d)
    conv = []
    for half, (cw_ref, cbias_ref, hist_ref, cout_ref) in enumerate(
            ((cwa_ref, cba_ref, ha_ref, ca_ref), (cwb_ref, cbb_ref, hb_ref, cb_ref))):
        u = jnp.dot(h, w_scr[half], preferred_element_type=F32).reshape(bb, r, tf)
        ext_scr[half, :, HALO:HALO + r, :] = u
        ext_scr[half, :, HALO - 2:HALO, :] = carry_scr[half] if carried else hist_ref[...]
        u1 = ext_scr[half, :, HALO - 1:HALO - 1 + r, :]
        u2 = ext_scr[half, :, HALO - 2:HALO - 2 + r, :]
        cw = cw_ref[...]
        conv.append(cbias_ref[...] + cw[0:1] * u2 + cw[1:2] * u1 + cw[2:3] * u)
        if carried:
            carry_scr[half] = ext_scr[half, :, HALO + r - 2:HALO + r, :]
        cout_ref[0] = ext_scr[half, :, HALO + loc:HALO + loc + 2, :]

    act_ref[...] = ((conv[0] * _sigmoid(conv[0])) * conv[1]).astype(act_ref.dtype)


def _ffn_down_kernel(act_ref, wd_ref, x1_ref, gf_ref, y_ref):
    down = jnp.dot(act_ref[...], wd_ref[...], preferred_element_type=F32)
    y_ref[...] = _rmsnorm(x1_ref[...] + down, gf_ref[...])


def _ffn(h2, x1, w_up, w_down, conv_w, conv_b, hist, g_final, *, B, T, Tp):
    d = h2.shape[1]
    dff = w_down.shape[0]
    cast_down = w_down.dtype != BF16
    tf = _pick(dff, (512, 256, 128))
    nf = dff // tf
    if Tp <= 128:
        bb, r = B, Tp
    else:
        bb, r = 1, _pick(Tp, (ROW_TILE, 128))
    nrt = Tp // r
    carried = nrt > 1
    loc = (T - 2) - (nrt - 1) * r
    assert 0 <= loc <= r - 2, "final two valid rows must sit in the last row tile"
    kern = functools.partial(_ffn_up_kernel, bb=bb, r=r, tf=tf, loc=loc, carried=carried,
                             cast_down=cast_down)
    carry_shape = (2, bb, 2, tf) if carried else (1, 1, 2, LANE)
    in_specs = [pl.BlockSpec((bb, r, d), lambda f, s, t: (s, t, 0)),
                pl.BlockSpec((d, tf), lambda f, s, t: (0, f)),
                pl.BlockSpec((d, tf), lambda f, s, t: (0, nf + f)),
                pl.BlockSpec((CONV_W, tf), lambda f, s, t: (0, f)),
                pl.BlockSpec((CONV_W, tf), lambda f, s, t: (0, nf + f)),
                pl.BlockSpec((1, tf), lambda f, s, t: (0, f)),
                pl.BlockSpec((1, tf), lambda f, s, t: (0, nf + f)),
                pl.BlockSpec((bb, 2, tf), lambda f, s, t: (s, 0, f)),
                pl.BlockSpec((bb, 2, tf), lambda f, s, t: (s, 0, nf + f))]
    out_specs = [pl.BlockSpec((bb, r, tf), lambda f, s, t: (s, t, f)),
                 pl.BlockSpec((1, bb, 2, tf), lambda f, s, t: (t, s, 0, f)),
                 pl.BlockSpec((1, bb, 2, tf), lambda f, s, t: (t, s, 0, f))]
    out_shape = [jax.ShapeDtypeStruct((B, Tp, dff), BF16),
                 jax.ShapeDtypeStruct((nrt, B, 2, dff), F32),
                 jax.ShapeDtypeStruct((nrt, B, 2, dff), F32)]
    args = [h2.reshape(B, Tp, d), w_up, w_up, conv_w, conv_w, conv_b.reshape(1, -1),
            conv_b.reshape(1, -1), hist, hist]
    if cast_down:
        in_specs.append(pl.BlockSpec((tf, d), lambda f, s, t: (f, 0)))
        out_specs.append(pl.BlockSpec((tf, d), lambda f, s, t: (f, 0)))
        out_shape.append(jax.ShapeDtypeStruct((dff, d), BF16))
        args.append(w_down)
    outs = pl.pallas_call(
        kern,
        grid=(nf, B // bb, nrt),
        in_specs=in_specs,
        out_specs=out_specs,
        out_shape=out_shape,
        scratch_shapes=[pltpu.VMEM((2, bb, HALO + r, tf), F32),
                        pltpu.VMEM(carry_shape, F32),
                        pltpu.VMEM((2, d, tf), BF16)],
        compiler_params=_cparams(("arbitrary", "arbitrary", "arbitrary")),
        name="conv_ffn_up",
    )(*args)
    act, ca, cb = outs[:3]
    if cast_down:
        w_down = outs[3]

    m = B * Tp
    tm = _pick(m, (256, 128))
    y = pl.pallas_call(
        _ffn_down_kernel,
        grid=(m // tm,),
        in_specs=[pl.BlockSpec((tm, dff), lambda i: (i, 0)),
                  pl.BlockSpec((dff, d), lambda i: (0, 0), pipeline_mode=pl.Buffered(1)),
                  pl.BlockSpec((tm, d), lambda i: (i, 0)),
                  pl.BlockSpec((1, d), lambda i: (0, 0))],
        out_specs=pl.BlockSpec((tm, d), lambda i: (i, 0)),
        out_shape=jax.ShapeDtypeStruct((m, d), F32),
        compiler_params=_cparams(("parallel",)),
        name="ffn_down",
    )(act.reshape(m, dff), w_down, x1, g_final.reshape(1, -1))
    return y.reshape(B, Tp, d), jnp.concatenate([ca[nrt - 1], cb[nrt - 1]], axis=-1), w_down


def _rope_tables(pos):
    half = MLA_ROPE // 2
    inv = ROPE_THETA ** (-jnp.arange(0, MLA_ROPE, 2, dtype=F32) / MLA_ROPE)
    ang = pos.astype(F32)[:, None] * inv[None, :]
    cos, sin = jnp.cos(ang), jnp.sin(ang)
    zero = jnp.zeros((pos.shape[0], LANE - 2 * half), F32)
    return (jnp.concatenate([cos, cos, zero], axis=1),
            jnp.concatenate([-sin, sin, zero], axis=1))


def _attn_tile(T):
    return _pick(T, (1024, 128))


def _project(x, pos, w, chunk_tile=0):
    col = w["col"]
    rows = w["in_rows"]
    cos_t, sin_t = _rope_tables(pos)
    h, small, lat, kr = _front(x, w["g_mix"], w["w_in_t"], rows["a"], w["g_kv"], cos_t, sin_t,
                               rank=rows["cq"] - rows["a"], rq=rows["ckv"] - rows["cq"],
                               rk=rows["kpe"] - rows["ckv"])
    qkvr = _matmul_wt(h, w["w_in_t"], rows["q"], rows["a"] - rows["q"], BF16, tn=1024)
    gates = _matmul_wt(h, w["w_in_t"], rows["ga"], rows["end"] - rows["ga"], BF16, tn=1024)
    q = _qprep(small, w["g_q"], w["wq_nope"], w["wq_pe"], w["wq_pe_sw"], cos_t, sin_t, col=col,
               chunk_tile=chunk_tile)
    return dict(qkvr=qkvr, gates=gates, small=small, q=q, lat=lat, kr=kr)


def _finish(x, pr, branch_a, o_m, w, hist, *, B, T):
    x1, h2 = _merge(branch_a, pr["gates"], o_m, x, w["w_o"], w["g_ffn"], col=w["col"])
    y, conv, w["w_down"] = _ffn(h2, x1, w["w_up"], w["w_down"], w["conv_w"], w["conv_b"], hist,
                                w["final_norm"], B=B, T=T, Tp=T)
    return y, conv


def _gla_group(pr, w, s0, *, B, T, row0=0):
    return _gla(pr["qkvr"], pr["gates"], pr["small"], w["wa_pad"], w["b_a"], w["g_gla_out"],
                s0, B=B, T=T, Tp=T, dk=w["dk"], dv=w["dv"], col=w["col"], row0=row0)


def _long_stream(x, pr, w, *, T, s0, hist, prefix):
    branch_a, state = _gla_group(pr, w, s0, B=1, T=T)
    k, vt = _kvup(pr["lat"], pr["kr"], w["w_uk"], w["w_uv_t"], v_transposed=True)
    o_m = _attention_t(pr["q"], k, vt, prefix[0], prefix[1], T=T, t=_attn_tile(T),
                       hps=MLA_HEADS // 2)
    y, conv = _finish(x, pr, branch_a, o_m, w, hist, B=1, T=T)
    return y, pr["lat"], pr["kr"], state, conv


def _short_streams(x, pr, w, *, B, T, past_lat, past_kr, s0_s, hist_s):
    ns = B * T
    dk, dv = w["dk"], w["dv"]

    ba_s, st_s = _gla_group(pr, w, s0_s, B=B, T=T)
    ba_m, st_m = _gla_group(pr, w, jnp.zeros((1, GLA_HEADS, dk, dv), F32), B=1, T=T, row0=ns)

    qlat = _absorb_q(pr["q"], w["w_uk_t3"])
    olat = _attn_latent(qlat, pr["q"], past_lat, jnp.swapaxes(past_kr, 1, 2), pr["lat"],
                        pr["kr"], B=B, T=T)
    om_s = _absorb_out(olat, w["w_uv3"])
    q_m, lat_m, kr_m = pr["q"][:, ns:], pr["lat"][ns:], pr["kr"][ns:]
    k_m, v_m = _kvup(lat_m, kr_m, w["w_uk"], w["w_uv"])
    prefix = _kvup(lat_m, kr_m, w["w_uk"], w["w_uv_t"], v_transposed=True)
    om_m = _attention(q_m, k_m, v_m, B=1, Tq=T, Tk=T, tq=T, tk=T, hps=MLA_HEADS,
                      q_off=0, k_off=0)

    hist = jnp.concatenate([hist_s, jnp.zeros((1,) + hist_s.shape[1:], F32)], axis=0)
    y, conv = _finish(x, pr, jnp.concatenate([ba_s, ba_m], axis=0),
                      jnp.concatenate([om_s, om_m], axis=0), w, hist, B=B + 1, T=T)
    sample = (y[:B], pr["lat"][:ns], pr["kr"][:ns], st_s, conv[:B])
    meta = (lat_m, kr_m, st_m, conv[B:], prefix)
    return sample, meta


def _prep_weights(g_mix, w_in, w_a2, b_a, g_gla_out, g_q, w_uq, g_kv, w_uk, w_uv, w_o,
                  g_ffn, w_up, conv_w, conv_b, w_down, final_norm):
    d = w_in.shape[0]
    rank, gqk = w_a2.shape
    gvw = GLA_HEADS * g_gla_out.shape[0]
    rq, rk = g_q.shape[0], g_kv.shape[0]
    half = MLA_ROPE // 2
    o, offs = 0, {}
    for name, width in (("q", gqk), ("k", gqk), ("v", gvw), ("r", gvw), ("a", rank),
                        ("cq", rq), ("ckv", rk), ("kpe", MLA_ROPE), ("ga", d), ("gb", d)):
        offs[name] = (o, o + width)
        o += width
    assert o == w_in.shape[1]
    in_rows = {name: lo for name, (lo, _) in offs.items()}
    in_rows["end"] = o
    assert all(v % 16 == 0 for v in in_rows.values())
    col = {"q": 0, "k": gqk, "v": 2 * gqk, "r": 2 * gqk + gvw, "ga": 0, "gb": d,
           "cq": 0, "ckv": rq, "kpe": rq + rk, "a": rq + rk + 2 * MLA_ROPE}

    w3 = w_uq.reshape(rq, MLA_HEADS, MLA_NOPE + MLA_ROPE)
    pe = w3[:, :, MLA_NOPE:]
    pe_sw = jnp.concatenate([pe[:, :, half:], pe[:, :, :half]], axis=2)
    zpad = jnp.zeros((rq, MLA_HEADS, LANE - MLA_ROPE), w_uq.dtype)
    flat = lambda t: t.reshape(rq, -1).astype(BF16)
    wa_pad = jnp.concatenate([w_a2, jnp.zeros((LANE - rank, gqk), w_a2.dtype)], axis=0)
    return dict(
        col=col, dk=gqk // GLA_HEADS, dv=g_gla_out.shape[0],
        g_mix=g_mix, w_in_t=jnp.swapaxes(w_in, 0, 1), in_rows=in_rows,
        wa_pad=wa_pad.astype(BF16), b_a=b_a, g_gla_out=g_gla_out, g_q=g_q,
        wq_nope=flat(w3[:, :, :MLA_NOPE]),
        wq_pe=flat(jnp.concatenate([pe, zpad], axis=2)),
        wq_pe_sw=flat(jnp.concatenate([pe_sw, zpad], axis=2)),
        g_kv=g_kv, w_uk=w_uk.astype(BF16), w_uv=w_uv.astype(BF16),
        w_uv_t=w_uv.T.astype(BF16),
        w_uk_t3=w_uk.reshape(rk, MLA_HEADS, MLA_NOPE).transpose(1, 2, 0).astype(BF16),
        w_uv3=w_uv.reshape(rk, MLA_HEADS, MLA_V).transpose(1, 0, 2).astype(BF16),
        w_o=w_o.astype(BF16),
        g_ffn=g_ffn, w_up=w_up, conv_w=conv_w, conv_b=conv_b,
        w_down=w_down, final_norm=final_norm)


def kernel(x_prompt, x_sample, cache_mla_latent, cache_mla_krope, state_gla, cache_ffn_conv,
           meta_tokens, g_mix, w_in, w_a2, b_a, g_gla_out, g_q, w_uq, g_kv, w_uk, w_uv, w_o,
           g_ffn, w_up, conv_w, conv_b, w_down, final_norm):
    assert w_in.shape[0] == 1, "single trunk layer"
    bp, seq, d = x_prompt.shape
    assert bp == 1
    bs, ts, _ = x_sample.shape
    P = cache_mla_latent.shape[2]
    w = _prep_weights(g_mix[0], w_in[0], w_a2[0], b_a[0], g_gla_out[0], g_q[0], w_uq[0],
                      g_kv[0], w_uk[0], w_uv[0], w_o[0], g_ffn[0], w_up[0], conv_w[0],
                      conv_b[0], w_down[0], final_norm)

    n_meta = meta_tokens.shape[0]
    assert n_meta == N_META == ts and seq % CHUNK == 0
    x_short = jnp.concatenate([x_sample.reshape(bs * ts, d), meta_tokens.astype(F32)], axis=0)
    pos_short = jnp.concatenate([jnp.tile(P + jnp.arange(ts, dtype=jnp.int32), bs),
                                 jnp.arange(n_meta, dtype=jnp.int32)])
    pos_long = n_meta + jnp.arange(seq, dtype=jnp.int32)
    pr_short = _project(x_short, pos_short, w)
    pr_long = _project(x_prompt[0], pos_long, w, chunk_tile=_attn_tile(seq))
    (ys, lat_s, kr_s, st_s, cv_s), (lat_m, kr_m, st_m, cv_m, prefix) = _short_streams(
        x_short, pr_short, w, B=bs, T=ts, past_lat=cache_mla_latent[0],
        past_kr=cache_mla_krope[0], s0_s=state_gla[0], hist_s=cache_ffn_conv[0])
    yp, lat_p, kr_p, st_p, cv_p = _long_stream(
        x_prompt[0], pr_long, w, T=seq, s0=st_m, hist=cv_m, prefix=prefix)

    rk = lat_p.shape[1]
    T = n_meta + seq
    return (yp,
            ys,
            jnp.concatenate([lat_m, lat_p], axis=0).reshape(1, 1, T, rk),
            jnp.concatenate([kr_m, kr_p], axis=0)[:, :MLA_ROPE].reshape(1, 1, T, MLA_ROPE),
            st_p[None],
            cv_p[None],
            lat_s.reshape(1, bs, ts, rk),
            kr_s[:, :MLA_ROPE].reshape(1, bs, ts, MLA_ROPE),
            st_s[None],
            cv_s[None])
```

```python
import functools

import jax
import jax.numpy as jnp
from jax import lax
from jax.experimental import pallas as pl
from jax.experimental.pallas import tpu as pltpu

BF16 = jnp.bfloat16
F32 = jnp.float32

CHUNK = 64
CHUNK_SHIFT = 6
N_META = 16
EPS = 1e-6
GLA_HEADS = 4
GLA_GATE_NORM = 16.0
GLA_LOG_ALPHA_MIN = -5.0
MLA_HEADS = 16
MLA_NOPE = 128
MLA_ROPE = 64
MLA_V = 128
ROPE_THETA = 10000.0
CONV_W = 3
NEG_BIG = -1e30
LOG2E = 1.4426950408889634
QK_SCALE_LOG2E = (MLA_NOPE + MLA_ROPE) ** -0.5 * LOG2E

LANE = 128
VT_ONES = 16
GLA_CHUNK = 256
GLA_SEQS = 4
MASK_LANE0 = MLA_ROPE + 1
SAFE_EXP = 64.0
ROW_TILE = 1024
VMEM_LIMIT = 56 * 1024 * 1024


def _cparams(sem, vmem=VMEM_LIMIT):
    return pltpu.CompilerParams(dimension_semantics=sem, vmem_limit_bytes=vmem)


def _rmsnorm(x, g):
    return x * lax.rsqrt(jnp.mean(x * x, axis=-1, keepdims=True) + EPS) * g


def _sigmoid(x):
    return 0.5 * jnp.tanh(0.5 * x) + 0.5


def _pick(n, cands):
    for c in cands:
        if n % c == 0:
            return c
    fits = [t for t in range(16, min(n, max(cands)) + 1, 16) if n % t == 0]
    if not fits:
        raise ValueError(f"no tile in {cands} divides {n}")
    return fits[-1]


_NT = (((1,), (1,)), ((), ()))


def _matmul_wt_kernel(a_ref, w_ref, o_ref, w_scr):
    @pl.when(pl.program_id(1) == 0)
    def _():
        w_scr[...] = w_ref[...].astype(BF16)

    o_ref[...] = lax.dot_general(a_ref[...], w_scr[...], _NT,
                                 preferred_element_type=F32).astype(o_ref.dtype)


def _matmul_wt(a, w_t, row0, n, out_dtype, tn):
    m, k = a.shape
    tm = _pick(m, (ROW_TILE, 512, 384, 128))
    return pl.pallas_call(
        _matmul_wt_kernel,
        grid=(n // tn, m // tm),
        in_specs=[pl.BlockSpec((tm, k), lambda j, i: (i, 0)),
                  pl.BlockSpec((pl.Element(tn), pl.Element(k)),
                               lambda j, i: (pl.multiple_of(row0 + j * tn, 16), 0))],
        out_specs=pl.BlockSpec((tm, tn), lambda j, i: (i, j)),
        out_shape=jax.ShapeDtypeStruct((m, n), out_dtype),
        scratch_shapes=[pltpu.VMEM((tn, k), BF16)],
        compiler_params=_cparams(("parallel", "arbitrary")),
        name="in_proj_wt",
    )(a, w_t)


def _front_kernel(x_ref, g_ref, w_ref, gkv_ref, cos_ref, sin_ref,
                  h_ref, o_ref, lat_ref, kr_ref, w_scr, *, rank, rq, rk):
    @pl.when(pl.program_id(0) == 0)
    def _():
        w = w_ref[...].astype(BF16)
        half = MLA_ROPE // 2
        pe0 = rank + rq + rk
        o_pe = rq + rk
        w_scr[0:rq] = w[rank:rank + rq]
        w_scr[rq:o_pe] = w[rank + rq:pe0]
        w_scr[o_pe:o_pe + MLA_ROPE] = w[pe0:pe0 + MLA_ROPE]
        w_scr[o_pe + MLA_ROPE:o_pe + MLA_ROPE + half] = w[pe0 + half:pe0 + MLA_ROPE]
        w_scr[o_pe + MLA_ROPE + half:o_pe + 2 * MLA_ROPE] = w[pe0:pe0 + half]
        o_a = o_pe + 2 * MLA_ROPE
        w_scr[o_a:o_a + rank] = w[0:rank]
        w_scr[o_a + rank:] = jnp.zeros((w_scr.shape[0] - o_a - rank, w_scr.shape[1]), BF16)

    h = _rmsnorm(x_ref[...], g_ref[...]).astype(BF16)
    h_ref[...] = h
    small = lax.dot_general(h, w_scr[...], _NT, preferred_element_type=F32)
    o_ref[...] = small
    lat_ref[...] = _rmsnorm(small[:, rq:rq + rk], gkv_ref[...])
    blk = small[:, rq + rk:rq + rk + LANE]
    kr_ref[...] = blk * cos_ref[...] + pltpu.roll(blk, LANE // 2, 1) * sin_ref[...]


def _front(x, g_mix, w_t, row0, g_kv, cos_t, sin_t, *, rank, rq, rk):
    m, k = x.shape
    n_in = rank + rq + rk + MLA_ROPE
    n_out = rq + rk + 2 * MLA_ROPE + LANE
    tm = _pick(m, (512, 384, 128))
    kern = functools.partial(_front_kernel, rank=rank, rq=rq, rk=rk)
    row = lambda i: (i, 0)
    return pl.pallas_call(
        kern,
        grid=(m // tm,),
        in_specs=[pl.BlockSpec((tm, k), row),
                  pl.BlockSpec((1, k), lambda i: (0, 0)),
                  pl.BlockSpec((pl.Element(n_in), pl.Element(k)), lambda i: (row0, 0),
                               pipeline_mode=pl.Buffered(1)),
                  pl.BlockSpec((1, rk), lambda i: (0, 0)),
                  pl.BlockSpec((tm, LANE), row),
                  pl.BlockSpec((tm, LANE), row)],
        out_specs=[pl.BlockSpec((tm, k), row),
                   pl.BlockSpec((tm, n_out), row),
                   pl.BlockSpec((tm, rk), row),
                   pl.BlockSpec((tm, LANE), row)],
        out_shape=[jax.ShapeDtypeStruct((m, k), BF16),
                   jax.ShapeDtypeStruct((m, n_out), F32),
                   jax.ShapeDtypeStruct((m, rk), F32),
                   jax.ShapeDtypeStruct((m, LANE), F32)],
        scratch_shapes=[pltpu.VMEM((n_out, k), BF16)],
        compiler_params=_cparams(("arbitrary",)),
        name="front_proj",
    )(x, g_mix.reshape(1, -1), w_t, g_kv.reshape(1, -1), cos_t, sin_t)


def _split3(x):
    a = x.astype(BF16)
    r1 = x - a.astype(F32)
    b = r1.astype(BF16)
    c = (r1 - b.astype(F32)).astype(BF16)
    return a, b, c


def _gla_kernel(q_ref, k_ref, v_ref, r_ref, ga_ref, a_ref, wa_ref, ba_ref, go_ref, s0_ref,
                o_ref, sout_ref, s_scr, *, C, SB, T, H, dk, dv, S):
    c_idx = pl.program_id(1)
    n_chunks = pl.num_programs(1)
    R = S * C

    @pl.when(c_idx == 0)
    def _():
        s_scr[...] = s0_ref[...]

    z = jnp.dot(a_ref[...].astype(BF16), wa_ref[...], preferred_element_type=F32) + ba_ref[...]
    log_sig = jnp.minimum(z, 0.0) - jnp.log(1.0 + jnp.exp(-jnp.abs(z)))
    la = jnp.maximum(log_sig * (1.0 / GLA_GATE_NORM), GLA_LOG_ALPHA_MIN)
    if T % C:
        rows = c_idx * C + lax.broadcasted_iota(jnp.int32, (C, 1), 0)
        la = jnp.where(rows < T, la, 0.0)

    ri = lax.broadcasted_iota(jnp.int32, (R, R), 0)
    ci = lax.broadcasted_iota(jnp.int32, (R, R), 1)
    same_seq = (ri >= ci) if S == 1 else ((ri >= ci) & (ri - ci <= lax.rem(ri, C)))
    tri = jnp.where(same_seq, 1.0, 0.0).astype(BF16)
    ones = jnp.ones((C, LANE), BF16)
    cs_all = jnp.zeros_like(la)
    dsum_all = [jnp.zeros((la.shape[1], LANE), F32) for _ in range(S)]
    for piece in _split3(la):
        cs_all = cs_all + jnp.dot(tri, piece, preferred_element_type=F32)
        for si in range(S):
            dsum_all[si] = dsum_all[si] + lax.dot_general(
                piece[si * C:(si + 1) * C], ones, (((0,), (0,)), ((), ())),
                preferred_element_type=F32)

    sr = lax.broadcasted_iota(jnp.int32, (SB, SB), 0)
    sc = lax.broadcasted_iota(jnp.int32, (SB, SB), 1)
    causal = sr >= sc
    nt = (((1,), (1,)), ((), ()))
    scale = dk ** -0.5

    for si, h in [(si, h) for si in range(S) for h in range(H)]:
        rs = slice(si * C, (si + 1) * C)
        ksl = slice(h * dk, (h + 1) * dk)
        vsl = slice(h * dv, (h + 1) * dv)
        cs = cs_all[rs, ksl]
        c_last = cs[C - 1:C, :]
        q = q_ref[rs, ksl].astype(F32) * scale
        k = k_ref[rs, ksl].astype(F32)
        v = v_ref[rs, vsl]
        s_old = s_scr[si, h]

        o_inter = jnp.dot((q * jnp.exp(cs)).astype(BF16), s_old.astype(BF16),
                          preferred_element_type=F32)
        k_end = (k * jnp.exp(c_last - cs)).astype(BF16)
        upd = lax.dot_general(k_end, v, (((0,), (0,)), ((), ())), preferred_element_type=F32)
        dcol = jnp.exp(dsum_all[si][ksl, :])
        s_scr[si, h] = jnp.concatenate([dcol] * (dv // LANE), axis=1) * s_old + upd

        outs = []
        for i in range(C // SB):
            lo = i * SB
            cs_i = cs[lo:lo + SB]
            q_i = q[lo:lo + SB]
            k_i = k[lo:lo + SB]
            start = cs[lo - 1:lo] if i > 0 else jnp.zeros_like(c_last)
            mid = 0.5 * (start + cs[lo + SB - 1:lo + SB])
            qd = (q_i * jnp.exp(cs_i - mid)).astype(BF16)
            kd = (k_i * jnp.exp(mid - cs_i)).astype(BF16)
            att = lax.dot_general(qd, kd, nt, preferred_element_type=F32)
            att = jnp.where(causal, att, 0.0)
            o_i = jnp.dot(att.astype(BF16), v[lo:lo + SB], preferred_element_type=F32)
            if i > 0:
                qo = (q_i * jnp.exp(cs_i - start)).astype(BF16)
                ko = (k[:lo] * jnp.exp(start - cs[:lo])).astype(BF16)
                att_o = lax.dot_general(qo, ko, nt, preferred_element_type=F32)
                o_i = o_i + jnp.dot(att_o.astype(BF16), v[:lo], preferred_element_type=F32)
            outs.append(o_i)
        o = o_inter + (jnp.concatenate(outs, axis=0) if len(outs) > 1 else outs[0])

        on = _rmsnorm(o, go_ref[...])
        r = r_ref[rs, vsl].astype(F32)
        g = ga_ref[rs, vsl].astype(F32)
        o_ref[rs, vsl] = (_sigmoid(g) * (on * (r * _sigmoid(r)))).astype(o_ref.dtype)

    @pl.when(c_idx == n_chunks - 1)
    def _():
        sout_ref[...] = s_scr[...]


def _gla(qkvr, gates, small, wa_pad, b_a, g_out, s0, *, B, T, Tp, dk, dv, col, row0=0):
    C = min(GLA_CHUNK, Tp)
    SB = min(32, C)
    assert Tp % C == 0 and C % SB == 0
    nc = Tp // C
    H = GLA_HEADS
    qk, vw = H * dk, H * dv
    S = _pick(B, (GLA_SEQS, 1)) if nc == 1 else 1
    R = S * C
    assert row0 % R == 0
    rb = lambda b, c: row0 // R + b * nc + c
    kern = functools.partial(_gla_kernel, C=C, SB=SB, T=T, H=H, dk=dk, dv=dv, S=S)
    return pl.pallas_call(
        kern,
        grid=(B // S, nc),
        in_specs=[
            pl.BlockSpec((R, qk), lambda b, c: (rb(b, c), col["q"] // qk)),
            pl.BlockSpec((R, qk), lambda b, c: (rb(b, c), col["k"] // qk)),
            pl.BlockSpec((R, vw), lambda b, c: (rb(b, c), col["v"] // vw)),
            pl.BlockSpec((R, vw), lambda b, c: (rb(b, c), col["r"] // vw)),
            pl.BlockSpec((R, vw), lambda b, c: (rb(b, c), col["ga"] // vw)),
            pl.BlockSpec((R, LANE), lambda b, c: (rb(b, c), col["a"] // LANE)),
            pl.BlockSpec((LANE, qk), lambda b, c: (0, 0)),
            pl.BlockSpec((1, qk), lambda b, c: (0, 0)),
            pl.BlockSpec((1, dv), lambda b, c: (0, 0)),
            pl.BlockSpec((S, H, dk, dv), lambda b, c: (b, 0, 0, 0)),
        ],
        out_specs=[
            pl.BlockSpec((R, vw), lambda b, c: (b * nc + c, 0)),
            pl.BlockSpec((S, H, dk, dv), lambda b, c: (b, 0, 0, 0)),
        ],
        out_shape=[jax.ShapeDtypeStruct((B * Tp, vw), BF16),
                   jax.ShapeDtypeStruct((B, H, dk, dv), F32)],
        scratch_shapes=[pltpu.VMEM((S, H, dk, dv), F32)],
        compiler_params=_cparams(("parallel", "arbitrary")),
        name="gla",
    )(qkvr, qkvr, qkvr, qkvr, gates, small, wa_pad, b_a.reshape(1, -1), g_out.reshape(1, -1), s0)


def _qprep_kernel(cq_ref, gq_ref, wn_ref, wp_ref, wps_ref, cos_ref, sin_ref, q_ref, *,
                  chunk_tile):
    hq = _rmsnorm(cq_ref[...], gq_ref[...]).astype(BF16)
    qn = jnp.dot(hq, wn_ref[...], preferred_element_type=F32)
    qp = jnp.dot(hq, wp_ref[...], preferred_element_type=F32)
    qs = jnp.dot(hq, wps_ref[...], preferred_element_type=F32)
    cos = cos_ref[...] * QK_SCALE_LOG2E
    sin = sin_ref[...] * QK_SCALE_LOG2E
    tag = 0.0
    if chunk_tile:
        tm = cos.shape[0]
        row = pl.program_id(0) * tm + lax.broadcasted_iota(jnp.int32, (tm, LANE), 0)
        lane = lax.broadcasted_iota(jnp.int32, (tm, LANE), 1)
        chunk = (row & (chunk_tile - 1)) >> CHUNK_SHIFT
        tag = jnp.where(lane - MASK_LANE0 == chunk, 1.0, 0.0)
    for h in range(MLA_HEADS):
        sl = slice(h * LANE, (h + 1) * LANE)
        q_ref[h, :, 0:LANE] = (qn[:, sl] * QK_SCALE_LOG2E).astype(BF16)
        q_ref[h, :, LANE:2 * LANE] = (qp[:, sl] * cos + qs[:, sl] * sin + tag).astype(BF16)


def _qprep(small, g_q, wn, wp, wps, cos_t, sin_t, *, col, chunk_tile=0):
    m = small.shape[0]
    rq = wn.shape[0]
    tm = _pick(m, (512, 256, 128))
    full = lambda i: (0, 0)
    return pl.pallas_call(
        functools.partial(_qprep_kernel, chunk_tile=chunk_tile),
        grid=(m // tm,),
        in_specs=[pl.BlockSpec((tm, rq), lambda i: (i, col["cq"] // rq)),
                  pl.BlockSpec((1, rq), full),
                  pl.BlockSpec(wn.shape, full),
                  pl.BlockSpec(wp.shape, full),
                  pl.BlockSpec(wps.shape, full),
                  pl.BlockSpec((tm, LANE), lambda i: (i, 0)),
                  pl.BlockSpec((tm, LANE), lambda i: (i, 0))],
        out_specs=pl.BlockSpec((MLA_HEADS, tm, 2 * LANE), lambda i: (0, i, 0)),
        out_shape=jax.ShapeDtypeStruct((MLA_HEADS, m, 2 * LANE), BF16),
        compiler_params=_cparams(("parallel",)),
        name="mla_q",
    )(small, g_q.reshape(1, -1), wn, wp, wps, cos_t, sin_t)


def _kvup_kernel(lat_ref, kr_ref, wuk_ref, wuv_ref, k_ref, v_ref, *, v_transposed):
    lat = lat_ref[...].astype(BF16)
    kn = jnp.dot(lat, wuk_ref[...], preferred_element_type=F32)
    kr = kr_ref[...]
    lane = lax.broadcasted_iota(jnp.int32, kr.shape, 1)
    kp = jnp.where(lane == MLA_ROPE, 1.0, kr).astype(BF16)
    if v_transposed:
        vv = lax.dot_general(wuv_ref[...], lat, (((1,), (1,)), ((), ())),
                             preferred_element_type=F32)
    else:
        vv = jnp.dot(lat, wuv_ref[...], preferred_element_type=F32)
    for h in range(MLA_HEADS):
        sl = slice(h * LANE, (h + 1) * LANE)
        k_ref[h, :, 0:LANE] = kn[:, sl].astype(BF16)
        k_ref[h, :, LANE:2 * LANE] = kp
        if v_transposed:
            v_ref[h, 0:LANE, :] = vv[sl, :].astype(BF16)
            v_ref[h, LANE:LANE + VT_ONES, :] = jnp.ones((VT_ONES, vv.shape[1]), BF16)
        else:
            v_ref[h] = vv[:, sl].astype(BF16)


def _kvup(lat, kr, wuk, wuv, *, v_transposed=False):
    m, rk = lat.shape
    tm = _pick(m, (512, 256, 128))
    full = lambda i: (0, 0)
    if v_transposed:
        v_spec = pl.BlockSpec((MLA_HEADS, LANE + VT_ONES, tm), lambda i: (0, 0, i))
        v_shape = (MLA_HEADS, LANE + VT_ONES, m)
    else:
        v_spec = pl.BlockSpec((MLA_HEADS, tm, LANE), lambda i: (0, i, 0))
        v_shape = (MLA_HEADS, m, LANE)
    return pl.pallas_call(
        functools.partial(_kvup_kernel, v_transposed=v_transposed),
        grid=(m // tm,),
        in_specs=[pl.BlockSpec((tm, rk), lambda i: (i, 0)),
                  pl.BlockSpec((tm, LANE), lambda i: (i, 0)),
                  pl.BlockSpec(wuk.shape, full),
                  pl.BlockSpec(wuv.shape, full)],
        out_specs=[pl.BlockSpec((MLA_HEADS, tm, 2 * LANE), lambda i: (0, i, 0)), v_spec],
        out_shape=[jax.ShapeDtypeStruct((MLA_HEADS, m, 2 * LANE), BF16),
                   jax.ShapeDtypeStruct(v_shape, BF16)],
        compiler_params=_cparams(("parallel",)),
        name="mla_kv",
    )(lat, kr, wuk, wuv)


def _last_kblock(qi, *, tq, tk, nk, q_off, k_off):
    top_chunk = ((qi + 1) * tq - 1 + q_off) // CHUNK
    last_key = (top_chunk + 1) * CHUNK - 1 - k_off
    return jnp.minimum(last_key // tk, nk - 1)


def _attn_kernel(q_ref, k_ref, v_ref, o_ref, m_scr, l_scr, acc_scr, *, hps, tq, tk, nk,
                 q_off, k_off):
    qi = pl.program_id(2)
    ki = pl.program_id(3)

    @pl.when(ki == 0)
    def _():
        m_scr[...] = jnp.full(m_scr.shape, NEG_BIG, F32)
        l_scr[...] = jnp.zeros(l_scr.shape, F32)
        acc_scr[...] = jnp.zeros(acc_scr.shape, F32)

    @pl.when(ki <= _last_kblock(qi, tq=tq, tk=tk, nk=nk, q_off=q_off, k_off=k_off))
    def _():
        q_chunk = (qi * tq + q_off + lax.broadcasted_iota(jnp.int32, (tq, 1), 0)) >> CHUNK_SHIFT
        k_chunk = (ki * tk + k_off + lax.broadcasted_iota(jnp.int32, (1, tk), 1)) >> CHUNK_SHIFT
        visible = q_chunk >= k_chunk

        def head(h, carry):
            s = lax.dot_general(q_ref[h], k_ref[h], (((1,), (1,)), ((), ())),
                                preferred_element_type=F32)
            s = jnp.where(visible, s, NEG_BIG)
            m_prev = m_scr[h]
            m_new = jnp.maximum(m_prev, jnp.max(s, axis=-1, keepdims=True))
            p = jnp.exp2(s - m_new)
            alpha = jnp.exp2(m_prev - m_new)
            l_scr[h] = alpha * l_scr[h] + jnp.sum(p, axis=-1, keepdims=True)
            acc_scr[h] = alpha * acc_scr[h] + jnp.dot(p.astype(BF16), v_ref[h],
                                                      preferred_element_type=F32)
            m_scr[h] = m_new
            return carry

        lax.fori_loop(0, hps, head, 0)

    @pl.when(ki == nk - 1)
    def _():
        for h in range(hps):
            o_ref[:, h * LANE:(h + 1) * LANE] = (acc_scr[h] / l_scr[h]).astype(o_ref.dtype)


def _attention(q, k, v, *, B, Tq, Tk, tq, tk, hps, q_off, k_off):
    nq = Tq // tq
    nk = Tk // tk
    hg = MLA_HEADS // hps
    dqk = q.shape[2]
    dvh = v.shape[2]
    last = functools.partial(_last_kblock, tq=tq, tk=tk, nk=nk, q_off=q_off, k_off=k_off)
    kern = functools.partial(_attn_kernel, hps=hps, tq=tq, tk=tk, nk=nk, q_off=q_off,
                             k_off=k_off)
    kv_row = lambda b, g, i, j: b * nk + jnp.minimum(j, last(i))
    return pl.pallas_call(
        kern,
        grid=(B, hg, nq, nk),
        in_specs=[pl.BlockSpec((hps, tq, dqk), lambda b, g, i, j: (g, b * nq + i, 0)),
                  pl.BlockSpec((hps, tk, dqk), lambda b, g, i, j: (g, kv_row(b, g, i, j), 0)),
                  pl.BlockSpec((hps, tk, dvh), lambda b, g, i, j: (g, kv_row(b, g, i, j), 0))],
        out_specs=pl.BlockSpec((tq, hps * dvh), lambda b, g, i, j: (b * nq + i, g)),
        out_shape=jax.ShapeDtypeStruct((B * Tq, MLA_HEADS * dvh), BF16),
        scratch_shapes=[pltpu.VMEM((hps, tq, 1), F32),
                        pltpu.VMEM((hps, tq, 1), F32),
                        pltpu.VMEM((hps, tq, dvh), F32)],
        compiler_params=_cparams(("parallel", "parallel", "parallel", "arbitrary")),
        name="mla_attn",
    )(q, k, v)


def _attn_t_kernel(qi_ref, ki_ref, q_ref, k_ref, vt_ref, kp_ref, vtp_ref, o_ref,
                   q_scr, r_scr, acc_scr, *, hps, t):
    pair = pl.program_id(1)
    qi = qi_ref[pair]
    ki = ki_ref[pair]
    nt = (((1,), (1,)), ((), ()))
    pe = slice(LANE, 2 * LANE)
    lane = lax.broadcasted_iota(jnp.int32, (t, LANE), 1)

    def set_reference(h, r):
        neg_r = jnp.transpose(jnp.broadcast_to(-r, (LANE, t)))
        q_scr[h, :, pe] = jnp.where(lane == MLA_ROPE, neg_r.astype(BF16), q_ref[h, :, pe])
        r_scr[h] = r

    def shifted_scores(h, own_tile=False):
        k = k_ref[h]
        if own_tile:
            ahead = lane - MASK_LANE0
            k_chunk = lax.broadcasted_iota(jnp.int32, (t, LANE), 0) >> CHUNK_SHIFT
            hidden = (ahead >= 0) & (ahead < k_chunk)
            k = jnp.concatenate(
                [k[:, 0:LANE], jnp.where(hidden, jnp.asarray(NEG_BIG, BF16), k[:, pe])], axis=1)
        return lax.dot_general(k, q_scr[h], nt, preferred_element_type=F32)

    @pl.when(ki == 0)
    def _():
        for h in range(hps):
            q_scr[h, :, 0:LANE] = q_ref[h, :, 0:LANE]
            s = lax.dot_general(kp_ref[h], q_ref[h], nt, preferred_element_type=F32)
            r = jnp.max(s, axis=0, keepdims=True).astype(BF16).astype(F32)
            p = jnp.exp2((s - r).astype(BF16))
            acc_scr[h] = jnp.dot(vtp_ref[h], p, preferred_element_type=F32)
            set_reference(h, r)

    def general(h, own_tile):
        sp = shifted_scores(h, own_tile)
        r = r_scr[h]
        rise = jnp.maximum(jnp.max(sp, axis=0, keepdims=True), 0.0)
        r_new = (r + rise).astype(BF16).astype(F32)
        delta = r_new - r
        p = jnp.exp2((sp - delta).astype(BF16))
        acc_scr[h] = jnp.exp2(-delta) * acc_scr[h] + jnp.dot(vt_ref[h], p,
                                                               preferred_element_type=F32)
        if not own_tile:
            set_reference(h, r_new)

    def tile_step(own_tile):
        unsafe = []
        sp_next = shifted_scores(0, own_tile)
        for h in range(hps):
            sp = sp_next
            if h + 1 < hps:
                sp_next = shifted_scores(h + 1, own_tile)
            safe = jnp.max(sp) <= SAFE_EXP
            part = jnp.dot(vt_ref[h], jnp.exp2(sp.astype(BF16)), preferred_element_type=F32)
            acc_scr[h] += jnp.where(safe, part, 0.0)
            unsafe.append(jnp.logical_not(safe))

        @pl.when(functools.reduce(jnp.logical_or, unsafe))
        def _():
            for h in range(hps):
                @pl.when(unsafe[h])
                def _():
                    general(h, own_tile)

    @pl.when(ki < qi)
    def _():
        tile_step(False)

    @pl.when(ki == qi)
    def _():
        tile_step(True)
        for h in range(hps):
            acc = acc_scr[h]
            o_t = acc[0:LANE] / acc[LANE:LANE + 1]
            o_ref[:, h * LANE:(h + 1) * LANE] = o_t.T.astype(o_ref.dtype)


def _attention_t(q, k, vt, k_pre, vt_pre, *, T, t, hps):
    n = T // t
    hg = MLA_HEADS // hps
    dqk = q.shape[2]
    npre = k_pre.shape[1]
    vrows = vt.shape[1]
    pairs = [(i, j) for i in range(n) for j in range(i + 1)]
    qi_arr = jnp.asarray([p[0] for p in pairs], jnp.int32)
    ki_arr = jnp.asarray([p[1] for p in pairs], jnp.int32)
    kern = functools.partial(_attn_t_kernel, hps=hps, t=t)
    grid_spec = pltpu.PrefetchScalarGridSpec(
        num_scalar_prefetch=2,
        grid=(hg, len(pairs)),
        in_specs=[pl.BlockSpec((hps, t, dqk), lambda g, p, qi, ki: (g, qi[p], 0)),
                  pl.BlockSpec((hps, t, dqk), lambda g, p, qi, ki: (g, ki[p], 0)),
                  pl.BlockSpec((hps, vrows, t), lambda g, p, qi, ki: (g, 0, ki[p])),
                  pl.BlockSpec((hps, npre, dqk), lambda g, p, qi, ki: (g, 0, 0)),
                  pl.BlockSpec((hps, vrows, npre), lambda g, p, qi, ki: (g, 0, 0))],
        out_specs=pl.BlockSpec((t, hps * LANE), lambda g, p, qi, ki: (qi[p], g)),
        scratch_shapes=[pltpu.VMEM((hps, t, dqk), BF16),
                        pltpu.VMEM((hps, 1, t), F32),
                        pltpu.VMEM((hps, vrows, t), F32)])
    return pl.pallas_call(
        kern,
        grid_spec=grid_spec,
        out_shape=jax.ShapeDtypeStruct((T, MLA_HEADS * LANE), BF16),
        compiler_params=_cparams(("parallel", "arbitrary")),
        name="mla_attn_t",
    )(qi_arr, ki_arr, q, k, vt, k_pre, vt_pre)


def _absorb_q_kernel(q_ref, w_ref, o_ref):
    o_ref[0] = jnp.dot(q_ref[0, :, 0:MLA_NOPE], w_ref[0],
                       preferred_element_type=F32).astype(o_ref.dtype)


def _absorb_q(q, w_uk_t3):
    heads, rows, dqk = q.shape
    rk = w_uk_t3.shape[2]
    return pl.pallas_call(
        _absorb_q_kernel,
        grid=(heads,),
        in_specs=[pl.BlockSpec((1, rows, dqk), lambda h: (h, 0, 0)),
                  pl.BlockSpec((1, MLA_NOPE, rk), lambda h: (h, 0, 0))],
        out_specs=pl.BlockSpec((1, rows, rk), lambda h: (h, 0, 0)),
        out_shape=jax.ShapeDtypeStruct((heads, rows, rk), BF16),
        compiler_params=_cparams(("parallel",)),
        name="mla_absorb_q",
    )(q, w_uk_t3)


def _attn_latent_kernel(ql_ref, q_ref, plat_ref, pkr_ref, lat_ref, kr_ref, o_ref, *, T, P, S):
    heads, _, rk = ql_ref.shape
    rows = heads * T
    nt = (((1,), (1,)), ((), ()))
    tok = lax.rem(lax.broadcasted_iota(jnp.int32, (rows, 1), 0), T)
    q_chunk = (P + tok) >> CHUNK_SHIFT
    k_chunk = lax.broadcasted_iota(jnp.int32, (1, P + T), 1) >> CHUNK_SHIFT
    visible = q_chunk >= k_chunk
    for si in range(S):
        ts = slice(si * T, (si + 1) * T)
        ql = ql_ref[:, ts, :].reshape(rows, rk)
        qpe = q_ref[:, ts, LANE:2 * LANE].reshape(rows, LANE)[:, 0:MLA_ROPE]
        lat_all = jnp.concatenate([plat_ref[si].astype(BF16), lat_ref[ts, :].astype(BF16)],
                                  axis=0)
        s_pe = jnp.concatenate(
            [jnp.dot(qpe, pkr_ref[si].astype(BF16), preferred_element_type=F32),
             lax.dot_general(qpe, kr_ref[ts, 0:MLA_ROPE].astype(BF16), nt,
                             preferred_element_type=F32)], axis=1)
        s = lax.dot_general(ql, lat_all, nt, preferred_element_type=F32) + s_pe
        s = jnp.where(visible, s, NEG_BIG)
        p = jnp.exp2(s - jnp.max(s, axis=-1, keepdims=True))
        o = jnp.dot(p.astype(BF16), lat_all, preferred_element_type=F32)
        o = o / jnp.sum(p, axis=-1, keepdims=True)
        o_ref[:, ts, :] = o.reshape(heads, T, rk).astype(o_ref.dtype)


def _attn_latent(qlat, q, past_lat, past_kr_t, lat, kr, *, B, T):
    heads, _, rk = qlat.shape
    P = past_lat.shape[1]
    S = _pick(B, (2, 1))
    kern = functools.partial(_attn_latent_kernel, T=T, P=P, S=S)
    return pl.pallas_call(
        kern,
        grid=(B // S,),
        in_specs=[pl.BlockSpec((heads, S * T, rk), lambda b: (0, b, 0)),
                  pl.BlockSpec((heads, S * T, q.shape[2]), lambda b: (0, b, 0)),
                  pl.BlockSpec((S, P, rk), lambda b: (b, 0, 0)),
                  pl.BlockSpec((S, past_kr_t.shape[1], P), lambda b: (b, 0, 0)),
                  pl.BlockSpec((S * T, rk), lambda b: (b, 0)),
                  pl.BlockSpec((S * T, LANE), lambda b: (b, 0))],
        out_specs=pl.BlockSpec((heads, S * T, rk), lambda b: (0, b, 0)),
        out_shape=jax.ShapeDtypeStruct((heads, B * T, rk), BF16),
        compiler_params=_cparams(("parallel",)),
        name="mla_attn_latent",
    )(qlat, q, past_lat, past_kr_t, lat, kr)


def _absorb_out_kernel(o_ref, w_ref, out_ref):
    out_ref[...] = jnp.dot(o_ref[0], w_ref[0], preferred_element_type=F32).astype(out_ref.dtype)


def _absorb_out(olat, w_uv3):
    heads, rows, rk = olat.shape
    dvh = w_uv3.shape[2]
    return pl.pallas_call(
        _absorb_out_kernel,
        grid=(heads,),
        in_specs=[pl.BlockSpec((1, rows, rk), lambda h: (h, 0, 0)),
                  pl.BlockSpec((1, rk, dvh), lambda h: (h, 0, 0))],
        out_specs=pl.BlockSpec((rows, dvh), lambda h: (0, h)),
        out_shape=jax.ShapeDtypeStruct((rows, heads * dvh), BF16),
        compiler_params=_cparams(("parallel",)),
        name="mla_absorb_out",
    )(olat, w_uv3)


def _merge_kernel(a_ref, gb_ref, om_ref, x_ref, wo_ref, gf_ref, x1_ref, h2_ref):
    merged = a_ref[...].astype(F32) + _sigmoid(gb_ref[...].astype(F32)) * om_ref[...].astype(F32)
    x1 = x_ref[...] + jnp.dot(merged.astype(BF16), wo_ref[...], preferred_element_type=F32)
    x1_ref[...] = x1
    h2_ref[...] = _rmsnorm(x1, gf_ref[...]).astype(BF16)


def _merge(branch_a, gates, o_m, x, wo, g_ffn, *, col):
    m, d = x.shape
    tm = _pick(m, (512, 384, 256, 128))
    row = lambda i: (i, 0)
    return pl.pallas_call(
        _merge_kernel,
        grid=(m // tm,),
        in_specs=[pl.BlockSpec((tm, d), row),
                  pl.BlockSpec((tm, d), lambda i: (i, col["gb"] // d)),
                  pl.BlockSpec((tm, d), row),
                  pl.BlockSpec((tm, d), row),
                  pl.BlockSpec(wo.shape, lambda i: (0, 0), pipeline_mode=pl.Buffered(1)),
                  pl.BlockSpec((1, d), lambda i: (0, 0))],
        out_specs=[pl.BlockSpec((tm, d), row), pl.BlockSpec((tm, d), row)],
        out_shape=[jax.ShapeDtypeStruct((m, d), F32), jax.ShapeDtypeStruct((m, d), BF16)],
        compiler_params=_cparams(("parallel",)),
        name="merge_out_proj",
    )(branch_a, gates, o_m, x, wo, g_ffn.reshape(1, -1))


HALO = 8


def _ffn_up_kernel(*refs, bb, r, tf, loc, carried, cast_down):
    (h_ref, wa_ref, wb_ref, cwa_ref, cwb_ref, cba_ref, cbb_ref, ha_ref, hb_ref), refs = \
        refs[:9], refs[9:]
    if cast_down:
        wd_ref, act_ref, ca_ref, cb_ref, wdb_ref, ext_scr, carry_scr, w_scr = refs
    else:
        act_ref, ca_ref, cb_ref, ext_scr, carry_scr, w_scr = refs
    s = pl.program_id(1)
    rt = pl.program_id(2)
    d = h_ref.shape[2]

    @pl.when((s == 0) & (rt == 0))
    def _():
        w_scr[0] = wa_ref[...].astype(BF16)
        w_scr[1] = wb_ref[...].astype(BF16)
        if cast_down:
            wdb_ref[...] = wd_ref[...].astype(BF16)

    if carried:
        @pl.when(rt == 0)
        def _():
            carry_scr[0] = ha_ref[...]
            carry_scr[1] = hb_ref[...]

    h = h_ref[...].reshape(bb * r, d)
    conv = []
    for half, (cw_ref, cbias_ref, hist_ref, cout_ref) in enumerate(
            ((cwa_ref, cba_ref, ha_ref, ca_ref), (cwb_ref, cbb_ref, hb_ref, cb_ref))):
        u = jnp.dot(h, w_scr[half], preferred_element_type=F32).reshape(bb, r, tf)
        ext_scr[half, :, HALO:HALO + r, :] = u
        ext_scr[half, :, HALO - 2:HALO, :] = carry_scr[half] if carried else hist_ref[...]
        u1 = ext_scr[half, :, HALO - 1:HALO - 1 + r, :]
        u2 = ext_scr[half, :, HALO - 2:HALO - 2 + r, :]
        cw = cw_ref[...]
        conv.append(cbias_ref[...] + cw[0:1] * u2 + cw[1:2] * u1 + cw[2:3] * u)
        if carried:
            carry_scr[half] = ext_scr[half, :, HALO + r - 2:HALO + r, :]
        cout_ref[0] = ext_scr[half, :, HALO + loc:HALO + loc + 2, :]

    act_ref[...] = ((conv[0] * _sigmoid(conv[0])) * conv[1]).astype(act_ref.dtype)


def _ffn_down_kernel(act_ref, wd_ref, x1_ref, gf_ref, y_ref):
    down = jnp.dot(act_ref[...], wd_ref[...], preferred_element_type=F32)
    y_ref[...] = _rmsnorm(x1_ref[...] + down, gf_ref[...])


def _ffn(h2, x1, w_up, w_down, conv_w, conv_b, hist, g_final, *, B, T, Tp):
    d = h2.shape[1]
    dff = w_down.shape[0]
    cast_down = w_down.dtype != BF16
    tf = _pick(dff, (512, 256, 128))
    nf = dff // tf
    if Tp <= 128:
        bb, r = B, Tp
    else:
        bb, r = 1, _pick(Tp, (ROW_TILE, 128))
    nrt = Tp // r
    carried = nrt > 1
    loc = (T - 2) - (nrt - 1) * r
    assert 0 <= loc <= r - 2, "final two valid rows must sit in the last row tile"
    kern = functools.partial(_ffn_up_kernel, bb=bb, r=r, tf=tf, loc=loc, carried=carried,
                             cast_down=cast_down)
    carry_shape = (2, bb, 2, tf) if carried else (1, 1, 2, LANE)
    in_specs = [pl.BlockSpec((bb, r, d), lambda f, s, t: (s, t, 0)),
                pl.BlockSpec((d, tf), lambda f, s, t: (0, f)),
                pl.BlockSpec((d, tf), lambda f, s, t: (0, nf + f)),
                pl.BlockSpec((CONV_W, tf), lambda f, s, t: (0, f)),
                pl.BlockSpec((CONV_W, tf), lambda f, s, t: (0, nf + f)),
                pl.BlockSpec((1, tf), lambda f, s, t: (0, f)),
                pl.BlockSpec((1, tf), lambda f, s, t: (0, nf + f)),
                pl.BlockSpec((bb, 2, tf), lambda f, s, t: (s, 0, f)),
                pl.BlockSpec((bb, 2, tf), lambda f, s, t: (s, 0, nf + f))]
    out_specs = [pl.BlockSpec((bb, r, tf), lambda f, s, t: (s, t, f)),
                 pl.BlockSpec((1, bb, 2, tf), lambda f, s, t: (t, s, 0, f)),
                 pl.BlockSpec((1, bb, 2, tf), lambda f, s, t: (t, s, 0, f))]
    out_shape = [jax.ShapeDtypeStruct((B, Tp, dff), BF16),
                 jax.ShapeDtypeStruct((nrt, B, 2, dff), F32),
                 jax.ShapeDtypeStruct((nrt, B, 2, dff), F32)]
    args = [h2.reshape(B, Tp, d), w_up, w_up, conv_w, conv_w, conv_b.reshape(1, -1),
            conv_b.reshape(1, -1), hist, hist]
    if cast_down:
        in_specs.append(pl.BlockSpec((tf, d), lambda f, s, t: (f, 0)))
        out_specs.append(pl.BlockSpec((tf, d), lambda f, s, t: (f, 0)))
        out_shape.append(jax.ShapeDtypeStruct((dff, d), BF16))
        args.append(w_down)
    outs = pl.pallas_call(
        kern,
        grid=(nf, B // bb, nrt),
        in_specs=in_specs,
        out_specs=out_specs,
        out_shape=out_shape,
        scratch_shapes=[pltpu.VMEM((2, bb, HALO + r, tf), F32),
                        pltpu.VMEM(carry_shape, F32),
                        pltpu.VMEM((2, d, tf), BF16)],
        compiler_params=_cparams(("arbitrary", "arbitrary", "arbitrary")),
        name="conv_ffn_up",
    )(*args)
    act, ca, cb = outs[:3]
    if cast_down:
        w_down = outs[3]

    m = B * Tp
    tm = _pick(m, (256, 128))
    y = pl.pallas_call(
        _ffn_down_kernel,
        grid=(m // tm,),
        in_specs=[pl.BlockSpec((tm, dff), lambda i: (i, 0)),
                  pl.BlockSpec((dff, d), lambda i: (0, 0), pipeline_mode=pl.Buffered(1)),
                  pl.BlockSpec((tm, d), lambda i: (i, 0)),
                  pl.BlockSpec((1, d), lambda i: (0, 0))],
        out_specs=pl.BlockSpec((tm, d), lambda i: (i, 0)),
        out_shape=jax.ShapeDtypeStruct((m, d), F32),
        compiler_params=_cparams(("parallel",)),
        name="ffn_down",
    )(act.reshape(m, dff), w_down, x1, g_final.reshape(1, -1))
    return y.reshape(B, Tp, d), jnp.concatenate([ca[nrt - 1], cb[nrt - 1]], axis=-1), w_down


def _rope_tables(pos):
    half = MLA_ROPE // 2
    inv = ROPE_THETA ** (-jnp.arange(0, MLA_ROPE, 2, dtype=F32) / MLA_ROPE)
    ang = pos.astype(F32)[:, None] * inv[None, :]
    cos, sin = jnp.cos(ang), jnp.sin(ang)
    zero = jnp.zeros((pos.shape[0], LANE - 2 * half), F32)
    return (jnp.concatenate([cos, cos, zero], axis=1),
            jnp.concatenate([-sin, sin, zero], axis=1))


def _attn_tile(T):
    return _pick(T, (1024, 128))


def _project(x, pos, w, chunk_tile=0):
    col = w["col"]
    rows = w["in_rows"]
    cos_t, sin_t = _rope_tables(pos)
    h, small, lat, kr = _front(x, w["g_mix"], w["w_in_t"], rows["a"], w["g_kv"], cos_t, sin_t,
                               rank=rows["cq"] - rows["a"], rq=rows["ckv"] - rows["cq"],
                               rk=rows["kpe"] - rows["ckv"])
    qkvr = _matmul_wt(h, w["w_in_t"], rows["q"], rows["a"] - rows["q"], BF16, tn=1024)
    gates = _matmul_wt(h, w["w_in_t"], rows["ga"], rows["end"] - rows["ga"], BF16, tn=1024)
    q = _qprep(small, w["g_q"], w["wq_nope"], w["wq_pe"], w["wq_pe_sw"], cos_t, sin_t, col=col,
               chunk_tile=chunk_tile)
    return dict(qkvr=qkvr, gates=gates, small=small, q=q, lat=lat, kr=kr)


def _finish(x, pr, branch_a, o_m, w, hist, *, B, T):
    x1, h2 = _merge(branch_a, pr["gates"], o_m, x, w["w_o"], w["g_ffn"], col=w["col"])
    y, conv, w["w_down"] = _ffn(h2, x1, w["w_up"], w["w_down"], w["conv_w"], w["conv_b"], hist,
                                w["final_norm"], B=B, T=T, Tp=T)
    return y, conv


def _gla_group(pr, w, s0, *, B, T, row0=0):
    return _gla(pr["qkvr"], pr["gates"], pr["small"], w["wa_pad"], w["b_a"], w["g_gla_out"],
                s0, B=B, T=T, Tp=T, dk=w["dk"], dv=w["dv"], col=w["col"], row0=row0)


def _long_stream(x, pr, w, *, T, s0, hist, prefix):
    branch_a, state = _gla_group(pr, w, s0, B=1, T=T)
    k, vt = _kvup(pr["lat"], pr["kr"], w["w_uk"], w["w_uv_t"], v_transposed=True)
    o_m = _attention_t(pr["q"], k, vt, prefix[0], prefix[1], T=T, t=_attn_tile(T),
                       hps=MLA_HEADS // 8)
    y, conv = _finish(x, pr, branch_a, o_m, w, hist, B=1, T=T)
    return y, pr["lat"], pr["kr"], state, conv


def _short_streams(x, pr, w, *, B, T, past_lat, past_kr, s0_s, hist_s):
    ns = B * T
    dk, dv = w["dk"], w["dv"]

    ba_s, st_s = _gla_group(pr, w, s0_s, B=B, T=T)
    ba_m, st_m = _gla_group(pr, w, jnp.zeros((1, GLA_HEADS, dk, dv), F32), B=1, T=T, row0=ns)

    qlat = _absorb_q(pr["q"], w["w_uk_t3"])
    olat = _attn_latent(qlat, pr["q"], past_lat, jnp.swapaxes(past_kr, 1, 2), pr["lat"],
                        pr["kr"], B=B, T=T)
    om_s = _absorb_out(olat, w["w_uv3"])
    q_m, lat_m, kr_m = pr["q"][:, ns:], pr["lat"][ns:], pr["kr"][ns:]
    k_m, v_m = _kvup(lat_m, kr_m, w["w_uk"], w["w_uv"])
    prefix = _kvup(lat_m, kr_m, w["w_uk"], w["w_uv_t"], v_transposed=True)
    om_m = _attention(q_m, k_m, v_m, B=1, Tq=T, Tk=T, tq=T, tk=T, hps=MLA_HEADS,
                      q_off=0, k_off=0)

    hist = jnp.concatenate([hist_s, jnp.zeros((1,) + hist_s.shape[1:], F32)], axis=0)
    y, conv = _finish(x, pr, jnp.concatenate([ba_s, ba_m], axis=0),
                      jnp.concatenate([om_s, om_m], axis=0), w, hist, B=B + 1, T=T)
    sample = (y[:B], pr["lat"][:ns], pr["kr"][:ns], st_s, conv[:B])
    meta = (lat_m, kr_m, st_m, conv[B:], prefix)
    return sample, meta


def _prep_weights(g_mix, w_in, w_a2, b_a, g_gla_out, g_q, w_uq, g_kv, w_uk, w_uv, w_o,
                  g_ffn, w_up, conv_w, conv_b, w_down, final_norm):
    d = w_in.shape[0]
    rank, gqk = w_a2.shape
    gvw = GLA_HEADS * g_gla_out.shape[0]
    rq, rk = g_q.shape[0], g_kv.shape[0]
    half = MLA_ROPE // 2
    o, offs = 0, {}
    for name, width in (("q", gqk), ("k", gqk), ("v", gvw), ("r", gvw), ("a", rank),
                        ("cq", rq), ("ckv", rk), ("kpe", MLA_ROPE), ("ga", d), ("gb", d)):
        offs[name] = (o, o + width)
        o += width
    assert o == w_in.shape[1]
    in_rows = {name: lo for name, (lo, _) in offs.items()}
    in_rows["end"] = o
    assert all(v % 16 == 0 for v in in_rows.values())
    col = {"q": 0, "k": gqk, "v": 2 * gqk, "r": 2 * gqk + gvw, "ga": 0, "gb": d,
           "cq": 0, "ckv": rq, "kpe": rq + rk, "a": rq + rk + 2 * MLA_ROPE}

    w3 = w_uq.reshape(rq, MLA_HEADS, MLA_NOPE + MLA_ROPE)
    pe = w3[:, :, MLA_NOPE:]
    pe_sw = jnp.concatenate([pe[:, :, half:], pe[:, :, :half]], axis=2)
    zpad = jnp.zeros((rq, MLA_HEADS, LANE - MLA_ROPE), w_uq.dtype)
    flat = lambda t: t.reshape(rq, -1).astype(BF16)
    wa_pad = jnp.concatenate([w_a2, jnp.zeros((LANE - rank, gqk), w_a2.dtype)], axis=0)
    return dict(
        col=col, dk=gqk // GLA_HEADS, dv=g_gla_out.shape[0],
        g_mix=g_mix, w_in_t=jnp.swapaxes(w_in, 0, 1), in_rows=in_rows,
        wa_pad=wa_pad.astype(BF16), b_a=b_a, g_gla_out=g_gla_out, g_q=g_q,
        wq_nope=flat(w3[:, :, :MLA_NOPE]),
        wq_pe=flat(jnp.concatenate([pe, zpad], axis=2)),
        wq_pe_sw=flat(jnp.concatenate([pe_sw, zpad], axis=2)),
        g_kv=g_kv, w_uk=w_uk.astype(BF16), w_uv=w_uv.astype(BF16),
        w_uv_t=w_uv.T.astype(BF16),
        w_uk_t3=w_uk.reshape(rk, MLA_HEADS, MLA_NOPE).transpose(1, 2, 0).astype(BF16),
        w_uv3=w_uv.reshape(rk, MLA_HEADS, MLA_V).transpose(1, 0, 2).astype(BF16),
        w_o=w_o.astype(BF16),
        g_ffn=g_ffn, w_up=w_up, conv_w=conv_w, conv_b=conv_b,
        w_down=w_down, final_norm=final_norm)


def kernel(x_prompt, x_sample, cache_mla_latent, cache_mla_krope, state_gla, cache_ffn_conv,
           meta_tokens, g_mix, w_in, w_a2, b_a, g_gla_out, g_q, w_uq, g_kv, w_uk, w_uv, w_o,
           g_ffn, w_up, conv_w, conv_b, w_down, final_norm):
    assert w_in.shape[0] == 1, "single trunk layer"
    bp, seq, d = x_prompt.shape
    assert bp == 1
    bs, ts, _ = x_sample.shape
    P = cache_mla_latent.shape[2]
    w = _prep_weights(g_mix[0], w_in[0], w_a2[0], b_a[0], g_gla_out[0], g_q[0], w_uq[0],
                      g_kv[0], w_uk[0], w_uv[0], w_o[0], g_ffn[0], w_up[0], conv_w[0],
                      conv_b[0], w_down[0], final_norm)

    n_meta = meta_tokens.shape[0]
    assert n_meta == N_META == ts and seq % CHUNK == 0
    x_short = jnp.concatenate([x_sample.reshape(bs * ts, d), meta_tokens.astype(F32)], axis=0)
    pos_short = jnp.concatenate([jnp.tile(P + jnp.arange(ts, dtype=jnp.int32), bs),
                                 jnp.arange(n_meta, dtype=jnp.int32)])
    pos_long = n_meta + jnp.arange(seq, dtype=jnp.int32)
    pr_short = _project(x_short, pos_short, w)
    pr_long = _project(x_prompt[0], pos_long, w, chunk_tile=_attn_tile(seq))
    (ys, lat_s, kr_s, st_s, cv_s), (lat_m, kr_m, st_m, cv_m, prefix) = _short_streams(
        x_short, pr_short, w, B=bs, T=ts, past_lat=cache_mla_latent[0],
        past_kr=cache_mla_krope[0], s0_s=state_gla[0], hist_s=cache_ffn_conv[0])
    yp, lat_p, kr_p, st_p, cv_p = _long_stream(
        x_prompt[0], pr_long, w, T=seq, s0=st_m, hist=cv_m, prefix=prefix)

    rk = lat_p.shape[1]
    T = n_meta + seq
    return (yp,
            ys,
            jnp.concatenate([lat_m, lat_p], axis=0).reshape(1, 1, T, rk),
            jnp.concatenate([kr_m, kr_p], axis=0)[:, :MLA_ROPE].reshape(1, 1, T, MLA_ROPE),
            st_p[None],
            cv_p[None],
            lat_s.reshape(1, bs, ts, rk),
            kr_s[:, :MLA_ROPE].reshape(1, bs, ts, MLA_ROPE),
            st_s[None],
            cv_s[None])
```

```python
import functools

import jax
import jax.numpy as jnp
from jax import lax
from jax.experimental import pallas as pl
from jax.experimental.pallas import tpu as pltpu

BF16 = jnp.bfloat16
F32 = jnp.float32

CHUNK = 64
CHUNK_SHIFT = 6
N_META = 16
EPS = 1e-6
GLA_HEADS = 4
GLA_GATE_NORM = 16.0
GLA_LOG_ALPHA_MIN = -5.0
MLA_HEADS = 16
MLA_NOPE = 128
MLA_ROPE = 64
MLA_V = 128
ROPE_THETA = 10000.0
CONV_W = 3
NEG_BIG = -1e30
LOG2E = 1.4426950408889634
QK_SCALE_LOG2E = (MLA_NOPE + MLA_ROPE) ** -0.5 * LOG2E

LANE = 128
VT_ONES = 16
GLA_CHUNK = 256
GLA_SEQS = 4
MASK_LANE0 = MLA_ROPE + 1
SAFE_EXP = 64.0
ROW_TILE = 1024
VMEM_LIMIT = 56 * 1024 * 1024


def _cparams(sem, vmem=VMEM_LIMIT):
    return pltpu.CompilerParams(dimension_semantics=sem, vmem_limit_bytes=vmem)


def _rmsnorm(x, g):
    return x * lax.rsqrt(jnp.mean(x * x, axis=-1, keepdims=True) + EPS) * g


def _sigmoid(x):
    return 0.5 * jnp.tanh(0.5 * x) + 0.5


def _pick(n, cands):
    for c in cands:
        if n % c == 0:
            return c
    fits = [t for t in range(16, min(n, max(cands)) + 1, 16) if n % t == 0]
    if not fits:
        raise ValueError(f"no tile in {cands} divides {n}")
    return fits[-1]


_NT = (((1,), (1,)), ((), ()))


def _matmul_wt_kernel(a_ref, w_ref, o_ref, w_scr):
    @pl.when(pl.program_id(1) == 0)
    def _():
        w_scr[...] = w_ref[...].astype(BF16)

    o_ref[...] = lax.dot_general(a_ref[...], w_scr[...], _NT,
                                 preferred_element_type=F32).astype(o_ref.dtype)


def _matmul_wt(a, w_t, row0, n, out_dtype, tn):
    m, k = a.shape
    tm = _pick(m, (ROW_TILE, 512, 384, 128))
    return pl.pallas_call(
        _matmul_wt_kernel,
        grid=(n // tn, m // tm),
        in_specs=[pl.BlockSpec((tm, k), lambda j, i: (i, 0)),
                  pl.BlockSpec((pl.Element(tn), pl.Element(k)),
                               lambda j, i: (pl.multiple_of(row0 + j * tn, 16), 0))],
        out_specs=pl.BlockSpec((tm, tn), lambda j, i: (i, j)),
        out_shape=jax.ShapeDtypeStruct((m, n), out_dtype),
        scratch_shapes=[pltpu.VMEM((tn, k), BF16)],
        compiler_params=_cparams(("parallel", "arbitrary")),
        name="in_proj_wt",
    )(a, w_t)


def _front_kernel(x_ref, g_ref, w_ref, gkv_ref, cos_ref, sin_ref,
                  h_ref, o_ref, lat_ref, kr_ref, w_scr, *, rank, rq, rk):
    @pl.when(pl.program_id(0) == 0)
    def _():
        w = w_ref[...].astype(BF16)
        half = MLA_ROPE // 2
        pe0 = rank + rq + rk
        o_pe = rq + rk
        w_scr[0:rq] = w[rank:rank + rq]
        w_scr[rq:o_pe] = w[rank + rq:pe0]
        w_scr[o_pe:o_pe + MLA_ROPE] = w[pe0:pe0 + MLA_ROPE]
        w_scr[o_pe + MLA_ROPE:o_pe + MLA_ROPE + half] = w[pe0 + half:pe0 + MLA_ROPE]
        w_scr[o_pe + MLA_ROPE + half:o_pe + 2 * MLA_ROPE] = w[pe0:pe0 + half]
        o_a = o_pe + 2 * MLA_ROPE
        w_scr[o_a:o_a + rank] = w[0:rank]
        w_scr[o_a + rank:] = jnp.zeros((w_scr.shape[0] - o_a - rank, w_scr.shape[1]), BF16)

    h = _rmsnorm(x_ref[...], g_ref[...]).astype(BF16)
    h_ref[...] = h
    small = lax.dot_general(h, w_scr[...], _NT, preferred_element_type=F32)
    o_ref[...] = small
    lat_ref[...] = _rmsnorm(small[:, rq:rq + rk], gkv_ref[...])
    blk = small[:, rq + rk:rq + rk + LANE]
    kr_ref[...] = blk * cos_ref[...] + pltpu.roll(blk, LANE // 2, 1) * sin_ref[...]


def _front(x, g_mix, w_t, row0, g_kv, cos_t, sin_t, *, rank, rq, rk):
    m, k = x.shape
    n_in = rank + rq + rk + MLA_ROPE
    n_out = rq + rk + 2 * MLA_ROPE + LANE
    tm = _pick(m, (512, 384, 128))
    kern = functools.partial(_front_kernel, rank=rank, rq=rq, rk=rk)
    row = lambda i: (i, 0)
    return pl.pallas_call(
        kern,
        grid=(m // tm,),
        in_specs=[pl.BlockSpec((tm, k), row),
                  pl.BlockSpec((1, k), lambda i: (0, 0)),
                  pl.BlockSpec((pl.Element(n_in), pl.Element(k)), lambda i: (row0, 0),
                               pipeline_mode=pl.Buffered(1)),
                  pl.BlockSpec((1, rk), lambda i: (0, 0)),
                  pl.BlockSpec((tm, LANE), row),
                  pl.BlockSpec((tm, LANE), row)],
        out_specs=[pl.BlockSpec((tm, k), row),
                   pl.BlockSpec((tm, n_out), row),
                   pl.BlockSpec((tm, rk), row),
                   pl.BlockSpec((tm, LANE), row)],
        out_shape=[jax.ShapeDtypeStruct((m, k), BF16),
                   jax.ShapeDtypeStruct((m, n_out), F32),
                   jax.ShapeDtypeStruct((m, rk), F32),
                   jax.ShapeDtypeStruct((m, LANE), F32)],
        scratch_shapes=[pltpu.VMEM((n_out, k), BF16)],
        compiler_params=_cparams(("arbitrary",)),
        name="front_proj",
    )(x, g_mix.reshape(1, -1), w_t, g_kv.reshape(1, -1), cos_t, sin_t)


def _split3(x):
    a = x.astype(BF16)
    r1 = x - a.astype(F32)
    b = r1.astype(BF16)
    c = (r1 - b.astype(F32)).astype(BF16)
    return a, b, c


def _gla_kernel(q_ref, k_ref, v_ref, r_ref, ga_ref, a_ref, wa_ref, ba_ref, go_ref, s0_ref,
                o_ref, sout_ref, s_scr, *, C, SB, T, H, dk, dv, S):
    c_idx = pl.program_id(1)
    n_chunks = pl.num_programs(1)
    R = S * C

    @pl.when(c_idx == 0)
    def _():
        s_scr[...] = s0_ref[...]

    z = jnp.dot(a_ref[...].astype(BF16), wa_ref[...], preferred_element_type=F32) + ba_ref[...]
    log_sig = jnp.minimum(z, 0.0) - jnp.log(1.0 + jnp.exp(-jnp.abs(z)))
    la = jnp.maximum(log_sig * (1.0 / GLA_GATE_NORM), GLA_LOG_ALPHA_MIN)
    if T % C:
        rows = c_idx * C + lax.broadcasted_iota(jnp.int32, (C, 1), 0)
        la = jnp.where(rows < T, la, 0.0)

    ri = lax.broadcasted_iota(jnp.int32, (R, R), 0)
    ci = lax.broadcasted_iota(jnp.int32, (R, R), 1)
    same_seq = (ri >= ci) if S == 1 else ((ri >= ci) & (ri - ci <= lax.rem(ri, C)))
    tri = jnp.where(same_seq, 1.0, 0.0).astype(BF16)
    ones = jnp.ones((C, LANE), BF16)
    cs_all = jnp.zeros_like(la)
    dsum_all = [jnp.zeros((la.shape[1], LANE), F32) for _ in range(S)]
    for piece in _split3(la):
        cs_all = cs_all + jnp.dot(tri, piece, preferred_element_type=F32)
        for si in range(S):
            dsum_all[si] = dsum_all[si] + lax.dot_general(
                piece[si * C:(si + 1) * C], ones, (((0,), (0,)), ((), ())),
                preferred_element_type=F32)

    sr = lax.broadcasted_iota(jnp.int32, (SB, SB), 0)
    sc = lax.broadcasted_iota(jnp.int32, (SB, SB), 1)
    causal = sr >= sc
    nt = (((1,), (1,)), ((), ()))
    scale = dk ** -0.5

    for si, h in [(si, h) for si in range(S) for h in range(H)]:
        rs = slice(si * C, (si + 1) * C)
        ksl = slice(h * dk, (h + 1) * dk)
        vsl = slice(h * dv, (h + 1) * dv)
        cs = cs_all[rs, ksl]
        c_last = cs[C - 1:C, :]
        q = q_ref[rs, ksl].astype(F32) * scale
        k = k_ref[rs, ksl].astype(F32)
        v = v_ref[rs, vsl]
        s_old = s_scr[si, h]

        o_inter = jnp.dot((q * jnp.exp(cs)).astype(BF16), s_old.astype(BF16),
                          preferred_element_type=F32)
        k_end = (k * jnp.exp(c_last - cs)).astype(BF16)
        upd = lax.dot_general(k_end, v, (((0,), (0,)), ((), ())), preferred_element_type=F32)
        dcol = jnp.exp(dsum_all[si][ksl, :])
        s_scr[si, h] = jnp.concatenate([dcol] * (dv // LANE), axis=1) * s_old + upd

        outs = []
        for i in range(C // SB):
            lo = i * SB
            cs_i = cs[lo:lo + SB]
            q_i = q[lo:lo + SB]
            k_i = k[lo:lo + SB]
            start = cs[lo - 1:lo] if i > 0 else jnp.zeros_like(c_last)
            mid = 0.5 * (start + cs[lo + SB - 1:lo + SB])
            qd = (q_i * jnp.exp(cs_i - mid)).astype(BF16)
            kd = (k_i * jnp.exp(mid - cs_i)).astype(BF16)
            att = lax.dot_general(qd, kd, nt, preferred_element_type=F32)
            att = jnp.where(causal, att, 0.0)
            o_i = jnp.dot(att.astype(BF16), v[lo:lo + SB], preferred_element_type=F32)
            if i > 0:
                qo = (q_i * jnp.exp(cs_i - start)).astype(BF16)
                ko = (k[:lo] * jnp.exp(start - cs[:lo])).astype(BF16)
                att_o = lax.dot_general(qo, ko, nt, preferred_element_type=F32)
                o_i = o_i + jnp.dot(att_o.astype(BF16), v[:lo], preferred_element_type=F32)
            outs.append(o_i)
        o = o_inter + (jnp.concatenate(outs, axis=0) if len(outs) > 1 else outs[0])

        on = _rmsnorm(o, go_ref[...])
        r = r_ref[rs, vsl].astype(F32)
        g = ga_ref[rs, vsl].astype(F32)
        o_ref[rs, vsl] = (_sigmoid(g) * (on * (r * _sigmoid(r)))).astype(o_ref.dtype)

    @pl.when(c_idx == n_chunks - 1)
    def _():
        sout_ref[...] = s_scr[...]


def _gla(qkvr, gates, small, wa_pad, b_a, g_out, s0, *, B, T, Tp, dk, dv, col, row0=0):
    C = min(GLA_CHUNK, Tp)
    SB = min(32, C)
    assert Tp % C == 0 and C % SB == 0
    nc = Tp // C
    H = GLA_HEADS
    qk, vw = H * dk, H * dv
    S = _pick(B, (GLA_SEQS, 1)) if nc == 1 else 1
    R = S * C
    assert row0 % R == 0
    rb = lambda b, c: row0 // R + b * nc + c
    kern = functools.partial(_gla_kernel, C=C, SB=SB, T=T, H=H, dk=dk, dv=dv, S=S)
    return pl.pallas_call(
        kern,
        grid=(B // S, nc),
        in_specs=[
            pl.BlockSpec((R, qk), lambda b, c: (rb(b, c), col["q"] // qk)),
            pl.BlockSpec((R, qk), lambda b, c: (rb(b, c), col["k"] // qk)),
            pl.BlockSpec((R, vw), lambda b, c: (rb(b, c), col["v"] // vw)),
            pl.BlockSpec((R, vw), lambda b, c: (rb(b, c), col["r"] // vw)),
            pl.BlockSpec((R, vw), lambda b, c: (rb(b, c), col["ga"] // vw)),
            pl.BlockSpec((R, LANE), lambda b, c: (rb(b, c), col["a"] // LANE)),
            pl.BlockSpec((LANE, qk), lambda b, c: (0, 0)),
            pl.BlockSpec((1, qk), lambda b, c: (0, 0)),
            pl.BlockSpec((1, dv), lambda b, c: (0, 0)),
            pl.BlockSpec((S, H, dk, dv), lambda b, c: (b, 0, 0, 0)),
        ],
        out_specs=[
            pl.BlockSpec((R, vw), lambda b, c: (b * nc + c, 0)),
            pl.BlockSpec((S, H, dk, dv), lambda b, c: (b, 0, 0, 0)),
        ],
        out_shape=[jax.ShapeDtypeStruct((B * Tp, vw), BF16),
                   jax.ShapeDtypeStruct((B, H, dk, dv), F32)],
        scratch_shapes=[pltpu.VMEM((S, H, dk, dv), F32)],
        compiler_params=_cparams(("parallel", "arbitrary")),
        name="gla",
    )(qkvr, qkvr, qkvr, qkvr, gates, small, wa_pad, b_a.reshape(1, -1), g_out.reshape(1, -1), s0)


def _qprep_kernel(cq_ref, gq_ref, wn_ref, wp_ref, wps_ref, cos_ref, sin_ref, q_ref, *,
                  chunk_tile):
    hq = _rmsnorm(cq_ref[...], gq_ref[...]).astype(BF16)
    qn = jnp.dot(hq, wn_ref[...], preferred_element_type=F32)
    qp = jnp.dot(hq, wp_ref[...], preferred_element_type=F32)
    qs = jnp.dot(hq, wps_ref[...], preferred_element_type=F32)
    cos = cos_ref[...] * QK_SCALE_LOG2E
    sin = sin_ref[...] * QK_SCALE_LOG2E
    tag = 0.0
    if chunk_tile:
        tm = cos.shape[0]
        row = pl.program_id(0) * tm + lax.broadcasted_iota(jnp.int32, (tm, LANE), 0)
        lane = lax.broadcasted_iota(jnp.int32, (tm, LANE), 1)
        chunk = (row & (chunk_tile - 1)) >> CHUNK_SHIFT
        tag = jnp.where(lane - MASK_LANE0 == chunk, 1.0, 0.0)
    for h in range(MLA_HEADS):
        sl = slice(h * LANE, (h + 1) * LANE)
        q_ref[h, :, 0:LANE] = (qn[:, sl] * QK_SCALE_LOG2E).astype(BF16)
        q_ref[h, :, LANE:2 * LANE] = (qp[:, sl] * cos + qs[:, sl] * sin + tag).astype(BF16)


def _qprep(small, g_q, wn, wp, wps, cos_t, sin_t, *, col, chunk_tile=0):
    m = small.shape[0]
    rq = wn.shape[0]
    tm = _pick(m, (512, 256, 128))
    full = lambda i: (0, 0)
    return pl.pallas_call(
        functools.partial(_qprep_kernel, chunk_tile=chunk_tile),
        grid=(m // tm,),
        in_specs=[pl.BlockSpec((tm, rq), lambda i: (i, col["cq"] // rq)),
                  pl.BlockSpec((1, rq), full),
                  pl.BlockSpec(wn.shape, full),
                  pl.BlockSpec(wp.shape, full),
                  pl.BlockSpec(wps.shape, full),
                  pl.BlockSpec((tm, LANE), lambda i: (i, 0)),
                  pl.BlockSpec((tm, LANE), lambda i: (i, 0))],
        out_specs=pl.BlockSpec((MLA_HEADS, tm, 2 * LANE), lambda i: (0, i, 0)),
        out_shape=jax.ShapeDtypeStruct((MLA_HEADS, m, 2 * LANE), BF16),
        compiler_params=_cparams(("parallel",)),
        name="mla_q",
    )(small, g_q.reshape(1, -1), wn, wp, wps, cos_t, sin_t)


def _kvup_kernel(lat_ref, kr_ref, wuk_ref, wuv_ref, k_ref, v_ref, *, v_transposed):
    lat = lat_ref[...].astype(BF16)
    kn = jnp.dot(lat, wuk_ref[...], preferred_element_type=F32)
    kr = kr_ref[...]
    lane = lax.broadcasted_iota(jnp.int32, kr.shape, 1)
    kp = jnp.where(lane == MLA_ROPE, 1.0, kr).astype(BF16)
    if v_transposed:
        vv = lax.dot_general(wuv_ref[...], lat, (((1,), (1,)), ((), ())),
                             preferred_element_type=F32)
    else:
        vv = jnp.dot(lat, wuv_ref[...], preferred_element_type=F32)
    for h in range(MLA_HEADS):
        sl = slice(h * LANE, (h + 1) * LANE)
        k_ref[h, :, 0:LANE] = kn[:, sl].astype(BF16)
        k_ref[h, :, LANE:2 * LANE] = kp
        if v_transposed:
            v_ref[h, 0:LANE, :] = vv[sl, :].astype(BF16)
            v_ref[h, LANE:LANE + VT_ONES, :] = jnp.ones((VT_ONES, vv.shape[1]), BF16)
        else:
            v_ref[h] = vv[:, sl].astype(BF16)


def _kvup(lat, kr, wuk, wuv, *, v_transposed=False):
    m, rk = lat.shape
    tm = _pick(m, (512, 256, 128))
    full = lambda i: (0, 0)
    if v_transposed:
        v_spec = pl.BlockSpec((MLA_HEADS, LANE + VT_ONES, tm), lambda i: (0, 0, i))
        v_shape = (MLA_HEADS, LANE + VT_ONES, m)
    else:
        v_spec = pl.BlockSpec((MLA_HEADS, tm, LANE), lambda i: (0, i, 0))
        v_shape = (MLA_HEADS, m, LANE)
    return pl.pallas_call(
        functools.partial(_kvup_kernel, v_transposed=v_transposed),
        grid=(m // tm,),
        in_specs=[pl.BlockSpec((tm, rk), lambda i: (i, 0)),
                  pl.BlockSpec((tm, LANE), lambda i: (i, 0)),
                  pl.BlockSpec(wuk.shape, full),
                  pl.BlockSpec(wuv.shape, full)],
        out_specs=[pl.BlockSpec((MLA_HEADS, tm, 2 * LANE), lambda i: (0, i, 0)), v_spec],
        out_shape=[jax.ShapeDtypeStruct((MLA_HEADS, m, 2 * LANE), BF16),
                   jax.ShapeDtypeStruct(v_shape, BF16)],
        compiler_params=_cparams(("parallel",)),
        name="mla_kv",
    )(lat, kr, wuk, wuv)


def _last_kblock(qi, *, tq, tk, nk, q_off, k_off):
    top_chunk = ((qi + 1) * tq - 1 + q_off) // CHUNK
    last_key = (top_chunk + 1) * CHUNK - 1 - k_off
    return jnp.minimum(last_key // tk, nk - 1)


def _attn_kernel(q_ref, k_ref, v_ref, o_ref, m_scr, l_scr, acc_scr, *, hps, tq, tk, nk,
                 q_off, k_off):
    qi = pl.program_id(2)
    ki = pl.program_id(3)

    @pl.when(ki == 0)
    def _():
        m_scr[...] = jnp.full(m_scr.shape, NEG_BIG, F32)
        l_scr[...] = jnp.zeros(l_scr.shape, F32)
        acc_scr[...] = jnp.zeros(acc_scr.shape, F32)

    @pl.when(ki <= _last_kblock(qi, tq=tq, tk=tk, nk=nk, q_off=q_off, k_off=k_off))
    def _():
        q_chunk = (qi * tq + q_off + lax.broadcasted_iota(jnp.int32, (tq, 1), 0)) >> CHUNK_SHIFT
        k_chunk = (ki * tk + k_off + lax.broadcasted_iota(jnp.int32, (1, tk), 1)) >> CHUNK_SHIFT
        visible = q_chunk >= k_chunk

        def head(h, carry):
            s = lax.dot_general(q_ref[h], k_ref[h], (((1,), (1,)), ((), ())),
                                preferred_element_type=F32)
            s = jnp.where(visible, s, NEG_BIG)
            m_prev = m_scr[h]
            m_new = jnp.maximum(m_prev, jnp.max(s, axis=-1, keepdims=True))
            p = jnp.exp2(s - m_new)
            alpha = jnp.exp2(m_prev - m_new)
            l_scr[h] = alpha * l_scr[h] + jnp.sum(p, axis=-1, keepdims=True)
            acc_scr[h] = alpha * acc_scr[h] + jnp.dot(p.astype(BF16), v_ref[h],
                                                      preferred_element_type=F32)
            m_scr[h] = m_new
            return carry

        lax.fori_loop(0, hps, head, 0)

    @pl.when(ki == nk - 1)
    def _():
        for h in range(hps):
            o_ref[:, h * LANE:(h + 1) * LANE] = (acc_scr[h] / l_scr[h]).astype(o_ref.dtype)


def _attention(q, k, v, *, B, Tq, Tk, tq, tk, hps, q_off, k_off):
    nq = Tq // tq
    nk = Tk // tk
    hg = MLA_HEADS // hps
    dqk = q.shape[2]
    dvh = v.shape[2]
    last = functools.partial(_last_kblock, tq=tq, tk=tk, nk=nk, q_off=q_off, k_off=k_off)
    kern = functools.partial(_attn_kernel, hps=hps, tq=tq, tk=tk, nk=nk, q_off=q_off,
                             k_off=k_off)
    kv_row = lambda b, g, i, j: b * nk + jnp.minimum(j, last(i))
    return pl.pallas_call(
        kern,
        grid=(B, hg, nq, nk),
        in_specs=[pl.BlockSpec((hps, tq, dqk), lambda b, g, i, j: (g, b * nq + i, 0)),
                  pl.BlockSpec((hps, tk, dqk), lambda b, g, i, j: (g, kv_row(b, g, i, j), 0)),
                  pl.BlockSpec((hps, tk, dvh), lambda b, g, i, j: (g, kv_row(b, g, i, j), 0))],
        out_specs=pl.BlockSpec((tq, hps * dvh), lambda b, g, i, j: (b * nq + i, g)),
        out_shape=jax.ShapeDtypeStruct((B * Tq, MLA_HEADS * dvh), BF16),
        scratch_shapes=[pltpu.VMEM((hps, tq, 1), F32),
                        pltpu.VMEM((hps, tq, 1), F32),
                        pltpu.VMEM((hps, tq, dvh), F32)],
        compiler_params=_cparams(("parallel", "parallel", "parallel", "arbitrary")),
        name="mla_attn",
    )(q, k, v)


def _attn_t_kernel(qi_ref, ki_ref, q_ref, k_ref, vt_ref, kp_ref, vtp_ref, o_ref,
                   q_scr, r_scr, acc_scr, redo_scr, *, hps, t):
    pair = pl.program_id(1)
    qi = qi_ref[pair]
    ki = ki_ref[pair]
    nt = (((1,), (1,)), ((), ()))
    pe = slice(LANE, 2 * LANE)
    lane = lax.broadcasted_iota(jnp.int32, (t, LANE), 1)

    def set_reference(h, r):
        neg_r = jnp.transpose(jnp.broadcast_to(-r, (LANE, t)))
        q_scr[h, :, pe] = jnp.where(lane == MLA_ROPE, neg_r.astype(BF16), q_ref[h, :, pe])
        r_scr[h] = r

    def shifted_scores(h, own_tile=False):
        k = k_ref[h]
        if own_tile:
            ahead = lane - MASK_LANE0
            k_chunk = lax.broadcasted_iota(jnp.int32, (t, LANE), 0) >> CHUNK_SHIFT
            hidden = (ahead >= 0) & (ahead < k_chunk)
            k = jnp.concatenate(
                [k[:, 0:LANE], jnp.where(hidden, jnp.asarray(NEG_BIG, BF16), k[:, pe])], axis=1)
        return lax.dot_general(k, q_scr[h], nt, preferred_element_type=F32)

    @pl.when(ki == 0)
    def _():
        for h in range(hps):
            q_scr[h, :, 0:LANE] = q_ref[h, :, 0:LANE]
            s = lax.dot_general(kp_ref[h], q_ref[h], nt, preferred_element_type=F32)
            r = jnp.max(s, axis=0, keepdims=True).astype(BF16).astype(F32)
            p = jnp.exp2((s - r).astype(BF16))
            acc_scr[h] = jnp.dot(vtp_ref[h], p, preferred_element_type=F32)
            set_reference(h, r)

    def general(h, own_tile):
        sp = shifted_scores(h, own_tile)
        r = r_scr[h]
        rise = jnp.maximum(jnp.max(sp, axis=0, keepdims=True), 0.0)
        r_new = (r + rise).astype(BF16).astype(F32)
        delta = r_new - r
        p = jnp.exp2((sp - delta).astype(BF16))
        acc_scr[h] = jnp.exp2(-delta) * acc_scr[h] + jnp.dot(vt_ref[h], p,
                                                               preferred_element_type=F32)
        if not own_tile:
            set_reference(h, r_new)

    def tile_step(own_tile):
        unsafe = []
        sp_next = shifted_scores(0, own_tile)
        for h in range(hps):
            sp = sp_next
            if h + 1 < hps:
                sp_next = shifted_scores(h + 1, own_tile)
            safe = jnp.max(sp) <= SAFE_EXP
            part = jnp.dot(vt_ref[h], jnp.exp2(sp.astype(BF16)), preferred_element_type=F32)
            acc_scr[h] += jnp.where(safe, part, 0.0)
            unsafe.append(jnp.logical_not(safe))
            redo_scr[h] = unsafe[-1].astype(jnp.int32)

        @pl.when(functools.reduce(jnp.logical_or, unsafe))
        def _():
            def redo(h, carry):
                @pl.when(redo_scr[h] != 0)
                def _():
                    general(h, own_tile)
                return carry
            lax.fori_loop(0, hps, redo, 0)

    @pl.when(ki < qi)
    def _():
        tile_step(False)

    @pl.when(ki == qi)
    def _():
        tile_step(True)
        for h in range(hps):
            acc = acc_scr[h]
            o_t = acc[0:LANE] / acc[LANE:LANE + 1]
            o_ref[:, h * LANE:(h + 1) * LANE] = o_t.T.astype(o_ref.dtype)


def _attention_t(q, k, vt, k_pre, vt_pre, *, T, t, hps):
    n = T // t
    hg = MLA_HEADS // hps
    dqk = q.shape[2]
    npre = k_pre.shape[1]
    vrows = vt.shape[1]
    pairs = [(i, j) for i in range(n) for j in range(i + 1)]
    qi_arr = jnp.asarray([p[0] for p in pairs], jnp.int32)
    ki_arr = jnp.asarray([p[1] for p in pairs], jnp.int32)
    kern = functools.partial(_attn_t_kernel, hps=hps, t=t)
    grid_spec = pltpu.PrefetchScalarGridSpec(
        num_scalar_prefetch=2,
        grid=(hg, len(pairs)),
        in_specs=[pl.BlockSpec((hps, t, dqk), lambda g, p, qi, ki: (g, qi[p], 0)),
                  pl.BlockSpec((hps, t, dqk), lambda g, p, qi, ki: (g, ki[p], 0)),
                  pl.BlockSpec((hps, vrows, t), lambda g, p, qi, ki: (g, 0, ki[p])),
                  pl.BlockSpec((hps, npre, dqk), lambda g, p, qi, ki: (g, 0, 0)),
                  pl.BlockSpec((hps, vrows, npre), lambda g, p, qi, ki: (g, 0, 0))],
        out_specs=pl.BlockSpec((t, hps * LANE), lambda g, p, qi, ki: (qi[p], g)),
        scratch_shapes=[pltpu.VMEM((hps, t, dqk), BF16),
                        pltpu.VMEM((hps, 1, t), F32),
                        pltpu.VMEM((hps, vrows, t), F32),
                        pltpu.SMEM((hps,), jnp.int32)])
    return pl.pallas_call(
        kern,
        grid_spec=grid_spec,
        out_shape=jax.ShapeDtypeStruct((T, MLA_HEADS * LANE), BF16),
        compiler_params=_cparams(("parallel", "arbitrary")),
        name="mla_attn_t",
    )(qi_arr, ki_arr, q, k, vt, k_pre, vt_pre)


def _absorb_q_kernel(q_ref, w_ref, o_ref):
    o_ref[0] = jnp.dot(q_ref[0, :, 0:MLA_NOPE], w_ref[0],
                       preferred_element_type=F32).astype(o_ref.dtype)


def _absorb_q(q, w_uk_t3):
    heads, rows, dqk = q.shape
    rk = w_uk_t3.shape[2]
    return pl.pallas_call(
        _absorb_q_kernel,
        grid=(heads,),
        in_specs=[pl.BlockSpec((1, rows, dqk), lambda h: (h, 0, 0)),
                  pl.BlockSpec((1, MLA_NOPE, rk), lambda h: (h, 0, 0))],
        out_specs=pl.BlockSpec((1, rows, rk), lambda h: (h, 0, 0)),
        out_shape=jax.ShapeDtypeStruct((heads, rows, rk), BF16),
        compiler_params=_cparams(("parallel",)),
        name="mla_absorb_q",
    )(q, w_uk_t3)


def _attn_latent_kernel(ql_ref, q_ref, plat_ref, pkr_ref, lat_ref, kr_ref, o_ref, *, T, P, S):
    heads, _, rk = ql_ref.shape
    rows = heads * T
    nt = (((1,), (1,)), ((), ()))
    tok = lax.rem(lax.broadcasted_iota(jnp.int32, (rows, 1), 0), T)
    q_chunk = (P + tok) >> CHUNK_SHIFT
    k_chunk = lax.broadcasted_iota(jnp.int32, (1, P + T), 1) >> CHUNK_SHIFT
    visible = q_chunk >= k_chunk
    for si in range(S):
        ts = slice(si * T, (si + 1) * T)
        ql = ql_ref[:, ts, :].reshape(rows, rk)
        qpe = q_ref[:, ts, LANE:2 * LANE].reshape(rows, LANE)[:, 0:MLA_ROPE]
        lat_all = jnp.concatenate([plat_ref[si].astype(BF16), lat_ref[ts, :].astype(BF16)],
                                  axis=0)
        s_pe = jnp.concatenate(
            [jnp.dot(qpe, pkr_ref[si].astype(BF16), preferred_element_type=F32),
             lax.dot_general(qpe, kr_ref[ts, 0:MLA_ROPE].astype(BF16), nt,
                             preferred_element_type=F32)], axis=1)
        s = lax.dot_general(ql, lat_all, nt, preferred_element_type=F32) + s_pe
        s = jnp.where(visible, s, NEG_BIG)
        p = jnp.exp2(s - jnp.max(s, axis=-1, keepdims=True))
        o = jnp.dot(p.astype(BF16), lat_all, preferred_element_type=F32)
        o = o / jnp.sum(p, axis=-1, keepdims=True)
        o_ref[:, ts, :] = o.reshape(heads, T, rk).astype(o_ref.dtype)


def _attn_latent(qlat, q, past_lat, past_kr_t, lat, kr, *, B, T):
    heads, _, rk = qlat.shape
    P = past_lat.shape[1]
    S = _pick(B, (2, 1))
    kern = functools.partial(_attn_latent_kernel, T=T, P=P, S=S)
    return pl.pallas_call(
        kern,
        grid=(B // S,),
        in_specs=[pl.BlockSpec((heads, S * T, rk), lambda b: (0, b, 0)),
                  pl.BlockSpec((heads, S * T, q.shape[2]), lambda b: (0, b, 0)),
                  pl.BlockSpec((S, P, rk), lambda b: (b, 0, 0)),
                  pl.BlockSpec((S, past_kr_t.shape[1], P), lambda b: (b, 0, 0)),
                  pl.BlockSpec((S * T, rk), lambda b: (b, 0)),
                  pl.BlockSpec((S * T, LANE), lambda b: (b, 0))],
        out_specs=pl.BlockSpec((heads, S * T, rk), lambda b: (0, b, 0)),
        out_shape=jax.ShapeDtypeStruct((heads, B * T, rk), BF16),
        compiler_params=_cparams(("parallel",)),
        name="mla_attn_latent",
    )(qlat, q, past_lat, past_kr_t, lat, kr)


def _absorb_out_kernel(o_ref, w_ref, out_ref):
    out_ref[...] = jnp.dot(o_ref[0], w_ref[0], preferred_element_type=F32).astype(out_ref.dtype)


def _absorb_out(olat, w_uv3):
    heads, rows, rk = olat.shape
    dvh = w_uv3.shape[2]
    return pl.pallas_call(
        _absorb_out_kernel,
        grid=(heads,),
        in_specs=[pl.BlockSpec((1, rows, rk), lambda h: (h, 0, 0)),
                  pl.BlockSpec((1, rk, dvh), lambda h: (h, 0, 0))],
        out_specs=pl.BlockSpec((rows, dvh), lambda h: (0, h)),
        out_shape=jax.ShapeDtypeStruct((rows, heads * dvh), BF16),
        compiler_params=_cparams(("parallel",)),
        name="mla_absorb_out",
    )(olat, w_uv3)


def _merge_kernel(a_ref, gb_ref, om_ref, x_ref, wo_ref, gf_ref, x1_ref, h2_ref):
    merged = a_ref[...].astype(F32) + _sigmoid(gb_ref[...].astype(F32)) * om_ref[...].astype(F32)
    x1 = x_ref[...] + jnp.dot(merged.astype(BF16), wo_ref[...], preferred_element_type=F32)
    x1_ref[...] = x1
    h2_ref[...] = _rmsnorm(x1, gf_ref[...]).astype(BF16)


def _merge(branch_a, gates, o_m, x, wo, g_ffn, *, col):
    m, d = x.shape
    tm = _pick(m, (512, 384, 256, 128))
    row = lambda i: (i, 0)
    return pl.pallas_call(
        _merge_kernel,
        grid=(m // tm,),
        in_specs=[pl.BlockSpec((tm, d), row),
                  pl.BlockSpec((tm, d), lambda i: (i, col["gb"] // d)),
                  pl.BlockSpec((tm, d), row),
                  pl.BlockSpec((tm, d), row),
                  pl.BlockSpec(wo.shape, lambda i: (0, 0), pipeline_mode=pl.Buffered(1)),
                  pl.BlockSpec((1, d), lambda i: (0, 0))],
        out_specs=[pl.BlockSpec((tm, d), row), pl.BlockSpec((tm, d), row)],
        out_shape=[jax.ShapeDtypeStruct((m, d), F32), jax.ShapeDtypeStruct((m, d), BF16)],
        compiler_params=_cparams(("parallel",)),
        name="merge_out_proj",
    )(branch_a, gates, o_m, x, wo, g_ffn.reshape(1, -1))


HALO = 8


def _ffn_up_kernel(*refs, bb, r, tf, loc, carried, cast_down):
    (h_ref, wa_ref, wb_ref, cwa_ref, cwb_ref, cba_ref, cbb_ref, ha_ref, hb_ref), refs = \
        refs[:9], refs[9:]
    if cast_down:
        wd_ref, act_ref, ca_ref, cb_ref, wdb_ref, ext_scr, carry_scr, w_scr = refs
    else:
        act_ref, ca_ref, cb_ref, ext_scr, carry_scr, w_scr = refs
    s = pl.program_id(1)
    rt = pl.program_id(2)
    d = h_ref.shape[2]

    @pl.when((s == 0) & (rt == 0))
    def _():
        w_scr[0] = wa_ref[...].astype(BF16)
        w_scr[1] = wb_ref[...].astype(BF16)
        if cast_down:
            wdb_ref[...] = wd_ref[...].astype(BF16)

    if carried:
        @pl.when(rt == 0)
        def _():
            carry_scr[0] = ha_ref[...]
            carry_scr[1] = hb_ref[...]

    h = h_ref[...].reshape(bb * r, d)
    conv = []
    for half, (cw_ref, cbias_ref, hist_ref, cout_ref) in enumerate(
            ((cwa_ref, cba_ref, ha_ref, ca_ref), (cwb_ref, cbb_ref, hb_ref, cb_ref))):
        u = jnp.dot(h, w_scr[half], preferred_element_type=F32).reshape(bb, r, tf)
        ext_scr[half, :, HALO:HALO + r, :] = u
        ext_scr[half, :, HALO - 2:HALO, :] = carry_scr[half] if carried else hist_ref[...]
        u1 = ext_scr[half, :, HALO - 1:HALO - 1 + r, :]
        u2 = ext_scr[half, :, HALO - 2:HALO - 2 + r, :]
        cw = cw_ref[...]
        conv.append(cbias_ref[...] + cw[0:1] * u2 + cw[1:2] * u1 + cw[2:3] * u)
        if carried:
            carry_scr[half] = ext_scr[half, :, HALO + r - 2:HALO + r, :]
        cout_ref[0] = ext_scr[half, :, HALO + loc:HALO + loc + 2, :]

    act_ref[...] = ((conv[0] * _sigmoid(conv[0])) * conv[1]).astype(act_ref.dtype)


def _ffn_down_kernel(act_ref, wd_ref, x1_ref, gf_ref, y_ref):
    down = jnp.dot(act_ref[...], wd_ref[...], preferred_element_type=F32)
    y_ref[...] = _rmsnorm(x1_ref[...] + down, gf_ref[...])


def _ffn(h2, x1, w_up, w_down, conv_w, conv_b, hist, g_final, *, B, T, Tp):
    d = h2.shape[1]
    dff = w_down.shape[0]
    cast_down = w_down.dtype != BF16
    tf = _pick(dff, (512, 256, 128))
    nf = dff // tf
    if Tp <= 128:
        bb, r = B, Tp
    else:
        bb, r = 1, _pick(Tp, (ROW_TILE, 128))
    nrt = Tp // r
    carried = nrt > 1
    loc = (T - 2) - (nrt - 1) * r
    assert 0 <= loc <= r - 2, "final two valid rows must sit in the last row tile"
    kern = functools.partial(_ffn_up_kernel, bb=bb, r=r, tf=tf, loc=loc, carried=carried,
                             cast_down=cast_down)
    carry_shape = (2, bb, 2, tf) if carried else (1, 1, 2, LANE)
    in_specs = [pl.BlockSpec((bb, r, d), lambda f, s, t: (s, t, 0)),
                pl.BlockSpec((d, tf), lambda f, s, t: (0, f)),
                pl.BlockSpec((d, tf), lambda f, s, t: (0, nf + f)),
                pl.BlockSpec((CONV_W, tf), lambda f, s, t: (0, f)),
                pl.BlockSpec((CONV_W, tf), lambda f, s, t: (0, nf + f)),
                pl.BlockSpec((1, tf), lambda f, s, t: (0, f)),
                pl.BlockSpec((1, tf), lambda f, s, t: (0, nf + f)),
                pl.BlockSpec((bb, 2, tf), lambda f, s, t: (s, 0, f)),
                pl.BlockSpec((bb, 2, tf), lambda f, s, t: (s, 0, nf + f))]
    out_specs = [pl.BlockSpec((bb, r, tf), lambda f, s, t: (s, t, f)),
                 pl.BlockSpec((1, bb, 2, tf), lambda f, s, t: (t, s, 0, f)),
                 pl.BlockSpec((1, bb, 2, tf), lambda f, s, t: (t, s, 0, f))]
    out_shape = [jax.ShapeDtypeStruct((B, Tp, dff), BF16),
                 jax.ShapeDtypeStruct((nrt, B, 2, dff), F32),
                 jax.ShapeDtypeStruct((nrt, B, 2, dff), F32)]
    args = [h2.reshape(B, Tp, d), w_up, w_up, conv_w, conv_w, conv_b.reshape(1, -1),
            conv_b.reshape(1, -1), hist, hist]
    if cast_down:
        in_specs.append(pl.BlockSpec((tf, d), lambda f, s, t: (f, 0)))
        out_specs.append(pl.BlockSpec((tf, d), lambda f, s, t: (f, 0)))
        out_shape.append(jax.ShapeDtypeStruct((dff, d), BF16))
        args.append(w_down)
    outs = pl.pallas_call(
        kern,
        grid=(nf, B // bb, nrt),
        in_specs=in_specs,
        out_specs=out_specs,
        out_shape=out_shape,
        scratch_shapes=[pltpu.VMEM((2, bb, HALO + r, tf), F32),
                        pltpu.VMEM(carry_shape, F32),
                        pltpu.VMEM((2, d, tf), BF16)],
        compiler_params=_cparams(("arbitrary", "arbitrary", "arbitrary")),
        name="conv_ffn_up",
    )(*args)
    act, ca, cb = outs[:3]
    if cast_down:
        w_down = outs[3]

    m = B * Tp
    tm = _pick(m, (256, 128))
    y = pl.pallas_call(
        _ffn_down_kernel,
        grid=(m // tm,),
        in_specs=[pl.BlockSpec((tm, dff), lambda i: (i, 0)),
                  pl.BlockSpec((dff, d), lambda i: (0, 0), pipeline_mode=pl.Buffered(1)),
                  pl.BlockSpec((tm, d), lambda i: (i, 0)),
                  pl.BlockSpec((1, d), lambda i: (0, 0))],
        out_specs=pl.BlockSpec((tm, d), lambda i: (i, 0)),
        out_shape=jax.ShapeDtypeStruct((m, d), F32),
        compiler_params=_cparams(("parallel",)),
        name="ffn_down",
    )(act.reshape(m, dff), w_down, x1, g_final.reshape(1, -1))
    return y.reshape(B, Tp, d), jnp.concatenate([ca[nrt - 1], cb[nrt - 1]], axis=-1), w_down


def _rope_tables(pos):
    half = MLA_ROPE // 2
    inv = ROPE_THETA ** (-jnp.arange(0, MLA_ROPE, 2, dtype=F32) / MLA_ROPE)
    ang = pos.astype(F32)[:, None] * inv[None, :]
    cos, sin = jnp.cos(ang), jnp.sin(ang)
    zero = jnp.zeros((pos.shape[0], LANE - 2 * half), F32)
    return (jnp.concatenate([cos, cos, zero], axis=1),
            jnp.concatenate([-sin, sin, zero], axis=1))


def _attn_tile(T):
    return _pick(T, (1024, 128))


def _project(x, pos, w, chunk_tile=0):
    col = w["col"]
    rows = w["in_rows"]
    cos_t, sin_t = _rope_tables(pos)
    h, small, lat, kr = _front(x, w["g_mix"], w["w_in_t"], rows["a"], w["g_kv"], cos_t, sin_t,
                               rank=rows["cq"] - rows["a"], rq=rows["ckv"] - rows["cq"],
                               rk=rows["kpe"] - rows["ckv"])
    qkvr = _matmul_wt(h, w["w_in_t"], rows["q"], rows["a"] - rows["q"], BF16, tn=1024)
    gates = _matmul_wt(h, w["w_in_t"], rows["ga"], rows["end"] - rows["ga"], BF16, tn=1024)
    q = _qprep(small, w["g_q"], w["wq_nope"], w["wq_pe"], w["wq_pe_sw"], cos_t, sin_t, col=col,
               chunk_tile=chunk_tile)
    return dict(qkvr=qkvr, gates=gates, small=small, q=q, lat=lat, kr=kr)


def _finish(x, pr, branch_a, o_m, w, hist, *, B, T):
    x1, h2 = _merge(branch_a, pr["gates"], o_m, x, w["w_o"], w["g_ffn"], col=w["col"])
    y, conv, w["w_down"] = _ffn(h2, x1, w["w_up"], w["w_down"], w["conv_w"], w["conv_b"], hist,
                                w["final_norm"], B=B, T=T, Tp=T)
    return y, conv


def _gla_group(pr, w, s0, *, B, T, row0=0):
    return _gla(pr["qkvr"], pr["gates"], pr["small"], w["wa_pad"], w["b_a"], w["g_gla_out"],
                s0, B=B, T=T, Tp=T, dk=w["dk"], dv=w["dv"], col=w["col"], row0=row0)


def _long_stream(x, pr, w, *, T, s0, hist, prefix):
    branch_a, state = _gla_group(pr, w, s0, B=1, T=T)
    k, vt = _kvup(pr["lat"], pr["kr"], w["w_uk"], w["w_uv_t"], v_transposed=True)
    o_m = _attention_t(pr["q"], k, vt, prefix[0], prefix[1], T=T, t=_attn_tile(T),
                       hps=MLA_HEADS // 4)
    y, conv = _finish(x, pr, branch_a, o_m, w, hist, B=1, T=T)
    return y, pr["lat"], pr["kr"], state, conv


def _short_streams(x, pr, w, *, B, T, past_lat, past_kr, s0_s, hist_s):
    ns = B * T
    dk, dv = w["dk"], w["dv"]

    ba_s, st_s = _gla_group(pr, w, s0_s, B=B, T=T)
    ba_m, st_m = _gla_group(pr, w, jnp.zeros((1, GLA_HEADS, dk, dv), F32), B=1, T=T, row0=ns)

    qlat = _absorb_q(pr["q"], w["w_uk_t3"])
    olat = _attn_latent(qlat, pr["q"], past_lat, jnp.swapaxes(past_kr, 1, 2), pr["lat"],
                        pr["kr"], B=B, T=T)
    om_s = _absorb_out(olat, w["w_uv3"])
    q_m, lat_m, kr_m = pr["q"][:, ns:], pr["lat"][ns:], pr["kr"][ns:]
    k_m, v_m = _kvup(lat_m, kr_m, w["w_uk"], w["w_uv"])
    prefix = _kvup(lat_m, kr_m, w["w_uk"], w["w_uv_t"], v_transposed=True)
    om_m = _attention(q_m, k_m, v_m, B=1, Tq=T, Tk=T, tq=T, tk=T, hps=MLA_HEADS,
                      q_off=0, k_off=0)

    hist = jnp.concatenate([hist_s, jnp.zeros((1,) + hist_s.shape[1:], F32)], axis=0)
    y, conv = _finish(x, pr, jnp.concatenate([ba_s, ba_m], axis=0),
                      jnp.concatenate([om_s, om_m], axis=0), w, hist, B=B + 1, T=T)
    sample = (y[:B], pr["lat"][:ns], pr["kr"][:ns], st_s, conv[:B])
    meta = (lat_m, kr_m, st_m, conv[B:], prefix)
    return sample, meta


def _prep_weights(g_mix, w_in, w_a2, b_a, g_gla_out, g_q, w_uq, g_kv, w_uk, w_uv, w_o,
                  g_ffn, w_up, conv_w, conv_b, w_down, final_norm):
    d = w_in.shape[0]
    rank, gqk = w_a2.shape
    gvw = GLA_HEADS * g_gla_out.shape[0]
    rq, rk = g_q.shape[0], g_kv.shape[0]
    half = MLA_ROPE // 2
    o, offs = 0, {}
    for name, width in (("q", gqk), ("k", gqk), ("v", gvw), ("r", gvw), ("a", rank),
                        ("cq", rq), ("ckv", rk), ("kpe", MLA_ROPE), ("ga", d), ("gb", d)):
        offs[name] = (o, o + width)
        o += width
    assert o == w_in.shape[1]
    in_rows = {name: lo for name, (lo, _) in offs.items()}
    in_rows["end"] = o
    assert all(v % 16 == 0 for v in in_rows.values())
    col = {"q": 0, "k": gqk, "v": 2 * gqk, "r": 2 * gqk + gvw, "ga": 0, "gb": d,
           "cq": 0, "ckv": rq, "kpe": rq + rk, "a": rq + rk + 2 * MLA_ROPE}

    w3 = w_uq.reshape(rq, MLA_HEADS, MLA_NOPE + MLA_ROPE)
    pe = w3[:, :, MLA_NOPE:]
    pe_sw = jnp.concatenate([pe[:, :, half:], pe[:, :, :half]], axis=2)
    zpad = jnp.zeros((rq, MLA_HEADS, LANE - MLA_ROPE), w_uq.dtype)
    flat = lambda t: t.reshape(rq, -1).astype(BF16)
    wa_pad = jnp.concatenate([w_a2, jnp.zeros((LANE - rank, gqk), w_a2.dtype)], axis=0)
    return dict(
        col=col, dk=gqk // GLA_HEADS, dv=g_gla_out.shape[0],
        g_mix=g_mix, w_in_t=jnp.swapaxes(w_in, 0, 1), in_rows=in_rows,
        wa_pad=wa_pad.astype(BF16), b_a=b_a, g_gla_out=g_gla_out, g_q=g_q,
        wq_nope=flat(w3[:, :, :MLA_NOPE]),
        wq_pe=flat(jnp.concatenate([pe, zpad], axis=2)),
        wq_pe_sw=flat(jnp.concatenate([pe_sw, zpad], axis=2)),
        g_kv=g_kv, w_uk=w_uk.astype(BF16), w_uv=w_uv.astype(BF16),
        w_uv_t=w_uv.T.astype(BF16),
        w_uk_t3=w_uk.reshape(rk, MLA_HEADS, MLA_NOPE).transpose(1, 2, 0).astype(BF16),
        w_uv3=w_uv.reshape(rk, MLA_HEADS, MLA_V).transpose(1, 0, 2).astype(BF16),
        w_o=w_o.astype(BF16),
        g_ffn=g_ffn, w_up=w_up, conv_w=conv_w, conv_b=conv_b,
        w_down=w_down, final_norm=final_norm)


def kernel(x_prompt, x_sample, cache_mla_latent, cache_mla_krope, state_gla, cache_ffn_conv,
           meta_tokens, g_mix, w_in, w_a2, b_a, g_gla_out, g_q, w_uq, g_kv, w_uk, w_uv, w_o,
           g_ffn, w_up, conv_w, conv_b, w_down, final_norm):
    assert w_in.shape[0] == 1, "single trunk layer"
    bp, seq, d = x_prompt.shape
    assert bp == 1
    bs, ts, _ = x_sample.shape
    P = cache_mla_latent.shape[2]
    w = _prep_weights(g_mix[0], w_in[0], w_a2[0], b_a[0], g_gla_out[0], g_q[0], w_uq[0],
                      g_kv[0], w_uk[0], w_uv[0], w_o[0], g_ffn[0], w_up[0], conv_w[0],
                      conv_b[0], w_down[0], final_norm)

    n_meta = meta_tokens.shape[0]
    assert n_meta == N_META == ts and seq % CHUNK == 0
    x_short = jnp.concatenate([x_sample.reshape(bs * ts, d), meta_tokens.astype(F32)], axis=0)
    pos_short = jnp.concatenate([jnp.tile(P + jnp.arange(ts, dtype=jnp.int32), bs),
                                 jnp.arange(n_meta, dtype=jnp.int32)])
    pos_long = n_meta + jnp.arange(seq, dtype=jnp.int32)
    pr_short = _project(x_short, pos_short, w)
    pr_long = _project(x_prompt[0], pos_long, w, chunk_tile=_attn_tile(seq))
    (ys, lat_s, kr_s, st_s, cv_s), (lat_m, kr_m, st_m, cv_m, prefix) = _short_streams(
        x_short, pr_short, w, B=bs, T=ts, past_lat=cache_mla_latent[0],
        past_kr=cache_mla_krope[0], s0_s=state_gla[0], hist_s=cache_ffn_conv[0])
    yp, lat_p, kr_p, st_p, cv_p = _long_stream(
        x_prompt[0], pr_long, w, T=seq, s0=st_m, hist=cv_m, prefix=prefix)

    rk = lat_p.shape[1]
    T = n_meta + seq
    return (yp,
            ys,
            jnp.concatenate([lat_m, lat_p], axis=0).reshape(1, 1, T, rk),
            jnp.concatenate([kr_m, kr_p], axis=0)[:, :MLA_ROPE].reshape(1, 1, T, MLA_ROPE),
            st_p[None],
            cv_p[None],
            lat_s.reshape(1, bs, ts, rk),
            kr_s[:, :MLA_ROPE].reshape(1, bs, ts, MLA_ROPE),
            st_s[None],
            cv_s[None])
```

```python
import functools

import jax
import jax.numpy as jnp
from jax import lax
from jax.experimental import pallas as pl
from jax.experimental.pallas import tpu as pltpu

BF16 = jnp.bfloat16
F32 = jnp.float32

CHUNK = 64
CHUNK_SHIFT = 6
N_META = 16
EPS = 1e-6
GLA_HEADS = 4
GLA_GATE_NORM = 16.0
GLA_LOG_ALPHA_MIN = -5.0
MLA_HEADS = 16
MLA_NOPE = 128
MLA_ROPE = 64
MLA_V = 128
ROPE_THETA = 10000.0
CONV_W = 3
NEG_BIG = -1e30
LOG2E = 1.4426950408889634
QK_SCALE_LOG2E = (MLA_NOPE + MLA_ROPE) ** -0.5 * LOG2E

LANE = 128
VT_ONES = 16
GLA_CHUNK = 256
GLA_SEQS = 4
MASK_LANE0 = MLA_ROPE + 1
SAFE_EXP = 64.0
ROW_TILE = 1024
VMEM_LIMIT = 56 * 1024 * 1024


def _cparams(sem, vmem=VMEM_LIMIT):
    return pltpu.CompilerParams(dimension_semantics=sem, vmem_limit_bytes=vmem)


def _rmsnorm(x, g):
    return x * lax.rsqrt(jnp.mean(x * x, axis=-1, keepdims=True) + EPS) * g


def _sigmoid(x):
    return 0.5 * jnp.tanh(0.5 * x) + 0.5


def _pick(n, cands):
    for c in cands:
        if n % c == 0:
            return c
    fits = [t for t in range(16, min(n, max(cands)) + 1, 16) if n % t == 0]
    if not fits:
        raise ValueError(f"no tile in {cands} divides {n}")
    return fits[-1]


_NT = (((1,), (1,)), ((), ()))


def _matmul_wt_kernel(a_ref, w_ref, o_ref, w_scr):
    @pl.when(pl.program_id(1) == 0)
    def _():
        w_scr[...] = w_ref[...].astype(BF16)

    o_ref[...] = lax.dot_general(a_ref[...], w_scr[...], _NT,
                                 preferred_element_type=F32).astype(o_ref.dtype)


def _matmul_wt(a, w_t, row0, n, out_dtype, tn):
    m, k = a.shape
    tm = _pick(m, (ROW_TILE, 512, 384, 128))
    return pl.pallas_call(
        _matmul_wt_kernel,
        grid=(n // tn, m // tm),
        in_specs=[pl.BlockSpec((tm, k), lambda j, i: (i, 0)),
                  pl.BlockSpec((pl.Element(tn), pl.Element(k)),
                               lambda j, i: (pl.multiple_of(row0 + j * tn, 16), 0))],
        out_specs=pl.BlockSpec((tm, tn), lambda j, i: (i, j)),
        out_shape=jax.ShapeDtypeStruct((m, n), out_dtype),
        scratch_shapes=[pltpu.VMEM((tn, k), BF16)],
        compiler_params=_cparams(("parallel", "arbitrary")),
        name="in_proj_wt",
    )(a, w_t)


def _front_kernel(x_ref, g_ref, w_ref, gkv_ref, cos_ref, sin_ref,
                  h_ref, o_ref, lat_ref, kr_ref, w_scr, *, rank, rq, rk):
    @pl.when(pl.program_id(0) == 0)
    def _():
        w = w_ref[...].astype(BF16)
        half = MLA_ROPE // 2
        pe0 = rank + rq + rk
        o_pe = rq + rk
        w_scr[0:rq] = w[rank:rank + rq]
        w_scr[rq:o_pe] = w[rank + rq:pe0]
        w_scr[o_pe:o_pe + MLA_ROPE] = w[pe0:pe0 + MLA_ROPE]
        w_scr[o_pe + MLA_ROPE:o_pe + MLA_ROPE + half] = w[pe0 + half:pe0 + MLA_ROPE]
        w_scr[o_pe + MLA_ROPE + half:o_pe + 2 * MLA_ROPE] = w[pe0:pe0 + half]
        o_a = o_pe + 2 * MLA_ROPE
        w_scr[o_a:o_a + rank] = w[0:rank]
        w_scr[o_a + rank:] = jnp.zeros((w_scr.shape[0] - o_a - rank, w_scr.shape[1]), BF16)

    h = _rmsnorm(x_ref[...], g_ref[...]).astype(BF16)
    h_ref[...] = h
    small = lax.dot_general(h, w_scr[...], _NT, preferred_element_type=F32)
    o_ref[...] = small
    lat_ref[...] = _rmsnorm(small[:, rq:rq + rk], gkv_ref[...])
    blk = small[:, rq + rk:rq + rk + LANE]
    kr_ref[...] = blk * cos_ref[...] + pltpu.roll(blk, LANE // 2, 1) * sin_ref[...]


def _front(x, g_mix, w_t, row0, g_kv, cos_t, sin_t, *, rank, rq, rk):
    m, k = x.shape
    n_in = rank + rq + rk + MLA_ROPE
    n_out = rq + rk + 2 * MLA_ROPE + LANE
    tm = _pick(m, (512, 384, 128))
    kern = functools.partial(_front_kernel, rank=rank, rq=rq, rk=rk)
    row = lambda i: (i, 0)
    return pl.pallas_call(
        kern,
        grid=(m // tm,),
        in_specs=[pl.BlockSpec((tm, k), row),
                  pl.BlockSpec((1, k), lambda i: (0, 0)),
                  pl.BlockSpec((pl.Element(n_in), pl.Element(k)), lambda i: (row0, 0),
                               pipeline_mode=pl.Buffered(1)),
                  pl.BlockSpec((1, rk), lambda i: (0, 0)),
                  pl.BlockSpec((tm, LANE), row),
                  pl.BlockSpec((tm, LANE), row)],
        out_specs=[pl.BlockSpec((tm, k), row),
                   pl.BlockSpec((tm, n_out), row),
                   pl.BlockSpec((tm, rk), row),
                   pl.BlockSpec((tm, LANE), row)],
        out_shape=[jax.ShapeDtypeStruct((m, k), BF16),
                   jax.ShapeDtypeStruct((m, n_out), F32),
                   jax.ShapeDtypeStruct((m, rk), F32),
                   jax.ShapeDtypeStruct((m, LANE), F32)],
        scratch_shapes=[pltpu.VMEM((n_out, k), BF16)],
        compiler_params=_cparams(("arbitrary",)),
        name="front_proj",
    )(x, g_mix.reshape(1, -1), w_t, g_kv.reshape(1, -1), cos_t, sin_t)


def _split3(x):
    a = x.astype(BF16)
    r1 = x - a.astype(F32)
    b = r1.astype(BF16)
    c = (r1 - b.astype(F32)).astype(BF16)
    return a, b, c


def _gla_kernel(q_ref, k_ref, v_ref, r_ref, ga_ref, a_ref, wa_ref, ba_ref, go_ref, s0_ref,
                o_ref, sout_ref, s_scr, *, C, SB, T, H, dk, dv, S):
    c_idx = pl.program_id(1)
    n_chunks = pl.num_programs(1)
    R = S * C

    @pl.when(c_idx == 0)
    def _():
        s_scr[...] = s0_ref[...]

    z = jnp.dot(a_ref[...].astype(BF16), wa_ref[...], preferred_element_type=F32) + ba_ref[...]
    log_sig = jnp.minimum(z, 0.0) - jnp.log(1.0 + jnp.exp(-jnp.abs(z)))
    la = jnp.maximum(log_sig * (1.0 / GLA_GATE_NORM), GLA_LOG_ALPHA_MIN)
    if T % C:
        rows = c_idx * C + lax.broadcasted_iota(jnp.int32, (C, 1), 0)
        la = jnp.where(rows < T, la, 0.0)

    ri = lax.broadcasted_iota(jnp.int32, (R, R), 0)
    ci = lax.broadcasted_iota(jnp.int32, (R, R), 1)
    same_seq = (ri >= ci) if S == 1 else ((ri >= ci) & (ri - ci <= lax.rem(ri, C)))
    tri = jnp.where(same_seq, 1.0, 0.0).astype(BF16)
    ones = jnp.ones((C, LANE), BF16)
    cs_all = jnp.zeros_like(la)
    dsum_all = [jnp.zeros((la.shape[1], LANE), F32) for _ in range(S)]
    for piece in _split3(la):
        cs_all = cs_all + jnp.dot(tri, piece, preferred_element_type=F32)
        for si in range(S):
            dsum_all[si] = dsum_all[si] + lax.dot_general(
                piece[si * C:(si + 1) * C], ones, (((0,), (0,)), ((), ())),
                preferred_element_type=F32)

    sr = lax.broadcasted_iota(jnp.int32, (SB, SB), 0)
    sc = lax.broadcasted_iota(jnp.int32, (SB, SB), 1)
    causal = sr >= sc
    nt = (((1,), (1,)), ((), ()))
    scale = dk ** -0.5

    for si, h in [(si, h) for si in range(S) for h in range(H)]:
        rs = slice(si * C, (si + 1) * C)
        ksl = slice(h * dk, (h + 1) * dk)
        vsl = slice(h * dv, (h + 1) * dv)
        cs = cs_all[rs, ksl]
        c_last = cs[C - 1:C, :]
        q = q_ref[rs, ksl].astype(F32) * scale
        k = k_ref[rs, ksl].astype(F32)
        v = v_ref[rs, vsl]
        s_old = s_scr[si, h]

        o_inter = jnp.dot((q * jnp.exp(cs)).astype(BF16), s_old.astype(BF16),
                          preferred_element_type=F32)
        k_end = (k * jnp.exp(c_last - cs)).astype(BF16)
        upd = lax.dot_general(k_end, v, (((0,), (0,)), ((), ())), preferred_element_type=F32)
        dcol = jnp.exp(dsum_all[si][ksl, :])
        s_scr[si, h] = jnp.concatenate([dcol] * (dv // LANE), axis=1) * s_old + upd

        outs = []
        for i in range(C // SB):
            lo = i * SB
            cs_i = cs[lo:lo + SB]
            q_i = q[lo:lo + SB]
            k_i = k[lo:lo + SB]
            start = cs[lo - 1:lo] if i > 0 else jnp.zeros_like(c_last)
            mid = 0.5 * (start + cs[lo + SB - 1:lo + SB])
            qd = (q_i * jnp.exp(cs_i - mid)).astype(BF16)
            kd = (k_i * jnp.exp(mid - cs_i)).astype(BF16)
            att = lax.dot_general(qd, kd, nt, preferred_element_type=F32)
            att = jnp.where(causal, att, 0.0)
            o_i = jnp.dot(att.astype(BF16), v[lo:lo + SB], preferred_element_type=F32)
            if i > 0:
                qo = (q_i * jnp.exp(cs_i - start)).astype(BF16)
                ko = (k[:lo] * jnp.exp(start - cs[:lo])).astype(BF16)
                att_o = lax.dot_general(qo, ko, nt, preferred_element_type=F32)
                o_i = o_i + jnp.dot(att_o.astype(BF16), v[:lo], preferred_element_type=F32)
            outs.append(o_i)
        o = o_inter + (jnp.concatenate(outs, axis=0) if len(outs) > 1 else outs[0])

        on = _rmsnorm(o, go_ref[...])
        r = r_ref[rs, vsl].astype(F32)
        g = ga_ref[rs, vsl].astype(F32)
        o_ref[rs, vsl] = (_sigmoid(g) * (on * (r * _sigmoid(r)))).astype(o_ref.dtype)

    @pl.when(c_idx == n_chunks - 1)
    def _():
        sout_ref[...] = s_scr[...]


def _gla(qkvr, gates, small, wa_pad, b_a, g_out, s0, *, B, T, Tp, dk, dv, col, row0=0):
    C = min(GLA_CHUNK, Tp)
    SB = min(32, C)
    assert Tp % C == 0 and C % SB == 0
    nc = Tp // C
    H = GLA_HEADS
    qk, vw = H * dk, H * dv
    S = _pick(B, (GLA_SEQS, 1)) if nc == 1 else 1
    R = S * C
    assert row0 % R == 0
    rb = lambda b, c: row0 // R + b * nc + c
    kern = functools.partial(_gla_kernel, C=C, SB=SB, T=T, H=H, dk=dk, dv=dv, S=S)
    return pl.pallas_call(
        kern,
        grid=(B // S, nc),
        in_specs=[
            pl.BlockSpec((R, qk), lambda b, c: (rb(b, c), col["q"] // qk)),
            pl.BlockSpec((R, qk), lambda b, c: (rb(b, c), col["k"] // qk)),
            pl.BlockSpec((R, vw), lambda b, c: (rb(b, c), col["v"] // vw)),
            pl.BlockSpec((R, vw), lambda b, c: (rb(b, c), col["r"] // vw)),
            pl.BlockSpec((R, vw), lambda b, c: (rb(b, c), col["ga"] // vw)),
            pl.BlockSpec((R, LANE), lambda b, c: (rb(b, c), col["a"] // LANE)),
            pl.BlockSpec((LANE, qk), lambda b, c: (0, 0)),
            pl.BlockSpec((1, qk), lambda b, c: (0, 0)),
            pl.BlockSpec((1, dv), lambda b, c: (0, 0)),
            pl.BlockSpec((S, H, dk, dv), lambda b, c: (b, 0, 0, 0)),
        ],
        out_specs=[
            pl.BlockSpec((R, vw), lambda b, c: (b * nc + c, 0)),
            pl.BlockSpec((S, H, dk, dv), lambda b, c: (b, 0, 0, 0)),
        ],
        out_shape=[jax.ShapeDtypeStruct((B * Tp, vw), BF16),
                   jax.ShapeDtypeStruct((B, H, dk, dv), F32)],
        scratch_shapes=[pltpu.VMEM((S, H, dk, dv), F32)],
        compiler_params=_cparams(("parallel", "arbitrary")),
        name="gla",
    )(qkvr, qkvr, qkvr, qkvr, gates, small, wa_pad, b_a.reshape(1, -1), g_out.reshape(1, -1), s0)


def _qprep_kernel(cq_ref, gq_ref, wn_ref, wp_ref, wps_ref, cos_ref, sin_ref, q_ref, *,
                  chunk_tile):
    hq = _rmsnorm(cq_ref[...], gq_ref[...]).astype(BF16)
    qn = jnp.dot(hq, wn_ref[...], preferred_element_type=F32)
    qp = jnp.dot(hq, wp_ref[...], preferred_element_type=F32)
    qs = jnp.dot(hq, wps_ref[...], preferred_element_type=F32)
    cos = cos_ref[...] * QK_SCALE_LOG2E
    sin = sin_ref[...] * QK_SCALE_LOG2E
    tag = 0.0
    if chunk_tile:
        tm = cos.shape[0]
        row = pl.program_id(0) * tm + lax.broadcasted_iota(jnp.int32, (tm, LANE), 0)
        lane = lax.broadcasted_iota(jnp.int32, (tm, LANE), 1)
        chunk = (row & (chunk_tile - 1)) >> CHUNK_SHIFT
        tag = jnp.where(lane - MASK_LANE0 == chunk, 1.0, 0.0)
    for h in range(MLA_HEADS):
        sl = slice(h * LANE, (h + 1) * LANE)
        q_ref[h, :, 0:LANE] = (qn[:, sl] * QK_SCALE_LOG2E).astype(BF16)
        q_ref[h, :, LANE:2 * LANE] = (qp[:, sl] * cos + qs[:, sl] * sin + tag).astype(BF16)


def _qprep(small, g_q, wn, wp, wps, cos_t, sin_t, *, col, chunk_tile=0):
    m = small.shape[0]
    rq = wn.shape[0]
    tm = _pick(m, (512, 256, 128))
    full = lambda i: (0, 0)
    return pl.pallas_call(
        functools.partial(_qprep_kernel, chunk_tile=chunk_tile),
        grid=(m // tm,),
        in_specs=[pl.BlockSpec((tm, rq), lambda i: (i, col["cq"] // rq)),
                  pl.BlockSpec((1, rq), full),
                  pl.BlockSpec(wn.shape, full),
                  pl.BlockSpec(wp.shape, full),
                  pl.BlockSpec(wps.shape, full),
                  pl.BlockSpec((tm, LANE), lambda i: (i, 0)),
                  pl.BlockSpec((tm, LANE), lambda i: (i, 0))],
        out_specs=pl.BlockSpec((MLA_HEADS, tm, 2 * LANE), lambda i: (0, i, 0)),
        out_shape=jax.ShapeDtypeStruct((MLA_HEADS, m, 2 * LANE), BF16),
        compiler_params=_cparams(("parallel",)),
        name="mla_q",
    )(small, g_q.reshape(1, -1), wn, wp, wps, cos_t, sin_t)


def _kvup_kernel(lat_ref, kr_ref, wuk_ref, wuv_ref, k_ref, v_ref, *, v_transposed):
    lat = lat_ref[...].astype(BF16)
    kn = jnp.dot(lat, wuk_ref[...], preferred_element_type=F32)
    kr = kr_ref[...]
    lane = lax.broadcasted_iota(jnp.int32, kr.shape, 1)
    kp = jnp.where(lane == MLA_ROPE, 1.0, kr).astype(BF16)
    if v_transposed:
        vv = lax.dot_general(wuv_ref[...], lat, (((1,), (1,)), ((), ())),
                             preferred_element_type=F32)
    else:
        vv = jnp.dot(lat, wuv_ref[...], preferred_element_type=F32)
    for h in range(MLA_HEADS):
        sl = slice(h * LANE, (h + 1) * LANE)
        k_ref[h, :, 0:LANE] = kn[:, sl].astype(BF16)
        k_ref[h, :, LANE:2 * LANE] = kp
        if v_transposed:
            v_ref[h, 0:LANE, :] = vv[sl, :].astype(BF16)
            v_ref[h, LANE:LANE + VT_ONES, :] = jnp.ones((VT_ONES, vv.shape[1]), BF16)
        else:
            v_ref[h] = vv[:, sl].astype(BF16)


def _kvup(lat, kr, wuk, wuv, *, v_transposed=False):
    m, rk = lat.shape
    tm = _pick(m, (512, 256, 128))
    full = lambda i: (0, 0)
    if v_transposed:
        v_spec = pl.BlockSpec((MLA_HEADS, LANE + VT_ONES, tm), lambda i: (0, 0, i))
        v_shape = (MLA_HEADS, LANE + VT_ONES, m)
    else:
        v_spec = pl.BlockSpec((MLA_HEADS, tm, LANE), lambda i: (0, i, 0))
        v_shape = (MLA_HEADS, m, LANE)
    return pl.pallas_call(
        functools.partial(_kvup_kernel, v_transposed=v_transposed),
        grid=(m // tm,),
        in_specs=[pl.BlockSpec((tm, rk), lambda i: (i, 0)),
                  pl.BlockSpec((tm, LANE), lambda i: (i, 0)),
                  pl.BlockSpec(wuk.shape, full),
                  pl.BlockSpec(wuv.shape, full)],
        out_specs=[pl.BlockSpec((MLA_HEADS, tm, 2 * LANE), lambda i: (0, i, 0)), v_spec],
        out_shape=[jax.ShapeDtypeStruct((MLA_HEADS, m, 2 * LANE), BF16),
                   jax.ShapeDtypeStruct(v_shape, BF16)],
        compiler_params=_cparams(("parallel",)),
        name="mla_kv",
    )(lat, kr, wuk, wuv)


def _last_kblock(qi, *, tq, tk, nk, q_off, k_off):
    top_chunk = ((qi + 1) * tq - 1 + q_off) // CHUNK
    last_key = (top_chunk + 1) * CHUNK - 1 - k_off
    return jnp.minimum(last_key // tk, nk - 1)


def _attn_kernel(q_ref, k_ref, v_ref, o_ref, m_scr, l_scr, acc_scr, *, hps, tq, tk, nk,
                 q_off, k_off):
    qi = pl.program_id(2)
    ki = pl.program_id(3)

    @pl.when(ki == 0)
    def _():
        m_scr[...] = jnp.full(m_scr.shape, NEG_BIG, F32)
        l_scr[...] = jnp.zeros(l_scr.shape, F32)
        acc_scr[...] = jnp.zeros(acc_scr.shape, F32)

    @pl.when(ki <= _last_kblock(qi, tq=tq, tk=tk, nk=nk, q_off=q_off, k_off=k_off))
    def _():
        q_chunk = (qi * tq + q_off + lax.broadcasted_iota(jnp.int32, (tq, 1), 0)) >> CHUNK_SHIFT
        k_chunk = (ki * tk + k_off + lax.broadcasted_iota(jnp.int32, (1, tk), 1)) >> CHUNK_SHIFT
        visible = q_chunk >= k_chunk

        def head(h, carry):
            s = lax.dot_general(q_ref[h], k_ref[h], (((1,), (1,)), ((), ())),
                                preferred_element_type=F32)
            s = jnp.where(visible, s, NEG_BIG)
            m_prev = m_scr[h]
            m_new = jnp.maximum(m_prev, jnp.max(s, axis=-1, keepdims=True))
            p = jnp.exp2(s - m_new)
            alpha = jnp.exp2(m_prev - m_new)
            l_scr[h] = alpha * l_scr[h] + jnp.sum(p, axis=-1, keepdims=True)
            acc_scr[h] = alpha * acc_scr[h] + jnp.dot(p.astype(BF16), v_ref[h],
                                                      preferred_element_type=F32)
            m_scr[h] = m_new
            return carry

        lax.fori_loop(0, hps, head, 0)

    @pl.when(ki == nk - 1)
    def _():
        for h in range(hps):
            o_ref[:, h * LANE:(h + 1) * LANE] = (acc_scr[h] / l_scr[h]).astype(o_ref.dtype)


def _attention(q, k, v, *, B, Tq, Tk, tq, tk, hps, q_off, k_off):
    nq = Tq // tq
    nk = Tk // tk
    hg = MLA_HEADS // hps
    dqk = q.shape[2]
    dvh = v.shape[2]
    last = functools.partial(_last_kblock, tq=tq, tk=tk, nk=nk, q_off=q_off, k_off=k_off)
    kern = functools.partial(_attn_kernel, hps=hps, tq=tq, tk=tk, nk=nk, q_off=q_off,
                             k_off=k_off)
    kv_row = lambda b, g, i, j: b * nk + jnp.minimum(j, last(i))
    return pl.pallas_call(
        kern,
        grid=(B, hg, nq, nk),
        in_specs=[pl.BlockSpec((hps, tq, dqk), lambda b, g, i, j: (g, b * nq + i, 0)),
                  pl.BlockSpec((hps, tk, dqk), lambda b, g, i, j: (g, kv_row(b, g, i, j), 0)),
                  pl.BlockSpec((hps, tk, dvh), lambda b, g, i, j: (g, kv_row(b, g, i, j), 0))],
        out_specs=pl.BlockSpec((tq, hps * dvh), lambda b, g, i, j: (b * nq + i, g)),
        out_shape=jax.ShapeDtypeStruct((B * Tq, MLA_HEADS * dvh), BF16),
        scratch_shapes=[pltpu.VMEM((hps, tq, 1), F32),
                        pltpu.VMEM((hps, tq, 1), F32),
                        pltpu.VMEM((hps, tq, dvh), F32)],
        compiler_params=_cparams(("parallel", "parallel", "parallel", "arbitrary")),
        name="mla_attn",
    )(q, k, v)


def _attn_t_kernel(qi_ref, ki_ref, q_ref, k_ref, vt_ref, kp_ref, vtp_ref, o_ref,
                   q_scr, r_scr, acc_scr, redo_scr, *, hps, t):
    pair = pl.program_id(1)
    qi = qi_ref[pair]
    ki = ki_ref[pair]
    nt = (((1,), (1,)), ((), ()))
    pe = slice(LANE, 2 * LANE)
    lane = lax.broadcasted_iota(jnp.int32, (t, LANE), 1)

    def set_reference(h, r):
        neg_r = jnp.transpose(jnp.broadcast_to(-r, (LANE, t)))
        q_scr[h, :, pe] = jnp.where(lane == MLA_ROPE, neg_r.astype(BF16), q_ref[h, :, pe])
        r_scr[h] = r

    def shifted_scores(h, own_tile=False):
        k = k_ref[h]
        if own_tile:
            ahead = lane - MASK_LANE0
            k_chunk = lax.broadcasted_iota(jnp.int32, (t, LANE), 0) >> CHUNK_SHIFT
            hidden = (ahead >= 0) & (ahead < k_chunk)
            k = jnp.concatenate(
                [k[:, 0:LANE], jnp.where(hidden, jnp.asarray(NEG_BIG, BF16), k[:, pe])], axis=1)
        return lax.dot_general(k, q_scr[h], nt, preferred_element_type=F32)

    @pl.when(ki == 0)
    def _():
        for h in range(hps):
            q_scr[h, :, 0:LANE] = q_ref[h, :, 0:LANE]
            s = lax.dot_general(kp_ref[h], q_ref[h], nt, preferred_element_type=F32)
            r = jnp.max(s, axis=0, keepdims=True).astype(BF16).astype(F32)
            p = jnp.exp2((s - r).astype(BF16))
            acc_scr[h] = jnp.dot(vtp_ref[h], p, preferred_element_type=F32)
            set_reference(h, r)

    def general(h, own_tile):
        sp = shifted_scores(h, own_tile)
        r = r_scr[h]
        rise = jnp.maximum(jnp.max(sp, axis=0, keepdims=True), 0.0)
        r_new = (r + rise).astype(BF16).astype(F32)
        delta = r_new - r
        p = jnp.exp2((sp - delta).astype(BF16))
        acc_scr[h] = jnp.exp2(-delta) * acc_scr[h] + jnp.dot(vt_ref[h], p,
                                                               preferred_element_type=F32)
        if not own_tile:
            set_reference(h, r_new)

    def tile_step(own_tile):
        unsafe = []
        sp_next = shifted_scores(0, own_tile)
        for h in range(hps):
            sp = sp_next
            if h + 1 < hps:
                sp_next = shifted_scores(h + 1, own_tile)
            safe = jnp.max(sp) <= SAFE_EXP
            part = jnp.dot(vt_ref[h], jnp.exp2(sp.astype(BF16)), preferred_element_type=F32)
            acc_scr[h] += jnp.where(safe, part, 0.0)
            unsafe.append(jnp.logical_not(safe))
            redo_scr[h] = unsafe[-1].astype(jnp.int32)

        @pl.when(functools.reduce(jnp.logical_or, unsafe))
        def _():
            def redo(h, carry):
                @pl.when(redo_scr[h] != 0)
                def _():
                    general(h, own_tile)
                return carry
            lax.fori_loop(0, hps, redo, 0)

    @pl.when(ki < qi)
    def _():
        tile_step(False)

    @pl.when(ki == qi)
    def _():
        tile_step(True)
        for h in range(hps):
            acc = acc_scr[h]
            o_t = acc[0:LANE] / acc[LANE:LANE + 1]
            o_ref[:, h * LANE:(h + 1) * LANE] = o_t.T.astype(o_ref.dtype)


def _attention_t(q, k, vt, k_pre, vt_pre, *, T, t, hps):
    assert T % t == 0 and t % CHUNK == 0 and t & (t - 1) == 0
    assert t // CHUNK <= LANE - MASK_LANE0
    n = T // t
    hg = MLA_HEADS // hps
    dqk = q.shape[2]
    npre = k_pre.shape[1]
    vrows = vt.shape[1]
    pairs = [(i, j) for i in range(n) for j in range(i + 1)]
    qi_arr = jnp.asarray([p[0] for p in pairs], jnp.int32)
    ki_arr = jnp.asarray([p[1] for p in pairs], jnp.int32)
    kern = functools.partial(_attn_t_kernel, hps=hps, t=t)
    grid_spec = pltpu.PrefetchScalarGridSpec(
        num_scalar_prefetch=2,
        grid=(hg, len(pairs)),
        in_specs=[pl.BlockSpec((hps, t, dqk), lambda g, p, qi, ki: (g, qi[p], 0)),
                  pl.BlockSpec((hps, t, dqk), lambda g, p, qi, ki: (g, ki[p], 0)),
                  pl.BlockSpec((hps, vrows, t), lambda g, p, qi, ki: (g, 0, ki[p])),
                  pl.BlockSpec((hps, npre, dqk), lambda g, p, qi, ki: (g, 0, 0)),
                  pl.BlockSpec((hps, vrows, npre), lambda g, p, qi, ki: (g, 0, 0))],
        out_specs=pl.BlockSpec((t, hps * LANE), lambda g, p, qi, ki: (qi[p], g)),
        scratch_shapes=[pltpu.VMEM((hps, t, dqk), BF16),
                        pltpu.VMEM((hps, 1, t), F32),
                        pltpu.VMEM((hps, vrows, t), F32),
                        pltpu.SMEM((hps,), jnp.int32)])
    return pl.pallas_call(
        kern,
        grid_spec=grid_spec,
        out_shape=jax.ShapeDtypeStruct((T, MLA_HEADS * LANE), BF16),
        compiler_params=_cparams(("parallel", "arbitrary")),
        name="mla_attn_t",
    )(qi_arr, ki_arr, q, k, vt, k_pre, vt_pre)


def _absorb_q_kernel(q_ref, w_ref, o_ref):
    o_ref[0] = jnp.dot(q_ref[0, :, 0:MLA_NOPE], w_ref[0],
                       preferred_element_type=F32).astype(o_ref.dtype)


def _absorb_q(q, w_uk_t3):
    heads, rows, dqk = q.shape
    rk = w_uk_t3.shape[2]
    return pl.pallas_call(
        _absorb_q_kernel,
        grid=(heads,),
        in_specs=[pl.BlockSpec((1, rows, dqk), lambda h: (h, 0, 0)),
                  pl.BlockSpec((1, MLA_NOPE, rk), lambda h: (h, 0, 0))],
        out_specs=pl.BlockSpec((1, rows, rk), lambda h: (h, 0, 0)),
        out_shape=jax.ShapeDtypeStruct((heads, rows, rk), BF16),
        compiler_params=_cparams(("parallel",)),
        name="mla_absorb_q",
    )(q, w_uk_t3)


def _attn_latent_kernel(ql_ref, q_ref, plat_ref, pkr_ref, lat_ref, kr_ref, o_ref, *, T, P, S):
    heads, _, rk = ql_ref.shape
    rows = heads * T
    nt = (((1,), (1,)), ((), ()))
    tok = lax.rem(lax.broadcasted_iota(jnp.int32, (rows, 1), 0), T)
    q_chunk = (P + tok) >> CHUNK_SHIFT
    k_chunk = lax.broadcasted_iota(jnp.int32, (1, P + T), 1) >> CHUNK_SHIFT
    visible = q_chunk >= k_chunk
    for si in range(S):
        ts = slice(si * T, (si + 1) * T)
        ql = ql_ref[:, ts, :].reshape(rows, rk)
        qpe = q_ref[:, ts, LANE:2 * LANE].reshape(rows, LANE)[:, 0:MLA_ROPE]
        lat_all = jnp.concatenate([plat_ref[si].astype(BF16), lat_ref[ts, :].astype(BF16)],
                                  axis=0)
        s_pe = jnp.concatenate(
            [jnp.dot(qpe, pkr_ref[si].astype(BF16), preferred_element_type=F32),
             lax.dot_general(qpe, kr_ref[ts, 0:MLA_ROPE].astype(BF16), nt,
                             preferred_element_type=F32)], axis=1)
        s = lax.dot_general(ql, lat_all, nt, preferred_element_type=F32) + s_pe
        s = jnp.where(visible, s, NEG_BIG)
        p = jnp.exp2(s - jnp.max(s, axis=-1, keepdims=True))
        o = jnp.dot(p.astype(BF16), lat_all, preferred_element_type=F32)
        o = o / jnp.sum(p, axis=-1, keepdims=True)
        o_ref[:, ts, :] = o.reshape(heads, T, rk).astype(o_ref.dtype)


def _attn_latent(qlat, q, past_lat, past_kr_t, lat, kr, *, B, T):
    heads, _, rk = qlat.shape
    P = past_lat.shape[1]
    S = _pick(B, (2, 1))
    kern = functools.partial(_attn_latent_kernel, T=T, P=P, S=S)
    return pl.pallas_call(
        kern,
        grid=(B // S,),
        in_specs=[pl.BlockSpec((heads, S * T, rk), lambda b: (0, b, 0)),
                  pl.BlockSpec((heads, S * T, q.shape[2]), lambda b: (0, b, 0)),
                  pl.BlockSpec((S, P, rk), lambda b: (b, 0, 0)),
                  pl.BlockSpec((S, past_kr_t.shape[1], P), lambda b: (b, 0, 0)),
                  pl.BlockSpec((S * T, rk), lambda b: (b, 0)),
                  pl.BlockSpec((S * T, LANE), lambda b: (b, 0))],
        out_specs=pl.BlockSpec((heads, S * T, rk), lambda b: (0, b, 0)),
        out_shape=jax.ShapeDtypeStruct((heads, B * T, rk), BF16),
        compiler_params=_cparams(("parallel",)),
        name="mla_attn_latent",
    )(qlat, q, past_lat, past_kr_t, lat, kr)


def _absorb_out_kernel(o_ref, w_ref, out_ref):
    out_ref[...] = jnp.dot(o_ref[0], w_ref[0], preferred_element_type=F32).astype(out_ref.dtype)


def _absorb_out(olat, w_uv3):
    heads, rows, rk = olat.shape
    dvh = w_uv3.shape[2]
    return pl.pallas_call(
        _absorb_out_kernel,
        grid=(heads,),
        in_specs=[pl.BlockSpec((1, rows, rk), lambda h: (h, 0, 0)),
                  pl.BlockSpec((1, rk, dvh), lambda h: (h, 0, 0))],
        out_specs=pl.BlockSpec((rows, dvh), lambda h: (0, h)),
        out_shape=jax.ShapeDtypeStruct((rows, heads * dvh), BF16),
        compiler_params=_cparams(("parallel",)),
        name="mla_absorb_out",
    )(olat, w_uv3)


def _merge_kernel(a_ref, gb_ref, om_ref, x_ref, wo_ref, gf_ref, x1_ref, h2_ref):
    merged = a_ref[...].astype(F32) + _sigmoid(gb_ref[...].astype(F32)) * om_ref[...].astype(F32)
    x1 = x_ref[...] + jnp.dot(merged.astype(BF16), wo_ref[...], preferred_element_type=F32)
    x1_ref[...] = x1
    h2_ref[...] = _rmsnorm(x1, gf_ref[...]).astype(BF16)


def _merge(branch_a, gates, o_m, x, wo, g_ffn, *, col):
    m, d = x.shape
    tm = _pick(m, (512, 384, 256, 128))
    row = lambda i: (i, 0)
    return pl.pallas_call(
        _merge_kernel,
        grid=(m // tm,),
        in_specs=[pl.BlockSpec((tm, d), row),
                  pl.BlockSpec((tm, d), lambda i: (i, col["gb"] // d)),
                  pl.BlockSpec((tm, d), row),
                  pl.BlockSpec((tm, d), row),
                  pl.BlockSpec(wo.shape, lambda i: (0, 0), pipeline_mode=pl.Buffered(1)),
                  pl.BlockSpec((1, d), lambda i: (0, 0))],
        out_specs=[pl.BlockSpec((tm, d), row), pl.BlockSpec((tm, d), row)],
        out_shape=[jax.ShapeDtypeStruct((m, d), F32), jax.ShapeDtypeStruct((m, d), BF16)],
        compiler_params=_cparams(("parallel",)),
        name="merge_out_proj",
    )(branch_a, gates, o_m, x, wo, g_ffn.reshape(1, -1))


HALO = 8


def _ffn_up_kernel(*refs, bb, r, tf, loc, carried, cast_down):
    (h_ref, wa_ref, wb_ref, cwa_ref, cwb_ref, cba_ref, cbb_ref, ha_ref, hb_ref), refs = \
        refs[:9], refs[9:]
    if cast_down:
        wd_ref, act_ref, ca_ref, cb_ref, wdb_ref, ext_scr, carry_scr, w_scr = refs
    else:
        act_ref, ca_ref, cb_ref, ext_scr, carry_scr, w_scr = refs
    s = pl.program_id(1)
    rt = pl.program_id(2)
    d = h_ref.shape[2]

    @pl.when((s == 0) & (rt == 0))
    def _():
        w_scr[0] = wa_ref[...].astype(BF16)
        w_scr[1] = wb_ref[...].astype(BF16)
        if cast_down:
            wdb_ref[...] = wd_ref[...].astype(BF16)

    if carried:
        @pl.when(rt == 0)
        def _():
            carry_scr[0] = ha_ref[...]
            carry_scr[1] = hb_ref[...]

    h = h_ref[...].reshape(bb * r, d)
    conv = []
    for half, (cw_ref, cbias_ref, hist_ref, cout_ref) in enumerate(
            ((cwa_ref, cba_ref, ha_ref, ca_ref), (cwb_ref, cbb_ref, hb_ref, cb_ref))):
        u = jnp.dot(h, w_scr[half], preferred_element_type=F32).reshape(bb, r, tf)
        ext_scr[half, :, HALO:HALO + r, :] = u
        ext_scr[half, :, HALO - 2:HALO, :] = carry_scr[half] if carried else hist_ref[...]
        u1 = ext_scr[half, :, HALO - 1:HALO - 1 + r, :]
        u2 = ext_scr[half, :, HALO - 2:HALO - 2 + r, :]
        cw = cw_ref[...]
        conv.append(cbias_ref[...] + cw[0:1] * u2 + cw[1:2] * u1 + cw[2:3] * u)
        if carried:
            carry_scr[half] = ext_scr[half, :, HALO + r - 2:HALO + r, :]
        cout_ref[0] = ext_scr[half, :, HALO + loc:HALO + loc + 2, :]

    act_ref[...] = ((conv[0] * _sigmoid(conv[0])) * conv[1]).astype(act_ref.dtype)


def _ffn_down_kernel(act_ref, wd_ref, x1_ref, gf_ref, y_ref):
    down = jnp.dot(act_ref[...], wd_ref[...], preferred_element_type=F32)
    y_ref[...] = _rmsnorm(x1_ref[...] + down, gf_ref[...])


def _ffn(h2, x1, w_up, w_down, conv_w, conv_b, hist, g_final, *, B, T, Tp):
    d = h2.shape[1]
    dff = w_down.shape[0]
    cast_down = w_down.dtype != BF16
    tf = _pick(dff, (512, 256, 128))
    nf = dff // tf
    if Tp <= 128:
        bb, r = B, Tp
    else:
        bb, r = 1, _pick(Tp, (ROW_TILE, 128))
    nrt = Tp // r
    carried = nrt > 1
    loc = (T - 2) - (nrt - 1) * r
    assert 0 <= loc <= r - 2, "final two valid rows must sit in the last row tile"
    kern = functools.partial(_ffn_up_kernel, bb=bb, r=r, tf=tf, loc=loc, carried=carried,
                             cast_down=cast_down)
    carry_shape = (2, bb, 2, tf) if carried else (1, 1, 2, LANE)
    in_specs = [pl.BlockSpec((bb, r, d), lambda f, s, t: (s, t, 0)),
                pl.BlockSpec((d, tf), lambda f, s, t: (0, f)),
                pl.BlockSpec((d, tf), lambda f, s, t: (0, nf + f)),
                pl.BlockSpec((CONV_W, tf), lambda f, s, t: (0, f)),
                pl.BlockSpec((CONV_W, tf), lambda f, s, t: (0, nf + f)),
                pl.BlockSpec((1, tf), lambda f, s, t: (0, f)),
                pl.BlockSpec((1, tf), lambda f, s, t: (0, nf + f)),
                pl.BlockSpec((bb, 2, tf), lambda f, s, t: (s, 0, f)),
                pl.BlockSpec((bb, 2, tf), lambda f, s, t: (s, 0, nf + f))]
    out_specs = [pl.BlockSpec((bb, r, tf), lambda f, s, t: (s, t, f)),
                 pl.BlockSpec((1, bb, 2, tf), lambda f, s, t: (t, s, 0, f)),
                 pl.BlockSpec((1, bb, 2, tf), lambda f, s, t: (t, s, 0, f))]
    out_shape = [jax.ShapeDtypeStruct((B, Tp, dff), BF16),
                 jax.ShapeDtypeStruct((nrt, B, 2, dff), F32),
                 jax.ShapeDtypeStruct((nrt, B, 2, dff), F32)]
    args = [h2.reshape(B, Tp, d), w_up, w_up, conv_w, conv_w, conv_b.reshape(1, -1),
            conv_b.reshape(1, -1), hist, hist]
    if cast_down:
        in_specs.append(pl.BlockSpec((tf, d), lambda f, s, t: (f, 0)))
        out_specs.append(pl.BlockSpec((tf, d), lambda f, s, t: (f, 0)))
        out_shape.append(jax.ShapeDtypeStruct((dff, d), BF16))
        args.append(w_down)
    outs = pl.pallas_call(
        kern,
        grid=(nf, B // bb, nrt),
        in_specs=in_specs,
        out_specs=out_specs,
        out_shape=out_shape,
        scratch_shapes=[pltpu.VMEM((2, bb, HALO + r, tf), F32),
                        pltpu.VMEM(carry_shape, F32),
                        pltpu.VMEM((2, d, tf), BF16)],
        compiler_params=_cparams(("arbitrary", "arbitrary", "arbitrary")),
        name="conv_ffn_up",
    )(*args)
    act, ca, cb = outs[:3]
    if cast_down:
        w_down = outs[3]

    m = B * Tp
    tm = _pick(m, (256, 128))
    y = pl.pallas_call(
        _ffn_down_kernel,
        grid=(m // tm,),
        in_specs=[pl.BlockSpec((tm, dff), lambda i: (i, 0)),
                  pl.BlockSpec((dff, d), lambda i: (0, 0), pipeline_mode=pl.Buffered(1)),
                  pl.BlockSpec((tm, d), lambda i: (i, 0)),
                  pl.BlockSpec((1, d), lambda i: (0, 0))],
        out_specs=pl.BlockSpec((tm, d), lambda i: (i, 0)),
        out_shape=jax.ShapeDtypeStruct((m, d), F32),
        compiler_params=_cparams(("parallel",)),
        name="ffn_down",
    )(act.reshape(m, dff), w_down, x1, g_final.reshape(1, -1))
    return y.reshape(B, Tp, d), jnp.concatenate([ca[nrt - 1], cb[nrt - 1]], axis=-1), w_down


def _rope_tables(pos):
    half = MLA_ROPE // 2
    inv = ROPE_THETA ** (-jnp.arange(0, MLA_ROPE, 2, dtype=F32) / MLA_ROPE)
    ang = pos.astype(F32)[:, None] * inv[None, :]
    cos, sin = jnp.cos(ang), jnp.sin(ang)
    zero = jnp.zeros((pos.shape[0], LANE - 2 * half), F32)
    return (jnp.concatenate([cos, cos, zero], axis=1),
            jnp.concatenate([-sin, sin, zero], axis=1))


def _attn_tile(T):
    return _pick(T, (1024, 128))


def _project(x, pos, w, chunk_tile=0):
    col = w["col"]
    rows = w["in_rows"]
    cos_t, sin_t = _rope_tables(pos)
    h, small, lat, kr = _front(x, w["g_mix"], w["w_in_t"], rows["a"], w["g_kv"], cos_t, sin_t,
                               rank=rows["cq"] - rows["a"], rq=rows["ckv"] - rows["cq"],
                               rk=rows["kpe"] - rows["ckv"])
    qkvr = _matmul_wt(h, w["w_in_t"], rows["q"], rows["a"] - rows["q"], BF16, tn=1024)
    gates = _matmul_wt(h, w["w_in_t"], rows["ga"], rows["end"] - rows["ga"], BF16, tn=1024)
    q = _qprep(small, w["g_q"], w["wq_nope"], w["wq_pe"], w["wq_pe_sw"], cos_t, sin_t, col=col,
               chunk_tile=chunk_tile)
    return dict(qkvr=qkvr, gates=gates, small=small, q=q, lat=lat, kr=kr)


def _finish(x, pr, branch_a, o_m, w, hist, *, B, T):
    x1, h2 = _merge(branch_a, pr["gates"], o_m, x, w["w_o"], w["g_ffn"], col=w["col"])
    y, conv, w["w_down"] = _ffn(h2, x1, w["w_up"], w["w_down"], w["conv_w"], w["conv_b"], hist,
                                w["final_norm"], B=B, T=T, Tp=T)
    return y, conv


def _gla_group(pr, w, s0, *, B, T, row0=0):
    return _gla(pr["qkvr"], pr["gates"], pr["small"], w["wa_pad"], w["b_a"], w["g_gla_out"],
                s0, B=B, T=T, Tp=T, dk=w["dk"], dv=w["dv"], col=w["col"], row0=row0)


def _long_stream(x, pr, w, *, T, s0, hist, prefix):
    branch_a, state = _gla_group(pr, w, s0, B=1, T=T)
    k, vt = _kvup(pr["lat"], pr["kr"], w["w_uk"], w["w_uv_t"], v_transposed=True)
    o_m = _attention_t(pr["q"], k, vt, prefix[0], prefix[1], T=T, t=_attn_tile(T),
                       hps=MLA_HEADS // 4)
    y, conv = _finish(x, pr, branch_a, o_m, w, hist, B=1, T=T)
    return y, pr["lat"], pr["kr"], state, conv


def _short_streams(x, pr, w, *, B, T, past_lat, past_kr, s0_s, hist_s):
    ns = B * T
    dk, dv = w["dk"], w["dv"]

    ba_s, st_s = _gla_group(pr, w, s0_s, B=B, T=T)
    ba_m, st_m = _gla_group(pr, w, jnp.zeros((1, GLA_HEADS, dk, dv), F32), B=1, T=T, row0=ns)

    qlat = _absorb_q(pr["q"], w["w_uk_t3"])
    olat = _attn_latent(qlat, pr["q"], past_lat, jnp.swapaxes(past_kr, 1, 2), pr["lat"],
                        pr["kr"], B=B, T=T)
    om_s = _absorb_out(olat, w["w_uv3"])
    q_m, lat_m, kr_m = pr["q"][:, ns:], pr["lat"][ns:], pr["kr"][ns:]
    k_m, v_m = _kvup(lat_m, kr_m, w["w_uk"], w["w_uv"])
    prefix = _kvup(lat_m, kr_m, w["w_uk"], w["w_uv_t"], v_transposed=True)
    om_m = _attention(q_m, k_m, v_m, B=1, Tq=T, Tk=T, tq=T, tk=T, hps=MLA_HEADS,
                      q_off=0, k_off=0)

    hist = jnp.concatenate([hist_s, jnp.zeros((1,) + hist_s.shape[1:], F32)], axis=0)
    y, conv = _finish(x, pr, jnp.concatenate([ba_s, ba_m], axis=0),
                      jnp.concatenate([om_s, om_m], axis=0), w, hist, B=B + 1, T=T)
    sample = (y[:B], pr["lat"][:ns], pr["kr"][:ns], st_s, conv[:B])
    meta = (lat_m, kr_m, st_m, conv[B:], prefix)
    return sample, meta


def _prep_weights(g_mix, w_in, w_a2, b_a, g_gla_out, g_q, w_uq, g_kv, w_uk, w_uv, w_o,
                  g_ffn, w_up, conv_w, conv_b, w_down, final_norm):
    d = w_in.shape[0]
    rank, gqk = w_a2.shape
    gvw = GLA_HEADS * g_gla_out.shape[0]
    rq, rk = g_q.shape[0], g_kv.shape[0]
    half = MLA_ROPE // 2
    o, offs = 0, {}
    for name, width in (("q", gqk), ("k", gqk), ("v", gvw), ("r", gvw), ("a", rank),
                        ("cq", rq), ("ckv", rk), ("kpe", MLA_ROPE), ("ga", d), ("gb", d)):
        offs[name] = (o, o + width)
        o += width
    assert o == w_in.shape[1]
    in_rows = {name: lo for name, (lo, _) in offs.items()}
    in_rows["end"] = o
    assert all(v % 16 == 0 for v in in_rows.values())
    col = {"q": 0, "k": gqk, "v": 2 * gqk, "r": 2 * gqk + gvw, "ga": 0, "gb": d,
           "cq": 0, "ckv": rq, "kpe": rq + rk, "a": rq + rk + 2 * MLA_ROPE}

    w3 = w_uq.reshape(rq, MLA_HEADS, MLA_NOPE + MLA_ROPE)
    pe = w3[:, :, MLA_NOPE:]
    pe_sw = jnp.concatenate([pe[:, :, half:], pe[:, :, :half]], axis=2)
    zpad = jnp.zeros((rq, MLA_HEADS, LANE - MLA_ROPE), w_uq.dtype)
    flat = lambda t: t.reshape(rq, -1).astype(BF16)
    wa_pad = jnp.concatenate([w_a2, jnp.zeros((LANE - rank, gqk), w_a2.dtype)], axis=0)
    return dict(
        col=col, dk=gqk // GLA_HEADS, dv=g_gla_out.shape[0],
        g_mix=g_mix, w_in_t=jnp.swapaxes(w_in, 0, 1), in_rows=in_rows,
        wa_pad=wa_pad.astype(BF16), b_a=b_a, g_gla_out=g_gla_out, g_q=g_q,
        wq_nope=flat(w3[:, :, :MLA_NOPE]),
        wq_pe=flat(jnp.concatenate([pe, zpad], axis=2)),
        wq_pe_sw=flat(jnp.concatenate([pe_sw, zpad], axis=2)),
        g_kv=g_kv, w_uk=w_uk.astype(BF16), w_uv=w_uv.astype(BF16),
        w_uv_t=w_uv.T.astype(BF16),
        w_uk_t3=w_uk.reshape(rk, MLA_HEADS, MLA_NOPE).transpose(1, 2, 0).astype(BF16),
        w_uv3=w_uv.reshape(rk, MLA_HEADS, MLA_V).transpose(1, 0, 2).astype(BF16),
        w_o=w_o.astype(BF16),
        g_ffn=g_ffn, w_up=w_up, conv_w=conv_w, conv_b=conv_b,
        w_down=w_down, final_norm=final_norm)


def kernel(x_prompt, x_sample, cache_mla_latent, cache_mla_krope, state_gla, cache_ffn_conv,
           meta_tokens, g_mix, w_in, w_a2, b_a, g_gla_out, g_q, w_uq, g_kv, w_uk, w_uv, w_o,
           g_ffn, w_up, conv_w, conv_b, w_down, final_norm):
    assert w_in.shape[0] == 1, "single trunk layer"
    bp, seq, d = x_prompt.shape
    assert bp == 1
    bs, ts, _ = x_sample.shape
    P = cache_mla_latent.shape[2]
    w = _prep_weights(g_mix[0], w_in[0], w_a2[0], b_a[0], g_gla_out[0], g_q[0], w_uq[0],
                      g_kv[0], w_uk[0], w_uv[0], w_o[0], g_ffn[0], w_up[0], conv_w[0],
                      conv_b[0], w_down[0], final_norm)

    n_meta = meta_tokens.shape[0]
    assert n_meta == N_META == ts and seq % CHUNK == 0
    x_short = jnp.concatenate([x_sample.reshape(bs * ts, d), meta_tokens.astype(F32)], axis=0)
    pos_short = jnp.concatenate([jnp.tile(P + jnp.arange(ts, dtype=jnp.int32), bs),
                                 jnp.arange(n_meta, dtype=jnp.int32)])
    pos_long = n_meta + jnp.arange(seq, dtype=jnp.int32)
    pr_short = _project(x_short, pos_short, w)
    pr_long = _project(x_prompt[0], pos_long, w, chunk_tile=_attn_tile(seq))
    (ys, lat_s, kr_s, st_s, cv_s), (lat_m, kr_m, st_m, cv_m, prefix) = _short_streams(
        x_short, pr_short, w, B=bs, T=ts, past_lat=cache_mla_latent[0],
        past_kr=cache_mla_krope[0], s0_s=state_gla[0], hist_s=cache_ffn_conv[0])
    yp, lat_p, kr_p, st_p, cv_p = _long_stream(
        x_prompt[0], pr_long, w, T=seq, s0=st_m, hist=cv_m, prefix=prefix)

    rk = lat_p.shape[1]
    T = n_meta + seq
    return (yp,
            ys,
            jnp.concatenate([lat_m, lat_p], axis=0).reshape(1, 1, T, rk),
            jnp.concatenate([kr_m, kr_p], axis=0)[:, :MLA_ROPE].reshape(1, 1, T, MLA_ROPE),
            st_p[None],
            cv_p[None],
            lat_s.reshape(1, bs, ts, rk),
            kr_s[:, :MLA_ROPE].reshape(1, bs, ts, MLA_ROPE),
            st_s[None],
            cv_s[None])
```

```python
import functools

import jax
import jax.numpy as jnp
from jax import lax
from jax.experimental import pallas as pl
from jax.experimental.pallas import tpu as pltpu

BF16 = jnp.bfloat16
F32 = jnp.float32

CHUNK = 64
CHUNK_SHIFT = 6
N_META = 16
EPS = 1e-6
GLA_HEADS = 4
GLA_GATE_NORM = 16.0
GLA_LOG_ALPHA_MIN = -5.0
MLA_HEADS = 16
MLA_NOPE = 128
MLA_ROPE = 64
MLA_V = 128
ROPE_THETA = 10000.0
CONV_W = 3
NEG_BIG = -1e30
LOG2E = 1.4426950408889634
QK_SCALE_LOG2E = (MLA_NOPE + MLA_ROPE) ** -0.5 * LOG2E

LANE = 128
VT_ONES = 16
GLA_CHUNK = 256
GLA_SEQS = 4
MASK_LANE0 = MLA_ROPE + 1
SAFE_EXP = 64.0
ROW_TILE = 1024
VMEM_LIMIT = 56 * 1024 * 1024


def _cparams(sem, vmem=VMEM_LIMIT):
    return pltpu.CompilerParams(dimension_semantics=sem, vmem_limit_bytes=vmem)


def _rmsnorm(x, g):
    return x * lax.rsqrt(jnp.mean(x * x, axis=-1, keepdims=True) + EPS) * g


def _sigmoid(x):
    return 0.5 * jnp.tanh(0.5 * x) + 0.5


def _pick(n, cands):
    for c in cands:
        if n % c == 0:
            return c
    fits = [t for t in range(16, min(n, max(cands)) + 1, 16) if n % t == 0]
    if not fits:
        raise ValueError(f"no tile in {cands} divides {n}")
    return fits[-1]


_NT = (((1,), (1,)), ((), ()))


def _matmul_wt_kernel(a_ref, w_ref, o_ref, w_scr):
    @pl.when(pl.program_id(1) == 0)
    def _():
        w_scr[...] = w_ref[...].astype(BF16)

    o_ref[...] = lax.dot_general(a_ref[...], w_scr[...], _NT,
                                 preferred_element_type=F32).astype(o_ref.dtype)


def _matmul_wt(a, w_t, row0, n, out_dtype, tn):
    m, k = a.shape
    tm = _pick(m, (ROW_TILE, 512, 384, 128))
    return pl.pallas_call(
        _matmul_wt_kernel,
        grid=(n // tn, m // tm),
        in_specs=[pl.BlockSpec((tm, k), lambda j, i: (i, 0)),
                  pl.BlockSpec((pl.Element(tn), pl.Element(k)),
                               lambda j, i: (pl.multiple_of(row0 + j * tn, 16), 0))],
        out_specs=pl.BlockSpec((tm, tn), lambda j, i: (i, j)),
        out_shape=jax.ShapeDtypeStruct((m, n), out_dtype),
        scratch_shapes=[pltpu.VMEM((tn, k), BF16)],
        compiler_params=_cparams(("parallel", "arbitrary")),
        name="in_proj_wt",
    )(a, w_t)


def _front_kernel(x_ref, g_ref, w_ref, gkv_ref, cos_ref, sin_ref,
                  h_ref, o_ref, lat_ref, kr_ref, w_scr, *, rank, rq, rk):
    @pl.when(pl.program_id(0) == 0)
    def _():
        w = w_ref[...].astype(BF16)
        half = MLA_ROPE // 2
        pe0 = rank + rq + rk
        o_pe = rq + rk
        w_scr[0:rq] = w[rank:rank + rq]
        w_scr[rq:o_pe] = w[rank + rq:pe0]
        w_scr[o_pe:o_pe + MLA_ROPE] = w[pe0:pe0 + MLA_ROPE]
        w_scr[o_pe + MLA_ROPE:o_pe + MLA_ROPE + half] = w[pe0 + half:pe0 + MLA_ROPE]
        w_scr[o_pe + MLA_ROPE + half:o_pe + 2 * MLA_ROPE] = w[pe0:pe0 + half]
        o_a = o_pe + 2 * MLA_ROPE
        w_scr[o_a:o_a + rank] = w[0:rank]
        w_scr[o_a + rank:] = jnp.zeros((w_scr.shape[0] - o_a - rank, w_scr.shape[1]), BF16)

    h = _rmsnorm(x_ref[...], g_ref[...]).astype(BF16)
    h_ref[...] = h
    small = lax.dot_general(h, w_scr[...], _NT, preferred_element_type=F32)
    o_ref[...] = small
    lat_ref[...] = _rmsnorm(small[:, rq:rq + rk], gkv_ref[...])
    blk = small[:, rq + rk:rq + rk + LANE]
    kr_ref[...] = blk * cos_ref[...] + pltpu.roll(blk, LANE // 2, 1) * sin_ref[...]


def _front(x, g_mix, w_t, row0, g_kv, cos_t, sin_t, *, rank, rq, rk):
    m, k = x.shape
    n_in = rank + rq + rk + MLA_ROPE
    n_out = rq + rk + 2 * MLA_ROPE + LANE
    tm = _pick(m, (512, 384, 128))
    kern = functools.partial(_front_kernel, rank=rank, rq=rq, rk=rk)
    row = lambda i: (i, 0)
    return pl.pallas_call(
        kern,
        grid=(m // tm,),
        in_specs=[pl.BlockSpec((tm, k), row),
                  pl.BlockSpec((1, k), lambda i: (0, 0)),
                  pl.BlockSpec((pl.Element(n_in), pl.Element(k)), lambda i: (row0, 0),
                               pipeline_mode=pl.Buffered(1)),
                  pl.BlockSpec((1, rk), lambda i: (0, 0)),
                  pl.BlockSpec((tm, LANE), row),
                  pl.BlockSpec((tm, LANE), row)],
        out_specs=[pl.BlockSpec((tm, k), row),
                   pl.BlockSpec((tm, n_out), row),
                   pl.BlockSpec((tm, rk), row),
                   pl.BlockSpec((tm, LANE), row)],
        out_shape=[jax.ShapeDtypeStruct((m, k), BF16),
                   jax.ShapeDtypeStruct((m, n_out), F32),
                   jax.ShapeDtypeStruct((m, rk), F32),
                   jax.ShapeDtypeStruct((m, LANE), F32)],
        scratch_shapes=[pltpu.VMEM((n_out, k), BF16)],
        compiler_params=_cparams(("arbitrary",)),
        name="front_proj",
    )(x, g_mix.reshape(1, -1), w_t, g_kv.reshape(1, -1), cos_t, sin_t)


def _split3(x):
    a = x.astype(BF16)
    r1 = x - a.astype(F32)
    b = r1.astype(BF16)
    c = (r1 - b.astype(F32)).astype(BF16)
    return a, b, c


def _gla_kernel(q_ref, k_ref, v_ref, r_ref, ga_ref, a_ref, wa_ref, ba_ref, go_ref, s0_ref,
                o_ref, sout_ref, s_scr, *, C, SB, T, H, dk, dv, S):
    c_idx = pl.program_id(1)
    n_chunks = pl.num_programs(1)
    R = S * C

    @pl.when(c_idx == 0)
    def _():
        s_scr[...] = s0_ref[...]

    z = jnp.dot(a_ref[...].astype(BF16), wa_ref[...], preferred_element_type=F32) + ba_ref[...]
    log_sig = jnp.minimum(z, 0.0) - jnp.log(1.0 + jnp.exp(-jnp.abs(z)))
    la = jnp.maximum(log_sig * (1.0 / GLA_GATE_NORM), GLA_LOG_ALPHA_MIN)
    if T % C:
        rows = c_idx * C + lax.broadcasted_iota(jnp.int32, (C, 1), 0)
        la = jnp.where(rows < T, la, 0.0)

    ri = lax.broadcasted_iota(jnp.int32, (R, R), 0)
    ci = lax.broadcasted_iota(jnp.int32, (R, R), 1)
    same_seq = (ri >= ci) if S == 1 else ((ri >= ci) & (ri - ci <= lax.rem(ri, C)))
    tri = jnp.where(same_seq, 1.0, 0.0).astype(BF16)
    ones = jnp.ones((C, LANE), BF16)
    cs_all = jnp.zeros_like(la)
    dsum_all = [jnp.zeros((la.shape[1], LANE), F32) for _ in range(S)]
    for piece in _split3(la):
        cs_all = cs_all + jnp.dot(tri, piece, preferred_element_type=F32)
        for si in range(S):
            dsum_all[si] = dsum_all[si] + lax.dot_general(
                piece[si * C:(si + 1) * C], ones, (((0,), (0,)), ((), ())),
                preferred_element_type=F32)

    sr = lax.broadcasted_iota(jnp.int32, (SB, SB), 0)
    sc = lax.broadcasted_iota(jnp.int32, (SB, SB), 1)
    causal = sr >= sc
    nt = (((1,), (1,)), ((), ()))
    scale = dk ** -0.5

    for si, h in [(si, h) for si in range(S) for h in range(H)]:
        rs = slice(si * C, (si + 1) * C)
        ksl = slice(h * dk, (h + 1) * dk)
        vsl = slice(h * dv, (h + 1) * dv)
        cs = cs_all[rs, ksl]
        c_last = cs[C - 1:C, :]
        q = q_ref[rs, ksl].astype(F32) * scale
        k = k_ref[rs, ksl].astype(F32)
        v = v_ref[rs, vsl]
        s_old = s_scr[si, h]

        o_inter = jnp.dot((q * jnp.exp(cs)).astype(BF16), s_old.astype(BF16),
                          preferred_element_type=F32)
        k_end = (k * jnp.exp(c_last - cs)).astype(BF16)
        upd = lax.dot_general(k_end, v, (((0,), (0,)), ((), ())), preferred_element_type=F32)
        dcol = jnp.exp(dsum_all[si][ksl, :])
        s_scr[si, h] = jnp.concatenate([dcol] * (dv // LANE), axis=1) * s_old + upd

        outs = []
        for i in range(C // SB):
            lo = i * SB
            cs_i = cs[lo:lo + SB]
            q_i = q[lo:lo + SB]
            k_i = k[lo:lo + SB]
            start = cs[lo - 1:lo] if i > 0 else jnp.zeros_like(c_last)
            mid = 0.5 * (start + cs[lo + SB - 1:lo + SB])
            qd = (q_i * jnp.exp(cs_i - mid)).astype(BF16)
            kd = (k_i * jnp.exp(mid - cs_i)).astype(BF16)
            att = lax.dot_general(qd, kd, nt, preferred_element_type=F32)
            att = jnp.where(causal, att, 0.0)
            o_i = jnp.dot(att.astype(BF16), v[lo:lo + SB], preferred_element_type=F32)
            if i > 0:
                qo = (q_i * jnp.exp(cs_i - start)).astype(BF16)
                ko = (k[:lo] * jnp.exp(start - cs[:lo])).astype(BF16)
                att_o = lax.dot_general(qo, ko, nt, preferred_element_type=F32)
                o_i = o_i + jnp.dot(att_o.astype(BF16), v[:lo], preferred_element_type=F32)
            outs.append(o_i)
        o = o_inter + (jnp.concatenate(outs, axis=0) if len(outs) > 1 else outs[0])

        on = _rmsnorm(o, go_ref[...])
        r = r_ref[rs, vsl].astype(F32)
        g = ga_ref[rs, vsl].astype(F32)
        o_ref[rs, vsl] = (_sigmoid(g) * (on * (r * _sigmoid(r)))).astype(o_ref.dtype)

    @pl.when(c_idx == n_chunks - 1)
    def _():
        sout_ref[...] = s_scr[...]


def _gla(qkvr, gates, small, wa_pad, b_a, g_out, s0, *, B, T, Tp, dk, dv, col, row0=0):
    C = min(GLA_CHUNK, Tp)
    SB = min(32, C)
    assert Tp % C == 0 and C % SB == 0
    nc = Tp // C
    H = GLA_HEADS
    qk, vw = H * dk, H * dv
    S = _pick(B, (GLA_SEQS, 1)) if nc == 1 else 1
    R = S * C
    assert row0 % R == 0
    rb = lambda b, c: row0 // R + b * nc + c
    kern = functools.partial(_gla_kernel, C=C, SB=SB, T=T, H=H, dk=dk, dv=dv, S=S)
    return pl.pallas_call(
        kern,
        grid=(B // S, nc),
        in_specs=[
            pl.BlockSpec((R, qk), lambda b, c: (rb(b, c), col["q"] // qk)),
            pl.BlockSpec((R, qk), lambda b, c: (rb(b, c), col["k"] // qk)),
            pl.BlockSpec((R, vw), lambda b, c: (rb(b, c), col["v"] // vw)),
            pl.BlockSpec((R, vw), lambda b, c: (rb(b, c), col["r"] // vw)),
            pl.BlockSpec((R, vw), lambda b, c: (rb(b, c), col["ga"] // vw)),
            pl.BlockSpec((R, LANE), lambda b, c: (rb(b, c), col["a"] // LANE)),
            pl.BlockSpec((LANE, qk), lambda b, c: (0, 0)),
            pl.BlockSpec((1, qk), lambda b, c: (0, 0)),
            pl.BlockSpec((1, dv), lambda b, c: (0, 0)),
            pl.BlockSpec((S, H, dk, dv), lambda b, c: (b, 0, 0, 0)),
        ],
        out_specs=[
            pl.BlockSpec((R, vw), lambda b, c: (b * nc + c, 0)),
            pl.BlockSpec((S, H, dk, dv), lambda b, c: (b, 0, 0, 0)),
        ],
        out_shape=[jax.ShapeDtypeStruct((B * Tp, vw), BF16),
                   jax.ShapeDtypeStruct((B, H, dk, dv), F32)],
        scratch_shapes=[pltpu.VMEM((S, H, dk, dv), F32)],
        compiler_params=_cparams(("parallel", "arbitrary")),
        name="gla",
    )(qkvr, qkvr, qkvr, qkvr, gates, small, wa_pad, b_a.reshape(1, -1), g_out.reshape(1, -1), s0)


def _qprep_kernel(cq_ref, gq_ref, wn_ref, wp_ref, wps_ref, cos_ref, sin_ref, q_ref, *,
                  chunk_tile):
    hq = _rmsnorm(cq_ref[...], gq_ref[...]).astype(BF16)
    qn = jnp.dot(hq, wn_ref[...], preferred_element_type=F32)
    qp = jnp.dot(hq, wp_ref[...], preferred_element_type=F32)
    qs = jnp.dot(hq, wps_ref[...], preferred_element_type=F32)
    cos = cos_ref[...] * QK_SCALE_LOG2E
    sin = sin_ref[...] * QK_SCALE_LOG2E
    tag = 0.0
    if chunk_tile:
        tm = cos.shape[0]
        row = pl.program_id(0) * tm + lax.broadcasted_iota(jnp.int32, (tm, LANE), 0)
        lane = lax.broadcasted_iota(jnp.int32, (tm, LANE), 1)
        chunk = (row & (chunk_tile - 1)) >> CHUNK_SHIFT
        tag = jnp.where(lane - MASK_LANE0 == chunk, 1.0, 0.0)
    for h in range(MLA_HEADS):
        sl = slice(h * LANE, (h + 1) * LANE)
        q_ref[h, :, 0:LANE] = (qn[:, sl] * QK_SCALE_LOG2E).astype(BF16)
        q_ref[h, :, LANE:2 * LANE] = (qp[:, sl] * cos + qs[:, sl] * sin + tag).astype(BF16)


def _qprep(small, g_q, wn, wp, wps, cos_t, sin_t, *, col, chunk_tile=0):
    m = small.shape[0]
    rq = wn.shape[0]
    tm = _pick(m, (512, 256, 128))
    full = lambda i: (0, 0)
    return pl.pallas_call(
        functools.partial(_qprep_kernel, chunk_tile=chunk_tile),
        grid=(m // tm,),
        in_specs=[pl.BlockSpec((tm, rq), lambda i: (i, col["cq"] // rq)),
                  pl.BlockSpec((1, rq), full),
                  pl.BlockSpec(wn.shape, full),
                  pl.BlockSpec(wp.shape, full),
                  pl.BlockSpec(wps.shape, full),
                  pl.BlockSpec((tm, LANE), lambda i: (i, 0)),
                  pl.BlockSpec((tm, LANE), lambda i: (i, 0))],
        out_specs=pl.BlockSpec((MLA_HEADS, tm, 2 * LANE), lambda i: (0, i, 0)),
        out_shape=jax.ShapeDtypeStruct((MLA_HEADS, m, 2 * LANE), BF16),
        compiler_params=_cparams(("parallel",)),
        name="mla_q",
    )(small, g_q.reshape(1, -1), wn, wp, wps, cos_t, sin_t)


def _kvup_kernel(lat_ref, kr_ref, wuk_ref, wuv_ref, k_ref, v_ref, *, v_transposed):
    lat = lat_ref[...].astype(BF16)
    kn = jnp.dot(lat, wuk_ref[...], preferred_element_type=F32)
    kr = kr_ref[...]
    lane = lax.broadcasted_iota(jnp.int32, kr.shape, 1)
    kp = jnp.where(lane == MLA_ROPE, 1.0, kr).astype(BF16)
    if v_transposed:
        vv = lax.dot_general(wuv_ref[...], lat, (((1,), (1,)), ((), ())),
                             preferred_element_type=F32)
    else:
        vv = jnp.dot(lat, wuv_ref[...], preferred_element_type=F32)
    for h in range(MLA_HEADS):
        sl = slice(h * LANE, (h + 1) * LANE)
        k_ref[h, :, 0:LANE] = kn[:, sl].astype(BF16)
        k_ref[h, :, LANE:2 * LANE] = kp
        if v_transposed:
            v_ref[h, 0:LANE, :] = vv[sl, :].astype(BF16)
            v_ref[h, LANE:LANE + VT_ONES, :] = jnp.ones((VT_ONES, vv.shape[1]), BF16)
        else:
            v_ref[h] = vv[:, sl].astype(BF16)


def _kvup(lat, kr, wuk, wuv, *, v_transposed=False):
    m, rk = lat.shape
    tm = _pick(m, (512, 256, 128))
    full = lambda i: (0, 0)
    if v_transposed:
        v_spec = pl.BlockSpec((MLA_HEADS, LANE + VT_ONES, tm), lambda i: (0, 0, i))
        v_shape = (MLA_HEADS, LANE + VT_ONES, m)
    else:
        v_spec = pl.BlockSpec((MLA_HEADS, tm, LANE), lambda i: (0, i, 0))
        v_shape = (MLA_HEADS, m, LANE)
    return pl.pallas_call(
        functools.partial(_kvup_kernel, v_transposed=v_transposed),
        grid=(m // tm,),
        in_specs=[pl.BlockSpec((tm, rk), lambda i: (i, 0)),
                  pl.BlockSpec((tm, LANE), lambda i: (i, 0)),
                  pl.BlockSpec(wuk.shape, full),
                  pl.BlockSpec(wuv.shape, full)],
        out_specs=[pl.BlockSpec((MLA_HEADS, tm, 2 * LANE), lambda i: (0, i, 0)), v_spec],
        out_shape=[jax.ShapeDtypeStruct((MLA_HEADS, m, 2 * LANE), BF16),
                   jax.ShapeDtypeStruct(v_shape, BF16)],
        compiler_params=_cparams(("parallel",)),
        name="mla_kv",
    )(lat, kr, wuk, wuv)


def _last_kblock(qi, *, tq, tk, nk, q_off, k_off):
    top_chunk = ((qi + 1) * tq - 1 + q_off) // CHUNK
    last_key = (top_chunk + 1) * CHUNK - 1 - k_off
    return jnp.minimum(last_key // tk, nk - 1)


def _attn_kernel(q_ref, k_ref, v_ref, o_ref, m_scr, l_scr, acc_scr, *, hps, tq, tk, nk,
                 q_off, k_off):
    qi = pl.program_id(2)
    ki = pl.program_id(3)

    @pl.when(ki == 0)
    def _():
        m_scr[...] = jnp.full(m_scr.shape, NEG_BIG, F32)
        l_scr[...] = jnp.zeros(l_scr.shape, F32)
        acc_scr[...] = jnp.zeros(acc_scr.shape, F32)

    @pl.when(ki <= _last_kblock(qi, tq=tq, tk=tk, nk=nk, q_off=q_off, k_off=k_off))
    def _():
        q_chunk = (qi * tq + q_off + lax.broadcasted_iota(jnp.int32, (tq, 1), 0)) >> CHUNK_SHIFT
        k_chunk = (ki * tk + k_off + lax.broadcasted_iota(jnp.int32, (1, tk), 1)) >> CHUNK_SHIFT
        visible = q_chunk >= k_chunk

        def head(h, carry):
            s = lax.dot_general(q_ref[h], k_ref[h], (((1,), (1,)), ((), ())),
                                preferred_element_type=F32)
            s = jnp.where(visible, s, NEG_BIG)
            m_prev = m_scr[h]
            m_new = jnp.maximum(m_prev, jnp.max(s, axis=-1, keepdims=True))
            p = jnp.exp2(s - m_new)
            alpha = jnp.exp2(m_prev - m_new)
            l_scr[h] = alpha * l_scr[h] + jnp.sum(p, axis=-1, keepdims=True)
            acc_scr[h] = alpha * acc_scr[h] + jnp.dot(p.astype(BF16), v_ref[h],
                                                      preferred_element_type=F32)
            m_scr[h] = m_new
            return carry

        lax.fori_loop(0, hps, head, 0)

    @pl.when(ki == nk - 1)
    def _():
        for h in range(hps):
            o_ref[:, h * LANE:(h + 1) * LANE] = (acc_scr[h] / l_scr[h]).astype(o_ref.dtype)


def _attention(q, k, v, *, B, Tq, Tk, tq, tk, hps, q_off, k_off):
    nq = Tq // tq
    nk = Tk // tk
    hg = MLA_HEADS // hps
    dqk = q.shape[2]
    dvh = v.shape[2]
    last = functools.partial(_last_kblock, tq=tq, tk=tk, nk=nk, q_off=q_off, k_off=k_off)
    kern = functools.partial(_attn_kernel, hps=hps, tq=tq, tk=tk, nk=nk, q_off=q_off,
                             k_off=k_off)
    kv_row = lambda b, g, i, j: b * nk + jnp.minimum(j, last(i))
    return pl.pallas_call(
        kern,
        grid=(B, hg, nq, nk),
        in_specs=[pl.BlockSpec((hps, tq, dqk), lambda b, g, i, j: (g, b * nq + i, 0)),
                  pl.BlockSpec((hps, tk, dqk), lambda b, g, i, j: (g, kv_row(b, g, i, j), 0)),
                  pl.BlockSpec((hps, tk, dvh), lambda b, g, i, j: (g, kv_row(b, g, i, j), 0))],
        out_specs=pl.BlockSpec((tq, hps * dvh), lambda b, g, i, j: (b * nq + i, g)),
        out_shape=jax.ShapeDtypeStruct((B * Tq, MLA_HEADS * dvh), BF16),
        scratch_shapes=[pltpu.VMEM((hps, tq, 1), F32),
                        pltpu.VMEM((hps, tq, 1), F32),
                        pltpu.VMEM((hps, tq, dvh), F32)],
        compiler_params=_cparams(("parallel", "parallel", "parallel", "arbitrary")),
        name="mla_attn",
    )(q, k, v)


def _attn_t_kernel(qi_ref, ki_ref, q_ref, k_ref, vt_ref, kp_ref, vtp_ref, o_ref,
                   q_scr, r_scr, acc_scr, worst_scr, *, hps, t):
    pair = pl.program_id(1)
    qi = qi_ref[pair]
    ki = ki_ref[pair]
    nt = (((1,), (1,)), ((), ()))
    pe = slice(LANE, 2 * LANE)
    lane = lax.broadcasted_iota(jnp.int32, (t, LANE), 1)

    def set_reference(h, r):
        neg_r = jnp.transpose(jnp.broadcast_to(-r, (LANE, t)))
        q_scr[h, :, pe] = jnp.where(lane == MLA_ROPE, neg_r.astype(BF16), q_ref[h, :, pe])
        r_scr[h] = r

    def shifted_scores(h, own_tile=False):
        k = k_ref[h]
        if own_tile:
            ahead = lane - MASK_LANE0
            k_chunk = lax.broadcasted_iota(jnp.int32, (t, LANE), 0) >> CHUNK_SHIFT
            hidden = (ahead >= 0) & (ahead < k_chunk)
            k = jnp.concatenate(
                [k[:, 0:LANE], jnp.where(hidden, jnp.asarray(NEG_BIG, BF16), k[:, pe])], axis=1)
        return lax.dot_general(k, q_scr[h], nt, preferred_element_type=F32)

    @pl.when(ki == 0)
    def _():
        for h in range(hps):
            q_scr[h, :, 0:LANE] = q_ref[h, :, 0:LANE]
            s = lax.dot_general(kp_ref[h], q_ref[h], nt, preferred_element_type=F32)
            r = jnp.max(s, axis=0, keepdims=True).astype(BF16).astype(F32)
            p = jnp.exp2((s - r).astype(BF16))
            acc_scr[h] = jnp.dot(vtp_ref[h], p, preferred_element_type=F32)
            set_reference(h, r)

    def general(h, own_tile):
        sp = shifted_scores(h, own_tile)
        r = r_scr[h]
        rise = jnp.maximum(jnp.max(sp, axis=0, keepdims=True), 0.0)
        r_new = (r + rise).astype(BF16).astype(F32)
        delta = r_new - r
        p = jnp.exp2((sp - delta).astype(BF16))
        acc_scr[h] = jnp.exp2(-delta) * acc_scr[h] + jnp.dot(vt_ref[h], p,
                                                               preferred_element_type=F32)
        if not own_tile:
            set_reference(h, r_new)

    def tile_step(own_tile):
        worst_all = None
        sp_next = shifted_scores(0, own_tile)
        for h in range(hps):
            sp = sp_next
            if h + 1 < hps:
                sp_next = shifted_scores(h + 1, own_tile)
            worst = jnp.max(jnp.max(sp, axis=0, keepdims=True), axis=1, keepdims=True)
            part = jnp.dot(vt_ref[h], jnp.exp2(sp.astype(BF16)), preferred_element_type=F32)
            acc_scr[h] += jnp.where(worst <= SAFE_EXP, part, 0.0)
            worst_scr[h] = jnp.broadcast_to(worst, (1, LANE))
            worst_all = worst if worst_all is None else jnp.maximum(worst_all, worst)

        @pl.when(jnp.max(worst_all) > SAFE_EXP)
        def _():
            def redo(h, carry):
                @pl.when(jnp.max(worst_scr[h]) > SAFE_EXP)
                def _():
                    general(h, own_tile)
                return carry
            lax.fori_loop(0, hps, redo, 0)

    @pl.when(ki < qi)
    def _():
        tile_step(False)

    @pl.when(ki == qi)
    def _():
        tile_step(True)
        for h in range(hps):
            acc = acc_scr[h]
            o_t = acc[0:LANE] / acc[LANE:LANE + 1]
            o_ref[:, h * LANE:(h + 1) * LANE] = o_t.T.astype(o_ref.dtype)


def _attention_t(q, k, vt, k_pre, vt_pre, *, T, t, hps):
    assert T % t == 0 and t % CHUNK == 0 and t & (t - 1) == 0
    assert t // CHUNK <= LANE - MASK_LANE0
    n = T // t
    hg = MLA_HEADS // hps
    dqk = q.shape[2]
    npre = k_pre.shape[1]
    vrows = vt.shape[1]
    pairs = [(i, j) for i in range(n) for j in range(i + 1)]
    qi_arr = jnp.asarray([p[0] for p in pairs], jnp.int32)
    ki_arr = jnp.asarray([p[1] for p in pairs], jnp.int32)
    kern = functools.partial(_attn_t_kernel, hps=hps, t=t)
    grid_spec = pltpu.PrefetchScalarGridSpec(
        num_scalar_prefetch=2,
        grid=(hg, len(pairs)),
        in_specs=[pl.BlockSpec((hps, t, dqk), lambda g, p, qi, ki: (g, qi[p], 0)),
                  pl.BlockSpec((hps, t, dqk), lambda g, p, qi, ki: (g, ki[p], 0)),
                  pl.BlockSpec((hps, vrows, t), lambda g, p, qi, ki: (g, 0, ki[p])),
                  pl.BlockSpec((hps, npre, dqk), lambda g, p, qi, ki: (g, 0, 0)),
                  pl.BlockSpec((hps, vrows, npre), lambda g, p, qi, ki: (g, 0, 0))],
        out_specs=pl.BlockSpec((t, hps * LANE), lambda g, p, qi, ki: (qi[p], g)),
        scratch_shapes=[pltpu.VMEM((hps, t, dqk), BF16),
                        pltpu.VMEM((hps, 1, t), F32),
                        pltpu.VMEM((hps, vrows, t), F32),
                        pltpu.VMEM((hps, 1, LANE), F32)])
    return pl.pallas_call(
        kern,
        grid_spec=grid_spec,
        out_shape=jax.ShapeDtypeStruct((T, MLA_HEADS * LANE), BF16),
        compiler_params=_cparams(("parallel", "arbitrary")),
        name="mla_attn_t",
    )(qi_arr, ki_arr, q, k, vt, k_pre, vt_pre)


def _absorb_q_kernel(q_ref, w_ref, o_ref):
    o_ref[0] = jnp.dot(q_ref[0, :, 0:MLA_NOPE], w_ref[0],
                       preferred_element_type=F32).astype(o_ref.dtype)


def _absorb_q(q, w_uk_t3):
    heads, rows, dqk = q.shape
    rk = w_uk_t3.shape[2]
    return pl.pallas_call(
        _absorb_q_kernel,
        grid=(heads,),
        in_specs=[pl.BlockSpec((1, rows, dqk), lambda h: (h, 0, 0)),
                  pl.BlockSpec((1, MLA_NOPE, rk), lambda h: (h, 0, 0))],
        out_specs=pl.BlockSpec((1, rows, rk), lambda h: (h, 0, 0)),
        out_shape=jax.ShapeDtypeStruct((heads, rows, rk), BF16),
        compiler_params=_cparams(("parallel",)),
        name="mla_absorb_q",
    )(q, w_uk_t3)


def _attn_latent_kernel(ql_ref, q_ref, plat_ref, pkr_ref, lat_ref, kr_ref, o_ref, *, T, P, S):
    heads, _, rk = ql_ref.shape
    rows = heads * T
    nt = (((1,), (1,)), ((), ()))
    tok = lax.rem(lax.broadcasted_iota(jnp.int32, (rows, 1), 0), T)
    q_chunk = (P + tok) >> CHUNK_SHIFT
    k_chunk = lax.broadcasted_iota(jnp.int32, (1, P + T), 1) >> CHUNK_SHIFT
    visible = q_chunk >= k_chunk
    for si in range(S):
        ts = slice(si * T, (si + 1) * T)
        ql = ql_ref[:, ts, :].reshape(rows, rk)
        qpe = q_ref[:, ts, LANE:2 * LANE].reshape(rows, LANE)[:, 0:MLA_ROPE]
        lat_all = jnp.concatenate([plat_ref[si].astype(BF16), lat_ref[ts, :].astype(BF16)],
                                  axis=0)
        s_pe = jnp.concatenate(
            [jnp.dot(qpe, pkr_ref[si].astype(BF16), preferred_element_type=F32),
             lax.dot_general(qpe, kr_ref[ts, 0:MLA_ROPE].astype(BF16), nt,
                             preferred_element_type=F32)], axis=1)
        s = lax.dot_general(ql, lat_all, nt, preferred_element_type=F32) + s_pe
        s = jnp.where(visible, s, NEG_BIG)
        p = jnp.exp2(s - jnp.max(s, axis=-1, keepdims=True))
        o = jnp.dot(p.astype(BF16), lat_all, preferred_element_type=F32)
        o = o / jnp.sum(p, axis=-1, keepdims=True)
        o_ref[:, ts, :] = o.reshape(heads, T, rk).astype(o_ref.dtype)


def _attn_latent(qlat, q, past_lat, past_kr_t, lat, kr, *, B, T):
    heads, _, rk = qlat.shape
    P = past_lat.shape[1]
    S = _pick(B, (2, 1))
    kern = functools.partial(_attn_latent_kernel, T=T, P=P, S=S)
    return pl.pallas_call(
        kern,
        grid=(B // S,),
        in_specs=[pl.BlockSpec((heads, S * T, rk), lambda b: (0, b, 0)),
                  pl.BlockSpec((heads, S * T, q.shape[2]), lambda b: (0, b, 0)),
                  pl.BlockSpec((S, P, rk), lambda b: (b, 0, 0)),
                  pl.BlockSpec((S, past_kr_t.shape[1], P), lambda b: (b, 0, 0)),
                  pl.BlockSpec((S * T, rk), lambda b: (b, 0)),
                  pl.BlockSpec((S * T, LANE), lambda b: (b, 0))],
        out_specs=pl.BlockSpec((heads, S * T, rk), lambda b: (0, b, 0)),
        out_shape=jax.ShapeDtypeStruct((heads, B * T, rk), BF16),
        compiler_params=_cparams(("parallel",)),
        name="mla_attn_latent",
    )(qlat, q, past_lat, past_kr_t, lat, kr)


def _absorb_out_kernel(o_ref, w_ref, out_ref):
    out_ref[...] = jnp.dot(o_ref[0], w_ref[0], preferred_element_type=F32).astype(out_ref.dtype)


def _absorb_out(olat, w_uv3):
    heads, rows, rk = olat.shape
    dvh = w_uv3.shape[2]
    return pl.pallas_call(
        _absorb_out_kernel,
        grid=(heads,),
        in_specs=[pl.BlockSpec((1, rows, rk), lambda h: (h, 0, 0)),
                  pl.BlockSpec((1, rk, dvh), lambda h: (h, 0, 0))],
        out_specs=pl.BlockSpec((rows, dvh), lambda h: (0, h)),
        out_shape=jax.ShapeDtypeStruct((rows, heads * dvh), BF16),
        compiler_params=_cparams(("parallel",)),
        name="mla_absorb_out",
    )(olat, w_uv3)


def _merge_kernel(a_ref, gb_ref, om_ref, x_ref, wo_ref, gf_ref, x1_ref, h2_ref):
    merged = a_ref[...].astype(F32) + _sigmoid(gb_ref[...].astype(F32)) * om_ref[...].astype(F32)
    x1 = x_ref[...] + jnp.dot(merged.astype(BF16), wo_ref[...], preferred_element_type=F32)
    x1_ref[...] = x1
    h2_ref[...] = _rmsnorm(x1, gf_ref[...]).astype(BF16)


def _merge(branch_a, gates, o_m, x, wo, g_ffn, *, col):
    m, d = x.shape
    tm = _pick(m, (512, 384, 256, 128))
    row = lambda i: (i, 0)
    return pl.pallas_call(
        _merge_kernel,
        grid=(m // tm,),
        in_specs=[pl.BlockSpec((tm, d), row),
                  pl.BlockSpec((tm, d), lambda i: (i, col["gb"] // d)),
                  pl.BlockSpec((tm, d), row),
                  pl.BlockSpec((tm, d), row),
                  pl.BlockSpec(wo.shape, lambda i: (0, 0), pipeline_mode=pl.Buffered(1)),
                  pl.BlockSpec((1, d), lambda i: (0, 0))],
        out_specs=[pl.BlockSpec((tm, d), row), pl.BlockSpec((tm, d), row)],
        out_shape=[jax.ShapeDtypeStruct((m, d), F32), jax.ShapeDtypeStruct((m, d), BF16)],
        compiler_params=_cparams(("parallel",)),
        name="merge_out_proj",
    )(branch_a, gates, o_m, x, wo, g_ffn.reshape(1, -1))


HALO = 8


def _ffn_up_kernel(*refs, bb, r, tf, loc, carried, cast_down):
    (h_ref, wa_ref, wb_ref, cwa_ref, cwb_ref, cba_ref, cbb_ref, ha_ref, hb_ref), refs = \
        refs[:9], refs[9:]
    if cast_down:
        wd_ref, act_ref, ca_ref, cb_ref, wdb_ref, ext_scr, carry_scr, w_scr = refs
    else:
        act_ref, ca_ref, cb_ref, ext_scr, carry_scr, w_scr = refs
    s = pl.program_id(1)
    rt = pl.program_id(2)
    d = h_ref.shape[2]

    @pl.when((s == 0) & (rt == 0))
    def _():
        w_scr[0] = wa_ref[...].astype(BF16)
        w_scr[1] = wb_ref[...].astype(BF16)
        if cast_down:
            wdb_ref[...] = wd_ref[...].astype(BF16)

    if carried:
        @pl.when(rt == 0)
        def _():
            carry_scr[0] = ha_ref[...]
            carry_scr[1] = hb_ref[...]

    h = h_ref[...].reshape(bb * r, d)
    conv = []
    for half, (cw_ref, cbias_ref, hist_ref, cout_ref) in enumerate(
            ((cwa_ref, cba_ref, ha_ref, ca_ref), (cwb_ref, cbb_ref, hb_ref, cb_ref))):
        u = jnp.dot(h, w_scr[half], preferred_element_type=F32).reshape(bb, r, tf)
        ext_scr[half, :, HALO:HALO + r, :] = u
        ext_scr[half, :, HALO - 2:HALO, :] = carry_scr[half] if carried else hist_ref[...]
        u1 = ext_scr[half, :, HALO - 1:HALO - 1 + r, :]
        u2 = ext_scr[half, :, HALO - 2:HALO - 2 + r, :]
        cw = cw_ref[...]
        conv.append(cbias_ref[...] + cw[0:1] * u2 + cw[1:2] * u1 + cw[2:3] * u)
        if carried:
            carry_scr[half] = ext_scr[half, :, HALO + r - 2:HALO + r, :]
        cout_ref[0] = ext_scr[half, :, HALO + loc:HALO + loc + 2, :]

    act_ref[...] = ((conv[0] * _sigmoid(conv[0])) * conv[1]).astype(act_ref.dtype)


def _ffn_down_kernel(act_ref, wd_ref, x1_ref, gf_ref, y_ref):
    down = jnp.dot(act_ref[...], wd_ref[...], preferred_element_type=F32)
    y_ref[...] = _rmsnorm(x1_ref[...] + down, gf_ref[...])


def _ffn(h2, x1, w_up, w_down, conv_w, conv_b, hist, g_final, *, B, T, Tp):
    d = h2.shape[1]
    dff = w_down.shape[0]
    cast_down = w_down.dtype != BF16
    tf = _pick(dff, (512, 256, 128))
    nf = dff // tf
    if Tp <= 128:
        bb, r = B, Tp
    else:
        bb, r = 1, _pick(Tp, (ROW_TILE, 128))
    nrt = Tp // r
    carried = nrt > 1
    loc = (T - 2) - (nrt - 1) * r
    assert 0 <= loc <= r - 2, "final two valid rows must sit in the last row tile"
    kern = functools.partial(_ffn_up_kernel, bb=bb, r=r, tf=tf, loc=loc, carried=carried,
                             cast_down=cast_down)
    carry_shape = (2, bb, 2, tf) if carried else (1, 1, 2, LANE)
    in_specs = [pl.BlockSpec((bb, r, d), lambda f, s, t: (s, t, 0)),
                pl.BlockSpec((d, tf), lambda f, s, t: (0, f)),
                pl.BlockSpec((d, tf), lambda f, s, t: (0, nf + f)),
                pl.BlockSpec((CONV_W, tf), lambda f, s, t: (0, f)),
                pl.BlockSpec((CONV_W, tf), lambda f, s, t: (0, nf + f)),
                pl.BlockSpec((1, tf), lambda f, s, t: (0, f)),
                pl.BlockSpec((1, tf), lambda f, s, t: (0, nf + f)),
                pl.BlockSpec((bb, 2, tf), lambda f, s, t: (s, 0, f)),
                pl.BlockSpec((bb, 2, tf), lambda f, s, t: (s, 0, nf + f))]
    out_specs = [pl.BlockSpec((bb, r, tf), lambda f, s, t: (s, t, f)),
                 pl.BlockSpec((1, bb, 2, tf), lambda f, s, t: (t, s, 0, f)),
                 pl.BlockSpec((1, bb, 2, tf), lambda f, s, t: (t, s, 0, f))]
    out_shape = [jax.ShapeDtypeStruct((B, Tp, dff), BF16),
                 jax.ShapeDtypeStruct((nrt, B, 2, dff), F32),
                 jax.ShapeDtypeStruct((nrt, B, 2, dff), F32)]
    args = [h2.reshape(B, Tp, d), w_up, w_up, conv_w, conv_w, conv_b.reshape(1, -1),
            conv_b.reshape(1, -1), hist, hist]
    if cast_down:
        in_specs.append(pl.BlockSpec((tf, d), lambda f, s, t: (f, 0)))
        out_specs.append(pl.BlockSpec((tf, d), lambda f, s, t: (f, 0)))
        out_shape.append(jax.ShapeDtypeStruct((dff, d), BF16))
        args.append(w_down)
    outs = pl.pallas_call(
        kern,
        grid=(nf, B // bb, nrt),
        in_specs=in_specs,
        out_specs=out_specs,
        out_shape=out_shape,
        scratch_shapes=[pltpu.VMEM((2, bb, HALO + r, tf), F32),
                        pltpu.VMEM(carry_shape, F32),
                        pltpu.VMEM((2, d, tf), BF16)],
        compiler_params=_cparams(("arbitrary", "arbitrary", "arbitrary")),
        name="conv_ffn_up",
    )(*args)
    act, ca, cb = outs[:3]
    if cast_down:
        w_down = outs[3]

    m = B * Tp
    tm = _pick(m, (256, 128))
    y = pl.pallas_call(
        _ffn_down_kernel,
        grid=(m // tm,),
        in_specs=[pl.BlockSpec((tm, dff), lambda i: (i, 0)),
                  pl.BlockSpec((dff, d), lambda i: (0, 0), pipeline_mode=pl.Buffered(1)),
                  pl.BlockSpec((tm, d), lambda i: (i, 0)),
                  pl.BlockSpec((1, d), lambda i: (0, 0))],
        out_specs=pl.BlockSpec((tm, d), lambda i: (i, 0)),
        out_shape=jax.ShapeDtypeStruct((m, d), F32),
        compiler_params=_cparams(("parallel",)),
        name="ffn_down",
    )(act.reshape(m, dff), w_down, x1, g_final.reshape(1, -1))
    return y.reshape(B, Tp, d), jnp.concatenate([ca[nrt - 1], cb[nrt - 1]], axis=-1), w_down


def _rope_tables(pos):
    half = MLA_ROPE // 2
    inv = ROPE_THETA ** (-jnp.arange(0, MLA_ROPE, 2, dtype=F32) / MLA_ROPE)
    ang = pos.astype(F32)[:, None] * inv[None, :]
    cos, sin = jnp.cos(ang), jnp.sin(ang)
    zero = jnp.zeros((pos.shape[0], LANE - 2 * half), F32)
    return (jnp.concatenate([cos, cos, zero], axis=1),
            jnp.concatenate([-sin, sin, zero], axis=1))


def _attn_tile(T):
    return _pick(T, (1024, 128))


def _project(x, pos, w, chunk_tile=0):
    col = w["col"]
    rows = w["in_rows"]
    cos_t, sin_t = _rope_tables(pos)
    h, small, lat, kr = _front(x, w["g_mix"], w["w_in_t"], rows["a"], w["g_kv"], cos_t, sin_t,
                               rank=rows["cq"] - rows["a"], rq=rows["ckv"] - rows["cq"],
                               rk=rows["kpe"] - rows["ckv"])
    qkvr = _matmul_wt(h, w["w_in_t"], rows["q"], rows["a"] - rows["q"], BF16, tn=1024)
    gates = _matmul_wt(h, w["w_in_t"], rows["ga"], rows["end"] - rows["ga"], BF16, tn=1024)
    q = _qprep(small, w["g_q"], w["wq_nope"], w["wq_pe"], w["wq_pe_sw"], cos_t, sin_t, col=col,
               chunk_tile=chunk_tile)
    return dict(qkvr=qkvr, gates=gates, small=small, q=q, lat=lat, kr=kr)


def _finish(x, pr, branch_a, o_m, w, hist, *, B, T):
    x1, h2 = _merge(branch_a, pr["gates"], o_m, x, w["w_o"], w["g_ffn"], col=w["col"])
    y, conv, w["w_down"] = _ffn(h2, x1, w["w_up"], w["w_down"], w["conv_w"], w["conv_b"], hist,
                                w["final_norm"], B=B, T=T, Tp=T)
    return y, conv


def _gla_group(pr, w, s0, *, B, T, row0=0):
    return _gla(pr["qkvr"], pr["gates"], pr["small"], w["wa_pad"], w["b_a"], w["g_gla_out"],
                s0, B=B, T=T, Tp=T, dk=w["dk"], dv=w["dv"], col=w["col"], row0=row0)


def _long_stream(x, pr, w, *, T, s0, hist, prefix):
    branch_a, state = _gla_group(pr, w, s0, B=1, T=T)
    k, vt = _kvup(pr["lat"], pr["kr"], w["w_uk"], w["w_uv_t"], v_transposed=True)
    o_m = _attention_t(pr["q"], k, vt, prefix[0], prefix[1], T=T, t=_attn_tile(T),
                       hps=MLA_HEADS // 4)
    y, conv = _finish(x, pr, branch_a, o_m, w, hist, B=1, T=T)
    return y, pr["lat"], pr["kr"], state, conv


def _short_streams(x, pr, w, *, B, T, past_lat, past_kr, s0_s, hist_s):
    ns = B * T
    dk, dv = w["dk"], w["dv"]

    ba_s, st_s = _gla_group(pr, w, s0_s, B=B, T=T)
    ba_m, st_m = _gla_group(pr, w, jnp.zeros((1, GLA_HEADS, dk, dv), F32), B=1, T=T, row0=ns)

    qlat = _absorb_q(pr["q"], w["w_uk_t3"])
    olat = _attn_latent(qlat, pr["q"], past_lat, jnp.swapaxes(past_kr, 1, 2), pr["lat"],
                        pr["kr"], B=B, T=T)
    om_s = _absorb_out(olat, w["w_uv3"])
    q_m, lat_m, kr_m = pr["q"][:, ns:], pr["lat"][ns:], pr["kr"][ns:]
    k_m, v_m = _kvup(lat_m, kr_m, w["w_uk"], w["w_uv"])
    prefix = _kvup(lat_m, kr_m, w["w_uk"], w["w_uv_t"], v_transposed=True)
    om_m = _attention(q_m, k_m, v_m, B=1, Tq=T, Tk=T, tq=T, tk=T, hps=MLA_HEADS,
                      q_off=0, k_off=0)

    hist = jnp.concatenate([hist_s, jnp.zeros((1,) + hist_s.shape[1:], F32)], axis=0)
    y, conv = _finish(x, pr, jnp.concatenate([ba_s, ba_m], axis=0),
                      jnp.concatenate([om_s, om_m], axis=0), w, hist, B=B + 1, T=T)
    sample = (y[:B], pr["lat"][:ns], pr["kr"][:ns], st_s, conv[:B])
    meta = (lat_m, kr_m, st_m, conv[B:], prefix)
    return sample, meta


def _prep_weights(g_mix, w_in, w_a2, b_a, g_gla_out, g_q, w_uq, g_kv, w_uk, w_uv, w_o,
                  g_ffn, w_up, conv_w, conv_b, w_down, final_norm):
    d = w_in.shape[0]
    rank, gqk = w_a2.shape
    gvw = GLA_HEADS * g_gla_out.shape[0]
    rq, rk = g_q.shape[0], g_kv.shape[0]
    half = MLA_ROPE // 2
    o, offs = 0, {}
    for name, width in (("q", gqk), ("k", gqk), ("v", gvw), ("r", gvw), ("a", rank),
                        ("cq", rq), ("ckv", rk), ("kpe", MLA_ROPE), ("ga", d), ("gb", d)):
        offs[name] = (o, o + width)
        o += width
    assert o == w_in.shape[1]
    in_rows = {name: lo for name, (lo, _) in offs.items()}
    in_rows["end"] = o
    assert all(v % 16 == 0 for v in in_rows.values())
    col = {"q": 0, "k": gqk, "v": 2 * gqk, "r": 2 * gqk + gvw, "ga": 0, "gb": d,
           "cq": 0, "ckv": rq, "kpe": rq + rk, "a": rq + rk + 2 * MLA_ROPE}

    w3 = w_uq.reshape(rq, MLA_HEADS, MLA_NOPE + MLA_ROPE)
    pe = w3[:, :, MLA_NOPE:]
    pe_sw = jnp.concatenate([pe[:, :, half:], pe[:, :, :half]], axis=2)
    zpad = jnp.zeros((rq, MLA_HEADS, LANE - MLA_ROPE), w_uq.dtype)
    flat = lambda t: t.reshape(rq, -1).astype(BF16)
    wa_pad = jnp.concatenate([w_a2, jnp.zeros((LANE - rank, gqk), w_a2.dtype)], axis=0)
    return dict(
        col=col, dk=gqk // GLA_HEADS, dv=g_gla_out.shape[0],
        g_mix=g_mix, w_in_t=jnp.swapaxes(w_in, 0, 1), in_rows=in_rows,
        wa_pad=wa_pad.astype(BF16), b_a=b_a, g_gla_out=g_gla_out, g_q=g_q,
        wq_nope=flat(w3[:, :, :MLA_NOPE]),
        wq_pe=flat(jnp.concatenate([pe, zpad], axis=2)),
        wq_pe_sw=flat(jnp.concatenate([pe_sw, zpad], axis=2)),
        g_kv=g_kv, w_uk=w_uk.astype(BF16), w_uv=w_uv.astype(BF16),
        w_uv_t=w_uv.T.astype(BF16),
        w_uk_t3=w_uk.reshape(rk, MLA_HEADS, MLA_NOPE).transpose(1, 2, 0).astype(BF16),
        w_uv3=w_uv.reshape(rk, MLA_HEADS, MLA_V).transpose(1, 0, 2).astype(BF16),
        w_o=w_o.astype(BF16),
        g_ffn=g_ffn, w_up=w_up, conv_w=conv_w, conv_b=conv_b,
        w_down=w_down, final_norm=final_norm)


def kernel(x_prompt, x_sample, cache_mla_latent, cache_mla_krope, state_gla, cache_ffn_conv,
           meta_tokens, g_mix, w_in, w_a2, b_a, g_gla_out, g_q, w_uq, g_kv, w_uk, w_uv, w_o,
           g_ffn, w_up, conv_w, conv_b, w_down, final_norm):
    assert w_in.shape[0] == 1, "single trunk layer"
    bp, seq, d = x_prompt.shape
    assert bp == 1
    bs, ts, _ = x_sample.shape
    P = cache_mla_latent.shape[2]
    w = _prep_weights(g_mix[0], w_in[0], w_a2[0], b_a[0], g_gla_out[0], g_q[0], w_uq[0],
                      g_kv[0], w_uk[0], w_uv[0], w_o[0], g_ffn[0], w_up[0], conv_w[0],
                      conv_b[0], w_down[0], final_norm)

    n_meta = meta_tokens.shape[0]
    assert n_meta == N_META == ts and seq % CHUNK == 0
    x_short = jnp.concatenate([x_sample.reshape(bs * ts, d), meta_tokens.astype(F32)], axis=0)
    pos_short = jnp.concatenate([jnp.tile(P + jnp.arange(ts, dtype=jnp.int32), bs),
                                 jnp.arange(n_meta, dtype=jnp.int32)])
    pos_long = n_meta + jnp.arange(seq, dtype=jnp.int32)
    pr_short = _project(x_short, pos_short, w)
    pr_long = _project(x_prompt[0], pos_long, w, chunk_tile=_attn_tile(seq))
    (ys, lat_s, kr_s, st_s, cv_s), (lat_m, kr_m, st_m, cv_m, prefix) = _short_streams(
        x_short, pr_short, w, B=bs, T=ts, past_lat=cache_mla_latent[0],
        past_kr=cache_mla_krope[0], s0_s=state_gla[0], hist_s=cache_ffn_conv[0])
    yp, lat_p, kr_p, st_p, cv_p = _long_stream(
        x_prompt[0], pr_long, w, T=seq, s0=st_m, hist=cv_m, prefix=prefix)

    rk = lat_p.shape[1]
    T = n_meta + seq
    return (yp,
            ys,
            jnp.concatenate([lat_m, lat_p], axis=0).reshape(1, 1, T, rk),
            jnp.concatenate([kr_m, kr_p], axis=0)[:, :MLA_ROPE].reshape(1, 1, T, MLA_ROPE),
            st_p[None],
            cv_p[None],
            lat_s.reshape(1, bs, ts, rk),
            kr_s[:, :MLA_ROPE].reshape(1, bs, ts, MLA_ROPE),
            st_s[None],
            cv_s[None])
```

```python
import functools

import jax
import jax.numpy as jnp
from jax import lax
from jax.experimental import pallas as pl
from jax.experimental.pallas import tpu as pltpu

BF16 = jnp.bfloat16
F32 = jnp.float32

CHUNK = 64
CHUNK_SHIFT = 6
N_META = 16
EPS = 1e-6
GLA_HEADS = 4
GLA_GATE_NORM = 16.0
GLA_LOG_ALPHA_MIN = -5.0
MLA_HEADS = 16
MLA_NOPE = 128
MLA_ROPE = 64
MLA_V = 128
ROPE_THETA = 10000.0
CONV_W = 3
NEG_BIG = -1e30
LOG2E = 1.4426950408889634
QK_SCALE_LOG2E = (MLA_NOPE + MLA_ROPE) ** -0.5 * LOG2E

LANE = 128
VT_ONES = 16
GLA_CHUNK = 256
GLA_SEQS = 4
MASK_LANE0 = MLA_ROPE + 1
SAFE_EXP = 64.0
ROW_TILE = 1024
VMEM_LIMIT = 56 * 1024 * 1024


def _cparams(sem, vmem=VMEM_LIMIT):
    return pltpu.CompilerParams(dimension_semantics=sem, vmem_limit_bytes=vmem)


def _rmsnorm(x, g):
    return x * lax.rsqrt(jnp.mean(x * x, axis=-1, keepdims=True) + EPS) * g


def _sigmoid(x):
    return 0.5 * jnp.tanh(0.5 * x) + 0.5


def _pick(n, cands):
    for c in cands:
        if n % c == 0:
            return c
    fits = [t for t in range(16, min(n, max(cands)) + 1, 16) if n % t == 0]
    if not fits:
        raise ValueError(f"no tile in {cands} divides {n}")
    return fits[-1]


_NT = (((1,), (1,)), ((), ()))


def _matmul_wt_kernel(a_ref, w_ref, o_ref, w_scr):
    @pl.when(pl.program_id(1) == 0)
    def _():
        w_scr[...] = w_ref[...].astype(BF16)

    o_ref[...] = lax.dot_general(a_ref[...], w_scr[...], _NT,
                                 preferred_element_type=F32).astype(o_ref.dtype)


def _matmul_wt(a, w_t, row0, n, out_dtype, tn):
    m, k = a.shape
    tm = _pick(m, (2 * ROW_TILE, ROW_TILE, 512, 384, 128))
    return pl.pallas_call(
        _matmul_wt_kernel,
        grid=(n // tn, m // tm),
        in_specs=[pl.BlockSpec((tm, k), lambda j, i: (i, 0)),
                  pl.BlockSpec((pl.Element(tn), pl.Element(k)),
                               lambda j, i: (pl.multiple_of(row0 + j * tn, 16), 0))],
        out_specs=pl.BlockSpec((tm, tn), lambda j, i: (i, j)),
        out_shape=jax.ShapeDtypeStruct((m, n), out_dtype),
        scratch_shapes=[pltpu.VMEM((tn, k), BF16)],
        compiler_params=_cparams(("parallel", "arbitrary")),
        name="in_proj_wt",
    )(a, w_t)


def _front_kernel(x_ref, g_ref, w_ref, gkv_ref, cos_ref, sin_ref,
                  h_ref, o_ref, lat_ref, kr_ref, w_scr, *, rank, rq, rk):
    @pl.when(pl.program_id(0) == 0)
    def _():
        w = w_ref[...].astype(BF16)
        half = MLA_ROPE // 2
        pe0 = rank + rq + rk
        o_pe = rq + rk
        w_scr[0:rq] = w[rank:rank + rq]
        w_scr[rq:o_pe] = w[rank + rq:pe0]
        w_scr[o_pe:o_pe + MLA_ROPE] = w[pe0:pe0 + MLA_ROPE]
        w_scr[o_pe + MLA_ROPE:o_pe + MLA_ROPE + half] = w[pe0 + half:pe0 + MLA_ROPE]
        w_scr[o_pe + MLA_ROPE + half:o_pe + 2 * MLA_ROPE] = w[pe0:pe0 + half]
        o_a = o_pe + 2 * MLA_ROPE
        w_scr[o_a:o_a + rank] = w[0:rank]
        w_scr[o_a + rank:] = jnp.zeros((w_scr.shape[0] - o_a - rank, w_scr.shape[1]), BF16)

    h = _rmsnorm(x_ref[...], g_ref[...]).astype(BF16)
    h_ref[...] = h
    small = lax.dot_general(h, w_scr[...], _NT, preferred_element_type=F32)
    o_ref[...] = small
    lat_ref[...] = _rmsnorm(small[:, rq:rq + rk], gkv_ref[...])
    blk = small[:, rq + rk:rq + rk + LANE]
    kr_ref[...] = blk * cos_ref[...] + pltpu.roll(blk, LANE // 2, 1) * sin_ref[...]


def _front(x, g_mix, w_t, row0, g_kv, cos_t, sin_t, *, rank, rq, rk):
    m, k = x.shape
    n_in = rank + rq + rk + MLA_ROPE
    n_out = rq + rk + 2 * MLA_ROPE + LANE
    tm = _pick(m, (512, 384, 128))
    kern = functools.partial(_front_kernel, rank=rank, rq=rq, rk=rk)
    row = lambda i: (i, 0)
    return pl.pallas_call(
        kern,
        grid=(m // tm,),
        in_specs=[pl.BlockSpec((tm, k), row),
                  pl.BlockSpec((1, k), lambda i: (0, 0)),
                  pl.BlockSpec((pl.Element(n_in), pl.Element(k)), lambda i: (row0, 0),
                               pipeline_mode=pl.Buffered(1)),
                  pl.BlockSpec((1, rk), lambda i: (0, 0)),
                  pl.BlockSpec((tm, LANE), row),
                  pl.BlockSpec((tm, LANE), row)],
        out_specs=[pl.BlockSpec((tm, k), row),
                   pl.BlockSpec((tm, n_out), row),
                   pl.BlockSpec((tm, rk), row),
                   pl.BlockSpec((tm, LANE), row)],
        out_shape=[jax.ShapeDtypeStruct((m, k), BF16),
                   jax.ShapeDtypeStruct((m, n_out), F32),
                   jax.ShapeDtypeStruct((m, rk), F32),
                   jax.ShapeDtypeStruct((m, LANE), F32)],
        scratch_shapes=[pltpu.VMEM((n_out, k), BF16)],
        compiler_params=_cparams(("arbitrary",)),
        name="front_proj",
    )(x, g_mix.reshape(1, -1), w_t, g_kv.reshape(1, -1), cos_t, sin_t)


def _split3(x):
    a = x.astype(BF16)
    r1 = x - a.astype(F32)
    b = r1.astype(BF16)
    c = (r1 - b.astype(F32)).astype(BF16)
    return a, b, c


def _gla_kernel(q_ref, k_ref, v_ref, r_ref, ga_ref, a_ref, wa_ref, ba_ref, go_ref, s0_ref,
                o_ref, sout_ref, s_scr, *, C, SB, T, H, dk, dv, S):
    c_idx = pl.program_id(1)
    n_chunks = pl.num_programs(1)
    R = S * C

    @pl.when(c_idx == 0)
    def _():
        s_scr[...] = s0_ref[...]

    z = jnp.dot(a_ref[...].astype(BF16), wa_ref[...], preferred_element_type=F32) + ba_ref[...]
    log_sig = jnp.minimum(z, 0.0) - jnp.log(1.0 + jnp.exp(-jnp.abs(z)))
    la = jnp.maximum(log_sig * (1.0 / GLA_GATE_NORM), GLA_LOG_ALPHA_MIN)
    if T % C:
        rows = c_idx * C + lax.broadcasted_iota(jnp.int32, (C, 1), 0)
        la = jnp.where(rows < T, la, 0.0)

    ri = lax.broadcasted_iota(jnp.int32, (R, R), 0)
    ci = lax.broadcasted_iota(jnp.int32, (R, R), 1)
    same_seq = (ri >= ci) if S == 1 else ((ri >= ci) & (ri - ci <= lax.rem(ri, C)))
    tri = jnp.where(same_seq, 1.0, 0.0).astype(BF16)
    ones = jnp.ones((C, LANE), BF16)
    cs_all = jnp.zeros_like(la)
    dsum_all = [jnp.zeros((la.shape[1], LANE), F32) for _ in range(S)]
    for piece in _split3(la):
        cs_all = cs_all + jnp.dot(tri, piece, preferred_element_type=F32)
        for si in range(S):
            dsum_all[si] = dsum_all[si] + lax.dot_general(
                piece[si * C:(si + 1) * C], ones, (((0,), (0,)), ((), ())),
                preferred_element_type=F32)

    sr = lax.broadcasted_iota(jnp.int32, (SB, SB), 0)
    sc = lax.broadcasted_iota(jnp.int32, (SB, SB), 1)
    causal = sr >= sc
    nt = (((1,), (1,)), ((), ()))
    scale = dk ** -0.5

    for si, h in [(si, h) for si in range(S) for h in range(H)]:
        rs = slice(si * C, (si + 1) * C)
        ksl = slice(h * dk, (h + 1) * dk)
        vsl = slice(h * dv, (h + 1) * dv)
        cs = cs_all[rs, ksl]
        c_last = cs[C - 1:C, :]
        q = q_ref[rs, ksl].astype(F32) * scale
        k = k_ref[rs, ksl].astype(F32)
        v = v_ref[rs, vsl]
        s_old = s_scr[si, h]

        o_inter = jnp.dot((q * jnp.exp(cs)).astype(BF16), s_old.astype(BF16),
                          preferred_element_type=F32)
        k_end = (k * jnp.exp(c_last - cs)).astype(BF16)
        upd = lax.dot_general(k_end, v, (((0,), (0,)), ((), ())), preferred_element_type=F32)
        dcol = jnp.exp(dsum_all[si][ksl, :])
        s_scr[si, h] = jnp.concatenate([dcol] * (dv // LANE), axis=1) * s_old + upd

        outs = []
        for i in range(C // SB):
            lo = i * SB
            cs_i = cs[lo:lo + SB]
            q_i = q[lo:lo + SB]
            k_i = k[lo:lo + SB]
            start = cs[lo - 1:lo] if i > 0 else jnp.zeros_like(c_last)
            mid = 0.5 * (start + cs[lo + SB - 1:lo + SB])
            qd = (q_i * jnp.exp(cs_i - mid)).astype(BF16)
            kd = (k_i * jnp.exp(mid - cs_i)).astype(BF16)
            att = lax.dot_general(qd, kd, nt, preferred_element_type=F32)
            att = jnp.where(causal, att, 0.0)
            o_i = jnp.dot(att.astype(BF16), v[lo:lo + SB], preferred_element_type=F32)
            if i > 0:
                qo = (q_i * jnp.exp(cs_i - start)).astype(BF16)
                ko = (k[:lo] * jnp.exp(start - cs[:lo])).astype(BF16)
                att_o = lax.dot_general(qo, ko, nt, preferred_element_type=F32)
                o_i = o_i + jnp.dot(att_o.astype(BF16), v[:lo], preferred_element_type=F32)
            outs.append(o_i)
        o = o_inter + (jnp.concatenate(outs, axis=0) if len(outs) > 1 else outs[0])

        on = _rmsnorm(o, go_ref[...])
        r = r_ref[rs, vsl].astype(F32)
        g = ga_ref[rs, vsl].astype(F32)
        o_ref[rs, vsl] = (_sigmoid(g) * (on * (r * _sigmoid(r)))).astype(o_ref.dtype)

    @pl.when(c_idx == n_chunks - 1)
    def _():
        sout_ref[...] = s_scr[...]


def _gla(qkvr, gates, small, wa_pad, b_a, g_out, s0, *, B, T, Tp, dk, dv, col, row0=0):
    C = min(GLA_CHUNK, Tp)
    SB = min(32, C)
    assert Tp % C == 0 and C % SB == 0
    nc = Tp // C
    H = GLA_HEADS
    qk, vw = H * dk, H * dv
    S = _pick(B, (GLA_SEQS, 1)) if nc == 1 else 1
    R = S * C
    assert row0 % R == 0
    rb = lambda b, c: row0 // R + b * nc + c
    kern = functools.partial(_gla_kernel, C=C, SB=SB, T=T, H=H, dk=dk, dv=dv, S=S)
    return pl.pallas_call(
        kern,
        grid=(B // S, nc),
        in_specs=[
            pl.BlockSpec((R, qk), lambda b, c: (rb(b, c), col["q"] // qk)),
            pl.BlockSpec((R, qk), lambda b, c: (rb(b, c), col["k"] // qk)),
            pl.BlockSpec((R, vw), lambda b, c: (rb(b, c), col["v"] // vw)),
            pl.BlockSpec((R, vw), lambda b, c: (rb(b, c), col["r"] // vw)),
            pl.BlockSpec((R, vw), lambda b, c: (rb(b, c), col["ga"] // vw)),
            pl.BlockSpec((R, LANE), lambda b, c: (rb(b, c), col["a"] // LANE)),
            pl.BlockSpec((LANE, qk), lambda b, c: (0, 0)),
            pl.BlockSpec((1, qk), lambda b, c: (0, 0)),
            pl.BlockSpec((1, dv), lambda b, c: (0, 0)),
            pl.BlockSpec((S, H, dk, dv), lambda b, c: (b, 0, 0, 0)),
        ],
        out_specs=[
            pl.BlockSpec((R, vw), lambda b, c: (b * nc + c, 0)),
            pl.BlockSpec((S, H, dk, dv), lambda b, c: (b, 0, 0, 0)),
        ],
        out_shape=[jax.ShapeDtypeStruct((B * Tp, vw), BF16),
                   jax.ShapeDtypeStruct((B, H, dk, dv), F32)],
        scratch_shapes=[pltpu.VMEM((S, H, dk, dv), F32)],
        compiler_params=_cparams(("parallel", "arbitrary")),
        name="gla",
    )(qkvr, qkvr, qkvr, qkvr, gates, small, wa_pad, b_a.reshape(1, -1), g_out.reshape(1, -1), s0)


def _qprep_kernel(cq_ref, gq_ref, wn_ref, wp_ref, wps_ref, cos_ref, sin_ref, q_ref, *,
                  chunk_tile):
    hq = _rmsnorm(cq_ref[...], gq_ref[...]).astype(BF16)
    qn = jnp.dot(hq, wn_ref[...], preferred_element_type=F32)
    qp = jnp.dot(hq, wp_ref[...], preferred_element_type=F32)
    qs = jnp.dot(hq, wps_ref[...], preferred_element_type=F32)
    cos = cos_ref[...] * QK_SCALE_LOG2E
    sin = sin_ref[...] * QK_SCALE_LOG2E
    tag = 0.0
    if chunk_tile:
        tm = cos.shape[0]
        row = pl.program_id(0) * tm + lax.broadcasted_iota(jnp.int32, (tm, LANE), 0)
        lane = lax.broadcasted_iota(jnp.int32, (tm, LANE), 1)
        chunk = (row & (chunk_tile - 1)) >> CHUNK_SHIFT
        tag = jnp.where(lane - MASK_LANE0 == chunk, 1.0, 0.0)
    for h in range(MLA_HEADS):
        sl = slice(h * LANE, (h + 1) * LANE)
        q_ref[h, :, 0:LANE] = (qn[:, sl] * QK_SCALE_LOG2E).astype(BF16)
        q_ref[h, :, LANE:2 * LANE] = (qp[:, sl] * cos + qs[:, sl] * sin + tag).astype(BF16)


def _qprep(small, g_q, wn, wp, wps, cos_t, sin_t, *, col, chunk_tile=0):
    m = small.shape[0]
    rq = wn.shape[0]
    tm = _pick(m, (512, 256, 128))
    full = lambda i: (0, 0)
    return pl.pallas_call(
        functools.partial(_qprep_kernel, chunk_tile=chunk_tile),
        grid=(m // tm,),
        in_specs=[pl.BlockSpec((tm, rq), lambda i: (i, col["cq"] // rq)),
                  pl.BlockSpec((1, rq), full),
                  pl.BlockSpec(wn.shape, full),
                  pl.BlockSpec(wp.shape, full),
                  pl.BlockSpec(wps.shape, full),
                  pl.BlockSpec((tm, LANE), lambda i: (i, 0)),
                  pl.BlockSpec((tm, LANE), lambda i: (i, 0))],
        out_specs=pl.BlockSpec((MLA_HEADS, tm, 2 * LANE), lambda i: (0, i, 0)),
        out_shape=jax.ShapeDtypeStruct((MLA_HEADS, m, 2 * LANE), BF16),
        compiler_params=_cparams(("parallel",)),
        name="mla_q",
    )(small, g_q.reshape(1, -1), wn, wp, wps, cos_t, sin_t)


def _kvup_kernel(lat_ref, kr_ref, wuk_ref, wuv_ref, k_ref, v_ref, *, v_transposed):
    lat = lat_ref[...].astype(BF16)
    kn = jnp.dot(lat, wuk_ref[...], preferred_element_type=F32)
    kr = kr_ref[...]
    lane = lax.broadcasted_iota(jnp.int32, kr.shape, 1)
    kp = jnp.where(lane == MLA_ROPE, 1.0, kr).astype(BF16)
    if v_transposed:
        vv = lax.dot_general(wuv_ref[...], lat, (((1,), (1,)), ((), ())),
                             preferred_element_type=F32)
    else:
        vv = jnp.dot(lat, wuv_ref[...], preferred_element_type=F32)
    for h in range(MLA_HEADS):
        sl = slice(h * LANE, (h + 1) * LANE)
        k_ref[h, :, 0:LANE] = kn[:, sl].astype(BF16)
        k_ref[h, :, LANE:2 * LANE] = kp
        if v_transposed:
            v_ref[h, 0:LANE, :] = vv[sl, :].astype(BF16)
            v_ref[h, LANE:LANE + VT_ONES, :] = jnp.ones((VT_ONES, vv.shape[1]), BF16)
        else:
            v_ref[h] = vv[:, sl].astype(BF16)


def _kvup(lat, kr, wuk, wuv, *, v_transposed=False):
    m, rk = lat.shape
    tm = _pick(m, (512, 256, 128))
    full = lambda i: (0, 0)
    if v_transposed:
        v_spec = pl.BlockSpec((MLA_HEADS, LANE + VT_ONES, tm), lambda i: (0, 0, i))
        v_shape = (MLA_HEADS, LANE + VT_ONES, m)
    else:
        v_spec = pl.BlockSpec((MLA_HEADS, tm, LANE), lambda i: (0, i, 0))
        v_shape = (MLA_HEADS, m, LANE)
    return pl.pallas_call(
        functools.partial(_kvup_kernel, v_transposed=v_transposed),
        grid=(m // tm,),
        in_specs=[pl.BlockSpec((tm, rk), lambda i: (i, 0)),
                  pl.BlockSpec((tm, LANE), lambda i: (i, 0)),
                  pl.BlockSpec(wuk.shape, full),
                  pl.BlockSpec(wuv.shape, full)],
        out_specs=[pl.BlockSpec((MLA_HEADS, tm, 2 * LANE), lambda i: (0, i, 0)), v_spec],
        out_shape=[jax.ShapeDtypeStruct((MLA_HEADS, m, 2 * LANE), BF16),
                   jax.ShapeDtypeStruct(v_shape, BF16)],
        compiler_params=_cparams(("parallel",)),
        name="mla_kv",
    )(lat, kr, wuk, wuv)


def _last_kblock(qi, *, tq, tk, nk, q_off, k_off):
    top_chunk = ((qi + 1) * tq - 1 + q_off) // CHUNK
    last_key = (top_chunk + 1) * CHUNK - 1 - k_off
    return jnp.minimum(last_key // tk, nk - 1)


def _attn_kernel(q_ref, k_ref, v_ref, o_ref, m_scr, l_scr, acc_scr, *, hps, tq, tk, nk,
                 q_off, k_off):
    qi = pl.program_id(2)
    ki = pl.program_id(3)

    @pl.when(ki == 0)
    def _():
        m_scr[...] = jnp.full(m_scr.shape, NEG_BIG, F32)
        l_scr[...] = jnp.zeros(l_scr.shape, F32)
        acc_scr[...] = jnp.zeros(acc_scr.shape, F32)

    @pl.when(ki <= _last_kblock(qi, tq=tq, tk=tk, nk=nk, q_off=q_off, k_off=k_off))
    def _():
        q_chunk = (qi * tq + q_off + lax.broadcasted_iota(jnp.int32, (tq, 1), 0)) >> CHUNK_SHIFT
        k_chunk = (ki * tk + k_off + lax.broadcasted_iota(jnp.int32, (1, tk), 1)) >> CHUNK_SHIFT
        visible = q_chunk >= k_chunk

        def head(h, carry):
            s = lax.dot_general(q_ref[h], k_ref[h], (((1,), (1,)), ((), ())),
                                preferred_element_type=F32)
            s = jnp.where(visible, s, NEG_BIG)
            m_prev = m_scr[h]
            m_new = jnp.maximum(m_prev, jnp.max(s, axis=-1, keepdims=True))
            p = jnp.exp2(s - m_new)
            alpha = jnp.exp2(m_prev - m_new)
            l_scr[h] = alpha * l_scr[h] + jnp.sum(p, axis=-1, keepdims=True)
            acc_scr[h] = alpha * acc_scr[h] + jnp.dot(p.astype(BF16), v_ref[h],
                                                      preferred_element_type=F32)
            m_scr[h] = m_new
            return carry

        lax.fori_loop(0, hps, head, 0)

    @pl.when(ki == nk - 1)
    def _():
        for h in range(hps):
            o_ref[:, h * LANE:(h + 1) * LANE] = (acc_scr[h] / l_scr[h]).astype(o_ref.dtype)


def _attention(q, k, v, *, B, Tq, Tk, tq, tk, hps, q_off, k_off):
    nq = Tq // tq
    nk = Tk // tk
    hg = MLA_HEADS // hps
    dqk = q.shape[2]
    dvh = v.shape[2]
    last = functools.partial(_last_kblock, tq=tq, tk=tk, nk=nk, q_off=q_off, k_off=k_off)
    kern = functools.partial(_attn_kernel, hps=hps, tq=tq, tk=tk, nk=nk, q_off=q_off,
                             k_off=k_off)
    kv_row = lambda b, g, i, j: b * nk + jnp.minimum(j, last(i))
    return pl.pallas_call(
        kern,
        grid=(B, hg, nq, nk),
        in_specs=[pl.BlockSpec((hps, tq, dqk), lambda b, g, i, j: (g, b * nq + i, 0)),
                  pl.BlockSpec((hps, tk, dqk), lambda b, g, i, j: (g, kv_row(b, g, i, j), 0)),
                  pl.BlockSpec((hps, tk, dvh), lambda b, g, i, j: (g, kv_row(b, g, i, j), 0))],
        out_specs=pl.BlockSpec((tq, hps * dvh), lambda b, g, i, j: (b * nq + i, g)),
        out_shape=jax.ShapeDtypeStruct((B * Tq, MLA_HEADS * dvh), BF16),
        scratch_shapes=[pltpu.VMEM((hps, tq, 1), F32),
                        pltpu.VMEM((hps, tq, 1), F32),
                        pltpu.VMEM((hps, tq, dvh), F32)],
        compiler_params=_cparams(("parallel", "parallel", "parallel", "arbitrary")),
        name="mla_attn",
    )(q, k, v)


def _attn_t_kernel(qi_ref, ki_ref, q_ref, k_ref, vt_ref, kp_ref, vtp_ref, o_ref,
                   q_scr, r_scr, acc_scr, worst_scr, *, hps, t):
    pair = pl.program_id(1)
    qi = qi_ref[pair]
    ki = ki_ref[pair]
    nt = (((1,), (1,)), ((), ()))
    pe = slice(LANE, 2 * LANE)
    lane = lax.broadcasted_iota(jnp.int32, (t, LANE), 1)

    def set_reference(h, r):
        neg_r = jnp.transpose(jnp.broadcast_to(-r, (LANE, t)))
        q_scr[h, :, pe] = jnp.where(lane == MLA_ROPE, neg_r.astype(BF16), q_ref[h, :, pe])
        r_scr[h] = r

    def shifted_scores(h, own_tile=False):
        k = k_ref[h]
        if own_tile:
            ahead = lane - MASK_LANE0
            k_chunk = lax.broadcasted_iota(jnp.int32, (t, LANE), 0) >> CHUNK_SHIFT
            hidden = (ahead >= 0) & (ahead < k_chunk)
            k = jnp.concatenate(
                [k[:, 0:LANE], jnp.where(hidden, jnp.asarray(NEG_BIG, BF16), k[:, pe])], axis=1)
        return lax.dot_general(k, q_scr[h], nt, preferred_element_type=F32)

    @pl.when(ki == 0)
    def _():
        for h in range(hps):
            q_scr[h, :, 0:LANE] = q_ref[h, :, 0:LANE]
            s = lax.dot_general(kp_ref[h], q_ref[h], nt, preferred_element_type=F32)
            r = jnp.max(s, axis=0, keepdims=True).astype(BF16).astype(F32)
            p = jnp.exp2((s - r).astype(BF16))
            acc_scr[h] = jnp.dot(vtp_ref[h], p, preferred_element_type=F32)
            set_reference(h, r)

    def general(h, own_tile):
        sp = shifted_scores(h, own_tile)
        r = r_scr[h]
        rise = jnp.maximum(jnp.max(sp, axis=0, keepdims=True), 0.0)
        r_new = (r + rise).astype(BF16).astype(F32)
        delta = r_new - r
        p = jnp.exp2((sp - delta).astype(BF16))
        acc_scr[h] = jnp.exp2(-delta) * acc_scr[h] + jnp.dot(vt_ref[h], p,
                                                               preferred_element_type=F32)
        if not own_tile:
            set_reference(h, r_new)

    def tile_step(own_tile):
        worst_all = None
        sp_next = shifted_scores(0, own_tile)
        for h in range(hps):
            sp = sp_next
            if h + 1 < hps:
                sp_next = shifted_scores(h + 1, own_tile)
            worst = jnp.max(jnp.max(sp, axis=0, keepdims=True), axis=1, keepdims=True)
            part = jnp.dot(vt_ref[h], jnp.exp2(sp.astype(BF16)), preferred_element_type=F32)
            acc_scr[h] += jnp.where(worst <= SAFE_EXP, part, 0.0)
            worst_scr[h] = jnp.broadcast_to(worst, (1, LANE))
            worst_all = worst if worst_all is None else jnp.maximum(worst_all, worst)

        @pl.when(jnp.max(worst_all) > SAFE_EXP)
        def _():
            def redo(h, carry):
                @pl.when(jnp.max(worst_scr[h]) > SAFE_EXP)
                def _():
                    general(h, own_tile)
                return carry
            lax.fori_loop(0, hps, redo, 0)

    @pl.when(ki < qi)
    def _():
        tile_step(False)

    @pl.when(ki == qi)
    def _():
        tile_step(True)
        for h in range(hps):
            acc = acc_scr[h]
            o_t = acc[0:LANE] / acc[LANE:LANE + 1]
            o_ref[:, h * LANE:(h + 1) * LANE] = o_t.T.astype(o_ref.dtype)


def _attention_t(q, k, vt, k_pre, vt_pre, *, T, t, hps):
    assert T % t == 0 and t % CHUNK == 0 and t & (t - 1) == 0
    assert t // CHUNK <= LANE - MASK_LANE0
    n = T // t
    hg = MLA_HEADS // hps
    dqk = q.shape[2]
    npre = k_pre.shape[1]
    vrows = vt.shape[1]
    pairs = [(i, j) for i in range(n) for j in range(i + 1)]
    qi_arr = jnp.asarray([p[0] for p in pairs], jnp.int32)
    ki_arr = jnp.asarray([p[1] for p in pairs], jnp.int32)
    kern = functools.partial(_attn_t_kernel, hps=hps, t=t)
    grid_spec = pltpu.PrefetchScalarGridSpec(
        num_scalar_prefetch=2,
        grid=(hg, len(pairs)),
        in_specs=[pl.BlockSpec((hps, t, dqk), lambda g, p, qi, ki: (g, qi[p], 0)),
                  pl.BlockSpec((hps, t, dqk), lambda g, p, qi, ki: (g, ki[p], 0)),
                  pl.BlockSpec((hps, vrows, t), lambda g, p, qi, ki: (g, 0, ki[p])),
                  pl.BlockSpec((hps, npre, dqk), lambda g, p, qi, ki: (g, 0, 0)),
                  pl.BlockSpec((hps, vrows, npre), lambda g, p, qi, ki: (g, 0, 0))],
        out_specs=pl.BlockSpec((t, hps * LANE), lambda g, p, qi, ki: (qi[p], g)),
        scratch_shapes=[pltpu.VMEM((hps, t, dqk), BF16),
                        pltpu.VMEM((hps, 1, t), F32),
                        pltpu.VMEM((hps, vrows, t), F32),
                        pltpu.VMEM((hps, 1, LANE), F32)])
    return pl.pallas_call(
        kern,
        grid_spec=grid_spec,
        out_shape=jax.ShapeDtypeStruct((T, MLA_HEADS * LANE), BF16),
        compiler_params=_cparams(("parallel", "arbitrary")),
        name="mla_attn_t",
    )(qi_arr, ki_arr, q, k, vt, k_pre, vt_pre)


def _absorb_q_kernel(q_ref, w_ref, o_ref):
    o_ref[0] = jnp.dot(q_ref[0, :, 0:MLA_NOPE], w_ref[0],
                       preferred_element_type=F32).astype(o_ref.dtype)


def _absorb_q(q, w_uk_t3):
    heads, rows, dqk = q.shape
    rk = w_uk_t3.shape[2]
    return pl.pallas_call(
        _absorb_q_kernel,
        grid=(heads,),
        in_specs=[pl.BlockSpec((1, rows, dqk), lambda h: (h, 0, 0)),
                  pl.BlockSpec((1, MLA_NOPE, rk), lambda h: (h, 0, 0))],
        out_specs=pl.BlockSpec((1, rows, rk), lambda h: (h, 0, 0)),
        out_shape=jax.ShapeDtypeStruct((heads, rows, rk), BF16),
        compiler_params=_cparams(("parallel",)),
        name="mla_absorb_q",
    )(q, w_uk_t3)


def _attn_latent_kernel(ql_ref, q_ref, plat_ref, pkr_ref, lat_ref, kr_ref, o_ref, *, T, P, S):
    heads, _, rk = ql_ref.shape
    rows = heads * T
    nt = (((1,), (1,)), ((), ()))
    tok = lax.rem(lax.broadcasted_iota(jnp.int32, (rows, 1), 0), T)
    q_chunk = (P + tok) >> CHUNK_SHIFT
    k_chunk = lax.broadcasted_iota(jnp.int32, (1, P + T), 1) >> CHUNK_SHIFT
    visible = q_chunk >= k_chunk
    for si in range(S):
        ts = slice(si * T, (si + 1) * T)
        ql = ql_ref[:, ts, :].reshape(rows, rk)
        qpe = q_ref[:, ts, LANE:2 * LANE].reshape(rows, LANE)[:, 0:MLA_ROPE]
        lat_all = jnp.concatenate([plat_ref[si].astype(BF16), lat_ref[ts, :].astype(BF16)],
                                  axis=0)
        s_pe = jnp.concatenate(
            [jnp.dot(qpe, pkr_ref[si].astype(BF16), preferred_element_type=F32),
             lax.dot_general(qpe, kr_ref[ts, 0:MLA_ROPE].astype(BF16), nt,
                             preferred_element_type=F32)], axis=1)
        s = lax.dot_general(ql, lat_all, nt, preferred_element_type=F32) + s_pe
        s = jnp.where(visible, s, NEG_BIG)
        p = jnp.exp2(s - jnp.max(s, axis=-1, keepdims=True))
        o = jnp.dot(p.astype(BF16), lat_all, preferred_element_type=F32)
        o = o / jnp.sum(p, axis=-1, keepdims=True)
        o_ref[:, ts, :] = o.reshape(heads, T, rk).astype(o_ref.dtype)


def _attn_latent(qlat, q, past_lat, past_kr_t, lat, kr, *, B, T):
    heads, _, rk = qlat.shape
    P = past_lat.shape[1]
    S = _pick(B, (2, 1))
    kern = functools.partial(_attn_latent_kernel, T=T, P=P, S=S)
    return pl.pallas_call(
        kern,
        grid=(B // S,),
        in_specs=[pl.BlockSpec((heads, S * T, rk), lambda b: (0, b, 0)),
                  pl.BlockSpec((heads, S * T, q.shape[2]), lambda b: (0, b, 0)),
                  pl.BlockSpec((S, P, rk), lambda b: (b, 0, 0)),
                  pl.BlockSpec((S, past_kr_t.shape[1], P), lambda b: (b, 0, 0)),
                  pl.BlockSpec((S * T, rk), lambda b: (b, 0)),
                  pl.BlockSpec((S * T, LANE), lambda b: (b, 0))],
        out_specs=pl.BlockSpec((heads, S * T, rk), lambda b: (0, b, 0)),
        out_shape=jax.ShapeDtypeStruct((heads, B * T, rk), BF16),
        compiler_params=_cparams(("parallel",)),
        name="mla_attn_latent",
    )(qlat, q, past_lat, past_kr_t, lat, kr)


def _absorb_out_kernel(o_ref, w_ref, out_ref):
    out_ref[...] = jnp.dot(o_ref[0], w_ref[0], preferred_element_type=F32).astype(out_ref.dtype)


def _absorb_out(olat, w_uv3):
    heads, rows, rk = olat.shape
    dvh = w_uv3.shape[2]
    return pl.pallas_call(
        _absorb_out_kernel,
        grid=(heads,),
        in_specs=[pl.BlockSpec((1, rows, rk), lambda h: (h, 0, 0)),
                  pl.BlockSpec((1, rk, dvh), lambda h: (h, 0, 0))],
        out_specs=pl.BlockSpec((rows, dvh), lambda h: (0, h)),
        out_shape=jax.ShapeDtypeStruct((rows, heads * dvh), BF16),
        compiler_params=_cparams(("parallel",)),
        name="mla_absorb_out",
    )(olat, w_uv3)


def _merge_kernel(a_ref, gb_ref, om_ref, x_ref, wo_ref, gf_ref, x1_ref, h2_ref):
    merged = a_ref[...].astype(F32) + _sigmoid(gb_ref[...].astype(F32)) * om_ref[...].astype(F32)
    x1 = x_ref[...] + jnp.dot(merged.astype(BF16), wo_ref[...], preferred_element_type=F32)
    x1_ref[...] = x1
    h2_ref[...] = _rmsnorm(x1, gf_ref[...]).astype(BF16)


def _merge(branch_a, gates, o_m, x, wo, g_ffn, *, col):
    m, d = x.shape
    tm = _pick(m, (512, 384, 256, 128))
    row = lambda i: (i, 0)
    return pl.pallas_call(
        _merge_kernel,
        grid=(m // tm,),
        in_specs=[pl.BlockSpec((tm, d), row),
                  pl.BlockSpec((tm, d), lambda i: (i, col["gb"] // d)),
                  pl.BlockSpec((tm, d), row),
                  pl.BlockSpec((tm, d), row),
                  pl.BlockSpec(wo.shape, lambda i: (0, 0), pipeline_mode=pl.Buffered(1)),
                  pl.BlockSpec((1, d), lambda i: (0, 0))],
        out_specs=[pl.BlockSpec((tm, d), row), pl.BlockSpec((tm, d), row)],
        out_shape=[jax.ShapeDtypeStruct((m, d), F32), jax.ShapeDtypeStruct((m, d), BF16)],
        compiler_params=_cparams(("parallel",)),
        name="merge_out_proj",
    )(branch_a, gates, o_m, x, wo, g_ffn.reshape(1, -1))


HALO = 8


def _ffn_up_kernel(*refs, bb, r, tf, loc, carried, cast_down):
    (h_ref, wa_ref, wb_ref, cwa_ref, cwb_ref, cba_ref, cbb_ref, ha_ref, hb_ref), refs = \
        refs[:9], refs[9:]
    if cast_down:
        wd_ref, act_ref, ca_ref, cb_ref, wdb_ref, ext_scr, carry_scr, w_scr = refs
    else:
        act_ref, ca_ref, cb_ref, ext_scr, carry_scr, w_scr = refs
    s = pl.program_id(1)
    rt = pl.program_id(2)
    d = h_ref.shape[2]

    @pl.when((s == 0) & (rt == 0))
    def _():
        w_scr[0] = wa_ref[...].astype(BF16)
        w_scr[1] = wb_ref[...].astype(BF16)
        if cast_down:
            wdb_ref[...] = wd_ref[...].astype(BF16)

    if carried:
        @pl.when(rt == 0)
        def _():
            carry_scr[0] = ha_ref[...]
            carry_scr[1] = hb_ref[...]

    h = h_ref[...].reshape(bb * r, d)
    conv = []
    for half, (cw_ref, cbias_ref, hist_ref, cout_ref) in enumerate(
            ((cwa_ref, cba_ref, ha_ref, ca_ref), (cwb_ref, cbb_ref, hb_ref, cb_ref))):
        u = jnp.dot(h, w_scr[half], preferred_element_type=F32).reshape(bb, r, tf)
        ext_scr[half, :, HALO:HALO + r, :] = u
        ext_scr[half, :, HALO - 2:HALO, :] = carry_scr[half] if carried else hist_ref[...]
        u1 = ext_scr[half, :, HALO - 1:HALO - 1 + r, :]
        u2 = ext_scr[half, :, HALO - 2:HALO - 2 + r, :]
        cw = cw_ref[...]
        conv.append(cbias_ref[...] + cw[0:1] * u2 + cw[1:2] * u1 + cw[2:3] * u)
        if carried:
            carry_scr[half] = ext_scr[half, :, HALO + r - 2:HALO + r, :]
        cout_ref[0] = ext_scr[half, :, HALO + loc:HALO + loc + 2, :]

    act_ref[...] = ((conv[0] * _sigmoid(conv[0])) * conv[1]).astype(act_ref.dtype)


def _ffn_down_kernel(act_ref, wd_ref, x1_ref, gf_ref, y_ref):
    down = jnp.dot(act_ref[...], wd_ref[...], preferred_element_type=F32)
    y_ref[...] = _rmsnorm(x1_ref[...] + down, gf_ref[...])


def _ffn(h2, x1, w_up, w_down, conv_w, conv_b, hist, g_final, *, B, T, Tp):
    d = h2.shape[1]
    dff = w_down.shape[0]
    cast_down = w_down.dtype != BF16
    tf = _pick(dff, (512, 256, 128))
    nf = dff // tf
    if Tp <= 128:
        bb, r = B, Tp
    else:
        bb, r = 1, _pick(Tp, (ROW_TILE, 128))
    nrt = Tp // r
    carried = nrt > 1
    loc = (T - 2) - (nrt - 1) * r
    assert 0 <= loc <= r - 2, "final two valid rows must sit in the last row tile"
    kern = functools.partial(_ffn_up_kernel, bb=bb, r=r, tf=tf, loc=loc, carried=carried,
                             cast_down=cast_down)
    carry_shape = (2, bb, 2, tf) if carried else (1, 1, 2, LANE)
    in_specs = [pl.BlockSpec((bb, r, d), lambda f, s, t: (s, t, 0)),
                pl.BlockSpec((d, tf), lambda f, s, t: (0, f)),
                pl.BlockSpec((d, tf), lambda f, s, t: (0, nf + f)),
                pl.BlockSpec((CONV_W, tf), lambda f, s, t: (0, f)),
                pl.BlockSpec((CONV_W, tf), lambda f, s, t: (0, nf + f)),
                pl.BlockSpec((1, tf), lambda f, s, t: (0, f)),
                pl.BlockSpec((1, tf), lambda f, s, t: (0, nf + f)),
                pl.BlockSpec((bb, 2, tf), lambda f, s, t: (s, 0, f)),
                pl.BlockSpec((bb, 2, tf), lambda f, s, t: (s, 0, nf + f))]
    out_specs = [pl.BlockSpec((bb, r, tf), lambda f, s, t: (s, t, f)),
                 pl.BlockSpec((1, bb, 2, tf), lambda f, s, t: (t, s, 0, f)),
                 pl.BlockSpec((1, bb, 2, tf), lambda f, s, t: (t, s, 0, f))]
    out_shape = [jax.ShapeDtypeStruct((B, Tp, dff), BF16),
                 jax.ShapeDtypeStruct((nrt, B, 2, dff), F32),
                 jax.ShapeDtypeStruct((nrt, B, 2, dff), F32)]
    args = [h2.reshape(B, Tp, d), w_up, w_up, conv_w, conv_w, conv_b.reshape(1, -1),
            conv_b.reshape(1, -1), hist, hist]
    if cast_down:
        in_specs.append(pl.BlockSpec((tf, d), lambda f, s, t: (f, 0)))
        out_specs.append(pl.BlockSpec((tf, d), lambda f, s, t: (f, 0)))
        out_shape.append(jax.ShapeDtypeStruct((dff, d), BF16))
        args.append(w_down)
    outs = pl.pallas_call(
        kern,
        grid=(nf, B // bb, nrt),
        in_specs=in_specs,
        out_specs=out_specs,
        out_shape=out_shape,
        scratch_shapes=[pltpu.VMEM((2, bb, HALO + r, tf), F32),
                        pltpu.VMEM(carry_shape, F32),
                        pltpu.VMEM((2, d, tf), BF16)],
        compiler_params=_cparams(("arbitrary", "arbitrary", "arbitrary")),
        name="conv_ffn_up",
    )(*args)
    act, ca, cb = outs[:3]
    if cast_down:
        w_down = outs[3]

    m = B * Tp
    tm = _pick(m, (256, 128))
    y = pl.pallas_call(
        _ffn_down_kernel,
        grid=(m // tm,),
        in_specs=[pl.BlockSpec((tm, dff), lambda i: (i, 0)),
                  pl.BlockSpec((dff, d), lambda i: (0, 0), pipeline_mode=pl.Buffered(1)),
                  pl.BlockSpec((tm, d), lambda i: (i, 0)),
                  pl.BlockSpec((1, d), lambda i: (0, 0))],
        out_specs=pl.BlockSpec((tm, d), lambda i: (i, 0)),
        out_shape=jax.ShapeDtypeStruct((m, d), F32),
        compiler_params=_cparams(("parallel",)),
        name="ffn_down",
    )(act.reshape(m, dff), w_down, x1, g_final.reshape(1, -1))
    return y.reshape(B, Tp, d), jnp.concatenate([ca[nrt - 1], cb[nrt - 1]], axis=-1), w_down


def _rope_tables(pos):
    half = MLA_ROPE // 2
    inv = ROPE_THETA ** (-jnp.arange(0, MLA_ROPE, 2, dtype=F32) / MLA_ROPE)
    ang = pos.astype(F32)[:, None] * inv[None, :]
    cos, sin = jnp.cos(ang), jnp.sin(ang)
    zero = jnp.zeros((pos.shape[0], LANE - 2 * half), F32)
    return (jnp.concatenate([cos, cos, zero], axis=1),
            jnp.concatenate([-sin, sin, zero], axis=1))


def _attn_tile(T):
    return _pick(T, (1024, 128))


def _project(x, pos, w, chunk_tile=0):
    col = w["col"]
    rows = w["in_rows"]
    cos_t, sin_t = _rope_tables(pos)
    h, small, lat, kr = _front(x, w["g_mix"], w["w_in_t"], rows["a"], w["g_kv"], cos_t, sin_t,
                               rank=rows["cq"] - rows["a"], rq=rows["ckv"] - rows["cq"],
                               rk=rows["kpe"] - rows["ckv"])
    qkvr = _matmul_wt(h, w["w_in_t"], rows["q"], rows["a"] - rows["q"], BF16, tn=1024)
    gates = _matmul_wt(h, w["w_in_t"], rows["ga"], rows["end"] - rows["ga"], BF16, tn=1024)
    q = _qprep(small, w["g_q"], w["wq_nope"], w["wq_pe"], w["wq_pe_sw"], cos_t, sin_t, col=col,
               chunk_tile=chunk_tile)
    return dict(qkvr=qkvr, gates=gates, small=small, q=q, lat=lat, kr=kr)


def _finish(x, pr, branch_a, o_m, w, hist, *, B, T):
    x1, h2 = _merge(branch_a, pr["gates"], o_m, x, w["w_o"], w["g_ffn"], col=w["col"])
    y, conv, w["w_down"] = _ffn(h2, x1, w["w_up"], w["w_down"], w["conv_w"], w["conv_b"], hist,
                                w["final_norm"], B=B, T=T, Tp=T)
    return y, conv


def _gla_group(pr, w, s0, *, B, T, row0=0):
    return _gla(pr["qkvr"], pr["gates"], pr["small"], w["wa_pad"], w["b_a"], w["g_gla_out"],
                s0, B=B, T=T, Tp=T, dk=w["dk"], dv=w["dv"], col=w["col"], row0=row0)


def _long_stream(x, pr, w, *, T, s0, hist, prefix):
    branch_a, state = _gla_group(pr, w, s0, B=1, T=T)
    k, vt = _kvup(pr["lat"], pr["kr"], w["w_uk"], w["w_uv_t"], v_transposed=True)
    o_m = _attention_t(pr["q"], k, vt, prefix[0], prefix[1], T=T, t=_attn_tile(T),
                       hps=MLA_HEADS // 4)
    y, conv = _finish(x, pr, branch_a, o_m, w, hist, B=1, T=T)
    return y, pr["lat"], pr["kr"], state, conv


def _short_streams(x, pr, w, *, B, T, past_lat, past_kr, s0_s, hist_s):
    ns = B * T
    dk, dv = w["dk"], w["dv"]

    ba_s, st_s = _gla_group(pr, w, s0_s, B=B, T=T)
    ba_m, st_m = _gla_group(pr, w, jnp.zeros((1, GLA_HEADS, dk, dv), F32), B=1, T=T, row0=ns)

    qlat = _absorb_q(pr["q"], w["w_uk_t3"])
    olat = _attn_latent(qlat, pr["q"], past_lat, jnp.swapaxes(past_kr, 1, 2), pr["lat"],
                        pr["kr"], B=B, T=T)
    om_s = _absorb_out(olat, w["w_uv3"])
    q_m, lat_m, kr_m = pr["q"][:, ns:], pr["lat"][ns:], pr["kr"][ns:]
    k_m, v_m = _kvup(lat_m, kr_m, w["w_uk"], w["w_uv"])
    prefix = _kvup(lat_m, kr_m, w["w_uk"], w["w_uv_t"], v_transposed=True)
    om_m = _attention(q_m, k_m, v_m, B=1, Tq=T, Tk=T, tq=T, tk=T, hps=MLA_HEADS,
                      q_off=0, k_off=0)

    hist = jnp.concatenate([hist_s, jnp.zeros((1,) + hist_s.shape[1:], F32)], axis=0)
    y, conv = _finish(x, pr, jnp.concatenate([ba_s, ba_m], axis=0),
                      jnp.concatenate([om_s, om_m], axis=0), w, hist, B=B + 1, T=T)
    sample = (y[:B], pr["lat"][:ns], pr["kr"][:ns], st_s, conv[:B])
    meta = (lat_m, kr_m, st_m, conv[B:], prefix)
    return sample, meta


def _prep_weights(g_mix, w_in, w_a2, b_a, g_gla_out, g_q, w_uq, g_kv, w_uk, w_uv, w_o,
                  g_ffn, w_up, conv_w, conv_b, w_down, final_norm):
    d = w_in.shape[0]
    rank, gqk = w_a2.shape
    gvw = GLA_HEADS * g_gla_out.shape[0]
    rq, rk = g_q.shape[0], g_kv.shape[0]
    half = MLA_ROPE // 2
    o, offs = 0, {}
    for name, width in (("q", gqk), ("k", gqk), ("v", gvw), ("r", gvw), ("a", rank),
                        ("cq", rq), ("ckv", rk), ("kpe", MLA_ROPE), ("ga", d), ("gb", d)):
        offs[name] = (o, o + width)
        o += width
    assert o == w_in.shape[1]
    in_rows = {name: lo for name, (lo, _) in offs.items()}
    in_rows["end"] = o
    assert all(v % 16 == 0 for v in in_rows.values())
    col = {"q": 0, "k": gqk, "v": 2 * gqk, "r": 2 * gqk + gvw, "ga": 0, "gb": d,
           "cq": 0, "ckv": rq, "kpe": rq + rk, "a": rq + rk + 2 * MLA_ROPE}

    w3 = w_uq.reshape(rq, MLA_HEADS, MLA_NOPE + MLA_ROPE)
    pe = w3[:, :, MLA_NOPE:]
    pe_sw = jnp.concatenate([pe[:, :, half:], pe[:, :, :half]], axis=2)
    zpad = jnp.zeros((rq, MLA_HEADS, LANE - MLA_ROPE), w_uq.dtype)
    flat = lambda t: t.reshape(rq, -1).astype(BF16)
    wa_pad = jnp.concatenate([w_a2, jnp.zeros((LANE - rank, gqk), w_a2.dtype)], axis=0)
    return dict(
        col=col, dk=gqk // GLA_HEADS, dv=g_gla_out.shape[0],
        g_mix=g_mix, w_in_t=jnp.swapaxes(w_in, 0, 1), in_rows=in_rows,
        wa_pad=wa_pad.astype(BF16), b_a=b_a, g_gla_out=g_gla_out, g_q=g_q,
        wq_nope=flat(w3[:, :, :MLA_NOPE]),
        wq_pe=flat(jnp.concatenate([pe, zpad], axis=2)),
        wq_pe_sw=flat(jnp.concatenate([pe_sw, zpad], axis=2)),
        g_kv=g_kv, w_uk=w_uk.astype(BF16), w_uv=w_uv.astype(BF16),
        w_uv_t=w_uv.T.astype(BF16),
        w_uk_t3=w_uk.reshape(rk, MLA_HEADS, MLA_NOPE).transpose(1, 2, 0).astype(BF16),
        w_uv3=w_uv.reshape(rk, MLA_HEADS, MLA_V).transpose(1, 0, 2).astype(BF16),
        w_o=w_o.astype(BF16),
        g_ffn=g_ffn, w_up=w_up, conv_w=conv_w, conv_b=conv_b,
        w_down=w_down, final_norm=final_norm)


def kernel(x_prompt, x_sample, cache_mla_latent, cache_mla_krope, state_gla, cache_ffn_conv,
           meta_tokens, g_mix, w_in, w_a2, b_a, g_gla_out, g_q, w_uq, g_kv, w_uk, w_uv, w_o,
           g_ffn, w_up, conv_w, conv_b, w_down, final_norm):
    assert w_in.shape[0] == 1, "single trunk layer"
    bp, seq, d = x_prompt.shape
    assert bp == 1
    bs, ts, _ = x_sample.shape
    P = cache_mla_latent.shape[2]
    w = _prep_weights(g_mix[0], w_in[0], w_a2[0], b_a[0], g_gla_out[0], g_q[0], w_uq[0],
                      g_kv[0], w_uk[0], w_uv[0], w_o[0], g_ffn[0], w_up[0], conv_w[0],
                      conv_b[0], w_down[0], final_norm)

    n_meta = meta_tokens.shape[0]
    assert n_meta == N_META == ts and seq % CHUNK == 0
    x_short = jnp.concatenate([x_sample.reshape(bs * ts, d), meta_tokens.astype(F32)], axis=0)
    pos_short = jnp.concatenate([jnp.tile(P + jnp.arange(ts, dtype=jnp.int32), bs),
                                 jnp.arange(n_meta, dtype=jnp.int32)])
    pos_long = n_meta + jnp.arange(seq, dtype=jnp.int32)
    pr_short = _project(x_short, pos_short, w)
    pr_long = _project(x_prompt[0], pos_long, w, chunk_tile=_attn_tile(seq))
    (ys, lat_s, kr_s, st_s, cv_s), (lat_m, kr_m, st_m, cv_m, prefix) = _short_streams(
        x_short, pr_short, w, B=bs, T=ts, past_lat=cache_mla_latent[0],
        past_kr=cache_mla_krope[0], s0_s=state_gla[0], hist_s=cache_ffn_conv[0])
    yp, lat_p, kr_p, st_p, cv_p = _long_stream(
        x_prompt[0], pr_long, w, T=seq, s0=st_m, hist=cv_m, prefix=prefix)

    rk = lat_p.shape[1]
    T = n_meta + seq
    return (yp,
            ys,
            jnp.concatenate([lat_m, lat_p], axis=0).reshape(1, 1, T, rk),
            jnp.concatenate([kr_m, kr_p], axis=0)[:, :MLA_ROPE].reshape(1, 1, T, MLA_ROPE),
            st_p[None],
            cv_p[None],
            lat_s.reshape(1, bs, ts, rk),
            kr_s[:, :MLA_ROPE].reshape(1, bs, ts, MLA_ROPE),
            st_s[None],
            cv_s[None])
```

```python
import functools

import jax
import jax.numpy as jnp
from jax import lax
from jax.experimental import pallas as pl
from jax.experimental.pallas import tpu as pltpu

BF16 = jnp.bfloat16
F32 = jnp.float32

CHUNK = 64
CHUNK_SHIFT = 6
N_META = 16
EPS = 1e-6
GLA_HEADS = 4
GLA_GATE_NORM = 16.0
GLA_LOG_ALPHA_MIN = -5.0
MLA_HEADS = 16
MLA_NOPE = 128
MLA_ROPE = 64
MLA_V = 128
ROPE_THETA = 10000.0
CONV_W = 3
NEG_BIG = -1e30
LOG2E = 1.4426950408889634
QK_SCALE_LOG2E = (MLA_NOPE + MLA_ROPE) ** -0.5 * LOG2E

LANE = 128
VT_ONES = 16
GLA_CHUNK = 256
GLA_SEQS = 4
MASK_LANE0 = MLA_ROPE + 1
SAFE_EXP = 64.0
ROW_TILE = 1024
VMEM_LIMIT = 56 * 1024 * 1024


def _cparams(sem, vmem=VMEM_LIMIT):
    return pltpu.CompilerParams(dimension_semantics=sem, vmem_limit_bytes=vmem)


def _rmsnorm(x, g):
    return x * lax.rsqrt(jnp.mean(x * x, axis=-1, keepdims=True) + EPS) * g


def _sigmoid(x):
    return 0.5 * jnp.tanh(0.5 * x) + 0.5


def _pick(n, cands):
    for c in cands:
        if n % c == 0:
            return c
    fits = [t for t in range(16, min(n, max(cands)) + 1, 16) if n % t == 0]
    if not fits:
        raise ValueError(f"no tile in {cands} divides {n}")
    return fits[-1]


_NT = (((1,), (1,)), ((), ()))


def _matmul_wt_kernel(a_ref, w_ref, o_ref, w_scr):
    @pl.when(pl.program_id(1) == 0)
    def _():
        w_scr[...] = w_ref[...].astype(BF16)

    o_ref[...] = lax.dot_general(a_ref[...], w_scr[...], _NT,
                                 preferred_element_type=F32).astype(o_ref.dtype)


def _matmul_wt(a, w_t, row0, n, out_dtype, tn):
    m, k = a.shape
    tm = _pick(m, (2 * ROW_TILE, ROW_TILE, 512, 384, 128))
    return pl.pallas_call(
        _matmul_wt_kernel,
        grid=(n // tn, m // tm),
        in_specs=[pl.BlockSpec((tm, k), lambda j, i: (i, 0)),
                  pl.BlockSpec((pl.Element(tn), pl.Element(k)),
                               lambda j, i: (pl.multiple_of(row0 + j * tn, 16), 0))],
        out_specs=pl.BlockSpec((tm, tn), lambda j, i: (i, j)),
        out_shape=jax.ShapeDtypeStruct((m, n), out_dtype),
        scratch_shapes=[pltpu.VMEM((tn, k), BF16)],
        compiler_params=_cparams(("parallel", "arbitrary")),
        name="in_proj_wt",
    )(a, w_t)


def _front_kernel(x_ref, g_ref, w_ref, gkv_ref, cos_ref, sin_ref,
                  h_ref, o_ref, lat_ref, kr_ref, w_scr, *, rank, rq, rk):
    @pl.when(pl.program_id(0) == 0)
    def _():
        w = w_ref[...].astype(BF16)
        half = MLA_ROPE // 2
        pe0 = rank + rq + rk
        o_pe = rq + rk
        w_scr[0:rq] = w[rank:rank + rq]
        w_scr[rq:o_pe] = w[rank + rq:pe0]
        w_scr[o_pe:o_pe + MLA_ROPE] = w[pe0:pe0 + MLA_ROPE]
        w_scr[o_pe + MLA_ROPE:o_pe + MLA_ROPE + half] = w[pe0 + half:pe0 + MLA_ROPE]
        w_scr[o_pe + MLA_ROPE + half:o_pe + 2 * MLA_ROPE] = w[pe0:pe0 + half]
        o_a = o_pe + 2 * MLA_ROPE
        w_scr[o_a:o_a + rank] = w[0:rank]
        w_scr[o_a + rank:] = jnp.zeros((w_scr.shape[0] - o_a - rank, w_scr.shape[1]), BF16)

    h = _rmsnorm(x_ref[...], g_ref[...]).astype(BF16)
    h_ref[...] = h
    small = lax.dot_general(h, w_scr[...], _NT, preferred_element_type=F32)
    o_ref[...] = small
    lat_ref[...] = _rmsnorm(small[:, rq:rq + rk], gkv_ref[...])
    blk = small[:, rq + rk:rq + rk + LANE]
    kr_ref[...] = blk * cos_ref[...] + pltpu.roll(blk, LANE // 2, 1) * sin_ref[...]


def _front(x, g_mix, w_t, row0, g_kv, cos_t, sin_t, *, rank, rq, rk):
    m, k = x.shape
    n_in = rank + rq + rk + MLA_ROPE
    n_out = rq + rk + 2 * MLA_ROPE + LANE
    tm = _pick(m, (512, 384, 128))
    kern = functools.partial(_front_kernel, rank=rank, rq=rq, rk=rk)
    row = lambda i: (i, 0)
    return pl.pallas_call(
        kern,
        grid=(m // tm,),
        in_specs=[pl.BlockSpec((tm, k), row),
                  pl.BlockSpec((1, k), lambda i: (0, 0)),
                  pl.BlockSpec((pl.Element(n_in), pl.Element(k)), lambda i: (row0, 0),
                               pipeline_mode=pl.Buffered(1)),
                  pl.BlockSpec((1, rk), lambda i: (0, 0)),
                  pl.BlockSpec((tm, LANE), row),
                  pl.BlockSpec((tm, LANE), row)],
        out_specs=[pl.BlockSpec((tm, k), row),
                   pl.BlockSpec((tm, n_out), row),
                   pl.BlockSpec((tm, rk), row),
                   pl.BlockSpec((tm, LANE), row)],
        out_shape=[jax.ShapeDtypeStruct((m, k), BF16),
                   jax.ShapeDtypeStruct((m, n_out), F32),
                   jax.ShapeDtypeStruct((m, rk), F32),
                   jax.ShapeDtypeStruct((m, LANE), F32)],
        scratch_shapes=[pltpu.VMEM((n_out, k), BF16)],
        compiler_params=_cparams(("arbitrary",)),
        name="front_proj",
    )(x, g_mix.reshape(1, -1), w_t, g_kv.reshape(1, -1), cos_t, sin_t)


def _split3(x):
    a = x.astype(BF16)
    r1 = x - a.astype(F32)
    b = r1.astype(BF16)
    c = (r1 - b.astype(F32)).astype(BF16)
    return a, b, c


def _gla_kernel(q_ref, k_ref, v_ref, r_ref, ga_ref, a_ref, wa_ref, ba_ref, go_ref, s0_ref,
                o_ref, sout_ref, s_scr, *, C, SB, T, H, dk, dv, S):
    c_idx = pl.program_id(1)
    n_chunks = pl.num_programs(1)
    R = S * C

    @pl.when(c_idx == 0)
    def _():
        s_scr[...] = s0_ref[...]

    z = jnp.dot(a_ref[...].astype(BF16), wa_ref[...], preferred_element_type=F32) + ba_ref[...]
    log_sig = jnp.minimum(z, 0.0) - jnp.log(1.0 + jnp.exp(-jnp.abs(z)))
    la = jnp.maximum(log_sig * (1.0 / GLA_GATE_NORM), GLA_LOG_ALPHA_MIN)
    if T % C:
        rows = c_idx * C + lax.broadcasted_iota(jnp.int32, (C, 1), 0)
        la = jnp.where(rows < T, la, 0.0)

    ri = lax.broadcasted_iota(jnp.int32, (R, R), 0)
    ci = lax.broadcasted_iota(jnp.int32, (R, R), 1)
    same_seq = (ri >= ci) if S == 1 else ((ri >= ci) & (ri - ci <= lax.rem(ri, C)))
    tri = jnp.where(same_seq, 1.0, 0.0).astype(BF16)
    ones = jnp.ones((C, LANE), BF16)
    cs_all = jnp.zeros_like(la)
    dsum_all = [jnp.zeros((la.shape[1], LANE), F32) for _ in range(S)]
    for piece in _split3(la):
        cs_all = cs_all + jnp.dot(tri, piece, preferred_element_type=F32)
        for si in range(S):
            dsum_all[si] = dsum_all[si] + lax.dot_general(
                piece[si * C:(si + 1) * C], ones, (((0,), (0,)), ((), ())),
                preferred_element_type=F32)

    sr = lax.broadcasted_iota(jnp.int32, (SB, SB), 0)
    sc = lax.broadcasted_iota(jnp.int32, (SB, SB), 1)
    causal = sr >= sc
    nt = (((1,), (1,)), ((), ()))
    scale = dk ** -0.5

    for si, h in [(si, h) for si in range(S) for h in range(H)]:
        rs = slice(si * C, (si + 1) * C)
        ksl = slice(h * dk, (h + 1) * dk)
        vsl = slice(h * dv, (h + 1) * dv)
        cs = cs_all[rs, ksl]
        c_last = cs[C - 1:C, :]
        q = q_ref[rs, ksl].astype(F32) * scale
        k = k_ref[rs, ksl].astype(F32)
        v = v_ref[rs, vsl]
        s_old = s_scr[si, h]

        o_inter = jnp.dot((q * jnp.exp(cs)).astype(BF16), s_old.astype(BF16),
                          preferred_element_type=F32)
        k_end = (k * jnp.exp(c_last - cs)).astype(BF16)
        upd = lax.dot_general(k_end, v, (((0,), (0,)), ((), ())), preferred_element_type=F32)
        dcol = jnp.exp(dsum_all[si][ksl, :])
        s_scr[si, h] = jnp.concatenate([dcol] * (dv // LANE), axis=1) * s_old + upd

        outs = []
        for i in range(C // SB):
            lo = i * SB
            cs_i = cs[lo:lo + SB]
            q_i = q[lo:lo + SB]
            k_i = k[lo:lo + SB]
            start = cs[lo - 1:lo] if i > 0 else jnp.zeros_like(c_last)
            mid = 0.5 * (start + cs[lo + SB - 1:lo + SB])
            qd = (q_i * jnp.exp(cs_i - mid)).astype(BF16)
            kd = (k_i * jnp.exp(mid - cs_i)).astype(BF16)
            att = lax.dot_general(qd, kd, nt, preferred_element_type=F32)
            att = jnp.where(causal, att, 0.0)
            o_i = jnp.dot(att.astype(BF16), v[lo:lo + SB], preferred_element_type=F32)
            if i > 0:
                qo = (q_i * jnp.exp(cs_i - start)).astype(BF16)
                ko = (k[:lo] * jnp.exp(start - cs[:lo])).astype(BF16)
                att_o = lax.dot_general(qo, ko, nt, preferred_element_type=F32)
                o_i = o_i + jnp.dot(att_o.astype(BF16), v[:lo], preferred_element_type=F32)
            outs.append(o_i)
        o = o_inter + (jnp.concatenate(outs, axis=0) if len(outs) > 1 else outs[0])

        on = _rmsnorm(o, go_ref[...])
        r = r_ref[rs, vsl].astype(F32)
        g = ga_ref[rs, vsl].astype(F32)
        o_ref[rs, vsl] = (_sigmoid(g) * (on * (r * _sigmoid(r)))).astype(o_ref.dtype)

    @pl.when(c_idx == n_chunks - 1)
    def _():
        sout_ref[...] = s_scr[...]


def _gla(qkvr, gates, small, wa_pad, b_a, g_out, s0, *, B, T, Tp, dk, dv, col, row0=0):
    C = min(GLA_CHUNK, Tp)
    SB = min(32, C)
    assert Tp % C == 0 and C % SB == 0
    nc = Tp // C
    H = GLA_HEADS
    qk, vw = H * dk, H * dv
    S = _pick(B, (GLA_SEQS, 1)) if nc == 1 else 1
    R = S * C
    assert row0 % R == 0
    rb = lambda b, c: row0 // R + b * nc + c
    kern = functools.partial(_gla_kernel, C=C, SB=SB, T=T, H=H, dk=dk, dv=dv, S=S)
    return pl.pallas_call(
        kern,
        grid=(B // S, nc),
        in_specs=[
            pl.BlockSpec((R, qk), lambda b, c: (rb(b, c), col["q"] // qk)),
            pl.BlockSpec((R, qk), lambda b, c: (rb(b, c), col["k"] // qk)),
            pl.BlockSpec((R, vw), lambda b, c: (rb(b, c), col["v"] // vw)),
            pl.BlockSpec((R, vw), lambda b, c: (rb(b, c), col["r"] // vw)),
            pl.BlockSpec((R, vw), lambda b, c: (rb(b, c), col["ga"] // vw)),
            pl.BlockSpec((R, LANE), lambda b, c: (rb(b, c), col["a"] // LANE)),
            pl.BlockSpec((LANE, qk), lambda b, c: (0, 0)),
            pl.BlockSpec((1, qk), lambda b, c: (0, 0)),
            pl.BlockSpec((1, dv), lambda b, c: (0, 0)),
            pl.BlockSpec((S, H, dk, dv), lambda b, c: (b, 0, 0, 0)),
        ],
        out_specs=[
            pl.BlockSpec((R, vw), lambda b, c: (b * nc + c, 0)),
            pl.BlockSpec((S, H, dk, dv), lambda b, c: (b, 0, 0, 0)),
        ],
        out_shape=[jax.ShapeDtypeStruct((B * Tp, vw), BF16),
                   jax.ShapeDtypeStruct((B, H, dk, dv), F32)],
        scratch_shapes=[pltpu.VMEM((S, H, dk, dv), F32)],
        compiler_params=_cparams(("parallel", "arbitrary")),
        name="gla",
    )(qkvr, qkvr, qkvr, qkvr, gates, small, wa_pad, b_a.reshape(1, -1), g_out.reshape(1, -1), s0)


def _qprep_kernel(cq_ref, gq_ref, wn_ref, wp_ref, wps_ref, cos_ref, sin_ref, q_ref, *,
                  chunk_tile):
    hq = _rmsnorm(cq_ref[...], gq_ref[...]).astype(BF16)
    qn = jnp.dot(hq, wn_ref[...], preferred_element_type=F32)
    qp = jnp.dot(hq, wp_ref[...], preferred_element_type=F32)
    qs = jnp.dot(hq, wps_ref[...], preferred_element_type=F32)
    cos = cos_ref[...] * QK_SCALE_LOG2E
    sin = sin_ref[...] * QK_SCALE_LOG2E
    tag = 0.0
    if chunk_tile:
        tm = cos.shape[0]
        row = pl.program_id(0) * tm + lax.broadcasted_iota(jnp.int32, (tm, LANE), 0)
        lane = lax.broadcasted_iota(jnp.int32, (tm, LANE), 1)
        chunk = (row & (chunk_tile - 1)) >> CHUNK_SHIFT
        tag = jnp.where(lane - MASK_LANE0 == chunk, 1.0, 0.0)
    for h in range(MLA_HEADS):
        sl = slice(h * LANE, (h + 1) * LANE)
        q_ref[h, :, 0:LANE] = (qn[:, sl] * QK_SCALE_LOG2E).astype(BF16)
        q_ref[h, :, LANE:2 * LANE] = (qp[:, sl] * cos + qs[:, sl] * sin + tag).astype(BF16)


def _qprep(small, g_q, wn, wp, wps, cos_t, sin_t, *, col, chunk_tile=0):
    m = small.shape[0]
    rq = wn.shape[0]
    tm = _pick(m, (512, 256, 128))
    full = lambda i: (0, 0)
    return pl.pallas_call(
        functools.partial(_qprep_kernel, chunk_tile=chunk_tile),
        grid=(m // tm,),
        in_specs=[pl.BlockSpec((tm, rq), lambda i: (i, col["cq"] // rq)),
                  pl.BlockSpec((1, rq), full),
                  pl.BlockSpec(wn.shape, full),
                  pl.BlockSpec(wp.shape, full),
                  pl.BlockSpec(wps.shape, full),
                  pl.BlockSpec((tm, LANE), lambda i: (i, 0)),
                  pl.BlockSpec((tm, LANE), lambda i: (i, 0))],
        out_specs=pl.BlockSpec((MLA_HEADS, tm, 2 * LANE), lambda i: (0, i, 0)),
        out_shape=jax.ShapeDtypeStruct((MLA_HEADS, m, 2 * LANE), BF16),
        compiler_params=_cparams(("parallel",)),
        name="mla_q",
    )(small, g_q.reshape(1, -1), wn, wp, wps, cos_t, sin_t)


def _kvup_kernel(lat_ref, kr_ref, wuk_ref, wuv_ref, k_ref, v_ref, *, v_transposed):
    lat = lat_ref[...].astype(BF16)
    kn = jnp.dot(lat, wuk_ref[...], preferred_element_type=F32)
    kr = kr_ref[...]
    lane = lax.broadcasted_iota(jnp.int32, kr.shape, 1)
    kp = jnp.where(lane == MLA_ROPE, 1.0, kr).astype(BF16)
    if v_transposed:
        vv = lax.dot_general(wuv_ref[...], lat, (((1,), (1,)), ((), ())),
                             preferred_element_type=F32)
    else:
        vv = jnp.dot(lat, wuv_ref[...], preferred_element_type=F32)
    for h in range(MLA_HEADS):
        sl = slice(h * LANE, (h + 1) * LANE)
        k_ref[h, :, 0:LANE] = kn[:, sl].astype(BF16)
        k_ref[h, :, LANE:2 * LANE] = kp
        if v_transposed:
            v_ref[h, 0:LANE, :] = vv[sl, :].astype(BF16)
            v_ref[h, LANE:LANE + VT_ONES, :] = jnp.ones((VT_ONES, vv.shape[1]), BF16)
        else:
            v_ref[h] = vv[:, sl].astype(BF16)


def _kvup(lat, kr, wuk, wuv, *, v_transposed=False):
    m, rk = lat.shape
    tm = _pick(m, (512, 256, 128))
    full = lambda i: (0, 0)
    if v_transposed:
        v_spec = pl.BlockSpec((MLA_HEADS, LANE + VT_ONES, tm), lambda i: (0, 0, i))
        v_shape = (MLA_HEADS, LANE + VT_ONES, m)
    else:
        v_spec = pl.BlockSpec((MLA_HEADS, tm, LANE), lambda i: (0, i, 0))
        v_shape = (MLA_HEADS, m, LANE)
    return pl.pallas_call(
        functools.partial(_kvup_kernel, v_transposed=v_transposed),
        grid=(m // tm,),
        in_specs=[pl.BlockSpec((tm, rk), lambda i: (i, 0)),
                  pl.BlockSpec((tm, LANE), lambda i: (i, 0)),
                  pl.BlockSpec(wuk.shape, full),
                  pl.BlockSpec(wuv.shape, full)],
        out_specs=[pl.BlockSpec((MLA_HEADS, tm, 2 * LANE), lambda i: (0, i, 0)), v_spec],
        out_shape=[jax.ShapeDtypeStruct((MLA_HEADS, m, 2 * LANE), BF16),
                   jax.ShapeDtypeStruct(v_shape, BF16)],
        compiler_params=_cparams(("parallel",)),
        name="mla_kv",
    )(lat, kr, wuk, wuv)


def _last_kblock(qi, *, tq, tk, nk, q_off, k_off):
    top_chunk = ((qi + 1) * tq - 1 + q_off) // CHUNK
    last_key = (top_chunk + 1) * CHUNK - 1 - k_off
    return jnp.minimum(last_key // tk, nk - 1)


def _attn_kernel(q_ref, k_ref, v_ref, o_ref, m_scr, l_scr, acc_scr, *, hps, tq, tk, nk,
                 q_off, k_off):
    qi = pl.program_id(2)
    ki = pl.program_id(3)

    @pl.when(ki == 0)
    def _():
        m_scr[...] = jnp.full(m_scr.shape, NEG_BIG, F32)
        l_scr[...] = jnp.zeros(l_scr.shape, F32)
        acc_scr[...] = jnp.zeros(acc_scr.shape, F32)

    @pl.when(ki <= _last_kblock(qi, tq=tq, tk=tk, nk=nk, q_off=q_off, k_off=k_off))
    def _():
        q_chunk = (qi * tq + q_off + lax.broadcasted_iota(jnp.int32, (tq, 1), 0)) >> CHUNK_SHIFT
        k_chunk = (ki * tk + k_off + lax.broadcasted_iota(jnp.int32, (1, tk), 1)) >> CHUNK_SHIFT
        visible = q_chunk >= k_chunk

        def head(h, carry):
            s = lax.dot_general(q_ref[h], k_ref[h], (((1,), (1,)), ((), ())),
                                preferred_element_type=F32)
            s = jnp.where(visible, s, NEG_BIG)
            m_prev = m_scr[h]
            m_new = jnp.maximum(m_prev, jnp.max(s, axis=-1, keepdims=True))
            p = jnp.exp2(s - m_new)
            alpha = jnp.exp2(m_prev - m_new)
            l_scr[h] = alpha * l_scr[h] + jnp.sum(p, axis=-1, keepdims=True)
            acc_scr[h] = alpha * acc_scr[h] + jnp.dot(p.astype(BF16), v_ref[h],
                                                      preferred_element_type=F32)
            m_scr[h] = m_new
            return carry

        lax.fori_loop(0, hps, head, 0)

    @pl.when(ki == nk - 1)
    def _():
        for h in range(hps):
            o_ref[:, h * LANE:(h + 1) * LANE] = (acc_scr[h] / l_scr[h]).astype(o_ref.dtype)


def _attention(q, k, v, *, B, Tq, Tk, tq, tk, hps, q_off, k_off):
    nq = Tq // tq
    nk = Tk // tk
    hg = MLA_HEADS // hps
    dqk = q.shape[2]
    dvh = v.shape[2]
    last = functools.partial(_last_kblock, tq=tq, tk=tk, nk=nk, q_off=q_off, k_off=k_off)
    kern = functools.partial(_attn_kernel, hps=hps, tq=tq, tk=tk, nk=nk, q_off=q_off,
                             k_off=k_off)
    kv_row = lambda b, g, i, j: b * nk + jnp.minimum(j, last(i))
    return pl.pallas_call(
        kern,
        grid=(B, hg, nq, nk),
        in_specs=[pl.BlockSpec((hps, tq, dqk), lambda b, g, i, j: (g, b * nq + i, 0)),
                  pl.BlockSpec((hps, tk, dqk), lambda b, g, i, j: (g, kv_row(b, g, i, j), 0)),
                  pl.BlockSpec((hps, tk, dvh), lambda b, g, i, j: (g, kv_row(b, g, i, j), 0))],
        out_specs=pl.BlockSpec((tq, hps * dvh), lambda b, g, i, j: (b * nq + i, g)),
        out_shape=jax.ShapeDtypeStruct((B * Tq, MLA_HEADS * dvh), BF16),
        scratch_shapes=[pltpu.VMEM((hps, tq, 1), F32),
                        pltpu.VMEM((hps, tq, 1), F32),
                        pltpu.VMEM((hps, tq, dvh), F32)],
        compiler_params=_cparams(("parallel", "parallel", "parallel", "arbitrary")),
        name="mla_attn",
    )(q, k, v)


def _attn_t_kernel(qi_ref, ki_ref, q_ref, k_ref, vt_ref, kp_ref, vtp_ref, o_ref,
                   q_scr, r_scr, acc_scr, worst_scr, *, hps, t):
    pair = pl.program_id(1)
    qi = qi_ref[pair]
    ki = ki_ref[pair]
    nt = (((1,), (1,)), ((), ()))
    pe = slice(LANE, 2 * LANE)
    lane = lax.broadcasted_iota(jnp.int32, (t, LANE), 1)

    def set_reference(h, r):
        neg_r = jnp.transpose(jnp.broadcast_to(-r, (LANE, t)))
        q_scr[h, :, pe] = jnp.where(lane == MLA_ROPE, neg_r.astype(BF16), q_ref[h, :, pe])
        r_scr[h] = r

    def shifted_scores(h, own_tile=False):
        k = k_ref[h]
        if own_tile:
            ahead = lane - MASK_LANE0
            k_chunk = lax.broadcasted_iota(jnp.int32, (t, LANE), 0) >> CHUNK_SHIFT
            hidden = (ahead >= 0) & (ahead < k_chunk)
            k = jnp.concatenate(
                [k[:, 0:LANE], jnp.where(hidden, jnp.asarray(NEG_BIG, BF16), k[:, pe])], axis=1)
        return lax.dot_general(k, q_scr[h], nt, preferred_element_type=F32)

    @pl.when(ki == 0)
    def _():
        for h in range(hps):
            q_scr[h, :, 0:LANE] = q_ref[h, :, 0:LANE]
            s = lax.dot_general(kp_ref[h], q_ref[h], nt, preferred_element_type=F32)
            r = jnp.max(s, axis=0, keepdims=True).astype(BF16).astype(F32)
            p = jnp.exp2((s - r).astype(BF16))
            acc_scr[h] = jnp.dot(vtp_ref[h], p, preferred_element_type=F32)
            set_reference(h, r)

    def general(h, own_tile):
        sp = shifted_scores(h, own_tile)
        r = r_scr[h]
        rise = jnp.maximum(jnp.max(sp, axis=0, keepdims=True), 0.0)
        r_new = (r + rise).astype(BF16).astype(F32)
        delta = r_new - r
        p = jnp.exp2((sp - delta).astype(BF16))
        acc_scr[h] = jnp.exp2(-delta) * acc_scr[h] + jnp.dot(vt_ref[h], p,
                                                               preferred_element_type=F32)
        if not own_tile:
            set_reference(h, r_new)

    def tile_step(own_tile):
        worst_all = None
        sp_next = shifted_scores(0, own_tile)
        for h in range(hps):
            sp = sp_next
            if h + 1 < hps:
                sp_next = shifted_scores(h + 1, own_tile)
            worst = jnp.max(jnp.max(sp, axis=0, keepdims=True), axis=1, keepdims=True)
            part = jnp.dot(vt_ref[h], jnp.exp2(sp.astype(BF16)), preferred_element_type=F32)
            acc_scr[h] += jnp.where(worst <= SAFE_EXP, part, 0.0)
            worst_scr[h] = jnp.broadcast_to(worst, (1, LANE))
            worst_all = worst if worst_all is None else jnp.maximum(worst_all, worst)

        @pl.when(jnp.max(worst_all) > SAFE_EXP)
        def _():
            def redo(h, carry):
                @pl.when(jnp.max(worst_scr[h]) > SAFE_EXP)
                def _():
                    general(h, own_tile)
                return carry
            lax.fori_loop(0, hps, redo, 0)

    @pl.when(ki < qi)
    def _():
        tile_step(False)

    @pl.when(ki == qi)
    def _():
        tile_step(True)
        for h in range(hps):
            acc = acc_scr[h]
            o_t = acc[0:LANE] / acc[LANE:LANE + 1]
            o_ref[:, h * LANE:(h + 1) * LANE] = o_t.T.astype(o_ref.dtype)


def _attention_t(q, k, vt, k_pre, vt_pre, *, T, t, hps):
    assert T % t == 0 and t % CHUNK == 0 and t & (t - 1) == 0
    assert t // CHUNK <= LANE - MASK_LANE0
    n = T // t
    hg = MLA_HEADS // hps
    dqk = q.shape[2]
    npre = k_pre.shape[1]
    vrows = vt.shape[1]
    pairs = [(i, j) for i in range(n) for j in range(i + 1)]
    qi_arr = jnp.asarray([p[0] for p in pairs], jnp.int32)
    ki_arr = jnp.asarray([p[1] for p in pairs], jnp.int32)
    kern = functools.partial(_attn_t_kernel, hps=hps, t=t)
    grid_spec = pltpu.PrefetchScalarGridSpec(
        num_scalar_prefetch=2,
        grid=(hg, len(pairs)),
        in_specs=[pl.BlockSpec((hps, t, dqk), lambda g, p, qi, ki: (g, qi[p], 0)),
                  pl.BlockSpec((hps, t, dqk), lambda g, p, qi, ki: (g, ki[p], 0)),
                  pl.BlockSpec((hps, vrows, t), lambda g, p, qi, ki: (g, 0, ki[p])),
                  pl.BlockSpec((hps, npre, dqk), lambda g, p, qi, ki: (g, 0, 0)),
                  pl.BlockSpec((hps, vrows, npre), lambda g, p, qi, ki: (g, 0, 0))],
        out_specs=pl.BlockSpec((t, hps * LANE), lambda g, p, qi, ki: (qi[p], g)),
        scratch_shapes=[pltpu.VMEM((hps, t, dqk), BF16),
                        pltpu.VMEM((hps, 1, t), F32),
                        pltpu.VMEM((hps, vrows, t), F32),
                        pltpu.VMEM((hps, 1, LANE), F32)])
    return pl.pallas_call(
        kern,
        grid_spec=grid_spec,
        out_shape=jax.ShapeDtypeStruct((T, MLA_HEADS * LANE), BF16),
        compiler_params=_cparams(("parallel", "arbitrary")),
        name="mla_attn_t",
    )(qi_arr, ki_arr, q, k, vt, k_pre, vt_pre)


def _absorb_q_kernel(q_ref, w_ref, o_ref):
    o_ref[0] = jnp.dot(q_ref[0, :, 0:MLA_NOPE], w_ref[0],
                       preferred_element_type=F32).astype(o_ref.dtype)


def _absorb_q(q, w_uk_t3):
    heads, rows, dqk = q.shape
    rk = w_uk_t3.shape[2]
    return pl.pallas_call(
        _absorb_q_kernel,
        grid=(heads,),
        in_specs=[pl.BlockSpec((1, rows, dqk), lambda h: (h, 0, 0)),
                  pl.BlockSpec((1, MLA_NOPE, rk), lambda h: (h, 0, 0))],
        out_specs=pl.BlockSpec((1, rows, rk), lambda h: (h, 0, 0)),
        out_shape=jax.ShapeDtypeStruct((heads, rows, rk), BF16),
        compiler_params=_cparams(("parallel",)),
        name="mla_absorb_q",
    )(q, w_uk_t3)


def _attn_latent_kernel(ql_ref, q_ref, plat_ref, pkr_ref, lat_ref, kr_ref, o_ref, *, T, P, S):
    heads, _, rk = ql_ref.shape
    rows = heads * T
    nt = (((1,), (1,)), ((), ()))
    tok = lax.rem(lax.broadcasted_iota(jnp.int32, (rows, 1), 0), T)
    q_chunk = (P + tok) >> CHUNK_SHIFT
    k_chunk = lax.broadcasted_iota(jnp.int32, (1, P + T), 1) >> CHUNK_SHIFT
    visible = q_chunk >= k_chunk
    for si in range(S):
        ts = slice(si * T, (si + 1) * T)
        ql = ql_ref[:, ts, :].reshape(rows, rk)
        qpe = q_ref[:, ts, LANE:2 * LANE].reshape(rows, LANE)[:, 0:MLA_ROPE]
        lat_all = jnp.concatenate([plat_ref[si].astype(BF16), lat_ref[ts, :].astype(BF16)],
                                  axis=0)
        s_pe = jnp.concatenate(
            [jnp.dot(qpe, pkr_ref[si].astype(BF16), preferred_element_type=F32),
             lax.dot_general(qpe, kr_ref[ts, 0:MLA_ROPE].astype(BF16), nt,
                             preferred_element_type=F32)], axis=1)
        s = lax.dot_general(ql, lat_all, nt, preferred_element_type=F32) + s_pe
        s = jnp.where(visible, s, NEG_BIG)
        p = jnp.exp2(s - jnp.max(s, axis=-1, keepdims=True))
        o = jnp.dot(p.astype(BF16), lat_all, preferred_element_type=F32)
        o = o / jnp.sum(p, axis=-1, keepdims=True)
        o_ref[:, ts, :] = o.reshape(heads, T, rk).astype(o_ref.dtype)


def _attn_latent(qlat, q, past_lat, past_kr_t, lat, kr, *, B, T):
    heads, _, rk = qlat.shape
    P = past_lat.shape[1]
    S = _pick(B, (2, 1))
    kern = functools.partial(_attn_latent_kernel, T=T, P=P, S=S)
    return pl.pallas_call(
        kern,
        grid=(B // S,),
        in_specs=[pl.BlockSpec((heads, S * T, rk), lambda b: (0, b, 0)),
                  pl.BlockSpec((heads, S * T, q.shape[2]), lambda b: (0, b, 0)),
                  pl.BlockSpec((S, P, rk), lambda b: (b, 0, 0)),
                  pl.BlockSpec((S, past_kr_t.shape[1], P), lambda b: (b, 0, 0)),
                  pl.BlockSpec((S * T, rk), lambda b: (b, 0)),
                  pl.BlockSpec((S * T, LANE), lambda b: (b, 0))],
        out_specs=pl.BlockSpec((heads, S * T, rk), lambda b: (0, b, 0)),
        out_shape=jax.ShapeDtypeStruct((heads, B * T, rk), BF16),
        compiler_params=_cparams(("parallel",)),
        name="mla_attn_latent",
    )(qlat, q, past_lat, past_kr_t, lat, kr)


def _absorb_out_kernel(o_ref, w_ref, out_ref):
    out_ref[...] = jnp.dot(o_ref[0], w_ref[0], preferred_element_type=F32).astype(out_ref.dtype)


def _absorb_out(olat, w_uv3):
    heads, rows, rk = olat.shape
    dvh = w_uv3.shape[2]
    return pl.pallas_call(
        _absorb_out_kernel,
        grid=(heads,),
        in_specs=[pl.BlockSpec((1, rows, rk), lambda h: (h, 0, 0)),
                  pl.BlockSpec((1, rk, dvh), lambda h: (h, 0, 0))],
        out_specs=pl.BlockSpec((rows, dvh), lambda h: (0, h)),
        out_shape=jax.ShapeDtypeStruct((rows, heads * dvh), BF16),
        compiler_params=_cparams(("parallel",)),
        name="mla_absorb_out",
    )(olat, w_uv3)


def _merge_kernel(a_ref, gb_ref, om_ref, x_ref, wo_ref, gf_ref, x1_ref, h2_ref):
    merged = a_ref[...].astype(F32) + _sigmoid(gb_ref[...].astype(F32)) * om_ref[...].astype(F32)
    x1 = x_ref[...] + jnp.dot(merged.astype(BF16), wo_ref[...], preferred_element_type=F32)
    x1_ref[...] = x1
    h2_ref[...] = _rmsnorm(x1, gf_ref[...]).astype(BF16)


def _merge(branch_a, gates, o_m, x, wo, g_ffn, *, col):
    m, d = x.shape
    tm = _pick(m, (512, 384, 256, 128))
    row = lambda i: (i, 0)
    return pl.pallas_call(
        _merge_kernel,
        grid=(m // tm,),
        in_specs=[pl.BlockSpec((tm, d), row),
                  pl.BlockSpec((tm, d), lambda i: (i, col["gb"] // d)),
                  pl.BlockSpec((tm, d), row),
                  pl.BlockSpec((tm, d), row),
                  pl.BlockSpec(wo.shape, lambda i: (0, 0), pipeline_mode=pl.Buffered(1)),
                  pl.BlockSpec((1, d), lambda i: (0, 0))],
        out_specs=[pl.BlockSpec((tm, d), row), pl.BlockSpec((tm, d), row)],
        out_shape=[jax.ShapeDtypeStruct((m, d), F32), jax.ShapeDtypeStruct((m, d), BF16)],
        compiler_params=_cparams(("parallel",)),
        name="merge_out_proj",
    )(branch_a, gates, o_m, x, wo, g_ffn.reshape(1, -1))


HALO = 8


def _ffn_up_kernel(*refs, bb, r, tf, loc, carried, cast_down):
    (h_ref, wa_ref, wb_ref, cwa_ref, cwb_ref, cba_ref, cbb_ref, ha_ref, hb_ref), refs = \
        refs[:9], refs[9:]
    if cast_down:
        wd_ref, act_ref, ca_ref, cb_ref, wdb_ref, ext_scr, carry_scr, w_scr = refs
    else:
        act_ref, ca_ref, cb_ref, ext_scr, carry_scr, w_scr = refs
    s = pl.program_id(1)
    rt = pl.program_id(2)
    d = h_ref.shape[2]

    @pl.when((s == 0) & (rt == 0))
    def _():
        w_scr[0] = wa_ref[...].astype(BF16)
        w_scr[1] = wb_ref[...].astype(BF16)
        if cast_down:
            wdb_ref[...] = wd_ref[...].astype(BF16)

    if carried:
        @pl.when(rt == 0)
        def _():
            carry_scr[0] = ha_ref[...]
            carry_scr[1] = hb_ref[...]

    h = h_ref[...].reshape(bb * r, d)
    conv = []
    for half, (cw_ref, cbias_ref, hist_ref, cout_ref) in enumerate(
            ((cwa_ref, cba_ref, ha_ref, ca_ref), (cwb_ref, cbb_ref, hb_ref, cb_ref))):
        u = jnp.dot(h, w_scr[half], preferred_element_type=F32).reshape(bb, r, tf)
        ext_scr[half, :, HALO:HALO + r, :] = u
        ext_scr[half, :, HALO - 2:HALO, :] = carry_scr[half] if carried else hist_ref[...]
        u1 = ext_scr[half, :, HALO - 1:HALO - 1 + r, :]
        u2 = ext_scr[half, :, HALO - 2:HALO - 2 + r, :]
        cw = cw_ref[...]
        conv.append(cbias_ref[...] + cw[0:1] * u2 + cw[1:2] * u1 + cw[2:3] * u)
        if carried:
            carry_scr[half] = ext_scr[half, :, HALO + r - 2:HALO + r, :]
        cout_ref[0] = ext_scr[half, :, HALO + loc:HALO + loc + 2, :]

    act_ref[...] = ((conv[0] * _sigmoid(conv[0])) * conv[1]).astype(act_ref.dtype)


def _ffn_down_kernel(act_ref, wd_ref, x1_ref, gf_ref, y_ref):
    down = jnp.dot(act_ref[...], wd_ref[...], preferred_element_type=F32)
    y_ref[...] = _rmsnorm(x1_ref[...] + down, gf_ref[...])


def _ffn(h2, x1, w_up, w_down, conv_w, conv_b, hist, g_final, *, B, T, Tp):
    d = h2.shape[1]
    dff = w_down.shape[0]
    cast_down = w_down.dtype != BF16
    tf = _pick(dff, (512, 256, 128))
    nf = dff // tf
    if Tp <= 128:
        bb, r = B, Tp
    else:
        bb, r = 1, _pick(Tp, (ROW_TILE, 128))
    nrt = Tp // r
    carried = nrt > 1
    loc = (T - 2) - (nrt - 1) * r
    assert 0 <= loc <= r - 2, "final two valid rows must sit in the last row tile"
    kern = functools.partial(_ffn_up_kernel, bb=bb, r=r, tf=tf, loc=loc, carried=carried,
                             cast_down=cast_down)
    carry_shape = (2, bb, 2, tf) if carried else (1, 1, 2, LANE)
    in_specs = [pl.BlockSpec((bb, r, d), lambda f, s, t: (s, t, 0)),
                pl.BlockSpec((d, tf), lambda f, s, t: (0, f)),
                pl.BlockSpec((d, tf), lambda f, s, t: (0, nf + f)),
                pl.BlockSpec((CONV_W, tf), lambda f, s, t: (0, f)),
                pl.BlockSpec((CONV_W, tf), lambda f, s, t: (0, nf + f)),
                pl.BlockSpec((1, tf), lambda f, s, t: (0, f)),
                pl.BlockSpec((1, tf), lambda f, s, t: (0, nf + f)),
                pl.BlockSpec((bb, 2, tf), lambda f, s, t: (s, 0, f)),
                pl.BlockSpec((bb, 2, tf), lambda f, s, t: (s, 0, nf + f))]
    out_specs = [pl.BlockSpec((bb, r, tf), lambda f, s, t: (s, t, f)),
                 pl.BlockSpec((1, bb, 2, tf), lambda f, s, t: (t, s, 0, f)),
                 pl.BlockSpec((1, bb, 2, tf), lambda f, s, t: (t, s, 0, f))]
    out_shape = [jax.ShapeDtypeStruct((B, Tp, dff), BF16),
                 jax.ShapeDtypeStruct((nrt, B, 2, dff), F32),
                 jax.ShapeDtypeStruct((nrt, B, 2, dff), F32)]
    args = [h2.reshape(B, Tp, d), w_up, w_up, conv_w, conv_w, conv_b.reshape(1, -1),
            conv_b.reshape(1, -1), hist, hist]
    if cast_down:
        in_specs.append(pl.BlockSpec((tf, d), lambda f, s, t: (f, 0)))
        out_specs.append(pl.BlockSpec((tf, d), lambda f, s, t: (f, 0)))
        out_shape.append(jax.ShapeDtypeStruct((dff, d), BF16))
        args.append(w_down)
    outs = pl.pallas_call(
        kern,
        grid=(nf, B // bb, nrt),
        in_specs=in_specs,
        out_specs=out_specs,
        out_shape=out_shape,
        scratch_shapes=[pltpu.VMEM((2, bb, HALO + r, tf), F32),
                        pltpu.VMEM(carry_shape, F32),
                        pltpu.VMEM((2, d, tf), BF16)],
        compiler_params=_cparams(("arbitrary", "arbitrary", "arbitrary")),
        name="conv_ffn_up",
    )(*args)
    act, ca, cb = outs[:3]
    if cast_down:
        w_down = outs[3]

    m = B * Tp
    tm = _pick(m, (512, 256, 128))
    y = pl.pallas_call(
        _ffn_down_kernel,
        grid=(m // tm,),
        in_specs=[pl.BlockSpec((tm, dff), lambda i: (i, 0)),
                  pl.BlockSpec((dff, d), lambda i: (0, 0), pipeline_mode=pl.Buffered(1)),
                  pl.BlockSpec((tm, d), lambda i: (i, 0)),
                  pl.BlockSpec((1, d), lambda i: (0, 0))],
        out_specs=pl.BlockSpec((tm, d), lambda i: (i, 0)),
        out_shape=jax.ShapeDtypeStruct((m, d), F32),
        compiler_params=_cparams(("parallel",)),
        name="ffn_down",
    )(act.reshape(m, dff), w_down, x1, g_final.reshape(1, -1))
    return y.reshape(B, Tp, d), jnp.concatenate([ca[nrt - 1], cb[nrt - 1]], axis=-1), w_down


def _rope_tables(pos):
    half = MLA_ROPE // 2
    inv = ROPE_THETA ** (-jnp.arange(0, MLA_ROPE, 2, dtype=F32) / MLA_ROPE)
    ang = pos.astype(F32)[:, None] * inv[None, :]
    cos, sin = jnp.cos(ang), jnp.sin(ang)
    zero = jnp.zeros((pos.shape[0], LANE - 2 * half), F32)
    return (jnp.concatenate([cos, cos, zero], axis=1),
            jnp.concatenate([-sin, sin, zero], axis=1))


def _attn_tile(T):
    return _pick(T, (1024, 128))


def _project(x, pos, w, chunk_tile=0):
    col = w["col"]
    rows = w["in_rows"]
    cos_t, sin_t = _rope_tables(pos)
    h, small, lat, kr = _front(x, w["g_mix"], w["w_in_t"], rows["a"], w["g_kv"], cos_t, sin_t,
                               rank=rows["cq"] - rows["a"], rq=rows["ckv"] - rows["cq"],
                               rk=rows["kpe"] - rows["ckv"])
    qkvr = _matmul_wt(h, w["w_in_t"], rows["q"], rows["a"] - rows["q"], BF16, tn=1024)
    gates = _matmul_wt(h, w["w_in_t"], rows["ga"], rows["end"] - rows["ga"], BF16, tn=1024)
    q = _qprep(small, w["g_q"], w["wq_nope"], w["wq_pe"], w["wq_pe_sw"], cos_t, sin_t, col=col,
               chunk_tile=chunk_tile)
    return dict(qkvr=qkvr, gates=gates, small=small, q=q, lat=lat, kr=kr)


def _finish(x, pr, branch_a, o_m, w, hist, *, B, T):
    x1, h2 = _merge(branch_a, pr["gates"], o_m, x, w["w_o"], w["g_ffn"], col=w["col"])
    y, conv, w["w_down"] = _ffn(h2, x1, w["w_up"], w["w_down"], w["conv_w"], w["conv_b"], hist,
                                w["final_norm"], B=B, T=T, Tp=T)
    return y, conv


def _gla_group(pr, w, s0, *, B, T, row0=0):
    return _gla(pr["qkvr"], pr["gates"], pr["small"], w["wa_pad"], w["b_a"], w["g_gla_out"],
                s0, B=B, T=T, Tp=T, dk=w["dk"], dv=w["dv"], col=w["col"], row0=row0)


def _long_stream(x, pr, w, *, T, s0, hist, prefix):
    branch_a, state = _gla_group(pr, w, s0, B=1, T=T)
    k, vt = _kvup(pr["lat"], pr["kr"], w["w_uk"], w["w_uv_t"], v_transposed=True)
    o_m = _attention_t(pr["q"], k, vt, prefix[0], prefix[1], T=T, t=_attn_tile(T),
                       hps=MLA_HEADS // 4)
    y, conv = _finish(x, pr, branch_a, o_m, w, hist, B=1, T=T)
    return y, pr["lat"], pr["kr"], state, conv


def _short_streams(x, pr, w, *, B, T, past_lat, past_kr, s0_s, hist_s):
    ns = B * T
    dk, dv = w["dk"], w["dv"]

    ba_s, st_s = _gla_group(pr, w, s0_s, B=B, T=T)
    ba_m, st_m = _gla_group(pr, w, jnp.zeros((1, GLA_HEADS, dk, dv), F32), B=1, T=T, row0=ns)

    qlat = _absorb_q(pr["q"], w["w_uk_t3"])
    olat = _attn_latent(qlat, pr["q"], past_lat, jnp.swapaxes(past_kr, 1, 2), pr["lat"],
                        pr["kr"], B=B, T=T)
    om_s = _absorb_out(olat, w["w_uv3"])
    q_m, lat_m, kr_m = pr["q"][:, ns:], pr["lat"][ns:], pr["kr"][ns:]
    k_m, v_m = _kvup(lat_m, kr_m, w["w_uk"], w["w_uv"])
    prefix = _kvup(lat_m, kr_m, w["w_uk"], w["w_uv_t"], v_transposed=True)
    om_m = _attention(q_m, k_m, v_m, B=1, Tq=T, Tk=T, tq=T, tk=T, hps=MLA_HEADS,
                      q_off=0, k_off=0)

    hist = jnp.concatenate([hist_s, jnp.zeros((1,) + hist_s.shape[1:], F32)], axis=0)
    y, conv = _finish(x, pr, jnp.concatenate([ba_s, ba_m], axis=0),
                      jnp.concatenate([om_s, om_m], axis=0), w, hist, B=B + 1, T=T)
    sample = (y[:B], pr["lat"][:ns], pr["kr"][:ns], st_s, conv[:B])
    meta = (lat_m, kr_m, st_m, conv[B:], prefix)
    return sample, meta


def _prep_weights(g_mix, w_in, w_a2, b_a, g_gla_out, g_q, w_uq, g_kv, w_uk, w_uv, w_o,
                  g_ffn, w_up, conv_w, conv_b, w_down, final_norm):
    d = w_in.shape[0]
    rank, gqk = w_a2.shape
    gvw = GLA_HEADS * g_gla_out.shape[0]
    rq, rk = g_q.shape[0], g_kv.shape[0]
    half = MLA_ROPE // 2
    o, offs = 0, {}
    for name, width in (("q", gqk), ("k", gqk), ("v", gvw), ("r", gvw), ("a", rank),
                        ("cq", rq), ("ckv", rk), ("kpe", MLA_ROPE), ("ga", d), ("gb", d)):
        offs[name] = (o, o + width)
        o += width
    assert o == w_in.shape[1]
    in_rows = {name: lo for name, (lo, _) in offs.items()}
    in_rows["end"] = o
    assert all(v % 16 == 0 for v in in_rows.values())
    col = {"q": 0, "k": gqk, "v": 2 * gqk, "r": 2 * gqk + gvw, "ga": 0, "gb": d,
           "cq": 0, "ckv": rq, "kpe": rq + rk, "a": rq + rk + 2 * MLA_ROPE}

    w3 = w_uq.reshape(rq, MLA_HEADS, MLA_NOPE + MLA_ROPE)
    pe = w3[:, :, MLA_NOPE:]
    pe_sw = jnp.concatenate([pe[:, :, half:], pe[:, :, :half]], axis=2)
    zpad = jnp.zeros((rq, MLA_HEADS, LANE - MLA_ROPE), w_uq.dtype)
    flat = lambda t: t.reshape(rq, -1).astype(BF16)
    wa_pad = jnp.concatenate([w_a2, jnp.zeros((LANE - rank, gqk), w_a2.dtype)], axis=0)
    return dict(
        col=col, dk=gqk // GLA_HEADS, dv=g_gla_out.shape[0],
        g_mix=g_mix, w_in_t=jnp.swapaxes(w_in, 0, 1), in_rows=in_rows,
        wa_pad=wa_pad.astype(BF16), b_a=b_a, g_gla_out=g_gla_out, g_q=g_q,
        wq_nope=flat(w3[:, :, :MLA_NOPE]),
        wq_pe=flat(jnp.concatenate([pe, zpad], axis=2)),
        wq_pe_sw=flat(jnp.concatenate([pe_sw, zpad], axis=2)),
        g_kv=g_kv, w_uk=w_uk.astype(BF16), w_uv=w_uv.astype(BF16),
        w_uv_t=w_uv.T.astype(BF16),
        w_uk_t3=w_uk.reshape(rk, MLA_HEADS, MLA_NOPE).transpose(1, 2, 0).astype(BF16),
        w_uv3=w_uv.reshape(rk, MLA_HEADS, MLA_V).transpose(1, 0, 2).astype(BF16),
        w_o=w_o.astype(BF16),
        g_ffn=g_ffn, w_up=w_up, conv_w=conv_w, conv_b=conv_b,
        w_down=w_down, final_norm=final_norm)


def kernel(x_prompt, x_sample, cache_mla_latent, cache_mla_krope, state_gla, cache_ffn_conv,
           meta_tokens, g_mix, w_in, w_a2, b_a, g_gla_out, g_q, w_uq, g_kv, w_uk, w_uv, w_o,
           g_ffn, w_up, conv_w, conv_b, w_down, final_norm):
    assert w_in.shape[0] == 1, "single trunk layer"
    bp, seq, d = x_prompt.shape
    assert bp == 1
    bs, ts, _ = x_sample.shape
    P = cache_mla_latent.shape[2]
    w = _prep_weights(g_mix[0], w_in[0], w_a2[0], b_a[0], g_gla_out[0], g_q[0], w_uq[0],
                      g_kv[0], w_uk[0], w_uv[0], w_o[0], g_ffn[0], w_up[0], conv_w[0],
                      conv_b[0], w_down[0], final_norm)

    n_meta = meta_tokens.shape[0]
    assert n_meta == N_META == ts and seq % CHUNK == 0
    x_short = jnp.concatenate([x_sample.reshape(bs * ts, d), meta_tokens.astype(F32)], axis=0)
    pos_short = jnp.concatenate([jnp.tile(P + jnp.arange(ts, dtype=jnp.int32), bs),
                                 jnp.arange(n_meta, dtype=jnp.int32)])
    pos_long = n_meta + jnp.arange(seq, dtype=jnp.int32)
    pr_short = _project(x_short, pos_short, w)
    pr_long = _project(x_prompt[0], pos_long, w, chunk_tile=_attn_tile(seq))
    (ys, lat_s, kr_s, st_s, cv_s), (lat_m, kr_m, st_m, cv_m, prefix) = _short_streams(
        x_short, pr_short, w, B=bs, T=ts, past_lat=cache_mla_latent[0],
        past_kr=cache_mla_krope[0], s0_s=state_gla[0], hist_s=cache_ffn_conv[0])
    yp, lat_p, kr_p, st_p, cv_p = _long_stream(
        x_prompt[0], pr_long, w, T=seq, s0=st_m, hist=cv_m, prefix=prefix)

    rk = lat_p.shape[1]
    T = n_meta + seq
    return (yp,
            ys,
            jnp.concatenate([lat_m, lat_p], axis=0).reshape(1, 1, T, rk),
            jnp.concatenate([kr_m, kr_p], axis=0)[:, :MLA_ROPE].reshape(1, 1, T, MLA_ROPE),
            st_p[None],
            cv_p[None],
            lat_s.reshape(1, bs, ts, rk),
            kr_s[:, :MLA_ROPE].reshape(1, bs, ts, MLA_ROPE),
            st_s[None],
            cv_s[None])
```

```python
import functools

import jax
import jax.numpy as jnp
from jax import lax
from jax.experimental import pallas as pl
from jax.experimental.pallas import tpu as pltpu

BF16 = jnp.bfloat16
F32 = jnp.float32

CHUNK = 64
CHUNK_SHIFT = 6
N_META = 16
EPS = 1e-6
GLA_HEADS = 4
GLA_GATE_NORM = 16.0
GLA_LOG_ALPHA_MIN = -5.0
MLA_HEADS = 16
MLA_NOPE = 128
MLA_ROPE = 64
MLA_V = 128
ROPE_THETA = 10000.0
CONV_W = 3
NEG_BIG = -1e30
LOG2E = 1.4426950408889634
QK_SCALE_LOG2E = (MLA_NOPE + MLA_ROPE) ** -0.5 * LOG2E

LANE = 128
VT_ONES = 16
GLA_CHUNK = 256
GLA_SEQS = 4
MASK_LANE0 = MLA_ROPE + 1
SAFE_EXP = 64.0
ROW_TILE = 1024
VMEM_LIMIT = 56 * 1024 * 1024


def _cparams(sem, vmem=VMEM_LIMIT):
    return pltpu.CompilerParams(dimension_semantics=sem, vmem_limit_bytes=vmem)


def _rmsnorm(x, g):
    return x * lax.rsqrt(jnp.mean(x * x, axis=-1, keepdims=True) + EPS) * g


def _sigmoid(x):
    return 0.5 * jnp.tanh(0.5 * x) + 0.5


def _pick(n, cands):
    for c in cands:
        if n % c == 0:
            return c
    fits = [t for t in range(16, min(n, max(cands)) + 1, 16) if n % t == 0]
    if not fits:
        raise ValueError(f"no tile in {cands} divides {n}")
    return fits[-1]


_NT = (((1,), (1,)), ((), ()))


def _matmul_wt_kernel(a_ref, w_ref, o_ref, w_scr):
    @pl.when(pl.program_id(1) == 0)
    def _():
        w_scr[...] = w_ref[...].astype(BF16)

    o_ref[...] = lax.dot_general(a_ref[...], w_scr[...], _NT,
                                 preferred_element_type=F32).astype(o_ref.dtype)


def _matmul_wt(a, w_t, row0, n, out_dtype, tn):
    m, k = a.shape
    tm = _pick(m, (2 * ROW_TILE, ROW_TILE, 512, 384, 128))
    return pl.pallas_call(
        _matmul_wt_kernel,
        grid=(n // tn, m // tm),
        in_specs=[pl.BlockSpec((tm, k), lambda j, i: (i, 0)),
                  pl.BlockSpec((pl.Element(tn), pl.Element(k)),
                               lambda j, i: (pl.multiple_of(row0 + j * tn, 16), 0))],
        out_specs=pl.BlockSpec((tm, tn), lambda j, i: (i, j)),
        out_shape=jax.ShapeDtypeStruct((m, n), out_dtype),
        scratch_shapes=[pltpu.VMEM((tn, k), BF16)],
        compiler_params=_cparams(("parallel", "arbitrary")),
        name="in_proj_wt",
    )(a, w_t)


def _front_kernel(x_ref, g_ref, w_ref, gkv_ref, cos_ref, sin_ref, gq_ref, wn_ref, wp_ref,
                  wps_ref, h_ref, o_ref, lat_ref, kr_ref, q_ref, w_scr, *, rank, rq, rk,
                  chunk_tile):
    @pl.when(pl.program_id(0) == 0)
    def _():
        w = w_ref[...].astype(BF16)
        half = MLA_ROPE // 2
        pe0 = rank + rq + rk
        o_pe = rq + rk
        w_scr[0:rq] = w[rank:rank + rq]
        w_scr[rq:o_pe] = w[rank + rq:pe0]
        w_scr[o_pe:o_pe + MLA_ROPE] = w[pe0:pe0 + MLA_ROPE]
        w_scr[o_pe + MLA_ROPE:o_pe + MLA_ROPE + half] = w[pe0 + half:pe0 + MLA_ROPE]
        w_scr[o_pe + MLA_ROPE + half:o_pe + 2 * MLA_ROPE] = w[pe0:pe0 + half]
        o_a = o_pe + 2 * MLA_ROPE
        w_scr[o_a:o_a + rank] = w[0:rank]
        w_scr[o_a + rank:] = jnp.zeros((w_scr.shape[0] - o_a - rank, w_scr.shape[1]), BF16)

    h = _rmsnorm(x_ref[...], g_ref[...]).astype(BF16)
    h_ref[...] = h
    small = lax.dot_general(h, w_scr[...], _NT, preferred_element_type=F32)
    o_ref[...] = small
    lat_ref[...] = _rmsnorm(small[:, rq:rq + rk], gkv_ref[...])
    blk = small[:, rq + rk:rq + rk + LANE]
    kr_ref[...] = blk * cos_ref[...] + pltpu.roll(blk, LANE // 2, 1) * sin_ref[...]
    _qprep_kernel(small[:, 0:rq], gq_ref, wn_ref, wp_ref, wps_ref, cos_ref, sin_ref, q_ref,
                  chunk_tile=chunk_tile)


def _front(x, g_mix, w_t, row0, g_kv, cos_t, sin_t, g_q, wn, wp, wps, *, rank, rq, rk,
           chunk_tile=0):
    m, k = x.shape
    n_in = rank + rq + rk + MLA_ROPE
    n_out = rq + rk + 2 * MLA_ROPE + LANE
    tm = _pick(m, (256, 128))
    kern = functools.partial(_front_kernel, rank=rank, rq=rq, rk=rk, chunk_tile=chunk_tile)
    row = lambda i: (i, 0)
    const = lambda i: (0, 0)
    once = dict(pipeline_mode=pl.Buffered(1))
    return pl.pallas_call(
        kern,
        grid=(m // tm,),
        in_specs=[pl.BlockSpec((tm, k), row),
                  pl.BlockSpec((1, k), const),
                  pl.BlockSpec((pl.Element(n_in), pl.Element(k)), lambda i: (row0, 0), **once),
                  pl.BlockSpec((1, rk), const),
                  pl.BlockSpec((tm, LANE), row),
                  pl.BlockSpec((tm, LANE), row),
                  pl.BlockSpec((1, rq), const),
                  pl.BlockSpec(wn.shape, const, **once),
                  pl.BlockSpec(wp.shape, const, **once),
                  pl.BlockSpec(wps.shape, const, **once)],
        out_specs=[pl.BlockSpec((tm, k), row),
                   pl.BlockSpec((tm, n_out), row),
                   pl.BlockSpec((tm, rk), row),
                   pl.BlockSpec((tm, LANE), row),
                   pl.BlockSpec((MLA_HEADS, tm, 2 * LANE), lambda i: (0, i, 0))],
        out_shape=[jax.ShapeDtypeStruct((m, k), BF16),
                   jax.ShapeDtypeStruct((m, n_out), F32),
                   jax.ShapeDtypeStruct((m, rk), F32),
                   jax.ShapeDtypeStruct((m, LANE), F32),
                   jax.ShapeDtypeStruct((MLA_HEADS, m, 2 * LANE), BF16)],
        scratch_shapes=[pltpu.VMEM((n_out, k), BF16)],
        compiler_params=_cparams(("arbitrary",)),
        name="front_proj",
    )(x, g_mix.reshape(1, -1), w_t, g_kv.reshape(1, -1), cos_t, sin_t, g_q.reshape(1, -1),
      wn, wp, wps)


def _split3(x):
    a = x.astype(BF16)
    r1 = x - a.astype(F32)
    b = r1.astype(BF16)
    c = (r1 - b.astype(F32)).astype(BF16)
    return a, b, c


def _gla_kernel(q_ref, k_ref, v_ref, r_ref, ga_ref, a_ref, wa_ref, ba_ref, go_ref, s0_ref,
                o_ref, sout_ref, s_scr, *, C, SB, T, H, dk, dv, S):
    c_idx = pl.program_id(1)
    n_chunks = pl.num_programs(1)
    R = S * C

    @pl.when(c_idx == 0)
    def _():
        s_scr[...] = s0_ref[...]

    z = jnp.dot(a_ref[...].astype(BF16), wa_ref[...], preferred_element_type=F32) + ba_ref[...]
    log_sig = jnp.minimum(z, 0.0) - jnp.log(1.0 + jnp.exp(-jnp.abs(z)))
    la = jnp.maximum(log_sig * (1.0 / GLA_GATE_NORM), GLA_LOG_ALPHA_MIN)
    if T % C:
        rows = c_idx * C + lax.broadcasted_iota(jnp.int32, (C, 1), 0)
        la = jnp.where(rows < T, la, 0.0)

    ri = lax.broadcasted_iota(jnp.int32, (R, R), 0)
    ci = lax.broadcasted_iota(jnp.int32, (R, R), 1)
    same_seq = (ri >= ci) if S == 1 else ((ri >= ci) & (ri - ci <= lax.rem(ri, C)))
    tri = jnp.where(same_seq, 1.0, 0.0).astype(BF16)
    ones = jnp.ones((C, LANE), BF16)
    cs_all = jnp.zeros_like(la)
    dsum_all = [jnp.zeros((la.shape[1], LANE), F32) for _ in range(S)]
    for piece in _split3(la):
        cs_all = cs_all + jnp.dot(tri, piece, preferred_element_type=F32)
        for si in range(S):
            dsum_all[si] = dsum_all[si] + lax.dot_general(
                piece[si * C:(si + 1) * C], ones, (((0,), (0,)), ((), ())),
                preferred_element_type=F32)

    sr = lax.broadcasted_iota(jnp.int32, (SB, SB), 0)
    sc = lax.broadcasted_iota(jnp.int32, (SB, SB), 1)
    causal = sr >= sc
    nt = (((1,), (1,)), ((), ()))
    scale = dk ** -0.5

    for si, h in [(si, h) for si in range(S) for h in range(H)]:
        rs = slice(si * C, (si + 1) * C)
        ksl = slice(h * dk, (h + 1) * dk)
        vsl = slice(h * dv, (h + 1) * dv)
        cs = cs_all[rs, ksl]
        c_last = cs[C - 1:C, :]
        q = q_ref[rs, ksl].astype(F32) * scale
        k = k_ref[rs, ksl].astype(F32)
        v = v_ref[rs, vsl]
        s_old = s_scr[si, h]

        o_inter = jnp.dot((q * jnp.exp(cs)).astype(BF16), s_old.astype(BF16),
                          preferred_element_type=F32)
        k_end = (k * jnp.exp(c_last - cs)).astype(BF16)
        upd = lax.dot_general(k_end, v, (((0,), (0,)), ((), ())), preferred_element_type=F32)
        dcol = jnp.exp(dsum_all[si][ksl, :])
        s_scr[si, h] = jnp.concatenate([dcol] * (dv // LANE), axis=1) * s_old + upd

        outs = []
        for i in range(C // SB):
            lo = i * SB
            cs_i = cs[lo:lo + SB]
            q_i = q[lo:lo + SB]
            k_i = k[lo:lo + SB]
            start = cs[lo - 1:lo] if i > 0 else jnp.zeros_like(c_last)
            mid = 0.5 * (start + cs[lo + SB - 1:lo + SB])
            qd = (q_i * jnp.exp(cs_i - mid)).astype(BF16)
            kd = (k_i * jnp.exp(mid - cs_i)).astype(BF16)
            att = lax.dot_general(qd, kd, nt, preferred_element_type=F32)
            att = jnp.where(causal, att, 0.0)
            o_i = jnp.dot(att.astype(BF16), v[lo:lo + SB], preferred_element_type=F32)
            if i > 0:
                qo = (q_i * jnp.exp(cs_i - start)).astype(BF16)
                ko = (k[:lo] * jnp.exp(start - cs[:lo])).astype(BF16)
                att_o = lax.dot_general(qo, ko, nt, preferred_element_type=F32)
                o_i = o_i + jnp.dot(att_o.astype(BF16), v[:lo], preferred_element_type=F32)
            outs.append(o_i)
        o = o_inter + (jnp.concatenate(outs, axis=0) if len(outs) > 1 else outs[0])

        on = _rmsnorm(o, go_ref[...])
        r = r_ref[rs, vsl].astype(F32)
        g = ga_ref[rs, vsl].astype(F32)
        o_ref[rs, vsl] = (_sigmoid(g) * (on * (r * _sigmoid(r)))).astype(o_ref.dtype)

    @pl.when(c_idx == n_chunks - 1)
    def _():
        sout_ref[...] = s_scr[...]


def _gla(qkvr, gates, small, wa_pad, b_a, g_out, s0, *, B, T, Tp, dk, dv, col, row0=0):
    C = min(GLA_CHUNK, Tp)
    SB = min(32, C)
    assert Tp % C == 0 and C % SB == 0
    nc = Tp // C
    H = GLA_HEADS
    qk, vw = H * dk, H * dv
    S = _pick(B, (GLA_SEQS, 1)) if nc == 1 else 1
    R = S * C
    assert row0 % R == 0
    rb = lambda b, c: row0 // R + b * nc + c
    kern = functools.partial(_gla_kernel, C=C, SB=SB, T=T, H=H, dk=dk, dv=dv, S=S)
    return pl.pallas_call(
        kern,
        grid=(B // S, nc),
        in_specs=[
            pl.BlockSpec((R, qk), lambda b, c: (rb(b, c), col["q"] // qk)),
            pl.BlockSpec((R, qk), lambda b, c: (rb(b, c), col["k"] // qk)),
            pl.BlockSpec((R, vw), lambda b, c: (rb(b, c), col["v"] // vw)),
            pl.BlockSpec((R, vw), lambda b, c: (rb(b, c), col["r"] // vw)),
            pl.BlockSpec((R, vw), lambda b, c: (rb(b, c), col["ga"] // vw)),
            pl.BlockSpec((R, LANE), lambda b, c: (rb(b, c), col["a"] // LANE)),
            pl.BlockSpec((LANE, qk), lambda b, c: (0, 0)),
            pl.BlockSpec((1, qk), lambda b, c: (0, 0)),
            pl.BlockSpec((1, dv), lambda b, c: (0, 0)),
            pl.BlockSpec((S, H, dk, dv), lambda b, c: (b, 0, 0, 0)),
        ],
        out_specs=[
            pl.BlockSpec((R, vw), lambda b, c: (b * nc + c, 0)),
            pl.BlockSpec((S, H, dk, dv), lambda b, c: (b, 0, 0, 0)),
        ],
        out_shape=[jax.ShapeDtypeStruct((B * Tp, vw), BF16),
                   jax.ShapeDtypeStruct((B, H, dk, dv), F32)],
        scratch_shapes=[pltpu.VMEM((S, H, dk, dv), F32)],
        compiler_params=_cparams(("parallel", "arbitrary")),
        name="gla",
    )(qkvr, qkvr, qkvr, qkvr, gates, small, wa_pad, b_a.reshape(1, -1), g_out.reshape(1, -1), s0)


def _qprep_kernel(cq_ref, gq_ref, wn_ref, wp_ref, wps_ref, cos_ref, sin_ref, q_ref, *,
                  chunk_tile):
    hq = _rmsnorm(cq_ref[...], gq_ref[...]).astype(BF16)
    qn = jnp.dot(hq, wn_ref[...], preferred_element_type=F32)
    qp = jnp.dot(hq, wp_ref[...], preferred_element_type=F32)
    qs = jnp.dot(hq, wps_ref[...], preferred_element_type=F32)
    cos = cos_ref[...] * QK_SCALE_LOG2E
    sin = sin_ref[...] * QK_SCALE_LOG2E
    tag = 0.0
    if chunk_tile:
        tm = cos.shape[0]
        row = pl.program_id(0) * tm + lax.broadcasted_iota(jnp.int32, (tm, LANE), 0)
        lane = lax.broadcasted_iota(jnp.int32, (tm, LANE), 1)
        chunk = (row & (chunk_tile - 1)) >> CHUNK_SHIFT
        tag = jnp.where(lane - MASK_LANE0 == chunk, 1.0, 0.0)
    for h in range(MLA_HEADS):
        sl = slice(h * LANE, (h + 1) * LANE)
        q_ref[h, :, 0:LANE] = (qn[:, sl] * QK_SCALE_LOG2E).astype(BF16)
        q_ref[h, :, LANE:2 * LANE] = (qp[:, sl] * cos + qs[:, sl] * sin + tag).astype(BF16)


def _qprep(small, g_q, wn, wp, wps, cos_t, sin_t, *, col, chunk_tile=0):
    m = small.shape[0]
    rq = wn.shape[0]
    tm = _pick(m, (512, 256, 128))
    full = lambda i: (0, 0)
    return pl.pallas_call(
        functools.partial(_qprep_kernel, chunk_tile=chunk_tile),
        grid=(m // tm,),
        in_specs=[pl.BlockSpec((tm, rq), lambda i: (i, col["cq"] // rq)),
                  pl.BlockSpec((1, rq), full),
                  pl.BlockSpec(wn.shape, full),
                  pl.BlockSpec(wp.shape, full),
                  pl.BlockSpec(wps.shape, full),
                  pl.BlockSpec((tm, LANE), lambda i: (i, 0)),
                  pl.BlockSpec((tm, LANE), lambda i: (i, 0))],
        out_specs=pl.BlockSpec((MLA_HEADS, tm, 2 * LANE), lambda i: (0, i, 0)),
        out_shape=jax.ShapeDtypeStruct((MLA_HEADS, m, 2 * LANE), BF16),
        compiler_params=_cparams(("parallel",)),
        name="mla_q",
    )(small, g_q.reshape(1, -1), wn, wp, wps, cos_t, sin_t)


def _kvup_kernel(lat_ref, kr_ref, wuk_ref, wuv_ref, k_ref, v_ref, *, v_transposed):
    lat = lat_ref[...].astype(BF16)
    kn = jnp.dot(lat, wuk_ref[...], preferred_element_type=F32)
    kr = kr_ref[...]
    lane = lax.broadcasted_iota(jnp.int32, kr.shape, 1)
    kp = jnp.where(lane == MLA_ROPE, 1.0, kr).astype(BF16)
    if v_transposed:
        vv = lax.dot_general(wuv_ref[...], lat, (((1,), (1,)), ((), ())),
                             preferred_element_type=F32)
    else:
        vv = jnp.dot(lat, wuv_ref[...], preferred_element_type=F32)
    for h in range(MLA_HEADS):
        sl = slice(h * LANE, (h + 1) * LANE)
        k_ref[h, :, 0:LANE] = kn[:, sl].astype(BF16)
        k_ref[h, :, LANE:2 * LANE] = kp
        if v_transposed:
            v_ref[h, 0:LANE, :] = vv[sl, :].astype(BF16)
            v_ref[h, LANE:LANE + VT_ONES, :] = jnp.ones((VT_ONES, vv.shape[1]), BF16)
        else:
            v_ref[h] = vv[:, sl].astype(BF16)


def _kvup(lat, kr, wuk, wuv, *, v_transposed=False):
    m, rk = lat.shape
    tm = _pick(m, (512, 256, 128))
    full = lambda i: (0, 0)
    if v_transposed:
        v_spec = pl.BlockSpec((MLA_HEADS, LANE + VT_ONES, tm), lambda i: (0, 0, i))
        v_shape = (MLA_HEADS, LANE + VT_ONES, m)
    else:
        v_spec = pl.BlockSpec((MLA_HEADS, tm, LANE), lambda i: (0, i, 0))
        v_shape = (MLA_HEADS, m, LANE)
    return pl.pallas_call(
        functools.partial(_kvup_kernel, v_transposed=v_transposed),
        grid=(m // tm,),
        in_specs=[pl.BlockSpec((tm, rk), lambda i: (i, 0)),
                  pl.BlockSpec((tm, LANE), lambda i: (i, 0)),
                  pl.BlockSpec(wuk.shape, full),
                  pl.BlockSpec(wuv.shape, full)],
        out_specs=[pl.BlockSpec((MLA_HEADS, tm, 2 * LANE), lambda i: (0, i, 0)), v_spec],
        out_shape=[jax.ShapeDtypeStruct((MLA_HEADS, m, 2 * LANE), BF16),
                   jax.ShapeDtypeStruct(v_shape, BF16)],
        compiler_params=_cparams(("parallel",)),
        name="mla_kv",
    )(lat, kr, wuk, wuv)


def _last_kblock(qi, *, tq, tk, nk, q_off, k_off):
    top_chunk = ((qi + 1) * tq - 1 + q_off) // CHUNK
    last_key = (top_chunk + 1) * CHUNK - 1 - k_off
    return jnp.minimum(last_key // tk, nk - 1)


def _attn_kernel(q_ref, k_ref, v_ref, o_ref, m_scr, l_scr, acc_scr, *, hps, tq, tk, nk,
                 q_off, k_off):
    qi = pl.program_id(2)
    ki = pl.program_id(3)

    @pl.when(ki == 0)
    def _():
        m_scr[...] = jnp.full(m_scr.shape, NEG_BIG, F32)
        l_scr[...] = jnp.zeros(l_scr.shape, F32)
        acc_scr[...] = jnp.zeros(acc_scr.shape, F32)

    @pl.when(ki <= _last_kblock(qi, tq=tq, tk=tk, nk=nk, q_off=q_off, k_off=k_off))
    def _():
        q_chunk = (qi * tq + q_off + lax.broadcasted_iota(jnp.int32, (tq, 1), 0)) >> CHUNK_SHIFT
        k_chunk = (ki * tk + k_off + lax.broadcasted_iota(jnp.int32, (1, tk), 1)) >> CHUNK_SHIFT
        visible = q_chunk >= k_chunk

        def head(h, carry):
            s = lax.dot_general(q_ref[h], k_ref[h], (((1,), (1,)), ((), ())),
                                preferred_element_type=F32)
            s = jnp.where(visible, s, NEG_BIG)
            m_prev = m_scr[h]
            m_new = jnp.maximum(m_prev, jnp.max(s, axis=-1, keepdims=True))
            p = jnp.exp2(s - m_new)
            alpha = jnp.exp2(m_prev - m_new)
            l_scr[h] = alpha * l_scr[h] + jnp.sum(p, axis=-1, keepdims=True)
            acc_scr[h] = alpha * acc_scr[h] + jnp.dot(p.astype(BF16), v_ref[h],
                                                      preferred_element_type=F32)
            m_scr[h] = m_new
            return carry

        lax.fori_loop(0, hps, head, 0)

    @pl.when(ki == nk - 1)
    def _():
        for h in range(hps):
            o_ref[:, h * LANE:(h + 1) * LANE] = (acc_scr[h] / l_scr[h]).astype(o_ref.dtype)


def _attention(q, k, v, *, B, Tq, Tk, tq, tk, hps, q_off, k_off):
    nq = Tq // tq
    nk = Tk // tk
    hg = MLA_HEADS // hps
    dqk = q.shape[2]
    dvh = v.shape[2]
    last = functools.partial(_last_kblock, tq=tq, tk=tk, nk=nk, q_off=q_off, k_off=k_off)
    kern = functools.partial(_attn_kernel, hps=hps, tq=tq, tk=tk, nk=nk, q_off=q_off,
                             k_off=k_off)
    kv_row = lambda b, g, i, j: b * nk + jnp.minimum(j, last(i))
    return pl.pallas_call(
        kern,
        grid=(B, hg, nq, nk),
        in_specs=[pl.BlockSpec((hps, tq, dqk), lambda b, g, i, j: (g, b * nq + i, 0)),
                  pl.BlockSpec((hps, tk, dqk), lambda b, g, i, j: (g, kv_row(b, g, i, j), 0)),
                  pl.BlockSpec((hps, tk, dvh), lambda b, g, i, j: (g, kv_row(b, g, i, j), 0))],
        out_specs=pl.BlockSpec((tq, hps * dvh), lambda b, g, i, j: (b * nq + i, g)),
        out_shape=jax.ShapeDtypeStruct((B * Tq, MLA_HEADS * dvh), BF16),
        scratch_shapes=[pltpu.VMEM((hps, tq, 1), F32),
                        pltpu.VMEM((hps, tq, 1), F32),
                        pltpu.VMEM((hps, tq, dvh), F32)],
        compiler_params=_cparams(("parallel", "parallel", "parallel", "arbitrary")),
        name="mla_attn",
    )(q, k, v)


def _attn_t_kernel(qi_ref, ki_ref, q_ref, k_ref, vt_ref, kp_ref, vtp_ref, o_ref,
                   q_scr, r_scr, acc_scr, worst_scr, *, hps, t):
    pair = pl.program_id(1)
    qi = qi_ref[pair]
    ki = ki_ref[pair]
    nt = (((1,), (1,)), ((), ()))
    pe = slice(LANE, 2 * LANE)
    lane = lax.broadcasted_iota(jnp.int32, (t, LANE), 1)

    def set_reference(h, r):
        neg_r = jnp.transpose(jnp.broadcast_to(-r, (LANE, t)))
        q_scr[h, :, pe] = jnp.where(lane == MLA_ROPE, neg_r.astype(BF16), q_ref[h, :, pe])
        r_scr[h] = r

    def shifted_scores(h, own_tile=False):
        k = k_ref[h]
        if own_tile:
            ahead = lane - MASK_LANE0
            k_chunk = lax.broadcasted_iota(jnp.int32, (t, LANE), 0) >> CHUNK_SHIFT
            hidden = (ahead >= 0) & (ahead < k_chunk)
            k = jnp.concatenate(
                [k[:, 0:LANE], jnp.where(hidden, jnp.asarray(NEG_BIG, BF16), k[:, pe])], axis=1)
        return lax.dot_general(k, q_scr[h], nt, preferred_element_type=F32)

    @pl.when(ki == 0)
    def _():
        for h in range(hps):
            q_scr[h, :, 0:LANE] = q_ref[h, :, 0:LANE]
            s = lax.dot_general(kp_ref[h], q_ref[h], nt, preferred_element_type=F32)
            r = jnp.max(s, axis=0, keepdims=True).astype(BF16).astype(F32)
            p = jnp.exp2((s - r).astype(BF16))
            acc_scr[h] = jnp.dot(vtp_ref[h], p, preferred_element_type=F32)
            set_reference(h, r)

    def general(h, own_tile):
        sp = shifted_scores(h, own_tile)
        r = r_scr[h]
        rise = jnp.maximum(jnp.max(sp, axis=0, keepdims=True), 0.0)
        r_new = (r + rise).astype(BF16).astype(F32)
        delta = r_new - r
        p = jnp.exp2((sp - delta).astype(BF16))
        acc_scr[h] = jnp.exp2(-delta) * acc_scr[h] + jnp.dot(vt_ref[h], p,
                                                               preferred_element_type=F32)
        if not own_tile:
            set_reference(h, r_new)

    def tile_step(own_tile):
        worst_all = None
        sp_next = shifted_scores(0, own_tile)
        for h in range(hps):
            sp = sp_next
            if h + 1 < hps:
                sp_next = shifted_scores(h + 1, own_tile)
            worst = jnp.max(jnp.max(sp, axis=0, keepdims=True), axis=1, keepdims=True)
            part = jnp.dot(vt_ref[h], jnp.exp2(sp.astype(BF16)), preferred_element_type=F32)
            acc_scr[h] += jnp.where(worst <= SAFE_EXP, part, 0.0)
            worst_scr[h] = jnp.broadcast_to(worst, (1, LANE))
            worst_all = worst if worst_all is None else jnp.maximum(worst_all, worst)

        @pl.when(jnp.max(worst_all) > SAFE_EXP)
        def _():
            def redo(h, carry):
                @pl.when(jnp.max(worst_scr[h]) > SAFE_EXP)
                def _():
                    general(h, own_tile)
                return carry
            lax.fori_loop(0, hps, redo, 0)

    @pl.when(ki < qi)
    def _():
        tile_step(False)

    @pl.when(ki == qi)
    def _():
        tile_step(True)
        for h in range(hps):
            acc = acc_scr[h]
            o_t = acc[0:LANE] / acc[LANE:LANE + 1]
            o_ref[:, h * LANE:(h + 1) * LANE] = o_t.T.astype(o_ref.dtype)


def _attention_t(q, k, vt, k_pre, vt_pre, *, T, t, hps):
    assert T % t == 0 and t % CHUNK == 0 and t & (t - 1) == 0
    assert t // CHUNK <= LANE - MASK_LANE0
    n = T // t
    hg = MLA_HEADS // hps
    dqk = q.shape[2]
    npre = k_pre.shape[1]
    vrows = vt.shape[1]
    pairs = [(i, j) for i in range(n) for j in range(i + 1)]
    qi_arr = jnp.asarray([p[0] for p in pairs], jnp.int32)
    ki_arr = jnp.asarray([p[1] for p in pairs], jnp.int32)
    kern = functools.partial(_attn_t_kernel, hps=hps, t=t)
    grid_spec = pltpu.PrefetchScalarGridSpec(
        num_scalar_prefetch=2,
        grid=(hg, len(pairs)),
        in_specs=[pl.BlockSpec((hps, t, dqk), lambda g, p, qi, ki: (g, qi[p], 0)),
                  pl.BlockSpec((hps, t, dqk), lambda g, p, qi, ki: (g, ki[p], 0)),
                  pl.BlockSpec((hps, vrows, t), lambda g, p, qi, ki: (g, 0, ki[p])),
                  pl.BlockSpec((hps, npre, dqk), lambda g, p, qi, ki: (g, 0, 0)),
                  pl.BlockSpec((hps, vrows, npre), lambda g, p, qi, ki: (g, 0, 0))],
        out_specs=pl.BlockSpec((t, hps * LANE), lambda g, p, qi, ki: (qi[p], g)),
        scratch_shapes=[pltpu.VMEM((hps, t, dqk), BF16),
                        pltpu.VMEM((hps, 1, t), F32),
                        pltpu.VMEM((hps, vrows, t), F32),
                        pltpu.VMEM((hps, 1, LANE), F32)])
    return pl.pallas_call(
        kern,
        grid_spec=grid_spec,
        out_shape=jax.ShapeDtypeStruct((T, MLA_HEADS * LANE), BF16),
        compiler_params=_cparams(("parallel", "arbitrary")),
        name="mla_attn_t",
    )(qi_arr, ki_arr, q, k, vt, k_pre, vt_pre)


def _absorb_q_kernel(q_ref, w_ref, o_ref):
    o_ref[0] = jnp.dot(q_ref[0, :, 0:MLA_NOPE], w_ref[0],
                       preferred_element_type=F32).astype(o_ref.dtype)


def _absorb_q(q, w_uk_t3):
    heads, rows, dqk = q.shape
    rk = w_uk_t3.shape[2]
    return pl.pallas_call(
        _absorb_q_kernel,
        grid=(heads,),
        in_specs=[pl.BlockSpec((1, rows, dqk), lambda h: (h, 0, 0)),
                  pl.BlockSpec((1, MLA_NOPE, rk), lambda h: (h, 0, 0))],
        out_specs=pl.BlockSpec((1, rows, rk), lambda h: (h, 0, 0)),
        out_shape=jax.ShapeDtypeStruct((heads, rows, rk), BF16),
        compiler_params=_cparams(("parallel",)),
        name="mla_absorb_q",
    )(q, w_uk_t3)


def _attn_latent_kernel(ql_ref, q_ref, plat_ref, pkr_ref, lat_ref, kr_ref, o_ref, *, T, P, S):
    heads, _, rk = ql_ref.shape
    rows = heads * T
    nt = (((1,), (1,)), ((), ()))
    tok = lax.rem(lax.broadcasted_iota(jnp.int32, (rows, 1), 0), T)
    q_chunk = (P + tok) >> CHUNK_SHIFT
    k_chunk = lax.broadcasted_iota(jnp.int32, (1, P + T), 1) >> CHUNK_SHIFT
    visible = q_chunk >= k_chunk
    for si in range(S):
        ts = slice(si * T, (si + 1) * T)
        ql = ql_ref[:, ts, :].reshape(rows, rk)
        qpe = q_ref[:, ts, LANE:2 * LANE].reshape(rows, LANE)[:, 0:MLA_ROPE]
        lat_all = jnp.concatenate([plat_ref[si].astype(BF16), lat_ref[ts, :].astype(BF16)],
                                  axis=0)
        s_pe = jnp.concatenate(
            [jnp.dot(qpe, pkr_ref[si].astype(BF16), preferred_element_type=F32),
             lax.dot_general(qpe, kr_ref[ts, 0:MLA_ROPE].astype(BF16), nt,
                             preferred_element_type=F32)], axis=1)
        s = lax.dot_general(ql, lat_all, nt, preferred_element_type=F32) + s_pe
        s = jnp.where(visible, s, NEG_BIG)
        p = jnp.exp2(s - jnp.max(s, axis=-1, keepdims=True))
        o = jnp.dot(p.astype(BF16), lat_all, preferred_element_type=F32)
        o = o / jnp.sum(p, axis=-1, keepdims=True)
        o_ref[:, ts, :] = o.reshape(heads, T, rk).astype(o_ref.dtype)


def _attn_latent(qlat, q, past_lat, past_kr_t, lat, kr, *, B, T):
    heads, _, rk = qlat.shape
    P = past_lat.shape[1]
    S = _pick(B, (2, 1))
    kern = functools.partial(_attn_latent_kernel, T=T, P=P, S=S)
    return pl.pallas_call(
        kern,
        grid=(B // S,),
        in_specs=[pl.BlockSpec((heads, S * T, rk), lambda b: (0, b, 0)),
                  pl.BlockSpec((heads, S * T, q.shape[2]), lambda b: (0, b, 0)),
                  pl.BlockSpec((S, P, rk), lambda b: (b, 0, 0)),
                  pl.BlockSpec((S, past_kr_t.shape[1], P), lambda b: (b, 0, 0)),
                  pl.BlockSpec((S * T, rk), lambda b: (b, 0)),
                  pl.BlockSpec((S * T, LANE), lambda b: (b, 0))],
        out_specs=pl.BlockSpec((heads, S * T, rk), lambda b: (0, b, 0)),
        out_shape=jax.ShapeDtypeStruct((heads, B * T, rk), BF16),
        compiler_params=_cparams(("parallel",)),
        name="mla_attn_latent",
    )(qlat, q, past_lat, past_kr_t, lat, kr)


def _absorb_out_kernel(o_ref, w_ref, out_ref):
    out_ref[...] = jnp.dot(o_ref[0], w_ref[0], preferred_element_type=F32).astype(out_ref.dtype)


def _absorb_out(olat, w_uv3):
    heads, rows, rk = olat.shape
    dvh = w_uv3.shape[2]
    return pl.pallas_call(
        _absorb_out_kernel,
        grid=(heads,),
        in_specs=[pl.BlockSpec((1, rows, rk), lambda h: (h, 0, 0)),
                  pl.BlockSpec((1, rk, dvh), lambda h: (h, 0, 0))],
        out_specs=pl.BlockSpec((rows, dvh), lambda h: (0, h)),
        out_shape=jax.ShapeDtypeStruct((rows, heads * dvh), BF16),
        compiler_params=_cparams(("parallel",)),
        name="mla_absorb_out",
    )(olat, w_uv3)


def _merge_kernel(a_ref, gb_ref, om_ref, x_ref, wo_ref, gf_ref, x1_ref, h2_ref):
    merged = a_ref[...].astype(F32) + _sigmoid(gb_ref[...].astype(F32)) * om_ref[...].astype(F32)
    x1 = x_ref[...] + jnp.dot(merged.astype(BF16), wo_ref[...], preferred_element_type=F32)
    x1_ref[...] = x1
    h2_ref[...] = _rmsnorm(x1, gf_ref[...]).astype(BF16)


def _merge(branch_a, gates, o_m, x, wo, g_ffn, *, col):
    m, d = x.shape
    tm = _pick(m, (512, 384, 256, 128))
    row = lambda i: (i, 0)
    return pl.pallas_call(
        _merge_kernel,
        grid=(m // tm,),
        in_specs=[pl.BlockSpec((tm, d), row),
                  pl.BlockSpec((tm, d), lambda i: (i, col["gb"] // d)),
                  pl.BlockSpec((tm, d), row),
                  pl.BlockSpec((tm, d), row),
                  pl.BlockSpec(wo.shape, lambda i: (0, 0), pipeline_mode=pl.Buffered(1)),
                  pl.BlockSpec((1, d), lambda i: (0, 0))],
        out_specs=[pl.BlockSpec((tm, d), row), pl.BlockSpec((tm, d), row)],
        out_shape=[jax.ShapeDtypeStruct((m, d), F32), jax.ShapeDtypeStruct((m, d), BF16)],
        compiler_params=_cparams(("parallel",)),
        name="merge_out_proj",
    )(branch_a, gates, o_m, x, wo, g_ffn.reshape(1, -1))


HALO = 8


def _ffn_up_kernel(*refs, bb, r, tf, loc, carried, cast_down):
    (h_ref, wa_ref, wb_ref, cwa_ref, cwb_ref, cba_ref, cbb_ref, ha_ref, hb_ref), refs = \
        refs[:9], refs[9:]
    if cast_down:
        wd_ref, act_ref, ca_ref, cb_ref, wdb_ref, ext_scr, carry_scr, w_scr = refs
    else:
        act_ref, ca_ref, cb_ref, ext_scr, carry_scr, w_scr = refs
    s = pl.program_id(1)
    rt = pl.program_id(2)
    d = h_ref.shape[2]

    @pl.when((s == 0) & (rt == 0))
    def _():
        w_scr[0] = wa_ref[...].astype(BF16)
        w_scr[1] = wb_ref[...].astype(BF16)
        if cast_down:
            wdb_ref[...] = wd_ref[...].astype(BF16)

    if carried:
        @pl.when(rt == 0)
        def _():
            carry_scr[0] = ha_ref[...]
            carry_scr[1] = hb_ref[...]

    h = h_ref[...].reshape(bb * r, d)
    conv = []
    for half, (cw_ref, cbias_ref, hist_ref, cout_ref) in enumerate(
            ((cwa_ref, cba_ref, ha_ref, ca_ref), (cwb_ref, cbb_ref, hb_ref, cb_ref))):
        u = jnp.dot(h, w_scr[half], preferred_element_type=F32).reshape(bb, r, tf)
        ext_scr[half, :, HALO:HALO + r, :] = u
        ext_scr[half, :, HALO - 2:HALO, :] = carry_scr[half] if carried else hist_ref[...]
        u1 = ext_scr[half, :, HALO - 1:HALO - 1 + r, :]
        u2 = ext_scr[half, :, HALO - 2:HALO - 2 + r, :]
        cw = cw_ref[...]
        conv.append(cbias_ref[...] + cw[0:1] * u2 + cw[1:2] * u1 + cw[2:3] * u)
        if carried:
            carry_scr[half] = ext_scr[half, :, HALO + r - 2:HALO + r, :]
        cout_ref[0] = ext_scr[half, :, HALO + loc:HALO + loc + 2, :]

    act_ref[...] = ((conv[0] * _sigmoid(conv[0])) * conv[1]).astype(act_ref.dtype)


def _ffn_down_kernel(act_ref, wd_ref, x1_ref, gf_ref, y_ref):
    down = jnp.dot(act_ref[...], wd_ref[...], preferred_element_type=F32)
    y_ref[...] = _rmsnorm(x1_ref[...] + down, gf_ref[...])


def _ffn(h2, x1, w_up, w_down, conv_w, conv_b, hist, g_final, *, B, T, Tp):
    d = h2.shape[1]
    dff = w_down.shape[0]
    cast_down = w_down.dtype != BF16
    tf = _pick(dff, (512, 256, 128))
    nf = dff // tf
    if Tp <= 128:
        bb, r = B, Tp
    else:
        bb, r = 1, _pick(Tp, (ROW_TILE, 128))
    nrt = Tp // r
    carried = nrt > 1
    loc = (T - 2) - (nrt - 1) * r
    assert 0 <= loc <= r - 2, "final two valid rows must sit in the last row tile"
    kern = functools.partial(_ffn_up_kernel, bb=bb, r=r, tf=tf, loc=loc, carried=carried,
                             cast_down=cast_down)
    carry_shape = (2, bb, 2, tf) if carried else (1, 1, 2, LANE)
    in_specs = [pl.BlockSpec((bb, r, d), lambda f, s, t: (s, t, 0)),
                pl.BlockSpec((d, tf), lambda f, s, t: (0, f)),
                pl.BlockSpec((d, tf), lambda f, s, t: (0, nf + f)),
                pl.BlockSpec((CONV_W, tf), lambda f, s, t: (0, f)),
                pl.BlockSpec((CONV_W, tf), lambda f, s, t: (0, nf + f)),
                pl.BlockSpec((1, tf), lambda f, s, t: (0, f)),
                pl.BlockSpec((1, tf), lambda f, s, t: (0, nf + f)),
                pl.BlockSpec((bb, 2, tf), lambda f, s, t: (s, 0, f)),
                pl.BlockSpec((bb, 2, tf), lambda f, s, t: (s, 0, nf + f))]
    out_specs = [pl.BlockSpec((bb, r, tf), lambda f, s, t: (s, t, f)),
                 pl.BlockSpec((1, bb, 2, tf), lambda f, s, t: (t, s, 0, f)),
                 pl.BlockSpec((1, bb, 2, tf), lambda f, s, t: (t, s, 0, f))]
    out_shape = [jax.ShapeDtypeStruct((B, Tp, dff), BF16),
                 jax.ShapeDtypeStruct((nrt, B, 2, dff), F32),
                 jax.ShapeDtypeStruct((nrt, B, 2, dff), F32)]
    args = [h2.reshape(B, Tp, d), w_up, w_up, conv_w, conv_w, conv_b.reshape(1, -1),
            conv_b.reshape(1, -1), hist, hist]
    if cast_down:
        in_specs.append(pl.BlockSpec((tf, d), lambda f, s, t: (f, 0)))
        out_specs.append(pl.BlockSpec((tf, d), lambda f, s, t: (f, 0)))
        out_shape.append(jax.ShapeDtypeStruct((dff, d), BF16))
        args.append(w_down)
    outs = pl.pallas_call(
        kern,
        grid=(nf, B // bb, nrt),
        in_specs=in_specs,
        out_specs=out_specs,
        out_shape=out_shape,
        scratch_shapes=[pltpu.VMEM((2, bb, HALO + r, tf), F32),
                        pltpu.VMEM(carry_shape, F32),
                        pltpu.VMEM((2, d, tf), BF16)],
        compiler_params=_cparams(("arbitrary", "arbitrary", "arbitrary")),
        name="conv_ffn_up",
    )(*args)
    act, ca, cb = outs[:3]
    if cast_down:
        w_down = outs[3]

    m = B * Tp
    tm = _pick(m, (256, 128))
    y = pl.pallas_call(
        _ffn_down_kernel,
        grid=(m // tm,),
        in_specs=[pl.BlockSpec((tm, dff), lambda i: (i, 0)),
                  pl.BlockSpec((dff, d), lambda i: (0, 0), pipeline_mode=pl.Buffered(1)),
                  pl.BlockSpec((tm, d), lambda i: (i, 0)),
                  pl.BlockSpec((1, d), lambda i: (0, 0))],
        out_specs=pl.BlockSpec((tm, d), lambda i: (i, 0)),
        out_shape=jax.ShapeDtypeStruct((m, d), F32),
        compiler_params=_cparams(("parallel",)),
        name="ffn_down",
    )(act.reshape(m, dff), w_down, x1, g_final.reshape(1, -1))
    return y.reshape(B, Tp, d), jnp.concatenate([ca[nrt - 1], cb[nrt - 1]], axis=-1), w_down


def _rope_tables(pos):
    half = MLA_ROPE // 2
    inv = ROPE_THETA ** (-jnp.arange(0, MLA_ROPE, 2, dtype=F32) / MLA_ROPE)
    ang = pos.astype(F32)[:, None] * inv[None, :]
    cos, sin = jnp.cos(ang), jnp.sin(ang)
    zero = jnp.zeros((pos.shape[0], LANE - 2 * half), F32)
    return (jnp.concatenate([cos, cos, zero], axis=1),
            jnp.concatenate([-sin, sin, zero], axis=1))


def _attn_tile(T):
    return _pick(T, (1024, 128))


def _project(x, pos, w, chunk_tile=0):
    col = w["col"]
    rows = w["in_rows"]
    cos_t, sin_t = _rope_tables(pos)
    h, small, lat, kr, q = _front(x, w["g_mix"], w["w_in_t"], rows["a"], w["g_kv"], cos_t, sin_t,
                                  w["g_q"], w["wq_nope"], w["wq_pe"], w["wq_pe_sw"],
                                  rank=rows["cq"] - rows["a"], rq=rows["ckv"] - rows["cq"],
                                  rk=rows["kpe"] - rows["ckv"], chunk_tile=chunk_tile)
    qkvr = _matmul_wt(h, w["w_in_t"], rows["q"], rows["a"] - rows["q"], BF16, tn=1024)
    gates = _matmul_wt(h, w["w_in_t"], rows["ga"], rows["end"] - rows["ga"], BF16, tn=1024)
    return dict(qkvr=qkvr, gates=gates, small=small, q=q, lat=lat, kr=kr)


def _finish(x, pr, branch_a, o_m, w, hist, *, B, T):
    x1, h2 = _merge(branch_a, pr["gates"], o_m, x, w["w_o"], w["g_ffn"], col=w["col"])
    y, conv, w["w_down"] = _ffn(h2, x1, w["w_up"], w["w_down"], w["conv_w"], w["conv_b"], hist,
                                w["final_norm"], B=B, T=T, Tp=T)
    return y, conv


def _gla_group(pr, w, s0, *, B, T, row0=0):
    return _gla(pr["qkvr"], pr["gates"], pr["small"], w["wa_pad"], w["b_a"], w["g_gla_out"],
                s0, B=B, T=T, Tp=T, dk=w["dk"], dv=w["dv"], col=w["col"], row0=row0)


def _long_stream(x, pr, w, *, T, s0, hist, prefix):
    branch_a, state = _gla_group(pr, w, s0, B=1, T=T)
    k, vt = _kvup(pr["lat"], pr["kr"], w["w_uk"], w["w_uv_t"], v_transposed=True)
    o_m = _attention_t(pr["q"], k, vt, prefix[0], prefix[1], T=T, t=_attn_tile(T),
                       hps=MLA_HEADS // 4)
    y, conv = _finish(x, pr, branch_a, o_m, w, hist, B=1, T=T)
    return y, pr["lat"], pr["kr"], state, conv


def _short_streams(x, pr, w, *, B, T, past_lat, past_kr, s0_s, hist_s):
    ns = B * T
    dk, dv = w["dk"], w["dv"]

    ba_s, st_s = _gla_group(pr, w, s0_s, B=B, T=T)
    ba_m, st_m = _gla_group(pr, w, jnp.zeros((1, GLA_HEADS, dk, dv), F32), B=1, T=T, row0=ns)

    qlat = _absorb_q(pr["q"], w["w_uk_t3"])
    olat = _attn_latent(qlat, pr["q"], past_lat, jnp.swapaxes(past_kr, 1, 2), pr["lat"],
                        pr["kr"], B=B, T=T)
    om_s = _absorb_out(olat, w["w_uv3"])
    q_m, lat_m, kr_m = pr["q"][:, ns:], pr["lat"][ns:], pr["kr"][ns:]
    k_m, v_m = _kvup(lat_m, kr_m, w["w_uk"], w["w_uv"])
    prefix = _kvup(lat_m, kr_m, w["w_uk"], w["w_uv_t"], v_transposed=True)
    om_m = _attention(q_m, k_m, v_m, B=1, Tq=T, Tk=T, tq=T, tk=T, hps=MLA_HEADS,
                      q_off=0, k_off=0)

    hist = jnp.concatenate([hist_s, jnp.zeros((1,) + hist_s.shape[1:], F32)], axis=0)
    y, conv = _finish(x, pr, jnp.concatenate([ba_s, ba_m], axis=0),
                      jnp.concatenate([om_s, om_m], axis=0), w, hist, B=B + 1, T=T)
    sample = (y[:B], pr["lat"][:ns], pr["kr"][:ns], st_s, conv[:B])
    meta = (lat_m, kr_m, st_m, conv[B:], prefix)
    return sample, meta


def _prep_weights(g_mix, w_in, w_a2, b_a, g_gla_out, g_q, w_uq, g_kv, w_uk, w_uv, w_o,
                  g_ffn, w_up, conv_w, conv_b, w_down, final_norm):
    d = w_in.shape[0]
    rank, gqk = w_a2.shape
    gvw = GLA_HEADS * g_gla_out.shape[0]
    rq, rk = g_q.shape[0], g_kv.shape[0]
    half = MLA_ROPE // 2
    o, offs = 0, {}
    for name, width in (("q", gqk), ("k", gqk), ("v", gvw), ("r", gvw), ("a", rank),
                        ("cq", rq), ("ckv", rk), ("kpe", MLA_ROPE), ("ga", d), ("gb", d)):
        offs[name] = (o, o + width)
        o += width
    assert o == w_in.shape[1]
    in_rows = {name: lo for name, (lo, _) in offs.items()}
    in_rows["end"] = o
    assert all(v % 16 == 0 for v in in_rows.values())
    col = {"q": 0, "k": gqk, "v": 2 * gqk, "r": 2 * gqk + gvw, "ga": 0, "gb": d,
           "cq": 0, "ckv": rq, "kpe": rq + rk, "a": rq + rk + 2 * MLA_ROPE}

    w3 = w_uq.reshape(rq, MLA_HEADS, MLA_NOPE + MLA_ROPE)
    pe = w3[:, :, MLA_NOPE:]
    pe_sw = jnp.concatenate([pe[:, :, half:], pe[:, :, :half]], axis=2)
    zpad = jnp.zeros((rq, MLA_HEADS, LANE - MLA_ROPE), w_uq.dtype)
    flat = lambda t: t.reshape(rq, -1).astype(BF16)
    wa_pad = jnp.concatenate([w_a2, jnp.zeros((LANE - rank, gqk), w_a2.dtype)], axis=0)
    return dict(
        col=col, dk=gqk // GLA_HEADS, dv=g_gla_out.shape[0],
        g_mix=g_mix, w_in_t=jnp.swapaxes(w_in, 0, 1), in_rows=in_rows,
        wa_pad=wa_pad.astype(BF16), b_a=b_a, g_gla_out=g_gla_out, g_q=g_q,
        wq_nope=flat(w3[:, :, :MLA_NOPE]),
        wq_pe=flat(jnp.concatenate([pe, zpad], axis=2)),
        wq_pe_sw=flat(jnp.concatenate([pe_sw, zpad], axis=2)),
        g_kv=g_kv, w_uk=w_uk.astype(BF16), w_uv=w_uv.astype(BF16),
        w_uv_t=w_uv.T.astype(BF16),
        w_uk_t3=w_uk.reshape(rk, MLA_HEADS, MLA_NOPE).transpose(1, 2, 0).astype(BF16),
        w_uv3=w_uv.reshape(rk, MLA_HEADS, MLA_V).transpose(1, 0, 2).astype(BF16),
        w_o=w_o.astype(BF16),
        g_ffn=g_ffn, w_up=w_up, conv_w=conv_w, conv_b=conv_b,
        w_down=w_down, final_norm=final_norm)


def kernel(x_prompt, x_sample, cache_mla_latent, cache_mla_krope, state_gla, cache_ffn_conv,
           meta_tokens, g_mix, w_in, w_a2, b_a, g_gla_out, g_q, w_uq, g_kv, w_uk, w_uv, w_o,
           g_ffn, w_up, conv_w, conv_b, w_down, final_norm):
    assert w_in.shape[0] == 1, "single trunk layer"
    bp, seq, d = x_prompt.shape
    assert bp == 1
    bs, ts, _ = x_sample.shape
    P = cache_mla_latent.shape[2]
    w = _prep_weights(g_mix[0], w_in[0], w_a2[0], b_a[0], g_gla_out[0], g_q[0], w_uq[0],
                      g_kv[0], w_uk[0], w_uv[0], w_o[0], g_ffn[0], w_up[0], conv_w[0],
                      conv_b[0], w_down[0], final_norm)

    n_meta = meta_tokens.shape[0]
    assert n_meta == N_META == ts and seq % CHUNK == 0
    x_short = jnp.concatenate([x_sample.reshape(bs * ts, d), meta_tokens.astype(F32)], axis=0)
    pos_short = jnp.concatenate([jnp.tile(P + jnp.arange(ts, dtype=jnp.int32), bs),
                                 jnp.arange(n_meta, dtype=jnp.int32)])
    pos_long = n_meta + jnp.arange(seq, dtype=jnp.int32)
    pr_short = _project(x_short, pos_short, w)
    pr_long = _project(x_prompt[0], pos_long, w, chunk_tile=_attn_tile(seq))
    (ys, lat_s, kr_s, st_s, cv_s), (lat_m, kr_m, st_m, cv_m, prefix) = _short_streams(
        x_short, pr_short, w, B=bs, T=ts, past_lat=cache_mla_latent[0],
        past_kr=cache_mla_krope[0], s0_s=state_gla[0], hist_s=cache_ffn_conv[0])
    yp, lat_p, kr_p, st_p, cv_p = _long_stream(
        x_prompt[0], pr_long, w, T=seq, s0=st_m, hist=cv_m, prefix=prefix)

    rk = lat_p.shape[1]
    T = n_meta + seq
    return (yp,
            ys,
            jnp.concatenate([lat_m, lat_p], axis=0).reshape(1, 1, T, rk),
            jnp.concatenate([kr_m, kr_p], axis=0)[:, :MLA_ROPE].reshape(1, 1, T, MLA_ROPE),
            st_p[None],
            cv_p[None],
            lat_s.reshape(1, bs, ts, rk),
            kr_s[:, :MLA_ROPE].reshape(1, bs, ts, MLA_ROPE),
            st_s[None],
            cv_s[None])
```
